```python
import jax, jax.numpy as jnp
from jax import lax
import numpy as np

D_MODEL = 2048
BATCH = 8
SEQ = 4096
DEPTH = 1

PLE_DIM = 256
RW_HEADS = 16
RW_HEAD_DIM = 64
RW_WIDTH = RW_HEADS * RW_HEAD_DIM
DECAY_LORA = 64
AAA_LORA = 64
FOX_HEADS = 16
FOX_HEAD_DIM = 64
FOX_WIDTH = FOX_HEADS * FOX_HEAD_DIM
Q_BLOCK = 128
NORM_EPS = 1e-6
GN_EPS = 64e-5

RW_COLS = 4 * RW_WIDTH + DECAY_LORA + AAA_LORA
FOX_COLS = 4 * FOX_WIDTH + FOX_HEADS
GATE_COLS = 2 * D_MODEL
N_IN = RW_COLS + FOX_COLS + GATE_COLS

kernel_name = 'hybrid_rwkv7_fox_gated_merge'


def _rmsnorm(x, g):
    xf = x.astype(jnp.float32)
    y = xf * lax.rsqrt(jnp.mean(xf * xf, axis=-1, keepdims=True) + NORM_EPS)
    return (y * g.astype(jnp.float32)).astype(x.dtype)


def _token_shift(z, mu):
    z_prev = jnp.pad(z, ((0, 0), (1, 0), (0, 0)))[:, :-1]
    return z + (z_prev - z) * mu


def _rwkv7_scan(r, w, k, v, kk, a):
    def step(S, inp):
        r_t, w_t, k_t, v_t, kk_t, a_t = inp
        s_kk = jnp.einsum('bhvk,bhk->bhv', S, kk_t)
        S = (S * w_t[:, :, None, :]
             - s_kk[..., None] * (kk_t * a_t)[:, :, None, :]
             + v_t[..., None] * k_t[:, :, None, :])
        y_t = jnp.einsum('bhvk,bhk->bhv', S, r_t)
        return S, y_t
    B, T, H, N = r.shape
    S0 = jnp.zeros((B, H, N, N), jnp.float32)
    xs = tuple(jnp.swapaxes(t, 0, 1) for t in (r, w, k, v, kk, a))
    _, ys = lax.scan(step, S0, xs)
    return jnp.swapaxes(ys, 0, 1)


def _rwkv7_branch(z, w0, w_lora_up, a0, a_lora_up, k_k, k_a, r_k, ln_g, ln_b):
    B, T, _ = z.shape
    f32 = jnp.float32
    C = RW_WIDTH
    r = z[..., 0:C]
    k = z[..., C:2 * C]
    v = z[..., 2 * C:3 * C]
    g = z[..., 3 * C:4 * C]
    wl = z[..., 4 * C:4 * C + DECAY_LORA]
    al = z[..., 4 * C + DECAY_LORA:]
    w_raw = (w0 + jnp.tanh(wl) @ w_lora_up).astype(f32)
    decay = jnp.exp(-jnp.exp(-jax.nn.softplus(-w_raw) - 0.5))
    a = jax.nn.sigmoid((a0 + al @ a_lora_up).astype(f32))
    hs = lambda t: t.astype(f32).reshape(B, T, RW_HEADS, RW_HEAD_DIM)
    r_h, k_h, v_h, a_h, w_h = hs(r), hs(k), hs(v), hs(a), hs(decay)
    kk = k_h * k_k.astype(f32).reshape(RW_HEADS, RW_HEAD_DIM)
    kk = kk / jnp.maximum(jnp.sqrt(jnp.sum(kk * kk, axis=-1, keepdims=True)), 1e-12)
    k_h = k_h * (1.0 + (a_h - 1.0) * k_a.astype(f32).reshape(RW_HEADS, RW_HEAD_DIM))
    y = _rwkv7_scan(r_h, w_h, k_h, v_h, kk, a_h)
    mu = jnp.mean(y, axis=-1, keepdims=True)
    var = jnp.mean(jnp.square(y - mu), axis=-1, keepdims=True)
    y = ((y - mu) * lax.rsqrt(var + GN_EPS)).reshape(B, T, C)
    y = y * ln_g.astype(f32) + ln_b.astype(f32)
    bonus = jnp.sum(r_h * k_h * r_k.astype(f32), axis=-1, keepdims=True) * v_h
    y = y + bonus.reshape(B, T, C)
    return (y * jax.nn.silu(g.astype(f32))).astype(z.dtype)


def _fox_branch(z, b_f):
    B, T, _ = z.shape
    f32 = jnp.float32
    C = FOX_WIDTH
    to_heads = lambda t: t.astype(f32).reshape(B, T, FOX_HEADS, FOX_HEAD_DIM).transpose(0, 2, 1, 3)
    qh = to_heads(z[..., 0:C])
    kh = to_heads(z[..., C:2 * C])
    vh = to_heads(z[..., 2 * C:3 * C])
    g = z[..., 3 * C:4 * C]
    fl = z[..., 4 * C:]
    log_f = jax.nn.log_sigmoid((fl + b_f).astype(f32))
    c = jnp.cumsum(log_f, axis=1).transpose(0, 2, 1)
    scale = FOX_HEAD_DIM ** -0.5
    outs = []
    for blk in range(T // Q_BLOCK):
        s = blk * Q_BLOCK
        e = s + Q_BLOCK
        logits = jnp.einsum('bhqd,bhkd->bhqk', qh[:, :, s:e], kh[:, :, :e]) * scale
        logits = logits + (c[:, :, s:e, None] - c[:, :, None, :e])
        causal = jnp.arange(s, e)[:, None] >= jnp.arange(e)[None, :]
        logits = jnp.where(causal, logits, -jnp.inf)
        probs = jax.nn.softmax(logits, axis=-1)
        outs.append(jnp.einsum('bhqk,bhkd->bhqd', probs, vh[:, :, :e]))
    o = jnp.concatenate(outs, axis=2).transpose(0, 2, 1, 3).reshape(B, T, C)
    return (o * jax.nn.silu(g.astype(f32))).astype(z.dtype)


def _fwd_setup_inputs(seed: int = 0) -> dict:
    key = jax.random.key(seed)
    ks = jax.random.split(key, 24)
    f32 = jnp.float32
    nrm = lambda k, shape, s: jax.random.normal(k, shape, f32) * s
    C = RW_WIDTH
    return {
        'x': nrm(ks[0], (BATCH, SEQ, D_MODEL), 1.0),
        'p': nrm(ks[1], (DEPTH, BATCH, SEQ, PLE_DIM), 1.0),
        'norm_g': 1.0 + nrm(ks[2], (DEPTH, D_MODEL), 0.02),
        'w_in': nrm(ks[3], (DEPTH, D_MODEL, N_IN), D_MODEL ** -0.5),
        'rw_shift_mu': jax.random.uniform(ks[4], (DEPTH, RW_COLS), f32),
        'rw_w0': jax.random.uniform(ks[5], (DEPTH, C), f32, -6.0, -1.0),
        'rw_w_lora_up': nrm(ks[6], (DEPTH, DECAY_LORA, C), 0.1 * DECAY_LORA ** -0.5),
        'rw_a0': nrm(ks[7], (DEPTH, C), 0.1),
        'rw_a_lora_up': nrm(ks[8], (DEPTH, AAA_LORA, C), 0.1 * AAA_LORA ** -0.5),
        'rw_k_k': 0.85 + nrm(ks[9], (DEPTH, C), 0.02),
        'rw_k_a': 1.0 + nrm(ks[10], (DEPTH, C), 0.02),
        'rw_r_k': nrm(ks[11], (DEPTH, RW_HEADS, RW_HEAD_DIM), 0.1),
        'rw_ln_g': 1.0 + nrm(ks[12], (DEPTH, C), 0.02),
        'rw_ln_b': nrm(ks[13], (DEPTH, C), 0.02),
        'fox_b_f': jax.random.uniform(ks[14], (DEPTH, FOX_HEADS), f32, 1.0, 5.0),
        'w_up_rwkv': nrm(ks[15], (DEPTH, RW_WIDTH, D_MODEL), RW_WIDTH ** -0.5),
        'w_up_fox': nrm(ks[16], (DEPTH, FOX_WIDTH, D_MODEL), FOX_WIDTH ** -0.5),
        'w_out': nrm(ks[17], (DEPTH, D_MODEL, D_MODEL), D_MODEL ** -0.5),
        'ple_proj': nrm(ks[18], (DEPTH, PLE_DIM, D_MODEL), PLE_DIM ** -0.5),
        'ple_gate_w': nrm(ks[19], (DEPTH, D_MODEL, D_MODEL), D_MODEL ** -0.5),
        'ple_norm_g': 1.0 + nrm(ks[20], (DEPTH, D_MODEL), 0.02),
        'final_norm_g': 1.0 + nrm(ks[21], (D_MODEL,), 0.02),
    }


def _fwd_reference(x, p, norm_g, w_in, rw_shift_mu, rw_w0, rw_w_lora_up, rw_a0, rw_a_lora_up,
              rw_k_k, rw_k_a, rw_r_k, rw_ln_g, rw_ln_b, fox_b_f, w_up_rwkv, w_up_fox, w_out,
              ple_proj, ple_gate_w, ple_norm_g, final_norm_g):
    for i in range(DEPTH):
        h = _rmsnorm(x, norm_g[i])
        z = h @ w_in[i]
        z_rw = _token_shift(z[..., :RW_COLS], rw_shift_mu[i])
        z_fox = z[..., RW_COLS:RW_COLS + FOX_COLS]
        z_gate = z[..., RW_COLS + FOX_COLS:]
        y_rw = _rwkv7_branch(z_rw, rw_w0[i], rw_w_lora_up[i], rw_a0[i], rw_a_lora_up[i],
                             rw_k_k[i], rw_k_a[i], rw_r_k[i], rw_ln_g[i], rw_ln_b[i])
        y_fox = _fox_branch(z_fox, fox_b_f[i])
        u_rw = y_rw @ w_up_rwkv[i]
        u_fox = y_fox @ w_up_fox[i]
        merged = (jax.nn.sigmoid(z_gate[..., :D_MODEL]) * u_rw
                  + jax.nn.sigmoid(z_gate[..., D_MODEL:]) * u_fox)
        x = x + merged @ w_out[i]
        ple = p[i] @ ple_proj[i]
        x = x + ple * jax.nn.sigmoid(_rmsnorm(x, ple_norm_g[i]) @ ple_gate_w[i])
    return _rmsnorm(x, final_norm_g)


import jax as _jax
import jax.numpy as _jnp

TWIN_FORMAT = 'train_step'
FWD_PARAMS = ['x', 'p', 'norm_g', 'w_in', 'rw_shift_mu', 'rw_w0', 'rw_w_lora_up', 'rw_a0', 'rw_a_lora_up', 'rw_k_k', 'rw_k_a', 'rw_r_k', 'rw_ln_g', 'rw_ln_b', 'fox_b_f', 'w_up_rwkv', 'w_up_fox', 'w_out', 'ple_proj', 'ple_gate_w', 'ple_norm_g', 'final_norm_g']
TWIN_WEIGHTS = ['norm_g', 'w_in', 'rw_shift_mu', 'rw_w0', 'rw_w_lora_up', 'rw_a0', 'rw_a_lora_up', 'rw_k_k', 'rw_k_a', 'rw_r_k', 'rw_ln_g', 'rw_ln_b', 'fox_b_f', 'w_up_rwkv', 'w_up_fox', 'w_out', 'ple_proj', 'ple_gate_w', 'ple_norm_g', 'final_norm_g']
TWIN_DIFF_INPUT = 'x'
TWIN_INPUTS = ['x', 'p', 'norm_g', 'w_in', 'rw_shift_mu', 'rw_w0', 'rw_w_lora_up', 'rw_a0', 'rw_a_lora_up', 'rw_k_k', 'rw_k_a', 'rw_r_k', 'rw_ln_g', 'rw_ln_b', 'fox_b_f', 'w_up_rwkv', 'w_up_fox', 'w_out', 'ple_proj', 'ple_gate_w', 'ple_norm_g', 'final_norm_g', 'loss_target', 'm_norm_g', 'm_w_in', 'm_rw_shift_mu', 'm_rw_w0', 'm_rw_w_lora_up', 'm_rw_a0', 'm_rw_a_lora_up', 'm_rw_k_k', 'm_rw_k_a', 'm_rw_r_k', 'm_rw_ln_g', 'm_rw_ln_b', 'm_fox_b_f', 'm_w_up_rwkv', 'm_w_up_fox', 'm_w_out', 'm_ple_proj', 'm_ple_gate_w', 'm_ple_norm_g', 'm_final_norm_g', 'v_norm_g', 'v_w_in', 'v_rw_shift_mu', 'v_rw_w0', 'v_rw_w_lora_up', 'v_rw_a0', 'v_rw_a_lora_up', 'v_rw_k_k', 'v_rw_k_a', 'v_rw_r_k', 'v_rw_ln_g', 'v_rw_ln_b', 'v_fox_b_f', 'v_w_up_rwkv', 'v_w_up_fox', 'v_w_out', 'v_ple_proj', 'v_ple_gate_w', 'v_ple_norm_g', 'v_final_norm_g']
TWIN_OUTPUTS = ['loss', 'grad_x', 'grad_norm_g', 'grad_w_in', 'grad_rw_shift_mu', 'grad_rw_w0', 'grad_rw_w_lora_up', 'grad_rw_a0', 'grad_rw_a_lora_up', 'grad_rw_k_k', 'grad_rw_k_a', 'grad_rw_r_k', 'grad_rw_ln_g', 'grad_rw_ln_b', 'grad_fox_b_f', 'grad_w_up_rwkv', 'grad_w_up_fox', 'grad_w_out', 'grad_ple_proj', 'grad_ple_gate_w', 'grad_ple_norm_g', 'grad_final_norm_g', 'delta_norm_g', 'delta_w_in', 'delta_rw_shift_mu', 'delta_rw_w0', 'delta_rw_w_lora_up', 'delta_rw_a0', 'delta_rw_a_lora_up', 'delta_rw_k_k', 'delta_rw_k_a', 'delta_rw_r_k', 'delta_rw_ln_g', 'delta_rw_ln_b', 'delta_fox_b_f', 'delta_w_up_rwkv', 'delta_w_up_fox', 'delta_w_out', 'delta_ple_proj', 'delta_ple_gate_w', 'delta_ple_norm_g', 'delta_final_norm_g', 'new_m_norm_g', 'new_m_w_in', 'new_m_rw_shift_mu', 'new_m_rw_w0', 'new_m_rw_w_lora_up', 'new_m_rw_a0', 'new_m_rw_a_lora_up', 'new_m_rw_k_k', 'new_m_rw_k_a', 'new_m_rw_r_k', 'new_m_rw_ln_g', 'new_m_rw_ln_b', 'new_m_fox_b_f', 'new_m_w_up_rwkv', 'new_m_w_up_fox', 'new_m_w_out', 'new_m_ple_proj', 'new_m_ple_gate_w', 'new_m_ple_norm_g', 'new_m_final_norm_g', 'new_v_norm_g', 'new_v_w_in', 'new_v_rw_shift_mu', 'new_v_rw_w0', 'new_v_rw_w_lora_up', 'new_v_rw_a0', 'new_v_rw_a_lora_up', 'new_v_rw_k_k', 'new_v_rw_k_a', 'new_v_rw_r_k', 'new_v_rw_ln_g', 'new_v_rw_ln_b', 'new_v_fox_b_f', 'new_v_w_up_rwkv', 'new_v_w_up_fox', 'new_v_w_out', 'new_v_ple_proj', 'new_v_ple_gate_w', 'new_v_ple_norm_g', 'new_v_final_norm_g']
TWIN_LEAF_KINDS = {'loss': 'loss', 'grad_x': 'grad_x', 'grad_norm_g': 'grad_w', 'grad_w_in': 'grad_w', 'grad_rw_shift_mu': 'grad_w', 'grad_rw_w0': 'grad_w', 'grad_rw_w_lora_up': 'grad_w', 'grad_rw_a0': 'grad_w', 'grad_rw_a_lora_up': 'grad_w', 'grad_rw_k_k': 'grad_w', 'grad_rw_k_a': 'grad_w', 'grad_rw_r_k': 'grad_w', 'grad_rw_ln_g': 'grad_w', 'grad_rw_ln_b': 'grad_w', 'grad_fox_b_f': 'grad_w', 'grad_w_up_rwkv': 'grad_w', 'grad_w_up_fox': 'grad_w', 'grad_w_out': 'grad_w', 'grad_ple_proj': 'grad_w', 'grad_ple_gate_w': 'grad_w', 'grad_ple_norm_g': 'grad_w', 'grad_final_norm_g': 'grad_w', 'delta_norm_g': 'delta_w', 'delta_w_in': 'delta_w', 'delta_rw_shift_mu': 'delta_w', 'delta_rw_w0': 'delta_w', 'delta_rw_w_lora_up': 'delta_w', 'delta_rw_a0': 'delta_w', 'delta_rw_a_lora_up': 'delta_w', 'delta_rw_k_k': 'delta_w', 'delta_rw_k_a': 'delta_w', 'delta_rw_r_k': 'delta_w', 'delta_rw_ln_g': 'delta_w', 'delta_rw_ln_b': 'delta_w', 'delta_fox_b_f': 'delta_w', 'delta_w_up_rwkv': 'delta_w', 'delta_w_up_fox': 'delta_w', 'delta_w_out': 'delta_w', 'delta_ple_proj': 'delta_w', 'delta_ple_gate_w': 'delta_w', 'delta_ple_norm_g': 'delta_w', 'delta_final_norm_g': 'delta_w', 'new_m_norm_g': 'new_m', 'new_m_w_in': 'new_m', 'new_m_rw_shift_mu': 'new_m', 'new_m_rw_w0': 'new_m', 'new_m_rw_w_lora_up': 'new_m', 'new_m_rw_a0': 'new_m', 'new_m_rw_a_lora_up': 'new_m', 'new_m_rw_k_k': 'new_m', 'new_m_rw_k_a': 'new_m', 'new_m_rw_r_k': 'new_m', 'new_m_rw_ln_g': 'new_m', 'new_m_rw_ln_b': 'new_m', 'new_m_fox_b_f': 'new_m', 'new_m_w_up_rwkv': 'new_m', 'new_m_w_up_fox': 'new_m', 'new_m_w_out': 'new_m', 'new_m_ple_proj': 'new_m', 'new_m_ple_gate_w': 'new_m', 'new_m_ple_norm_g': 'new_m', 'new_m_final_norm_g': 'new_m', 'new_v_norm_g': 'new_v', 'new_v_w_in': 'new_v', 'new_v_rw_shift_mu': 'new_v', 'new_v_rw_w0': 'new_v', 'new_v_rw_w_lora_up': 'new_v', 'new_v_rw_a0': 'new_v', 'new_v_rw_a_lora_up': 'new_v', 'new_v_rw_k_k': 'new_v', 'new_v_rw_k_a': 'new_v', 'new_v_rw_r_k': 'new_v', 'new_v_rw_ln_g': 'new_v', 'new_v_rw_ln_b': 'new_v', 'new_v_fox_b_f': 'new_v', 'new_v_w_up_rwkv': 'new_v', 'new_v_w_up_fox': 'new_v', 'new_v_w_out': 'new_v', 'new_v_ple_proj': 'new_v', 'new_v_ple_gate_w': 'new_v', 'new_v_ple_norm_g': 'new_v', 'new_v_final_norm_g': 'new_v'}


def _forward(args):
    return _fwd_reference(*[args[k] for k in FWD_PARAMS])


def _output_shape():
    def fwd():
        inp = _fwd_setup_inputs(0)
        return _fwd_reference(*[inp[k] for k in FWD_PARAMS])
    out = _jax.eval_shape(fwd)
    return out.shape, out.dtype

N_MICROBATCH = 1
ADAM_LR = 0.001
ADAM_B1 = 0.9
ADAM_B2 = 0.999
ADAM_EPS = 1e-08
ADAM_WD = 0.01
ADAM_STEP = 10
PER_EXAMPLE_BATCH_AXIS = {'x': 0, 'p': 1, 'loss_target': 0}
SHARED_INPUTS = []
_WEIGHT_DTYPES = {'norm_g': _jnp.float32, 'w_in': _jnp.float32, 'rw_shift_mu': _jnp.float32, 'rw_w0': _jnp.float32, 'rw_w_lora_up': _jnp.float32, 'rw_a0': _jnp.float32, 'rw_a_lora_up': _jnp.float32, 'rw_k_k': _jnp.float32, 'rw_k_a': _jnp.float32, 'rw_r_k': _jnp.float32, 'rw_ln_g': _jnp.float32, 'rw_ln_b': _jnp.float32, 'fox_b_f': _jnp.float32, 'w_up_rwkv': _jnp.float32, 'w_up_fox': _jnp.float32, 'w_out': _jnp.float32, 'ple_proj': _jnp.float32, 'ple_gate_w': _jnp.float32, 'ple_norm_g': _jnp.float32, 'final_norm_g': _jnp.float32}
MOMENT_SCALE = {'norm_g': 4.694996e-02, 'w_in': 1.878622e-02, 'rw_shift_mu': 5.000725e-02, 'rw_w0': 1.131445e-02, 'rw_w_lora_up': 1.291505e-03, 'rw_a0': 1.237992e-02, 'rw_a_lora_up': 1.198647e-02, 'rw_k_k': 3.174153e-02, 'rw_k_a': 3.028574e-02, 'rw_r_k': 6.289320e-02, 'rw_ln_g': 2.961616e-02, 'rw_ln_b': 2.830672e-02, 'fox_b_f': 6.444217e-02, 'w_up_rwkv': 2.041340e-02, 'w_up_fox': 9.316508e-03, 'w_out': 2.221708e-02, 'ple_proj': 4.130214e-02, 'ple_gate_w': 1.601031e-02, 'ple_norm_g': 1.594707e-02, 'final_norm_g': 1.599738e+01}


def _to_microbatches(a, axis):
    t = _jnp.moveaxis(a, axis, 0)
    t = t.reshape((N_MICROBATCH, t.shape[0] // N_MICROBATCH) + t.shape[1:])
    return _jnp.moveaxis(t, 1, axis + 1)


def setup_inputs(seed: int = 0) -> dict:
    inp = _fwd_setup_inputs(seed)
    key = _jax.random.fold_in(_jax.random.key(seed), 7919)
    shape, _ = _output_shape()
    out = dict(inp)
    out["loss_target"] = _jax.random.normal(_jax.random.fold_in(key, 0), shape, _jnp.float32)
    for i, name in enumerate(TWIN_WEIGHTS):
        w = inp[name].astype(_jnp.float32)
        if MOMENT_SCALE is None:
            s = _jnp.sqrt(_jnp.mean(_jnp.square(w)) + 1e-30)
        else:
            s = MOMENT_SCALE[name]
        km, kv = _jax.random.split(_jax.random.fold_in(key, i + 1))
        out[name] = w
        out["m_" + name] = s * _jax.random.normal(km, w.shape, _jnp.float32)
        out["v_" + name] = (s * s) * _jax.random.uniform(kv, w.shape, _jnp.float32, 0.5, 1.5)
    if N_MICROBATCH > 1:
        for name, axis in PER_EXAMPLE_BATCH_AXIS.items():
            out[name] = _to_microbatches(out[name], axis)
    return {'x': out['x'], 'p': out['p'], 'norm_g': out['norm_g'], 'w_in': out['w_in'], 'rw_shift_mu': out['rw_shift_mu'], 'rw_w0': out['rw_w0'], 'rw_w_lora_up': out['rw_w_lora_up'], 'rw_a0': out['rw_a0'], 'rw_a_lora_up': out['rw_a_lora_up'], 'rw_k_k': out['rw_k_k'], 'rw_k_a': out['rw_k_a'], 'rw_r_k': out['rw_r_k'], 'rw_ln_g': out['rw_ln_g'], 'rw_ln_b': out['rw_ln_b'], 'fox_b_f': out['fox_b_f'], 'w_up_rwkv': out['w_up_rwkv'], 'w_up_fox': out['w_up_fox'], 'w_out': out['w_out'], 'ple_proj': out['ple_proj'], 'ple_gate_w': out['ple_gate_w'], 'ple_norm_g': out['ple_norm_g'], 'final_norm_g': out['final_norm_g'], 'loss_target': out['loss_target'], 'm_norm_g': out['m_norm_g'], 'm_w_in': out['m_w_in'], 'm_rw_shift_mu': out['m_rw_shift_mu'], 'm_rw_w0': out['m_rw_w0'], 'm_rw_w_lora_up': out['m_rw_w_lora_up'], 'm_rw_a0': out['m_rw_a0'], 'm_rw_a_lora_up': out['m_rw_a_lora_up'], 'm_rw_k_k': out['m_rw_k_k'], 'm_rw_k_a': out['m_rw_k_a'], 'm_rw_r_k': out['m_rw_r_k'], 'm_rw_ln_g': out['m_rw_ln_g'], 'm_rw_ln_b': out['m_rw_ln_b'], 'm_fox_b_f': out['m_fox_b_f'], 'm_w_up_rwkv': out['m_w_up_rwkv'], 'm_w_up_fox': out['m_w_up_fox'], 'm_w_out': out['m_w_out'], 'm_ple_proj': out['m_ple_proj'], 'm_ple_gate_w': out['m_ple_gate_w'], 'm_ple_norm_g': out['m_ple_norm_g'], 'm_final_norm_g': out['m_final_norm_g'], 'v_norm_g': out['v_norm_g'], 'v_w_in': out['v_w_in'], 'v_rw_shift_mu': out['v_rw_shift_mu'], 'v_rw_w0': out['v_rw_w0'], 'v_rw_w_lora_up': out['v_rw_w_lora_up'], 'v_rw_a0': out['v_rw_a0'], 'v_rw_a_lora_up': out['v_rw_a_lora_up'], 'v_rw_k_k': out['v_rw_k_k'], 'v_rw_k_a': out['v_rw_k_a'], 'v_rw_r_k': out['v_rw_r_k'], 'v_rw_ln_g': out['v_rw_ln_g'], 'v_rw_ln_b': out['v_rw_ln_b'], 'v_fox_b_f': out['v_fox_b_f'], 'v_w_up_rwkv': out['v_w_up_rwkv'], 'v_w_up_fox': out['v_w_up_fox'], 'v_w_out': out['v_w_out'], 'v_ple_proj': out['v_ple_proj'], 'v_ple_gate_w': out['v_ple_gate_w'], 'v_ple_norm_g': out['v_ple_norm_g'], 'v_final_norm_g': out['v_final_norm_g']}


def _loss(weights, diff, rest, loss_target):
    with _jax.named_scope("forward"):
        args = {**rest, TWIN_DIFF_INPUT: diff, **{k: w.astype(_WEIGHT_DTYPES[k]) for k, w in weights.items()}}
        y = _forward(args)
    with _jax.named_scope("loss_head"):
        err = _jnp.square(y.astype(_jnp.float32) - loss_target)
        return 0.5 * _jnp.sum(_jnp.mean(err, axis=-1)) if err.ndim else 0.5 * err


def _adamw(w, g, m, v):
    m = ADAM_B1 * m + (1.0 - ADAM_B1) * g
    v = ADAM_B2 * v + (1.0 - ADAM_B2) * _jnp.square(g)
    m_hat = m / (1.0 - ADAM_B1 ** ADAM_STEP)
    v_hat = v / (1.0 - ADAM_B2 ** ADAM_STEP)
    delta = -ADAM_LR * (m_hat / (_jnp.sqrt(v_hat) + ADAM_EPS) + ADAM_WD * w)
    return delta, m, v


def reference(x, p, norm_g, w_in, rw_shift_mu, rw_w0, rw_w_lora_up, rw_a0, rw_a_lora_up, rw_k_k, rw_k_a, rw_r_k, rw_ln_g, rw_ln_b, fox_b_f, w_up_rwkv, w_up_fox, w_out, ple_proj, ple_gate_w, ple_norm_g, final_norm_g, loss_target, m_norm_g, m_w_in, m_rw_shift_mu, m_rw_w0, m_rw_w_lora_up, m_rw_a0, m_rw_a_lora_up, m_rw_k_k, m_rw_k_a, m_rw_r_k, m_rw_ln_g, m_rw_ln_b, m_fox_b_f, m_w_up_rwkv, m_w_up_fox, m_w_out, m_ple_proj, m_ple_gate_w, m_ple_norm_g, m_final_norm_g, v_norm_g, v_w_in, v_rw_shift_mu, v_rw_w0, v_rw_w_lora_up, v_rw_a0, v_rw_a_lora_up, v_rw_k_k, v_rw_k_a, v_rw_r_k, v_rw_ln_g, v_rw_ln_b, v_fox_b_f, v_w_up_rwkv, v_w_up_fox, v_w_out, v_ple_proj, v_ple_gate_w, v_ple_norm_g, v_final_norm_g):
    given = dict(x=x, p=p, norm_g=norm_g, w_in=w_in, rw_shift_mu=rw_shift_mu, rw_w0=rw_w0, rw_w_lora_up=rw_w_lora_up, rw_a0=rw_a0, rw_a_lora_up=rw_a_lora_up, rw_k_k=rw_k_k, rw_k_a=rw_k_a, rw_r_k=rw_r_k, rw_ln_g=rw_ln_g, rw_ln_b=rw_ln_b, fox_b_f=fox_b_f, w_up_rwkv=w_up_rwkv, w_up_fox=w_up_fox, w_out=w_out, ple_proj=ple_proj, ple_gate_w=ple_gate_w, ple_norm_g=ple_norm_g, final_norm_g=final_norm_g, loss_target=loss_target, m_norm_g=m_norm_g, m_w_in=m_w_in, m_rw_shift_mu=m_rw_shift_mu, m_rw_w0=m_rw_w0, m_rw_w_lora_up=m_rw_w_lora_up, m_rw_a0=m_rw_a0, m_rw_a_lora_up=m_rw_a_lora_up, m_rw_k_k=m_rw_k_k, m_rw_k_a=m_rw_k_a, m_rw_r_k=m_rw_r_k, m_rw_ln_g=m_rw_ln_g, m_rw_ln_b=m_rw_ln_b, m_fox_b_f=m_fox_b_f, m_w_up_rwkv=m_w_up_rwkv, m_w_up_fox=m_w_up_fox, m_w_out=m_w_out, m_ple_proj=m_ple_proj, m_ple_gate_w=m_ple_gate_w, m_ple_norm_g=m_ple_norm_g, m_final_norm_g=m_final_norm_g, v_norm_g=v_norm_g, v_w_in=v_w_in, v_rw_shift_mu=v_rw_shift_mu, v_rw_w0=v_rw_w0, v_rw_w_lora_up=v_rw_w_lora_up, v_rw_a0=v_rw_a0, v_rw_a_lora_up=v_rw_a_lora_up, v_rw_k_k=v_rw_k_k, v_rw_k_a=v_rw_k_a, v_rw_r_k=v_rw_r_k, v_rw_ln_g=v_rw_ln_g, v_rw_ln_b=v_rw_ln_b, v_fox_b_f=v_fox_b_f, v_w_up_rwkv=v_w_up_rwkv, v_w_up_fox=v_w_up_fox, v_w_out=v_w_out, v_ple_proj=v_ple_proj, v_ple_gate_w=v_ple_gate_w, v_ple_norm_g=v_ple_norm_g, v_final_norm_g=v_final_norm_g)
    weights = {n: given[n] for n in TWIN_WEIGHTS}
    shared = {n: given[n] for n in SHARED_INPUTS}
    per_example = {n: given[n] for n in ['x', 'p']}
    grad_fn = _jax.value_and_grad(_loss, argnums=(0, 1))

    def one_microbatch(ex, loss_target):
        ex = dict(ex)
        diff = ex.pop(TWIN_DIFF_INPUT)
        return grad_fn(weights, diff, {**shared, **ex}, loss_target)

    if N_MICROBATCH == 1:
        loss, (grad_w, grad_x) = one_microbatch(per_example, given["loss_target"])
    else:
        def body(carry, xs):
            loss_sum, grad_sum = carry
            l_k, (gw_k, gx_k) = one_microbatch(xs[0], xs[1])
            with _jax.named_scope("update"):
                return (loss_sum + l_k, _jax.tree.map(_jnp.add, grad_sum, gw_k)), gx_k

        init = (_jnp.zeros((), _jnp.float32), _jax.tree.map(_jnp.zeros_like, weights))
        (loss, grad_w), grad_x = _jax.lax.scan(body, init, (per_example, given["loss_target"]))
    with _jax.named_scope("update"):
        delta_w, new_m, new_v = {}, {}, {}
        for n in TWIN_WEIGHTS:
            delta_w[n], new_m[n], new_v[n] = _adamw(weights[n], grad_w[n], given["m_" + n], given["v_" + n])
    return (loss, grad_x, *[grad_w[n] for n in TWIN_WEIGHTS], *[delta_w[n] for n in TWIN_WEIGHTS],
            *[new_m[n] for n in TWIN_WEIGHTS], *[new_v[n] for n in TWIN_WEIGHTS])
```

```python
import functools

import jax
import jax.numpy as jnp
from jax import lax
from jax.experimental import pallas as pl
from jax.experimental.pallas import tpu as pltpu

F32 = jnp.float32
BF16 = jnp.bfloat16
HIGHEST = lax.Precision.HIGHEST
MESH = pl.DeviceIdType.MESH

LANES = 128
HEAD = 64
NORM_EPS = 1e-6
GN_EPS = 64e-5
ADAM_LR = 0.001
ADAM_B1 = 0.9
ADAM_B2 = 0.999
ADAM_EPS = 1e-08
ADAM_WD = 0.01
ADAM_STEP = 10
N_SHARD = 4
SLAB_W = 1024
VMEM_LIMIT = 56 * 1024 * 1024


def _params(*sem):
    return pltpu.CompilerParams(dimension_semantics=sem, vmem_limit_bytes=VMEM_LIMIT)


def _tile(n, cands):
    for c in cands:
        if c <= n and n % c == 0:
            return c
    return n


_ROW_TILES = (512, 256, 128, 64, 32, 16, 8)


def _dot(a, b, prec=None):
    return lax.dot_general(a, b, (((1,), (0,)), ((), ())), precision=prec, preferred_element_type=F32)


def _dot_nt(a, b, prec=None):
    return lax.dot_general(a, b, (((1,), (1,)), ((), ())), precision=prec, preferred_element_type=F32)


def _dot_tn(a, b, prec=None):
    return lax.dot_general(a, b, (((0,), (0,)), ((), ())), precision=prec, preferred_element_type=F32)


@jax.custom_vjp
def _bdot(x, w):
    return _dot(x.astype(BF16), w.astype(BF16))


def _bdot_fwd(x, w):
    return _bdot(x, w), (x, w)


def _bdot_bwd(res, ct):
    x, w = res
    return _dot_nt(ct.astype(BF16), w.astype(BF16)), _dot_tn(x.astype(BF16), ct.astype(BF16))


_bdot.defvjp(_bdot_fwd, _bdot_bwd)


def _head_matrix(width):
    c = lax.broadcasted_iota(jnp.int32, (width, LANES), 0)
    h = lax.broadcasted_iota(jnp.int32, (width, LANES), 1)
    return (c // HEAD == h).astype(F32)


def _head_sum(x, e):
    return _dot_nt(_dot(x, e, HIGHEST), e, HIGHEST)


def _softplus(x):
    return jnp.maximum(x, 0.0) + jnp.log1p(jnp.exp(-jnp.abs(x)))


def _lane_pick(x, idx):
    lane = lax.broadcasted_iota(jnp.int32, x.shape, 1)
    return jnp.sum(jnp.where(lane == idx, x, 0.0), axis=1, keepdims=True)


def _matmul(name, a, b, *, ta=False, tb=False, add=None, out_dtype=F32):
    m, k = (a.shape[1], a.shape[0]) if ta else a.shape
    n = b.shape[0] if tb else b.shape[1]
    tm = _tile(m, (512, 256, 128))
    tn = _tile(n, (1408, 1024, 768, 640, 512, 384, 256, 128))
    tk = _tile(k, (512, 256, 128, 64, 32, 16))
    nk = k // tk
    dims = (((0 if ta else 1,), (1 if tb else 0,)), ((), ()))

    def body(*refs):
        a_ref, b_ref = refs[0], refs[1]
        o_ref, acc_ref = refs[-2], refs[-1]
        kk = pl.program_id(2)

        @pl.when(kk == 0)
        def _():
            acc_ref[...] = jnp.zeros_like(acc_ref)

        acc_ref[...] += lax.dot_general(a_ref[...].astype(BF16), b_ref[...].astype(BF16), dims,
                                        preferred_element_type=F32)

        @pl.when(kk == nk - 1)
        def _():
            r = acc_ref[...]
            if add is not None:
                r = r + refs[2][...].astype(F32)
            o_ref[...] = r.astype(o_ref.dtype)

    a_spec = pl.BlockSpec((tk, tm), lambda i, j, kk: (kk, i)) if ta else pl.BlockSpec((tm, tk), lambda i, j, kk: (i, kk))
    b_spec = pl.BlockSpec((tn, tk), lambda i, j, kk: (j, kk)) if tb else pl.BlockSpec((tk, tn), lambda i, j, kk: (kk, j))
    o_spec = pl.BlockSpec((tm, tn), lambda i, j, kk: (i, j))
    ins, specs = [a, b], [a_spec, b_spec]
    if add is not None:
        ins.append(add)
        specs.append(o_spec)
    return pl.pallas_call(
        body, name=name, grid=(m // tm, n // tn, nk), in_specs=specs, out_specs=o_spec,
        out_shape=jax.ShapeDtypeStruct((m, n), out_dtype),
        scratch_shapes=[pltpu.VMEM((tm, tn), F32)],
        compiler_params=_params("parallel", "parallel", "arbitrary"),
    )(*ins)


def _rows(name, fn, n_rows, tile, ins, outs, accs=()):
    n_in, n_out = len(ins), len(outs)

    def body(*refs):
        i = pl.program_id(0)
        vals = fn(i, *[r[...] for r in refs[:n_in]])
        for r, v in zip(refs[n_in:n_in + n_out], vals[:n_out]):
            r[...] = v.astype(r.dtype)
        for r, v in zip(refs[n_in + n_out:], vals[n_out:]):
            @pl.when(i == 0)
            def _(r=r, v=v):
                r[...] = v

            @pl.when(i > 0)
            def _(r=r, v=v):
                r[...] += v

    out_specs = [pl.BlockSpec((tile, w), lambda i: (i, 0)) for w, _ in outs]
    out_specs += [pl.BlockSpec(s, lambda i: (0, 0)) for s in accs]
    out_shape = [jax.ShapeDtypeStruct((n_rows, w), d) for w, d in outs]
    out_shape += [jax.ShapeDtypeStruct(s, F32) for s in accs]
    return pl.pallas_call(
        body, name=name, grid=(n_rows // tile,), in_specs=[s for _, s in ins], out_specs=out_specs,
        out_shape=out_shape, compiler_params=_params("arbitrary"),
    )(*[a for a, _ in ins])


def _row_spec(tile, width, col=0):
    return pl.BlockSpec((tile, width), lambda i: (i, col))


def _full_spec(shape):
    return pl.BlockSpec(shape, lambda i: (0,) * len(shape))


def _prev_rows_spec(tile, width):
    return pl.BlockSpec((8, width), lambda i: (jnp.maximum(i * (tile // 8) - 1, 0), 0))


def _next_rows_spec(tile, width, n_tiles):
    return pl.BlockSpec((8, width), lambda i: (jnp.minimum(i + 1, n_tiles - 1), 0))


def _row_of(x8, idx):
    r = lax.broadcasted_iota(jnp.int32, x8.shape, 0)
    return jnp.sum(jnp.where(r == idx, x8, 0.0), axis=0, keepdims=True)


def _rms(x, g):
    return x * lax.rsqrt(jnp.mean(x * x, axis=-1, keepdims=True) + NORM_EPS) * g


def _shifted(i, z, prev8):
    first = jnp.where(i > 0, _row_of(prev8, 7), 0.0)
    row = lax.broadcasted_iota(jnp.int32, z.shape, 0)
    return jnp.where(row == 0, first, pltpu.roll(z, 1, 0))


def _rw_pre(z, zp, mu, w0, a0, wup, aup, k_k, k_a, *, cw):
    zs = z + (zp - z) * mu
    r, k, v, g = (zs[:, j * cw:(j + 1) * cw] for j in range(4))
    lo = zs[:, 4 * cw:4 * cw + LANES]
    w_raw = w0 + _bdot(jnp.tanh(lo), wup)
    decay = jnp.exp(-jnp.exp(-_softplus(-w_raw) - 0.5))
    a = jax.nn.sigmoid(a0 + _bdot(lo, aup))
    e = _head_matrix(cw)
    kk = k * k_k
    kk = kk / jnp.maximum(jnp.sqrt(_head_sum(kk * kk, e)), 1e-12)
    k_mod = k * (1.0 + (a - 1.0) * k_a)
    return r, decay, k_mod, v, kk, a, g


def _rw_post(y, r, k_mod, v, g, ln_g, ln_b, r_k, *, cw):
    e = _head_matrix(cw)
    mu = _head_sum(y, e) * (1.0 / HEAD)
    d = y - mu
    var = _head_sum(d * d, e) * (1.0 / HEAD)
    yn = d * lax.rsqrt(var + GN_EPS) * ln_g + ln_b
    bonus = _head_sum(r * k_mod * r_k, e) * v
    return (yn + bonus) * (g * jax.nn.sigmoid(g))


def _merge(zg, u_rw, u_fox, *, d):
    return jax.nn.sigmoid(zg[:, :d]) * u_rw + jax.nn.sigmoid(zg[:, d:]) * u_fox


def _head_loss(x1, ple, gl, gf, tgt):
    x2 = x1 + ple * jax.nn.sigmoid(gl)
    err = _rms(x2, gf) - tgt
    return 0.5 * jnp.sum(jnp.mean(err * err, axis=-1, keepdims=True), axis=0, keepdims=True)


def _unit_lower_inverse(lo, c):
    ri = lax.broadcasted_iota(jnp.int32, (c, c), 0)
    ci = lax.broadcasted_iota(jnp.int32, (c, c), 1)
    x = (ri == ci).astype(F32)
    for s in range(c - 1):
        col = jnp.sum(jnp.where(ci == s, lo, 0.0), axis=1, keepdims=True)
        row = jnp.sum(jnp.where(ri == s, x, 0.0), axis=0, keepdims=True)
        x = x - col * row
    return x


def _rwkv_chunk(s0, r, w, k, v, kk, a, *, c):
    lane = lax.broadcasted_iota(jnp.int32, (1, LANES), 1)
    head_a = lane < HEAD
    ti = lax.broadcasted_iota(jnp.int32, (c, c), 0)
    si = lax.broadcasted_iota(jnp.int32, (c, c), 1)
    incl = si <= ti
    strict = si < ti
    logw = jnp.log(w)
    cum = _dot(incl.astype(F32), logw, HIGHEST)
    cum_end = jnp.sum(logw, axis=0, keepdims=True)
    g_inv = jnp.exp(-cum)
    to_end = jnp.exp(cum_end - cum)
    b = kk * a
    beta = b * g_inv
    kap = kk * jnp.exp(cum - logw)
    kt = k * g_inv
    rt = r * jnp.exp(cum)
    kap_s = _dot_nt(kap, s0, HIGHEST)
    rt_s = _dot_nt(rt, s0, HIGHEST)

    def per_head(m):
        kap_h = jnp.where(m, kap, 0.0)
        rt_h = jnp.where(m, rt, 0.0)
        lo = jnp.where(strict, _dot_nt(kap_h, beta, HIGHEST), 0.0)
        mm = jnp.where(strict, _dot_nt(kap_h, kt, HIGHEST), 0.0)
        arb = jnp.where(incl, _dot_nt(rt_h, beta, HIGHEST), 0.0)
        ark = jnp.where(incl, _dot_nt(rt_h, kt, HIGHEST), 0.0)
        return mm, arb, ark, _unit_lower_inverse(lo, c)

    m_a, arb_a, ark_a, t_a = per_head(head_a)
    m_b, arb_b, ark_b, t_b = per_head(jnp.logical_not(head_a))
    rhs = kap_s + jnp.where(head_a, _dot(m_a, v, HIGHEST), _dot(m_b, v, HIGHEST))
    u = -jnp.where(head_a, _dot(t_a, rhs, HIGHEST), _dot(t_b, rhs, HIGHEST))
    y = rt_s + jnp.where(head_a, _dot(arb_a, u, HIGHEST) + _dot(ark_a, v, HIGHEST),
                         _dot(arb_b, u, HIGHEST) + _dot(ark_b, v, HIGHEST))
    rr = lax.broadcasted_iota(jnp.int32, (LANES, LANES), 0) < HEAD
    cc = lax.broadcasted_iota(jnp.int32, (LANES, LANES), 1) < HEAD
    ds = _dot_tn(u, b * to_end, HIGHEST) + _dot_tn(v, k * to_end, HIGHEST)
    s1 = s0 * jnp.exp(cum_end) + jnp.where(rr == cc, ds, 0.0)
    return y, s1


def _scan_tiles(t):
    c = _tile(t, (32, 16, 8))
    return c, _tile(t, (512, 256, 128, 64, 32))


def _scan_fwd(r, w, k, v, kk, a):
    t, width = r.shape
    c, tb = _scan_tiles(t)
    n_pair, n_blk, n_cb = width // LANES, t // tb, tb // c

    def body(r_ref, w_ref, k_ref, v_ref, kk_ref, a_ref, y_ref, st_ref, s_scr):
        @pl.when(pl.program_id(1) == 0)
        def _():
            s_scr[...] = jnp.zeros_like(s_scr)

        def chunk(j, carry):
            sl = pl.ds(pl.multiple_of(j * c, c), c)
            s0 = s_scr[...]
            st_ref[0, j] = s0
            y, s1 = _rwkv_chunk(s0, r_ref[sl, :], w_ref[sl, :], k_ref[sl, :], v_ref[sl, :], kk_ref[sl, :],
                                a_ref[sl, :], c=c)
            y_ref[sl, :] = y
            s_scr[...] = s1
            return carry

        lax.fori_loop(0, n_cb, chunk, 0)

    blk = pl.BlockSpec((tb, LANES), lambda p, i: (i, p))
    return pl.pallas_call(
        body, name="rwkv_scan_fwd", grid=(n_pair, n_blk), in_specs=[blk] * 6,
        out_specs=[blk, pl.BlockSpec((1, n_cb, LANES, LANES), lambda p, i: (p, i, 0, 0))],
        out_shape=[jax.ShapeDtypeStruct((t, width), F32),
                   jax.ShapeDtypeStruct((n_pair, t // c, LANES, LANES), F32)],
        scratch_shapes=[pltpu.VMEM((LANES, LANES), F32)],
        compiler_params=_params("arbitrary", "arbitrary"),
    )(r, w, k, v, kk, a)


def _scan_bwd(r, w, k, v, kk, a, st, dy):
    t, width = r.shape
    c, tb = _scan_tiles(t)
    n_pair, n_blk, n_cb = width // LANES, t // tb, tb // c

    def body(r_ref, w_ref, k_ref, v_ref, kk_ref, a_ref, st_ref, dy_ref,
             dr_ref, dw_ref, dk_ref, dv_ref, dkk_ref, da_ref, ds_scr):
        @pl.when(pl.program_id(1) == 0)
        def _():
            ds_scr[...] = jnp.zeros_like(ds_scr)

        def chunk(jj, carry):
            j = n_cb - 1 - jj
            sl = pl.ds(pl.multiple_of(j * c, c), c)
            args = (st_ref[0, j], r_ref[sl, :], w_ref[sl, :], k_ref[sl, :], v_ref[sl, :], kk_ref[sl, :], a_ref[sl, :])
            _, vjp = jax.vjp(functools.partial(_rwkv_chunk, c=c), *args)
            ds0, dr, dw, dk, dv, dkk, da = vjp((dy_ref[sl, :], ds_scr[...]))
            ds_scr[...] = ds0
            dr_ref[sl, :] = dr
            dw_ref[sl, :] = dw
            dk_ref[sl, :] = dk
            dv_ref[sl, :] = dv
            dkk_ref[sl, :] = dkk
            da_ref[sl, :] = da
            return carry

        lax.fori_loop(0, n_cb, chunk, 0)

    blk = pl.BlockSpec((tb, LANES), lambda p, i: (n_blk - 1 - i, p))
    stb = pl.BlockSpec((1, n_cb, LANES, LANES), lambda p, i: (p, n_blk - 1 - i, 0, 0))
    return pl.pallas_call(
        body, name="rwkv_scan_bwd", grid=(n_pair, n_blk), in_specs=[blk] * 6 + [stb, blk], out_specs=[blk] * 6,
        out_shape=[jax.ShapeDtypeStruct((t, width), F32)] * 6,
        scratch_shapes=[pltpu.VMEM((LANES, LANES), F32)],
        compiler_params=_params("arbitrary", "arbitrary"),
    )(r, w, k, v, kk, a, st, dy)


NEG = -1e30


def _fox_cumsum(zf, b_pad, *, fw, fh):
    t = zf.shape[0]
    tile = _tile(t, (256, 128, 64, 32, 16, 8))

    def body(fl_ref, b_ref, c_ref, carry):
        @pl.when(pl.program_id(0) == 0)
        def _():
            carry[...] = jnp.zeros_like(carry)

        lane = lax.broadcasted_iota(jnp.int32, (tile, LANES), 1)
        logf = jnp.where(lane < fh, -_softplus(-(fl_ref[...] + b_ref[...])), 0.0)
        ri = lax.broadcasted_iota(jnp.int32, (tile, tile), 0)
        ci = lax.broadcasted_iota(jnp.int32, (tile, tile), 1)
        c_ref[...] = carry[...] + _dot((ci <= ri).astype(F32), logf, HIGHEST)
        carry[...] += jnp.sum(logf, axis=0, keepdims=True)

    return pl.pallas_call(
        body, name="fox_cumsum", grid=(t // tile,),
        in_specs=[_row_spec(tile, LANES, 4 * fw // LANES), _full_spec((1, LANES))],
        out_specs=_row_spec(tile, LANES), out_shape=jax.ShapeDtypeStruct((t, LANES), F32),
        scratch_shapes=[pltpu.VMEM((1, LANES), F32)], compiler_params=_params("arbitrary"),
    )(zf, b_pad)


def _fox_cumsum_bwd(zf, b_pad, dc, *, fw, fh):
    t = zf.shape[0]
    tile = _tile(t, (256, 128, 64, 32, 16, 8))
    n = t // tile

    def body(fl_ref, b_ref, dc_ref, dfl_ref, db_ref, carry):
        i = pl.program_id(0)

        @pl.when(i == 0)
        def _():
            carry[...] = jnp.zeros_like(carry)
            db_ref[...] = jnp.zeros_like(db_ref)

        lane = lax.broadcasted_iota(jnp.int32, (tile, LANES), 1)
        dc_t = dc_ref[...]
        ri = lax.broadcasted_iota(jnp.int32, (tile, tile), 0)
        ci = lax.broadcasted_iota(jnp.int32, (tile, tile), 1)
        dlogf = carry[...] + _dot((ci >= ri).astype(F32), dc_t, HIGHEST)
        carry[...] += jnp.sum(dc_t, axis=0, keepdims=True)
        dfl = jnp.where(lane < fh, dlogf * jax.nn.sigmoid(-(fl_ref[...] + b_ref[...])), 0.0)
        dfl_ref[...] = dfl.astype(dfl_ref.dtype)
        db_ref[...] += jnp.sum(dfl, axis=0, keepdims=True)

    rev = lambda col: pl.BlockSpec((tile, LANES), lambda i: (n - 1 - i, col))
    return pl.pallas_call(
        body, name="fox_cumsum_bwd", grid=(n,),
        in_specs=[rev(4 * fw // LANES), _full_spec((1, LANES)), rev(0)],
        out_specs=[rev(0), _full_spec((1, LANES))],
        out_shape=[jax.ShapeDtypeStruct((t, LANES), BF16), jax.ShapeDtypeStruct((1, LANES), F32)],
        scratch_shapes=[pltpu.VMEM((1, LANES), F32)], compiler_params=_params("arbitrary"),
    )(zf, b_pad, dc)


def _fox_tile(t):
    return _tile(t, (256, 128))


def _fox_fwd(zf, c, ct, *, fw):
    t = zf.shape[0]
    tq = _fox_tile(t)
    n_pair, n_q = fw // LANES, t // tq
    scale = HEAD ** -0.5

    def body(q_ref, k_ref, v_ref, g_ref, c_ref, ct_ref, o_ref, lse_ref, y_ref):
        hp, i = pl.program_id(0), pl.program_id(1)
        lane = lax.broadcasted_iota(jnp.int32, (1, LANES), 1)
        q = q_ref[...] * scale
        c_q = c_ref[...]
        qpos = i * tq + lax.broadcasted_iota(jnp.int32, (tq, tq), 0)
        kidx = lax.broadcasted_iota(jnp.int32, (tq, tq), 1)
        outs, lses = [], []
        for h in range(2):
            in_head = (lane >= h * HEAD) & (lane < (h + 1) * HEAD)
            qh = jnp.where(in_head, q, 0.0).astype(BF16)
            cq = _lane_pick(c_q, 2 * hp + h)

            def kv_step(j, carry, h=h, qh=qh, cq=cq):
                m, l, acc = carry
                ks = pl.ds(pl.multiple_of(j * tq, tq), tq)
                s = _dot_nt(qh, k_ref[ks, :].astype(BF16)) + cq - ct_ref[0, j, pl.ds(h, 1), :]
                s = jnp.where(qpos >= j * tq + kidx, s, NEG)
                m_new = jnp.maximum(m, jnp.max(s, axis=1, keepdims=True))
                p = jnp.exp(s - m_new)
                alpha = jnp.exp(m - m_new)
                l = l * alpha + jnp.sum(p, axis=1, keepdims=True)
                acc = acc * alpha + _dot(p.astype(BF16), v_ref[ks, :].astype(BF16))
                return m_new, l, acc

            init = (jnp.full((tq, 1), NEG, F32), jnp.zeros((tq, 1), F32), jnp.zeros((tq, LANES), F32))
            m, l, acc = lax.fori_loop(0, i + 1, kv_step, init)
            outs.append(acc / l)
            lses.append(m + jnp.log(l))
        head_a = lane < HEAD
        o = jnp.where(head_a, outs[0], outs[1])
        g = g_ref[...]
        o_ref[...] = o
        lse_ref[...] = jnp.where(head_a, lses[0], lses[1])
        y_ref[...] = (o * (g * jax.nn.sigmoid(g))).astype(y_ref.dtype)

    npw = fw // LANES
    blk = lambda col0: pl.BlockSpec((tq, LANES), lambda hp, i: (i, col0 + hp))
    res = lambda col0: pl.BlockSpec((t, LANES), lambda hp, i: (0, col0 + hp))
    out_blk = pl.BlockSpec((tq, LANES), lambda hp, i: (i, hp))
    return pl.pallas_call(
        body, name="fox_attn_fwd", grid=(n_pair, n_q),
        in_specs=[blk(0), res(npw), res(2 * npw), blk(3 * npw),
                  pl.BlockSpec((tq, LANES), lambda hp, i: (i, 0)),
                  pl.BlockSpec((1, n_q, 2, tq), lambda hp, i: (hp, 0, 0, 0))],
        out_specs=[out_blk, out_blk, out_blk],
        out_shape=[jax.ShapeDtypeStruct((t, fw), F32), jax.ShapeDtypeStruct((t, fw), F32),
                   jax.ShapeDtypeStruct((t, fw), BF16)],
        compiler_params=_params("arbitrary", "arbitrary"),
    )(zf, zf, zf, zf, c, ct)


def _fox_bwd(zf, do, lse, dd, c, ct, *, fw):
    t = zf.shape[0]
    tq = _fox_tile(t)
    n_pair, n_q = fw // LANES, t // tq
    scale = HEAD ** -0.5

    def body(q_ref, k_ref, v_ref, do_ref, lse_ref, dd_ref, c_ref, ct_ref,
             dq_ref, dk_ref, dv_ref, dcq_ref, dck_ref, dq_acc, dcq_acc):
        hp, j = pl.program_id(0), pl.program_id(1)

        @pl.when(j == 0)
        def _():
            dq_acc[...] = jnp.zeros_like(dq_acc)
            dcq_acc[...] = jnp.zeros_like(dcq_acc)

        lane = lax.broadcasted_iota(jnp.int32, (1, LANES), 1)
        kb = k_ref[...]
        vb = v_ref[...].astype(BF16)
        kpos = j * tq + lax.broadcasted_iota(jnp.int32, (tq, tq), 1)
        qidx = lax.broadcasted_iota(jnp.int32, (tq, tq), 0)
        dk = jnp.zeros((tq, LANES), F32)
        dv = jnp.zeros((tq, LANES), F32)
        for h in range(2):
            in_head = (lane >= h * HEAD) & (lane < (h + 1) * HEAD)
            kh = jnp.where(in_head, kb, 0.0).astype(BF16)
            ck = ct_ref[0, j, pl.ds(h, 1), :]

            def q_step(i, carry, h=h, in_head=in_head, kh=kh, ck=ck):
                dk, dv, dck = carry
                qs = pl.ds(pl.multiple_of(i * tq, tq), tq)
                qh = jnp.where(in_head, q_ref[qs, :] * scale, 0.0).astype(BF16)
                doh = jnp.where(in_head, do_ref[qs, :], 0.0).astype(BF16)
                cq = _lane_pick(c_ref[qs, :], 2 * hp + h)
                lse_h = _lane_pick(lse_ref[qs, :], h * HEAD)
                dd_h = _lane_pick(dd_ref[qs, :], h * HEAD)
                s = _dot_nt(qh, kh) + cq - ck
                p = jnp.where(i * tq + qidx >= kpos, jnp.exp(s - lse_h), 0.0)
                ds = p * (_dot_nt(doh, vb) - dd_h)
                dsb = ds.astype(BF16)
                dv = dv + _dot_tn(p.astype(BF16), doh)
                dk = dk + _dot_tn(dsb, qh)
                dq_acc[qs, :] += _dot(dsb, kh)
                dcq_acc[qs, :] += jnp.where(lane == h, jnp.sum(ds, axis=1, keepdims=True), 0.0)
                return dk, dv, dck - jnp.sum(ds, axis=0, keepdims=True)

            dk, dv, dck = lax.fori_loop(j, n_q, q_step, (dk, dv, jnp.zeros((1, tq), F32)))
            dck_ref[0, 0, pl.ds(h, 1), :] = dck
        dk_ref[...] = dk.astype(dk_ref.dtype)
        dv_ref[...] = dv.astype(dv_ref.dtype)

        @pl.when(j == n_q - 1)
        def _():
            dq_ref[...] = (dq_acc[...] * scale).astype(dq_ref.dtype)
            dcq_ref[...] = dcq_acc[...]

    npw = fw // LANES
    res_z = lambda col0: pl.BlockSpec((t, LANES), lambda hp, j: (0, col0 + hp))
    blk_z = lambda col0: pl.BlockSpec((tq, LANES), lambda hp, j: (j, col0 + hp))
    res = pl.BlockSpec((t, LANES), lambda hp, j: (0, hp))
    blk = pl.BlockSpec((tq, LANES), lambda hp, j: (j, hp))
    return pl.pallas_call(
        body, name="fox_attn_bwd", grid=(n_pair, n_q),
        in_specs=[res_z(0), blk_z(npw), blk_z(2 * npw), res, res, res,
                  pl.BlockSpec((t, LANES), lambda hp, j: (0, 0)),
                  pl.BlockSpec((1, n_q, 2, tq), lambda hp, j: (hp, 0, 0, 0))],
        out_specs=[res, blk, blk, res, pl.BlockSpec((1, 1, 2, tq), lambda hp, j: (hp, j, 0, 0))],
        out_shape=[jax.ShapeDtypeStruct((t, fw), BF16), jax.ShapeDtypeStruct((t, fw), BF16),
                   jax.ShapeDtypeStruct((t, fw), BF16), jax.ShapeDtypeStruct((t, fw), F32),
                   jax.ShapeDtypeStruct((n_pair, n_q, 2, tq), F32)],
        scratch_shapes=[pltpu.VMEM((t, LANES), F32), pltpu.VMEM((t, LANES), F32)],
        compiler_params=_params("arbitrary", "arbitrary"),
    )(zf, zf, zf, do, lse, dd, c, ct)


def _adamw_math(w, g, m, v):
    m = ADAM_B1 * m + (1.0 - ADAM_B1) * g
    v = ADAM_B2 * v + (1.0 - ADAM_B2) * jnp.square(g)
    m_hat = m / (1.0 - ADAM_B1 ** ADAM_STEP)
    v_hat = v / (1.0 - ADAM_B2 ** ADAM_STEP)
    delta = -ADAM_LR * (m_hat / (jnp.sqrt(v_hat) + ADAM_EPS) + ADAM_WD * w)
    return delta, m, v


def _adamw(name, w, g, m, v):
    rows, cols = w.shape
    tile = _tile(rows, (128, 64, 32, 16, 8))
    spec = _row_spec(tile, cols)
    return _rows(name, lambda i, *a: _adamw_math(*a), rows, tile, [(w, spec), (g, spec), (m, spec), (v, spec)],
                 [(cols, F32)] * 3)


def _place():
    return lax.axis_index("x"), lax.axis_index("y"), lax.axis_index("c")


def _other_chips(x, y):
    return [(1 - x, y), (x, 1 - y), (1 - x, 1 - y)]


HBM_SPEC = pl.BlockSpec(memory_space=pltpu.HBM)


def _all_gather_shards(slab):
    rows, width = slab.shape
    rh = rows // 2

    def body(src_ref, out_ref, send_sems, recv_sems, local_sem):
        x, y, c = _place()
        me = 2 * x + y
        sibling = (x, y, 1 - c)
        chips = _other_chips(x, y)

        def part(chip, half):
            return out_ref.at[chip, pl.ds(half * rh, rh), :]

        def copy(k, src, dst, to):
            return pltpu.make_async_remote_copy(src_ref=src, dst_ref=dst, send_sem=send_sems.at[k],
                                                recv_sem=recv_sems.at[k], device_id=to, device_id_type=MESH)

        mine = pltpu.make_async_copy(src_ref, out_ref.at[me], local_sem)
        mine.start()
        first = [copy(j, src_ref.at[pl.ds(c * rh, rh), :], part(me, c), (px, py, c)) for j, (px, py) in enumerate(chips)]
        for cp in first:
            cp.start()
        passed = [copy(3 + j, part(2 * px + py, c), part(2 * px + py, c), sibling) for j, (px, py) in enumerate(chips)]
        for j, (px, py) in enumerate(chips):
            copy(j, part(2 * px + py, c), part(2 * px + py, c), sibling).wait_recv()
            passed[j].start()
        for j, (px, py) in enumerate(chips):
            copy(3 + j, part(2 * px + py, 1 - c), part(2 * px + py, 1 - c), sibling).wait_recv()
        for cp in first + passed:
            cp.wait_send()
        mine.wait()

    return pl.pallas_call(
        body, name="weights_all_gather", in_specs=[HBM_SPEC], out_specs=HBM_SPEC,
        out_shape=jax.ShapeDtypeStruct((N_SHARD, rows, width), slab.dtype),
        scratch_shapes=[pltpu.SemaphoreType.DMA((6,)), pltpu.SemaphoreType.DMA((6,)), pltpu.SemaphoreType.DMA],
    )(slab)


def _sibling_exchange(g):
    _, rows, width = g.shape
    rh = rows // 2

    def body(g_ref, out_ref, send_sems, recv_sems):
        x, y, c = _place()
        copies = [pltpu.make_async_remote_copy(
            src_ref=g_ref.at[s, pl.ds((1 - c) * rh, rh), :], dst_ref=out_ref.at[s], send_sem=send_sems.at[s],
            recv_sem=recv_sems.at[s], device_id=(x, y, 1 - c), device_id_type=MESH) for s in range(N_SHARD)]
        for cp in copies:
            cp.start()
        for cp in copies:
            cp.wait()

    return pl.pallas_call(
        body, name="grad_sibling_exchange", in_specs=[HBM_SPEC], out_specs=HBM_SPEC,
        out_shape=jax.ShapeDtypeStruct((N_SHARD, rh, width), g.dtype),
        scratch_shapes=[pltpu.SemaphoreType.DMA((N_SHARD,)), pltpu.SemaphoreType.DMA((N_SHARD,))],
    )(g)


def _add_sibling(g, got):
    _, rows, width = g.shape
    rh = rows // 2
    tile = _tile(rh, (256, 128, 64, 32, 16, 8))
    c_arr = jnp.reshape(lax.axis_index("c"), (1,)).astype(jnp.int32)

    def body(c_ref, a_ref, b_ref, o_ref):
        o_ref[...] = a_ref[0] + b_ref[...]

    return pl.pallas_call(
        body, name="grad_add_sibling",
        grid_spec=pltpu.PrefetchScalarGridSpec(
            num_scalar_prefetch=1, grid=(N_SHARD, rh // tile),
            in_specs=[pl.BlockSpec((1, 1, tile, width), lambda s, i, c: (s, c[0], i, 0)),
                      pl.BlockSpec((1, tile, width), lambda s, i, c: (s, i, 0))],
            out_specs=pl.BlockSpec((1, tile, width), lambda s, i, c: (s, i, 0))),
        out_shape=jax.ShapeDtypeStruct((N_SHARD, rh, width), F32),
        compiler_params=_params("parallel", "parallel"),
    )(c_arr, g.reshape(N_SHARD, 2, rh, width), got)


def _chip_exchange(p):
    def body(p_ref, out_ref, send_sems, recv_sems, local_sem):
        x, y, c = _place()
        me = 2 * x + y
        mine = pltpu.make_async_copy(p_ref.at[me], out_ref.at[me], local_sem)
        mine.start()
        copies = [pltpu.make_async_remote_copy(
            src_ref=p_ref.at[2 * px + py], dst_ref=out_ref.at[me], send_sem=send_sems.at[j],
            recv_sem=recv_sems.at[j], device_id=(px, py, c), device_id_type=MESH)
            for j, (px, py) in enumerate(_other_chips(x, y))]
        for cp in copies:
            cp.start()
        for cp in copies:
            cp.wait()
        mine.wait()

    return pl.pallas_call(
        body, name="grad_chip_exchange", in_specs=[HBM_SPEC], out_specs=HBM_SPEC,
        out_shape=jax.ShapeDtypeStruct(p.shape, p.dtype),
        scratch_shapes=[pltpu.SemaphoreType.DMA((3,)), pltpu.SemaphoreType.DMA((3,)), pltpu.SemaphoreType.DMA],
    )(p)


def _sum_chips(q):
    _, rh, width = q.shape
    tile = _tile(rh, (256, 128, 64, 32, 16, 8))

    def body(q_ref, o_ref):
        o_ref[...] = ((q_ref[0] + q_ref[1]) + q_ref[2]) + q_ref[3]

    return pl.pallas_call(
        body, name="grad_sum_chips", grid=(rh // tile,),
        in_specs=[pl.BlockSpec((N_SHARD, tile, width), lambda i: (0, i, 0))],
        out_specs=pl.BlockSpec((tile, width), lambda i: (i, 0)),
        out_shape=jax.ShapeDtypeStruct((rh, width), F32), compiler_params=_params("parallel"),
    )(q)


def _join_halves(half):
    rh, width = half.shape

    def body(h_ref, out_ref, send_sem, recv_sem, local_sem):
        x, y, c = _place()
        mine = pltpu.make_async_copy(h_ref, out_ref.at[pl.ds(c * rh, rh), :], local_sem)
        mine.start()
        cp = pltpu.make_async_remote_copy(
            src_ref=h_ref, dst_ref=out_ref.at[pl.ds(c * rh, rh), :], send_sem=send_sem, recv_sem=recv_sem,
            device_id=(x, y, 1 - c), device_id_type=MESH)
        cp.start()
        cp.wait_send()
        pltpu.make_async_remote_copy(
            src_ref=h_ref, dst_ref=out_ref.at[pl.ds((1 - c) * rh, rh), :], send_sem=send_sem, recv_sem=recv_sem,
            device_id=(x, y, 1 - c), device_id_type=MESH).wait_recv()
        mine.wait()

    return pl.pallas_call(
        body, name="grad_join_halves", in_specs=[HBM_SPEC], out_specs=HBM_SPEC,
        out_shape=jax.ShapeDtypeStruct((2 * rh, width), half.dtype),
        scratch_shapes=[pltpu.SemaphoreType.DMA, pltpu.SemaphoreType.DMA, pltpu.SemaphoreType.DMA],
    )(half)


def _reduce_scatter(g):
    chip_sum = _add_sibling(g, _sibling_exchange(g))
    return _join_halves(_sum_chips(_chip_exchange(chip_sum)))


def _all_reduce_small(v):
    rows = v.shape[0]

    def body(v_ref, out_ref, gather, send_sems, recv_sems):
        x, y, c = _place()
        gather[4 * x + 2 * y + c] = v_ref[...]
        flips = [(dx, dy, dc) for dx in (0, 1) for dy in (0, 1) for dc in (0, 1)][1:]
        peers = [((x + dx) % 2, (y + dy) % 2, (c + dc) % 2) for dx, dy, dc in flips]
        copies = [pltpu.make_async_remote_copy(
            src_ref=v_ref, dst_ref=gather.at[4 * x + 2 * y + c], send_sem=send_sems.at[j], recv_sem=recv_sems.at[j],
            device_id=peer, device_id_type=MESH) for j, peer in enumerate(peers)]
        for cp in copies:
            cp.start()
        for j, (px, py, pc) in enumerate(peers):
            pltpu.make_async_remote_copy(
                src_ref=v_ref, dst_ref=gather.at[4 * px + 2 * py + pc], send_sem=send_sems.at[j],
                recv_sem=recv_sems.at[j], device_id=(px, py, pc), device_id_type=MESH).wait_recv()
        for cp in copies:
            cp.wait_send()
        acc = gather[0]
        for d in range(1, 8):
            acc = acc + gather[d]
        out_ref[...] = acc

    vm = pl.BlockSpec(memory_space=pltpu.VMEM)
    return pl.pallas_call(
        body, name="small_grads_all_reduce", in_specs=[vm], out_specs=vm,
        out_shape=jax.ShapeDtypeStruct(v.shape, F32),
        scratch_shapes=[pltpu.VMEM((8, rows, LANES), F32), pltpu.SemaphoreType.DMA((7,)), pltpu.SemaphoreType.DMA((7,))],
    )(v)


def _pack_slab(parts, rows):
    flat = jnp.concatenate([p.reshape(-1) for p in parts])
    return jnp.pad(flat, (0, rows * SLAB_W - flat.shape[0])).reshape(rows, SLAB_W)


def _unpack_slab(slab, shapes):
    flat = slab.reshape(-1)
    out, off = [], 0
    for s in shapes:
        n = 1
        for d in s:
            n *= d
        out.append(flat[off:off + n].reshape(s))
        off += n
    return out


def _pad_lanes(v):
    v = v.reshape(1, -1)
    return jnp.pad(v, ((0, 0), (0, -v.shape[1] % LANES)))


def _pack_small(vs, rows):
    flat = jnp.concatenate([_pad_lanes(v) for v in vs], axis=1)
    return jnp.pad(flat, ((0, 0), (0, rows * LANES - flat.shape[1]))).reshape(rows, LANES)


def _unpack_small(packed, shapes):
    flat = packed.reshape(-1)
    out, off = [], 0
    for s in shapes:
        n = 1
        for d in s:
            n *= d
        out.append(flat[off:off + n].reshape(s))
        off += n + (-n % LANES)
    return out


BIG = ("w_in", "rw_w_lora_up", "rw_a_lora_up", "w_up_rwkv", "w_up_fox", "w_out", "ple_proj", "ple_gate_w")
ROW_SHARDED = ("w_out", "ple_gate_w")
SMALL = ("norm_g", "rw_shift_mu", "rw_w0", "rw_a0", "rw_k_k", "rw_k_a", "rw_r_k", "rw_ln_g", "rw_ln_b", "fox_b_f",
         "ple_norm_g", "final_norm_g")
WEIGHTS = ("norm_g", "w_in", "rw_shift_mu", "rw_w0", "rw_w_lora_up", "rw_a0", "rw_a_lora_up", "rw_k_k", "rw_k_a",
           "rw_r_k", "rw_ln_g", "rw_ln_b", "fox_b_f", "w_up_rwkv", "w_up_fox", "w_out", "ple_proj", "ple_gate_w",
           "ple_norm_g", "final_norm_g")


def _full_from_shards(name, gathered):
    axis = 0 if name in ROW_SHARDED else 1
    return jnp.concatenate([gathered[s] for s in range(N_SHARD)], axis=axis)


def _shards_of_full(name, full):
    axis = 0 if name in ROW_SHARDED else 1
    return jnp.split(full, N_SHARD, axis=axis)


def kernel(x, p, norm_g, w_in, rw_shift_mu, rw_w0, rw_w_lora_up, rw_a0, rw_a_lora_up, rw_k_k, rw_k_a, rw_r_k, rw_ln_g, rw_ln_b, fox_b_f, w_up_rwkv, w_up_fox, w_out, ple_proj, ple_gate_w, ple_norm_g, final_norm_g, loss_target, m_norm_g, m_w_in, m_rw_shift_mu, m_rw_w0, m_rw_w_lora_up, m_rw_a0, m_rw_a_lora_up, m_rw_k_k, m_rw_k_a, m_rw_r_k, m_rw_ln_g, m_rw_ln_b, m_fox_b_f, m_w_up_rwkv, m_w_up_fox, m_w_out, m_ple_proj, m_ple_gate_w, m_ple_norm_g, m_final_norm_g, v_norm_g, v_w_in, v_rw_shift_mu, v_rw_w0, v_rw_w_lora_up, v_rw_a0, v_rw_a_lora_up, v_rw_k_k, v_rw_k_a, v_rw_r_k, v_rw_ln_g, v_rw_ln_b, v_fox_b_f, v_w_up_rwkv, v_w_up_fox, v_w_out, v_ple_proj, v_ple_gate_w, v_ple_norm_g, v_final_norm_g):
    wts = dict(norm_g=norm_g, w_in=w_in, rw_shift_mu=rw_shift_mu, rw_w0=rw_w0, rw_w_lora_up=rw_w_lora_up, rw_a0=rw_a0,
               rw_a_lora_up=rw_a_lora_up, rw_k_k=rw_k_k, rw_k_a=rw_k_a, rw_r_k=rw_r_k, rw_ln_g=rw_ln_g, rw_ln_b=rw_ln_b,
               fox_b_f=fox_b_f, w_up_rwkv=w_up_rwkv, w_up_fox=w_up_fox, w_out=w_out, ple_proj=ple_proj,
               ple_gate_w=ple_gate_w, ple_norm_g=ple_norm_g, final_norm_g=final_norm_g)
    mom = dict(norm_g=m_norm_g, w_in=m_w_in, rw_shift_mu=m_rw_shift_mu, rw_w0=m_rw_w0, rw_w_lora_up=m_rw_w_lora_up,
               rw_a0=m_rw_a0, rw_a_lora_up=m_rw_a_lora_up, rw_k_k=m_rw_k_k, rw_k_a=m_rw_k_a, rw_r_k=m_rw_r_k,
               rw_ln_g=m_rw_ln_g, rw_ln_b=m_rw_ln_b, fox_b_f=m_fox_b_f, w_up_rwkv=m_w_up_rwkv, w_up_fox=m_w_up_fox,
               w_out=m_w_out, ple_proj=m_ple_proj, ple_gate_w=m_ple_gate_w, ple_norm_g=m_ple_norm_g,
               final_norm_g=m_final_norm_g)
    vel = dict(norm_g=v_norm_g, w_in=v_w_in, rw_shift_mu=v_rw_shift_mu, rw_w0=v_rw_w0, rw_w_lora_up=v_rw_w_lora_up,
               rw_a0=v_rw_a0, rw_a_lora_up=v_rw_a_lora_up, rw_k_k=v_rw_k_k, rw_k_a=v_rw_k_a, rw_r_k=v_rw_r_k,
               rw_ln_g=v_rw_ln_g, rw_ln_b=v_rw_ln_b, fox_b_f=v_fox_b_f, w_up_rwkv=v_w_up_rwkv, w_up_fox=v_w_up_fox,
               w_out=v_w_out, ple_proj=v_ple_proj, ple_gate_w=v_ple_gate_w, ple_norm_g=v_ple_norm_g,
               final_norm_g=v_final_norm_g)

    t, d = x.shape[1], x.shape[2]
    cw = rw_w0.shape[1]
    lr = rw_w_lora_up.shape[1]
    fh = fox_b_f.shape[1]
    fw = fh * HEAD
    rw_cols = 4 * cw + 2 * lr
    fox_cols = 4 * fw + fh
    assert 2 * lr == LANES and cw % LANES == 0 and fw % LANES == 0 and fh <= LANES
    xs = x[0]
    ps = p[0, 0]
    tgt = loss_target[0]

    shard_shapes = [wts[n].shape[1:] for n in BIG]
    n_elem = sum(s[0] * s[1] for s in shard_shapes)
    slab_rows = -(-n_elem // (SLAB_W * 32)) * 32
    w_slab = _pack_slab([wts[n][0] for n in BIG], slab_rows)
    cast_tile = _tile(slab_rows, (256, 128, 64, 32, 16))
    (w_slab16,) = _rows("weights_to_bf16", lambda i, a: (a,), slab_rows, cast_tile,
                        [(w_slab, _row_spec(cast_tile, SLAB_W))], [(SLAB_W, BF16)])
    gathered = _all_gather_shards(w_slab16)
    per_shard = [_unpack_slab(gathered[s], shard_shapes) for s in range(N_SHARD)]
    full = {n: _full_from_shards(n, [per_shard[s][j] for s in range(N_SHARD)]) for j, n in enumerate(BIG)}
    w_rw = full["w_in"][:, :rw_cols]
    w_fox = jnp.pad(full["w_in"][:, rw_cols:rw_cols + fox_cols], ((0, 0), (0, LANES - fh)))
    w_gate = full["w_in"][:, rw_cols + fox_cols:]
    wup_pad = jnp.pad(full["rw_w_lora_up"], ((0, lr), (0, 0)))
    aup_pad = jnp.pad(full["rw_a_lora_up"], ((lr, 0), (0, 0)))
    b_pad = _pad_lanes(fox_b_f)
    r_k_row = rw_r_k.reshape(1, cw)
    gf_row = final_norm_g.reshape(1, d)

    tile = _tile(t, (256, 128, 64, 32, 16, 8))
    tile_s = _tile(t, (128, 64, 32, 16, 8))
    n_s = t // tile_s
    full2 = lambda a: (a, _full_spec(a.shape))

    (h,) = _rows("norm1", lambda i, a, g: (_rms(a, g),), t, tile, [(xs, _row_spec(tile, d)), full2(norm_g)], [(d, BF16)])
    z_rw = _matmul("proj_rw", h, w_rw)
    z_fox = _matmul("proj_fox", h, w_fox)
    z_gate = _matmul("proj_gate", h, w_gate)

    pre_consts = [full2(rw_shift_mu), full2(rw_w0), full2(rw_a0), full2(wup_pad), full2(aup_pad), full2(rw_k_k),
                  full2(rw_k_a)]

    def pre_fwd(i, z, prev8, *consts):
        return _rw_pre(z, _shifted(i, z, prev8), *consts, cw=cw)

    r_, w_, k_, v_, kk_, a_, g_ = _rows(
        "rwkv_pre", pre_fwd, t, tile_s,
        [(z_rw, _row_spec(tile_s, rw_cols)), (z_rw, _prev_rows_spec(tile_s, rw_cols))] + pre_consts, [(cw, F32)] * 7)
    y_scan, states = _scan_fwd(r_, w_, k_, v_, kk_, a_)
    post_consts = [full2(rw_ln_g), full2(rw_ln_b), full2(r_k_row)]
    post_rows = lambda *arrs: [(a, _row_spec(tile_s, cw)) for a in arrs]
    (y_rw,) = _rows("rwkv_post", lambda i, *a: (_rw_post(*a, cw=cw),), t, tile_s,
                    post_rows(y_scan, r_, k_, v_, g_) + post_consts, [(cw, BF16)])

    c_fox = _fox_cumsum(z_fox, b_pad, fw=fw, fh=fh)
    tq = _fox_tile(t)
    n_pair_f = fw // LANES
    ct_fox = c_fox[:, :fh].T.reshape(n_pair_f, 2, t // tq, tq).transpose(0, 2, 1, 3)
    o_fox, lse_fox, y_fox = _fox_fwd(z_fox, c_fox, ct_fox, fw=fw)

    u_rw = _matmul("up_rwkv", y_rw, full["w_up_rwkv"])
    u_fox = _matmul("up_fox", y_fox, full["w_up_fox"])
    (merged,) = _rows("merge", lambda i, zg, a, b: (_merge(zg, a, b, d=d),), t, tile,
                      [(z_gate, _row_spec(tile, 2 * d)), (u_rw, _row_spec(tile, d)), (u_fox, _row_spec(tile, d))],
                      [(d, BF16)])
    x1 = _matmul("out_proj", merged, full["w_out"], add=xs)
    (n2,) = _rows("norm2", lambda i, a, g: (_rms(a, g),), t, tile, [(x1, _row_spec(tile, d)), full2(ple_norm_g)],
                  [(d, BF16)])
    gl = _matmul("ple_gate", n2, full["ple_gate_w"])
    ple = _matmul("ple_proj", ps, full["ple_proj"])

    def head_bwd(i, x1_t, ple_t, gl_t, gf, tg):
        loss, vjp = jax.vjp(lambda a, b, cc, g: _head_loss(a, b, cc, g, tg), x1_t, ple_t, gl_t, gf)
        dx1, dple, dgl, dgf = vjp(jnp.ones((1, 1), F32))
        return dx1, dple, dgl, jnp.broadcast_to(loss, (1, LANES)), dgf

    dx2, dple, dgl, loss_row, d_gf = _rows(
        "loss_head", head_bwd, t, tile_s,
        [(x1, _row_spec(tile_s, d)), (ple, _row_spec(tile_s, d)), (gl, _row_spec(tile_s, d)), full2(gf_row),
         (tgt, _row_spec(tile_s, d))],
        [(d, F32), (d, BF16), (d, BF16)], [(1, LANES), (1, d)])

    g_ple_proj = _matmul("d_ple_proj", ps, dple, ta=True)
    g_ple_gate = _matmul("d_ple_gate_w", n2, dgl, ta=True)
    dn2 = _matmul("d_n2", dgl, full["ple_gate_w"], tb=True)

    def norm_bwd(i, a, g, dh, res):
        _, vjp = jax.vjp(_rms, a, g)
        da, dg = vjp(dh)
        return res + da, dg

    dx1, d_g2 = _rows("norm2_bwd", norm_bwd, t, tile_s,
                      [(x1, _row_spec(tile_s, d)), full2(ple_norm_g), (dn2, _row_spec(tile_s, d)),
                       (dx2, _row_spec(tile_s, d))], [(d, F32)], [(1, d)])
    g_w_out = _matmul("d_w_out", merged, dx1, ta=True)
    dmerged = _matmul("d_merged", dx1, full["w_out"], tb=True)

    def merge_bwd(i, zg, a, b, dm):
        _, vjp = jax.vjp(functools.partial(_merge, d=d), zg, a, b)
        return vjp(dm)

    dz_gate, du_rw, du_fox = _rows(
        "merge_bwd", merge_bwd, t, tile_s,
        [(z_gate, _row_spec(tile_s, 2 * d)), (u_rw, _row_spec(tile_s, d)), (u_fox, _row_spec(tile_s, d)),
         (dmerged, _row_spec(tile_s, d))], [(2 * d, BF16), (d, BF16), (d, BF16)])
    g_up_rw = _matmul("d_w_up_rwkv", y_rw, du_rw, ta=True)
    g_up_fox = _matmul("d_w_up_fox", y_fox, du_fox, ta=True)
    dy_rw = _matmul("d_y_rwkv", du_rw, full["w_up_rwkv"], tb=True)
    dy_fox = _matmul("d_y_fox", du_fox, full["w_up_fox"], tb=True)

    def post_bwd(i, y, r, k, v, g, ln_g, ln_b, r_k, dy):
        _, vjp = jax.vjp(functools.partial(_rw_post, cw=cw), y, r, k, v, g, ln_g, ln_b, r_k)
        return vjp(dy)

    dys, dr1, dk1, dv1, dg1, d_ln_g, d_ln_b, d_r_k = _rows(
        "rwkv_post_bwd", post_bwd, t, tile_s,
        post_rows(y_scan, r_, k_, v_, g_) + post_consts + post_rows(dy_rw), [(cw, F32)] * 5, [(1, cw)] * 3)
    dr2, dw2, dk2, dv2, dkk2, da2 = _scan_bwd(r_, w_, k_, v_, kk_, a_, states, dys)

    def pre_bwd(i, z, prev8, mu, w0, a0, wup, aup, k_k, k_a, dr_a, dr_b, dk_a, dk_b, dv_a, dv_b, dw, dkk, da, dg):
        zp = _shifted(i, z, prev8)
        _, vjp = jax.vjp(functools.partial(_rw_pre, cw=cw), z, zp, mu, w0, a0, wup, aup, k_k, k_a)
        dz, dzp, dmu, dw0, da0, dwup, daup, dk_k, dk_a = vjp((dr_a + dr_b, dw, dk_a + dk_b, dv_a + dv_b, dkk, da, dg))
        row = lax.broadcasted_iota(jnp.int32, dz.shape, 0)
        dz = dz + jnp.where(row < tile_s - 1, pltpu.roll(dzp, tile_s - 1, 0), 0.0)
        first = jnp.where(lax.broadcasted_iota(jnp.int32, (8, dz.shape[1]), 0) == 0, _row_of(dzp, 0), 0.0)
        return dz, first, dmu, dw0, da0, dwup, daup, dk_k, dk_a

    def pre_bwd_call():
        n_in = 2 + len(pre_consts) + 10
        ins = ([(z_rw, _row_spec(tile_s, rw_cols)), (z_rw, _prev_rows_spec(tile_s, rw_cols))] + pre_consts
               + post_rows(dr1, dr2, dk1, dk2, dv1, dv2, dw2, dkk2, da2, dg1))

        def body(*refs):
            i = pl.program_id(0)
            vals = pre_bwd(i, *[r[...] for r in refs[:n_in]])
            refs[n_in][...] = vals[0]
            refs[n_in + 1][...] = vals[1]
            for r, v in zip(refs[n_in + 2:], vals[2:]):
                @pl.when(i == 0)
                def _(r=r, v=v):
                    r[...] = v

                @pl.when(i > 0)
                def _(r=r, v=v):
                    r[...] += v

        acc_shapes = [(1, rw_cols), (1, cw), (1, cw), (LANES, cw), (LANES, cw), (1, cw), (1, cw)]
        return pl.pallas_call(
            body, name="rwkv_pre_bwd", grid=(n_s,), in_specs=[s for _, s in ins],
            out_specs=[_row_spec(tile_s, rw_cols), pl.BlockSpec((8, rw_cols), lambda i: (i, 0))]
            + [_full_spec(s) for s in acc_shapes],
            out_shape=[jax.ShapeDtypeStruct((t, rw_cols), F32), jax.ShapeDtypeStruct((8 * n_s, rw_cols), F32)]
            + [jax.ShapeDtypeStruct(s, F32) for s in acc_shapes],
            compiler_params=_params("arbitrary"),
        )(*[a for a, _ in ins])

    dz_main, dz_first, d_mu, d_w0, d_a0, d_wup, d_aup, d_k_k, d_k_a = pre_bwd_call()

    def add_next_row(i, dz, nxt8):
        row = lax.broadcasted_iota(jnp.int32, dz.shape, 0)
        carry = jnp.where(i < n_s - 1, _row_of(nxt8, 0), 0.0)
        return (dz + jnp.where(row == tile_s - 1, carry, 0.0),)

    (dz_rw,) = _rows("rwkv_shift_bwd", add_next_row, t, tile_s,
                     [(dz_main, _row_spec(tile_s, rw_cols)), (dz_first, _next_rows_spec(tile_s, rw_cols, n_s))],
                     [(rw_cols, BF16)])

    def fox_post_bwd(i, o, g, dy):
        _, vjp = jax.vjp(lambda oo, gg: oo * (gg * jax.nn.sigmoid(gg)), o, g)
        do, dg = vjp(dy)
        return do, _head_sum(do * o, _head_matrix(fw)), dg

    do_fox, dd_fox, dg_fox = _rows(
        "fox_post_bwd", fox_post_bwd, t, tile_s,
        [(o_fox, _row_spec(tile_s, fw)), (z_fox, _row_spec(tile_s, fw, 3)), (dy_fox, _row_spec(tile_s, fw))],
        [(fw, F32), (fw, F32), (fw, BF16)])
    dq_f, dk_f, dv_f, dcq, dck = _fox_bwd(z_fox, do_fox, lse_fox, dd_fox, c_fox, ct_fox, fw=fw)
    dc = (dcq.reshape(t, n_pair_f, LANES)[:, :, :2].reshape(t, fh)
          + dck.transpose(0, 2, 1, 3).reshape(fh, t).T)
    dfl, d_bf = _fox_cumsum_bwd(z_fox, b_pad, jnp.pad(dc, ((0, 0), (0, LANES - fh))), fw=fw, fh=fh)
    dz_fox = jnp.concatenate([dq_f, dk_f, dv_f, dg_fox, dfl], axis=1)

    g_w_rw = _matmul("d_w_in_rw", h, dz_rw, ta=True)
    g_w_fox = _matmul("d_w_in_fox", h, dz_fox, ta=True)
    g_w_gate = _matmul("d_w_in_gate", h, dz_gate, ta=True)
    dh = _matmul("d_h_rw", dz_rw, w_rw, tb=True)
    dh = _matmul("d_h_fox", dz_fox, w_fox, tb=True, add=dh)
    dh = _matmul("d_h_gate", dz_gate, w_gate, tb=True, add=dh)
    grad_x, d_g1 = _rows("norm1_bwd", norm_bwd, t, tile_s,
                         [(xs, _row_spec(tile_s, d)), full2(norm_g), (dh, _row_spec(tile_s, d)),
                          (dx1, _row_spec(tile_s, d))], [(d, F32)], [(1, d)])

    g_full = {
        "w_in": jnp.concatenate([g_w_rw, g_w_fox[:, :fox_cols], g_w_gate], axis=1),
        "rw_w_lora_up": d_wup[:lr], "rw_a_lora_up": d_aup[lr:], "w_up_rwkv": g_up_rw, "w_up_fox": g_up_fox,
        "w_out": g_w_out, "ple_proj": g_ple_proj, "ple_gate_w": g_ple_gate,
    }
    by_shard = [_shards_of_full(n, g_full[n]) for n in BIG]
    g_slab = jnp.stack([_pack_slab([by_shard[j][s] for j in range(len(BIG))], slab_rows) for s in range(N_SHARD)])
    g_red = _unpack_slab(_reduce_scatter(g_slab), shard_shapes)
    grads = {n: g_red[j][None] for j, n in enumerate(BIG)}

    small_parts = dict(norm_g=d_g1, rw_shift_mu=d_mu, rw_w0=d_w0, rw_a0=d_a0, rw_k_k=d_k_k, rw_k_a=d_k_a, rw_r_k=d_r_k,
                       rw_ln_g=d_ln_g, rw_ln_b=d_ln_b, fox_b_f=d_bf[:, :fh], ple_norm_g=d_g2, final_norm_g=d_gf)
    n_small = sum(-(-wts[n].size // LANES) for n in SMALL)
    small_rows = -(-n_small // 8) * 8
    small_shapes = [wts[n].shape for n in SMALL]
    g_small = _all_reduce_small(_pack_small([small_parts[n] for n in SMALL], small_rows))
    for n, g in zip(SMALL, _unpack_small(g_small, small_shapes)):
        grads[n] = g

    delta, new_m, new_v = {}, {}, {}
    for n in BIG:
        shp = wts[n].shape
        two = lambda a: a.reshape(shp[1], shp[2])
        dl, mm, vv = _adamw("adamw_" + n, two(wts[n]), two(grads[n]), two(mom[n]), two(vel[n]))
        delta[n], new_m[n], new_v[n] = dl.reshape(shp), mm.reshape(shp), vv.reshape(shp)
    dl, mm, vv = _adamw("adamw_small", _pack_small([wts[n] for n in SMALL], small_rows), g_small,
                        _pack_small([mom[n] for n in SMALL], small_rows), _pack_small([vel[n] for n in SMALL], small_rows))
    for store, packed in ((delta, dl), (new_m, mm), (new_v, vv)):
        for n, a in zip(SMALL, _unpack_small(packed, small_shapes)):
            store[n] = a

    loss = lax.psum(loss_row[0, 0], ("x", "y", "c"))
    return (loss, grad_x[None], *[grads[n] for n in WEIGHTS], *[delta[n] for n in WEIGHTS],
            *[new_m[n] for n in WEIGHTS], *[new_v[n] for n in WEIGHTS])
```

```python
import functools

import jax
import jax.numpy as jnp
from jax import lax
from jax.experimental import pallas as pl
from jax.experimental.pallas import tpu as pltpu

F32 = jnp.float32
BF16 = jnp.bfloat16
HIGHEST = lax.Precision.HIGHEST
MESH = pl.DeviceIdType.MESH

LANES = 128
HEAD = 64
NORM_EPS = 1e-6
GN_EPS = 64e-5
ADAM_LR = 0.001
ADAM_B1 = 0.9
ADAM_B2 = 0.999
ADAM_EPS = 1e-08
ADAM_WD = 0.01
ADAM_STEP = 10
N_SHARD = 4
SLAB_W = 1024
VMEM_LIMIT = 56 * 1024 * 1024
PAIRS_PER_STEP = 4


def _params(*sem):
    return pltpu.CompilerParams(dimension_semantics=sem, vmem_limit_bytes=VMEM_LIMIT)


def _tile(n, cands):
    for c in cands:
        if c <= n and n % c == 0:
            return c
    return n


_ROW_TILES = (512, 256, 128, 64, 32, 16, 8)


def _dot(a, b, prec=None):
    return lax.dot_general(a, b, (((1,), (0,)), ((), ())), precision=prec, preferred_element_type=F32)


def _dot_nt(a, b, prec=None):
    return lax.dot_general(a, b, (((1,), (1,)), ((), ())), precision=prec, preferred_element_type=F32)


def _dot_tn(a, b, prec=None):
    return lax.dot_general(a, b, (((0,), (0,)), ((), ())), precision=prec, preferred_element_type=F32)


@jax.custom_vjp
def _bdot(x, w):
    return _dot(x.astype(BF16), w.astype(BF16))


def _bdot_fwd(x, w):
    return _bdot(x, w), (x, w)


def _bdot_bwd(res, ct):
    x, w = res
    return _dot_nt(ct.astype(BF16), w.astype(BF16)), _dot_tn(x.astype(BF16), ct.astype(BF16))


_bdot.defvjp(_bdot_fwd, _bdot_bwd)


def _head_matrix(width):
    c = lax.broadcasted_iota(jnp.int32, (width, LANES), 0)
    h = lax.broadcasted_iota(jnp.int32, (width, LANES), 1)
    return (c // HEAD == h).astype(F32)


def _head_sum(x, e):
    return _dot_nt(_dot(x, e, HIGHEST), e, HIGHEST)


def _softplus(x):
    return jnp.maximum(x, 0.0) + jnp.log1p(jnp.exp(-jnp.abs(x)))


def _lane_pick(x, idx):
    lane = lax.broadcasted_iota(jnp.int32, x.shape, 1)
    return jnp.sum(jnp.where(lane == idx, x, 0.0), axis=1, keepdims=True)


def _matmul(name, a, b, *, ta=False, tb=False, add=None, out_dtype=F32):
    m, k = (a.shape[1], a.shape[0]) if ta else a.shape
    n = b.shape[0] if tb else b.shape[1]
    tm = _tile(m, (512, 256, 128))
    tn = _tile(n, (1408, 1024, 768, 640, 512, 384, 256, 128))
    tk = _tile(k, (1408, 1024, 768, 640, 512, 384, 256, 128, 64, 32, 16))
    nk = k // tk
    dims = (((0 if ta else 1,), (1 if tb else 0,)), ((), ()))

    def body(*refs):
        a_ref, b_ref = refs[0], refs[1]
        o_ref, acc_ref = refs[-2], refs[-1]
        kk = pl.program_id(2)

        @pl.when(kk == 0)
        def _():
            acc_ref[...] = jnp.zeros_like(acc_ref)

        acc_ref[...] += lax.dot_general(a_ref[...].astype(BF16), b_ref[...].astype(BF16), dims,
                                        preferred_element_type=F32)

        @pl.when(kk == nk - 1)
        def _():
            r = acc_ref[...]
            if add is not None:
                r = r + refs[2][...].astype(F32)
            o_ref[...] = r.astype(o_ref.dtype)

    a_spec = pl.BlockSpec((tk, tm), lambda i, j, kk: (kk, i)) if ta else pl.BlockSpec((tm, tk), lambda i, j, kk: (i, kk))
    b_spec = pl.BlockSpec((tn, tk), lambda i, j, kk: (j, kk)) if tb else pl.BlockSpec((tk, tn), lambda i, j, kk: (kk, j))
    o_spec = pl.BlockSpec((tm, tn), lambda i, j, kk: (i, j))
    ins, specs = [a, b], [a_spec, b_spec]
    if add is not None:
        ins.append(add)
        specs.append(o_spec)
    return pl.pallas_call(
        body, name=name, grid=(m // tm, n // tn, nk), in_specs=specs, out_specs=o_spec,
        out_shape=jax.ShapeDtypeStruct((m, n), out_dtype),
        scratch_shapes=[pltpu.VMEM((tm, tn), F32)],
        compiler_params=_params("parallel", "parallel", "arbitrary"),
    )(*ins)


def _rows(name, fn, n_rows, tile, ins, outs, accs=()):
    n_in, n_out = len(ins), len(outs)

    def body(*refs):
        i = pl.program_id(0)
        vals = fn(i, *[r[...] for r in refs[:n_in]])
        for r, v in zip(refs[n_in:n_in + n_out], vals[:n_out]):
            r[...] = v.astype(r.dtype)
        for r, v in zip(refs[n_in + n_out:], vals[n_out:]):
            @pl.when(i == 0)
            def _(r=r, v=v):
                r[...] = v

            @pl.when(i > 0)
            def _(r=r, v=v):
                r[...] += v

    out_specs = [pl.BlockSpec((tile, w), lambda i: (i, 0)) for w, _ in outs]
    out_specs += [pl.BlockSpec(s, lambda i: (0, 0)) for s in accs]
    out_shape = [jax.ShapeDtypeStruct((n_rows, w), d) for w, d in outs]
    out_shape += [jax.ShapeDtypeStruct(s, F32) for s in accs]
    return pl.pallas_call(
        body, name=name, grid=(n_rows // tile,), in_specs=[s for _, s in ins], out_specs=out_specs,
        out_shape=out_shape, compiler_params=_params("arbitrary"),
    )(*[a for a, _ in ins])


def _row_spec(tile, width, col=0):
    return pl.BlockSpec((tile, width), lambda i: (i, col))


def _full_spec(shape):
    return pl.BlockSpec(shape, lambda i: (0,) * len(shape))


def _prev_rows_spec(tile, width):
    return pl.BlockSpec((8, width), lambda i: (jnp.maximum(i * (tile // 8) - 1, 0), 0))


def _next_rows_spec(tile, width, n_tiles):
    return pl.BlockSpec((8, width), lambda i: (jnp.minimum(i + 1, n_tiles - 1), 0))


def _row_of(x8, idx):
    r = lax.broadcasted_iota(jnp.int32, x8.shape, 0)
    return jnp.sum(jnp.where(r == idx, x8, 0.0), axis=0, keepdims=True)


def _rms(x, g):
    return x * lax.rsqrt(jnp.mean(x * x, axis=-1, keepdims=True) + NORM_EPS) * g


def _shifted(i, z, prev8):
    first = jnp.where(i > 0, _row_of(prev8, 7), 0.0)
    row = lax.broadcasted_iota(jnp.int32, z.shape, 0)
    return jnp.where(row == 0, first, pltpu.roll(z, 1, 0))


def _rw_pre(z, zp, mu, w0, a0, wup, aup, k_k, k_a, *, cw):
    zs = z + (zp - z) * mu
    r, k, v, g = (zs[:, j * cw:(j + 1) * cw] for j in range(4))
    lo = zs[:, 4 * cw:4 * cw + LANES]
    w_raw = w0 + _bdot(jnp.tanh(lo), wup)
    decay = jnp.exp(-jnp.exp(-_softplus(-w_raw) - 0.5))
    a = jax.nn.sigmoid(a0 + _bdot(lo, aup))
    e = _head_matrix(cw)
    kk = k * k_k
    kk = kk / jnp.maximum(jnp.sqrt(_head_sum(kk * kk, e)), 1e-12)
    k_mod = k * (1.0 + (a - 1.0) * k_a)
    return r, decay, k_mod, v, kk, a, g


def _rw_post(y, r, k_mod, v, g, ln_g, ln_b, r_k, *, cw):
    e = _head_matrix(cw)
    mu = _head_sum(y, e) * (1.0 / HEAD)
    d = y - mu
    var = _head_sum(d * d, e) * (1.0 / HEAD)
    yn = d * lax.rsqrt(var + GN_EPS) * ln_g + ln_b
    bonus = _head_sum(r * k_mod * r_k, e) * v
    return (yn + bonus) * (g * jax.nn.sigmoid(g))


def _merge(zg, u_rw, u_fox, *, d):
    return jax.nn.sigmoid(zg[:, :d]) * u_rw + jax.nn.sigmoid(zg[:, d:]) * u_fox


def _head_loss(x1, ple, gl, gf, tgt):
    x2 = x1 + ple * jax.nn.sigmoid(gl)
    err = _rms(x2, gf) - tgt
    return 0.5 * jnp.sum(jnp.mean(err * err, axis=-1, keepdims=True), axis=0, keepdims=True)


def _unit_lower_inverse(lo, c):
    n = lo.shape[0]
    ri = lax.broadcasted_iota(jnp.int32, (n, c, c), 1)
    ci = lax.broadcasted_iota(jnp.int32, (n, c, c), 2)
    x = (ri == ci).astype(F32)
    for s in range(c - 1):
        col = jnp.sum(jnp.where(ci == s, lo, 0.0), axis=2, keepdims=True)
        row = jnp.sum(jnp.where(ri == s, x, 0.0), axis=1, keepdims=True)
        x = x - col * row
    return x


def _rwkv_chunk(s0, r, w, k, v, kk, a, *, c):
    pairs = range(len(s0))
    lane = lax.broadcasted_iota(jnp.int32, (1, LANES), 1)
    heads = (lane < HEAD, lane >= HEAD)
    ti = lax.broadcasted_iota(jnp.int32, (c, c), 0)
    si = lax.broadcasted_iota(jnp.int32, (c, c), 1)
    incl = si <= ti
    strict = si < ti
    tri = incl.astype(F32)
    logw = [jnp.log(w[p]) for p in pairs]
    cum = [_dot(tri, logw[p], HIGHEST) for p in pairs]
    cum_end = [jnp.sum(logw[p], axis=0, keepdims=True) for p in pairs]
    g_inv = [jnp.exp(-cum[p]) for p in pairs]
    to_end = [jnp.exp(cum_end[p] - cum[p]) for p in pairs]
    b = [kk[p] * a[p] for p in pairs]
    beta = [b[p] * g_inv[p] for p in pairs]
    kap = [kk[p] * jnp.exp(cum[p] - logw[p]) for p in pairs]
    kt = [k[p] * g_inv[p] for p in pairs]
    rt = [r[p] * jnp.exp(cum[p]) for p in pairs]
    lhs = [jnp.concatenate([jnp.where(m, x[p], 0.0) for x in (kap, rt) for m in heads], axis=0) for p in pairs]
    vs_beta = [_dot_nt(lhs[p], beta[p], HIGHEST) for p in pairs]
    vs_kt = [_dot_nt(lhs[p], kt[p], HIGHEST) for p in pairs]
    strict2 = jnp.concatenate([strict, strict], axis=0)
    incl2 = jnp.concatenate([incl, incl], axis=0)
    lo = [jnp.where(strict2, vs_beta[p][:2 * c], 0.0) for p in pairs]
    mm = [jnp.where(strict2, vs_kt[p][:2 * c], 0.0) for p in pairs]
    arb = [jnp.where(incl2, vs_beta[p][2 * c:], 0.0) for p in pairs]
    ark = [jnp.where(incl2, vs_kt[p][2 * c:], 0.0) for p in pairs]
    tinv = _unit_lower_inverse(jnp.concatenate([lo[p][h * c:(h + 1) * c][None] for p in pairs for h in (0, 1)]), c)
    tinv = [jnp.concatenate([tinv[2 * p], tinv[2 * p + 1]], axis=0) for p in pairs]
    both = lambda x: jnp.where(heads[0], x[:c], x[c:])
    vs_s = [_dot_nt(jnp.concatenate([kap[p], rt[p]], axis=0), s0[p], HIGHEST) for p in pairs]
    rhs = [vs_s[p][:c] + both(_dot(mm[p], v[p], HIGHEST)) for p in pairs]
    u = [-both(_dot(tinv[p], rhs[p], HIGHEST)) for p in pairs]
    y = [vs_s[p][c:] + both(_dot(arb[p], u[p], HIGHEST) + _dot(ark[p], v[p], HIGHEST)) for p in pairs]
    rr = lax.broadcasted_iota(jnp.int32, (LANES, LANES), 0) < HEAD
    cc = lax.broadcasted_iota(jnp.int32, (LANES, LANES), 1) < HEAD
    ds = [_dot_tn(jnp.concatenate([u[p], v[p]], axis=0),
                  jnp.concatenate([b[p] * to_end[p], k[p] * to_end[p]], axis=0), HIGHEST) for p in pairs]
    s1 = [s0[p] * jnp.exp(cum_end[p]) + jnp.where(rr == cc, ds[p], 0.0) for p in pairs]
    return tuple(y), tuple(s1)


def _scan_tiles(t, n_pair):
    return _tile(t, (32, 16, 8)), _tile(t, (512, 256, 128, 64, 32)), _tile(n_pair, (PAIRS_PER_STEP, 2, 1))


def _scan_fwd(r, w, k, v, kk, a):
    t, width = r.shape
    c, tb, npb = _scan_tiles(t, width // LANES)
    n_grp, n_blk, n_cb = width // (LANES * npb), t // tb, tb // c

    def body(r_ref, w_ref, k_ref, v_ref, kk_ref, a_ref, y_ref, st_ref, s_scr):
        @pl.when(pl.program_id(1) == 0)
        def _():
            s_scr[...] = jnp.zeros_like(s_scr)

        def chunk(j, carry):
            sl = pl.ds(pl.multiple_of(j * c, c), c)
            lanes = [pl.ds(q * LANES, LANES) for q in range(npb)]
            s0 = tuple(s_scr[q] for q in range(npb))
            cols = lambda ref: tuple(ref[sl, ln] for ln in lanes)
            y, s1 = _rwkv_chunk(s0, cols(r_ref), cols(w_ref), cols(k_ref), cols(v_ref), cols(kk_ref), cols(a_ref), c=c)
            for q, ln in enumerate(lanes):
                st_ref[q, j] = s0[q]
                y_ref[sl, ln] = y[q]
                s_scr[q] = s1[q]
            return carry

        lax.fori_loop(0, n_cb, chunk, 0)

    blk = pl.BlockSpec((tb, npb * LANES), lambda p, i: (i, p))
    return pl.pallas_call(
        body, name="rwkv_scan_fwd", grid=(n_grp, n_blk), in_specs=[blk] * 6,
        out_specs=[blk, pl.BlockSpec((npb, n_cb, LANES, LANES), lambda p, i: (p, i, 0, 0))],
        out_shape=[jax.ShapeDtypeStruct((t, width), F32),
                   jax.ShapeDtypeStruct((width // LANES, t // c, LANES, LANES), F32)],
        scratch_shapes=[pltpu.VMEM((npb, LANES, LANES), F32)],
        compiler_params=_params("arbitrary", "arbitrary"),
    )(r, w, k, v, kk, a)


def _scan_bwd(r, w, k, v, kk, a, st, dy):
    t, width = r.shape
    c, tb, npb = _scan_tiles(t, width // LANES)
    n_grp, n_blk, n_cb = width // (LANES * npb), t // tb, tb // c

    def body(r_ref, w_ref, k_ref, v_ref, kk_ref, a_ref, st_ref, dy_ref,
             dr_ref, dw_ref, dk_ref, dv_ref, dkk_ref, da_ref, ds_scr):
        @pl.when(pl.program_id(1) == 0)
        def _():
            ds_scr[...] = jnp.zeros_like(ds_scr)

        def chunk(jj, carry):
            j = n_cb - 1 - jj
            sl = pl.ds(pl.multiple_of(j * c, c), c)
            lanes = [pl.ds(q * LANES, LANES) for q in range(npb)]
            cols = lambda ref: tuple(ref[sl, ln] for ln in lanes)
            args = (tuple(st_ref[q, j] for q in range(npb)), cols(r_ref), cols(w_ref), cols(k_ref), cols(v_ref),
                    cols(kk_ref), cols(a_ref))
            _, vjp = jax.vjp(functools.partial(_rwkv_chunk, c=c), *args)
            grads = vjp((cols(dy_ref), tuple(ds_scr[q] for q in range(npb))))
            for q, ln in enumerate(lanes):
                ds_scr[q] = grads[0][q]
                for ref, g in zip((dr_ref, dw_ref, dk_ref, dv_ref, dkk_ref, da_ref), grads[1:]):
                    ref[sl, ln] = g[q]
            return carry

        lax.fori_loop(0, n_cb, chunk, 0)

    blk = pl.BlockSpec((tb, npb * LANES), lambda p, i: (n_blk - 1 - i, p))
    stb = pl.BlockSpec((npb, n_cb, LANES, LANES), lambda p, i: (p, n_blk - 1 - i, 0, 0))
    return pl.pallas_call(
        body, name="rwkv_scan_bwd", grid=(n_grp, n_blk), in_specs=[blk] * 6 + [stb, blk], out_specs=[blk] * 6,
        out_shape=[jax.ShapeDtypeStruct((t, width), F32)] * 6,
        scratch_shapes=[pltpu.VMEM((npb, LANES, LANES), F32)],
        compiler_params=_params("arbitrary", "arbitrary"),
    )(r, w, k, v, kk, a, st, dy)


NEG = -1e30


def _fox_cumsum(zf, b_pad, *, fw, fh):
    t = zf.shape[0]
    tile = _tile(t, (256, 128, 64, 32, 16, 8))

    def body(fl_ref, b_ref, c_ref, carry):
        @pl.when(pl.program_id(0) == 0)
        def _():
            carry[...] = jnp.zeros_like(carry)

        lane = lax.broadcasted_iota(jnp.int32, (tile, LANES), 1)
        logf = jnp.where(lane < fh, -_softplus(-(fl_ref[...] + b_ref[...])), 0.0)
        ri = lax.broadcasted_iota(jnp.int32, (tile, tile), 0)
        ci = lax.broadcasted_iota(jnp.int32, (tile, tile), 1)
        c_ref[...] = carry[...] + _dot((ci <= ri).astype(F32), logf, HIGHEST)
        carry[...] += jnp.sum(logf, axis=0, keepdims=True)

    return pl.pallas_call(
        body, name="fox_cumsum", grid=(t // tile,),
        in_specs=[_row_spec(tile, LANES, 4 * fw // LANES), _full_spec((1, LANES))],
        out_specs=_row_spec(tile, LANES), out_shape=jax.ShapeDtypeStruct((t, LANES), F32),
        scratch_shapes=[pltpu.VMEM((1, LANES), F32)], compiler_params=_params("arbitrary"),
    )(zf, b_pad)


def _fox_cumsum_bwd(zf, b_pad, dc, *, fw, fh):
    t = zf.shape[0]
    tile = _tile(t, (256, 128, 64, 32, 16, 8))
    n = t // tile

    def body(fl_ref, b_ref, dc_ref, dfl_ref, db_ref, carry):
        i = pl.program_id(0)

        @pl.when(i == 0)
        def _():
            carry[...] = jnp.zeros_like(carry)
            db_ref[...] = jnp.zeros_like(db_ref)

        lane = lax.broadcasted_iota(jnp.int32, (tile, LANES), 1)
        dc_t = dc_ref[...]
        ri = lax.broadcasted_iota(jnp.int32, (tile, tile), 0)
        ci = lax.broadcasted_iota(jnp.int32, (tile, tile), 1)
        dlogf = carry[...] + _dot((ci >= ri).astype(F32), dc_t, HIGHEST)
        carry[...] += jnp.sum(dc_t, axis=0, keepdims=True)
        dfl = jnp.where(lane < fh, dlogf * jax.nn.sigmoid(-(fl_ref[...] + b_ref[...])), 0.0)
        dfl_ref[...] = dfl.astype(dfl_ref.dtype)
        db_ref[...] += jnp.sum(dfl, axis=0, keepdims=True)

    rev = lambda col: pl.BlockSpec((tile, LANES), lambda i: (n - 1 - i, col))
    return pl.pallas_call(
        body, name="fox_cumsum_bwd", grid=(n,),
        in_specs=[rev(4 * fw // LANES), _full_spec((1, LANES)), rev(0)],
        out_specs=[rev(0), _full_spec((1, LANES))],
        out_shape=[jax.ShapeDtypeStruct((t, LANES), BF16), jax.ShapeDtypeStruct((1, LANES), F32)],
        scratch_shapes=[pltpu.VMEM((1, LANES), F32)], compiler_params=_params("arbitrary"),
    )(zf, b_pad, dc)


def _fox_tile(t):
    return _tile(t, (512, 256, 128))


def _fox_fwd(zf, c, ct, *, fw):
    t = zf.shape[0]
    tq = _fox_tile(t)
    th = tq // 2
    n_pair, n_q = fw // LANES, t // tq
    scale = HEAD ** -0.5
    chains = [(h, qq) for h in (0, 1) for qq in (0, 1)]

    def body(q_ref, k_ref, v_ref, g_ref, c_ref, ct_ref, o_ref, lse_ref, y_ref):
        hp, i = pl.program_id(0), pl.program_id(1)
        lane = lax.broadcasted_iota(jnp.int32, (1, LANES), 1)
        in_head = (lane < HEAD, lane >= HEAD)
        rows = [pl.ds(qq * th, th) for qq in (0, 1)]
        qh = [jnp.where(in_head[h], q_ref[rows[qq], :] * scale, 0.0).astype(BF16) for h, qq in chains]
        cq = [_lane_pick(c_ref[rows[qq], :], 2 * hp + h) for h, qq in chains]
        qidx = lax.broadcasted_iota(jnp.int32, (th, tq), 0)
        kidx = lax.broadcasted_iota(jnp.int32, (th, tq), 1)

        def kv_step(j, carry, diagonal):
            m, l, acc = carry
            ks = pl.ds(pl.multiple_of(j * tq, tq), tq)
            kb = k_ref[ks, :].astype(BF16)
            vb = v_ref[ks, :]
            vh = [jnp.where(in_head[h], vb, 0.0).astype(BF16) for h in (0, 1)]
            ck = [ct_ref[0, j, pl.ds(h, 1), :] for h in (0, 1)]
            s = [_dot_nt(qh[n], kb) + cq[n] - ck[h] for n, (h, qq) in enumerate(chains)]
            if diagonal:
                s = [jnp.where(qq * th + qidx >= kidx, s[n], NEG) for n, (h, qq) in enumerate(chains)]
            m_new = [jnp.maximum(m[n], jnp.max(s[n], axis=1, keepdims=True)) for n in range(4)]
            p = [jnp.exp(s[n] - m_new[n]) for n in range(4)]
            alpha = [jnp.exp(m[n] - m_new[n]) for n in range(4)]
            l = [l[n] * alpha[n] + jnp.sum(p[n], axis=1, keepdims=True) for n in range(4)]
            pv = [_dot(p[n].astype(BF16), vh[h]) for n, (h, qq) in enumerate(chains)]
            acc = [acc[qq] * jnp.where(in_head[0], alpha[qq], alpha[2 + qq]) + pv[qq] + pv[2 + qq] for qq in (0, 1)]
            return tuple(m_new), tuple(l), tuple(acc)

        init = (tuple(jnp.full((th, 1), NEG, F32) for _ in chains), tuple(jnp.zeros((th, 1), F32) for _ in chains),
                tuple(jnp.zeros((th, LANES), F32) for _ in (0, 1)))
        carry = lax.fori_loop(0, i, functools.partial(kv_step, diagonal=False), init)
        m, l, acc = kv_step(i, carry, True)
        for qq in (0, 1):
            o = acc[qq] / jnp.where(in_head[0], l[qq], l[2 + qq])
            g = g_ref[rows[qq], :]
            o_ref[rows[qq], :] = o
            lse_ref[rows[qq], :] = jnp.where(in_head[0], m[qq] + jnp.log(l[qq]), m[2 + qq] + jnp.log(l[2 + qq]))
            y_ref[rows[qq], :] = (o * (g * jax.nn.sigmoid(g))).astype(y_ref.dtype)

    npw = fw // LANES
    blk = lambda col0: pl.BlockSpec((tq, LANES), lambda hp, i: (i, col0 + hp))
    res = lambda col0: pl.BlockSpec((t, LANES), lambda hp, i: (0, col0 + hp))
    out_blk = pl.BlockSpec((tq, LANES), lambda hp, i: (i, hp))
    return pl.pallas_call(
        body, name="fox_attn_fwd", grid=(n_pair, n_q),
        in_specs=[blk(0), res(npw), res(2 * npw), blk(3 * npw),
                  pl.BlockSpec((tq, LANES), lambda hp, i: (i, 0)),
                  pl.BlockSpec((1, n_q, 2, tq), lambda hp, i: (hp, 0, 0, 0))],
        out_specs=[out_blk, out_blk, out_blk],
        out_shape=[jax.ShapeDtypeStruct((t, fw), F32), jax.ShapeDtypeStruct((t, fw), F32),
                   jax.ShapeDtypeStruct((t, fw), BF16)],
        compiler_params=_params("arbitrary", "arbitrary"),
    )(zf, zf, zf, zf, c, ct)


def _fox_bwd(zf, do, c, ct, lse_r, dd_r, *, fw):
    t = zf.shape[0]
    tq = _fox_tile(t)
    n_pair, n_q = fw // LANES, t // tq
    scale = HEAD ** -0.5

    def body(q_ref, k_ref, v_ref, do_ref, c_ref, ct_ref, lse_ref, dd_ref,
             dq_ref, dk_ref, dv_ref, dcq_ref, dck_ref, dq_acc, dcq_acc):
        hp, j = pl.program_id(0), pl.program_id(1)

        @pl.when(j == 0)
        def _():
            dq_acc[...] = jnp.zeros_like(dq_acc)
            dcq_acc[...] = jnp.zeros_like(dcq_acc)

        lane = lax.broadcasted_iota(jnp.int32, (1, LANES), 1)
        in_head = (lane < HEAD, lane >= HEAD)
        kb = k_ref[...]
        kh = [jnp.where(m, kb, 0.0).astype(BF16) for m in in_head]
        vb = v_ref[...].astype(BF16)
        c_k = c_ref[...]
        ck = [_lane_pick(c_k, 2 * hp + h) for h in (0, 1)]
        kidx = lax.broadcasted_iota(jnp.int32, (tq, tq), 0)
        qidx = lax.broadcasted_iota(jnp.int32, (tq, tq), 1)

        def q_step(i, carry, diagonal):
            dk, dv, dck = carry
            qs = pl.ds(pl.multiple_of(i * tq, tq), tq)
            qf = q_ref[qs, :] * scale
            dof = do_ref[qs, :]
            qh = [jnp.where(m, qf, 0.0).astype(BF16) for m in in_head]
            doh = [jnp.where(m, dof, 0.0).astype(BF16) for m in in_head]
            row = lambda ref, h: ref[0, i, pl.ds(h, 1), :]
            st = [_dot_nt(kh[h], qh[h]) + row(ct_ref, h) - ck[h] for h in (0, 1)]
            p = [jnp.exp(st[h] - row(lse_ref, h)) for h in (0, 1)]
            if diagonal:
                p = [jnp.where(kidx <= qidx, p[h], 0.0) for h in (0, 1)]
            dst = [p[h] * (_dot_nt(vb, doh[h]) - row(dd_ref, h)) for h in (0, 1)]
            p16 = [x.astype(BF16) for x in p]
            ds16 = [x.astype(BF16) for x in dst]
            dv = dv + _dot(p16[0], doh[0]) + _dot(p16[1], doh[1])
            dk = dk + _dot(ds16[0], qh[0]) + _dot(ds16[1], qh[1])
            dq_acc[qs, :] += _dot_tn(ds16[0], kh[0]) + _dot_tn(ds16[1], kh[1])
            for h in (0, 1):
                dcq_acc[i, pl.ds(h, 1), :] += jnp.sum(dst[h], axis=0, keepdims=True)
            dck = tuple(dck[h] - jnp.sum(dst[h], axis=1, keepdims=True) for h in (0, 1))
            return dk, dv, dck

        zero = jnp.zeros((tq, LANES), F32)
        carry = q_step(j, (zero, zero, (jnp.zeros((tq, 1), F32),) * 2), True)
        dk, dv, dck = lax.fori_loop(j + 1, n_q, functools.partial(q_step, diagonal=False), carry)
        dk_ref[...] = dk.astype(dk_ref.dtype)
        dv_ref[...] = dv.astype(dv_ref.dtype)
        dck_ref[...] = jnp.where(lane == 0, dck[0], jnp.where(lane == 1, dck[1], 0.0))

        @pl.when(j == n_q - 1)
        def _():
            dq_ref[...] = (dq_acc[...] * scale).astype(dq_ref.dtype)
            dcq_ref[0] = dcq_acc[...]

    npw = fw // LANES
    res_z = lambda col0: pl.BlockSpec((t, LANES), lambda hp, j: (0, col0 + hp))
    blk_z = lambda col0: pl.BlockSpec((tq, LANES), lambda hp, j: (j, col0 + hp))
    res = pl.BlockSpec((t, LANES), lambda hp, j: (0, hp))
    blk = pl.BlockSpec((tq, LANES), lambda hp, j: (j, hp))
    rows = pl.BlockSpec((1, n_q, 2, tq), lambda hp, j: (hp, 0, 0, 0))
    return pl.pallas_call(
        body, name="fox_attn_bwd", grid=(n_pair, n_q),
        in_specs=[res_z(0), blk_z(npw), blk_z(2 * npw), res, pl.BlockSpec((tq, LANES), lambda hp, j: (j, 0)),
                  rows, rows, rows],
        out_specs=[res, blk, blk, rows, blk],
        out_shape=[jax.ShapeDtypeStruct((t, fw), BF16), jax.ShapeDtypeStruct((t, fw), BF16),
                   jax.ShapeDtypeStruct((t, fw), BF16), jax.ShapeDtypeStruct((n_pair, n_q, 2, tq), F32),
                   jax.ShapeDtypeStruct((t, fw), F32)],
        scratch_shapes=[pltpu.VMEM((t, LANES), F32), pltpu.VMEM((n_q, 2, tq), F32)],
        compiler_params=_params("arbitrary", "arbitrary"),
    )(zf, zf, zf, do, c, ct, lse_r, dd_r)


def _adamw_math(w, g, m, v):
    m = ADAM_B1 * m + (1.0 - ADAM_B1) * g
    v = ADAM_B2 * v + (1.0 - ADAM_B2) * jnp.square(g)
    m_hat = m / (1.0 - ADAM_B1 ** ADAM_STEP)
    v_hat = v / (1.0 - ADAM_B2 ** ADAM_STEP)
    delta = -ADAM_LR * (m_hat / (jnp.sqrt(v_hat) + ADAM_EPS) + ADAM_WD * w)
    return delta, m, v


def _adamw(name, w, g, m, v):
    rows, cols = w.shape
    tile = _tile(rows, (128, 64, 32, 16, 8))
    spec = _row_spec(tile, cols)
    return _rows(name, lambda i, *a: _adamw_math(*a), rows, tile, [(w, spec), (g, spec), (m, spec), (v, spec)],
                 [(cols, F32)] * 3)


def _place():
    return lax.axis_index("x"), lax.axis_index("y"), lax.axis_index("c")


def _other_chips(x, y):
    return [(1 - x, y), (x, 1 - y), (1 - x, 1 - y)]


HBM_SPEC = pl.BlockSpec(memory_space=pltpu.HBM)


def _all_gather_shards(slab):
    rows, width = slab.shape
    rh = rows // 2

    def body(src_ref, out_ref, send_sems, recv_sems, local_sem):
        x, y, c = _place()
        me = 2 * x + y
        sibling = (x, y, 1 - c)
        chips = _other_chips(x, y)

        def part(chip, half):
            return out_ref.at[chip, pl.ds(half * rh, rh), :]

        def copy(k, src, dst, to):
            return pltpu.make_async_remote_copy(src_ref=src, dst_ref=dst, send_sem=send_sems.at[k],
                                                recv_sem=recv_sems.at[k], device_id=to, device_id_type=MESH)

        mine = pltpu.make_async_copy(src_ref, out_ref.at[me], local_sem)
        mine.start()
        first = [copy(j, src_ref.at[pl.ds(c * rh, rh), :], part(me, c), (px, py, c)) for j, (px, py) in enumerate(chips)]
        for cp in first:
            cp.start()
        passed = [copy(3 + j, part(2 * px + py, c), part(2 * px + py, c), sibling) for j, (px, py) in enumerate(chips)]
        for j, (px, py) in enumerate(chips):
            copy(j, part(2 * px + py, c), part(2 * px + py, c), sibling).wait_recv()
            passed[j].start()
        for j, (px, py) in enumerate(chips):
            copy(3 + j, part(2 * px + py, 1 - c), part(2 * px + py, 1 - c), sibling).wait_recv()
        for cp in first + passed:
            cp.wait_send()
        mine.wait()

    return pl.pallas_call(
        body, name="weights_all_gather", in_specs=[HBM_SPEC], out_specs=HBM_SPEC,
        out_shape=jax.ShapeDtypeStruct((N_SHARD, rows, width), slab.dtype),
        scratch_shapes=[pltpu.SemaphoreType.DMA((6,)), pltpu.SemaphoreType.DMA((6,)), pltpu.SemaphoreType.DMA],
    )(slab)


def _sibling_exchange(g):
    _, rows, width = g.shape
    rh = rows // 2

    def body(g_ref, out_ref, send_sems, recv_sems):
        x, y, c = _place()
        copies = [pltpu.make_async_remote_copy(
            src_ref=g_ref.at[s, pl.ds((1 - c) * rh, rh), :], dst_ref=out_ref.at[s], send_sem=send_sems.at[s],
            recv_sem=recv_sems.at[s], device_id=(x, y, 1 - c), device_id_type=MESH) for s in range(N_SHARD)]
        for cp in copies:
            cp.start()
        for cp in copies:
            cp.wait()

    return pl.pallas_call(
        body, name="grad_sibling_exchange", in_specs=[HBM_SPEC], out_specs=HBM_SPEC,
        out_shape=jax.ShapeDtypeStruct((N_SHARD, rh, width), g.dtype),
        scratch_shapes=[pltpu.SemaphoreType.DMA((N_SHARD,)), pltpu.SemaphoreType.DMA((N_SHARD,))],
    )(g)


def _add_sibling(g, got):
    _, rows, width = g.shape
    rh = rows // 2
    tile = _tile(rh, (256, 128, 64, 32, 16, 8))
    c_arr = jnp.reshape(lax.axis_index("c"), (1,)).astype(jnp.int32)

    def body(c_ref, a_ref, b_ref, o_ref):
        o_ref[...] = a_ref[0] + b_ref[...]

    return pl.pallas_call(
        body, name="grad_add_sibling",
        grid_spec=pltpu.PrefetchScalarGridSpec(
            num_scalar_prefetch=1, grid=(N_SHARD, rh // tile),
            in_specs=[pl.BlockSpec((1, 1, tile, width), lambda s, i, c: (s, c[0], i, 0)),
                      pl.BlockSpec((1, tile, width), lambda s, i, c: (s, i, 0))],
            out_specs=pl.BlockSpec((1, tile, width), lambda s, i, c: (s, i, 0))),
        out_shape=jax.ShapeDtypeStruct((N_SHARD, rh, width), F32),
        compiler_params=_params("parallel", "parallel"),
    )(c_arr, g.reshape(N_SHARD, 2, rh, width), got)


def _chip_exchange(p):
    def body(p_ref, out_ref, send_sems, recv_sems, local_sem):
        x, y, c = _place()
        me = 2 * x + y
        mine = pltpu.make_async_copy(p_ref.at[me], out_ref.at[me], local_sem)
        mine.start()
        copies = [pltpu.make_async_remote_copy(
            src_ref=p_ref.at[2 * px + py], dst_ref=out_ref.at[me], send_sem=send_sems.at[j],
            recv_sem=recv_sems.at[j], device_id=(px, py, c), device_id_type=MESH)
            for j, (px, py) in enumerate(_other_chips(x, y))]
        for cp in copies:
            cp.start()
        for cp in copies:
            cp.wait()
        mine.wait()

    return pl.pallas_call(
        body, name="grad_chip_exchange", in_specs=[HBM_SPEC], out_specs=HBM_SPEC,
        out_shape=jax.ShapeDtypeStruct(p.shape, p.dtype),
        scratch_shapes=[pltpu.SemaphoreType.DMA((3,)), pltpu.SemaphoreType.DMA((3,)), pltpu.SemaphoreType.DMA],
    )(p)


def _sum_chips(q):
    _, rh, width = q.shape
    tile = _tile(rh, (256, 128, 64, 32, 16, 8))

    def body(q_ref, o_ref):
        o_ref[...] = ((q_ref[0] + q_ref[1]) + q_ref[2]) + q_ref[3]

    return pl.pallas_call(
        body, name="grad_sum_chips", grid=(rh // tile,),
        in_specs=[pl.BlockSpec((N_SHARD, tile, width), lambda i: (0, i, 0))],
        out_specs=pl.BlockSpec((tile, width), lambda i: (i, 0)),
        out_shape=jax.ShapeDtypeStruct((rh, width), F32), compiler_params=_params("parallel"),
    )(q)


def _join_halves(half):
    rh, width = half.shape

    def body(h_ref, out_ref, send_sem, recv_sem, local_sem):
        x, y, c = _place()
        mine = pltpu.make_async_copy(h_ref, out_ref.at[pl.ds(c * rh, rh), :], local_sem)
        mine.start()
        cp = pltpu.make_async_remote_copy(
            src_ref=h_ref, dst_ref=out_ref.at[pl.ds(c * rh, rh), :], send_sem=send_sem, recv_sem=recv_sem,
            device_id=(x, y, 1 - c), device_id_type=MESH)
        cp.start()
        cp.wait_send()
        pltpu.make_async_remote_copy(
            src_ref=h_ref, dst_ref=out_ref.at[pl.ds((1 - c) * rh, rh), :], send_sem=send_sem, recv_sem=recv_sem,
            device_id=(x, y, 1 - c), device_id_type=MESH).wait_recv()
        mine.wait()

    return pl.pallas_call(
        body, name="grad_join_halves", in_specs=[HBM_SPEC], out_specs=HBM_SPEC,
        out_shape=jax.ShapeDtypeStruct((2 * rh, width), half.dtype),
        scratch_shapes=[pltpu.SemaphoreType.DMA, pltpu.SemaphoreType.DMA, pltpu.SemaphoreType.DMA],
    )(half)


def _reduce_scatter(g):
    chip_sum = _add_sibling(g, _sibling_exchange(g))
    return _join_halves(_sum_chips(_chip_exchange(chip_sum)))


def _all_reduce_small(v):
    rows = v.shape[0]

    def body(v_ref, out_ref, gather, send_sems, recv_sems):
        x, y, c = _place()
        gather[4 * x + 2 * y + c] = v_ref[...]
        flips = [(dx, dy, dc) for dx in (0, 1) for dy in (0, 1) for dc in (0, 1)][1:]
        peers = [((x + dx) % 2, (y + dy) % 2, (c + dc) % 2) for dx, dy, dc in flips]
        copies = [pltpu.make_async_remote_copy(
            src_ref=v_ref, dst_ref=gather.at[4 * x + 2 * y + c], send_sem=send_sems.at[j], recv_sem=recv_sems.at[j],
            device_id=peer, device_id_type=MESH) for j, peer in enumerate(peers)]
        for cp in copies:
            cp.start()
        for j, (px, py, pc) in enumerate(peers):
            pltpu.make_async_remote_copy(
                src_ref=v_ref, dst_ref=gather.at[4 * px + 2 * py + pc], send_sem=send_sems.at[j],
                recv_sem=recv_sems.at[j], device_id=(px, py, pc), device_id_type=MESH).wait_recv()
        for cp in copies:
            cp.wait_send()
        acc = gather[0]
        for d in range(1, 8):
            acc = acc + gather[d]
        out_ref[...] = acc

    vm = pl.BlockSpec(memory_space=pltpu.VMEM)
    return pl.pallas_call(
        body, name="small_grads_all_reduce", in_specs=[vm], out_specs=vm,
        out_shape=jax.ShapeDtypeStruct(v.shape, F32),
        scratch_shapes=[pltpu.VMEM((8, rows, LANES), F32), pltpu.SemaphoreType.DMA((7,)), pltpu.SemaphoreType.DMA((7,))],
    )(v)


def _pack_slab(parts, rows):
    flat = jnp.concatenate([p.reshape(-1) for p in parts])
    return jnp.pad(flat, (0, rows * SLAB_W - flat.shape[0])).reshape(rows, SLAB_W)


def _unpack_slab(slab, shapes):
    flat = slab.reshape(-1)
    out, off = [], 0
    for s in shapes:
        n = 1
        for d in s:
            n *= d
        out.append(flat[off:off + n].reshape(s))
        off += n
    return out


def _pad_lanes(v):
    v = v.reshape(1, -1)
    return jnp.pad(v, ((0, 0), (0, -v.shape[1] % LANES)))


def _pack_small(vs, rows):
    flat = jnp.concatenate([_pad_lanes(v) for v in vs], axis=1)
    return jnp.pad(flat, ((0, 0), (0, rows * LANES - flat.shape[1]))).reshape(rows, LANES)


def _unpack_small(packed, shapes):
    flat = packed.reshape(-1)
    out, off = [], 0
    for s in shapes:
        n = 1
        for d in s:
            n *= d
        out.append(flat[off:off + n].reshape(s))
        off += n + (-n % LANES)
    return out


BIG = ("w_in", "rw_w_lora_up", "rw_a_lora_up", "w_up_rwkv", "w_up_fox", "w_out", "ple_proj", "ple_gate_w")
ROW_SHARDED = ("w_out", "ple_gate_w")
SMALL = ("norm_g", "rw_shift_mu", "rw_w0", "rw_a0", "rw_k_k", "rw_k_a", "rw_r_k", "rw_ln_g", "rw_ln_b", "fox_b_f",
         "ple_norm_g", "final_norm_g")
WEIGHTS = ("norm_g", "w_in", "rw_shift_mu", "rw_w0", "rw_w_lora_up", "rw_a0", "rw_a_lora_up", "rw_k_k", "rw_k_a",
           "rw_r_k", "rw_ln_g", "rw_ln_b", "fox_b_f", "w_up_rwkv", "w_up_fox", "w_out", "ple_proj", "ple_gate_w",
           "ple_norm_g", "final_norm_g")


def _full_from_shards(name, gathered):
    axis = 0 if name in ROW_SHARDED else 1
    return jnp.concatenate([gathered[s] for s in range(N_SHARD)], axis=axis)


def _shards_of_full(name, full):
    axis = 0 if name in ROW_SHARDED else 1
    return jnp.split(full, N_SHARD, axis=axis)


def kernel(x, p, norm_g, w_in, rw_shift_mu, rw_w0, rw_w_lora_up, rw_a0, rw_a_lora_up, rw_k_k, rw_k_a, rw_r_k, rw_ln_g, rw_ln_b, fox_b_f, w_up_rwkv, w_up_fox, w_out, ple_proj, ple_gate_w, ple_norm_g, final_norm_g, loss_target, m_norm_g, m_w_in, m_rw_shift_mu, m_rw_w0, m_rw_w_lora_up, m_rw_a0, m_rw_a_lora_up, m_rw_k_k, m_rw_k_a, m_rw_r_k, m_rw_ln_g, m_rw_ln_b, m_fox_b_f, m_w_up_rwkv, m_w_up_fox, m_w_out, m_ple_proj, m_ple_gate_w, m_ple_norm_g, m_final_norm_g, v_norm_g, v_w_in, v_rw_shift_mu, v_rw_w0, v_rw_w_lora_up, v_rw_a0, v_rw_a_lora_up, v_rw_k_k, v_rw_k_a, v_rw_r_k, v_rw_ln_g, v_rw_ln_b, v_fox_b_f, v_w_up_rwkv, v_w_up_fox, v_w_out, v_ple_proj, v_ple_gate_w, v_ple_norm_g, v_final_norm_g):
    wts = dict(norm_g=norm_g, w_in=w_in, rw_shift_mu=rw_shift_mu, rw_w0=rw_w0, rw_w_lora_up=rw_w_lora_up, rw_a0=rw_a0,
               rw_a_lora_up=rw_a_lora_up, rw_k_k=rw_k_k, rw_k_a=rw_k_a, rw_r_k=rw_r_k, rw_ln_g=rw_ln_g, rw_ln_b=rw_ln_b,
               fox_b_f=fox_b_f, w_up_rwkv=w_up_rwkv, w_up_fox=w_up_fox, w_out=w_out, ple_proj=ple_proj,
               ple_gate_w=ple_gate_w, ple_norm_g=ple_norm_g, final_norm_g=final_norm_g)
    mom = dict(norm_g=m_norm_g, w_in=m_w_in, rw_shift_mu=m_rw_shift_mu, rw_w0=m_rw_w0, rw_w_lora_up=m_rw_w_lora_up,
               rw_a0=m_rw_a0, rw_a_lora_up=m_rw_a_lora_up, rw_k_k=m_rw_k_k, rw_k_a=m_rw_k_a, rw_r_k=m_rw_r_k,
               rw_ln_g=m_rw_ln_g, rw_ln_b=m_rw_ln_b, fox_b_f=m_fox_b_f, w_up_rwkv=m_w_up_rwkv, w_up_fox=m_w_up_fox,
               w_out=m_w_out, ple_proj=m_ple_proj, ple_gate_w=m_ple_gate_w, ple_norm_g=m_ple_norm_g,
               final_norm_g=m_final_norm_g)
    vel = dict(norm_g=v_norm_g, w_in=v_w_in, rw_shift_mu=v_rw_shift_mu, rw_w0=v_rw_w0, rw_w_lora_up=v_rw_w_lora_up,
               rw_a0=v_rw_a0, rw_a_lora_up=v_rw_a_lora_up, rw_k_k=v_rw_k_k, rw_k_a=v_rw_k_a, rw_r_k=v_rw_r_k,
               rw_ln_g=v_rw_ln_g, rw_ln_b=v_rw_ln_b, fox_b_f=v_fox_b_f, w_up_rwkv=v_w_up_rwkv, w_up_fox=v_w_up_fox,
               w_out=v_w_out, ple_proj=v_ple_proj, ple_gate_w=v_ple_gate_w, ple_norm_g=v_ple_norm_g,
               final_norm_g=v_final_norm_g)

    t, d = x.shape[1], x.shape[2]
    cw = rw_w0.shape[1]
    lr = rw_w_lora_up.shape[1]
    fh = fox_b_f.shape[1]
    fw = fh * HEAD
    rw_cols = 4 * cw + 2 * lr
    fox_cols = 4 * fw + fh
    assert 2 * lr == LANES and cw % LANES == 0 and fw % LANES == 0 and fh <= LANES
    xs = x[0]
    ps = p[0, 0]
    tgt = loss_target[0]

    shard_shapes = [wts[n].shape[1:] for n in BIG]
    n_elem = sum(s[0] * s[1] for s in shard_shapes)
    slab_rows = -(-n_elem // (SLAB_W * 32)) * 32
    w_slab = _pack_slab([wts[n][0] for n in BIG], slab_rows)
    cast_tile = _tile(slab_rows, (256, 128, 64, 32, 16))
    (w_slab16,) = _rows("weights_to_bf16", lambda i, a: (a,), slab_rows, cast_tile,
                        [(w_slab, _row_spec(cast_tile, SLAB_W))], [(SLAB_W, BF16)])
    gathered = _all_gather_shards(w_slab16)
    per_shard = [_unpack_slab(gathered[s], shard_shapes) for s in range(N_SHARD)]
    full = {n: _full_from_shards(n, [per_shard[s][j] for s in range(N_SHARD)]) for j, n in enumerate(BIG)}
    w_rw = full["w_in"][:, :rw_cols]
    w_fox = jnp.pad(full["w_in"][:, rw_cols:rw_cols + fox_cols], ((0, 0), (0, LANES - fh)))
    w_gate = full["w_in"][:, rw_cols + fox_cols:]
    wup_pad = jnp.pad(full["rw_w_lora_up"], ((0, lr), (0, 0)))
    aup_pad = jnp.pad(full["rw_a_lora_up"], ((lr, 0), (0, 0)))
    b_pad = _pad_lanes(fox_b_f)
    r_k_row = rw_r_k.reshape(1, cw)
    gf_row = final_norm_g.reshape(1, d)

    tile = _tile(t, (256, 128, 64, 32, 16, 8))
    tile_s = _tile(t, (128, 64, 32, 16, 8))
    n_s = t // tile_s
    full2 = lambda a: (a, _full_spec(a.shape))

    (h,) = _rows("norm1", lambda i, a, g: (_rms(a, g),), t, tile, [(xs, _row_spec(tile, d)), full2(norm_g)], [(d, BF16)])
    z_rw = _matmul("proj_rw", h, w_rw)
    z_fox = _matmul("proj_fox", h, w_fox)
    z_gate = _matmul("proj_gate", h, w_gate)

    pre_consts = [full2(rw_shift_mu), full2(rw_w0), full2(rw_a0), full2(wup_pad), full2(aup_pad), full2(rw_k_k),
                  full2(rw_k_a)]

    def pre_fwd(i, z, prev8, *consts):
        return _rw_pre(z, _shifted(i, z, prev8), *consts, cw=cw)

    r_, w_, k_, v_, kk_, a_, g_ = _rows(
        "rwkv_pre", pre_fwd, t, tile_s,
        [(z_rw, _row_spec(tile_s, rw_cols)), (z_rw, _prev_rows_spec(tile_s, rw_cols))] + pre_consts, [(cw, F32)] * 7)
    y_scan, states = _scan_fwd(r_, w_, k_, v_, kk_, a_)
    post_consts = [full2(rw_ln_g), full2(rw_ln_b), full2(r_k_row)]
    post_rows = lambda *arrs: [(a, _row_spec(tile_s, cw)) for a in arrs]
    (y_rw,) = _rows("rwkv_post", lambda i, *a: (_rw_post(*a, cw=cw),), t, tile_s,
                    post_rows(y_scan, r_, k_, v_, g_) + post_consts, [(cw, BF16)])

    c_fox = _fox_cumsum(z_fox, b_pad, fw=fw, fh=fh)
    tq = _fox_tile(t)
    n_pair_f = fw // LANES
    head_rows = lambda a: a.T.reshape(n_pair_f, 2, t // tq, tq).transpose(0, 2, 1, 3)
    head_cols = lambda a: a.transpose(0, 2, 1, 3).reshape(fh, t).T
    ct_fox = head_rows(c_fox[:, :fh])
    o_fox, lse_fox, y_fox = _fox_fwd(z_fox, c_fox, ct_fox, fw=fw)

    u_rw = _matmul("up_rwkv", y_rw, full["w_up_rwkv"])
    u_fox = _matmul("up_fox", y_fox, full["w_up_fox"])
    (merged,) = _rows("merge", lambda i, zg, a, b: (_merge(zg, a, b, d=d),), t, tile,
                      [(z_gate, _row_spec(tile, 2 * d)), (u_rw, _row_spec(tile, d)), (u_fox, _row_spec(tile, d))],
                      [(d, BF16)])
    x1 = _matmul("out_proj", merged, full["w_out"], add=xs)
    (n2,) = _rows("norm2", lambda i, a, g: (_rms(a, g),), t, tile, [(x1, _row_spec(tile, d)), full2(ple_norm_g)],
                  [(d, BF16)])
    gl = _matmul("ple_gate", n2, full["ple_gate_w"])
    ple = _matmul("ple_proj", ps, full["ple_proj"])

    def head_bwd(i, x1_t, ple_t, gl_t, gf, tg):
        loss, vjp = jax.vjp(lambda a, b, cc, g: _head_loss(a, b, cc, g, tg), x1_t, ple_t, gl_t, gf)
        dx1, dple, dgl, dgf = vjp(jnp.ones((1, 1), F32))
        return dx1, dple, dgl, jnp.broadcast_to(loss, (1, LANES)), dgf

    dx2, dple, dgl, loss_row, d_gf = _rows(
        "loss_head", head_bwd, t, tile_s,
        [(x1, _row_spec(tile_s, d)), (ple, _row_spec(tile_s, d)), (gl, _row_spec(tile_s, d)), full2(gf_row),
         (tgt, _row_spec(tile_s, d))],
        [(d, F32), (d, BF16), (d, BF16)], [(1, LANES), (1, d)])

    g_ple_proj = _matmul("d_ple_proj", ps, dple, ta=True)
    g_ple_gate = _matmul("d_ple_gate_w", n2, dgl, ta=True)
    dn2 = _matmul("d_n2", dgl, full["ple_gate_w"], tb=True)

    def norm_bwd(i, a, g, dh, res):
        _, vjp = jax.vjp(_rms, a, g)
        da, dg = vjp(dh)
        return res + da, dg

    dx1, d_g2 = _rows("norm2_bwd", norm_bwd, t, tile_s,
                      [(x1, _row_spec(tile_s, d)), full2(ple_norm_g), (dn2, _row_spec(tile_s, d)),
                       (dx2, _row_spec(tile_s, d))], [(d, F32)], [(1, d)])
    g_w_out = _matmul("d_w_out", merged, dx1, ta=True)
    dmerged = _matmul("d_merged", dx1, full["w_out"], tb=True)

    def merge_bwd(i, zg, a, b, dm):
        _, vjp = jax.vjp(functools.partial(_merge, d=d), zg, a, b)
        return vjp(dm)

    dz_gate, du_rw, du_fox = _rows(
        "merge_bwd", merge_bwd, t, tile_s,
        [(z_gate, _row_spec(tile_s, 2 * d)), (u_rw, _row_spec(tile_s, d)), (u_fox, _row_spec(tile_s, d)),
         (dmerged, _row_spec(tile_s, d))], [(2 * d, BF16), (d, BF16), (d, BF16)])
    g_up_rw = _matmul("d_w_up_rwkv", y_rw, du_rw, ta=True)
    g_up_fox = _matmul("d_w_up_fox", y_fox, du_fox, ta=True)
    dy_rw = _matmul("d_y_rwkv", du_rw, full["w_up_rwkv"], tb=True)
    dy_fox = _matmul("d_y_fox", du_fox, full["w_up_fox"], tb=True)

    def post_bwd(i, y, r, k, v, g, ln_g, ln_b, r_k, dy):
        _, vjp = jax.vjp(functools.partial(_rw_post, cw=cw), y, r, k, v, g, ln_g, ln_b, r_k)
        return vjp(dy)

    dys, dr1, dk1, dv1, dg1, d_ln_g, d_ln_b, d_r_k = _rows(
        "rwkv_post_bwd", post_bwd, t, tile_s,
        post_rows(y_scan, r_, k_, v_, g_) + post_consts + post_rows(dy_rw), [(cw, F32)] * 5, [(1, cw)] * 3)
    dr2, dw2, dk2, dv2, dkk2, da2 = _scan_bwd(r_, w_, k_, v_, kk_, a_, states, dys)

    def pre_bwd(i, z, prev8, mu, w0, a0, wup, aup, k_k, k_a, dr_a, dr_b, dk_a, dk_b, dv_a, dv_b, dw, dkk, da, dg):
        zp = _shifted(i, z, prev8)
        _, vjp = jax.vjp(functools.partial(_rw_pre, cw=cw), z, zp, mu, w0, a0, wup, aup, k_k, k_a)
        dz, dzp, dmu, dw0, da0, dwup, daup, dk_k, dk_a = vjp((dr_a + dr_b, dw, dk_a + dk_b, dv_a + dv_b, dkk, da, dg))
        row = lax.broadcasted_iota(jnp.int32, dz.shape, 0)
        dz = dz + jnp.where(row < tile_s - 1, pltpu.roll(dzp, tile_s - 1, 0), 0.0)
        first = jnp.where(lax.broadcasted_iota(jnp.int32, (8, dz.shape[1]), 0) == 0, _row_of(dzp, 0), 0.0)
        return dz, first, dmu, dw0, da0, dwup, daup, dk_k, dk_a

    def pre_bwd_call():
        n_in = 2 + len(pre_consts) + 10
        ins = ([(z_rw, _row_spec(tile_s, rw_cols)), (z_rw, _prev_rows_spec(tile_s, rw_cols))] + pre_consts
               + post_rows(dr1, dr2, dk1, dk2, dv1, dv2, dw2, dkk2, da2, dg1))

        def body(*refs):
            i = pl.program_id(0)
            vals = pre_bwd(i, *[r[...] for r in refs[:n_in]])
            refs[n_in][...] = vals[0]
            refs[n_in + 1][...] = vals[1]
            for r, v in zip(refs[n_in + 2:], vals[2:]):
                @pl.when(i == 0)
                def _(r=r, v=v):
                    r[...] = v

                @pl.when(i > 0)
                def _(r=r, v=v):
                    r[...] += v

        acc_shapes = [(1, rw_cols), (1, cw), (1, cw), (LANES, cw), (LANES, cw), (1, cw), (1, cw)]
        return pl.pallas_call(
            body, name="rwkv_pre_bwd", grid=(n_s,), in_specs=[s for _, s in ins],
            out_specs=[_row_spec(tile_s, rw_cols), pl.BlockSpec((8, rw_cols), lambda i: (i, 0))]
            + [_full_spec(s) for s in acc_shapes],
            out_shape=[jax.ShapeDtypeStruct((t, rw_cols), F32), jax.ShapeDtypeStruct((8 * n_s, rw_cols), F32)]
            + [jax.ShapeDtypeStruct(s, F32) for s in acc_shapes],
            compiler_params=_params("arbitrary"),
        )(*[a for a, _ in ins])

    dz_main, dz_first, d_mu, d_w0, d_a0, d_wup, d_aup, d_k_k, d_k_a = pre_bwd_call()

    def add_next_row(i, dz, nxt8):
        row = lax.broadcasted_iota(jnp.int32, dz.shape, 0)
        carry = jnp.where(i < n_s - 1, _row_of(nxt8, 0), 0.0)
        return (dz + jnp.where(row == tile_s - 1, carry, 0.0),)

    (dz_rw,) = _rows("rwkv_shift_bwd", add_next_row, t, tile_s,
                     [(dz_main, _row_spec(tile_s, rw_cols)), (dz_first, _next_rows_spec(tile_s, rw_cols, n_s))],
                     [(rw_cols, BF16)])

    def fox_post_bwd(i, o, g, dy):
        _, vjp = jax.vjp(lambda oo, gg: oo * (gg * jax.nn.sigmoid(gg)), o, g)
        do, dg = vjp(dy)
        return do, _head_sum(do * o, _head_matrix(fw)), dg

    do_fox, dd_fox, dg_fox = _rows(
        "fox_post_bwd", fox_post_bwd, t, tile_s,
        [(o_fox, _row_spec(tile_s, fw)), (z_fox, _row_spec(tile_s, fw, 3)), (dy_fox, _row_spec(tile_s, fw))],
        [(fw, F32), (fw, F32), (fw, BF16)])
    dq_f, dk_f, dv_f, dcq, dck = _fox_bwd(z_fox, do_fox, c_fox, ct_fox, head_rows(lse_fox[:, ::HEAD]),
                                          head_rows(dd_fox[:, ::HEAD]), fw=fw)
    dc = head_cols(dcq) + dck.reshape(t, n_pair_f, LANES)[:, :, :2].reshape(t, fh)
    dfl, d_bf = _fox_cumsum_bwd(z_fox, b_pad, jnp.pad(dc, ((0, 0), (0, LANES - fh))), fw=fw, fh=fh)
    dz_fox = jnp.concatenate([dq_f, dk_f, dv_f, dg_fox, dfl], axis=1)

    g_w_rw = _matmul("d_w_in_rw", h, dz_rw, ta=True)
    g_w_fox = _matmul("d_w_in_fox", h, dz_fox, ta=True)
    g_w_gate = _matmul("d_w_in_gate", h, dz_gate, ta=True)
    dh = _matmul("d_h_rw", dz_rw, w_rw, tb=True)
    dh = _matmul("d_h_fox", dz_fox, w_fox, tb=True, add=dh)
    dh = _matmul("d_h_gate", dz_gate, w_gate, tb=True, add=dh)
    grad_x, d_g1 = _rows("norm1_bwd", norm_bwd, t, tile_s,
                         [(xs, _row_spec(tile_s, d)), full2(norm_g), (dh, _row_spec(tile_s, d)),
                          (dx1, _row_spec(tile_s, d))], [(d, F32)], [(1, d)])

    g_full = {
        "w_in": jnp.concatenate([g_w_rw, g_w_fox[:, :fox_cols], g_w_gate], axis=1),
        "rw_w_lora_up": d_wup[:lr], "rw_a_lora_up": d_aup[lr:], "w_up_rwkv": g_up_rw, "w_up_fox": g_up_fox,
        "w_out": g_w_out, "ple_proj": g_ple_proj, "ple_gate_w": g_ple_gate,
    }
    by_shard = [_shards_of_full(n, g_full[n]) for n in BIG]
    g_slab = jnp.stack([_pack_slab([by_shard[j][s] for j in range(len(BIG))], slab_rows) for s in range(N_SHARD)])
    g_red = _unpack_slab(_reduce_scatter(g_slab), shard_shapes)
    grads = {n: g_red[j][None] for j, n in enumerate(BIG)}

    small_parts = dict(norm_g=d_g1, rw_shift_mu=d_mu, rw_w0=d_w0, rw_a0=d_a0, rw_k_k=d_k_k, rw_k_a=d_k_a, rw_r_k=d_r_k,
                       rw_ln_g=d_ln_g, rw_ln_b=d_ln_b, fox_b_f=d_bf[:, :fh], ple_norm_g=d_g2, final_norm_g=d_gf)
    n_small = sum(-(-wts[n].size // LANES) for n in SMALL)
    small_rows = -(-n_small // 8) * 8
    small_shapes = [wts[n].shape for n in SMALL]
    g_small = _all_reduce_small(_pack_small([small_parts[n] for n in SMALL], small_rows))
    for n, g in zip(SMALL, _unpack_small(g_small, small_shapes)):
        grads[n] = g

    delta, new_m, new_v = {}, {}, {}
    for n in BIG:
        shp = wts[n].shape
        two = lambda a: a.reshape(shp[1], shp[2])
        dl, mm, vv = _adamw("adamw_" + n, two(wts[n]), two(grads[n]), two(mom[n]), two(vel[n]))
        delta[n], new_m[n], new_v[n] = dl.reshape(shp), mm.reshape(shp), vv.reshape(shp)
    dl, mm, vv = _adamw("adamw_small", _pack_small([wts[n] for n in SMALL], small_rows), g_small,
                        _pack_small([mom[n] for n in SMALL], small_rows), _pack_small([vel[n] for n in SMALL], small_rows))
    for store, packed in ((delta, dl), (new_m, mm), (new_v, vv)):
        for n, a in zip(SMALL, _unpack_small(packed, small_shapes)):
            store[n] = a

    loss = lax.psum(loss_row[0, 0], ("x", "y", "c"))
    return (loss, grad_x[None], *[grads[n] for n in WEIGHTS], *[delta[n] for n in WEIGHTS],
            *[new_m[n] for n in WEIGHTS], *[new_v[n] for n in WEIGHTS])
```

```python
import functools

import jax
import jax.numpy as jnp
from jax import lax
from jax.experimental import pallas as pl
from jax.experimental.pallas import tpu as pltpu

F32 = jnp.float32
BF16 = jnp.bfloat16
HIGHEST = lax.Precision.HIGHEST
MESH = pl.DeviceIdType.MESH

LANES = 128
HEAD = 64
NORM_EPS = 1e-6
GN_EPS = 64e-5
ADAM_LR = 0.001
ADAM_B1 = 0.9
ADAM_B2 = 0.999
ADAM_EPS = 1e-08
ADAM_WD = 0.01
ADAM_STEP = 10
N_SHARD = 4
VMEM_LIMIT = 56 * 1024 * 1024
PAIRS_PER_STEP = 4


def _params(*sem):
    return pltpu.CompilerParams(dimension_semantics=sem, vmem_limit_bytes=VMEM_LIMIT)


def _tile(n, cands):
    for c in cands:
        if c <= n and n % c == 0:
            return c
    return n


def _div_tile(n, cap, mult):
    return max(c for c in range(mult, min(n, cap) + 1, mult) if n % c == 0)


_ROW_TILES = (512, 256, 128, 64, 32, 16, 8)


def _dot(a, b, prec=None):
    return lax.dot_general(a, b, (((1,), (0,)), ((), ())), precision=prec, preferred_element_type=F32)


def _dot_nt(a, b, prec=None):
    return lax.dot_general(a, b, (((1,), (1,)), ((), ())), precision=prec, preferred_element_type=F32)


def _dot_tn(a, b, prec=None):
    return lax.dot_general(a, b, (((0,), (0,)), ((), ())), precision=prec, preferred_element_type=F32)


@jax.custom_vjp
def _bdot(x, w):
    return _dot(x.astype(BF16), w.astype(BF16))


def _bdot_fwd(x, w):
    return _bdot(x, w), (x, w)


def _bdot_bwd(res, ct):
    x, w = res
    return _dot_nt(ct.astype(BF16), w.astype(BF16)), _dot_tn(x.astype(BF16), ct.astype(BF16))


_bdot.defvjp(_bdot_fwd, _bdot_bwd)


def _head_matrix(width):
    c = lax.broadcasted_iota(jnp.int32, (width, LANES), 0)
    h = lax.broadcasted_iota(jnp.int32, (width, LANES), 1)
    return (c // HEAD == h).astype(F32)


def _head_sum(x, e):
    return _dot_nt(_dot(x, e, HIGHEST), e, HIGHEST)


def _softplus(x):
    return jnp.maximum(x, 0.0) + jnp.log1p(jnp.exp(-jnp.abs(x)))


def _lane_pick(x, idx):
    lane = lax.broadcasted_iota(jnp.int32, x.shape, 1)
    return jnp.sum(jnp.where(lane == idx, x, 0.0), axis=1, keepdims=True)


def _matmul(name, a, b, *, ta=False, tb=False, add=None, out_dtype=F32):
    m, k = (a.shape[1], a.shape[0]) if ta else a.shape
    n = b.shape[0] if tb else b.shape[1]
    tm = _tile(m, (512, 256, 128))
    tn = _tile(n, (1408, 1024, 768, 640, 512, 384, 256, 128))
    tk = _tile(k, (1408, 1024, 768, 640, 512, 384, 256, 128, 64, 32, 16))
    nk = k // tk
    dims = (((0 if ta else 1,), (1 if tb else 0,)), ((), ()))

    def body(*refs):
        a_ref, b_ref = refs[0], refs[1]
        o_ref, acc_ref = refs[-2], refs[-1]
        kk = pl.program_id(2)

        @pl.when(kk == 0)
        def _():
            acc_ref[...] = jnp.zeros_like(acc_ref)

        acc_ref[...] += lax.dot_general(a_ref[...].astype(BF16), b_ref[...].astype(BF16), dims,
                                        preferred_element_type=F32)

        @pl.when(kk == nk - 1)
        def _():
            r = acc_ref[...]
            if add is not None:
                r = r + refs[2][...].astype(F32)
            o_ref[...] = r.astype(o_ref.dtype)

    a_spec = pl.BlockSpec((tk, tm), lambda i, j, kk: (kk, i)) if ta else pl.BlockSpec((tm, tk), lambda i, j, kk: (i, kk))
    b_spec = pl.BlockSpec((tn, tk), lambda i, j, kk: (j, kk)) if tb else pl.BlockSpec((tk, tn), lambda i, j, kk: (kk, j))
    o_spec = pl.BlockSpec((tm, tn), lambda i, j, kk: (i, j))
    ins, specs = [a, b], [a_spec, b_spec]
    if add is not None:
        ins.append(add)
        specs.append(o_spec)
    return pl.pallas_call(
        body, name=name, grid=(m // tm, n // tn, nk), in_specs=specs, out_specs=o_spec,
        out_shape=jax.ShapeDtypeStruct((m, n), out_dtype),
        scratch_shapes=[pltpu.VMEM((tm, tn), F32)],
        compiler_params=_params("parallel", "parallel", "arbitrary"),
    )(*ins)


def _rows(name, fn, n_rows, tile, ins, outs, accs=()):
    n_in, n_out = len(ins), len(outs)

    def body(*refs):
        i = pl.program_id(0)
        vals = fn(i, *[r[...] for r in refs[:n_in]])
        for r, v in zip(refs[n_in:n_in + n_out], vals[:n_out]):
            r[...] = v.astype(r.dtype)
        for r, v in zip(refs[n_in + n_out:], vals[n_out:]):
            @pl.when(i == 0)
            def _(r=r, v=v):
                r[...] = v

            @pl.when(i > 0)
            def _(r=r, v=v):
                r[...] += v

    out_specs = [pl.BlockSpec((tile, w), lambda i: (i, 0)) for w, _ in outs]
    out_specs += [pl.BlockSpec(s, lambda i: (0, 0)) for s in accs]
    out_shape = [jax.ShapeDtypeStruct((n_rows, w), d) for w, d in outs]
    out_shape += [jax.ShapeDtypeStruct(s, F32) for s in accs]
    return pl.pallas_call(
        body, name=name, grid=(n_rows // tile,), in_specs=[s for _, s in ins], out_specs=out_specs,
        out_shape=out_shape, compiler_params=_params("arbitrary"),
    )(*[a for a, _ in ins])


def _row_spec(tile, width, col=0):
    return pl.BlockSpec((tile, width), lambda i: (i, col))


def _full_spec(shape):
    return pl.BlockSpec(shape, lambda i: (0,) * len(shape))


def _prev_rows_spec(tile, width):
    return pl.BlockSpec((8, width), lambda i: (jnp.maximum(i * (tile // 8) - 1, 0), 0))


def _next_rows_spec(tile, width, n_tiles):
    return pl.BlockSpec((8, width), lambda i: (jnp.minimum(i + 1, n_tiles - 1), 0))


def _row_of(x8, idx):
    r = lax.broadcasted_iota(jnp.int32, x8.shape, 0)
    return jnp.sum(jnp.where(r == idx, x8, 0.0), axis=0, keepdims=True)


def _rms(x, g):
    return x * lax.rsqrt(jnp.mean(x * x, axis=-1, keepdims=True) + NORM_EPS) * g


def _shifted(i, z, prev8):
    first = jnp.where(i > 0, _row_of(prev8, 7), 0.0)
    row = lax.broadcasted_iota(jnp.int32, z.shape, 0)
    return jnp.where(row == 0, first, pltpu.roll(z, 1, 0))


def _rw_pre(z, zp, mu, w0, a0, wup, aup, k_k, k_a, *, cw):
    zs = z + (zp - z) * mu
    r, k, v, g = (zs[:, j * cw:(j + 1) * cw] for j in range(4))
    lo = zs[:, 4 * cw:4 * cw + LANES]
    w_raw = w0 + _bdot(jnp.tanh(lo), wup)
    decay = jnp.exp(-jnp.exp(-_softplus(-w_raw) - 0.5))
    a = jax.nn.sigmoid(a0 + _bdot(lo, aup))
    e = _head_matrix(cw)
    kk = k * k_k
    kk = kk / jnp.maximum(jnp.sqrt(_head_sum(kk * kk, e)), 1e-12)
    k_mod = k * (1.0 + (a - 1.0) * k_a)
    return r, decay, k_mod, v, kk, a, g


def _rw_post(y, r, k_mod, v, g, ln_g, ln_b, r_k, *, cw):
    e = _head_matrix(cw)
    mu = _head_sum(y, e) * (1.0 / HEAD)
    d = y - mu
    var = _head_sum(d * d, e) * (1.0 / HEAD)
    yn = d * lax.rsqrt(var + GN_EPS) * ln_g + ln_b
    bonus = _head_sum(r * k_mod * r_k, e) * v
    return (yn + bonus) * (g * jax.nn.sigmoid(g))


def _merge(zg, u_rw, u_fox, *, d):
    return jax.nn.sigmoid(zg[:, :d]) * u_rw + jax.nn.sigmoid(zg[:, d:]) * u_fox


def _head_loss(x1, ple, gl, gf, tgt):
    x2 = x1 + ple * jax.nn.sigmoid(gl)
    err = _rms(x2, gf) - tgt
    return 0.5 * jnp.sum(jnp.mean(err * err, axis=-1, keepdims=True), axis=0, keepdims=True)


def _unit_lower_inverse(lo, c):
    n = lo.shape[0]
    ri = lax.broadcasted_iota(jnp.int32, (n, c, c), 1)
    ci = lax.broadcasted_iota(jnp.int32, (n, c, c), 2)
    x = (ri == ci).astype(F32)
    for s in range(c - 1):
        col = jnp.sum(jnp.where(ci == s, lo, 0.0), axis=2, keepdims=True)
        row = jnp.sum(jnp.where(ri == s, x, 0.0), axis=1, keepdims=True)
        x = x - col * row
    return x


def _rwkv_chunk(s0, r, w, k, v, kk, a, *, c):
    pairs = range(len(s0))
    lane = lax.broadcasted_iota(jnp.int32, (1, LANES), 1)
    heads = (lane < HEAD, lane >= HEAD)
    ti = lax.broadcasted_iota(jnp.int32, (c, c), 0)
    si = lax.broadcasted_iota(jnp.int32, (c, c), 1)
    incl = si <= ti
    strict = si < ti
    tri = incl.astype(F32)
    logw = [jnp.log(w[p]) for p in pairs]
    cum = [_dot(tri, logw[p], HIGHEST) for p in pairs]
    cum_end = [jnp.sum(logw[p], axis=0, keepdims=True) for p in pairs]
    g_inv = [jnp.exp(-cum[p]) for p in pairs]
    to_end = [jnp.exp(cum_end[p] - cum[p]) for p in pairs]
    b = [kk[p] * a[p] for p in pairs]
    beta = [b[p] * g_inv[p] for p in pairs]
    kap = [kk[p] * jnp.exp(cum[p] - logw[p]) for p in pairs]
    kt = [k[p] * g_inv[p] for p in pairs]
    rt = [r[p] * jnp.exp(cum[p]) for p in pairs]
    lhs = [jnp.concatenate([jnp.where(m, x[p], 0.0) for x in (kap, rt) for m in heads], axis=0) for p in pairs]
    vs_beta = [_dot_nt(lhs[p], beta[p], HIGHEST) for p in pairs]
    vs_kt = [_dot_nt(lhs[p], kt[p], HIGHEST) for p in pairs]
    strict2 = jnp.concatenate([strict, strict], axis=0)
    incl2 = jnp.concatenate([incl, incl], axis=0)
    lo = [jnp.where(strict2, vs_beta[p][:2 * c], 0.0) for p in pairs]
    mm = [jnp.where(strict2, vs_kt[p][:2 * c], 0.0) for p in pairs]
    arb = [jnp.where(incl2, vs_beta[p][2 * c:], 0.0) for p in pairs]
    ark = [jnp.where(incl2, vs_kt[p][2 * c:], 0.0) for p in pairs]
    tinv = _unit_lower_inverse(jnp.concatenate([lo[p][h * c:(h + 1) * c][None] for p in pairs for h in (0, 1)]), c)
    tinv = [jnp.concatenate([tinv[2 * p], tinv[2 * p + 1]], axis=0) for p in pairs]
    both = lambda x: jnp.where(heads[0], x[:c], x[c:])
    vs_s = [_dot_nt(jnp.concatenate([kap[p], rt[p]], axis=0), s0[p], HIGHEST) for p in pairs]
    rhs = [vs_s[p][:c] + both(_dot(mm[p], v[p], HIGHEST)) for p in pairs]
    u = [-both(_dot(tinv[p], rhs[p], HIGHEST)) for p in pairs]
    y = [vs_s[p][c:] + both(_dot(arb[p], u[p], HIGHEST) + _dot(ark[p], v[p], HIGHEST)) for p in pairs]
    rr = lax.broadcasted_iota(jnp.int32, (LANES, LANES), 0) < HEAD
    cc = lax.broadcasted_iota(jnp.int32, (LANES, LANES), 1) < HEAD
    ds = [_dot_tn(jnp.concatenate([u[p], v[p]], axis=0),
                  jnp.concatenate([b[p] * to_end[p], k[p] * to_end[p]], axis=0), HIGHEST) for p in pairs]
    s1 = [s0[p] * jnp.exp(cum_end[p]) + jnp.where(rr == cc, ds[p], 0.0) for p in pairs]
    return tuple(y), tuple(s1)


def _scan_tiles(t, n_pair):
    return _tile(t, (32, 16, 8)), _tile(t, (512, 256, 128, 64, 32)), _tile(n_pair, (PAIRS_PER_STEP, 2, 1))


def _scan_fwd(r, w, k, v, kk, a):
    t, width = r.shape
    c, tb, npb = _scan_tiles(t, width // LANES)
    n_grp, n_blk, n_cb = width // (LANES * npb), t // tb, tb // c

    def body(r_ref, w_ref, k_ref, v_ref, kk_ref, a_ref, y_ref, st_ref, s_scr):
        @pl.when(pl.program_id(1) == 0)
        def _():
            s_scr[...] = jnp.zeros_like(s_scr)

        def chunk(j, carry):
            sl = pl.ds(pl.multiple_of(j * c, c), c)
            lanes = [pl.ds(q * LANES, LANES) for q in range(npb)]
            s0 = tuple(s_scr[q] for q in range(npb))
            cols = lambda ref: tuple(ref[sl, ln] for ln in lanes)
            y, s1 = _rwkv_chunk(s0, cols(r_ref), cols(w_ref), cols(k_ref), cols(v_ref), cols(kk_ref), cols(a_ref), c=c)
            for q, ln in enumerate(lanes):
                st_ref[q, j] = s0[q]
                y_ref[sl, ln] = y[q]
                s_scr[q] = s1[q]
            return carry

        lax.fori_loop(0, n_cb, chunk, 0)

    blk = pl.BlockSpec((tb, npb * LANES), lambda p, i: (i, p))
    return pl.pallas_call(
        body, name="rwkv_scan_fwd", grid=(n_grp, n_blk), in_specs=[blk] * 6,
        out_specs=[blk, pl.BlockSpec((npb, n_cb, LANES, LANES), lambda p, i: (p, i, 0, 0))],
        out_shape=[jax.ShapeDtypeStruct((t, width), F32),
                   jax.ShapeDtypeStruct((width // LANES, t // c, LANES, LANES), F32)],
        scratch_shapes=[pltpu.VMEM((npb, LANES, LANES), F32)],
        compiler_params=_params("arbitrary", "arbitrary"),
    )(r, w, k, v, kk, a)


def _scan_bwd(r, w, k, v, kk, a, st, dy):
    t, width = r.shape
    c, tb, npb = _scan_tiles(t, width // LANES)
    n_grp, n_blk, n_cb = width // (LANES * npb), t // tb, tb // c

    def body(r_ref, w_ref, k_ref, v_ref, kk_ref, a_ref, st_ref, dy_ref,
             dr_ref, dw_ref, dk_ref, dv_ref, dkk_ref, da_ref, ds_scr):
        @pl.when(pl.program_id(1) == 0)
        def _():
            ds_scr[...] = jnp.zeros_like(ds_scr)

        def chunk(jj, carry):
            j = n_cb - 1 - jj
            sl = pl.ds(pl.multiple_of(j * c, c), c)
            lanes = [pl.ds(q * LANES, LANES) for q in range(npb)]
            cols = lambda ref: tuple(ref[sl, ln] for ln in lanes)
            args = (tuple(st_ref[q, j] for q in range(npb)), cols(r_ref), cols(w_ref), cols(k_ref), cols(v_ref),
                    cols(kk_ref), cols(a_ref))
            _, vjp = jax.vjp(functools.partial(_rwkv_chunk, c=c), *args)
            grads = vjp((cols(dy_ref), tuple(ds_scr[q] for q in range(npb))))
            for q, ln in enumerate(lanes):
                ds_scr[q] = grads[0][q]
                for ref, g in zip((dr_ref, dw_ref, dk_ref, dv_ref, dkk_ref, da_ref), grads[1:]):
                    ref[sl, ln] = g[q]
            return carry

        lax.fori_loop(0, n_cb, chunk, 0)

    blk = pl.BlockSpec((tb, npb * LANES), lambda p, i: (n_blk - 1 - i, p))
    stb = pl.BlockSpec((npb, n_cb, LANES, LANES), lambda p, i: (p, n_blk - 1 - i, 0, 0))
    return pl.pallas_call(
        body, name="rwkv_scan_bwd", grid=(n_grp, n_blk), in_specs=[blk] * 6 + [stb, blk], out_specs=[blk] * 6,
        out_shape=[jax.ShapeDtypeStruct((t, width), F32)] * 6,
        scratch_shapes=[pltpu.VMEM((npb, LANES, LANES), F32)],
        compiler_params=_params("arbitrary", "arbitrary"),
    )(r, w, k, v, kk, a, st, dy)


NEG = -1e30


def _fox_cumsum(zf, b_pad, *, fw, fh):
    t = zf.shape[0]
    tile = _tile(t, (256, 128, 64, 32, 16, 8))

    def body(fl_ref, b_ref, c_ref, carry):
        @pl.when(pl.program_id(0) == 0)
        def _():
            carry[...] = jnp.zeros_like(carry)

        lane = lax.broadcasted_iota(jnp.int32, (tile, LANES), 1)
        logf = jnp.where(lane < fh, -_softplus(-(fl_ref[...] + b_ref[...])), 0.0)
        ri = lax.broadcasted_iota(jnp.int32, (tile, tile), 0)
        ci = lax.broadcasted_iota(jnp.int32, (tile, tile), 1)
        c_ref[...] = carry[...] + _dot((ci <= ri).astype(F32), logf, HIGHEST)
        carry[...] += jnp.sum(logf, axis=0, keepdims=True)

    return pl.pallas_call(
        body, name="fox_cumsum", grid=(t // tile,),
        in_specs=[_row_spec(tile, LANES, 4 * fw // LANES), _full_spec((1, LANES))],
        out_specs=_row_spec(tile, LANES), out_shape=jax.ShapeDtypeStruct((t, LANES), F32),
        scratch_shapes=[pltpu.VMEM((1, LANES), F32)], compiler_params=_params("arbitrary"),
    )(zf, b_pad)


def _fox_cumsum_bwd(zf, b_pad, dc, *, fw, fh):
    t = zf.shape[0]
    tile = _tile(t, (256, 128, 64, 32, 16, 8))
    n = t // tile

    def body(fl_ref, b_ref, dc_ref, dfl_ref, db_ref, carry):
        i = pl.program_id(0)

        @pl.when(i == 0)
        def _():
            carry[...] = jnp.zeros_like(carry)
            db_ref[...] = jnp.zeros_like(db_ref)

        lane = lax.broadcasted_iota(jnp.int32, (tile, LANES), 1)
        dc_t = dc_ref[...]
        ri = lax.broadcasted_iota(jnp.int32, (tile, tile), 0)
        ci = lax.broadcasted_iota(jnp.int32, (tile, tile), 1)
        dlogf = carry[...] + _dot((ci >= ri).astype(F32), dc_t, HIGHEST)
        carry[...] += jnp.sum(dc_t, axis=0, keepdims=True)
        dfl = jnp.where(lane < fh, dlogf * jax.nn.sigmoid(-(fl_ref[...] + b_ref[...])), 0.0)
        dfl_ref[...] = dfl.astype(dfl_ref.dtype)
        db_ref[...] += jnp.sum(dfl, axis=0, keepdims=True)

    rev = lambda col: pl.BlockSpec((tile, LANES), lambda i: (n - 1 - i, col))
    return pl.pallas_call(
        body, name="fox_cumsum_bwd", grid=(n,),
        in_specs=[rev(4 * fw // LANES), _full_spec((1, LANES)), rev(0)],
        out_specs=[rev(0), _full_spec((1, LANES))],
        out_shape=[jax.ShapeDtypeStruct((t, LANES), BF16), jax.ShapeDtypeStruct((1, LANES), F32)],
        scratch_shapes=[pltpu.VMEM((1, LANES), F32)], compiler_params=_params("arbitrary"),
    )(zf, b_pad, dc)


def _fox_tile(t):
    return _tile(t, (512, 256, 128))


def _fox_fwd(zf, c, ct, *, fw):
    t = zf.shape[0]
    tq = _fox_tile(t)
    th = tq // 2
    n_pair, n_q = fw // LANES, t // tq
    scale = HEAD ** -0.5
    chains = [(h, qq) for h in (0, 1) for qq in (0, 1)]

    def body(q_ref, k_ref, v_ref, g_ref, c_ref, ct_ref, o_ref, lse_ref, y_ref):
        hp, i = pl.program_id(0), pl.program_id(1)
        lane = lax.broadcasted_iota(jnp.int32, (1, LANES), 1)
        in_head = (lane < HEAD, lane >= HEAD)
        rows = [pl.ds(qq * th, th) for qq in (0, 1)]
        qh = [jnp.where(in_head[h], q_ref[rows[qq], :] * scale, 0.0).astype(BF16) for h, qq in chains]
        cq = [_lane_pick(c_ref[rows[qq], :], 2 * hp + h) for h, qq in chains]
        qidx = lax.broadcasted_iota(jnp.int32, (th, tq), 0)
        kidx = lax.broadcasted_iota(jnp.int32, (th, tq), 1)

        def kv_step(j, carry, diagonal):
            m, l, acc = carry
            ks = pl.ds(pl.multiple_of(j * tq, tq), tq)
            kb = k_ref[ks, :].astype(BF16)
            vb = v_ref[ks, :]
            vh = [jnp.where(in_head[h], vb, 0.0).astype(BF16) for h in (0, 1)]
            ck = [ct_ref[0, j, pl.ds(h, 1), :] for h in (0, 1)]
            s = [_dot_nt(qh[n], kb) + cq[n] - ck[h] for n, (h, qq) in enumerate(chains)]
            if diagonal:
                s = [jnp.where(qq * th + qidx >= kidx, s[n], NEG) for n, (h, qq) in enumerate(chains)]
            m_new = [jnp.maximum(m[n], jnp.max(s[n], axis=1, keepdims=True)) for n in range(4)]
            p = [jnp.exp(s[n] - m_new[n]) for n in range(4)]
            alpha = [jnp.exp(m[n] - m_new[n]) for n in range(4)]
            l = [l[n] * alpha[n] + jnp.sum(p[n], axis=1, keepdims=True) for n in range(4)]
            pv = [_dot(p[n].astype(BF16), vh[h]) for n, (h, qq) in enumerate(chains)]
            acc = [acc[qq] * jnp.where(in_head[0], alpha[qq], alpha[2 + qq]) + pv[qq] + pv[2 + qq] for qq in (0, 1)]
            return tuple(m_new), tuple(l), tuple(acc)

        init = (tuple(jnp.full((th, 1), NEG, F32) for _ in chains), tuple(jnp.zeros((th, 1), F32) for _ in chains),
                tuple(jnp.zeros((th, LANES), F32) for _ in (0, 1)))
        carry = lax.fori_loop(0, i, functools.partial(kv_step, diagonal=False), init)
        m, l, acc = kv_step(i, carry, True)
        for qq in (0, 1):
            o = acc[qq] / jnp.where(in_head[0], l[qq], l[2 + qq])
            g = g_ref[rows[qq], :]
            o_ref[rows[qq], :] = o
            lse_ref[rows[qq], :] = jnp.where(in_head[0], m[qq] + jnp.log(l[qq]), m[2 + qq] + jnp.log(l[2 + qq]))
            y_ref[rows[qq], :] = (o * (g * jax.nn.sigmoid(g))).astype(y_ref.dtype)

    npw = fw // LANES
    blk = lambda col0: pl.BlockSpec((tq, LANES), lambda hp, i: (i, col0 + hp))
    res = lambda col0: pl.BlockSpec((t, LANES), lambda hp, i: (0, col0 + hp))
    out_blk = pl.BlockSpec((tq, LANES), lambda hp, i: (i, hp))
    return pl.pallas_call(
        body, name="fox_attn_fwd", grid=(n_pair, n_q),
        in_specs=[blk(0), res(npw), res(2 * npw), blk(3 * npw),
                  pl.BlockSpec((tq, LANES), lambda hp, i: (i, 0)),
                  pl.BlockSpec((1, n_q, 2, tq), lambda hp, i: (hp, 0, 0, 0))],
        out_specs=[out_blk, out_blk, out_blk],
        out_shape=[jax.ShapeDtypeStruct((t, fw), F32), jax.ShapeDtypeStruct((t, fw), F32),
                   jax.ShapeDtypeStruct((t, fw), BF16)],
        compiler_params=_params("arbitrary", "arbitrary"),
    )(zf, zf, zf, zf, c, ct)


def _fox_bwd(zf, do, c, ct, lse_r, dd_r, *, fw):
    t = zf.shape[0]
    tq = _fox_tile(t)
    n_pair, n_q = fw // LANES, t // tq
    scale = HEAD ** -0.5

    def body(q_ref, k_ref, v_ref, do_ref, c_ref, ct_ref, lse_ref, dd_ref,
             dq_ref, dk_ref, dv_ref, dcq_ref, dck_ref, dq_acc, dcq_acc):
        hp, j = pl.program_id(0), pl.program_id(1)

        @pl.when(j == 0)
        def _():
            dq_acc[...] = jnp.zeros_like(dq_acc)
            dcq_acc[...] = jnp.zeros_like(dcq_acc)

        lane = lax.broadcasted_iota(jnp.int32, (1, LANES), 1)
        in_head = (lane < HEAD, lane >= HEAD)
        kb = k_ref[...]
        kh = [jnp.where(m, kb, 0.0).astype(BF16) for m in in_head]
        vb = v_ref[...].astype(BF16)
        c_k = c_ref[...]
        ck = [_lane_pick(c_k, 2 * hp + h) for h in (0, 1)]
        kidx = lax.broadcasted_iota(jnp.int32, (tq, tq), 0)
        qidx = lax.broadcasted_iota(jnp.int32, (tq, tq), 1)

        def q_step(i, carry, diagonal):
            dk, dv, dck = carry
            qs = pl.ds(pl.multiple_of(i * tq, tq), tq)
            qf = q_ref[qs, :] * scale
            dof = do_ref[qs, :]
            qh = [jnp.where(m, qf, 0.0).astype(BF16) for m in in_head]
            doh = [jnp.where(m, dof, 0.0).astype(BF16) for m in in_head]
            row = lambda ref, h: ref[0, i, pl.ds(h, 1), :]
            st = [_dot_nt(kh[h], qh[h]) + row(ct_ref, h) - ck[h] for h in (0, 1)]
            p = [jnp.exp(st[h] - row(lse_ref, h)) for h in (0, 1)]
            if diagonal:
                p = [jnp.where(kidx <= qidx, p[h], 0.0) for h in (0, 1)]
            dst = [p[h] * (_dot_nt(vb, doh[h]) - row(dd_ref, h)) for h in (0, 1)]
            p16 = [x.astype(BF16) for x in p]
            ds16 = [x.astype(BF16) for x in dst]
            dv = dv + _dot(p16[0], doh[0]) + _dot(p16[1], doh[1])
            dk = dk + _dot(ds16[0], qh[0]) + _dot(ds16[1], qh[1])
            dq_acc[qs, :] += _dot_tn(ds16[0], kh[0]) + _dot_tn(ds16[1], kh[1])
            for h in (0, 1):
                dcq_acc[i, pl.ds(h, 1), :] += jnp.sum(dst[h], axis=0, keepdims=True)
            dck = tuple(dck[h] - jnp.sum(dst[h], axis=1, keepdims=True) for h in (0, 1))
            return dk, dv, dck

        zero = jnp.zeros((tq, LANES), F32)
        carry = q_step(j, (zero, zero, (jnp.zeros((tq, 1), F32),) * 2), True)
        dk, dv, dck = lax.fori_loop(j + 1, n_q, functools.partial(q_step, diagonal=False), carry)
        dk_ref[...] = dk.astype(dk_ref.dtype)
        dv_ref[...] = dv.astype(dv_ref.dtype)
        dck_ref[...] = jnp.where(lane == 0, dck[0], jnp.where(lane == 1, dck[1], 0.0))

        @pl.when(j == n_q - 1)
        def _():
            dq_ref[...] = (dq_acc[...] * scale).astype(dq_ref.dtype)
            dcq_ref[0] = dcq_acc[...]

    npw = fw // LANES
    res_z = lambda col0: pl.BlockSpec((t, LANES), lambda hp, j: (0, col0 + hp))
    blk_z = lambda col0: pl.BlockSpec((tq, LANES), lambda hp, j: (j, col0 + hp))
    res = pl.BlockSpec((t, LANES), lambda hp, j: (0, hp))
    blk = pl.BlockSpec((tq, LANES), lambda hp, j: (j, hp))
    rows = pl.BlockSpec((1, n_q, 2, tq), lambda hp, j: (hp, 0, 0, 0))
    return pl.pallas_call(
        body, name="fox_attn_bwd", grid=(n_pair, n_q),
        in_specs=[res_z(0), blk_z(npw), blk_z(2 * npw), res, pl.BlockSpec((tq, LANES), lambda hp, j: (j, 0)),
                  rows, rows, rows],
        out_specs=[res, blk, blk, rows, blk],
        out_shape=[jax.ShapeDtypeStruct((t, fw), BF16), jax.ShapeDtypeStruct((t, fw), BF16),
                   jax.ShapeDtypeStruct((t, fw), BF16), jax.ShapeDtypeStruct((n_pair, n_q, 2, tq), F32),
                   jax.ShapeDtypeStruct((t, fw), F32)],
        scratch_shapes=[pltpu.VMEM((t, LANES), F32), pltpu.VMEM((n_q, 2, tq), F32)],
        compiler_params=_params("arbitrary", "arbitrary"),
    )(zf, zf, zf, do, c, ct, lse_r, dd_r)


def _adamw_math(w, g, m, v):
    m = ADAM_B1 * m + (1.0 - ADAM_B1) * g
    v = ADAM_B2 * v + (1.0 - ADAM_B2) * jnp.square(g)
    m_hat = m / (1.0 - ADAM_B1 ** ADAM_STEP)
    v_hat = v / (1.0 - ADAM_B2 ** ADAM_STEP)
    delta = -ADAM_LR * (m_hat / (jnp.sqrt(v_hat) + ADAM_EPS) + ADAM_WD * w)
    return delta, m, v


def _adamw(name, w, g, m, v):
    rows, cols = w.shape
    tile = _tile(rows, (128, 64, 32, 16, 8))
    spec = _row_spec(tile, cols)
    return _rows(name, lambda i, *a: _adamw_math(*a), rows, tile, [(w, spec), (g, spec), (m, spec), (v, spec)],
                 [(cols, F32)] * 3)


def _place():
    return lax.axis_index("x"), lax.axis_index("y"), lax.axis_index("c")


def _other_chips(x, y):
    return [(1 - x, y), (x, 1 - y), (1 - x, 1 - y)]


HBM_SPEC = pl.BlockSpec(memory_space=pltpu.HBM)


def _all_gather_shards(slab):
    rows, width = slab.shape
    rh = rows // 2

    def body(src_ref, out_ref, send_sems, recv_sems):
        x, y, c = _place()
        me = 2 * x + y
        sibling = (x, y, 1 - c)
        chips = _other_chips(x, y)

        def part(chip, half):
            return out_ref.at[chip, pl.ds(half * rh, rh), :]

        def copy(k, src, dst, to):
            return pltpu.make_async_remote_copy(src_ref=src, dst_ref=dst, send_sem=send_sems.at[k],
                                                recv_sem=recv_sems.at[k], device_id=to, device_id_type=MESH)

        first = [copy(j, src_ref.at[pl.ds(c * rh, rh), :], part(me, c), (px, py, c)) for j, (px, py) in enumerate(chips)]
        for cp in first:
            cp.start()
        passed = [copy(3 + j, part(2 * px + py, c), part(2 * px + py, c), sibling) for j, (px, py) in enumerate(chips)]
        for j, (px, py) in enumerate(chips):
            copy(j, part(2 * px + py, c), part(2 * px + py, c), sibling).wait_recv()
            passed[j].start()
        for j, (px, py) in enumerate(chips):
            copy(3 + j, part(2 * px + py, 1 - c), part(2 * px + py, 1 - c), sibling).wait_recv()
        for cp in first + passed:
            cp.wait_send()

    return pl.pallas_call(
        body, name="weights_all_gather", in_specs=[HBM_SPEC], out_specs=HBM_SPEC,
        out_shape=jax.ShapeDtypeStruct((N_SHARD, rows, width), slab.dtype),
        scratch_shapes=[pltpu.SemaphoreType.DMA((6,)), pltpu.SemaphoreType.DMA((6,))],
    )(slab)


def _chip_index():
    return jnp.reshape(2 * lax.axis_index("x") + lax.axis_index("y"), (1,)).astype(jnp.int32)


def _place_own_shard(gathered, slab):
    rows, width = slab.shape
    tile = _div_tile(rows, 256, 16)

    def body(me_ref, s_ref, g_ref, o_ref):
        o_ref[0] = s_ref[...]

    return pl.pallas_call(
        body, name="weights_place_own",
        grid_spec=pltpu.PrefetchScalarGridSpec(
            num_scalar_prefetch=1, grid=(rows // tile,),
            in_specs=[pl.BlockSpec((tile, width), lambda i, me: (i, 0)), pl.BlockSpec(memory_space=pl.ANY)],
            out_specs=pl.BlockSpec((1, tile, width), lambda i, me: (me[0], i, 0))),
        out_shape=jax.ShapeDtypeStruct(gathered.shape, gathered.dtype), input_output_aliases={2: 0},
        compiler_params=_params("parallel"),
    )(_chip_index(), slab, gathered)


def _sibling_exchange(g):
    _, rows, width = g.shape
    rh = rows // 2

    def body(g_ref, out_ref, send_sems, recv_sems):
        x, y, c = _place()
        copies = [pltpu.make_async_remote_copy(
            src_ref=g_ref.at[s, pl.ds((1 - c) * rh, rh), :], dst_ref=out_ref.at[s], send_sem=send_sems.at[s],
            recv_sem=recv_sems.at[s], device_id=(x, y, 1 - c), device_id_type=MESH) for s in range(N_SHARD)]
        for cp in copies:
            cp.start()
        for cp in copies:
            cp.wait()

    return pl.pallas_call(
        body, name="grad_sibling_exchange", in_specs=[HBM_SPEC], out_specs=HBM_SPEC,
        out_shape=jax.ShapeDtypeStruct((N_SHARD, rh, width), g.dtype),
        scratch_shapes=[pltpu.SemaphoreType.DMA((N_SHARD,)), pltpu.SemaphoreType.DMA((N_SHARD,))],
    )(g)


def _add_sibling(g, got):
    _, rows, width = g.shape
    rh = rows // 2
    tile = _div_tile(rh, 256, 16)
    c_arr = jnp.reshape(lax.axis_index("c"), (1,)).astype(jnp.int32)

    def body(c_ref, a_ref, b_ref, o_ref):
        o_ref[...] = (a_ref[0] + b_ref[...]).astype(o_ref.dtype)

    return pl.pallas_call(
        body, name="grad_add_sibling",
        grid_spec=pltpu.PrefetchScalarGridSpec(
            num_scalar_prefetch=1, grid=(N_SHARD, rh // tile),
            in_specs=[pl.BlockSpec((1, 1, tile, width), lambda s, i, c: (s, c[0], i, 0)),
                      pl.BlockSpec((1, tile, width), lambda s, i, c: (s, i, 0))],
            out_specs=pl.BlockSpec((1, tile, width), lambda s, i, c: (s, i, 0))),
        out_shape=jax.ShapeDtypeStruct((N_SHARD, rh, width), BF16),
        compiler_params=_params("parallel", "parallel"),
    )(c_arr, g.reshape(N_SHARD, 2, rh, width), got)


def _chip_exchange(p):
    def body(p_ref, out_ref, send_sems, recv_sems):
        x, y, c = _place()
        me = 2 * x + y
        copies = [pltpu.make_async_remote_copy(
            src_ref=p_ref.at[2 * px + py], dst_ref=out_ref.at[me], send_sem=send_sems.at[j],
            recv_sem=recv_sems.at[j], device_id=(px, py, c), device_id_type=MESH)
            for j, (px, py) in enumerate(_other_chips(x, y))]
        for cp in copies:
            cp.start()
        for cp in copies:
            cp.wait()

    return pl.pallas_call(
        body, name="grad_chip_exchange", in_specs=[HBM_SPEC], out_specs=HBM_SPEC,
        out_shape=jax.ShapeDtypeStruct(p.shape, p.dtype),
        scratch_shapes=[pltpu.SemaphoreType.DMA((3,)), pltpu.SemaphoreType.DMA((3,))],
    )(p)


def _sum_chips(p, got):
    _, rh, width = p.shape
    tile = _div_tile(rh, 256, 16)
    n_t = rh // tile
    place = jnp.stack([2 * lax.axis_index("x") + lax.axis_index("y"), lax.axis_index("c")]).astype(jnp.int32)

    def body(pl_ref, own_ref, r0, r1, r2, r3, o_ref):
        me = pl_ref[0]
        own = own_ref[0].astype(F32)
        t = [jnp.where(me == s, own, r[0].astype(F32)) for s, r in enumerate((r0, r1, r2, r3))]
        o_ref[...] = ((t[0] + t[1]) + t[2]) + t[3]

    def slot(s):
        return pl.BlockSpec((1, tile, width), lambda i, pc: (jnp.where(pc[0] == s, (s + 1) % N_SHARD, s), i, 0))

    return pl.pallas_call(
        body, name="grad_sum_chips",
        grid_spec=pltpu.PrefetchScalarGridSpec(
            num_scalar_prefetch=1, grid=(n_t,),
            in_specs=[pl.BlockSpec((1, tile, width), lambda i, pc: (pc[0], i, 0))] + [slot(s) for s in range(N_SHARD)],
            out_specs=pl.BlockSpec((tile, width), lambda i, pc: (pc[1] * n_t + i, 0))),
        out_shape=jax.ShapeDtypeStruct((2 * rh, width), F32), compiler_params=_params("parallel"),
    )(place, p, got, got, got, got)


def _join_halves(full):
    rows, width = full.shape
    rh = rows // 2

    def body(f_ref, out_ref, send_sem, recv_sem):
        x, y, c = _place()
        cp = pltpu.make_async_remote_copy(
            src_ref=f_ref.at[pl.ds(c * rh, rh), :], dst_ref=out_ref.at[pl.ds(c * rh, rh), :], send_sem=send_sem,
            recv_sem=recv_sem, device_id=(x, y, 1 - c), device_id_type=MESH)
        cp.start()
        cp.wait_send()
        pltpu.make_async_remote_copy(
            src_ref=f_ref.at[pl.ds((1 - c) * rh, rh), :], dst_ref=out_ref.at[pl.ds((1 - c) * rh, rh), :],
            send_sem=send_sem, recv_sem=recv_sem, device_id=(x, y, 1 - c), device_id_type=MESH).wait_recv()

    return pl.pallas_call(
        body, name="grad_join_halves", in_specs=[HBM_SPEC], out_specs=HBM_SPEC,
        out_shape=jax.ShapeDtypeStruct(full.shape, full.dtype), input_output_aliases={0: 0},
        scratch_shapes=[pltpu.SemaphoreType.DMA, pltpu.SemaphoreType.DMA],
    )(full)


def _reduce_scatter(g):
    chip_sum = _add_sibling(g, _sibling_exchange(g))
    return _join_halves(_sum_chips(chip_sum, _chip_exchange(chip_sum)))


def _all_reduce_small(v):
    rows = v.shape[0]

    def body(v_ref, out_ref, gather, send_sems, recv_sems):
        x, y, c = _place()
        gather[4 * x + 2 * y + c] = v_ref[...]
        flips = [(dx, dy, dc) for dx in (0, 1) for dy in (0, 1) for dc in (0, 1)][1:]
        peers = [((x + dx) % 2, (y + dy) % 2, (c + dc) % 2) for dx, dy, dc in flips]
        copies = [pltpu.make_async_remote_copy(
            src_ref=v_ref, dst_ref=gather.at[4 * x + 2 * y + c], send_sem=send_sems.at[j], recv_sem=recv_sems.at[j],
            device_id=peer, device_id_type=MESH) for j, peer in enumerate(peers)]
        for cp in copies:
            cp.start()
        for j, (px, py, pc) in enumerate(peers):
            pltpu.make_async_remote_copy(
                src_ref=v_ref, dst_ref=gather.at[4 * px + 2 * py + pc], send_sem=send_sems.at[j],
                recv_sem=recv_sems.at[j], device_id=(px, py, pc), device_id_type=MESH).wait_recv()
        for cp in copies:
            cp.wait_send()
        acc = gather[0]
        for d in range(1, 8):
            acc = acc + gather[d]
        out_ref[...] = acc

    vm = pl.BlockSpec(memory_space=pltpu.VMEM)
    return pl.pallas_call(
        body, name="small_grads_all_reduce", in_specs=[vm], out_specs=vm,
        out_shape=jax.ShapeDtypeStruct(v.shape, F32),
        scratch_shapes=[pltpu.VMEM((8, rows, LANES), F32), pltpu.SemaphoreType.DMA((7,)), pltpu.SemaphoreType.DMA((7,))],
    )(v)


def _pack_slab(first, rest, rows):
    width = first.shape[1]
    flat = jnp.concatenate([p.reshape(-1) for p in rest])
    below = jnp.pad(flat, (0, (rows - first.shape[0]) * width - flat.shape[0])).reshape(-1, width)
    return jnp.concatenate([first, below], axis=0)


def _unpack_slab(slab, first_rows, shapes):
    flat = slab[first_rows:].reshape(-1)
    out, off = [slab[:first_rows]], 0
    for s in shapes:
        n = s[0] * s[1]
        out.append(flat[off:off + n].reshape(s))
        off += n
    return out


def _pad_lanes(v):
    v = v.reshape(1, -1)
    return jnp.pad(v, ((0, 0), (0, -v.shape[1] % LANES)))


def _pack_small(vs, rows):
    flat = jnp.concatenate([_pad_lanes(v) for v in vs], axis=1)
    return jnp.pad(flat, ((0, 0), (0, rows * LANES - flat.shape[1]))).reshape(rows, LANES)


def _unpack_small(packed, shapes):
    flat = packed.reshape(-1)
    out, off = [], 0
    for s in shapes:
        n = 1
        for d in s:
            n *= d
        out.append(flat[off:off + n].reshape(s))
        off += n + (-n % LANES)
    return out


BIG = ("w_in", "rw_w_lora_up", "rw_a_lora_up", "w_up_rwkv", "w_up_fox", "w_out", "ple_proj", "ple_gate_w")
ROW_SHARDED = ("w_out", "ple_gate_w")
SMALL = ("norm_g", "rw_shift_mu", "rw_w0", "rw_a0", "rw_k_k", "rw_k_a", "rw_r_k", "rw_ln_g", "rw_ln_b", "fox_b_f",
         "ple_norm_g", "final_norm_g")
WEIGHTS = ("norm_g", "w_in", "rw_shift_mu", "rw_w0", "rw_w_lora_up", "rw_a0", "rw_a_lora_up", "rw_k_k", "rw_k_a",
           "rw_r_k", "rw_ln_g", "rw_ln_b", "fox_b_f", "w_up_rwkv", "w_up_fox", "w_out", "ple_proj", "ple_gate_w",
           "ple_norm_g", "final_norm_g")


def _full_from_shards(name, gathered):
    axis = 0 if name in ROW_SHARDED else 1
    return jnp.concatenate([gathered[s] for s in range(N_SHARD)], axis=axis)


def _shards_of_full(name, full):
    axis = 0 if name in ROW_SHARDED else 1
    return jnp.split(full, N_SHARD, axis=axis)


def kernel(x, p, norm_g, w_in, rw_shift_mu, rw_w0, rw_w_lora_up, rw_a0, rw_a_lora_up, rw_k_k, rw_k_a, rw_r_k, rw_ln_g, rw_ln_b, fox_b_f, w_up_rwkv, w_up_fox, w_out, ple_proj, ple_gate_w, ple_norm_g, final_norm_g, loss_target, m_norm_g, m_w_in, m_rw_shift_mu, m_rw_w0, m_rw_w_lora_up, m_rw_a0, m_rw_a_lora_up, m_rw_k_k, m_rw_k_a, m_rw_r_k, m_rw_ln_g, m_rw_ln_b, m_fox_b_f, m_w_up_rwkv, m_w_up_fox, m_w_out, m_ple_proj, m_ple_gate_w, m_ple_norm_g, m_final_norm_g, v_norm_g, v_w_in, v_rw_shift_mu, v_rw_w0, v_rw_w_lora_up, v_rw_a0, v_rw_a_lora_up, v_rw_k_k, v_rw_k_a, v_rw_r_k, v_rw_ln_g, v_rw_ln_b, v_fox_b_f, v_w_up_rwkv, v_w_up_fox, v_w_out, v_ple_proj, v_ple_gate_w, v_ple_norm_g, v_final_norm_g):
    wts = dict(norm_g=norm_g, w_in=w_in, rw_shift_mu=rw_shift_mu, rw_w0=rw_w0, rw_w_lora_up=rw_w_lora_up, rw_a0=rw_a0,
               rw_a_lora_up=rw_a_lora_up, rw_k_k=rw_k_k, rw_k_a=rw_k_a, rw_r_k=rw_r_k, rw_ln_g=rw_ln_g, rw_ln_b=rw_ln_b,
               fox_b_f=fox_b_f, w_up_rwkv=w_up_rwkv, w_up_fox=w_up_fox, w_out=w_out, ple_proj=ple_proj,
               ple_gate_w=ple_gate_w, ple_norm_g=ple_norm_g, final_norm_g=final_norm_g)
    mom = dict(norm_g=m_norm_g, w_in=m_w_in, rw_shift_mu=m_rw_shift_mu, rw_w0=m_rw_w0, rw_w_lora_up=m_rw_w_lora_up,
               rw_a0=m_rw_a0, rw_a_lora_up=m_rw_a_lora_up, rw_k_k=m_rw_k_k, rw_k_a=m_rw_k_a, rw_r_k=m_rw_r_k,
               rw_ln_g=m_rw_ln_g, rw_ln_b=m_rw_ln_b, fox_b_f=m_fox_b_f, w_up_rwkv=m_w_up_rwkv, w_up_fox=m_w_up_fox,
               w_out=m_w_out, ple_proj=m_ple_proj, ple_gate_w=m_ple_gate_w, ple_norm_g=m_ple_norm_g,
               final_norm_g=m_final_norm_g)
    vel = dict(norm_g=v_norm_g, w_in=v_w_in, rw_shift_mu=v_rw_shift_mu, rw_w0=v_rw_w0, rw_w_lora_up=v_rw_w_lora_up,
               rw_a0=v_rw_a0, rw_a_lora_up=v_rw_a_lora_up, rw_k_k=v_rw_k_k, rw_k_a=v_rw_k_a, rw_r_k=v_rw_r_k,
               rw_ln_g=v_rw_ln_g, rw_ln_b=v_rw_ln_b, fox_b_f=v_fox_b_f, w_up_rwkv=v_w_up_rwkv, w_up_fox=v_w_up_fox,
               w_out=v_w_out, ple_proj=v_ple_proj, ple_gate_w=v_ple_gate_w, ple_norm_g=v_ple_norm_g,
               final_norm_g=v_final_norm_g)

    t, d = x.shape[1], x.shape[2]
    cw = rw_w0.shape[1]
    lr = rw_w_lora_up.shape[1]
    fh = fox_b_f.shape[1]
    fw = fh * HEAD
    rw_cols = 4 * cw + 2 * lr
    fox_cols = 4 * fw + fh
    assert 2 * lr == LANES and cw % LANES == 0 and fw % LANES == 0 and fh <= LANES
    xs = x[0]
    ps = p[0, 0]
    tgt = loss_target[0]

    assert BIG[0] == "w_in"
    shard_shapes = [wts[n].shape[1:] for n in BIG]
    slab_w = shard_shapes[0][1]
    n_rest = sum(s[0] * s[1] for s in shard_shapes[1:])
    slab_rows = -(-(d + -(-n_rest // slab_w)) // 32) * 32
    w_slab = _pack_slab(w_in[0], [wts[n][0] for n in BIG[1:]], slab_rows)
    cast_tile = _div_tile(slab_rows, 256, 16)
    (w_slab16,) = _rows("weights_to_bf16", lambda i, a: (a,), slab_rows, cast_tile,
                        [(w_slab, _row_spec(cast_tile, slab_w))], [(slab_w, BF16)])
    gathered = _place_own_shard(_all_gather_shards(w_slab16), w_slab16)
    per_shard = [_unpack_slab(gathered[s], d, shard_shapes[1:]) for s in range(N_SHARD)]
    full = {n: _full_from_shards(n, [per_shard[s][j] for s in range(N_SHARD)]) for j, n in enumerate(BIG)}
    w_rw = full["w_in"][:, :rw_cols]
    w_fox = jnp.pad(full["w_in"][:, rw_cols:rw_cols + fox_cols], ((0, 0), (0, LANES - fh)))
    w_gate = full["w_in"][:, rw_cols + fox_cols:]
    wup_pad = jnp.pad(full["rw_w_lora_up"], ((0, lr), (0, 0)))
    aup_pad = jnp.pad(full["rw_a_lora_up"], ((lr, 0), (0, 0)))
    b_pad = _pad_lanes(fox_b_f)
    r_k_row = rw_r_k.reshape(1, cw)
    gf_row = final_norm_g.reshape(1, d)

    tile = _tile(t, (256, 128, 64, 32, 16, 8))
    tile_s = _tile(t, (128, 64, 32, 16, 8))
    n_s = t // tile_s
    full2 = lambda a: (a, _full_spec(a.shape))

    (h,) = _rows("norm1", lambda i, a, g: (_rms(a, g),), t, tile, [(xs, _row_spec(tile, d)), full2(norm_g)], [(d, BF16)])
    z_rw = _matmul("proj_rw", h, w_rw)
    z_fox = _matmul("proj_fox", h, w_fox)
    z_gate = _matmul("proj_gate", h, w_gate)

    pre_consts = [full2(rw_shift_mu), full2(rw_w0), full2(rw_a0), full2(wup_pad), full2(aup_pad), full2(rw_k_k),
                  full2(rw_k_a)]

    def pre_fwd(i, z, prev8, *consts):
        return _rw_pre(z, _shifted(i, z, prev8), *consts, cw=cw)

    r_, w_, k_, v_, kk_, a_, g_ = _rows(
        "rwkv_pre", pre_fwd, t, tile_s,
        [(z_rw, _row_spec(tile_s, rw_cols)), (z_rw, _prev_rows_spec(tile_s, rw_cols))] + pre_consts, [(cw, F32)] * 7)
    y_scan, states = _scan_fwd(r_, w_, k_, v_, kk_, a_)
    post_consts = [full2(rw_ln_g), full2(rw_ln_b), full2(r_k_row)]
    post_rows = lambda *arrs: [(a, _row_spec(tile_s, cw)) for a in arrs]
    (y_rw,) = _rows("rwkv_post", lambda i, *a: (_rw_post(*a, cw=cw),), t, tile_s,
                    post_rows(y_scan, r_, k_, v_, g_) + post_consts, [(cw, BF16)])

    c_fox = _fox_cumsum(z_fox, b_pad, fw=fw, fh=fh)
    tq = _fox_tile(t)
    n_pair_f = fw // LANES
    head_rows = lambda a: a.T.reshape(n_pair_f, 2, t // tq, tq).transpose(0, 2, 1, 3)
    head_cols = lambda a: a.transpose(0, 2, 1, 3).reshape(fh, t).T
    ct_fox = head_rows(c_fox[:, :fh])
    o_fox, lse_fox, y_fox = _fox_fwd(z_fox, c_fox, ct_fox, fw=fw)

    u_rw = _matmul("up_rwkv", y_rw, full["w_up_rwkv"])
    u_fox = _matmul("up_fox", y_fox, full["w_up_fox"])
    (merged,) = _rows("merge", lambda i, zg, a, b: (_merge(zg, a, b, d=d),), t, tile,
                      [(z_gate, _row_spec(tile, 2 * d)), (u_rw, _row_spec(tile, d)), (u_fox, _row_spec(tile, d))],
                      [(d, BF16)])
    x1 = _matmul("out_proj", merged, full["w_out"], add=xs)
    (n2,) = _rows("norm2", lambda i, a, g: (_rms(a, g),), t, tile, [(x1, _row_spec(tile, d)), full2(ple_norm_g)],
                  [(d, BF16)])
    gl = _matmul("ple_gate", n2, full["ple_gate_w"])
    ple = _matmul("ple_proj", ps, full["ple_proj"])

    def head_bwd(i, x1_t, ple_t, gl_t, gf, tg):
        loss, vjp = jax.vjp(lambda a, b, cc, g: _head_loss(a, b, cc, g, tg), x1_t, ple_t, gl_t, gf)
        dx1, dple, dgl, dgf = vjp(jnp.ones((1, 1), F32))
        return dx1, dple, dgl, jnp.broadcast_to(loss, (1, LANES)), dgf

    dx2, dple, dgl, loss_row, d_gf = _rows(
        "loss_head", head_bwd, t, tile_s,
        [(x1, _row_spec(tile_s, d)), (ple, _row_spec(tile_s, d)), (gl, _row_spec(tile_s, d)), full2(gf_row),
         (tgt, _row_spec(tile_s, d))],
        [(d, F32), (d, BF16), (d, BF16)], [(1, LANES), (1, d)])

    g_ple_proj = _matmul("d_ple_proj", ps, dple, ta=True)
    g_ple_gate = _matmul("d_ple_gate_w", n2, dgl, ta=True)
    dn2 = _matmul("d_n2", dgl, full["ple_gate_w"], tb=True)

    def norm_bwd(i, a, g, dh, res):
        _, vjp = jax.vjp(_rms, a, g)
        da, dg = vjp(dh)
        return res + da, dg

    dx1, d_g2 = _rows("norm2_bwd", norm_bwd, t, tile_s,
                      [(x1, _row_spec(tile_s, d)), full2(ple_norm_g), (dn2, _row_spec(tile_s, d)),
                       (dx2, _row_spec(tile_s, d))], [(d, F32)], [(1, d)])
    g_w_out = _matmul("d_w_out", merged, dx1, ta=True)
    dmerged = _matmul("d_merged", dx1, full["w_out"], tb=True)

    def merge_bwd(i, zg, a, b, dm):
        _, vjp = jax.vjp(functools.partial(_merge, d=d), zg, a, b)
        return vjp(dm)

    dz_gate, du_rw, du_fox = _rows(
        "merge_bwd", merge_bwd, t, tile_s,
        [(z_gate, _row_spec(tile_s, 2 * d)), (u_rw, _row_spec(tile_s, d)), (u_fox, _row_spec(tile_s, d)),
         (dmerged, _row_spec(tile_s, d))], [(2 * d, BF16), (d, BF16), (d, BF16)])
    g_up_rw = _matmul("d_w_up_rwkv", y_rw, du_rw, ta=True)
    g_up_fox = _matmul("d_w_up_fox", y_fox, du_fox, ta=True)
    dy_rw = _matmul("d_y_rwkv", du_rw, full["w_up_rwkv"], tb=True)
    dy_fox = _matmul("d_y_fox", du_fox, full["w_up_fox"], tb=True)

    def post_bwd(i, y, r, k, v, g, ln_g, ln_b, r_k, dy):
        _, vjp = jax.vjp(functools.partial(_rw_post, cw=cw), y, r, k, v, g, ln_g, ln_b, r_k)
        return vjp(dy)

    dys, dr1, dk1, dv1, dg1, d_ln_g, d_ln_b, d_r_k = _rows(
        "rwkv_post_bwd", post_bwd, t, tile_s,
        post_rows(y_scan, r_, k_, v_, g_) + post_consts + post_rows(dy_rw), [(cw, F32)] * 5, [(1, cw)] * 3)
    dr2, dw2, dk2, dv2, dkk2, da2 = _scan_bwd(r_, w_, k_, v_, kk_, a_, states, dys)

    def pre_bwd(i, z, prev8, mu, w0, a0, wup, aup, k_k, k_a, dr_a, dr_b, dk_a, dk_b, dv_a, dv_b, dw, dkk, da, dg):
        zp = _shifted(i, z, prev8)
        _, vjp = jax.vjp(functools.partial(_rw_pre, cw=cw), z, zp, mu, w0, a0, wup, aup, k_k, k_a)
        dz, dzp, dmu, dw0, da0, dwup, daup, dk_k, dk_a = vjp((dr_a + dr_b, dw, dk_a + dk_b, dv_a + dv_b, dkk, da, dg))
        row = lax.broadcasted_iota(jnp.int32, dz.shape, 0)
        dz = dz + jnp.where(row < tile_s - 1, pltpu.roll(dzp, tile_s - 1, 0), 0.0)
        first = jnp.where(lax.broadcasted_iota(jnp.int32, (8, dz.shape[1]), 0) == 0, _row_of(dzp, 0), 0.0)
        return dz, first, dmu, dw0, da0, dwup, daup, dk_k, dk_a

    def pre_bwd_call():
        n_in = 2 + len(pre_consts) + 10
        ins = ([(z_rw, _row_spec(tile_s, rw_cols)), (z_rw, _prev_rows_spec(tile_s, rw_cols))] + pre_consts
               + post_rows(dr1, dr2, dk1, dk2, dv1, dv2, dw2, dkk2, da2, dg1))

        def body(*refs):
            i = pl.program_id(0)
            vals = pre_bwd(i, *[r[...] for r in refs[:n_in]])
            refs[n_in][...] = vals[0]
            refs[n_in + 1][...] = vals[1]
            for r, v in zip(refs[n_in + 2:], vals[2:]):
                @pl.when(i == 0)
                def _(r=r, v=v):
                    r[...] = v

                @pl.when(i > 0)
                def _(r=r, v=v):
                    r[...] += v

        acc_shapes = [(1, rw_cols), (1, cw), (1, cw), (LANES, cw), (LANES, cw), (1, cw), (1, cw)]
        return pl.pallas_call(
            body, name="rwkv_pre_bwd", grid=(n_s,), in_specs=[s for _, s in ins],
            out_specs=[_row_spec(tile_s, rw_cols), pl.BlockSpec((8, rw_cols), lambda i: (i, 0))]
            + [_full_spec(s) for s in acc_shapes],
            out_shape=[jax.ShapeDtypeStruct((t, rw_cols), F32), jax.ShapeDtypeStruct((8 * n_s, rw_cols), F32)]
            + [jax.ShapeDtypeStruct(s, F32) for s in acc_shapes],
            compiler_params=_params("arbitrary"),
        )(*[a for a, _ in ins])

    dz_main, dz_first, d_mu, d_w0, d_a0, d_wup, d_aup, d_k_k, d_k_a = pre_bwd_call()

    def add_next_row(i, dz, nxt8):
        row = lax.broadcasted_iota(jnp.int32, dz.shape, 0)
        carry = jnp.where(i < n_s - 1, _row_of(nxt8, 0), 0.0)
        return (dz + jnp.where(row == tile_s - 1, carry, 0.0),)

    (dz_rw,) = _rows("rwkv_shift_bwd", add_next_row, t, tile_s,
                     [(dz_main, _row_spec(tile_s, rw_cols)), (dz_first, _next_rows_spec(tile_s, rw_cols, n_s))],
                     [(rw_cols, BF16)])

    def fox_post_bwd(i, o, g, dy):
        _, vjp = jax.vjp(lambda oo, gg: oo * (gg * jax.nn.sigmoid(gg)), o, g)
        do, dg = vjp(dy)
        return do, _head_sum(do * o, _head_matrix(fw)), dg

    do_fox, dd_fox, dg_fox = _rows(
        "fox_post_bwd", fox_post_bwd, t, tile_s,
        [(o_fox, _row_spec(tile_s, fw)), (z_fox, _row_spec(tile_s, fw, 3)), (dy_fox, _row_spec(tile_s, fw))],
        [(fw, F32), (fw, F32), (fw, BF16)])
    dq_f, dk_f, dv_f, dcq, dck = _fox_bwd(z_fox, do_fox, c_fox, ct_fox, head_rows(lse_fox[:, ::HEAD]),
                                          head_rows(dd_fox[:, ::HEAD]), fw=fw)
    dc = head_cols(dcq) + dck.reshape(t, n_pair_f, LANES)[:, :, :2].reshape(t, fh)
    dfl, d_bf = _fox_cumsum_bwd(z_fox, b_pad, jnp.pad(dc, ((0, 0), (0, LANES - fh))), fw=fw, fh=fh)
    dz_fox = jnp.concatenate([dq_f, dk_f, dv_f, dg_fox, dfl], axis=1)

    g_w_rw = _matmul("d_w_in_rw", h, dz_rw, ta=True)
    g_w_fox = _matmul("d_w_in_fox", h, dz_fox, ta=True)
    g_w_gate = _matmul("d_w_in_gate", h, dz_gate, ta=True)
    dh = _matmul("d_h_rw", dz_rw, w_rw, tb=True)
    dh = _matmul("d_h_fox", dz_fox, w_fox, tb=True, add=dh)
    dh = _matmul("d_h_gate", dz_gate, w_gate, tb=True, add=dh)
    grad_x, d_g1 = _rows("norm1_bwd", norm_bwd, t, tile_s,
                         [(xs, _row_spec(tile_s, d)), full2(norm_g), (dh, _row_spec(tile_s, d)),
                          (dx1, _row_spec(tile_s, d))], [(d, F32)], [(1, d)])

    g_full = {
        "w_in": jnp.concatenate([g_w_rw, g_w_fox[:, :fox_cols], g_w_gate], axis=1),
        "rw_w_lora_up": d_wup[:lr], "rw_a_lora_up": d_aup[lr:], "w_up_rwkv": g_up_rw, "w_up_fox": g_up_fox,
        "w_out": g_w_out, "ple_proj": g_ple_proj, "ple_gate_w": g_ple_gate,
    }
    by_shard = [_shards_of_full(n, g_full[n]) for n in BIG]
    g_slab = jnp.stack([_pack_slab(by_shard[0][s], [by_shard[j][s] for j in range(1, len(BIG))], slab_rows)
                        for s in range(N_SHARD)])
    g_red = _unpack_slab(_reduce_scatter(g_slab), d, shard_shapes[1:])
    grads = {n: g_red[j][None] for j, n in enumerate(BIG)}

    small_parts = dict(norm_g=d_g1, rw_shift_mu=d_mu, rw_w0=d_w0, rw_a0=d_a0, rw_k_k=d_k_k, rw_k_a=d_k_a, rw_r_k=d_r_k,
                       rw_ln_g=d_ln_g, rw_ln_b=d_ln_b, fox_b_f=d_bf[:, :fh], ple_norm_g=d_g2, final_norm_g=d_gf)
    n_small = sum(-(-wts[n].size // LANES) for n in SMALL)
    small_rows = -(-n_small // 8) * 8
    small_shapes = [wts[n].shape for n in SMALL]
    g_small = _all_reduce_small(_pack_small([small_parts[n] for n in SMALL], small_rows))
    for n, g in zip(SMALL, _unpack_small(g_small, small_shapes)):
        grads[n] = g

    delta, new_m, new_v = {}, {}, {}
    for n in BIG:
        shp = wts[n].shape
        two = lambda a: a.reshape(shp[1], shp[2])
        dl, mm, vv = _adamw("adamw_" + n, two(wts[n]), two(grads[n]), two(mom[n]), two(vel[n]))
        delta[n], new_m[n], new_v[n] = dl.reshape(shp), mm.reshape(shp), vv.reshape(shp)
    dl, mm, vv = _adamw("adamw_small", _pack_small([wts[n] for n in SMALL], small_rows), g_small,
                        _pack_small([mom[n] for n in SMALL], small_rows), _pack_small([vel[n] for n in SMALL], small_rows))
    for store, packed in ((delta, dl), (new_m, mm), (new_v, vv)):
        for n, a in zip(SMALL, _unpack_small(packed, small_shapes)):
            store[n] = a

    loss = lax.psum(loss_row[0, 0], ("x", "y", "c"))
    return (loss, grad_x[None], *[grads[n] for n in WEIGHTS], *[delta[n] for n in WEIGHTS],
            *[new_m[n] for n in WEIGHTS], *[new_v[n] for n in WEIGHTS])
```

```python
import functools

import jax
import jax.numpy as jnp
from jax import lax
from jax.experimental import pallas as pl
from jax.experimental.pallas import tpu as pltpu

F32 = jnp.float32
BF16 = jnp.bfloat16
HIGHEST = lax.Precision.HIGHEST
SCAN_PREC = lax.Precision.HIGH
MESH = pl.DeviceIdType.MESH

LANES = 128
HEAD = 64
NORM_EPS = 1e-6
GN_EPS = 64e-5
ADAM_LR = 0.001
ADAM_B1 = 0.9
ADAM_B2 = 0.999
ADAM_EPS = 1e-08
ADAM_WD = 0.01
ADAM_STEP = 10
N_SHARD = 4
VMEM_LIMIT = 56 * 1024 * 1024
PAIRS_PER_STEP = 4


def _params(*sem):
    return pltpu.CompilerParams(dimension_semantics=sem, vmem_limit_bytes=VMEM_LIMIT)


def _tile(n, cands):
    for c in cands:
        if c <= n and n % c == 0:
            return c
    return n


def _div_tile(n, cap, mult):
    return max(c for c in range(mult, min(n, cap) + 1, mult) if n % c == 0)


_ROW_TILES = (512, 256, 128, 64, 32, 16, 8)


def _dot(a, b, prec=None):
    return lax.dot_general(a, b, (((1,), (0,)), ((), ())), precision=prec, preferred_element_type=F32)


def _dot_nt(a, b, prec=None):
    return lax.dot_general(a, b, (((1,), (1,)), ((), ())), precision=prec, preferred_element_type=F32)


def _dot_tn(a, b, prec=None):
    return lax.dot_general(a, b, (((0,), (0,)), ((), ())), precision=prec, preferred_element_type=F32)


@jax.custom_vjp
def _bdot(x, w):
    return _dot(x.astype(BF16), w.astype(BF16))


def _bdot_fwd(x, w):
    return _bdot(x, w), (x, w)


def _bdot_bwd(res, ct):
    x, w = res
    return _dot_nt(ct.astype(BF16), w.astype(BF16)), _dot_tn(x.astype(BF16), ct.astype(BF16))


_bdot.defvjp(_bdot_fwd, _bdot_bwd)


def _head_matrix(width):
    c = lax.broadcasted_iota(jnp.int32, (width, LANES), 0)
    h = lax.broadcasted_iota(jnp.int32, (width, LANES), 1)
    return (c // HEAD == h).astype(F32)


def _head_sum(x, e):
    return _dot_nt(_dot(x, e, SCAN_PREC), e, SCAN_PREC)


def _softplus(x):
    return jnp.maximum(x, 0.0) + jnp.log1p(jnp.exp(-jnp.abs(x)))


def _lane_pick(x, idx):
    lane = lax.broadcasted_iota(jnp.int32, x.shape, 1)
    return jnp.sum(jnp.where(lane == idx, x, 0.0), axis=1, keepdims=True)


def _matmul(name, a, b, *, ta=False, tb=False, add=None, out_dtype=F32):
    m, k = (a.shape[1], a.shape[0]) if ta else a.shape
    n = b.shape[0] if tb else b.shape[1]
    tm = _tile(m, (1024, 512, 256, 128))
    tn = _tile(n, (1408, 1024, 768, 640, 512, 384, 256, 128))
    tk = _tile(k, (1408, 1024, 768, 640, 512, 384, 256, 128, 64, 32, 16))
    nk = k // tk
    dims = (((0 if ta else 1,), (1 if tb else 0,)), ((), ()))

    def body(*refs):
        a_ref, b_ref = refs[0], refs[1]
        o_ref, acc_ref = refs[-2], refs[-1]
        kk = pl.program_id(2)

        @pl.when(kk == 0)
        def _():
            acc_ref[...] = jnp.zeros_like(acc_ref)

        acc_ref[...] += lax.dot_general(a_ref[...].astype(BF16), b_ref[...].astype(BF16), dims,
                                        preferred_element_type=F32)

        @pl.when(kk == nk - 1)
        def _():
            r = acc_ref[...]
            if add is not None:
                r = r + refs[2][...].astype(F32)
            o_ref[...] = r.astype(o_ref.dtype)

    a_spec = pl.BlockSpec((tk, tm), lambda i, j, kk: (kk, i)) if ta else pl.BlockSpec((tm, tk), lambda i, j, kk: (i, kk))
    b_spec = pl.BlockSpec((tn, tk), lambda i, j, kk: (j, kk)) if tb else pl.BlockSpec((tk, tn), lambda i, j, kk: (kk, j))
    o_spec = pl.BlockSpec((tm, tn), lambda i, j, kk: (i, j))
    ins, specs = [a, b], [a_spec, b_spec]
    if add is not None:
        ins.append(add)
        specs.append(o_spec)
    return pl.pallas_call(
        body, name=name, grid=(m // tm, n // tn, nk), in_specs=specs, out_specs=o_spec,
        out_shape=jax.ShapeDtypeStruct((m, n), out_dtype),
        scratch_shapes=[pltpu.VMEM((tm, tn), F32)],
        compiler_params=_params("parallel", "parallel", "arbitrary"),
    )(*ins)


def _rows(name, fn, n_rows, tile, ins, outs, accs=()):
    n_in, n_out = len(ins), len(outs)

    def body(*refs):
        i = pl.program_id(0)
        vals = fn(i, *[r[...] for r in refs[:n_in]])
        for r, v in zip(refs[n_in:n_in + n_out], vals[:n_out]):
            r[...] = v.astype(r.dtype)
        for r, v in zip(refs[n_in + n_out:], vals[n_out:]):
            @pl.when(i == 0)
            def _(r=r, v=v):
                r[...] = v

            @pl.when(i > 0)
            def _(r=r, v=v):
                r[...] += v

    out_specs = [pl.BlockSpec((tile, w), lambda i: (i, 0)) for w, _ in outs]
    out_specs += [pl.BlockSpec(s, lambda i: (0, 0)) for s in accs]
    out_shape = [jax.ShapeDtypeStruct((n_rows, w), d) for w, d in outs]
    out_shape += [jax.ShapeDtypeStruct(s, F32) for s in accs]
    return pl.pallas_call(
        body, name=name, grid=(n_rows // tile,), in_specs=[s for _, s in ins], out_specs=out_specs,
        out_shape=out_shape, compiler_params=_params("arbitrary"),
    )(*[a for a, _ in ins])


def _row_spec(tile, width, col=0):
    return pl.BlockSpec((tile, width), lambda i: (i, col))


def _full_spec(shape):
    return pl.BlockSpec(shape, lambda i: (0,) * len(shape))


def _prev_rows_spec(tile, width):
    return pl.BlockSpec((8, width), lambda i: (jnp.maximum(i * (tile // 8) - 1, 0), 0))


def _next_rows_spec(tile, width, n_tiles):
    return pl.BlockSpec((8, width), lambda i: (jnp.minimum(i + 1, n_tiles - 1), 0))


def _row_of(x8, idx):
    r = lax.broadcasted_iota(jnp.int32, x8.shape, 0)
    return jnp.sum(jnp.where(r == idx, x8, 0.0), axis=0, keepdims=True)


def _rms(x, g):
    return x * lax.rsqrt(jnp.mean(x * x, axis=-1, keepdims=True) + NORM_EPS) * g


def _shifted(i, z, prev8):
    first = jnp.where(i > 0, _row_of(prev8, 7), 0.0)
    row = lax.broadcasted_iota(jnp.int32, z.shape, 0)
    return jnp.where(row == 0, first, pltpu.roll(z, 1, 0))


def _rw_pre(z, zp, mu, w0, a0, wup, aup, k_k, k_a, *, cw):
    zs = z + (zp - z) * mu
    r, k, v, g = (zs[:, j * cw:(j + 1) * cw] for j in range(4))
    lo = zs[:, 4 * cw:4 * cw + LANES]
    w_raw = w0 + _bdot(jnp.tanh(lo), wup)
    decay = jnp.exp(-jnp.exp(-_softplus(-w_raw) - 0.5))
    a = jax.nn.sigmoid(a0 + _bdot(lo, aup))
    e = _head_matrix(cw)
    kk = k * k_k
    kk = kk / jnp.maximum(jnp.sqrt(_head_sum(kk * kk, e)), 1e-12)
    k_mod = k * (1.0 + (a - 1.0) * k_a)
    return r, decay, k_mod, v, kk, a, g


def _rw_post(y, r, k_mod, v, g, ln_g, ln_b, r_k, *, cw):
    e = _head_matrix(cw)
    mu = _head_sum(y, e) * (1.0 / HEAD)
    d = y - mu
    var = _head_sum(d * d, e) * (1.0 / HEAD)
    yn = d * lax.rsqrt(var + GN_EPS) * ln_g + ln_b
    bonus = _head_sum(r * k_mod * r_k, e) * v
    return (yn + bonus) * (g * jax.nn.sigmoid(g))


def _merge(zg, u_rw, u_fox, *, d):
    return jax.nn.sigmoid(zg[:, :d]) * u_rw + jax.nn.sigmoid(zg[:, d:]) * u_fox


def _head_loss(x1, ple, gl, gf, tgt):
    x2 = x1 + ple * jax.nn.sigmoid(gl)
    err = _rms(x2, gf) - tgt
    return 0.5 * jnp.sum(jnp.mean(err * err, axis=-1, keepdims=True), axis=0, keepdims=True)


def _eliminate(lo):
    n, c, _ = lo.shape
    ri = lax.broadcasted_iota(jnp.int32, (n, c, c), 1)
    ci = lax.broadcasted_iota(jnp.int32, (n, c, c), 2)
    x = (ri == ci).astype(F32)
    for s in range(c - 1):
        col = jnp.sum(jnp.where(ci == s, lo, 0.0), axis=2, keepdims=True)
        row = jnp.sum(jnp.where(ri == s, x, 0.0), axis=1, keepdims=True)
        x = x - col * row
    return x


def _batched(a, b, ca, cb):
    return lax.dot_general(a, b, (((ca,), (cb,)), ((0,), (0,))), precision=SCAN_PREC, preferred_element_type=F32)


@jax.custom_vjp
def _unit_lower_inverse(lo, known):
    return _eliminate(lo) if known is None else known


def _uli_fwd(lo, known):
    x = _unit_lower_inverse(lo, known)
    return x, (x, known)


def _uli_bwd(res, dx):
    x, known = res
    dlo = -_batched(_batched(x, dx, 1, 1), x, 2, 2)
    return dlo, (None if known is None else jnp.zeros_like(known))


_unit_lower_inverse.defvjp(_uli_fwd, _uli_bwd)


def _rwkv_chunk(s0, r, w, k, v, kk, a, *, c, tinv_known=None):
    pairs = range(len(s0))
    lane = lax.broadcasted_iota(jnp.int32, (1, LANES), 1)
    heads = (lane < HEAD, lane >= HEAD)
    ti = lax.broadcasted_iota(jnp.int32, (c, c), 0)
    si = lax.broadcasted_iota(jnp.int32, (c, c), 1)
    incl = si <= ti
    strict = si < ti
    tri = incl.astype(F32)
    logw = [jnp.log(w[p]) for p in pairs]
    cum = [_dot(tri, logw[p], SCAN_PREC) for p in pairs]
    cum_end = [jnp.sum(logw[p], axis=0, keepdims=True) for p in pairs]
    g_inv = [jnp.exp(-cum[p]) for p in pairs]
    to_end = [jnp.exp(cum_end[p] - cum[p]) for p in pairs]
    b = [kk[p] * a[p] for p in pairs]
    beta = [b[p] * g_inv[p] for p in pairs]
    kap = [kk[p] * jnp.exp(cum[p] - logw[p]) for p in pairs]
    kt = [k[p] * g_inv[p] for p in pairs]
    rt = [r[p] * jnp.exp(cum[p]) for p in pairs]
    lhs = [jnp.concatenate([jnp.where(m, x[p], 0.0) for x in (kap, rt) for m in heads], axis=0) for p in pairs]
    vs_beta = [_dot_nt(lhs[p], beta[p], SCAN_PREC) for p in pairs]
    vs_kt = [_dot_nt(lhs[p], kt[p], SCAN_PREC) for p in pairs]
    strict2 = jnp.concatenate([strict, strict], axis=0)
    incl2 = jnp.concatenate([incl, incl], axis=0)
    lo = [jnp.where(strict2, vs_beta[p][:2 * c], 0.0) for p in pairs]
    mm = [jnp.where(strict2, vs_kt[p][:2 * c], 0.0) for p in pairs]
    arb = [jnp.where(incl2, vs_beta[p][2 * c:], 0.0) for p in pairs]
    ark = [jnp.where(incl2, vs_kt[p][2 * c:], 0.0) for p in pairs]
    per_head = lambda xs: jnp.concatenate([xs[p][h * c:(h + 1) * c][None] for p in pairs for h in (0, 1)])
    tinv = _unit_lower_inverse(per_head(lo), None if tinv_known is None else per_head(tinv_known))
    tinv = [jnp.concatenate([tinv[2 * p], tinv[2 * p + 1]], axis=0) for p in pairs]
    both = lambda x: jnp.where(heads[0], x[:c], x[c:])
    vs_s = [_dot_nt(jnp.concatenate([kap[p], rt[p]], axis=0), s0[p], SCAN_PREC) for p in pairs]
    rhs = [vs_s[p][:c] + both(_dot(mm[p], v[p], SCAN_PREC)) for p in pairs]
    u = [-both(_dot(tinv[p], rhs[p], SCAN_PREC)) for p in pairs]
    y = [vs_s[p][c:] + both(_dot(arb[p], u[p], SCAN_PREC) + _dot(ark[p], v[p], SCAN_PREC)) for p in pairs]
    rr = lax.broadcasted_iota(jnp.int32, (LANES, LANES), 0) < HEAD
    cc = lax.broadcasted_iota(jnp.int32, (LANES, LANES), 1) < HEAD
    ds = [_dot_tn(jnp.concatenate([u[p], v[p]], axis=0),
                  jnp.concatenate([b[p] * to_end[p], k[p] * to_end[p]], axis=0), SCAN_PREC) for p in pairs]
    s1 = [s0[p] * jnp.exp(cum_end[p]) + jnp.where(rr == cc, ds[p], 0.0) for p in pairs]
    return tuple(y), tuple(s1), tuple(tinv)


def _scan_tiles(t, n_pair):
    return _tile(t, (32, 16, 8)), _tile(t, (512, 256, 128, 64, 32)), _tile(n_pair, (PAIRS_PER_STEP, 2, 1))


def _scan_fwd(r, w, k, v, kk, a):
    t, width = r.shape
    c, tb, npb = _scan_tiles(t, width // LANES)
    n_grp, n_blk, n_cb = width // (LANES * npb), t // tb, tb // c

    def body(r_ref, w_ref, k_ref, v_ref, kk_ref, a_ref, y_ref, st_ref, ti_ref, s_scr):
        @pl.when(pl.program_id(1) == 0)
        def _():
            s_scr[...] = jnp.zeros_like(s_scr)

        def chunk(j, carry):
            sl = pl.ds(pl.multiple_of(j * c, c), c)
            lanes = [pl.ds(q * LANES, LANES) for q in range(npb)]
            s0 = tuple(s_scr[q] for q in range(npb))
            cols = lambda ref: tuple(ref[sl, ln] for ln in lanes)
            y, s1, tinv = _rwkv_chunk(s0, cols(r_ref), cols(w_ref), cols(k_ref), cols(v_ref), cols(kk_ref), cols(a_ref),
                                      c=c)
            for q, ln in enumerate(lanes):
                st_ref[q, j] = s0[q]
                ti_ref[q, j] = tinv[q]
                y_ref[sl, ln] = y[q]
                s_scr[q] = s1[q]
            return carry

        lax.fori_loop(0, n_cb, chunk, 0)

    blk = pl.BlockSpec((tb, npb * LANES), lambda p, i: (i, p))
    return pl.pallas_call(
        body, name="rwkv_scan_fwd", grid=(n_grp, n_blk), in_specs=[blk] * 6,
        out_specs=[blk, pl.BlockSpec((npb, n_cb, LANES, LANES), lambda p, i: (p, i, 0, 0)),
                   pl.BlockSpec((npb, n_cb, 2 * c, c), lambda p, i: (p, i, 0, 0))],
        out_shape=[jax.ShapeDtypeStruct((t, width), F32),
                   jax.ShapeDtypeStruct((width // LANES, t // c, LANES, LANES), F32),
                   jax.ShapeDtypeStruct((width // LANES, t // c, 2 * c, c), F32)],
        scratch_shapes=[pltpu.VMEM((npb, LANES, LANES), F32)],
        compiler_params=_params("arbitrary", "arbitrary"),
    )(r, w, k, v, kk, a)


def _scan_bwd(r, w, k, v, kk, a, st, ti, dy):
    t, width = r.shape
    c, tb, npb = _scan_tiles(t, width // LANES)
    n_grp, n_blk, n_cb = width // (LANES * npb), t // tb, tb // c

    def body(r_ref, w_ref, k_ref, v_ref, kk_ref, a_ref, st_ref, ti_ref, dy_ref,
             dr_ref, dw_ref, dk_ref, dv_ref, dkk_ref, da_ref, ds_scr):
        @pl.when(pl.program_id(1) == 0)
        def _():
            ds_scr[...] = jnp.zeros_like(ds_scr)

        def chunk(jj, carry):
            j = n_cb - 1 - jj
            sl = pl.ds(pl.multiple_of(j * c, c), c)
            lanes = [pl.ds(q * LANES, LANES) for q in range(npb)]
            cols = lambda ref: tuple(ref[sl, ln] for ln in lanes)
            args = (tuple(st_ref[q, j] for q in range(npb)), cols(r_ref), cols(w_ref), cols(k_ref), cols(v_ref),
                    cols(kk_ref), cols(a_ref))
            known = tuple(ti_ref[q, j] for q in range(npb))
            _, vjp = jax.vjp(lambda *xs: _rwkv_chunk(*xs, c=c, tinv_known=known)[:2], *args)
            grads = vjp((cols(dy_ref), tuple(ds_scr[q] for q in range(npb))))
            for q, ln in enumerate(lanes):
                ds_scr[q] = grads[0][q]
                for ref, g in zip((dr_ref, dw_ref, dk_ref, dv_ref, dkk_ref, da_ref), grads[1:]):
                    ref[sl, ln] = g[q]
            return carry

        lax.fori_loop(0, n_cb, chunk, 0)

    blk = pl.BlockSpec((tb, npb * LANES), lambda p, i: (n_blk - 1 - i, p))
    stb = pl.BlockSpec((npb, n_cb, LANES, LANES), lambda p, i: (p, n_blk - 1 - i, 0, 0))
    tib = pl.BlockSpec((npb, n_cb, 2 * c, c), lambda p, i: (p, n_blk - 1 - i, 0, 0))
    return pl.pallas_call(
        body, name="rwkv_scan_bwd", grid=(n_grp, n_blk), in_specs=[blk] * 6 + [stb, tib, blk], out_specs=[blk] * 6,
        out_shape=[jax.ShapeDtypeStruct((t, width), F32)] * 6,
        scratch_shapes=[pltpu.VMEM((npb, LANES, LANES), F32)],
        compiler_params=_params("arbitrary", "arbitrary"),
    )(r, w, k, v, kk, a, st, ti, dy)


NEG = -1e30


def _fox_cumsum(zf, b_pad, *, fw, fh):
    t = zf.shape[0]
    tile = _tile(t, (256, 128, 64, 32, 16, 8))

    def body(fl_ref, b_ref, c_ref, carry):
        @pl.when(pl.program_id(0) == 0)
        def _():
            carry[...] = jnp.zeros_like(carry)

        lane = lax.broadcasted_iota(jnp.int32, (tile, LANES), 1)
        logf = jnp.where(lane < fh, -_softplus(-(fl_ref[...] + b_ref[...])), 0.0)
        ri = lax.broadcasted_iota(jnp.int32, (tile, tile), 0)
        ci = lax.broadcasted_iota(jnp.int32, (tile, tile), 1)
        c_ref[...] = carry[...] + _dot((ci <= ri).astype(F32), logf, HIGHEST)
        carry[...] += jnp.sum(logf, axis=0, keepdims=True)

    return pl.pallas_call(
        body, name="fox_cumsum", grid=(t // tile,),
        in_specs=[_row_spec(tile, LANES, 4 * fw // LANES), _full_spec((1, LANES))],
        out_specs=_row_spec(tile, LANES), out_shape=jax.ShapeDtypeStruct((t, LANES), F32),
        scratch_shapes=[pltpu.VMEM((1, LANES), F32)], compiler_params=_params("arbitrary"),
    )(zf, b_pad)


def _fox_cumsum_bwd(zf, b_pad, dc, *, fw, fh):
    t = zf.shape[0]
    tile = _tile(t, (256, 128, 64, 32, 16, 8))
    n = t // tile

    def body(fl_ref, b_ref, dc_ref, dfl_ref, db_ref, carry):
        i = pl.program_id(0)

        @pl.when(i == 0)
        def _():
            carry[...] = jnp.zeros_like(carry)
            db_ref[...] = jnp.zeros_like(db_ref)

        lane = lax.broadcasted_iota(jnp.int32, (tile, LANES), 1)
        dc_t = dc_ref[...]
        ri = lax.broadcasted_iota(jnp.int32, (tile, tile), 0)
        ci = lax.broadcasted_iota(jnp.int32, (tile, tile), 1)
        dlogf = carry[...] + _dot((ci >= ri).astype(F32), dc_t, HIGHEST)
        carry[...] += jnp.sum(dc_t, axis=0, keepdims=True)
        dfl = jnp.where(lane < fh, dlogf * jax.nn.sigmoid(-(fl_ref[...] + b_ref[...])), 0.0)
        dfl_ref[...] = dfl.astype(dfl_ref.dtype)
        db_ref[...] += jnp.sum(dfl, axis=0, keepdims=True)

    rev = lambda col: pl.BlockSpec((tile, LANES), lambda i: (n - 1 - i, col))
    return pl.pallas_call(
        body, name="fox_cumsum_bwd", grid=(n,),
        in_specs=[rev(4 * fw // LANES), _full_spec((1, LANES)), rev(0)],
        out_specs=[rev(0), _full_spec((1, LANES))],
        out_shape=[jax.ShapeDtypeStruct((t, LANES), BF16), jax.ShapeDtypeStruct((1, LANES), F32)],
        scratch_shapes=[pltpu.VMEM((1, LANES), F32)], compiler_params=_params("arbitrary"),
    )(zf, b_pad, dc)


def _fox_tile(t):
    return _tile(t, (512, 256, 128))


def _fox_fwd(zf, c, ct, *, fw):
    t = zf.shape[0]
    tq = _fox_tile(t)
    th = tq // 2
    n_pair, n_q = fw // LANES, t // tq
    scale = HEAD ** -0.5
    chains = [(h, qq) for h in (0, 1) for qq in (0, 1)]

    def body(q_ref, k_ref, v_ref, g_ref, c_ref, ct_ref, o_ref, lse_ref, y_ref):
        hp, i = pl.program_id(0), pl.program_id(1)
        lane = lax.broadcasted_iota(jnp.int32, (1, LANES), 1)
        in_head = (lane < HEAD, lane >= HEAD)
        rows = [pl.ds(qq * th, th) for qq in (0, 1)]
        qh = [jnp.where(in_head[h], q_ref[rows[qq], :] * scale, 0.0).astype(BF16) for h, qq in chains]
        cq = [_lane_pick(c_ref[rows[qq], :], 2 * hp + h) for h, qq in chains]
        qidx = lax.broadcasted_iota(jnp.int32, (th, tq), 0)
        kidx = lax.broadcasted_iota(jnp.int32, (th, tq), 1)

        def kv_step(j, carry, diagonal):
            m, l, acc = carry
            ks = pl.ds(pl.multiple_of(j * tq, tq), tq)
            kb = k_ref[ks, :].astype(BF16)
            vb = v_ref[ks, :]
            vh = [jnp.where(in_head[h], vb, 0.0).astype(BF16) for h in (0, 1)]
            ck = [ct_ref[0, j, pl.ds(h, 1), :] for h in (0, 1)]
            s = [_dot_nt(qh[n], kb) + cq[n] - ck[h] for n, (h, qq) in enumerate(chains)]
            if diagonal:
                s = [jnp.where(qq * th + qidx >= kidx, s[n], NEG) for n, (h, qq) in enumerate(chains)]
            m_new = [jnp.maximum(m[n], jnp.max(s[n], axis=1, keepdims=True)) for n in range(4)]
            p = [jnp.exp(s[n] - m_new[n]) for n in range(4)]
            alpha = [jnp.exp(m[n] - m_new[n]) for n in range(4)]
            l = [l[n] * alpha[n] + jnp.sum(p[n], axis=1, keepdims=True) for n in range(4)]
            pv = [_dot(p[n].astype(BF16), vh[h]) for n, (h, qq) in enumerate(chains)]
            acc = [acc[qq] * jnp.where(in_head[0], alpha[qq], alpha[2 + qq]) + pv[qq] + pv[2 + qq] for qq in (0, 1)]
            return tuple(m_new), tuple(l), tuple(acc)

        init = (tuple(jnp.full((th, 1), NEG, F32) for _ in chains), tuple(jnp.zeros((th, 1), F32) for _ in chains),
                tuple(jnp.zeros((th, LANES), F32) for _ in (0, 1)))
        carry = lax.fori_loop(0, i, functools.partial(kv_step, diagonal=False), init)
        m, l, acc = kv_step(i, carry, True)
        for qq in (0, 1):
            o = acc[qq] / jnp.where(in_head[0], l[qq], l[2 + qq])
            g = g_ref[rows[qq], :]
            o_ref[rows[qq], :] = o
            lse_ref[rows[qq], :] = jnp.where(in_head[0], m[qq] + jnp.log(l[qq]), m[2 + qq] + jnp.log(l[2 + qq]))
            y_ref[rows[qq], :] = (o * (g * jax.nn.sigmoid(g))).astype(y_ref.dtype)

    npw = fw // LANES
    blk = lambda col0: pl.BlockSpec((tq, LANES), lambda hp, i: (i, col0 + hp))
    res = lambda col0: pl.BlockSpec((t, LANES), lambda hp, i: (0, col0 + hp))
    out_blk = pl.BlockSpec((tq, LANES), lambda hp, i: (i, hp))
    return pl.pallas_call(
        body, name="fox_attn_fwd", grid=(n_pair, n_q),
        in_specs=[blk(0), res(npw), res(2 * npw), blk(3 * npw),
                  pl.BlockSpec((tq, LANES), lambda hp, i: (i, 0)),
                  pl.BlockSpec((1, n_q, 2, tq), lambda hp, i: (hp, 0, 0, 0))],
        out_specs=[out_blk, out_blk, out_blk],
        out_shape=[jax.ShapeDtypeStruct((t, fw), F32), jax.ShapeDtypeStruct((t, fw), F32),
                   jax.ShapeDtypeStruct((t, fw), BF16)],
        compiler_params=_params("arbitrary", "arbitrary"),
    )(zf, zf, zf, zf, c, ct)


def _fox_bwd(zf, do, c, ct, lse_r, dd_r, *, fw):
    t = zf.shape[0]
    tq = _fox_tile(t)
    n_pair, n_q = fw // LANES, t // tq
    scale = HEAD ** -0.5

    def body(q_ref, k_ref, v_ref, do_ref, c_ref, ct_ref, lse_ref, dd_ref,
             dq_ref, dk_ref, dv_ref, dcq_ref, dck_ref, dq_acc, dcq_acc):
        hp, j = pl.program_id(0), pl.program_id(1)

        @pl.when(j == 0)
        def _():
            dq_acc[...] = jnp.zeros_like(dq_acc)
            dcq_acc[...] = jnp.zeros_like(dcq_acc)

        lane = lax.broadcasted_iota(jnp.int32, (1, LANES), 1)
        in_head = (lane < HEAD, lane >= HEAD)
        kb = k_ref[...]
        kh = [jnp.where(m, kb, 0.0).astype(BF16) for m in in_head]
        vb = v_ref[...].astype(BF16)
        c_k = c_ref[...]
        ck = [_lane_pick(c_k, 2 * hp + h) for h in (0, 1)]
        kidx = lax.broadcasted_iota(jnp.int32, (tq, tq), 0)
        qidx = lax.broadcasted_iota(jnp.int32, (tq, tq), 1)

        def q_step(i, carry, diagonal):
            dk, dv, dck = carry
            qs = pl.ds(pl.multiple_of(i * tq, tq), tq)
            qf = q_ref[qs, :] * scale
            dof = do_ref[qs, :]
            qh = [jnp.where(m, qf, 0.0).astype(BF16) for m in in_head]
            doh = [jnp.where(m, dof, 0.0).astype(BF16) for m in in_head]
            row = lambda ref, h: ref[0, i, pl.ds(h, 1), :]
            st = [_dot_nt(kh[h], qh[h]) + row(ct_ref, h) - ck[h] for h in (0, 1)]
            p = [jnp.exp(st[h] - row(lse_ref, h)) for h in (0, 1)]
            if diagonal:
                p = [jnp.where(kidx <= qidx, p[h], 0.0) for h in (0, 1)]
            dst = [p[h] * (_dot_nt(vb, doh[h]) - row(dd_ref, h)) for h in (0, 1)]
            p16 = [x.astype(BF16) for x in p]
            ds16 = [x.astype(BF16) for x in dst]
            dv = dv + _dot(p16[0], doh[0]) + _dot(p16[1], doh[1])
            dk = dk + _dot(ds16[0], qh[0]) + _dot(ds16[1], qh[1])
            dq_acc[qs, :] += _dot_tn(ds16[0], kh[0]) + _dot_tn(ds16[1], kh[1])
            for h in (0, 1):
                dcq_acc[i, pl.ds(h, 1), :] += jnp.sum(dst[h], axis=0, keepdims=True)
            dck = tuple(dck[h] - jnp.sum(dst[h], axis=1, keepdims=True) for h in (0, 1))
            return dk, dv, dck

        zero = jnp.zeros((tq, LANES), F32)
        carry = q_step(j, (zero, zero, (jnp.zeros((tq, 1), F32),) * 2), True)
        dk, dv, dck = lax.fori_loop(j + 1, n_q, functools.partial(q_step, diagonal=False), carry)
        dk_ref[...] = dk.astype(dk_ref.dtype)
        dv_ref[...] = dv.astype(dv_ref.dtype)
        dck_ref[...] = jnp.where(lane == 0, dck[0], jnp.where(lane == 1, dck[1], 0.0))

        @pl.when(j == n_q - 1)
        def _():
            dq_ref[...] = (dq_acc[...] * scale).astype(dq_ref.dtype)
            dcq_ref[0] = dcq_acc[...]

    npw = fw // LANES
    res_z = lambda col0: pl.BlockSpec((t, LANES), lambda hp, j: (0, col0 + hp))
    blk_z = lambda col0: pl.BlockSpec((tq, LANES), lambda hp, j: (j, col0 + hp))
    res = pl.BlockSpec((t, LANES), lambda hp, j: (0, hp))
    blk = pl.BlockSpec((tq, LANES), lambda hp, j: (j, hp))
    rows = pl.BlockSpec((1, n_q, 2, tq), lambda hp, j: (hp, 0, 0, 0))
    return pl.pallas_call(
        body, name="fox_attn_bwd", grid=(n_pair, n_q),
        in_specs=[res_z(0), blk_z(npw), blk_z(2 * npw), res, pl.BlockSpec((tq, LANES), lambda hp, j: (j, 0)),
                  rows, rows, rows],
        out_specs=[res, blk, blk, rows, blk],
        out_shape=[jax.ShapeDtypeStruct((t, fw), BF16), jax.ShapeDtypeStruct((t, fw), BF16),
                   jax.ShapeDtypeStruct((t, fw), BF16), jax.ShapeDtypeStruct((n_pair, n_q, 2, tq), F32),
                   jax.ShapeDtypeStruct((t, fw), F32)],
        scratch_shapes=[pltpu.VMEM((t, LANES), F32), pltpu.VMEM((n_q, 2, tq), F32)],
        compiler_params=_params("arbitrary", "arbitrary"),
    )(zf, zf, zf, do, c, ct, lse_r, dd_r)


def _adamw_math(w, g, m, v):
    m = ADAM_B1 * m + (1.0 - ADAM_B1) * g
    v = ADAM_B2 * v + (1.0 - ADAM_B2) * jnp.square(g)
    m_hat = m / (1.0 - ADAM_B1 ** ADAM_STEP)
    v_hat = v / (1.0 - ADAM_B2 ** ADAM_STEP)
    delta = -ADAM_LR * (m_hat / (jnp.sqrt(v_hat) + ADAM_EPS) + ADAM_WD * w)
    return delta, m, v


def _adamw(name, w, g, m, v):
    _, rows, cols = w.shape
    tile = _tile(rows, (128, 64, 32, 16, 8))
    spec = pl.BlockSpec((1, tile, cols), lambda i: (0, i, 0))

    def body(w_ref, g_ref, m_ref, v_ref, d_ref, mo_ref, vo_ref):
        d_ref[...], mo_ref[...], vo_ref[...] = _adamw_math(w_ref[...], g_ref[...], m_ref[...], v_ref[...])

    return pl.pallas_call(
        body, name=name, grid=(rows // tile,), in_specs=[spec] * 4, out_specs=[spec] * 3,
        out_shape=[jax.ShapeDtypeStruct(w.shape, F32)] * 3, compiler_params=_params("parallel"),
    )(w, g, m, v)


def _place():
    return lax.axis_index("x"), lax.axis_index("y"), lax.axis_index("c")


def _other_chips(x, y):
    return [(1 - x, y), (x, 1 - y), (1 - x, 1 - y)]


HBM_SPEC = pl.BlockSpec(memory_space=pltpu.HBM)


def _all_gather_shards(slab):
    rows, width = slab.shape
    rh = rows // 2

    def body(src_ref, out_ref, send_sems, recv_sems):
        x, y, c = _place()
        me = 2 * x + y
        sibling = (x, y, 1 - c)
        chips = _other_chips(x, y)

        def part(chip, half):
            return out_ref.at[chip, pl.ds(half * rh, rh), :]

        def copy(k, src, dst, to):
            return pltpu.make_async_remote_copy(src_ref=src, dst_ref=dst, send_sem=send_sems.at[k],
                                                recv_sem=recv_sems.at[k], device_id=to, device_id_type=MESH)

        first = [copy(j, src_ref.at[pl.ds(c * rh, rh), :], part(me, c), (px, py, c)) for j, (px, py) in enumerate(chips)]
        for cp in first:
            cp.start()
        passed = [copy(3 + j, part(2 * px + py, c), part(2 * px + py, c), sibling) for j, (px, py) in enumerate(chips)]
        for j, (px, py) in enumerate(chips):
            copy(j, part(2 * px + py, c), part(2 * px + py, c), sibling).wait_recv()
            passed[j].start()
        for j, (px, py) in enumerate(chips):
            copy(3 + j, part(2 * px + py, 1 - c), part(2 * px + py, 1 - c), sibling).wait_recv()
        for cp in first + passed:
            cp.wait_send()

    return pl.pallas_call(
        body, name="weights_all_gather", in_specs=[HBM_SPEC], out_specs=HBM_SPEC,
        out_shape=jax.ShapeDtypeStruct((N_SHARD, rows, width), slab.dtype),
        scratch_shapes=[pltpu.SemaphoreType.DMA((6,)), pltpu.SemaphoreType.DMA((6,))],
    )(slab)


def _chip_index():
    return jnp.reshape(2 * lax.axis_index("x") + lax.axis_index("y"), (1,)).astype(jnp.int32)


def _place_own_shard(gathered, slab):
    rows, width = slab.shape
    tile = _div_tile(rows, 256, 16)

    def body(me_ref, s_ref, g_ref, o_ref):
        o_ref[0] = s_ref[...]

    return pl.pallas_call(
        body, name="weights_place_own",
        grid_spec=pltpu.PrefetchScalarGridSpec(
            num_scalar_prefetch=1, grid=(rows // tile,),
            in_specs=[pl.BlockSpec((tile, width), lambda i, me: (i, 0)), pl.BlockSpec(memory_space=pl.ANY)],
            out_specs=pl.BlockSpec((1, tile, width), lambda i, me: (me[0], i, 0))),
        out_shape=jax.ShapeDtypeStruct(gathered.shape, gathered.dtype), input_output_aliases={2: 0},
        compiler_params=_params("parallel"),
    )(_chip_index(), slab, gathered)


def _sibling_exchange(g):
    _, _, rh, width = g.shape

    def body(g_ref, out_ref, send_sems, recv_sems):
        x, y, c = _place()
        copies = [pltpu.make_async_remote_copy(
            src_ref=g_ref.at[s, 1 - c], dst_ref=out_ref.at[s], send_sem=send_sems.at[s],
            recv_sem=recv_sems.at[s], device_id=(x, y, 1 - c), device_id_type=MESH) for s in range(N_SHARD)]
        for cp in copies:
            cp.start()
        for cp in copies:
            cp.wait()

    return pl.pallas_call(
        body, name="grad_sibling_exchange", in_specs=[HBM_SPEC], out_specs=HBM_SPEC,
        out_shape=jax.ShapeDtypeStruct((N_SHARD, rh, width), g.dtype),
        scratch_shapes=[pltpu.SemaphoreType.DMA((N_SHARD,)), pltpu.SemaphoreType.DMA((N_SHARD,))],
    )(g)


def _add_sibling(g, got):
    _, _, rh, width = g.shape
    tile = _div_tile(rh, 256, 16)
    c_arr = jnp.reshape(lax.axis_index("c"), (1,)).astype(jnp.int32)

    def body(c_ref, a_ref, b_ref, o_ref):
        o_ref[...] = (a_ref[0] + b_ref[...]).astype(o_ref.dtype)

    return pl.pallas_call(
        body, name="grad_add_sibling",
        grid_spec=pltpu.PrefetchScalarGridSpec(
            num_scalar_prefetch=1, grid=(N_SHARD, rh // tile),
            in_specs=[pl.BlockSpec((1, 1, tile, width), lambda s, i, c: (s, c[0], i, 0)),
                      pl.BlockSpec((1, tile, width), lambda s, i, c: (s, i, 0))],
            out_specs=pl.BlockSpec((1, tile, width), lambda s, i, c: (s, i, 0))),
        out_shape=jax.ShapeDtypeStruct((N_SHARD, rh, width), BF16),
        compiler_params=_params("parallel", "parallel"),
    )(c_arr, g, got)


def _chip_exchange(p):
    def body(p_ref, out_ref, send_sems, recv_sems):
        x, y, c = _place()
        me = 2 * x + y
        copies = [pltpu.make_async_remote_copy(
            src_ref=p_ref.at[2 * px + py], dst_ref=out_ref.at[me], send_sem=send_sems.at[j],
            recv_sem=recv_sems.at[j], device_id=(px, py, c), device_id_type=MESH)
            for j, (px, py) in enumerate(_other_chips(x, y))]
        for cp in copies:
            cp.start()
        for cp in copies:
            cp.wait()

    return pl.pallas_call(
        body, name="grad_chip_exchange", in_specs=[HBM_SPEC], out_specs=HBM_SPEC,
        out_shape=jax.ShapeDtypeStruct(p.shape, p.dtype),
        scratch_shapes=[pltpu.SemaphoreType.DMA((3,)), pltpu.SemaphoreType.DMA((3,))],
    )(p)


def _sum_chips(p, got):
    _, rh, width = p.shape
    tile = _div_tile(rh, 256, 16)
    n_t = rh // tile
    place = jnp.stack([2 * lax.axis_index("x") + lax.axis_index("y"), lax.axis_index("c")]).astype(jnp.int32)

    def body(pl_ref, own_ref, r0, r1, r2, r3, o_ref):
        me = pl_ref[0]
        own = own_ref[0].astype(F32)
        t = [jnp.where(me == s, own, r[0].astype(F32)) for s, r in enumerate((r0, r1, r2, r3))]
        o_ref[...] = ((t[0] + t[1]) + t[2]) + t[3]

    def slot(s):
        return pl.BlockSpec((1, tile, width), lambda i, pc: (jnp.where(pc[0] == s, (s + 1) % N_SHARD, s), i, 0))

    return pl.pallas_call(
        body, name="grad_sum_chips",
        grid_spec=pltpu.PrefetchScalarGridSpec(
            num_scalar_prefetch=1, grid=(n_t,),
            in_specs=[pl.BlockSpec((1, tile, width), lambda i, pc: (pc[0], i, 0))] + [slot(s) for s in range(N_SHARD)],
            out_specs=pl.BlockSpec((tile, width), lambda i, pc: (pc[1] * n_t + i, 0))),
        out_shape=jax.ShapeDtypeStruct((2 * rh, width), F32), compiler_params=_params("parallel"),
    )(place, p, got, got, got, got)


def _join_halves(full):
    rows, width = full.shape
    rh = rows // 2

    def body(f_ref, out_ref, send_sem, recv_sem):
        x, y, c = _place()
        cp = pltpu.make_async_remote_copy(
            src_ref=f_ref.at[pl.ds(c * rh, rh), :], dst_ref=out_ref.at[pl.ds(c * rh, rh), :], send_sem=send_sem,
            recv_sem=recv_sem, device_id=(x, y, 1 - c), device_id_type=MESH)
        cp.start()
        cp.wait_send()
        pltpu.make_async_remote_copy(
            src_ref=f_ref.at[pl.ds((1 - c) * rh, rh), :], dst_ref=out_ref.at[pl.ds((1 - c) * rh, rh), :],
            send_sem=send_sem, recv_sem=recv_sem, device_id=(x, y, 1 - c), device_id_type=MESH).wait_recv()

    return pl.pallas_call(
        body, name="grad_join_halves", in_specs=[HBM_SPEC], out_specs=HBM_SPEC,
        out_shape=jax.ShapeDtypeStruct(full.shape, full.dtype), input_output_aliases={0: 0},
        scratch_shapes=[pltpu.SemaphoreType.DMA, pltpu.SemaphoreType.DMA],
    )(full)


def _reduce_scatter(g):
    g = g.reshape(N_SHARD, 2, g.shape[1] // 2, g.shape[2])
    chip_sum = _add_sibling(g, _sibling_exchange(g))
    return _join_halves(_sum_chips(chip_sum, _chip_exchange(chip_sum)))


def _all_reduce_small(v):
    rows = v.shape[0]

    def body(v_ref, out_ref, gather, send_sems, recv_sems):
        x, y, c = _place()
        gather[4 * x + 2 * y + c] = v_ref[...]
        flips = [(dx, dy, dc) for dx in (0, 1) for dy in (0, 1) for dc in (0, 1)][1:]
        peers = [((x + dx) % 2, (y + dy) % 2, (c + dc) % 2) for dx, dy, dc in flips]
        copies = [pltpu.make_async_remote_copy(
            src_ref=v_ref, dst_ref=gather.at[4 * x + 2 * y + c], send_sem=send_sems.at[j], recv_sem=recv_sems.at[j],
            device_id=peer, device_id_type=MESH) for j, peer in enumerate(peers)]
        for cp in copies:
            cp.start()
        for j, (px, py, pc) in enumerate(peers):
            pltpu.make_async_remote_copy(
                src_ref=v_ref, dst_ref=gather.at[4 * px + 2 * py + pc], send_sem=send_sems.at[j],
                recv_sem=recv_sems.at[j], device_id=(px, py, pc), device_id_type=MESH).wait_recv()
        for cp in copies:
            cp.wait_send()
        acc = gather[0]
        for d in range(1, 8):
            acc = acc + gather[d]
        out_ref[...] = acc

    vm = pl.BlockSpec(memory_space=pltpu.VMEM)
    return pl.pallas_call(
        body, name="small_grads_all_reduce", in_specs=[vm], out_specs=vm,
        out_shape=jax.ShapeDtypeStruct(v.shape, F32),
        scratch_shapes=[pltpu.VMEM((8, rows, LANES), F32), pltpu.SemaphoreType.DMA((7,)), pltpu.SemaphoreType.DMA((7,))],
    )(v)


def _pack_slab(first, rest, rows):
    width = first.shape[1]
    flat = jnp.concatenate([p.reshape(-1) for p in rest])
    below = jnp.pad(flat, (0, (rows - first.shape[0]) * width - flat.shape[0])).reshape(-1, width)
    return jnp.concatenate([first, below], axis=0)


def _unpack_slab(slab, first_rows, shapes):
    flat = slab[first_rows:].reshape(-1)
    out, off = [slab[:first_rows]], 0
    for s in shapes:
        n = s[0] * s[1]
        out.append(flat[off:off + n].reshape(s))
        off += n
    return out


def _pad_lanes(v):
    v = v.reshape(1, -1)
    return jnp.pad(v, ((0, 0), (0, -v.shape[1] % LANES)))


def _pack_small(vs, rows):
    flat = jnp.concatenate([_pad_lanes(v) for v in vs], axis=1)
    return jnp.pad(flat, ((0, 0), (0, rows * LANES - flat.shape[1]))).reshape(rows, LANES)


def _unpack_small(packed, shapes):
    flat = packed.reshape(-1)
    out, off = [], 0
    for s in shapes:
        n = 1
        for d in s:
            n *= d
        out.append(flat[off:off + n].reshape(s))
        off += n + (-n % LANES)
    return out


BIG = ("w_in", "rw_w_lora_up", "rw_a_lora_up", "w_up_rwkv", "w_up_fox", "w_out", "ple_proj", "ple_gate_w")
ROW_SHARDED = ("w_out", "ple_gate_w")
SMALL = ("norm_g", "rw_shift_mu", "rw_w0", "rw_a0", "rw_k_k", "rw_k_a", "rw_r_k", "rw_ln_g", "rw_ln_b", "fox_b_f",
         "ple_norm_g", "final_norm_g")
WEIGHTS = ("norm_g", "w_in", "rw_shift_mu", "rw_w0", "rw_w_lora_up", "rw_a0", "rw_a_lora_up", "rw_k_k", "rw_k_a",
           "rw_r_k", "rw_ln_g", "rw_ln_b", "fox_b_f", "w_up_rwkv", "w_up_fox", "w_out", "ple_proj", "ple_gate_w",
           "ple_norm_g", "final_norm_g")


def _full_from_shards(name, gathered):
    axis = 0 if name in ROW_SHARDED else 1
    return jnp.concatenate([gathered[s] for s in range(N_SHARD)], axis=axis)


def _shards_of_full(name, full):
    axis = 0 if name in ROW_SHARDED else 1
    return jnp.split(full, N_SHARD, axis=axis)


def kernel(x, p, norm_g, w_in, rw_shift_mu, rw_w0, rw_w_lora_up, rw_a0, rw_a_lora_up, rw_k_k, rw_k_a, rw_r_k, rw_ln_g, rw_ln_b, fox_b_f, w_up_rwkv, w_up_fox, w_out, ple_proj, ple_gate_w, ple_norm_g, final_norm_g, loss_target, m_norm_g, m_w_in, m_rw_shift_mu, m_rw_w0, m_rw_w_lora_up, m_rw_a0, m_rw_a_lora_up, m_rw_k_k, m_rw_k_a, m_rw_r_k, m_rw_ln_g, m_rw_ln_b, m_fox_b_f, m_w_up_rwkv, m_w_up_fox, m_w_out, m_ple_proj, m_ple_gate_w, m_ple_norm_g, m_final_norm_g, v_norm_g, v_w_in, v_rw_shift_mu, v_rw_w0, v_rw_w_lora_up, v_rw_a0, v_rw_a_lora_up, v_rw_k_k, v_rw_k_a, v_rw_r_k, v_rw_ln_g, v_rw_ln_b, v_fox_b_f, v_w_up_rwkv, v_w_up_fox, v_w_out, v_ple_proj, v_ple_gate_w, v_ple_norm_g, v_final_norm_g):
    wts = dict(norm_g=norm_g, w_in=w_in, rw_shift_mu=rw_shift_mu, rw_w0=rw_w0, rw_w_lora_up=rw_w_lora_up, rw_a0=rw_a0,
               rw_a_lora_up=rw_a_lora_up, rw_k_k=rw_k_k, rw_k_a=rw_k_a, rw_r_k=rw_r_k, rw_ln_g=rw_ln_g, rw_ln_b=rw_ln_b,
               fox_b_f=fox_b_f, w_up_rwkv=w_up_rwkv, w_up_fox=w_up_fox, w_out=w_out, ple_proj=ple_proj,
               ple_gate_w=ple_gate_w, ple_norm_g=ple_norm_g, final_norm_g=final_norm_g)
    mom = dict(norm_g=m_norm_g, w_in=m_w_in, rw_shift_mu=m_rw_shift_mu, rw_w0=m_rw_w0, rw_w_lora_up=m_rw_w_lora_up,
               rw_a0=m_rw_a0, rw_a_lora_up=m_rw_a_lora_up, rw_k_k=m_rw_k_k, rw_k_a=m_rw_k_a, rw_r_k=m_rw_r_k,
               rw_ln_g=m_rw_ln_g, rw_ln_b=m_rw_ln_b, fox_b_f=m_fox_b_f, w_up_rwkv=m_w_up_rwkv, w_up_fox=m_w_up_fox,
               w_out=m_w_out, ple_proj=m_ple_proj, ple_gate_w=m_ple_gate_w, ple_norm_g=m_ple_norm_g,
               final_norm_g=m_final_norm_g)
    vel = dict(norm_g=v_norm_g, w_in=v_w_in, rw_shift_mu=v_rw_shift_mu, rw_w0=v_rw_w0, rw_w_lora_up=v_rw_w_lora_up,
               rw_a0=v_rw_a0, rw_a_lora_up=v_rw_a_lora_up, rw_k_k=v_rw_k_k, rw_k_a=v_rw_k_a, rw_r_k=v_rw_r_k,
               rw_ln_g=v_rw_ln_g, rw_ln_b=v_rw_ln_b, fox_b_f=v_fox_b_f, w_up_rwkv=v_w_up_rwkv, w_up_fox=v_w_up_fox,
               w_out=v_w_out, ple_proj=v_ple_proj, ple_gate_w=v_ple_gate_w, ple_norm_g=v_ple_norm_g,
               final_norm_g=v_final_norm_g)

    t, d = x.shape[1], x.shape[2]
    cw = rw_w0.shape[1]
    lr = rw_w_lora_up.shape[1]
    fh = fox_b_f.shape[1]
    fw = fh * HEAD
    rw_cols = 4 * cw + 2 * lr
    fox_cols = 4 * fw + fh
    assert 2 * lr == LANES and cw % LANES == 0 and fw % LANES == 0 and fh <= LANES
    xs = x[0]
    ps = p[0, 0]
    tgt = loss_target[0]

    assert BIG[0] == "w_in"
    shard_shapes = [wts[n].shape[1:] for n in BIG]
    slab_w = shard_shapes[0][1]
    n_rest = sum(s[0] * s[1] for s in shard_shapes[1:])
    slab_rows = -(-(d + -(-n_rest // slab_w)) // 32) * 32
    w_slab = _pack_slab(w_in[0], [wts[n][0] for n in BIG[1:]], slab_rows)
    cast_tile = _div_tile(slab_rows, 256, 16)
    (w_slab16,) = _rows("weights_to_bf16", lambda i, a: (a,), slab_rows, cast_tile,
                        [(w_slab, _row_spec(cast_tile, slab_w))], [(slab_w, BF16)])
    gathered = _place_own_shard(_all_gather_shards(w_slab16), w_slab16)
    per_shard = [_unpack_slab(gathered[s], d, shard_shapes[1:]) for s in range(N_SHARD)]
    full = {n: _full_from_shards(n, [per_shard[s][j] for s in range(N_SHARD)]) for j, n in enumerate(BIG)}
    w_rw = full["w_in"][:, :rw_cols]
    w_fox = jnp.pad(full["w_in"][:, rw_cols:rw_cols + fox_cols], ((0, 0), (0, LANES - fh)))
    w_gate = full["w_in"][:, rw_cols + fox_cols:]
    wup_pad = jnp.pad(full["rw_w_lora_up"], ((0, lr), (0, 0)))
    aup_pad = jnp.pad(full["rw_a_lora_up"], ((lr, 0), (0, 0)))
    b_pad = _pad_lanes(fox_b_f)
    r_k_row = rw_r_k.reshape(1, cw)
    gf_row = final_norm_g.reshape(1, d)

    tile = _tile(t, (256, 128, 64, 32, 16, 8))
    tile_s = _tile(t, (128, 64, 32, 16, 8))
    n_s = t // tile_s
    full2 = lambda a: (a, _full_spec(a.shape))

    (h,) = _rows("norm1", lambda i, a, g: (_rms(a, g),), t, tile, [(xs, _row_spec(tile, d)), full2(norm_g)], [(d, BF16)])
    z_rw = _matmul("proj_rw", h, w_rw)
    z_fox = _matmul("proj_fox", h, w_fox)
    z_gate = _matmul("proj_gate", h, w_gate)

    pre_consts = [full2(rw_shift_mu), full2(rw_w0), full2(rw_a0), full2(wup_pad), full2(aup_pad), full2(rw_k_k),
                  full2(rw_k_a)]

    def pre_fwd(i, z, prev8, *consts):
        return _rw_pre(z, _shifted(i, z, prev8), *consts, cw=cw)

    r_, w_, k_, v_, kk_, a_, g_ = _rows(
        "rwkv_pre", pre_fwd, t, tile_s,
        [(z_rw, _row_spec(tile_s, rw_cols)), (z_rw, _prev_rows_spec(tile_s, rw_cols))] + pre_consts, [(cw, F32)] * 7)
    y_scan, states, tinvs = _scan_fwd(r_, w_, k_, v_, kk_, a_)
    post_consts = [full2(rw_ln_g), full2(rw_ln_b), full2(r_k_row)]
    post_rows = lambda *arrs: [(a, _row_spec(tile_s, cw)) for a in arrs]
    (y_rw,) = _rows("rwkv_post", lambda i, *a: (_rw_post(*a, cw=cw),), t, tile_s,
                    post_rows(y_scan, r_, k_, v_, g_) + post_consts, [(cw, BF16)])

    c_fox = _fox_cumsum(z_fox, b_pad, fw=fw, fh=fh)
    tq = _fox_tile(t)
    n_pair_f = fw // LANES
    head_rows = lambda a: a.T.reshape(n_pair_f, 2, t // tq, tq).transpose(0, 2, 1, 3)
    head_cols = lambda a: a.transpose(0, 2, 1, 3).reshape(fh, t).T
    ct_fox = head_rows(c_fox[:, :fh])
    o_fox, lse_fox, y_fox = _fox_fwd(z_fox, c_fox, ct_fox, fw=fw)

    u_rw = _matmul("up_rwkv", y_rw, full["w_up_rwkv"])
    u_fox = _matmul("up_fox", y_fox, full["w_up_fox"])
    (merged,) = _rows("merge", lambda i, zg, a, b: (_merge(zg, a, b, d=d),), t, tile,
                      [(z_gate, _row_spec(tile, 2 * d)), (u_rw, _row_spec(tile, d)), (u_fox, _row_spec(tile, d))],
                      [(d, BF16)])
    x1 = _matmul("out_proj", merged, full["w_out"], add=xs)
    (n2,) = _rows("norm2", lambda i, a, g: (_rms(a, g),), t, tile, [(x1, _row_spec(tile, d)), full2(ple_norm_g)],
                  [(d, BF16)])
    gl = _matmul("ple_gate", n2, full["ple_gate_w"])
    ple = _matmul("ple_proj", ps, full["ple_proj"])

    def head_bwd(i, x1_t, ple_t, gl_t, gf, tg):
        loss, vjp = jax.vjp(lambda a, b, cc, g: _head_loss(a, b, cc, g, tg), x1_t, ple_t, gl_t, gf)
        dx1, dple, dgl, dgf = vjp(jnp.ones((1, 1), F32))
        return dx1, dple, dgl, jnp.broadcast_to(loss, (1, LANES)), dgf

    dx2, dple, dgl, loss_row, d_gf = _rows(
        "loss_head", head_bwd, t, tile_s,
        [(x1, _row_spec(tile_s, d)), (ple, _row_spec(tile_s, d)), (gl, _row_spec(tile_s, d)), full2(gf_row),
         (tgt, _row_spec(tile_s, d))],
        [(d, F32), (d, BF16), (d, BF16)], [(1, LANES), (1, d)])

    g_ple_proj = _matmul("d_ple_proj", ps, dple, ta=True)
    g_ple_gate = _matmul("d_ple_gate_w", n2, dgl, ta=True)
    dn2 = _matmul("d_n2", dgl, full["ple_gate_w"], tb=True)

    def norm_bwd(i, a, g, dh, res):
        _, vjp = jax.vjp(_rms, a, g)
        da, dg = vjp(dh)
        return res + da, dg

    dx1, d_g2 = _rows("norm2_bwd", norm_bwd, t, tile_s,
                      [(x1, _row_spec(tile_s, d)), full2(ple_norm_g), (dn2, _row_spec(tile_s, d)),
                       (dx2, _row_spec(tile_s, d))], [(d, F32)], [(1, d)])
    g_w_out = _matmul("d_w_out", merged, dx1, ta=True)
    dmerged = _matmul("d_merged", dx1, full["w_out"], tb=True)

    def merge_bwd(i, zg, a, b, dm):
        _, vjp = jax.vjp(functools.partial(_merge, d=d), zg, a, b)
        return vjp(dm)

    dz_gate, du_rw, du_fox = _rows(
        "merge_bwd", merge_bwd, t, tile_s,
        [(z_gate, _row_spec(tile_s, 2 * d)), (u_rw, _row_spec(tile_s, d)), (u_fox, _row_spec(tile_s, d)),
         (dmerged, _row_spec(tile_s, d))], [(2 * d, BF16), (d, BF16), (d, BF16)])
    g_up_rw = _matmul("d_w_up_rwkv", y_rw, du_rw, ta=True)
    g_up_fox = _matmul("d_w_up_fox", y_fox, du_fox, ta=True)
    dy_rw = _matmul("d_y_rwkv", du_rw, full["w_up_rwkv"], tb=True)
    dy_fox = _matmul("d_y_fox", du_fox, full["w_up_fox"], tb=True)

    def post_bwd(i, y, r, k, v, g, ln_g, ln_b, r_k, dy):
        _, vjp = jax.vjp(functools.partial(_rw_post, cw=cw), y, r, k, v, g, ln_g, ln_b, r_k)
        return vjp(dy)

    dys, dr1, dk1, dv1, dg1, d_ln_g, d_ln_b, d_r_k = _rows(
        "rwkv_post_bwd", post_bwd, t, tile_s,
        post_rows(y_scan, r_, k_, v_, g_) + post_consts + post_rows(dy_rw), [(cw, F32)] * 5, [(1, cw)] * 3)
    dr2, dw2, dk2, dv2, dkk2, da2 = _scan_bwd(r_, w_, k_, v_, kk_, a_, states, tinvs, dys)

    def pre_bwd(i, z, prev8, mu, w0, a0, wup, aup, k_k, k_a, dr_a, dr_b, dk_a, dk_b, dv_a, dv_b, dw, dkk, da, dg):
        zp = _shifted(i, z, prev8)
        _, vjp = jax.vjp(functools.partial(_rw_pre, cw=cw), z, zp, mu, w0, a0, wup, aup, k_k, k_a)
        dz, dzp, dmu, dw0, da0, dwup, daup, dk_k, dk_a = vjp((dr_a + dr_b, dw, dk_a + dk_b, dv_a + dv_b, dkk, da, dg))
        row = lax.broadcasted_iota(jnp.int32, dz.shape, 0)
        dz = dz + jnp.where(row < tile_s - 1, pltpu.roll(dzp, tile_s - 1, 0), 0.0)
        first = jnp.where(lax.broadcasted_iota(jnp.int32, (8, dz.shape[1]), 0) == 0, _row_of(dzp, 0), 0.0)
        return dz, first, dmu, dw0, da0, dwup, daup, dk_k, dk_a

    def pre_bwd_call():
        n_in = 2 + len(pre_consts) + 10
        ins = ([(z_rw, _row_spec(tile_s, rw_cols)), (z_rw, _prev_rows_spec(tile_s, rw_cols))] + pre_consts
               + post_rows(dr1, dr2, dk1, dk2, dv1, dv2, dw2, dkk2, da2, dg1))

        def body(*refs):
            i = pl.program_id(0)
            vals = pre_bwd(i, *[r[...] for r in refs[:n_in]])
            refs[n_in][...] = vals[0]
            refs[n_in + 1][...] = vals[1]
            for r, v in zip(refs[n_in + 2:], vals[2:]):
                @pl.when(i == 0)
                def _(r=r, v=v):
                    r[...] = v

                @pl.when(i > 0)
                def _(r=r, v=v):
                    r[...] += v

        acc_shapes = [(1, rw_cols), (1, cw), (1, cw), (LANES, cw), (LANES, cw), (1, cw), (1, cw)]
        return pl.pallas_call(
            body, name="rwkv_pre_bwd", grid=(n_s,), in_specs=[s for _, s in ins],
            out_specs=[_row_spec(tile_s, rw_cols), pl.BlockSpec((8, rw_cols), lambda i: (i, 0))]
            + [_full_spec(s) for s in acc_shapes],
            out_shape=[jax.ShapeDtypeStruct((t, rw_cols), F32), jax.ShapeDtypeStruct((8 * n_s, rw_cols), F32)]
            + [jax.ShapeDtypeStruct(s, F32) for s in acc_shapes],
            compiler_params=_params("arbitrary"),
        )(*[a for a, _ in ins])

    dz_main, dz_first, d_mu, d_w0, d_a0, d_wup, d_aup, d_k_k, d_k_a = pre_bwd_call()

    def add_next_row(i, dz, nxt8):
        row = lax.broadcasted_iota(jnp.int32, dz.shape, 0)
        carry = jnp.where(i < n_s - 1, _row_of(nxt8, 0), 0.0)
        return (dz + jnp.where(row == tile_s - 1, carry, 0.0),)

    (dz_rw,) = _rows("rwkv_shift_bwd", add_next_row, t, tile_s,
                     [(dz_main, _row_spec(tile_s, rw_cols)), (dz_first, _next_rows_spec(tile_s, rw_cols, n_s))],
                     [(rw_cols, BF16)])

    def fox_post_bwd(i, o, g, dy):
        _, vjp = jax.vjp(lambda oo, gg: oo * (gg * jax.nn.sigmoid(gg)), o, g)
        do, dg = vjp(dy)
        return do, _head_sum(do * o, _head_matrix(fw)), dg

    do_fox, dd_fox, dg_fox = _rows(
        "fox_post_bwd", fox_post_bwd, t, tile_s,
        [(o_fox, _row_spec(tile_s, fw)), (z_fox, _row_spec(tile_s, fw, 3)), (dy_fox, _row_spec(tile_s, fw))],
        [(fw, F32), (fw, F32), (fw, BF16)])
    dq_f, dk_f, dv_f, dcq, dck = _fox_bwd(z_fox, do_fox, c_fox, ct_fox, head_rows(lse_fox[:, ::HEAD]),
                                          head_rows(dd_fox[:, ::HEAD]), fw=fw)
    dc = head_cols(dcq) + dck.reshape(t, n_pair_f, LANES)[:, :, :2].reshape(t, fh)
    dfl, d_bf = _fox_cumsum_bwd(z_fox, b_pad, jnp.pad(dc, ((0, 0), (0, LANES - fh))), fw=fw, fh=fh)
    dz_fox = jnp.concatenate([dq_f, dk_f, dv_f, dg_fox, dfl], axis=1)

    g_w_rw = _matmul("d_w_in_rw", h, dz_rw, ta=True)
    g_w_fox = _matmul("d_w_in_fox", h, dz_fox, ta=True)
    g_w_gate = _matmul("d_w_in_gate", h, dz_gate, ta=True)
    dh = _matmul("d_h_rw", dz_rw, w_rw, tb=True)
    dh = _matmul("d_h_fox", dz_fox, w_fox, tb=True, add=dh)
    dh = _matmul("d_h_gate", dz_gate, w_gate, tb=True, add=dh)
    grad_x, d_g1 = _rows("norm1_bwd", norm_bwd, t, tile_s,
                         [(xs, _row_spec(tile_s, d)), full2(norm_g), (dh, _row_spec(tile_s, d)),
                          (dx1, _row_spec(tile_s, d))], [(d, F32)], [(1, d)])

    g_full = {
        "w_in": jnp.concatenate([g_w_rw, g_w_fox[:, :fox_cols], g_w_gate], axis=1),
        "rw_w_lora_up": d_wup[:lr], "rw_a_lora_up": d_aup[lr:], "w_up_rwkv": g_up_rw, "w_up_fox": g_up_fox,
        "w_out": g_w_out, "ple_proj": g_ple_proj, "ple_gate_w": g_ple_gate,
    }
    by_shard = [_shards_of_full(n, g_full[n]) for n in BIG]
    g_slab = jnp.stack([_pack_slab(by_shard[0][s], [by_shard[j][s] for j in range(1, len(BIG))], slab_rows)
                        for s in range(N_SHARD)])
    g_red = _unpack_slab(_reduce_scatter(g_slab), d, shard_shapes[1:])
    grads = {n: g_red[j][None] for j, n in enumerate(BIG)}

    small_parts = dict(norm_g=d_g1, rw_shift_mu=d_mu, rw_w0=d_w0, rw_a0=d_a0, rw_k_k=d_k_k, rw_k_a=d_k_a, rw_r_k=d_r_k,
                       rw_ln_g=d_ln_g, rw_ln_b=d_ln_b, fox_b_f=d_bf[:, :fh], ple_norm_g=d_g2, final_norm_g=d_gf)
    n_small = sum(-(-wts[n].size // LANES) for n in SMALL)
    small_rows = -(-n_small // 8) * 8
    small_shapes = [wts[n].shape for n in SMALL]
    g_small = _all_reduce_small(_pack_small([small_parts[n] for n in SMALL], small_rows))
    for n, g in zip(SMALL, _unpack_small(g_small, small_shapes)):
        grads[n] = g

    delta, new_m, new_v = {}, {}, {}
    for n in BIG:
        delta[n], new_m[n], new_v[n] = _adamw("adamw_" + n, wts[n], grads[n], mom[n], vel[n])
    packed = lambda src: _pack_small([src[n] for n in SMALL], small_rows)[None]
    for store, out in zip((delta, new_m, new_v), _adamw("adamw_small", packed(wts), g_small[None], packed(mom), packed(vel))):
        for n, a in zip(SMALL, _unpack_small(out[0], small_shapes)):
            store[n] = a

    loss = lax.psum(loss_row[0, 0], ("x", "y", "c"))
    return (loss, grad_x[None], *[grads[n] for n in WEIGHTS], *[delta[n] for n in WEIGHTS],
            *[new_m[n] for n in WEIGHTS], *[new_v[n] for n in WEIGHTS])
```

```python
import functools

import jax
import jax.numpy as jnp
from jax import lax
from jax.experimental import pallas as pl
from jax.experimental.pallas import tpu as pltpu

F32 = jnp.float32
BF16 = jnp.bfloat16
HIGHEST = lax.Precision.HIGHEST
SCAN_PREC = lax.Precision.HIGH
MESH = pl.DeviceIdType.MESH

LANES = 128
HEAD = 64
NORM_EPS = 1e-6
GN_EPS = 64e-5
ADAM_LR = 0.001
ADAM_B1 = 0.9
ADAM_B2 = 0.999
ADAM_EPS = 1e-08
ADAM_WD = 0.01
ADAM_STEP = 10
N_SHARD = 4
VMEM_LIMIT = 56 * 1024 * 1024
PAIRS_PER_STEP = 4


def _params(*sem):
    return pltpu.CompilerParams(dimension_semantics=sem, vmem_limit_bytes=VMEM_LIMIT)


def _tile(n, cands):
    for c in cands:
        if c <= n and n % c == 0:
            return c
    return n


def _div_tile(n, cap, mult):
    return max(c for c in range(mult, min(n, cap) + 1, mult) if n % c == 0)


_ROW_TILES = (512, 256, 128, 64, 32, 16, 8)


def _dot(a, b, prec=None):
    return lax.dot_general(a, b, (((1,), (0,)), ((), ())), precision=prec, preferred_element_type=F32)


def _dot_nt(a, b, prec=None):
    return lax.dot_general(a, b, (((1,), (1,)), ((), ())), precision=prec, preferred_element_type=F32)


def _dot_tn(a, b, prec=None):
    return lax.dot_general(a, b, (((0,), (0,)), ((), ())), precision=prec, preferred_element_type=F32)


@jax.custom_vjp
def _bdot(x, w):
    return _dot(x.astype(BF16), w.astype(BF16))


def _bdot_fwd(x, w):
    return _bdot(x, w), (x, w)


def _bdot_bwd(res, ct):
    x, w = res
    return _dot_nt(ct.astype(BF16), w.astype(BF16)), _dot_tn(x.astype(BF16), ct.astype(BF16))


_bdot.defvjp(_bdot_fwd, _bdot_bwd)


def _head_matrix(width):
    c = lax.broadcasted_iota(jnp.int32, (width, LANES), 0)
    h = lax.broadcasted_iota(jnp.int32, (width, LANES), 1)
    return (c // HEAD == h).astype(F32)


def _head_sum(x, e):
    return _dot_nt(_dot(x, e, SCAN_PREC), e, SCAN_PREC)


def _softplus(x):
    return jnp.maximum(x, 0.0) + jnp.log1p(jnp.exp(-jnp.abs(x)))


def _lane_pick(x, idx):
    lane = lax.broadcasted_iota(jnp.int32, x.shape, 1)
    return jnp.sum(jnp.where(lane == idx, x, 0.0), axis=1, keepdims=True)


def _matmul(name, a, b, *, ta=False, tb=False, add=None, out_dtype=F32):
    m, k = (a.shape[1], a.shape[0]) if ta else a.shape
    n = b.shape[0] if tb else b.shape[1]
    tm = _tile(m, (1024, 512, 256, 128))
    tn = _tile(n, (1408, 1024, 768, 640, 512, 384, 256, 128))
    tk = _tile(k, (1408, 1024, 768, 640, 512, 384, 256, 128, 64, 32, 16))
    nk = k // tk
    dims = (((0 if ta else 1,), (1 if tb else 0,)), ((), ()))

    def body(*refs):
        a_ref, b_ref = refs[0], refs[1]
        o_ref, acc_ref = refs[-2], refs[-1]
        kk = pl.program_id(2)

        @pl.when(kk == 0)
        def _():
            acc_ref[...] = jnp.zeros_like(acc_ref)

        acc_ref[...] += lax.dot_general(a_ref[...].astype(BF16), b_ref[...].astype(BF16), dims,
                                        preferred_element_type=F32)

        @pl.when(kk == nk - 1)
        def _():
            r = acc_ref[...]
            if add is not None:
                r = r + refs[2][...].astype(F32)
            o_ref[...] = r.astype(o_ref.dtype)

    a_spec = pl.BlockSpec((tk, tm), lambda i, j, kk: (kk, i)) if ta else pl.BlockSpec((tm, tk), lambda i, j, kk: (i, kk))
    b_spec = pl.BlockSpec((tn, tk), lambda i, j, kk: (j, kk)) if tb else pl.BlockSpec((tk, tn), lambda i, j, kk: (kk, j))
    o_spec = pl.BlockSpec((tm, tn), lambda i, j, kk: (i, j))
    ins, specs = [a, b], [a_spec, b_spec]
    if add is not None:
        ins.append(add)
        specs.append(o_spec)
    return pl.pallas_call(
        body, name=name, grid=(m // tm, n // tn, nk), in_specs=specs, out_specs=o_spec,
        out_shape=jax.ShapeDtypeStruct((m, n), out_dtype),
        scratch_shapes=[pltpu.VMEM((tm, tn), F32)],
        compiler_params=_params("parallel", "parallel", "arbitrary"),
    )(*ins)


def _rows(name, fn, n_rows, tile, ins, outs, accs=()):
    n_in, n_out = len(ins), len(outs)

    def body(*refs):
        i = pl.program_id(0)
        vals = fn(i, *[r[...] for r in refs[:n_in]])
        for r, v in zip(refs[n_in:n_in + n_out], vals[:n_out]):
            r[...] = v.astype(r.dtype)
        for r, v in zip(refs[n_in + n_out:], vals[n_out:]):
            @pl.when(i == 0)
            def _(r=r, v=v):
                r[...] = v

            @pl.when(i > 0)
            def _(r=r, v=v):
                r[...] += v

    out_specs = [pl.BlockSpec((tile, w), lambda i: (i, 0)) for w, _ in outs]
    out_specs += [pl.BlockSpec(s, lambda i: (0, 0)) for s in accs]
    out_shape = [jax.ShapeDtypeStruct((n_rows, w), d) for w, d in outs]
    out_shape += [jax.ShapeDtypeStruct(s, F32) for s in accs]
    return pl.pallas_call(
        body, name=name, grid=(n_rows // tile,), in_specs=[s for _, s in ins], out_specs=out_specs,
        out_shape=out_shape, compiler_params=_params("arbitrary"),
    )(*[a for a, _ in ins])


def _row_spec(tile, width, col=0):
    return pl.BlockSpec((tile, width), lambda i: (i, col))


def _full_spec(shape):
    return pl.BlockSpec(shape, lambda i: (0,) * len(shape))


def _prev_rows_spec(tile, width):
    return pl.BlockSpec((8, width), lambda i: (jnp.maximum(i * (tile // 8) - 1, 0), 0))


def _next_rows_spec(tile, width, n_tiles):
    return pl.BlockSpec((8, width), lambda i: (jnp.minimum(i + 1, n_tiles - 1), 0))


def _row_of(x8, idx):
    r = lax.broadcasted_iota(jnp.int32, x8.shape, 0)
    return jnp.sum(jnp.where(r == idx, x8, 0.0), axis=0, keepdims=True)


def _rms(x, g):
    return x * lax.rsqrt(jnp.mean(x * x, axis=-1, keepdims=True) + NORM_EPS) * g


def _shifted(i, z, prev8):
    first = jnp.where(i > 0, _row_of(prev8, 7), 0.0)
    row = lax.broadcasted_iota(jnp.int32, z.shape, 0)
    return jnp.where(row == 0, first, pltpu.roll(z, 1, 0))


def _rw_pre(z, zp, mu, w0, a0, wup, aup, k_k, k_a, *, cw):
    zs = z + (zp - z) * mu
    r, k, v, g = (zs[:, j * cw:(j + 1) * cw] for j in range(4))
    lo = zs[:, 4 * cw:4 * cw + LANES]
    w_raw = w0 + _bdot(jnp.tanh(lo), wup)
    decay = jnp.exp(-jnp.exp(-_softplus(-w_raw) - 0.5))
    a = jax.nn.sigmoid(a0 + _bdot(lo, aup))
    e = _head_matrix(cw)
    kk = k * k_k
    kk = kk / jnp.maximum(jnp.sqrt(_head_sum(kk * kk, e)), 1e-12)
    k_mod = k * (1.0 + (a - 1.0) * k_a)
    return r, decay, k_mod, v, kk, a, g


def _rw_post(y, r, k_mod, v, g, ln_g, ln_b, r_k, *, cw):
    e = _head_matrix(cw)
    mu = _head_sum(y, e) * (1.0 / HEAD)
    d = y - mu
    var = _head_sum(d * d, e) * (1.0 / HEAD)
    yn = d * lax.rsqrt(var + GN_EPS) * ln_g + ln_b
    bonus = _head_sum(r * k_mod * r_k, e) * v
    return (yn + bonus) * (g * jax.nn.sigmoid(g))


def _merge(zg, u_rw, u_fox, *, d):
    return jax.nn.sigmoid(zg[:, :d]) * u_rw + jax.nn.sigmoid(zg[:, d:]) * u_fox


def _head_loss(x1, ple, gl, gf, tgt):
    x2 = x1 + ple * jax.nn.sigmoid(gl)
    err = _rms(x2, gf) - tgt
    return 0.5 * jnp.sum(jnp.mean(err * err, axis=-1, keepdims=True), axis=0, keepdims=True)


def _eliminate(lo):
    n, c, _ = lo.shape
    ri = lax.broadcasted_iota(jnp.int32, (n, c, c), 1)
    ci = lax.broadcasted_iota(jnp.int32, (n, c, c), 2)
    x = (ri == ci).astype(F32)
    for s in range(c - 1):
        col = jnp.sum(jnp.where(ci == s, lo, 0.0), axis=2, keepdims=True)
        row = jnp.sum(jnp.where(ri == s, x, 0.0), axis=1, keepdims=True)
        x = x - col * row
    return x


def _batched(a, b, ca, cb):
    return lax.dot_general(a, b, (((ca,), (cb,)), ((0,), (0,))), precision=SCAN_PREC, preferred_element_type=F32)


@jax.custom_vjp
def _unit_lower_inverse(lo, known):
    return _eliminate(lo) if known is None else known


def _uli_fwd(lo, known):
    x = _unit_lower_inverse(lo, known)
    return x, (x, known)


def _uli_bwd(res, dx):
    x, known = res
    dlo = -_batched(_batched(x, dx, 1, 1), x, 2, 2)
    return dlo, (None if known is None else jnp.zeros_like(known))


_unit_lower_inverse.defvjp(_uli_fwd, _uli_bwd)


def _rwkv_chunk(s0, r, w, k, v, kk, a, *, c, tinv_known=None):
    pairs = range(len(s0))
    lane = lax.broadcasted_iota(jnp.int32, (1, LANES), 1)
    heads = (lane < HEAD, lane >= HEAD)
    ti = lax.broadcasted_iota(jnp.int32, (c, c), 0)
    si = lax.broadcasted_iota(jnp.int32, (c, c), 1)
    incl = si <= ti
    strict = si < ti
    tri = incl.astype(F32)
    logw = [jnp.log(w[p]) for p in pairs]
    cum = [_dot(tri, logw[p], HIGHEST) for p in pairs]
    cum_end = [jnp.sum(logw[p], axis=0, keepdims=True) for p in pairs]
    g_inv = [jnp.exp(-cum[p]) for p in pairs]
    to_end = [jnp.exp(cum_end[p] - cum[p]) for p in pairs]
    b = [kk[p] * a[p] for p in pairs]
    beta = [b[p] * g_inv[p] for p in pairs]
    kap = [kk[p] * jnp.exp(cum[p] - logw[p]) for p in pairs]
    kt = [k[p] * g_inv[p] for p in pairs]
    rt = [r[p] * jnp.exp(cum[p]) for p in pairs]
    lhs = [jnp.concatenate([jnp.where(m, x[p], 0.0) for x in (kap, rt) for m in heads], axis=0) for p in pairs]
    vs_beta = [_dot_nt(lhs[p], beta[p], None) for p in pairs]
    vs_kt = [_dot_nt(lhs[p], kt[p], None) for p in pairs]
    strict2 = jnp.concatenate([strict, strict], axis=0)
    incl2 = jnp.concatenate([incl, incl], axis=0)
    lo = [jnp.where(strict2, vs_beta[p][:2 * c], 0.0) for p in pairs]
    mm = [jnp.where(strict2, vs_kt[p][:2 * c], 0.0) for p in pairs]
    arb = [jnp.where(incl2, vs_beta[p][2 * c:], 0.0) for p in pairs]
    ark = [jnp.where(incl2, vs_kt[p][2 * c:], 0.0) for p in pairs]
    per_head = lambda xs: jnp.concatenate([xs[p][h * c:(h + 1) * c][None] for p in pairs for h in (0, 1)])
    tinv = _unit_lower_inverse(per_head(lo), None if tinv_known is None else per_head(tinv_known))
    tinv = [jnp.concatenate([tinv[2 * p], tinv[2 * p + 1]], axis=0) for p in pairs]
    both = lambda x: jnp.where(heads[0], x[:c], x[c:])
    vs_s = [_dot_nt(jnp.concatenate([kap[p], rt[p]], axis=0), s0[p], None) for p in pairs]
    rhs = [vs_s[p][:c] + both(_dot(mm[p], v[p], None)) for p in pairs]
    u = [-both(_dot(tinv[p], rhs[p], None)) for p in pairs]
    y = [vs_s[p][c:] + both(_dot(arb[p], u[p], None) + _dot(ark[p], v[p], None)) for p in pairs]
    rr = lax.broadcasted_iota(jnp.int32, (LANES, LANES), 0) < HEAD
    cc = lax.broadcasted_iota(jnp.int32, (LANES, LANES), 1) < HEAD
    ds = [_dot_tn(jnp.concatenate([u[p], v[p]], axis=0),
                  jnp.concatenate([b[p] * to_end[p], k[p] * to_end[p]], axis=0), None) for p in pairs]
    s1 = [s0[p] * jnp.exp(cum_end[p]) + jnp.where(rr == cc, ds[p], 0.0) for p in pairs]
    return tuple(y), tuple(s1), tuple(tinv)


def _scan_tiles(t, n_pair):
    return _tile(t, (32, 16, 8)), _tile(t, (512, 256, 128, 64, 32)), _tile(n_pair, (PAIRS_PER_STEP, 2, 1))


def _scan_fwd(r, w, k, v, kk, a):
    t, width = r.shape
    c, tb, npb = _scan_tiles(t, width // LANES)
    n_grp, n_blk, n_cb = width // (LANES * npb), t // tb, tb // c

    def body(r_ref, w_ref, k_ref, v_ref, kk_ref, a_ref, y_ref, st_ref, ti_ref, s_scr):
        @pl.when(pl.program_id(1) == 0)
        def _():
            s_scr[...] = jnp.zeros_like(s_scr)

        def chunk(j, carry):
            sl = pl.ds(pl.multiple_of(j * c, c), c)
            lanes = [pl.ds(q * LANES, LANES) for q in range(npb)]
            s0 = tuple(s_scr[q] for q in range(npb))
            cols = lambda ref: tuple(ref[sl, ln] for ln in lanes)
            y, s1, tinv = _rwkv_chunk(s0, cols(r_ref), cols(w_ref), cols(k_ref), cols(v_ref), cols(kk_ref), cols(a_ref),
                                      c=c)
            for q, ln in enumerate(lanes):
                st_ref[q, j] = s0[q]
                ti_ref[q, j] = tinv[q]
                y_ref[sl, ln] = y[q]
                s_scr[q] = s1[q]
            return carry

        lax.fori_loop(0, n_cb, chunk, 0)

    blk = pl.BlockSpec((tb, npb * LANES), lambda p, i: (i, p))
    return pl.pallas_call(
        body, name="rwkv_scan_fwd", grid=(n_grp, n_blk), in_specs=[blk] * 6,
        out_specs=[blk, pl.BlockSpec((npb, n_cb, LANES, LANES), lambda p, i: (p, i, 0, 0)),
                   pl.BlockSpec((npb, n_cb, 2 * c, c), lambda p, i: (p, i, 0, 0))],
        out_shape=[jax.ShapeDtypeStruct((t, width), F32),
                   jax.ShapeDtypeStruct((width // LANES, t // c, LANES, LANES), F32),
                   jax.ShapeDtypeStruct((width // LANES, t // c, 2 * c, c), F32)],
        scratch_shapes=[pltpu.VMEM((npb, LANES, LANES), F32)],
        compiler_params=_params("arbitrary", "arbitrary"),
    )(r, w, k, v, kk, a)


def _scan_bwd(r, w, k, v, kk, a, st, ti, dy):
    t, width = r.shape
    c, tb, npb = _scan_tiles(t, width // LANES)
    n_grp, n_blk, n_cb = width // (LANES * npb), t // tb, tb // c

    def body(r_ref, w_ref, k_ref, v_ref, kk_ref, a_ref, st_ref, ti_ref, dy_ref,
             dr_ref, dw_ref, dk_ref, dv_ref, dkk_ref, da_ref, ds_scr):
        @pl.when(pl.program_id(1) == 0)
        def _():
            ds_scr[...] = jnp.zeros_like(ds_scr)

        def chunk(jj, carry):
            j = n_cb - 1 - jj
            sl = pl.ds(pl.multiple_of(j * c, c), c)
            lanes = [pl.ds(q * LANES, LANES) for q in range(npb)]
            cols = lambda ref: tuple(ref[sl, ln] for ln in lanes)
            args = (tuple(st_ref[q, j] for q in range(npb)), cols(r_ref), cols(w_ref), cols(k_ref), cols(v_ref),
                    cols(kk_ref), cols(a_ref))
            known = tuple(ti_ref[q, j] for q in range(npb))
            _, vjp = jax.vjp(lambda *xs: _rwkv_chunk(*xs, c=c, tinv_known=known)[:2], *args)
            grads = vjp((cols(dy_ref), tuple(ds_scr[q] for q in range(npb))))
            for q, ln in enumerate(lanes):
                ds_scr[q] = grads[0][q]
                for ref, g in zip((dr_ref, dw_ref, dk_ref, dv_ref, dkk_ref, da_ref), grads[1:]):
                    ref[sl, ln] = g[q]
            return carry

        lax.fori_loop(0, n_cb, chunk, 0)

    blk = pl.BlockSpec((tb, npb * LANES), lambda p, i: (n_blk - 1 - i, p))
    stb = pl.BlockSpec((npb, n_cb, LANES, LANES), lambda p, i: (p, n_blk - 1 - i, 0, 0))
    tib = pl.BlockSpec((npb, n_cb, 2 * c, c), lambda p, i: (p, n_blk - 1 - i, 0, 0))
    return pl.pallas_call(
        body, name="rwkv_scan_bwd", grid=(n_grp, n_blk), in_specs=[blk] * 6 + [stb, tib, blk], out_specs=[blk] * 6,
        out_shape=[jax.ShapeDtypeStruct((t, width), F32)] * 6,
        scratch_shapes=[pltpu.VMEM((npb, LANES, LANES), F32)],
        compiler_params=_params("arbitrary", "arbitrary"),
    )(r, w, k, v, kk, a, st, ti, dy)


NEG = -1e30


def _fox_cumsum(zf, b_pad, *, fw, fh):
    t = zf.shape[0]
    tile = _tile(t, (256, 128, 64, 32, 16, 8))

    def body(fl_ref, b_ref, c_ref, carry):
        @pl.when(pl.program_id(0) == 0)
        def _():
            carry[...] = jnp.zeros_like(carry)

        lane = lax.broadcasted_iota(jnp.int32, (tile, LANES), 1)
        logf = jnp.where(lane < fh, -_softplus(-(fl_ref[...] + b_ref[...])), 0.0)
        ri = lax.broadcasted_iota(jnp.int32, (tile, tile), 0)
        ci = lax.broadcasted_iota(jnp.int32, (tile, tile), 1)
        c_ref[...] = carry[...] + _dot((ci <= ri).astype(F32), logf, HIGHEST)
        carry[...] += jnp.sum(logf, axis=0, keepdims=True)

    return pl.pallas_call(
        body, name="fox_cumsum", grid=(t // tile,),
        in_specs=[_row_spec(tile, LANES, 4 * fw // LANES), _full_spec((1, LANES))],
        out_specs=_row_spec(tile, LANES), out_shape=jax.ShapeDtypeStruct((t, LANES), F32),
        scratch_shapes=[pltpu.VMEM((1, LANES), F32)], compiler_params=_params("arbitrary"),
    )(zf, b_pad)


def _fox_cumsum_bwd(zf, b_pad, dc, *, fw, fh):
    t = zf.shape[0]
    tile = _tile(t, (256, 128, 64, 32, 16, 8))
    n = t // tile

    def body(fl_ref, b_ref, dc_ref, dfl_ref, db_ref, carry):
        i = pl.program_id(0)

        @pl.when(i == 0)
        def _():
            carry[...] = jnp.zeros_like(carry)
            db_ref[...] = jnp.zeros_like(db_ref)

        lane = lax.broadcasted_iota(jnp.int32, (tile, LANES), 1)
        dc_t = dc_ref[...]
        ri = lax.broadcasted_iota(jnp.int32, (tile, tile), 0)
        ci = lax.broadcasted_iota(jnp.int32, (tile, tile), 1)
        dlogf = carry[...] + _dot((ci >= ri).astype(F32), dc_t, HIGHEST)
        carry[...] += jnp.sum(dc_t, axis=0, keepdims=True)
        dfl = jnp.where(lane < fh, dlogf * jax.nn.sigmoid(-(fl_ref[...] + b_ref[...])), 0.0)
        dfl_ref[...] = dfl.astype(dfl_ref.dtype)
        db_ref[...] += jnp.sum(dfl, axis=0, keepdims=True)

    rev = lambda col: pl.BlockSpec((tile, LANES), lambda i: (n - 1 - i, col))
    return pl.pallas_call(
        body, name="fox_cumsum_bwd", grid=(n,),
        in_specs=[rev(4 * fw // LANES), _full_spec((1, LANES)), rev(0)],
        out_specs=[rev(0), _full_spec((1, LANES))],
        out_shape=[jax.ShapeDtypeStruct((t, LANES), BF16), jax.ShapeDtypeStruct((1, LANES), F32)],
        scratch_shapes=[pltpu.VMEM((1, LANES), F32)], compiler_params=_params("arbitrary"),
    )(zf, b_pad, dc)


def _fox_tile(t):
    return _tile(t, (512, 256, 128))


def _fox_fwd(zf, c, ct, *, fw):
    t = zf.shape[0]
    tq = _fox_tile(t)
    th = tq // 2
    n_pair, n_q = fw // LANES, t // tq
    scale = HEAD ** -0.5
    chains = [(h, qq) for h in (0, 1) for qq in (0, 1)]

    def body(q_ref, k_ref, v_ref, g_ref, c_ref, ct_ref, o_ref, lse_ref, y_ref):
        hp, i = pl.program_id(0), pl.program_id(1)
        lane = lax.broadcasted_iota(jnp.int32, (1, LANES), 1)
        in_head = (lane < HEAD, lane >= HEAD)
        rows = [pl.ds(qq * th, th) for qq in (0, 1)]
        qh = [jnp.where(in_head[h], q_ref[rows[qq], :] * scale, 0.0).astype(BF16) for h, qq in chains]
        cq = [_lane_pick(c_ref[rows[qq], :], 2 * hp + h) for h, qq in chains]
        qidx = lax.broadcasted_iota(jnp.int32, (th, tq), 0)
        kidx = lax.broadcasted_iota(jnp.int32, (th, tq), 1)

        def kv_step(j, carry, diagonal):
            m, l, acc = carry
            ks = pl.ds(pl.multiple_of(j * tq, tq), tq)
            kb = k_ref[ks, :].astype(BF16)
            vb = v_ref[ks, :]
            vh = [jnp.where(in_head[h], vb, 0.0).astype(BF16) for h in (0, 1)]
            ck = [ct_ref[0, j, pl.ds(h, 1), :] for h in (0, 1)]
            s = [_dot_nt(qh[n], kb) + cq[n] - ck[h] for n, (h, qq) in enumerate(chains)]
            if diagonal:
                s = [jnp.where(qq * th + qidx >= kidx, s[n], NEG) for n, (h, qq) in enumerate(chains)]
            m_new = [jnp.maximum(m[n], jnp.max(s[n], axis=1, keepdims=True)) for n in range(4)]
            p = [jnp.exp(s[n] - m_new[n]) for n in range(4)]
            alpha = [jnp.exp(m[n] - m_new[n]) for n in range(4)]
            l = [l[n] * alpha[n] + jnp.sum(p[n], axis=1, keepdims=True) for n in range(4)]
            pv = [_dot(p[n].astype(BF16), vh[h]) for n, (h, qq) in enumerate(chains)]
            acc = [acc[qq] * jnp.where(in_head[0], alpha[qq], alpha[2 + qq]) + pv[qq] + pv[2 + qq] for qq in (0, 1)]
            return tuple(m_new), tuple(l), tuple(acc)

        init = (tuple(jnp.full((th, 1), NEG, F32) for _ in chains), tuple(jnp.zeros((th, 1), F32) for _ in chains),
                tuple(jnp.zeros((th, LANES), F32) for _ in (0, 1)))
        carry = lax.fori_loop(0, i, functools.partial(kv_step, diagonal=False), init)
        m, l, acc = kv_step(i, carry, True)
        for qq in (0, 1):
            o = acc[qq] / jnp.where(in_head[0], l[qq], l[2 + qq])
            g = g_ref[rows[qq], :]
            o_ref[rows[qq], :] = o
            lse_ref[rows[qq], :] = jnp.where(in_head[0], m[qq] + jnp.log(l[qq]), m[2 + qq] + jnp.log(l[2 + qq]))
            y_ref[rows[qq], :] = (o * (g * jax.nn.sigmoid(g))).astype(y_ref.dtype)

    npw = fw // LANES
    blk = lambda col0: pl.BlockSpec((tq, LANES), lambda hp, i: (i, col0 + hp))
    res = lambda col0: pl.BlockSpec((t, LANES), lambda hp, i: (0, col0 + hp))
    out_blk = pl.BlockSpec((tq, LANES), lambda hp, i: (i, hp))
    return pl.pallas_call(
        body, name="fox_attn_fwd", grid=(n_pair, n_q),
        in_specs=[blk(0), res(npw), res(2 * npw), blk(3 * npw),
                  pl.BlockSpec((tq, LANES), lambda hp, i: (i, 0)),
                  pl.BlockSpec((1, n_q, 2, tq), lambda hp, i: (hp, 0, 0, 0))],
        out_specs=[out_blk, out_blk, out_blk],
        out_shape=[jax.ShapeDtypeStruct((t, fw), F32), jax.ShapeDtypeStruct((t, fw), F32),
                   jax.ShapeDtypeStruct((t, fw), BF16)],
        compiler_params=_params("arbitrary", "arbitrary"),
    )(zf, zf, zf, zf, c, ct)


def _fox_bwd(zf, do, c, ct, lse_r, dd_r, *, fw):
    t = zf.shape[0]
    tq = _fox_tile(t)
    n_pair, n_q = fw // LANES, t // tq
    scale = HEAD ** -0.5

    def body(q_ref, k_ref, v_ref, do_ref, c_ref, ct_ref, lse_ref, dd_ref,
             dq_ref, dk_ref, dv_ref, dcq_ref, dck_ref, dq_acc, dcq_acc):
        hp, j = pl.program_id(0), pl.program_id(1)

        @pl.when(j == 0)
        def _():
            dq_acc[...] = jnp.zeros_like(dq_acc)
            dcq_acc[...] = jnp.zeros_like(dcq_acc)

        lane = lax.broadcasted_iota(jnp.int32, (1, LANES), 1)
        in_head = (lane < HEAD, lane >= HEAD)
        kb = k_ref[...]
        kh = [jnp.where(m, kb, 0.0).astype(BF16) for m in in_head]
        vb = v_ref[...].astype(BF16)
        c_k = c_ref[...]
        ck = [_lane_pick(c_k, 2 * hp + h) for h in (0, 1)]
        kidx = lax.broadcasted_iota(jnp.int32, (tq, tq), 0)
        qidx = lax.broadcasted_iota(jnp.int32, (tq, tq), 1)

        def q_step(i, carry, diagonal):
            dk, dv, dck = carry
            qs = pl.ds(pl.multiple_of(i * tq, tq), tq)
            qf = q_ref[qs, :] * scale
            dof = do_ref[qs, :]
            qh = [jnp.where(m, qf, 0.0).astype(BF16) for m in in_head]
            doh = [jnp.where(m, dof, 0.0).astype(BF16) for m in in_head]
            row = lambda ref, h: ref[0, i, pl.ds(h, 1), :]
            st = [_dot_nt(kh[h], qh[h]) + row(ct_ref, h) - ck[h] for h in (0, 1)]
            p = [jnp.exp(st[h] - row(lse_ref, h)) for h in (0, 1)]
            if diagonal:
                p = [jnp.where(kidx <= qidx, p[h], 0.0) for h in (0, 1)]
            dst = [p[h] * (_dot_nt(vb, doh[h]) - row(dd_ref, h)) for h in (0, 1)]
            p16 = [x.astype(BF16) for x in p]
            ds16 = [x.astype(BF16) for x in dst]
            dv = dv + _dot(p16[0], doh[0]) + _dot(p16[1], doh[1])
            dk = dk + _dot(ds16[0], qh[0]) + _dot(ds16[1], qh[1])
            dq_acc[qs, :] += _dot_tn(ds16[0], kh[0]) + _dot_tn(ds16[1], kh[1])
            for h in (0, 1):
                dcq_acc[i, pl.ds(h, 1), :] += jnp.sum(dst[h], axis=0, keepdims=True)
            dck = tuple(dck[h] - jnp.sum(dst[h], axis=1, keepdims=True) for h in (0, 1))
            return dk, dv, dck

        zero = jnp.zeros((tq, LANES), F32)
        carry = q_step(j, (zero, zero, (jnp.zeros((tq, 1), F32),) * 2), True)
        dk, dv, dck = lax.fori_loop(j + 1, n_q, functools.partial(q_step, diagonal=False), carry)
        dk_ref[...] = dk.astype(dk_ref.dtype)
        dv_ref[...] = dv.astype(dv_ref.dtype)
        dck_ref[...] = jnp.where(lane == 0, dck[0], jnp.where(lane == 1, dck[1], 0.0))

        @pl.when(j == n_q - 1)
        def _():
            dq_ref[...] = (dq_acc[...] * scale).astype(dq_ref.dtype)
            dcq_ref[0] = dcq_acc[...]

    npw = fw // LANES
    res_z = lambda col0: pl.BlockSpec((t, LANES), lambda hp, j: (0, col0 + hp))
    blk_z = lambda col0: pl.BlockSpec((tq, LANES), lambda hp, j: (j, col0 + hp))
    res = pl.BlockSpec((t, LANES), lambda hp, j: (0, hp))
    blk = pl.BlockSpec((tq, LANES), lambda hp, j: (j, hp))
    rows = pl.BlockSpec((1, n_q, 2, tq), lambda hp, j: (hp, 0, 0, 0))
    return pl.pallas_call(
        body, name="fox_attn_bwd", grid=(n_pair, n_q),
        in_specs=[res_z(0), blk_z(npw), blk_z(2 * npw), res, pl.BlockSpec((tq, LANES), lambda hp, j: (j, 0)),
                  rows, rows, rows],
        out_specs=[res, blk, blk, rows, blk],
        out_shape=[jax.ShapeDtypeStruct((t, fw), BF16), jax.ShapeDtypeStruct((t, fw), BF16),
                   jax.ShapeDtypeStruct((t, fw), BF16), jax.ShapeDtypeStruct((n_pair, n_q, 2, tq), F32),
                   jax.ShapeDtypeStruct((t, fw), F32)],
        scratch_shapes=[pltpu.VMEM((t, LANES), F32), pltpu.VMEM((n_q, 2, tq), F32)],
        compiler_params=_params("arbitrary", "arbitrary"),
    )(zf, zf, zf, do, c, ct, lse_r, dd_r)


def _adamw_math(w, g, m, v):
    m = ADAM_B1 * m + (1.0 - ADAM_B1) * g
    v = ADAM_B2 * v + (1.0 - ADAM_B2) * jnp.square(g)
    m_hat = m / (1.0 - ADAM_B1 ** ADAM_STEP)
    v_hat = v / (1.0 - ADAM_B2 ** ADAM_STEP)
    delta = -ADAM_LR * (m_hat / (jnp.sqrt(v_hat) + ADAM_EPS) + ADAM_WD * w)
    return delta, m, v


def _adamw(name, w, g, m, v):
    _, rows, cols = w.shape
    tile = _tile(rows, (128, 64, 32, 16, 8))
    spec = pl.BlockSpec((1, tile, cols), lambda i: (0, i, 0))

    def body(w_ref, g_ref, m_ref, v_ref, d_ref, mo_ref, vo_ref):
        d_ref[...], mo_ref[...], vo_ref[...] = _adamw_math(w_ref[...], g_ref[...], m_ref[...], v_ref[...])

    return pl.pallas_call(
        body, name=name, grid=(rows // tile,), in_specs=[spec] * 4, out_specs=[spec] * 3,
        out_shape=[jax.ShapeDtypeStruct(w.shape, F32)] * 3, compiler_params=_params("parallel"),
    )(w, g, m, v)


def _place():
    return lax.axis_index("x"), lax.axis_index("y"), lax.axis_index("c")


def _other_chips(x, y):
    return [(1 - x, y), (x, 1 - y), (1 - x, 1 - y)]


HBM_SPEC = pl.BlockSpec(memory_space=pltpu.HBM)


def _all_gather_shards(slabs):
    n = len(slabs)

    def body(*refs):
        src_refs, out_refs, send_sems, recv_sems = refs[:n], refs[n:2 * n], refs[2 * n], refs[2 * n + 1]
        x, y, c = _place()
        me = 2 * x + y
        sibling = (x, y, 1 - c)
        chips = _other_chips(x, y)
        first, passed, waits = [], [], []
        for g, (src_ref, out_ref) in enumerate(zip(src_refs, out_refs)):
            rh = src_ref.shape[0] // 2

            def part(chip, half, out_ref=out_ref, rh=rh):
                return out_ref.at[chip, pl.ds(half * rh, rh), :]

            def copy(k, src, dst, to, g=g):
                return pltpu.make_async_remote_copy(src_ref=src, dst_ref=dst, send_sem=send_sems.at[6 * g + k],
                                                    recv_sem=recv_sems.at[6 * g + k], device_id=to, device_id_type=MESH)

            first += [copy(j, src_ref.at[pl.ds(c * rh, rh), :], part(me, c), (px, py, c))
                      for j, (px, py) in enumerate(chips)]
            for j, (px, py) in enumerate(chips):
                theirs = part(2 * px + py, c)
                passed.append((copy(j, theirs, theirs, sibling), copy(3 + j, theirs, theirs, sibling)))
                other = part(2 * px + py, 1 - c)
                waits.append(copy(3 + j, other, other, sibling))
        for cp in first:
            cp.start()
        for landed, forward in passed:
            landed.wait_recv()
            forward.start()
        for cp in waits:
            cp.wait_recv()
        for cp in first + [fwd for _, fwd in passed]:
            cp.wait_send()

    return pl.pallas_call(
        body, name="weights_all_gather", in_specs=[HBM_SPEC] * n, out_specs=[HBM_SPEC] * n,
        out_shape=[jax.ShapeDtypeStruct((N_SHARD,) + a.shape, a.dtype) for a in slabs],
        scratch_shapes=[pltpu.SemaphoreType.DMA((6 * n,)), pltpu.SemaphoreType.DMA((6 * n,))],
    )(*slabs)


def _chip_index():
    return jnp.reshape(2 * lax.axis_index("x") + lax.axis_index("y"), (1,)).astype(jnp.int32)


def _place_own_shard(name, gathered, slab):
    rows, width = slab.shape
    tile = _div_tile(rows, 256, 16)

    def body(me_ref, s_ref, g_ref, o_ref):
        o_ref[0] = s_ref[...]

    return pl.pallas_call(
        body, name=name,
        grid_spec=pltpu.PrefetchScalarGridSpec(
            num_scalar_prefetch=1, grid=(rows // tile,),
            in_specs=[pl.BlockSpec((tile, width), lambda i, me: (i, 0)), pl.BlockSpec(memory_space=pl.ANY)],
            out_specs=pl.BlockSpec((1, tile, width), lambda i, me: (me[0], i, 0))),
        out_shape=jax.ShapeDtypeStruct(gathered.shape, gathered.dtype), input_output_aliases={2: 0},
        compiler_params=_params("parallel"),
    )(_chip_index(), slab, gathered)


def _sibling_exchange(gs):
    n = len(gs)

    def body(*refs):
        g_refs, out_refs, send_sems, recv_sems = refs[:n], refs[n:2 * n], refs[2 * n], refs[2 * n + 1]
        x, y, c = _place()
        copies = [pltpu.make_async_remote_copy(
            src_ref=g_ref.at[s, 1 - c], dst_ref=out_ref.at[s], send_sem=send_sems.at[N_SHARD * g + s],
            recv_sem=recv_sems.at[N_SHARD * g + s], device_id=(x, y, 1 - c), device_id_type=MESH)
            for g, (g_ref, out_ref) in enumerate(zip(g_refs, out_refs)) for s in range(N_SHARD)]
        for cp in copies:
            cp.start()
        for cp in copies:
            cp.wait()

    return pl.pallas_call(
        body, name="grad_sibling_exchange", in_specs=[HBM_SPEC] * n, out_specs=[HBM_SPEC] * n,
        out_shape=[jax.ShapeDtypeStruct((N_SHARD,) + g.shape[2:], g.dtype) for g in gs],
        scratch_shapes=[pltpu.SemaphoreType.DMA((N_SHARD * n,)), pltpu.SemaphoreType.DMA((N_SHARD * n,))],
    )(*gs)


def _add_sibling(name, g, got):
    _, _, rh, width = g.shape
    tile = _div_tile(rh, 256, 16)
    c_arr = jnp.reshape(lax.axis_index("c"), (1,)).astype(jnp.int32)

    def body(c_ref, a_ref, b_ref, o_ref):
        o_ref[...] = (a_ref[0] + b_ref[...]).astype(o_ref.dtype)

    return pl.pallas_call(
        body, name=name,
        grid_spec=pltpu.PrefetchScalarGridSpec(
            num_scalar_prefetch=1, grid=(N_SHARD, rh // tile),
            in_specs=[pl.BlockSpec((1, 1, tile, width), lambda s, i, c: (s, c[0], i, 0)),
                      pl.BlockSpec((1, tile, width), lambda s, i, c: (s, i, 0))],
            out_specs=pl.BlockSpec((1, tile, width), lambda s, i, c: (s, i, 0))),
        out_shape=jax.ShapeDtypeStruct((N_SHARD, rh, width), BF16),
        compiler_params=_params("parallel", "parallel"),
    )(c_arr, g, got)


def _chip_exchange(ps):
    n = len(ps)

    def body(*refs):
        p_refs, out_refs, send_sems, recv_sems = refs[:n], refs[n:2 * n], refs[2 * n], refs[2 * n + 1]
        x, y, c = _place()
        me = 2 * x + y
        copies = [pltpu.make_async_remote_copy(
            src_ref=p_ref.at[2 * px + py], dst_ref=out_ref.at[me], send_sem=send_sems.at[3 * g + j],
            recv_sem=recv_sems.at[3 * g + j], device_id=(px, py, c), device_id_type=MESH)
            for g, (p_ref, out_ref) in enumerate(zip(p_refs, out_refs)) for j, (px, py) in enumerate(_other_chips(x, y))]
        for cp in copies:
            cp.start()
        for cp in copies:
            cp.wait()

    return pl.pallas_call(
        body, name="grad_chip_exchange", in_specs=[HBM_SPEC] * n, out_specs=[HBM_SPEC] * n,
        out_shape=[jax.ShapeDtypeStruct(p.shape, p.dtype) for p in ps],
        scratch_shapes=[pltpu.SemaphoreType.DMA((3 * n,)), pltpu.SemaphoreType.DMA((3 * n,))],
    )(*ps)


def _sum_chips(name, p, got):
    _, rh, width = p.shape
    tile = _div_tile(rh, 256, 16)
    n_t = rh // tile
    place = jnp.stack([2 * lax.axis_index("x") + lax.axis_index("y"), lax.axis_index("c")]).astype(jnp.int32)

    def body(pl_ref, own_ref, r0, r1, r2, r3, o_ref):
        me = pl_ref[0]
        own = own_ref[0].astype(F32)
        t = [jnp.where(me == s, own, r[0].astype(F32)) for s, r in enumerate((r0, r1, r2, r3))]
        o_ref[...] = ((t[0] + t[1]) + t[2]) + t[3]

    def slot(s):
        return pl.BlockSpec((1, tile, width), lambda i, pc: (jnp.where(pc[0] == s, (s + 1) % N_SHARD, s), i, 0))

    return pl.pallas_call(
        body, name=name,
        grid_spec=pltpu.PrefetchScalarGridSpec(
            num_scalar_prefetch=1, grid=(n_t,),
            in_specs=[pl.BlockSpec((1, tile, width), lambda i, pc: (pc[0], i, 0))] + [slot(s) for s in range(N_SHARD)],
            out_specs=pl.BlockSpec((tile, width), lambda i, pc: (pc[1] * n_t + i, 0))),
        out_shape=jax.ShapeDtypeStruct((2 * rh, width), F32), compiler_params=_params("parallel"),
    )(place, p, got, got, got, got)


def _join_halves(fulls):
    n = len(fulls)

    def body(*refs):
        f_refs, out_refs, send_sems, recv_sems = refs[:n], refs[n:2 * n], refs[2 * n], refs[2 * n + 1]
        x, y, c = _place()

        def copy(g, half):
            rh = f_refs[g].shape[0] // 2
            return pltpu.make_async_remote_copy(
                src_ref=f_refs[g].at[pl.ds(half * rh, rh), :], dst_ref=out_refs[g].at[pl.ds(half * rh, rh), :],
                send_sem=send_sems.at[g], recv_sem=recv_sems.at[g], device_id=(x, y, 1 - c), device_id_type=MESH)

        for g in range(n):
            copy(g, c).start()
        for g in range(n):
            copy(g, c).wait_send()
            copy(g, 1 - c).wait_recv()

    return pl.pallas_call(
        body, name="grad_join_halves", in_specs=[HBM_SPEC] * n, out_specs=[HBM_SPEC] * n,
        out_shape=[jax.ShapeDtypeStruct(f.shape, f.dtype) for f in fulls],
        input_output_aliases={g: g for g in range(n)},
        scratch_shapes=[pltpu.SemaphoreType.DMA((n,)), pltpu.SemaphoreType.DMA((n,))],
    )(*fulls)


def _reduce_scatter(gs):
    gs = [g.reshape(N_SHARD, 2, g.shape[1] // 2, g.shape[2]) for g in gs]
    got = _sibling_exchange(gs)
    chip_sums = [_add_sibling(f"grad_add_sibling_{i}", g, r) for i, (g, r) in enumerate(zip(gs, got))]
    landed = _chip_exchange(chip_sums)
    return _join_halves([_sum_chips(f"grad_sum_chips_{i}", p, r) for i, (p, r) in enumerate(zip(chip_sums, landed))])


def _all_reduce_small(v):
    rows = v.shape[0]

    def body(v_ref, out_ref, gather, send_sems, recv_sems):
        x, y, c = _place()
        gather[4 * x + 2 * y + c] = v_ref[...]
        flips = [(dx, dy, dc) for dx in (0, 1) for dy in (0, 1) for dc in (0, 1)][1:]
        peers = [((x + dx) % 2, (y + dy) % 2, (c + dc) % 2) for dx, dy, dc in flips]
        copies = [pltpu.make_async_remote_copy(
            src_ref=v_ref, dst_ref=gather.at[4 * x + 2 * y + c], send_sem=send_sems.at[j], recv_sem=recv_sems.at[j],
            device_id=peer, device_id_type=MESH) for j, peer in enumerate(peers)]
        for cp in copies:
            cp.start()
        for j, (px, py, pc) in enumerate(peers):
            pltpu.make_async_remote_copy(
                src_ref=v_ref, dst_ref=gather.at[4 * px + 2 * py + pc], send_sem=send_sems.at[j],
                recv_sem=recv_sems.at[j], device_id=(px, py, pc), device_id_type=MESH).wait_recv()
        for cp in copies:
            cp.wait_send()
        acc = gather[0]
        for d in range(1, 8):
            acc = acc + gather[d]
        out_ref[...] = acc

    vm = pl.BlockSpec(memory_space=pltpu.VMEM)
    return pl.pallas_call(
        body, name="small_grads_all_reduce", in_specs=[vm], out_specs=vm,
        out_shape=jax.ShapeDtypeStruct(v.shape, F32),
        scratch_shapes=[pltpu.VMEM((8, rows, LANES), F32), pltpu.SemaphoreType.DMA((7,)), pltpu.SemaphoreType.DMA((7,))],
    )(v)


def _pad_lanes(v):
    v = v.reshape(1, -1)
    return jnp.pad(v, ((0, 0), (0, -v.shape[1] % LANES)))


def _pack_small(vs, rows):
    flat = jnp.concatenate([_pad_lanes(v) for v in vs], axis=1)
    return jnp.pad(flat, ((0, 0), (0, rows * LANES - flat.shape[1]))).reshape(rows, LANES)


def _unpack_small(packed, shapes):
    flat = packed.reshape(-1)
    out, off = [], 0
    for s in shapes:
        n = 1
        for d in s:
            n *= d
        out.append(flat[off:off + n].reshape(s))
        off += n + (-n % LANES)
    return out


BIG = ("w_in", "rw_w_lora_up", "rw_a_lora_up", "w_up_rwkv", "w_up_fox", "w_out", "ple_proj", "ple_gate_w")
ROW_SHARDED = ("w_out", "ple_gate_w")
SMALL = ("norm_g", "rw_shift_mu", "rw_w0", "rw_a0", "rw_k_k", "rw_k_a", "rw_r_k", "rw_ln_g", "rw_ln_b", "fox_b_f",
         "ple_norm_g", "final_norm_g")
WEIGHTS = ("norm_g", "w_in", "rw_shift_mu", "rw_w0", "rw_w_lora_up", "rw_a0", "rw_a_lora_up", "rw_k_k", "rw_k_a",
           "rw_r_k", "rw_ln_g", "rw_ln_b", "fox_b_f", "w_up_rwkv", "w_up_fox", "w_out", "ple_proj", "ple_gate_w",
           "ple_norm_g", "final_norm_g")


def kernel(x, p, norm_g, w_in, rw_shift_mu, rw_w0, rw_w_lora_up, rw_a0, rw_a_lora_up, rw_k_k, rw_k_a, rw_r_k, rw_ln_g, rw_ln_b, fox_b_f, w_up_rwkv, w_up_fox, w_out, ple_proj, ple_gate_w, ple_norm_g, final_norm_g, loss_target, m_norm_g, m_w_in, m_rw_shift_mu, m_rw_w0, m_rw_w_lora_up, m_rw_a0, m_rw_a_lora_up, m_rw_k_k, m_rw_k_a, m_rw_r_k, m_rw_ln_g, m_rw_ln_b, m_fox_b_f, m_w_up_rwkv, m_w_up_fox, m_w_out, m_ple_proj, m_ple_gate_w, m_ple_norm_g, m_final_norm_g, v_norm_g, v_w_in, v_rw_shift_mu, v_rw_w0, v_rw_w_lora_up, v_rw_a0, v_rw_a_lora_up, v_rw_k_k, v_rw_k_a, v_rw_r_k, v_rw_ln_g, v_rw_ln_b, v_fox_b_f, v_w_up_rwkv, v_w_up_fox, v_w_out, v_ple_proj, v_ple_gate_w, v_ple_norm_g, v_final_norm_g):
    wts = dict(norm_g=norm_g, w_in=w_in, rw_shift_mu=rw_shift_mu, rw_w0=rw_w0, rw_w_lora_up=rw_w_lora_up, rw_a0=rw_a0,
               rw_a_lora_up=rw_a_lora_up, rw_k_k=rw_k_k, rw_k_a=rw_k_a, rw_r_k=rw_r_k, rw_ln_g=rw_ln_g, rw_ln_b=rw_ln_b,
               fox_b_f=fox_b_f, w_up_rwkv=w_up_rwkv, w_up_fox=w_up_fox, w_out=w_out, ple_proj=ple_proj,
               ple_gate_w=ple_gate_w, ple_norm_g=ple_norm_g, final_norm_g=final_norm_g)
    mom = dict(norm_g=m_norm_g, w_in=m_w_in, rw_shift_mu=m_rw_shift_mu, rw_w0=m_rw_w0, rw_w_lora_up=m_rw_w_lora_up,
               rw_a0=m_rw_a0, rw_a_lora_up=m_rw_a_lora_up, rw_k_k=m_rw_k_k, rw_k_a=m_rw_k_a, rw_r_k=m_rw_r_k,
               rw_ln_g=m_rw_ln_g, rw_ln_b=m_rw_ln_b, fox_b_f=m_fox_b_f, w_up_rwkv=m_w_up_rwkv, w_up_fox=m_w_up_fox,
               w_out=m_w_out, ple_proj=m_ple_proj, ple_gate_w=m_ple_gate_w, ple_norm_g=m_ple_norm_g,
               final_norm_g=m_final_norm_g)
    vel = dict(norm_g=v_norm_g, w_in=v_w_in, rw_shift_mu=v_rw_shift_mu, rw_w0=v_rw_w0, rw_w_lora_up=v_rw_w_lora_up,
               rw_a0=v_rw_a0, rw_a_lora_up=v_rw_a_lora_up, rw_k_k=v_rw_k_k, rw_k_a=v_rw_k_a, rw_r_k=v_rw_r_k,
               rw_ln_g=v_rw_ln_g, rw_ln_b=v_rw_ln_b, fox_b_f=v_fox_b_f, w_up_rwkv=v_w_up_rwkv, w_up_fox=v_w_up_fox,
               w_out=v_w_out, ple_proj=v_ple_proj, ple_gate_w=v_ple_gate_w, ple_norm_g=v_ple_norm_g,
               final_norm_g=v_final_norm_g)

    t, d = x.shape[1], x.shape[2]
    cw = rw_w0.shape[1]
    lr = rw_w_lora_up.shape[1]
    fh = fox_b_f.shape[1]
    fw = fh * HEAD
    rw_cols = 4 * cw + 2 * lr
    fox_cols = 4 * fw + fh
    assert 2 * lr == LANES and cw % LANES == 0 and fw % LANES == 0 and fh <= LANES
    xs = x[0]
    ps = p[0, 0]
    tgt = loss_target[0]

    groups = {}
    for n in BIG:
        groups.setdefault(wts[n].shape[2], []).append(n)
    groups = list(groups.values())
    slabs16 = []
    for gi, names in enumerate(groups):
        slab = jnp.concatenate([wts[n][0] for n in names], axis=0)
        rows, width = slab.shape
        tile_c = _div_tile(rows, 256, 32)
        slabs16 += _rows(f"weights_to_bf16_{gi}", lambda i, a: (a,), rows, tile_c, [(slab, _row_spec(tile_c, width))],
                         [(width, BF16)])
    gathered = [_place_own_shard(f"weights_place_own_{gi}", g, own)
                for gi, (g, own) in enumerate(zip(_all_gather_shards(slabs16), slabs16))]
    full = {}
    for names, g in zip(groups, gathered):
        off = 0
        for n in names:
            r = wts[n].shape[1]
            part = g[:, off:off + r, :]
            full[n] = (part.reshape(N_SHARD * r, -1) if n in ROW_SHARDED
                       else jnp.concatenate([part[s] for s in range(N_SHARD)], axis=1))
            off += r
    w_rw = full["w_in"][:, :rw_cols]
    w_fox = jnp.pad(full["w_in"][:, rw_cols:rw_cols + fox_cols], ((0, 0), (0, LANES - fh)))
    w_gate = full["w_in"][:, rw_cols + fox_cols:]
    wup_pad = jnp.pad(full["rw_w_lora_up"], ((0, lr), (0, 0)))
    aup_pad = jnp.pad(full["rw_a_lora_up"], ((lr, 0), (0, 0)))
    b_pad = _pad_lanes(fox_b_f)
    r_k_row = rw_r_k.reshape(1, cw)
    gf_row = final_norm_g.reshape(1, d)

    tile = _tile(t, (256, 128, 64, 32, 16, 8))
    tile_s = _tile(t, (128, 64, 32, 16, 8))
    n_s = t // tile_s
    full2 = lambda a: (a, _full_spec(a.shape))

    (h,) = _rows("norm1", lambda i, a, g: (_rms(a, g),), t, tile, [(xs, _row_spec(tile, d)), full2(norm_g)], [(d, BF16)])
    z_rw = _matmul("proj_rw", h, w_rw)
    z_fox = _matmul("proj_fox", h, w_fox)
    z_gate = _matmul("proj_gate", h, w_gate)

    pre_consts = [full2(rw_shift_mu), full2(rw_w0), full2(rw_a0), full2(wup_pad), full2(aup_pad), full2(rw_k_k),
                  full2(rw_k_a)]

    def pre_fwd(i, z, prev8, *consts):
        return _rw_pre(z, _shifted(i, z, prev8), *consts, cw=cw)

    r_, w_, k_, v_, kk_, a_, g_ = _rows(
        "rwkv_pre", pre_fwd, t, tile_s,
        [(z_rw, _row_spec(tile_s, rw_cols)), (z_rw, _prev_rows_spec(tile_s, rw_cols))] + pre_consts, [(cw, F32)] * 7)
    y_scan, states, tinvs = _scan_fwd(r_, w_, k_, v_, kk_, a_)
    post_consts = [full2(rw_ln_g), full2(rw_ln_b), full2(r_k_row)]
    post_rows = lambda *arrs: [(a, _row_spec(tile_s, cw)) for a in arrs]
    (y_rw,) = _rows("rwkv_post", lambda i, *a: (_rw_post(*a, cw=cw),), t, tile_s,
                    post_rows(y_scan, r_, k_, v_, g_) + post_consts, [(cw, BF16)])

    c_fox = _fox_cumsum(z_fox, b_pad, fw=fw, fh=fh)
    tq = _fox_tile(t)
    n_pair_f = fw // LANES
    head_rows = lambda a: a.T.reshape(n_pair_f, 2, t // tq, tq).transpose(0, 2, 1, 3)
    head_cols = lambda a: a.transpose(0, 2, 1, 3).reshape(fh, t).T
    ct_fox = head_rows(c_fox[:, :fh])
    o_fox, lse_fox, y_fox = _fox_fwd(z_fox, c_fox, ct_fox, fw=fw)

    u_rw = _matmul("up_rwkv", y_rw, full["w_up_rwkv"])
    u_fox = _matmul("up_fox", y_fox, full["w_up_fox"])
    (merged,) = _rows("merge", lambda i, zg, a, b: (_merge(zg, a, b, d=d),), t, tile,
                      [(z_gate, _row_spec(tile, 2 * d)), (u_rw, _row_spec(tile, d)), (u_fox, _row_spec(tile, d))],
                      [(d, BF16)])
    x1 = _matmul("out_proj", merged, full["w_out"], add=xs)
    (n2,) = _rows("norm2", lambda i, a, g: (_rms(a, g),), t, tile, [(x1, _row_spec(tile, d)), full2(ple_norm_g)],
                  [(d, BF16)])
    gl = _matmul("ple_gate", n2, full["ple_gate_w"])
    ple = _matmul("ple_proj", ps, full["ple_proj"])

    def head_bwd(i, x1_t, ple_t, gl_t, gf, tg):
        loss, vjp = jax.vjp(lambda a, b, cc, g: _head_loss(a, b, cc, g, tg), x1_t, ple_t, gl_t, gf)
        dx1, dple, dgl, dgf = vjp(jnp.ones((1, 1), F32))
        return dx1, dple, dgl, jnp.broadcast_to(loss, (1, LANES)), dgf

    dx2, dple, dgl, loss_row, d_gf = _rows(
        "loss_head", head_bwd, t, tile_s,
        [(x1, _row_spec(tile_s, d)), (ple, _row_spec(tile_s, d)), (gl, _row_spec(tile_s, d)), full2(gf_row),
         (tgt, _row_spec(tile_s, d))],
        [(d, F32), (d, BF16), (d, BF16)], [(1, LANES), (1, d)])

    g_ple_proj = _matmul("d_ple_proj", ps, dple, ta=True)
    g_ple_gate = _matmul("d_ple_gate_w", n2, dgl, ta=True)
    dn2 = _matmul("d_n2", dgl, full["ple_gate_w"], tb=True)

    def norm_bwd(i, a, g, dh, res):
        _, vjp = jax.vjp(_rms, a, g)
        da, dg = vjp(dh)
        return res + da, dg

    dx1, d_g2 = _rows("norm2_bwd", norm_bwd, t, tile_s,
                      [(x1, _row_spec(tile_s, d)), full2(ple_norm_g), (dn2, _row_spec(tile_s, d)),
                       (dx2, _row_spec(tile_s, d))], [(d, F32)], [(1, d)])
    g_w_out = _matmul("d_w_out", merged, dx1, ta=True)
    dmerged = _matmul("d_merged", dx1, full["w_out"], tb=True)

    def merge_bwd(i, zg, a, b, dm):
        _, vjp = jax.vjp(functools.partial(_merge, d=d), zg, a, b)
        return vjp(dm)

    dz_gate, du_rw, du_fox = _rows(
        "merge_bwd", merge_bwd, t, tile_s,
        [(z_gate, _row_spec(tile_s, 2 * d)), (u_rw, _row_spec(tile_s, d)), (u_fox, _row_spec(tile_s, d)),
         (dmerged, _row_spec(tile_s, d))], [(2 * d, BF16), (d, BF16), (d, BF16)])
    g_up_rw = _matmul("d_w_up_rwkv", y_rw, du_rw, ta=True)
    g_up_fox = _matmul("d_w_up_fox", y_fox, du_fox, ta=True)
    dy_rw = _matmul("d_y_rwkv", du_rw, full["w_up_rwkv"], tb=True)
    dy_fox = _matmul("d_y_fox", du_fox, full["w_up_fox"], tb=True)

    def post_bwd(i, y, r, k, v, g, ln_g, ln_b, r_k, dy):
        _, vjp = jax.vjp(functools.partial(_rw_post, cw=cw), y, r, k, v, g, ln_g, ln_b, r_k)
        return vjp(dy)

    dys, dr1, dk1, dv1, dg1, d_ln_g, d_ln_b, d_r_k = _rows(
        "rwkv_post_bwd", post_bwd, t, tile_s,
        post_rows(y_scan, r_, k_, v_, g_) + post_consts + post_rows(dy_rw), [(cw, F32)] * 5, [(1, cw)] * 3)
    dr2, dw2, dk2, dv2, dkk2, da2 = _scan_bwd(r_, w_, k_, v_, kk_, a_, states, tinvs, dys)

    def pre_bwd(i, z, prev8, mu, w0, a0, wup, aup, k_k, k_a, dr_a, dr_b, dk_a, dk_b, dv_a, dv_b, dw, dkk, da, dg):
        zp = _shifted(i, z, prev8)
        _, vjp = jax.vjp(functools.partial(_rw_pre, cw=cw), z, zp, mu, w0, a0, wup, aup, k_k, k_a)
        dz, dzp, dmu, dw0, da0, dwup, daup, dk_k, dk_a = vjp((dr_a + dr_b, dw, dk_a + dk_b, dv_a + dv_b, dkk, da, dg))
        row = lax.broadcasted_iota(jnp.int32, dz.shape, 0)
        dz = dz + jnp.where(row < tile_s - 1, pltpu.roll(dzp, tile_s - 1, 0), 0.0)
        first = jnp.where(lax.broadcasted_iota(jnp.int32, (8, dz.shape[1]), 0) == 0, _row_of(dzp, 0), 0.0)
        return dz, first, dmu, dw0, da0, dwup, daup, dk_k, dk_a

    def pre_bwd_call():
        n_in = 2 + len(pre_consts) + 10
        ins = ([(z_rw, _row_spec(tile_s, rw_cols)), (z_rw, _prev_rows_spec(tile_s, rw_cols))] + pre_consts
               + post_rows(dr1, dr2, dk1, dk2, dv1, dv2, dw2, dkk2, da2, dg1))

        def body(*refs):
            i = pl.program_id(0)
            vals = pre_bwd(i, *[r[...] for r in refs[:n_in]])
            refs[n_in][...] = vals[0]
            refs[n_in + 1][...] = vals[1]
            for r, v in zip(refs[n_in + 2:], vals[2:]):
                @pl.when(i == 0)
                def _(r=r, v=v):
                    r[...] = v

                @pl.when(i > 0)
                def _(r=r, v=v):
                    r[...] += v

        acc_shapes = [(1, rw_cols), (1, cw), (1, cw), (LANES, cw), (LANES, cw), (1, cw), (1, cw)]
        return pl.pallas_call(
            body, name="rwkv_pre_bwd", grid=(n_s,), in_specs=[s for _, s in ins],
            out_specs=[_row_spec(tile_s, rw_cols), pl.BlockSpec((8, rw_cols), lambda i: (i, 0))]
            + [_full_spec(s) for s in acc_shapes],
            out_shape=[jax.ShapeDtypeStruct((t, rw_cols), F32), jax.ShapeDtypeStruct((8 * n_s, rw_cols), F32)]
            + [jax.ShapeDtypeStruct(s, F32) for s in acc_shapes],
            compiler_params=_params("arbitrary"),
        )(*[a for a, _ in ins])

    dz_main, dz_first, d_mu, d_w0, d_a0, d_wup, d_aup, d_k_k, d_k_a = pre_bwd_call()

    def add_next_row(i, dz, nxt8):
        row = lax.broadcasted_iota(jnp.int32, dz.shape, 0)
        carry = jnp.where(i < n_s - 1, _row_of(nxt8, 0), 0.0)
        return (dz + jnp.where(row == tile_s - 1, carry, 0.0),)

    (dz_rw,) = _rows("rwkv_shift_bwd", add_next_row, t, tile_s,
                     [(dz_main, _row_spec(tile_s, rw_cols)), (dz_first, _next_rows_spec(tile_s, rw_cols, n_s))],
                     [(rw_cols, BF16)])

    def fox_post_bwd(i, o, g, dy):
        _, vjp = jax.vjp(lambda oo, gg: oo * (gg * jax.nn.sigmoid(gg)), o, g)
        do, dg = vjp(dy)
        return do, _head_sum(do * o, _head_matrix(fw)), dg

    do_fox, dd_fox, dg_fox = _rows(
        "fox_post_bwd", fox_post_bwd, t, tile_s,
        [(o_fox, _row_spec(tile_s, fw)), (z_fox, _row_spec(tile_s, fw, 3)), (dy_fox, _row_spec(tile_s, fw))],
        [(fw, F32), (fw, F32), (fw, BF16)])
    dq_f, dk_f, dv_f, dcq, dck = _fox_bwd(z_fox, do_fox, c_fox, ct_fox, head_rows(lse_fox[:, ::HEAD]),
                                          head_rows(dd_fox[:, ::HEAD]), fw=fw)
    dc = head_cols(dcq) + dck.reshape(t, n_pair_f, LANES)[:, :, :2].reshape(t, fh)
    dfl, d_bf = _fox_cumsum_bwd(z_fox, b_pad, jnp.pad(dc, ((0, 0), (0, LANES - fh))), fw=fw, fh=fh)
    dz_fox = jnp.concatenate([dq_f, dk_f, dv_f, dg_fox, dfl], axis=1)

    g_w_rw = _matmul("d_w_in_rw", h, dz_rw, ta=True)
    g_w_fox = _matmul("d_w_in_fox", h, dz_fox, ta=True)
    g_w_gate = _matmul("d_w_in_gate", h, dz_gate, ta=True)
    dh = _matmul("d_h_rw", dz_rw, w_rw, tb=True)
    dh = _matmul("d_h_fox", dz_fox, w_fox, tb=True, add=dh)
    dh = _matmul("d_h_gate", dz_gate, w_gate, tb=True, add=dh)
    grad_x, d_g1 = _rows("norm1_bwd", norm_bwd, t, tile_s,
                         [(xs, _row_spec(tile_s, d)), full2(norm_g), (dh, _row_spec(tile_s, d)),
                          (dx1, _row_spec(tile_s, d))], [(d, F32)], [(1, d)])

    g_full = {
        "w_in": jnp.concatenate([g_w_rw, g_w_fox[:, :fox_cols], g_w_gate], axis=1),
        "rw_w_lora_up": d_wup[:lr], "rw_a_lora_up": d_aup[lr:], "w_up_rwkv": g_up_rw, "w_up_fox": g_up_fox,
        "w_out": g_w_out, "ple_proj": g_ple_proj, "ple_gate_w": g_ple_gate,
    }
    def by_shard(n):
        g = g_full[n]
        if n in ROW_SHARDED:
            return g.reshape(N_SHARD, g.shape[0] // N_SHARD, g.shape[1])
        return jnp.stack(jnp.split(g, N_SHARD, axis=1))

    reduced = _reduce_scatter([jnp.concatenate([by_shard(n) for n in names], axis=1) for names in groups])
    grads = {}
    for names, g in zip(groups, reduced):
        off = 0
        for n in names:
            r = wts[n].shape[1]
            grads[n] = g[off:off + r][None]
            off += r

    small_parts = dict(norm_g=d_g1, rw_shift_mu=d_mu, rw_w0=d_w0, rw_a0=d_a0, rw_k_k=d_k_k, rw_k_a=d_k_a, rw_r_k=d_r_k,
                       rw_ln_g=d_ln_g, rw_ln_b=d_ln_b, fox_b_f=d_bf[:, :fh], ple_norm_g=d_g2, final_norm_g=d_gf)
    n_small = sum(-(-wts[n].size // LANES) for n in SMALL)
    small_rows = -(-n_small // 8) * 8
    small_shapes = [wts[n].shape for n in SMALL]
    g_small = _all_reduce_small(_pack_small([small_parts[n] for n in SMALL], small_rows))
    for n, g in zip(SMALL, _unpack_small(g_small, small_shapes)):
        grads[n] = g

    delta, new_m, new_v = {}, {}, {}
    for n in BIG:
        delta[n], new_m[n], new_v[n] = _adamw("adamw_" + n, wts[n], grads[n], mom[n], vel[n])
    packed = lambda src: _pack_small([src[n] for n in SMALL], small_rows)[None]
    for store, out in zip((delta, new_m, new_v), _adamw("adamw_small", packed(wts), g_small[None], packed(mom), packed(vel))):
        for n, a in zip(SMALL, _unpack_small(out[0], small_shapes)):
            store[n] = a

    loss = lax.psum(loss_row[0, 0], ("x", "y", "c"))
    return (loss, grad_x[None], *[grads[n] for n in WEIGHTS], *[delta[n] for n in WEIGHTS],
            *[new_m[n] for n in WEIGHTS], *[new_v[n] for n in WEIGHTS])
```

```python
import functools

import jax
import jax.numpy as jnp
from jax import lax
from jax.experimental import pallas as pl
from jax.experimental.pallas import tpu as pltpu

F32 = jnp.float32
BF16 = jnp.bfloat16
HIGHEST = lax.Precision.HIGHEST
SCAN_PREC = lax.Precision.HIGH
MESH = pl.DeviceIdType.MESH

LANES = 128
HEAD = 64
NORM_EPS = 1e-6
GN_EPS = 64e-5
ADAM_LR = 0.001
ADAM_B1 = 0.9
ADAM_B2 = 0.999
ADAM_EPS = 1e-08
ADAM_WD = 0.01
ADAM_STEP = 10
N_SHARD = 4
VMEM_LIMIT = 56 * 1024 * 1024
PAIRS_PER_STEP = 4


def _params(*sem):
    return pltpu.CompilerParams(dimension_semantics=sem, vmem_limit_bytes=VMEM_LIMIT)


def _tile(n, cands):
    for c in cands:
        if c <= n and n % c == 0:
            return c
    return n


def _div_tile(n, cap, mult):
    return max(c for c in range(mult, min(n, cap) + 1, mult) if n % c == 0)


_ROW_TILES = (512, 256, 128, 64, 32, 16, 8)


def _dot(a, b, prec=None):
    return lax.dot_general(a, b, (((1,), (0,)), ((), ())), precision=prec, preferred_element_type=F32)


def _dot_nt(a, b, prec=None):
    return lax.dot_general(a, b, (((1,), (1,)), ((), ())), precision=prec, preferred_element_type=F32)


def _dot_tn(a, b, prec=None):
    return lax.dot_general(a, b, (((0,), (0,)), ((), ())), precision=prec, preferred_element_type=F32)


@jax.custom_vjp
def _bdot(x, w):
    return _dot(x.astype(BF16), w.astype(BF16))


def _bdot_fwd(x, w):
    return _bdot(x, w), (x, w)


def _bdot_bwd(res, ct):
    x, w = res
    return _dot_nt(ct.astype(BF16), w.astype(BF16)), _dot_tn(x.astype(BF16), ct.astype(BF16))


_bdot.defvjp(_bdot_fwd, _bdot_bwd)


def _head_matrix(width):
    c = lax.broadcasted_iota(jnp.int32, (width, LANES), 0)
    h = lax.broadcasted_iota(jnp.int32, (width, LANES), 1)
    return (c // HEAD == h).astype(F32)


def _head_sum(x, e):
    return _dot_nt(_dot(x, e, SCAN_PREC), e, SCAN_PREC)


def _softplus(x):
    return jnp.maximum(x, 0.0) + jnp.log1p(jnp.exp(-jnp.abs(x)))


def _lane_pick(x, idx):
    lane = lax.broadcasted_iota(jnp.int32, x.shape, 1)
    return jnp.sum(jnp.where(lane == idx, x, 0.0), axis=1, keepdims=True)


def _matmul(name, a, b, *, ta=False, tb=False, add=None, out_dtype=F32):
    m, k = (a.shape[1], a.shape[0]) if ta else a.shape
    n = b.shape[0] if tb else b.shape[1]
    tm = _tile(m, (1024, 512, 256, 128))
    tn = _tile(n, (1408, 1024, 768, 640, 512, 384, 256, 128))
    tk = _tile(k, (1408, 1024, 768, 640, 512, 384, 256, 128, 64, 32, 16))
    nk = k // tk
    dims = (((0 if ta else 1,), (1 if tb else 0,)), ((), ()))

    def body(*refs):
        a_ref, b_ref = refs[0], refs[1]
        o_ref, acc_ref = refs[-2], refs[-1]
        kk = pl.program_id(2)

        @pl.when(kk == 0)
        def _():
            acc_ref[...] = jnp.zeros_like(acc_ref)

        acc_ref[...] += lax.dot_general(a_ref[...].astype(BF16), b_ref[...].astype(BF16), dims,
                                        preferred_element_type=F32)

        @pl.when(kk == nk - 1)
        def _():
            r = acc_ref[...]
            if add is not None:
                r = r + refs[2][...].astype(F32)
            o_ref[...] = r.astype(o_ref.dtype)

    a_spec = pl.BlockSpec((tk, tm), lambda i, j, kk: (kk, i)) if ta else pl.BlockSpec((tm, tk), lambda i, j, kk: (i, kk))
    b_spec = pl.BlockSpec((tn, tk), lambda i, j, kk: (j, kk)) if tb else pl.BlockSpec((tk, tn), lambda i, j, kk: (kk, j))
    o_spec = pl.BlockSpec((tm, tn), lambda i, j, kk: (i, j))
    ins, specs = [a, b], [a_spec, b_spec]
    if add is not None:
        ins.append(add)
        specs.append(o_spec)
    return pl.pallas_call(
        body, name=name, grid=(m // tm, n // tn, nk), in_specs=specs, out_specs=o_spec,
        out_shape=jax.ShapeDtypeStruct((m, n), out_dtype),
        scratch_shapes=[pltpu.VMEM((tm, tn), F32)],
        compiler_params=_params("parallel", "parallel", "arbitrary"),
    )(*ins)


def _rows(name, fn, n_rows, tile, ins, outs, accs=()):
    n_in, n_out = len(ins), len(outs)

    def body(*refs):
        i = pl.program_id(0)
        vals = fn(i, *[r[...] for r in refs[:n_in]])
        for r, v in zip(refs[n_in:n_in + n_out], vals[:n_out]):
            r[...] = v.astype(r.dtype)
        for r, v in zip(refs[n_in + n_out:], vals[n_out:]):
            @pl.when(i == 0)
            def _(r=r, v=v):
                r[...] = v

            @pl.when(i > 0)
            def _(r=r, v=v):
                r[...] += v

    out_specs = [pl.BlockSpec((tile, w), lambda i: (i, 0)) for w, _ in outs]
    out_specs += [pl.BlockSpec(s, lambda i: (0, 0)) for s in accs]
    out_shape = [jax.ShapeDtypeStruct((n_rows, w), d) for w, d in outs]
    out_shape += [jax.ShapeDtypeStruct(s, F32) for s in accs]
    return pl.pallas_call(
        body, name=name, grid=(n_rows // tile,), in_specs=[s for _, s in ins], out_specs=out_specs,
        out_shape=out_shape, compiler_params=_params("arbitrary"),
    )(*[a for a, _ in ins])


def _row_spec(tile, width, col=0):
    return pl.BlockSpec((tile, width), lambda i: (i, col))


def _full_spec(shape):
    return pl.BlockSpec(shape, lambda i: (0,) * len(shape))


def _prev_rows_spec(tile, width):
    return pl.BlockSpec((8, width), lambda i: (jnp.maximum(i * (tile // 8) - 1, 0), 0))


def _next_rows_spec(tile, width, n_tiles):
    return pl.BlockSpec((8, width), lambda i: (jnp.minimum(i + 1, n_tiles - 1), 0))


def _row_of(x8, idx):
    r = lax.broadcasted_iota(jnp.int32, x8.shape, 0)
    return jnp.sum(jnp.where(r == idx, x8, 0.0), axis=0, keepdims=True)


def _rms(x, g):
    return x * lax.rsqrt(jnp.mean(x * x, axis=-1, keepdims=True) + NORM_EPS) * g


def _shifted(i, z, prev8):
    first = jnp.where(i > 0, _row_of(prev8, 7), 0.0)
    row = lax.broadcasted_iota(jnp.int32, z.shape, 0)
    return jnp.where(row == 0, first, pltpu.roll(z, 1, 0))


def _rw_pre(z, zp, mu, w0, a0, wup, aup, k_k, k_a, *, cw):
    zs = z + (zp - z) * mu
    r, k, v, g = (zs[:, j * cw:(j + 1) * cw] for j in range(4))
    lo = zs[:, 4 * cw:4 * cw + LANES]
    w_raw = w0 + _bdot(jnp.tanh(lo), wup)
    decay = jnp.exp(-jnp.exp(-_softplus(-w_raw) - 0.5))
    a = jax.nn.sigmoid(a0 + _bdot(lo, aup))
    e = _head_matrix(cw)
    kk = k * k_k
    kk = kk / jnp.maximum(jnp.sqrt(_head_sum(kk * kk, e)), 1e-12)
    k_mod = k * (1.0 + (a - 1.0) * k_a)
    return r, decay, k_mod, v, kk, a, g


def _rw_post(y, r, k_mod, v, g, ln_g, ln_b, r_k, *, cw):
    e = _head_matrix(cw)
    mu = _head_sum(y, e) * (1.0 / HEAD)
    d = y - mu
    var = _head_sum(d * d, e) * (1.0 / HEAD)
    yn = d * lax.rsqrt(var + GN_EPS) * ln_g + ln_b
    bonus = _head_sum(r * k_mod * r_k, e) * v
    return (yn + bonus) * (g * jax.nn.sigmoid(g))


def _merge(zg, u_rw, u_fox, *, d):
    return jax.nn.sigmoid(zg[:, :d]) * u_rw + jax.nn.sigmoid(zg[:, d:]) * u_fox


def _head_loss(x1, ple, gl, gf, tgt):
    x2 = x1 + ple * jax.nn.sigmoid(gl)
    err = _rms(x2, gf) - tgt
    return 0.5 * jnp.sum(jnp.mean(err * err, axis=-1, keepdims=True), axis=0, keepdims=True)


def _eliminate(lo):
    n, c, _ = lo.shape
    ri = lax.broadcasted_iota(jnp.int32, (n, c, c), 1)
    ci = lax.broadcasted_iota(jnp.int32, (n, c, c), 2)
    x = (ri == ci).astype(F32)
    for s in range(c - 1):
        col = jnp.sum(jnp.where(ci == s, lo, 0.0), axis=2, keepdims=True)
        row = jnp.sum(jnp.where(ri == s, x, 0.0), axis=1, keepdims=True)
        x = x - col * row
    return x


def _batched(a, b, ca, cb):
    return lax.dot_general(a, b, (((ca,), (cb,)), ((0,), (0,))), precision=SCAN_PREC, preferred_element_type=F32)


@jax.custom_vjp
def _unit_lower_inverse(lo, known):
    return _eliminate(lo) if known is None else known


def _uli_fwd(lo, known):
    x = _unit_lower_inverse(lo, known)
    return x, (x, known)


def _uli_bwd(res, dx):
    x, known = res
    dlo = -_batched(_batched(x, dx, 1, 1), x, 2, 2)
    return dlo, (None if known is None else jnp.zeros_like(known))


_unit_lower_inverse.defvjp(_uli_fwd, _uli_bwd)


def _rwkv_chunk(s0, r, w, k, v, kk, a, *, c, tinv_known=None):
    pairs = range(len(s0))
    lane = lax.broadcasted_iota(jnp.int32, (1, LANES), 1)
    heads = (lane < HEAD, lane >= HEAD)
    ti = lax.broadcasted_iota(jnp.int32, (c, c), 0)
    si = lax.broadcasted_iota(jnp.int32, (c, c), 1)
    incl = si <= ti
    strict = si < ti
    tri = incl.astype(F32)
    logw = [jnp.log(w[p]) for p in pairs]
    cum = [_dot(tri, logw[p], HIGHEST) for p in pairs]
    cum_end = [jnp.sum(logw[p], axis=0, keepdims=True) for p in pairs]
    g_inv = [jnp.exp(-cum[p]) for p in pairs]
    to_end = [jnp.exp(cum_end[p] - cum[p]) for p in pairs]
    b = [kk[p] * a[p] for p in pairs]
    beta = [b[p] * g_inv[p] for p in pairs]
    kap = [kk[p] * jnp.exp(cum[p] - logw[p]) for p in pairs]
    kt = [k[p] * g_inv[p] for p in pairs]
    rt = [r[p] * jnp.exp(cum[p]) for p in pairs]
    lhs = [jnp.concatenate([jnp.where(m, x[p], 0.0) for x in (kap, rt) for m in heads], axis=0) for p in pairs]
    vs_beta = [_dot_nt(lhs[p], beta[p], None) for p in pairs]
    vs_kt = [_dot_nt(lhs[p], kt[p], None) for p in pairs]
    strict2 = jnp.concatenate([strict, strict], axis=0)
    incl2 = jnp.concatenate([incl, incl], axis=0)
    lo = [jnp.where(strict2, vs_beta[p][:2 * c], 0.0) for p in pairs]
    mm = [jnp.where(strict2, vs_kt[p][:2 * c], 0.0) for p in pairs]
    arb = [jnp.where(incl2, vs_beta[p][2 * c:], 0.0) for p in pairs]
    ark = [jnp.where(incl2, vs_kt[p][2 * c:], 0.0) for p in pairs]
    per_head = lambda xs: jnp.concatenate([xs[p][h * c:(h + 1) * c][None] for p in pairs for h in (0, 1)])
    tinv = _unit_lower_inverse(per_head(lo), None if tinv_known is None else per_head(tinv_known))
    tinv = [jnp.concatenate([tinv[2 * p], tinv[2 * p + 1]], axis=0) for p in pairs]
    both = lambda x: jnp.where(heads[0], x[:c], x[c:])
    vs_s = [_dot_nt(jnp.concatenate([kap[p], rt[p]], axis=0), s0[p], None) for p in pairs]
    rhs = [vs_s[p][:c] + both(_dot(mm[p], v[p], None)) for p in pairs]
    u = [-both(_dot(tinv[p], rhs[p], None)) for p in pairs]
    y = [vs_s[p][c:] + both(_dot(arb[p], u[p], None) + _dot(ark[p], v[p], None)) for p in pairs]
    rr = lax.broadcasted_iota(jnp.int32, (LANES, LANES), 0) < HEAD
    cc = lax.broadcasted_iota(jnp.int32, (LANES, LANES), 1) < HEAD
    ds = [_dot_tn(jnp.concatenate([u[p], v[p]], axis=0),
                  jnp.concatenate([b[p] * to_end[p], k[p] * to_end[p]], axis=0), None) for p in pairs]
    s1 = [s0[p] * jnp.exp(cum_end[p]) + jnp.where(rr == cc, ds[p], 0.0) for p in pairs]
    return tuple(y), tuple(s1), tuple(tinv)


def _scan_tiles(t, n_pair):
    return _tile(t, (32, 16, 8)), _tile(t, (512, 256, 128, 64, 32)), _tile(n_pair, (PAIRS_PER_STEP, 2, 1))


def _scan_fwd(r, w, k, v, kk, a):
    t, width = r.shape
    c, tb, npb = _scan_tiles(t, width // LANES)
    n_grp, n_blk, n_cb = width // (LANES * npb), t // tb, tb // c

    def body(r_ref, w_ref, k_ref, v_ref, kk_ref, a_ref, y_ref, st_ref, ti_ref, s_scr):
        @pl.when(pl.program_id(1) == 0)
        def _():
            s_scr[...] = jnp.zeros_like(s_scr)

        def chunk(j, carry):
            sl = pl.ds(pl.multiple_of(j * c, c), c)
            lanes = [pl.ds(q * LANES, LANES) for q in range(npb)]
            s0 = tuple(s_scr[q] for q in range(npb))
            cols = lambda ref: tuple(ref[sl, ln] for ln in lanes)
            y, s1, tinv = _rwkv_chunk(s0, cols(r_ref), cols(w_ref), cols(k_ref), cols(v_ref), cols(kk_ref), cols(a_ref),
                                      c=c)
            for q, ln in enumerate(lanes):
                st_ref[q, j] = s0[q]
                ti_ref[q, j] = tinv[q]
                y_ref[sl, ln] = y[q]
                s_scr[q] = s1[q]
            return carry

        lax.fori_loop(0, n_cb, chunk, 0)

    blk = pl.BlockSpec((tb, npb * LANES), lambda p, i: (i, p))
    return pl.pallas_call(
        body, name="rwkv_scan_fwd", grid=(n_grp, n_blk), in_specs=[blk] * 6,
        out_specs=[blk, pl.BlockSpec((npb, n_cb, LANES, LANES), lambda p, i: (p, i, 0, 0)),
                   pl.BlockSpec((npb, n_cb, 2 * c, c), lambda p, i: (p, i, 0, 0))],
        out_shape=[jax.ShapeDtypeStruct((t, width), F32),
                   jax.ShapeDtypeStruct((width // LANES, t // c, LANES, LANES), F32),
                   jax.ShapeDtypeStruct((width // LANES, t // c, 2 * c, c), F32)],
        scratch_shapes=[pltpu.VMEM((npb, LANES, LANES), F32)],
        compiler_params=_params("arbitrary", "arbitrary"),
    )(r, w, k, v, kk, a)


def _scan_bwd(r, w, k, v, kk, a, st, ti, dy):
    t, width = r.shape
    c, tb, npb = _scan_tiles(t, width // LANES)
    n_grp, n_blk, n_cb = width // (LANES * npb), t // tb, tb // c

    def body(r_ref, w_ref, k_ref, v_ref, kk_ref, a_ref, st_ref, ti_ref, dy_ref,
             dr_ref, dw_ref, dk_ref, dv_ref, dkk_ref, da_ref, ds_scr):
        @pl.when(pl.program_id(1) == 0)
        def _():
            ds_scr[...] = jnp.zeros_like(ds_scr)

        def chunk(jj, carry):
            j = n_cb - 1 - jj
            sl = pl.ds(pl.multiple_of(j * c, c), c)
            lanes = [pl.ds(q * LANES, LANES) for q in range(npb)]
            cols = lambda ref: tuple(ref[sl, ln] for ln in lanes)
            args = (tuple(st_ref[q, j] for q in range(npb)), cols(r_ref), cols(w_ref), cols(k_ref), cols(v_ref),
                    cols(kk_ref), cols(a_ref))
            known = tuple(ti_ref[q, j] for q in range(npb))
            _, vjp = jax.vjp(lambda *xs: _rwkv_chunk(*xs, c=c, tinv_known=known)[:2], *args)
            grads = vjp((cols(dy_ref), tuple(ds_scr[q] for q in range(npb))))
            for q, ln in enumerate(lanes):
                ds_scr[q] = grads[0][q]
                for ref, g in zip((dr_ref, dw_ref, dk_ref, dv_ref, dkk_ref, da_ref), grads[1:]):
                    ref[sl, ln] = g[q]
            return carry

        lax.fori_loop(0, n_cb, chunk, 0)

    blk = pl.BlockSpec((tb, npb * LANES), lambda p, i: (n_blk - 1 - i, p))
    stb = pl.BlockSpec((npb, n_cb, LANES, LANES), lambda p, i: (p, n_blk - 1 - i, 0, 0))
    tib = pl.BlockSpec((npb, n_cb, 2 * c, c), lambda p, i: (p, n_blk - 1 - i, 0, 0))
    return pl.pallas_call(
        body, name="rwkv_scan_bwd", grid=(n_grp, n_blk), in_specs=[blk] * 6 + [stb, tib, blk], out_specs=[blk] * 6,
        out_shape=[jax.ShapeDtypeStruct((t, width), F32)] * 6,
        scratch_shapes=[pltpu.VMEM((npb, LANES, LANES), F32)],
        compiler_params=_params("arbitrary", "arbitrary"),
    )(r, w, k, v, kk, a, st, ti, dy)


NEG = -1e30


def _fox_cumsum(zf, b_pad, *, fw, fh):
    t = zf.shape[0]
    tile = _tile(t, (256, 128, 64, 32, 16, 8))

    def body(fl_ref, b_ref, c_ref, carry):
        @pl.when(pl.program_id(0) == 0)
        def _():
            carry[...] = jnp.zeros_like(carry)

        lane = lax.broadcasted_iota(jnp.int32, (tile, LANES), 1)
        logf = jnp.where(lane < fh, -_softplus(-(fl_ref[...] + b_ref[...])), 0.0)
        ri = lax.broadcasted_iota(jnp.int32, (tile, tile), 0)
        ci = lax.broadcasted_iota(jnp.int32, (tile, tile), 1)
        c_ref[...] = carry[...] + _dot((ci <= ri).astype(F32), logf, HIGHEST)
        carry[...] += jnp.sum(logf, axis=0, keepdims=True)

    return pl.pallas_call(
        body, name="fox_cumsum", grid=(t // tile,),
        in_specs=[_row_spec(tile, LANES, 4 * fw // LANES), _full_spec((1, LANES))],
        out_specs=_row_spec(tile, LANES), out_shape=jax.ShapeDtypeStruct((t, LANES), F32),
        scratch_shapes=[pltpu.VMEM((1, LANES), F32)], compiler_params=_params("arbitrary"),
    )(zf, b_pad)


def _fox_cumsum_bwd(zf, b_pad, dc, *, fw, fh):
    t = zf.shape[0]
    tile = _tile(t, (256, 128, 64, 32, 16, 8))
    n = t // tile

    def body(fl_ref, b_ref, dc_ref, dfl_ref, db_ref, carry):
        i = pl.program_id(0)

        @pl.when(i == 0)
        def _():
            carry[...] = jnp.zeros_like(carry)
            db_ref[...] = jnp.zeros_like(db_ref)

        lane = lax.broadcasted_iota(jnp.int32, (tile, LANES), 1)
        dc_t = dc_ref[...]
        ri = lax.broadcasted_iota(jnp.int32, (tile, tile), 0)
        ci = lax.broadcasted_iota(jnp.int32, (tile, tile), 1)
        dlogf = carry[...] + _dot((ci >= ri).astype(F32), dc_t, HIGHEST)
        carry[...] += jnp.sum(dc_t, axis=0, keepdims=True)
        dfl = jnp.where(lane < fh, dlogf * jax.nn.sigmoid(-(fl_ref[...] + b_ref[...])), 0.0)
        dfl_ref[...] = dfl.astype(dfl_ref.dtype)
        db_ref[...] += jnp.sum(dfl, axis=0, keepdims=True)

    rev = lambda col: pl.BlockSpec((tile, LANES), lambda i: (n - 1 - i, col))
    return pl.pallas_call(
        body, name="fox_cumsum_bwd", grid=(n,),
        in_specs=[rev(4 * fw // LANES), _full_spec((1, LANES)), rev(0)],
        out_specs=[rev(0), _full_spec((1, LANES))],
        out_shape=[jax.ShapeDtypeStruct((t, LANES), BF16), jax.ShapeDtypeStruct((1, LANES), F32)],
        scratch_shapes=[pltpu.VMEM((1, LANES), F32)], compiler_params=_params("arbitrary"),
    )(zf, b_pad, dc)


def _fox_tile(t):
    return _tile(t, (512, 256, 128))


def _fox_fwd(zf, c, ct, *, fw):
    t = zf.shape[0]
    tq = _fox_tile(t)
    th = tq // 2
    n_pair, n_q = fw // LANES, t // tq
    scale = HEAD ** -0.5
    chains = [(h, qq) for h in (0, 1) for qq in (0, 1)]

    def body(q_ref, k_ref, v_ref, g_ref, c_ref, ct_ref, o_ref, lse_ref, y_ref):
        hp, i = pl.program_id(0), pl.program_id(1)
        lane = lax.broadcasted_iota(jnp.int32, (1, LANES), 1)
        in_head = (lane < HEAD, lane >= HEAD)
        rows = [pl.ds(qq * th, th) for qq in (0, 1)]
        qh = [jnp.where(in_head[h], q_ref[rows[qq], :] * scale, 0.0).astype(BF16) for h, qq in chains]
        cq = [_lane_pick(c_ref[rows[qq], :], 2 * hp + h) for h, qq in chains]
        qidx = lax.broadcasted_iota(jnp.int32, (th, tq), 0)
        kidx = lax.broadcasted_iota(jnp.int32, (th, tq), 1)

        def kv_step(j, carry, diagonal):
            m, l, acc = carry
            ks = pl.ds(pl.multiple_of(j * tq, tq), tq)
            kb = k_ref[ks, :].astype(BF16)
            vb = v_ref[ks, :]
            vh = [jnp.where(in_head[h], vb, 0.0).astype(BF16) for h in (0, 1)]
            ck = [ct_ref[0, j, pl.ds(h, 1), :] for h in (0, 1)]
            s = [_dot_nt(qh[n], kb) + cq[n] - ck[h] for n, (h, qq) in enumerate(chains)]
            if diagonal:
                s = [jnp.where(qq * th + qidx >= kidx, s[n], NEG) for n, (h, qq) in enumerate(chains)]
            m_new = [jnp.maximum(m[n], jnp.max(s[n], axis=1, keepdims=True)) for n in range(4)]
            p = [jnp.exp(s[n] - m_new[n]) for n in range(4)]
            alpha = [jnp.exp(m[n] - m_new[n]) for n in range(4)]
            l = [l[n] * alpha[n] + jnp.sum(p[n], axis=1, keepdims=True) for n in range(4)]
            pv = [_dot(p[n].astype(BF16), vh[h]) for n, (h, qq) in enumerate(chains)]
            acc = [acc[qq] * jnp.where(in_head[0], alpha[qq], alpha[2 + qq]) + pv[qq] + pv[2 + qq] for qq in (0, 1)]
            return tuple(m_new), tuple(l), tuple(acc)

        init = (tuple(jnp.full((th, 1), NEG, F32) for _ in chains), tuple(jnp.zeros((th, 1), F32) for _ in chains),
                tuple(jnp.zeros((th, LANES), F32) for _ in (0, 1)))
        carry = lax.fori_loop(0, i, functools.partial(kv_step, diagonal=False), init)
        m, l, acc = kv_step(i, carry, True)
        for qq in (0, 1):
            o = acc[qq] / jnp.where(in_head[0], l[qq], l[2 + qq])
            g = g_ref[rows[qq], :]
            o_ref[rows[qq], :] = o
            lse_ref[rows[qq], :] = jnp.where(in_head[0], m[qq] + jnp.log(l[qq]), m[2 + qq] + jnp.log(l[2 + qq]))
            y_ref[rows[qq], :] = (o * (g * jax.nn.sigmoid(g))).astype(y_ref.dtype)

    npw = fw // LANES
    blk = lambda col0: pl.BlockSpec((tq, LANES), lambda hp, i: (i, col0 + hp))
    res = lambda col0: pl.BlockSpec((t, LANES), lambda hp, i: (0, col0 + hp))
    out_blk = pl.BlockSpec((tq, LANES), lambda hp, i: (i, hp))
    return pl.pallas_call(
        body, name="fox_attn_fwd", grid=(n_pair, n_q),
        in_specs=[blk(0), res(npw), res(2 * npw), blk(3 * npw),
                  pl.BlockSpec((tq, LANES), lambda hp, i: (i, 0)),
                  pl.BlockSpec((1, n_q, 2, tq), lambda hp, i: (hp, 0, 0, 0))],
        out_specs=[out_blk, out_blk, out_blk],
        out_shape=[jax.ShapeDtypeStruct((t, fw), F32), jax.ShapeDtypeStruct((t, fw), F32),
                   jax.ShapeDtypeStruct((t, fw), BF16)],
        compiler_params=_params("arbitrary", "arbitrary"),
    )(zf, zf, zf, zf, c, ct)


def _fox_bwd(zf, do, c, ct, lse_r, dd_r, *, fw):
    t = zf.shape[0]
    tq = _fox_tile(t)
    n_pair, n_q = fw // LANES, t // tq
    scale = HEAD ** -0.5

    def body(q_ref, k_ref, v_ref, do_ref, c_ref, ct_ref, lse_ref, dd_ref,
             dq_ref, dk_ref, dv_ref, dcq_ref, dck_ref, dq_acc, dcq_acc):
        hp, j = pl.program_id(0), pl.program_id(1)

        @pl.when(j == 0)
        def _():
            dq_acc[...] = jnp.zeros_like(dq_acc)
            dcq_acc[...] = jnp.zeros_like(dcq_acc)

        lane = lax.broadcasted_iota(jnp.int32, (1, LANES), 1)
        in_head = (lane < HEAD, lane >= HEAD)
        kb = k_ref[...]
        kh = [jnp.where(m, kb, 0.0).astype(BF16) for m in in_head]
        vb = v_ref[...].astype(BF16)
        c_k = c_ref[...]
        ck = [_lane_pick(c_k, 2 * hp + h) for h in (0, 1)]
        kidx = lax.broadcasted_iota(jnp.int32, (tq, tq), 0)
        qidx = lax.broadcasted_iota(jnp.int32, (tq, tq), 1)

        def q_step(i, carry, diagonal):
            dk, dv, dck = carry
            qs = pl.ds(pl.multiple_of(i * tq, tq), tq)
            qf = q_ref[qs, :] * scale
            dof = do_ref[qs, :]
            qh = [jnp.where(m, qf, 0.0).astype(BF16) for m in in_head]
            doh = [jnp.where(m, dof, 0.0).astype(BF16) for m in in_head]
            row = lambda ref, h: ref[0, i, pl.ds(h, 1), :]
            st = [_dot_nt(kh[h], qh[h]) + row(ct_ref, h) - ck[h] for h in (0, 1)]
            p = [jnp.exp(st[h] - row(lse_ref, h)) for h in (0, 1)]
            if diagonal:
                p = [jnp.where(kidx <= qidx, p[h], 0.0) for h in (0, 1)]
            dst = [p[h] * (_dot_nt(vb, doh[h]) - row(dd_ref, h)) for h in (0, 1)]
            p16 = [x.astype(BF16) for x in p]
            ds16 = [x.astype(BF16) for x in dst]
            dv = dv + _dot(p16[0], doh[0]) + _dot(p16[1], doh[1])
            dk = dk + _dot(ds16[0], qh[0]) + _dot(ds16[1], qh[1])
            dq_acc[qs, :] += _dot_tn(ds16[0], kh[0]) + _dot_tn(ds16[1], kh[1])
            for h in (0, 1):
                dcq_acc[i, pl.ds(h, 1), :] += jnp.sum(dst[h], axis=0, keepdims=True)
            dck = tuple(dck[h] - jnp.sum(dst[h], axis=1, keepdims=True) for h in (0, 1))
            return dk, dv, dck

        zero = jnp.zeros((tq, LANES), F32)
        carry = q_step(j, (zero, zero, (jnp.zeros((tq, 1), F32),) * 2), True)
        dk, dv, dck = lax.fori_loop(j + 1, n_q, functools.partial(q_step, diagonal=False), carry)
        dk_ref[...] = dk.astype(dk_ref.dtype)
        dv_ref[...] = dv.astype(dv_ref.dtype)
        dck_ref[...] = jnp.where(lane == 0, dck[0], jnp.where(lane == 1, dck[1], 0.0))

        @pl.when(j == n_q - 1)
        def _():
            dq_ref[...] = (dq_acc[...] * scale).astype(dq_ref.dtype)
            dcq_ref[0] = dcq_acc[...]

    npw = fw // LANES
    res_z = lambda col0: pl.BlockSpec((t, LANES), lambda hp, j: (0, col0 + hp))
    blk_z = lambda col0: pl.BlockSpec((tq, LANES), lambda hp, j: (j, col0 + hp))
    res = pl.BlockSpec((t, LANES), lambda hp, j: (0, hp))
    blk = pl.BlockSpec((tq, LANES), lambda hp, j: (j, hp))
    rows = pl.BlockSpec((1, n_q, 2, tq), lambda hp, j: (hp, 0, 0, 0))
    return pl.pallas_call(
        body, name="fox_attn_bwd", grid=(n_pair, n_q),
        in_specs=[res_z(0), blk_z(npw), blk_z(2 * npw), res, pl.BlockSpec((tq, LANES), lambda hp, j: (j, 0)),
                  rows, rows, rows],
        out_specs=[res, blk, blk, rows, blk],
        out_shape=[jax.ShapeDtypeStruct((t, fw), BF16), jax.ShapeDtypeStruct((t, fw), BF16),
                   jax.ShapeDtypeStruct((t, fw), BF16), jax.ShapeDtypeStruct((n_pair, n_q, 2, tq), F32),
                   jax.ShapeDtypeStruct((t, fw), F32)],
        scratch_shapes=[pltpu.VMEM((t, LANES), F32), pltpu.VMEM((n_q, 2, tq), F32)],
        compiler_params=_params("arbitrary", "arbitrary"),
    )(zf, zf, zf, do, c, ct, lse_r, dd_r)


def _adamw_math(w, g, m, v):
    m = ADAM_B1 * m + (1.0 - ADAM_B1) * g
    v = ADAM_B2 * v + (1.0 - ADAM_B2) * jnp.square(g)
    m_hat = m / (1.0 - ADAM_B1 ** ADAM_STEP)
    v_hat = v / (1.0 - ADAM_B2 ** ADAM_STEP)
    delta = -ADAM_LR * (m_hat / (jnp.sqrt(v_hat) + ADAM_EPS) + ADAM_WD * w)
    return delta, m, v


def _adamw(name, w, g, m, v):
    lead, rows, cols = w.shape
    if lead == 1 and cols % LANES:
        outs = _adamw(name, *[jnp.transpose(a, (2, 0, 1)) for a in (w, g, m, v)])
        return [jnp.transpose(o, (1, 2, 0)) for o in outs]
    if lead == 1:
        tile = _tile(rows, (128, 64, 32, 16, 8))
        spec, steps = pl.BlockSpec((1, tile, cols), lambda i: (0, i, 0)), rows // tile
    else:
        tile = _div_tile(lead, 256, 1)
        spec, steps = pl.BlockSpec((tile, rows, cols), lambda i: (i, 0, 0)), lead // tile

    def body(w_ref, g_ref, m_ref, v_ref, d_ref, mo_ref, vo_ref):
        d_ref[...], mo_ref[...], vo_ref[...] = _adamw_math(w_ref[...], g_ref[...], m_ref[...], v_ref[...])

    return pl.pallas_call(
        body, name=name, grid=(steps,), in_specs=[spec] * 4, out_specs=[spec] * 3,
        out_shape=[jax.ShapeDtypeStruct(w.shape, F32)] * 3, compiler_params=_params("parallel"),
    )(w, g, m, v)


def _place():
    return lax.axis_index("x"), lax.axis_index("y"), lax.axis_index("c")


def _other_chips(x, y):
    return [(1 - x, y), (x, 1 - y), (1 - x, 1 - y)]


HBM_SPEC = pl.BlockSpec(memory_space=pltpu.HBM)


def _all_gather_shards(slabs):
    n = len(slabs)

    def body(*refs):
        src_refs, out_refs, send_sems, recv_sems = refs[:n], refs[n:2 * n], refs[2 * n], refs[2 * n + 1]
        x, y, c = _place()
        me = 2 * x + y
        sibling = (x, y, 1 - c)
        chips = _other_chips(x, y)
        first, passed, waits = [], [], []
        for g, (src_ref, out_ref) in enumerate(zip(src_refs, out_refs)):
            rh = src_ref.shape[0] // 2

            def part(chip, half, out_ref=out_ref, rh=rh):
                return out_ref.at[chip, pl.ds(half * rh, rh), :]

            def copy(k, src, dst, to, g=g):
                return pltpu.make_async_remote_copy(src_ref=src, dst_ref=dst, send_sem=send_sems.at[6 * g + k],
                                                    recv_sem=recv_sems.at[6 * g + k], device_id=to, device_id_type=MESH)

            first += [copy(j, src_ref.at[pl.ds(c * rh, rh), :], part(me, c), (px, py, c))
                      for j, (px, py) in enumerate(chips)]
            for j, (px, py) in enumerate(chips):
                theirs = part(2 * px + py, c)
                passed.append((copy(j, theirs, theirs, sibling), copy(3 + j, theirs, theirs, sibling)))
                other = part(2 * px + py, 1 - c)
                waits.append(copy(3 + j, other, other, sibling))
        for cp in first:
            cp.start()
        for landed, forward in passed:
            landed.wait_recv()
            forward.start()
        for cp in waits:
            cp.wait_recv()
        for cp in first + [fwd for _, fwd in passed]:
            cp.wait_send()

    return pl.pallas_call(
        body, name="weights_all_gather", in_specs=[HBM_SPEC] * n, out_specs=[HBM_SPEC] * n,
        out_shape=[jax.ShapeDtypeStruct((N_SHARD,) + a.shape, a.dtype) for a in slabs],
        scratch_shapes=[pltpu.SemaphoreType.DMA((6 * n,)), pltpu.SemaphoreType.DMA((6 * n,))],
    )(*slabs)


def _chip_index():
    return jnp.reshape(2 * lax.axis_index("x") + lax.axis_index("y"), (1,)).astype(jnp.int32)


def _place_own_shard(name, gathered, slab):
    rows, width = slab.shape
    tile = _div_tile(rows, 256, 16)

    def body(me_ref, s_ref, g_ref, o_ref):
        o_ref[0] = s_ref[...]

    return pl.pallas_call(
        body, name=name,
        grid_spec=pltpu.PrefetchScalarGridSpec(
            num_scalar_prefetch=1, grid=(rows // tile,),
            in_specs=[pl.BlockSpec((tile, width), lambda i, me: (i, 0)), pl.BlockSpec(memory_space=pl.ANY)],
            out_specs=pl.BlockSpec((1, tile, width), lambda i, me: (me[0], i, 0))),
        out_shape=jax.ShapeDtypeStruct(gathered.shape, gathered.dtype), input_output_aliases={2: 0},
        compiler_params=_params("parallel"),
    )(_chip_index(), slab, gathered)


def _sibling_exchange(gs):
    n = len(gs)

    def body(*refs):
        g_refs, out_refs, send_sems, recv_sems = refs[:n], refs[n:2 * n], refs[2 * n], refs[2 * n + 1]
        x, y, c = _place()
        copies = [pltpu.make_async_remote_copy(
            src_ref=g_ref.at[s, 1 - c], dst_ref=out_ref.at[s], send_sem=send_sems.at[N_SHARD * g + s],
            recv_sem=recv_sems.at[N_SHARD * g + s], device_id=(x, y, 1 - c), device_id_type=MESH)
            for g, (g_ref, out_ref) in enumerate(zip(g_refs, out_refs)) for s in range(N_SHARD)]
        for cp in copies:
            cp.start()
        for cp in copies:
            cp.wait()

    return pl.pallas_call(
        body, name="grad_sibling_exchange", in_specs=[HBM_SPEC] * n, out_specs=[HBM_SPEC] * n,
        out_shape=[jax.ShapeDtypeStruct((N_SHARD,) + g.shape[2:], g.dtype) for g in gs],
        scratch_shapes=[pltpu.SemaphoreType.DMA((N_SHARD * n,)), pltpu.SemaphoreType.DMA((N_SHARD * n,))],
    )(*gs)


def _add_sibling(name, g, got):
    _, _, rh, width = g.shape
    tile = _div_tile(rh, 256, 16)
    c_arr = jnp.reshape(lax.axis_index("c"), (1,)).astype(jnp.int32)

    def body(c_ref, a_ref, b_ref, o_ref):
        o_ref[...] = (a_ref[0] + b_ref[...]).astype(o_ref.dtype)

    return pl.pallas_call(
        body, name=name,
        grid_spec=pltpu.PrefetchScalarGridSpec(
            num_scalar_prefetch=1, grid=(N_SHARD, rh // tile),
            in_specs=[pl.BlockSpec((1, 1, tile, width), lambda s, i, c: (s, c[0], i, 0)),
                      pl.BlockSpec((1, tile, width), lambda s, i, c: (s, i, 0))],
            out_specs=pl.BlockSpec((1, tile, width), lambda s, i, c: (s, i, 0))),
        out_shape=jax.ShapeDtypeStruct((N_SHARD, rh, width), BF16),
        compiler_params=_params("parallel", "parallel"),
    )(c_arr, g, got)


def _chip_exchange(ps):
    n = len(ps)

    def body(*refs):
        p_refs, out_refs, send_sems, recv_sems = refs[:n], refs[n:2 * n], refs[2 * n], refs[2 * n + 1]
        x, y, c = _place()
        me = 2 * x + y
        copies = [pltpu.make_async_remote_copy(
            src_ref=p_ref.at[2 * px + py], dst_ref=out_ref.at[me], send_sem=send_sems.at[3 * g + j],
            recv_sem=recv_sems.at[3 * g + j], device_id=(px, py, c), device_id_type=MESH)
            for g, (p_ref, out_ref) in enumerate(zip(p_refs, out_refs)) for j, (px, py) in enumerate(_other_chips(x, y))]
        for cp in copies:
            cp.start()
        for cp in copies:
            cp.wait()

    return pl.pallas_call(
        body, name="grad_chip_exchange", in_specs=[HBM_SPEC] * n, out_specs=[HBM_SPEC] * n,
        out_shape=[jax.ShapeDtypeStruct(p.shape, p.dtype) for p in ps],
        scratch_shapes=[pltpu.SemaphoreType.DMA((3 * n,)), pltpu.SemaphoreType.DMA((3 * n,))],
    )(*ps)


def _sum_chips(name, p, got):
    _, rh, width = p.shape
    tile = _div_tile(rh, 256, 16)
    n_t = rh // tile
    place = jnp.stack([2 * lax.axis_index("x") + lax.axis_index("y"), lax.axis_index("c")]).astype(jnp.int32)

    def body(pl_ref, own_ref, r0, r1, r2, r3, o_ref):
        me = pl_ref[0]
        own = own_ref[0].astype(F32)
        t = [jnp.where(me == s, own, r[0].astype(F32)) for s, r in enumerate((r0, r1, r2, r3))]
        o_ref[...] = ((t[0] + t[1]) + t[2]) + t[3]

    def slot(s):
        return pl.BlockSpec((1, tile, width), lambda i, pc: (jnp.where(pc[0] == s, (s + 1) % N_SHARD, s), i, 0))

    return pl.pallas_call(
        body, name=name,
        grid_spec=pltpu.PrefetchScalarGridSpec(
            num_scalar_prefetch=1, grid=(n_t,),
            in_specs=[pl.BlockSpec((1, tile, width), lambda i, pc: (pc[0], i, 0))] + [slot(s) for s in range(N_SHARD)],
            out_specs=pl.BlockSpec((tile, width), lambda i, pc: (pc[1] * n_t + i, 0))),
        out_shape=jax.ShapeDtypeStruct((2 * rh, width), F32), compiler_params=_params("parallel"),
    )(place, p, got, got, got, got)


def _join_halves(fulls):
    n = len(fulls)

    def body(*refs):
        f_refs, out_refs, send_sems, recv_sems = refs[:n], refs[n:2 * n], refs[2 * n], refs[2 * n + 1]
        x, y, c = _place()

        def copy(g, half):
            rh = f_refs[g].shape[0] // 2
            return pltpu.make_async_remote_copy(
                src_ref=f_refs[g].at[pl.ds(half * rh, rh), :], dst_ref=out_refs[g].at[pl.ds(half * rh, rh), :],
                send_sem=send_sems.at[g], recv_sem=recv_sems.at[g], device_id=(x, y, 1 - c), device_id_type=MESH)

        for g in range(n):
            copy(g, c).start()
        for g in range(n):
            copy(g, c).wait_send()
            copy(g, 1 - c).wait_recv()

    return pl.pallas_call(
        body, name="grad_join_halves", in_specs=[HBM_SPEC] * n, out_specs=[HBM_SPEC] * n,
        out_shape=[jax.ShapeDtypeStruct(f.shape, f.dtype) for f in fulls],
        input_output_aliases={g: g for g in range(n)},
        scratch_shapes=[pltpu.SemaphoreType.DMA((n,)), pltpu.SemaphoreType.DMA((n,))],
    )(*fulls)


def _reduce_scatter(gs):
    gs = [g.reshape(N_SHARD, 2, g.shape[1] // 2, g.shape[2]) for g in gs]
    got = _sibling_exchange(gs)
    chip_sums = [_add_sibling(f"grad_add_sibling_{i}", g, r) for i, (g, r) in enumerate(zip(gs, got))]
    landed = _chip_exchange(chip_sums)
    return _join_halves([_sum_chips(f"grad_sum_chips_{i}", p, r) for i, (p, r) in enumerate(zip(chip_sums, landed))])


def _all_reduce_small(v):
    rows = v.shape[0]

    def body(v_ref, out_ref, gather, send_sems, recv_sems):
        x, y, c = _place()
        gather[4 * x + 2 * y + c] = v_ref[...]
        flips = [(dx, dy, dc) for dx in (0, 1) for dy in (0, 1) for dc in (0, 1)][1:]
        peers = [((x + dx) % 2, (y + dy) % 2, (c + dc) % 2) for dx, dy, dc in flips]
        copies = [pltpu.make_async_remote_copy(
            src_ref=v_ref, dst_ref=gather.at[4 * x + 2 * y + c], send_sem=send_sems.at[j], recv_sem=recv_sems.at[j],
            device_id=peer, device_id_type=MESH) for j, peer in enumerate(peers)]
        for cp in copies:
            cp.start()
        for j, (px, py, pc) in enumerate(peers):
            pltpu.make_async_remote_copy(
                src_ref=v_ref, dst_ref=gather.at[4 * px + 2 * py + pc], send_sem=send_sems.at[j],
                recv_sem=recv_sems.at[j], device_id=(px, py, pc), device_id_type=MESH).wait_recv()
        for cp in copies:
            cp.wait_send()
        acc = gather[0]
        for d in range(1, 8):
            acc = acc + gather[d]
        out_ref[...] = acc

    vm = pl.BlockSpec(memory_space=pltpu.VMEM)
    return pl.pallas_call(
        body, name="small_grads_all_reduce", in_specs=[vm], out_specs=vm,
        out_shape=jax.ShapeDtypeStruct(v.shape, F32),
        scratch_shapes=[pltpu.VMEM((8, rows, LANES), F32), pltpu.SemaphoreType.DMA((7,)), pltpu.SemaphoreType.DMA((7,))],
    )(v)


def _pad_lanes(v):
    v = v.reshape(1, -1)
    return jnp.pad(v, ((0, 0), (0, -v.shape[1] % LANES)))


def _pack_small(vs, rows):
    flat = jnp.concatenate([_pad_lanes(v) for v in vs], axis=1)
    return jnp.pad(flat, ((0, 0), (0, rows * LANES - flat.shape[1]))).reshape(rows, LANES)


def _unpack_small(packed, shapes):
    flat = packed.reshape(-1)
    out, off = [], 0
    for s in shapes:
        n = 1
        for d in s:
            n *= d
        out.append(flat[off:off + n].reshape(s))
        off += n + (-n % LANES)
    return out


BIG = ("w_in", "rw_w_lora_up", "rw_a_lora_up", "w_up_rwkv", "w_up_fox", "w_out", "ple_proj", "ple_gate_w")
ROW_SHARDED = ("w_out", "ple_gate_w")
SMALL = ("norm_g", "rw_shift_mu", "rw_w0", "rw_a0", "rw_k_k", "rw_k_a", "rw_r_k", "rw_ln_g", "rw_ln_b", "fox_b_f",
         "ple_norm_g", "final_norm_g")
WEIGHTS = ("norm_g", "w_in", "rw_shift_mu", "rw_w0", "rw_w_lora_up", "rw_a0", "rw_a_lora_up", "rw_k_k", "rw_k_a",
           "rw_r_k", "rw_ln_g", "rw_ln_b", "fox_b_f", "w_up_rwkv", "w_up_fox", "w_out", "ple_proj", "ple_gate_w",
           "ple_norm_g", "final_norm_g")


def kernel(x, p, norm_g, w_in, rw_shift_mu, rw_w0, rw_w_lora_up, rw_a0, rw_a_lora_up, rw_k_k, rw_k_a, rw_r_k, rw_ln_g, rw_ln_b, fox_b_f, w_up_rwkv, w_up_fox, w_out, ple_proj, ple_gate_w, ple_norm_g, final_norm_g, loss_target, m_norm_g, m_w_in, m_rw_shift_mu, m_rw_w0, m_rw_w_lora_up, m_rw_a0, m_rw_a_lora_up, m_rw_k_k, m_rw_k_a, m_rw_r_k, m_rw_ln_g, m_rw_ln_b, m_fox_b_f, m_w_up_rwkv, m_w_up_fox, m_w_out, m_ple_proj, m_ple_gate_w, m_ple_norm_g, m_final_norm_g, v_norm_g, v_w_in, v_rw_shift_mu, v_rw_w0, v_rw_w_lora_up, v_rw_a0, v_rw_a_lora_up, v_rw_k_k, v_rw_k_a, v_rw_r_k, v_rw_ln_g, v_rw_ln_b, v_fox_b_f, v_w_up_rwkv, v_w_up_fox, v_w_out, v_ple_proj, v_ple_gate_w, v_ple_norm_g, v_final_norm_g):
    wts = dict(norm_g=norm_g, w_in=w_in, rw_shift_mu=rw_shift_mu, rw_w0=rw_w0, rw_w_lora_up=rw_w_lora_up, rw_a0=rw_a0,
               rw_a_lora_up=rw_a_lora_up, rw_k_k=rw_k_k, rw_k_a=rw_k_a, rw_r_k=rw_r_k, rw_ln_g=rw_ln_g, rw_ln_b=rw_ln_b,
               fox_b_f=fox_b_f, w_up_rwkv=w_up_rwkv, w_up_fox=w_up_fox, w_out=w_out, ple_proj=ple_proj,
               ple_gate_w=ple_gate_w, ple_norm_g=ple_norm_g, final_norm_g=final_norm_g)
    mom = dict(norm_g=m_norm_g, w_in=m_w_in, rw_shift_mu=m_rw_shift_mu, rw_w0=m_rw_w0, rw_w_lora_up=m_rw_w_lora_up,
               rw_a0=m_rw_a0, rw_a_lora_up=m_rw_a_lora_up, rw_k_k=m_rw_k_k, rw_k_a=m_rw_k_a, rw_r_k=m_rw_r_k,
               rw_ln_g=m_rw_ln_g, rw_ln_b=m_rw_ln_b, fox_b_f=m_fox_b_f, w_up_rwkv=m_w_up_rwkv, w_up_fox=m_w_up_fox,
               w_out=m_w_out, ple_proj=m_ple_proj, ple_gate_w=m_ple_gate_w, ple_norm_g=m_ple_norm_g,
               final_norm_g=m_final_norm_g)
    vel = dict(norm_g=v_norm_g, w_in=v_w_in, rw_shift_mu=v_rw_shift_mu, rw_w0=v_rw_w0, rw_w_lora_up=v_rw_w_lora_up,
               rw_a0=v_rw_a0, rw_a_lora_up=v_rw_a_lora_up, rw_k_k=v_rw_k_k, rw_k_a=v_rw_k_a, rw_r_k=v_rw_r_k,
               rw_ln_g=v_rw_ln_g, rw_ln_b=v_rw_ln_b, fox_b_f=v_fox_b_f, w_up_rwkv=v_w_up_rwkv, w_up_fox=v_w_up_fox,
               w_out=v_w_out, ple_proj=v_ple_proj, ple_gate_w=v_ple_gate_w, ple_norm_g=v_ple_norm_g,
               final_norm_g=v_final_norm_g)

    t, d = x.shape[1], x.shape[2]
    cw = rw_w0.shape[1]
    lr = rw_w_lora_up.shape[1]
    fh = fox_b_f.shape[1]
    fw = fh * HEAD
    rw_cols = 4 * cw + 2 * lr
    fox_cols = 4 * fw + fh
    assert 2 * lr == LANES and cw % LANES == 0 and fw % LANES == 0 and fh <= LANES
    xs = x[0]
    ps = p[0, 0]
    tgt = loss_target[0]

    groups = {}
    for n in BIG:
        groups.setdefault(wts[n].shape[2], []).append(n)
    groups = list(groups.values())
    slabs16 = []
    for gi, names in enumerate(groups):
        slab = jnp.concatenate([wts[n][0] for n in names], axis=0)
        rows, width = slab.shape
        tile_c = _div_tile(rows, 256, 32)
        slabs16 += _rows(f"weights_to_bf16_{gi}", lambda i, a: (a,), rows, tile_c, [(slab, _row_spec(tile_c, width))],
                         [(width, BF16)])
    gathered = [_place_own_shard(f"weights_place_own_{gi}", g, own)
                for gi, (g, own) in enumerate(zip(_all_gather_shards(slabs16), slabs16))]
    full = {}
    for names, g in zip(groups, gathered):
        off = 0
        for n in names:
            r = wts[n].shape[1]
            part = g[:, off:off + r, :]
            full[n] = (part.reshape(N_SHARD * r, -1) if n in ROW_SHARDED
                       else jnp.concatenate([part[s] for s in range(N_SHARD)], axis=1))
            off += r
    w_rw = full["w_in"][:, :rw_cols]
    w_fox = jnp.pad(full["w_in"][:, rw_cols:rw_cols + fox_cols], ((0, 0), (0, LANES - fh)))
    w_gate = full["w_in"][:, rw_cols + fox_cols:]
    wup_pad = jnp.pad(full["rw_w_lora_up"], ((0, lr), (0, 0)))
    aup_pad = jnp.pad(full["rw_a_lora_up"], ((lr, 0), (0, 0)))
    b_pad = _pad_lanes(fox_b_f)
    r_k_row = rw_r_k.reshape(1, cw)
    gf_row = final_norm_g.reshape(1, d)

    tile = _tile(t, (256, 128, 64, 32, 16, 8))
    tile_s = _tile(t, (128, 64, 32, 16, 8))
    n_s = t // tile_s
    full2 = lambda a: (a, _full_spec(a.shape))

    (h,) = _rows("norm1", lambda i, a, g: (_rms(a, g),), t, tile, [(xs, _row_spec(tile, d)), full2(norm_g)], [(d, BF16)])
    z_rw = _matmul("proj_rw", h, w_rw)
    z_fox = _matmul("proj_fox", h, w_fox)
    z_gate = _matmul("proj_gate", h, w_gate)

    pre_consts = [full2(rw_shift_mu), full2(rw_w0), full2(rw_a0), full2(wup_pad), full2(aup_pad), full2(rw_k_k),
                  full2(rw_k_a)]

    def pre_fwd(i, z, prev8, *consts):
        return _rw_pre(z, _shifted(i, z, prev8), *consts, cw=cw)

    r_, w_, k_, v_, kk_, a_, g_ = _rows(
        "rwkv_pre", pre_fwd, t, tile_s,
        [(z_rw, _row_spec(tile_s, rw_cols)), (z_rw, _prev_rows_spec(tile_s, rw_cols))] + pre_consts, [(cw, F32)] * 7)
    y_scan, states, tinvs = _scan_fwd(r_, w_, k_, v_, kk_, a_)
    post_consts = [full2(rw_ln_g), full2(rw_ln_b), full2(r_k_row)]
    post_rows = lambda *arrs: [(a, _row_spec(tile_s, cw)) for a in arrs]
    (y_rw,) = _rows("rwkv_post", lambda i, *a: (_rw_post(*a, cw=cw),), t, tile_s,
                    post_rows(y_scan, r_, k_, v_, g_) + post_consts, [(cw, BF16)])

    c_fox = _fox_cumsum(z_fox, b_pad, fw=fw, fh=fh)
    tq = _fox_tile(t)
    n_pair_f = fw // LANES
    head_rows = lambda a: a.T.reshape(n_pair_f, 2, t // tq, tq).transpose(0, 2, 1, 3)
    head_cols = lambda a: a.transpose(0, 2, 1, 3).reshape(fh, t).T
    ct_fox = head_rows(c_fox[:, :fh])
    o_fox, lse_fox, y_fox = _fox_fwd(z_fox, c_fox, ct_fox, fw=fw)

    u_rw = _matmul("up_rwkv", y_rw, full["w_up_rwkv"])
    u_fox = _matmul("up_fox", y_fox, full["w_up_fox"])
    (merged,) = _rows("merge", lambda i, zg, a, b: (_merge(zg, a, b, d=d),), t, tile,
                      [(z_gate, _row_spec(tile, 2 * d)), (u_rw, _row_spec(tile, d)), (u_fox, _row_spec(tile, d))],
                      [(d, BF16)])
    x1 = _matmul("out_proj", merged, full["w_out"], add=xs)
    (n2,) = _rows("norm2", lambda i, a, g: (_rms(a, g),), t, tile, [(x1, _row_spec(tile, d)), full2(ple_norm_g)],
                  [(d, BF16)])
    gl = _matmul("ple_gate", n2, full["ple_gate_w"])
    ple = _matmul("ple_proj", ps, full["ple_proj"])

    def head_bwd(i, x1_t, ple_t, gl_t, gf, tg):
        loss, vjp = jax.vjp(lambda a, b, cc, g: _head_loss(a, b, cc, g, tg), x1_t, ple_t, gl_t, gf)
        dx1, dple, dgl, dgf = vjp(jnp.ones((1, 1), F32))
        return dx1, dple, dgl, jnp.broadcast_to(loss, (1, LANES)), dgf

    dx2, dple, dgl, loss_row, d_gf = _rows(
        "loss_head", head_bwd, t, tile_s,
        [(x1, _row_spec(tile_s, d)), (ple, _row_spec(tile_s, d)), (gl, _row_spec(tile_s, d)), full2(gf_row),
         (tgt, _row_spec(tile_s, d))],
        [(d, F32), (d, BF16), (d, BF16)], [(1, LANES), (1, d)])

    g_ple_proj = _matmul("d_ple_proj", ps, dple, ta=True)
    g_ple_gate = _matmul("d_ple_gate_w", n2, dgl, ta=True)
    dn2 = _matmul("d_n2", dgl, full["ple_gate_w"], tb=True)

    def norm_bwd(i, a, g, dh, res):
        _, vjp = jax.vjp(_rms, a, g)
        da, dg = vjp(dh)
        return res + da, dg

    dx1, d_g2 = _rows("norm2_bwd", norm_bwd, t, tile_s,
                      [(x1, _row_spec(tile_s, d)), full2(ple_norm_g), (dn2, _row_spec(tile_s, d)),
                       (dx2, _row_spec(tile_s, d))], [(d, F32)], [(1, d)])
    g_w_out = _matmul("d_w_out", merged, dx1, ta=True)
    dmerged = _matmul("d_merged", dx1, full["w_out"], tb=True)

    def merge_bwd(i, zg, a, b, dm):
        _, vjp = jax.vjp(functools.partial(_merge, d=d), zg, a, b)
        return vjp(dm)

    dz_gate, du_rw, du_fox = _rows(
        "merge_bwd", merge_bwd, t, tile_s,
        [(z_gate, _row_spec(tile_s, 2 * d)), (u_rw, _row_spec(tile_s, d)), (u_fox, _row_spec(tile_s, d)),
         (dmerged, _row_spec(tile_s, d))], [(2 * d, BF16), (d, BF16), (d, BF16)])
    g_up_rw = _matmul("d_w_up_rwkv", y_rw, du_rw, ta=True)
    g_up_fox = _matmul("d_w_up_fox", y_fox, du_fox, ta=True)
    dy_rw = _matmul("d_y_rwkv", du_rw, full["w_up_rwkv"], tb=True)
    dy_fox = _matmul("d_y_fox", du_fox, full["w_up_fox"], tb=True)

    def post_bwd(i, y, r, k, v, g, ln_g, ln_b, r_k, dy):
        _, vjp = jax.vjp(functools.partial(_rw_post, cw=cw), y, r, k, v, g, ln_g, ln_b, r_k)
        return vjp(dy)

    dys, dr1, dk1, dv1, dg1, d_ln_g, d_ln_b, d_r_k = _rows(
        "rwkv_post_bwd", post_bwd, t, tile_s,
        post_rows(y_scan, r_, k_, v_, g_) + post_consts + post_rows(dy_rw), [(cw, F32)] * 5, [(1, cw)] * 3)
    dr2, dw2, dk2, dv2, dkk2, da2 = _scan_bwd(r_, w_, k_, v_, kk_, a_, states, tinvs, dys)

    def pre_bwd(i, z, prev8, mu, w0, a0, wup, aup, k_k, k_a, dr_a, dr_b, dk_a, dk_b, dv_a, dv_b, dw, dkk, da, dg):
        zp = _shifted(i, z, prev8)
        _, vjp = jax.vjp(functools.partial(_rw_pre, cw=cw), z, zp, mu, w0, a0, wup, aup, k_k, k_a)
        dz, dzp, dmu, dw0, da0, dwup, daup, dk_k, dk_a = vjp((dr_a + dr_b, dw, dk_a + dk_b, dv_a + dv_b, dkk, da, dg))
        row = lax.broadcasted_iota(jnp.int32, dz.shape, 0)
        dz = dz + jnp.where(row < tile_s - 1, pltpu.roll(dzp, tile_s - 1, 0), 0.0)
        first = jnp.where(lax.broadcasted_iota(jnp.int32, (8, dz.shape[1]), 0) == 0, _row_of(dzp, 0), 0.0)
        return dz, first, dmu, dw0, da0, dwup, daup, dk_k, dk_a

    def pre_bwd_call():
        n_in = 2 + len(pre_consts) + 10
        ins = ([(z_rw, _row_spec(tile_s, rw_cols)), (z_rw, _prev_rows_spec(tile_s, rw_cols))] + pre_consts
               + post_rows(dr1, dr2, dk1, dk2, dv1, dv2, dw2, dkk2, da2, dg1))

        def body(*refs):
            i = pl.program_id(0)
            vals = pre_bwd(i, *[r[...] for r in refs[:n_in]])
            refs[n_in][...] = vals[0]
            refs[n_in + 1][...] = vals[1]
            for r, v in zip(refs[n_in + 2:], vals[2:]):
                @pl.when(i == 0)
                def _(r=r, v=v):
                    r[...] = v

                @pl.when(i > 0)
                def _(r=r, v=v):
                    r[...] += v

        acc_shapes = [(1, rw_cols), (1, cw), (1, cw), (LANES, cw), (LANES, cw), (1, cw), (1, cw)]
        return pl.pallas_call(
            body, name="rwkv_pre_bwd", grid=(n_s,), in_specs=[s for _, s in ins],
            out_specs=[_row_spec(tile_s, rw_cols), pl.BlockSpec((8, rw_cols), lambda i: (i, 0))]
            + [_full_spec(s) for s in acc_shapes],
            out_shape=[jax.ShapeDtypeStruct((t, rw_cols), F32), jax.ShapeDtypeStruct((8 * n_s, rw_cols), F32)]
            + [jax.ShapeDtypeStruct(s, F32) for s in acc_shapes],
            compiler_params=_params("arbitrary"),
        )(*[a for a, _ in ins])

    dz_main, dz_first, d_mu, d_w0, d_a0, d_wup, d_aup, d_k_k, d_k_a = pre_bwd_call()

    def add_next_row(i, dz, nxt8):
        row = lax.broadcasted_iota(jnp.int32, dz.shape, 0)
        carry = jnp.where(i < n_s - 1, _row_of(nxt8, 0), 0.0)
        return (dz + jnp.where(row == tile_s - 1, carry, 0.0),)

    (dz_rw,) = _rows("rwkv_shift_bwd", add_next_row, t, tile_s,
                     [(dz_main, _row_spec(tile_s, rw_cols)), (dz_first, _next_rows_spec(tile_s, rw_cols, n_s))],
                     [(rw_cols, BF16)])

    def fox_post_bwd(i, o, g, dy):
        _, vjp = jax.vjp(lambda oo, gg: oo * (gg * jax.nn.sigmoid(gg)), o, g)
        do, dg = vjp(dy)
        return do, _head_sum(do * o, _head_matrix(fw)), dg

    do_fox, dd_fox, dg_fox = _rows(
        "fox_post_bwd", fox_post_bwd, t, tile_s,
        [(o_fox, _row_spec(tile_s, fw)), (z_fox, _row_spec(tile_s, fw, 3)), (dy_fox, _row_spec(tile_s, fw))],
        [(fw, F32), (fw, F32), (fw, BF16)])
    dq_f, dk_f, dv_f, dcq, dck = _fox_bwd(z_fox, do_fox, c_fox, ct_fox, head_rows(lse_fox[:, ::HEAD]),
                                          head_rows(dd_fox[:, ::HEAD]), fw=fw)
    dc = head_cols(dcq) + dck.reshape(t, n_pair_f, LANES)[:, :, :2].reshape(t, fh)
    dfl, d_bf = _fox_cumsum_bwd(z_fox, b_pad, jnp.pad(dc, ((0, 0), (0, LANES - fh))), fw=fw, fh=fh)
    dz_fox = jnp.concatenate([dq_f, dk_f, dv_f, dg_fox, dfl], axis=1)

    g_w_rw = _matmul("d_w_in_rw", h, dz_rw, ta=True)
    g_w_fox = _matmul("d_w_in_fox", h, dz_fox, ta=True)
    g_w_gate = _matmul("d_w_in_gate", h, dz_gate, ta=True)
    dh = _matmul("d_h_rw", dz_rw, w_rw, tb=True)
    dh = _matmul("d_h_fox", dz_fox, w_fox, tb=True, add=dh)
    dh = _matmul("d_h_gate", dz_gate, w_gate, tb=True, add=dh)
    grad_x, d_g1 = _rows("norm1_bwd", norm_bwd, t, tile_s,
                         [(xs, _row_spec(tile_s, d)), full2(norm_g), (dh, _row_spec(tile_s, d)),
                          (dx1, _row_spec(tile_s, d))], [(d, F32)], [(1, d)])

    g_full = {
        "w_in": jnp.concatenate([g_w_rw, g_w_fox[:, :fox_cols], g_w_gate], axis=1),
        "rw_w_lora_up": d_wup[:lr], "rw_a_lora_up": d_aup[lr:], "w_up_rwkv": g_up_rw, "w_up_fox": g_up_fox,
        "w_out": g_w_out, "ple_proj": g_ple_proj, "ple_gate_w": g_ple_gate,
    }
    def by_shard(n):
        g = g_full[n]
        if n in ROW_SHARDED:
            return g.reshape(N_SHARD, g.shape[0] // N_SHARD, g.shape[1])
        return jnp.stack(jnp.split(g, N_SHARD, axis=1))

    reduced = _reduce_scatter([jnp.concatenate([by_shard(n) for n in names], axis=1) for names in groups])
    grads = {}
    for names, g in zip(groups, reduced):
        off = 0
        for n in names:
            r = wts[n].shape[1]
            grads[n] = g[off:off + r][None]
            off += r

    small_parts = dict(norm_g=d_g1, rw_shift_mu=d_mu, rw_w0=d_w0, rw_a0=d_a0, rw_k_k=d_k_k, rw_k_a=d_k_a, rw_r_k=d_r_k,
                       rw_ln_g=d_ln_g, rw_ln_b=d_ln_b, fox_b_f=d_bf[:, :fh], ple_norm_g=d_g2, final_norm_g=d_gf)
    n_small = sum(-(-wts[n].size // LANES) for n in SMALL)
    small_rows = -(-n_small // 8) * 8
    small_shapes = [wts[n].shape for n in SMALL]
    g_small = _all_reduce_small(_pack_small([small_parts[n] for n in SMALL], small_rows))
    for n, g in zip(SMALL, _unpack_small(g_small, small_shapes)):
        grads[n] = g

    delta, new_m, new_v = {}, {}, {}
    for n in BIG:
        delta[n], new_m[n], new_v[n] = _adamw("adamw_" + n, wts[n], grads[n], mom[n], vel[n])
    packed = lambda src: _pack_small([src[n] for n in SMALL], small_rows)[None]
    for store, out in zip((delta, new_m, new_v), _adamw("adamw_small", packed(wts), g_small[None], packed(mom), packed(vel))):
        for n, a in zip(SMALL, _unpack_small(out[0], small_shapes)):
            store[n] = a

    loss = lax.psum(loss_row[0, 0], ("x", "y", "c"))
    return (loss, grad_x[None], *[grads[n] for n in WEIGHTS], *[delta[n] for n in WEIGHTS],
            *[new_m[n] for n in WEIGHTS], *[new_v[n] for n in WEIGHTS])
```

```python
import functools

import jax
import jax.numpy as jnp
from jax import lax
from jax.experimental import pallas as pl
from jax.experimental.pallas import tpu as pltpu

F32 = jnp.float32
BF16 = jnp.bfloat16
HIGHEST = lax.Precision.HIGHEST
SCAN_PREC = lax.Precision.HIGH
MESH = pl.DeviceIdType.MESH

LANES = 128
HEAD = 64
NORM_EPS = 1e-6
GN_EPS = 64e-5
ADAM_LR = 0.001
ADAM_B1 = 0.9
ADAM_B2 = 0.999
ADAM_EPS = 1e-08
ADAM_WD = 0.01
ADAM_STEP = 10
N_SHARD = 4
VMEM_LIMIT = 56 * 1024 * 1024
PAIRS_PER_STEP = 4


def _params(*sem):
    return pltpu.CompilerParams(dimension_semantics=sem, vmem_limit_bytes=VMEM_LIMIT)


def _tile(n, cands):
    for c in cands:
        if c <= n and n % c == 0:
            return c
    return n


def _div_tile(n, cap, mult):
    return max(c for c in range(mult, min(n, cap) + 1, mult) if n % c == 0)


_ROW_TILES = (512, 256, 128, 64, 32, 16, 8)


def _dot(a, b, prec=None):
    return lax.dot_general(a, b, (((1,), (0,)), ((), ())), precision=prec, preferred_element_type=F32)


def _dot_nt(a, b, prec=None):
    return lax.dot_general(a, b, (((1,), (1,)), ((), ())), precision=prec, preferred_element_type=F32)


def _dot_tn(a, b, prec=None):
    return lax.dot_general(a, b, (((0,), (0,)), ((), ())), precision=prec, preferred_element_type=F32)


@jax.custom_vjp
def _bdot(x, w):
    return _dot(x.astype(BF16), w.astype(BF16))


def _bdot_fwd(x, w):
    return _bdot(x, w), (x, w)


def _bdot_bwd(res, ct):
    x, w = res
    return _dot_nt(ct.astype(BF16), w.astype(BF16)), _dot_tn(x.astype(BF16), ct.astype(BF16))


_bdot.defvjp(_bdot_fwd, _bdot_bwd)


def _head_matrix(width):
    c = lax.broadcasted_iota(jnp.int32, (width, LANES), 0)
    h = lax.broadcasted_iota(jnp.int32, (width, LANES), 1)
    return (c // HEAD == h).astype(F32)


def _head_sum(x, e):
    return _dot_nt(_dot(x, e, SCAN_PREC), e, SCAN_PREC)


def _softplus(x):
    return jnp.maximum(x, 0.0) + jnp.log1p(jnp.exp(-jnp.abs(x)))


def _lane_pick(x, idx):
    lane = lax.broadcasted_iota(jnp.int32, x.shape, 1)
    return jnp.sum(jnp.where(lane == idx, x, 0.0), axis=1, keepdims=True)


def _matmul(name, a, b, *, ta=False, tb=False, add=None, out_dtype=F32):
    m, k = (a.shape[1], a.shape[0]) if ta else a.shape
    n = b.shape[0] if tb else b.shape[1]
    tm = _tile(m, (1024, 512, 256, 128))
    tn = _tile(n, (1408, 1024, 768, 640, 512, 384, 256, 128))
    tk = _tile(k, (1408, 1024, 768, 640, 512, 384, 256, 128, 64, 32, 16))
    nk = k // tk
    dims = (((0 if ta else 1,), (1 if tb else 0,)), ((), ()))

    def body(*refs):
        a_ref, b_ref = refs[0], refs[1]
        o_ref, acc_ref = refs[-2], refs[-1]
        kk = pl.program_id(2)

        @pl.when(kk == 0)
        def _():
            acc_ref[...] = jnp.zeros_like(acc_ref)

        acc_ref[...] += lax.dot_general(a_ref[...].astype(BF16), b_ref[...].astype(BF16), dims,
                                        preferred_element_type=F32)

        @pl.when(kk == nk - 1)
        def _():
            r = acc_ref[...]
            if add is not None:
                r = r + refs[2][...].astype(F32)
            o_ref[...] = r.astype(o_ref.dtype)

    a_spec = pl.BlockSpec((tk, tm), lambda i, j, kk: (kk, i)) if ta else pl.BlockSpec((tm, tk), lambda i, j, kk: (i, kk))
    b_spec = pl.BlockSpec((tn, tk), lambda i, j, kk: (j, kk)) if tb else pl.BlockSpec((tk, tn), lambda i, j, kk: (kk, j))
    o_spec = pl.BlockSpec((tm, tn), lambda i, j, kk: (i, j))
    ins, specs = [a, b], [a_spec, b_spec]
    if add is not None:
        ins.append(add)
        specs.append(o_spec)
    return pl.pallas_call(
        body, name=name, grid=(m // tm, n // tn, nk), in_specs=specs, out_specs=o_spec,
        out_shape=jax.ShapeDtypeStruct((m, n), out_dtype),
        scratch_shapes=[pltpu.VMEM((tm, tn), F32)],
        compiler_params=_params("parallel", "parallel", "arbitrary"),
    )(*ins)


def _rows(name, fn, n_rows, tile, ins, outs, accs=()):
    n_in, n_out = len(ins), len(outs)

    def body(*refs):
        i = pl.program_id(0)
        vals = fn(i, *[r[...] for r in refs[:n_in]])
        for r, v in zip(refs[n_in:n_in + n_out], vals[:n_out]):
            r[...] = v.astype(r.dtype)
        for r, v in zip(refs[n_in + n_out:], vals[n_out:]):
            @pl.when(i == 0)
            def _(r=r, v=v):
                r[...] = v

            @pl.when(i > 0)
            def _(r=r, v=v):
                r[...] += v

    out_specs = [pl.BlockSpec((tile, w), lambda i: (i, 0)) for w, _ in outs]
    out_specs += [pl.BlockSpec(s, lambda i: (0, 0)) for s in accs]
    out_shape = [jax.ShapeDtypeStruct((n_rows, w), d) for w, d in outs]
    out_shape += [jax.ShapeDtypeStruct(s, F32) for s in accs]
    return pl.pallas_call(
        body, name=name, grid=(n_rows // tile,), in_specs=[s for _, s in ins], out_specs=out_specs,
        out_shape=out_shape, compiler_params=_params("arbitrary"),
    )(*[a for a, _ in ins])


def _row_spec(tile, width, col=0):
    return pl.BlockSpec((tile, width), lambda i: (i, col))


def _full_spec(shape):
    return pl.BlockSpec(shape, lambda i: (0,) * len(shape))


def _prev_rows_spec(tile, width):
    return pl.BlockSpec((8, width), lambda i: (jnp.maximum(i * (tile // 8) - 1, 0), 0))


def _next_rows_spec(tile, width, n_tiles):
    return pl.BlockSpec((8, width), lambda i: (jnp.minimum(i + 1, n_tiles - 1), 0))


def _row_of(x8, idx):
    r = lax.broadcasted_iota(jnp.int32, x8.shape, 0)
    return jnp.sum(jnp.where(r == idx, x8, 0.0), axis=0, keepdims=True)


def _rms(x, g):
    return x * lax.rsqrt(jnp.mean(x * x, axis=-1, keepdims=True) + NORM_EPS) * g


def _shifted(i, z, prev8):
    first = jnp.where(i > 0, _row_of(prev8, 7), 0.0)
    row = lax.broadcasted_iota(jnp.int32, z.shape, 0)
    return jnp.where(row == 0, first, pltpu.roll(z, 1, 0))


def _rw_pre(z, zp, mu, w0, a0, wup, aup, k_k, k_a, *, cw):
    zs = z + (zp - z) * mu
    r, k, v, g = (zs[:, j * cw:(j + 1) * cw] for j in range(4))
    lo = zs[:, 4 * cw:4 * cw + LANES]
    w_raw = w0 + _bdot(jnp.tanh(lo), wup)
    decay = jnp.exp(-jnp.exp(-_softplus(-w_raw) - 0.5))
    a = jax.nn.sigmoid(a0 + _bdot(lo, aup))
    e = _head_matrix(cw)
    kk = k * k_k
    kk = kk / jnp.maximum(jnp.sqrt(_head_sum(kk * kk, e)), 1e-12)
    k_mod = k * (1.0 + (a - 1.0) * k_a)
    return r, decay, k_mod, v, kk, a, g


def _rw_post(y, r, k_mod, v, g, ln_g, ln_b, r_k, *, cw):
    e = _head_matrix(cw)
    mu = _head_sum(y, e) * (1.0 / HEAD)
    d = y - mu
    var = _head_sum(d * d, e) * (1.0 / HEAD)
    yn = d * lax.rsqrt(var + GN_EPS) * ln_g + ln_b
    bonus = _head_sum(r * k_mod * r_k, e) * v
    return (yn + bonus) * (g * jax.nn.sigmoid(g))


def _merge(zg, u_rw, u_fox, *, d):
    return jax.nn.sigmoid(zg[:, :d]) * u_rw + jax.nn.sigmoid(zg[:, d:]) * u_fox


def _head_loss(x1, ple, gl, gf, tgt):
    x2 = x1 + ple * jax.nn.sigmoid(gl)
    err = _rms(x2, gf) - tgt
    return 0.5 * jnp.sum(jnp.mean(err * err, axis=-1, keepdims=True), axis=0, keepdims=True)


def _eliminate(lo):
    n, c, _ = lo.shape
    ri = lax.broadcasted_iota(jnp.int32, (n, c, c), 1)
    ci = lax.broadcasted_iota(jnp.int32, (n, c, c), 2)
    x = (ri == ci).astype(F32)
    for s in range(c - 1):
        col = jnp.sum(jnp.where(ci == s, lo, 0.0), axis=2, keepdims=True)
        row = jnp.sum(jnp.where(ri == s, x, 0.0), axis=1, keepdims=True)
        x = x - col * row
    return x


def _batched(a, b, ca, cb):
    return lax.dot_general(a, b, (((ca,), (cb,)), ((0,), (0,))), precision=SCAN_PREC, preferred_element_type=F32)


@jax.custom_vjp
def _unit_lower_inverse(lo, known):
    return _eliminate(lo) if known is None else known


def _uli_fwd(lo, known):
    x = _unit_lower_inverse(lo, known)
    return x, (x, known)


def _uli_bwd(res, dx):
    x, known = res
    dlo = -_batched(_batched(x, dx, 1, 1), x, 2, 2)
    return dlo, (None if known is None else jnp.zeros_like(known))


_unit_lower_inverse.defvjp(_uli_fwd, _uli_bwd)


def _rwkv_chunk(s0, r, w, k, v, kk, a, *, c, tinv_known=None):
    pairs = range(len(s0))
    lane = lax.broadcasted_iota(jnp.int32, (1, LANES), 1)
    heads = (lane < HEAD, lane >= HEAD)
    ti = lax.broadcasted_iota(jnp.int32, (c, c), 0)
    si = lax.broadcasted_iota(jnp.int32, (c, c), 1)
    incl = si <= ti
    strict = si < ti
    tri = incl.astype(F32)
    logw = [jnp.log(w[p]) for p in pairs]
    cum = [_dot(tri, logw[p], HIGHEST) for p in pairs]
    cum_end = [jnp.sum(logw[p], axis=0, keepdims=True) for p in pairs]
    g_inv = [jnp.exp(-cum[p]) for p in pairs]
    to_end = [jnp.exp(cum_end[p] - cum[p]) for p in pairs]
    b = [kk[p] * a[p] for p in pairs]
    beta = [b[p] * g_inv[p] for p in pairs]
    kap = [kk[p] * jnp.exp(cum[p] - logw[p]) for p in pairs]
    kt = [k[p] * g_inv[p] for p in pairs]
    rt = [r[p] * jnp.exp(cum[p]) for p in pairs]
    lhs = [jnp.concatenate([jnp.where(m, x[p], 0.0) for x in (kap, rt) for m in heads], axis=0) for p in pairs]
    vs_beta = [_dot_nt(lhs[p], beta[p], None) for p in pairs]
    vs_kt = [_dot_nt(lhs[p], kt[p], None) for p in pairs]
    strict2 = jnp.concatenate([strict, strict], axis=0)
    incl2 = jnp.concatenate([incl, incl], axis=0)
    lo = [jnp.where(strict2, vs_beta[p][:2 * c], 0.0) for p in pairs]
    mm = [jnp.where(strict2, vs_kt[p][:2 * c], 0.0) for p in pairs]
    arb = [jnp.where(incl2, vs_beta[p][2 * c:], 0.0) for p in pairs]
    ark = [jnp.where(incl2, vs_kt[p][2 * c:], 0.0) for p in pairs]
    per_head = lambda xs: jnp.concatenate([xs[p][h * c:(h + 1) * c][None] for p in pairs for h in (0, 1)])
    tinv = _unit_lower_inverse(per_head(lo), None if tinv_known is None else per_head(tinv_known))
    tinv = [jnp.concatenate([tinv[2 * p], tinv[2 * p + 1]], axis=0) for p in pairs]
    both = lambda x: jnp.where(heads[0], x[:c], x[c:])
    vs_s = [_dot_nt(jnp.concatenate([kap[p], rt[p]], axis=0), s0[p], None) for p in pairs]
    rhs = [vs_s[p][:c] + both(_dot(mm[p], v[p], None)) for p in pairs]
    u = [-both(_dot(tinv[p], rhs[p], None)) for p in pairs]
    y = [vs_s[p][c:] + both(_dot(arb[p], u[p], None) + _dot(ark[p], v[p], None)) for p in pairs]
    rr = lax.broadcasted_iota(jnp.int32, (LANES, LANES), 0) < HEAD
    cc = lax.broadcasted_iota(jnp.int32, (LANES, LANES), 1) < HEAD
    ds = [_dot_tn(jnp.concatenate([u[p], v[p]], axis=0),
                  jnp.concatenate([b[p] * to_end[p], k[p] * to_end[p]], axis=0), None) for p in pairs]
    s1 = [s0[p] * jnp.exp(cum_end[p]) + jnp.where(rr == cc, ds[p], 0.0) for p in pairs]
    return tuple(y), tuple(s1), tuple(tinv)


def _scan_tiles(t, n_pair):
    return _tile(t, (32, 16, 8)), _tile(t, (512, 256, 128, 64, 32)), _tile(n_pair, (PAIRS_PER_STEP, 2, 1))


def _scan_fwd(r, w, k, v, kk, a):
    t, width = r.shape
    c, tb, npb = _scan_tiles(t, width // LANES)
    n_grp, n_blk, n_cb = width // (LANES * npb), t // tb, tb // c

    def body(r_ref, w_ref, k_ref, v_ref, kk_ref, a_ref, y_ref, st_ref, ti_ref, s_scr):
        @pl.when(pl.program_id(1) == 0)
        def _():
            s_scr[...] = jnp.zeros_like(s_scr)

        def chunk(j, carry):
            sl = pl.ds(pl.multiple_of(j * c, c), c)
            lanes = [pl.ds(q * LANES, LANES) for q in range(npb)]
            s0 = tuple(s_scr[q] for q in range(npb))
            cols = lambda ref: tuple(ref[sl, ln] for ln in lanes)
            y, s1, tinv = _rwkv_chunk(s0, cols(r_ref), cols(w_ref), cols(k_ref), cols(v_ref), cols(kk_ref), cols(a_ref),
                                      c=c)
            for q, ln in enumerate(lanes):
                st_ref[q, j] = s0[q]
                ti_ref[q, j] = tinv[q]
                y_ref[sl, ln] = y[q]
                s_scr[q] = s1[q]
            return carry

        lax.fori_loop(0, n_cb, chunk, 0)

    blk = pl.BlockSpec((tb, npb * LANES), lambda p, i: (i, p))
    return pl.pallas_call(
        body, name="rwkv_scan_fwd", grid=(n_grp, n_blk), in_specs=[blk] * 6,
        out_specs=[blk, pl.BlockSpec((npb, n_cb, LANES, LANES), lambda p, i: (p, i, 0, 0)),
                   pl.BlockSpec((npb, n_cb, 2 * c, c), lambda p, i: (p, i, 0, 0))],
        out_shape=[jax.ShapeDtypeStruct((t, width), F32),
                   jax.ShapeDtypeStruct((width // LANES, t // c, LANES, LANES), F32),
                   jax.ShapeDtypeStruct((width // LANES, t // c, 2 * c, c), F32)],
        scratch_shapes=[pltpu.VMEM((npb, LANES, LANES), F32)],
        compiler_params=_params("arbitrary", "arbitrary"),
    )(r, w, k, v, kk, a)


def _scan_bwd(r, w, k, v, kk, a, st, ti, dy):
    t, width = r.shape
    c, tb, npb = _scan_tiles(t, width // LANES)
    n_grp, n_blk, n_cb = width // (LANES * npb), t // tb, tb // c

    def body(r_ref, w_ref, k_ref, v_ref, kk_ref, a_ref, st_ref, ti_ref, dy_ref,
             dr_ref, dw_ref, dk_ref, dv_ref, dkk_ref, da_ref, ds_scr):
        @pl.when(pl.program_id(1) == 0)
        def _():
            ds_scr[...] = jnp.zeros_like(ds_scr)

        def chunk(jj, carry):
            j = n_cb - 1 - jj
            sl = pl.ds(pl.multiple_of(j * c, c), c)
            lanes = [pl.ds(q * LANES, LANES) for q in range(npb)]
            cols = lambda ref: tuple(ref[sl, ln] for ln in lanes)
            args = (tuple(st_ref[q, j] for q in range(npb)), cols(r_ref), cols(w_ref), cols(k_ref), cols(v_ref),
                    cols(kk_ref), cols(a_ref))
            known = tuple(ti_ref[q, j] for q in range(npb))
            _, vjp = jax.vjp(lambda *xs: _rwkv_chunk(*xs, c=c, tinv_known=known)[:2], *args)
            grads = vjp((cols(dy_ref), tuple(ds_scr[q] for q in range(npb))))
            for q, ln in enumerate(lanes):
                ds_scr[q] = grads[0][q]
                for ref, g in zip((dr_ref, dw_ref, dk_ref, dv_ref, dkk_ref, da_ref), grads[1:]):
                    ref[sl, ln] = g[q]
            return carry

        lax.fori_loop(0, n_cb, chunk, 0)

    blk = pl.BlockSpec((tb, npb * LANES), lambda p, i: (n_blk - 1 - i, p))
    stb = pl.BlockSpec((npb, n_cb, LANES, LANES), lambda p, i: (p, n_blk - 1 - i, 0, 0))
    tib = pl.BlockSpec((npb, n_cb, 2 * c, c), lambda p, i: (p, n_blk - 1 - i, 0, 0))
    return pl.pallas_call(
        body, name="rwkv_scan_bwd", grid=(n_grp, n_blk), in_specs=[blk] * 6 + [stb, tib, blk], out_specs=[blk] * 6,
        out_shape=[jax.ShapeDtypeStruct((t, width), F32)] * 6,
        scratch_shapes=[pltpu.VMEM((npb, LANES, LANES), F32)],
        compiler_params=_params("arbitrary", "arbitrary"),
    )(r, w, k, v, kk, a, st, ti, dy)


NEG = -1e30


def _fox_cumsum(zf, b_pad, *, fw, fh):
    t = zf.shape[0]
    tile = _tile(t, (256, 128, 64, 32, 16, 8))

    def body(fl_ref, b_ref, c_ref, carry):
        @pl.when(pl.program_id(0) == 0)
        def _():
            carry[...] = jnp.zeros_like(carry)

        lane = lax.broadcasted_iota(jnp.int32, (tile, LANES), 1)
        logf = jnp.where(lane < fh, -_softplus(-(fl_ref[...] + b_ref[...])), 0.0)
        ri = lax.broadcasted_iota(jnp.int32, (tile, tile), 0)
        ci = lax.broadcasted_iota(jnp.int32, (tile, tile), 1)
        c_ref[...] = carry[...] + _dot((ci <= ri).astype(F32), logf, HIGHEST)
        carry[...] += jnp.sum(logf, axis=0, keepdims=True)

    return pl.pallas_call(
        body, name="fox_cumsum", grid=(t // tile,),
        in_specs=[_row_spec(tile, LANES, 4 * fw // LANES), _full_spec((1, LANES))],
        out_specs=_row_spec(tile, LANES), out_shape=jax.ShapeDtypeStruct((t, LANES), F32),
        scratch_shapes=[pltpu.VMEM((1, LANES), F32)], compiler_params=_params("arbitrary"),
    )(zf, b_pad)


def _fox_cumsum_bwd(zf, b_pad, dc, *, fw, fh):
    t = zf.shape[0]
    tile = _tile(t, (256, 128, 64, 32, 16, 8))
    n = t // tile

    def body(fl_ref, b_ref, dc_ref, dfl_ref, db_ref, carry):
        i = pl.program_id(0)

        @pl.when(i == 0)
        def _():
            carry[...] = jnp.zeros_like(carry)
            db_ref[...] = jnp.zeros_like(db_ref)

        lane = lax.broadcasted_iota(jnp.int32, (tile, LANES), 1)
        dc_t = dc_ref[...]
        ri = lax.broadcasted_iota(jnp.int32, (tile, tile), 0)
        ci = lax.broadcasted_iota(jnp.int32, (tile, tile), 1)
        dlogf = carry[...] + _dot((ci >= ri).astype(F32), dc_t, HIGHEST)
        carry[...] += jnp.sum(dc_t, axis=0, keepdims=True)
        dfl = jnp.where(lane < fh, dlogf * jax.nn.sigmoid(-(fl_ref[...] + b_ref[...])), 0.0)
        dfl_ref[...] = dfl.astype(dfl_ref.dtype)
        db_ref[...] += jnp.sum(dfl, axis=0, keepdims=True)

    rev = lambda col: pl.BlockSpec((tile, LANES), lambda i: (n - 1 - i, col))
    return pl.pallas_call(
        body, name="fox_cumsum_bwd", grid=(n,),
        in_specs=[rev(4 * fw // LANES), _full_spec((1, LANES)), rev(0)],
        out_specs=[rev(0), _full_spec((1, LANES))],
        out_shape=[jax.ShapeDtypeStruct((t, LANES), BF16), jax.ShapeDtypeStruct((1, LANES), F32)],
        scratch_shapes=[pltpu.VMEM((1, LANES), F32)], compiler_params=_params("arbitrary"),
    )(zf, b_pad, dc)


def _fox_tile(t):
    return _tile(t, (512, 256, 128))


def _fox_fwd(zf, c, ct, *, fw):
    t = zf.shape[0]
    tq = _fox_tile(t)
    th = tq // 2
    n_pair, n_q = fw // LANES, t // tq
    scale = HEAD ** -0.5
    chains = [(h, qq) for h in (0, 1) for qq in (0, 1)]

    def body(q_ref, k_ref, v_ref, g_ref, c_ref, ct_ref, o_ref, lse_ref, y_ref):
        hp, i = pl.program_id(0), pl.program_id(1)
        lane = lax.broadcasted_iota(jnp.int32, (1, LANES), 1)
        in_head = (lane < HEAD, lane >= HEAD)
        rows = [pl.ds(qq * th, th) for qq in (0, 1)]
        qh = [jnp.where(in_head[h], q_ref[rows[qq], :] * scale, 0.0).astype(BF16) for h, qq in chains]
        cq = [_lane_pick(c_ref[rows[qq], :], 2 * hp + h) for h, qq in chains]
        qidx = lax.broadcasted_iota(jnp.int32, (th, tq), 0)
        kidx = lax.broadcasted_iota(jnp.int32, (th, tq), 1)

        def kv_step(j, carry, diagonal):
            m, l, acc = carry
            ks = pl.ds(pl.multiple_of(j * tq, tq), tq)
            kb = k_ref[ks, :].astype(BF16)
            vb = v_ref[ks, :]
            vh = [jnp.where(in_head[h], vb, 0.0).astype(BF16) for h in (0, 1)]
            ck = [ct_ref[0, j, pl.ds(h, 1), :] for h in (0, 1)]
            s = [_dot_nt(qh[n], kb) + cq[n] - ck[h] for n, (h, qq) in enumerate(chains)]
            if diagonal:
                s = [jnp.where(qq * th + qidx >= kidx, s[n], NEG) for n, (h, qq) in enumerate(chains)]
            m_new = [jnp.maximum(m[n], jnp.max(s[n], axis=1, keepdims=True)) for n in range(4)]
            p = [jnp.exp(s[n] - m_new[n]) for n in range(4)]
            alpha = [jnp.exp(m[n] - m_new[n]) for n in range(4)]
            l = [l[n] * alpha[n] + jnp.sum(p[n], axis=1, keepdims=True) for n in range(4)]
            pv = [_dot(p[n].astype(BF16), vh[h]) for n, (h, qq) in enumerate(chains)]
            acc = [acc[qq] * jnp.where(in_head[0], alpha[qq], alpha[2 + qq]) + pv[qq] + pv[2 + qq] for qq in (0, 1)]
            return tuple(m_new), tuple(l), tuple(acc)

        init = (tuple(jnp.full((th, 1), NEG, F32) for _ in chains), tuple(jnp.zeros((th, 1), F32) for _ in chains),
                tuple(jnp.zeros((th, LANES), F32) for _ in (0, 1)))
        carry = lax.fori_loop(0, i, functools.partial(kv_step, diagonal=False), init)
        m, l, acc = kv_step(i, carry, True)
        for qq in (0, 1):
            o = acc[qq] / jnp.where(in_head[0], l[qq], l[2 + qq])
            g = g_ref[rows[qq], :]
            o_ref[rows[qq], :] = o
            lse_ref[rows[qq], :] = jnp.where(in_head[0], m[qq] + jnp.log(l[qq]), m[2 + qq] + jnp.log(l[2 + qq]))
            y_ref[rows[qq], :] = (o * (g * jax.nn.sigmoid(g))).astype(y_ref.dtype)

    npw = fw // LANES
    blk = lambda col0: pl.BlockSpec((tq, LANES), lambda hp, i: (i, col0 + hp))
    res = lambda col0: pl.BlockSpec((t, LANES), lambda hp, i: (0, col0 + hp))
    out_blk = pl.BlockSpec((tq, LANES), lambda hp, i: (i, hp))
    return pl.pallas_call(
        body, name="fox_attn_fwd", grid=(n_pair, n_q),
        in_specs=[blk(0), res(npw), res(2 * npw), blk(3 * npw),
                  pl.BlockSpec((tq, LANES), lambda hp, i: (i, 0)),
                  pl.BlockSpec((1, n_q, 2, tq), lambda hp, i: (hp, 0, 0, 0))],
        out_specs=[out_blk, out_blk, out_blk],
        out_shape=[jax.ShapeDtypeStruct((t, fw), F32), jax.ShapeDtypeStruct((t, fw), F32),
                   jax.ShapeDtypeStruct((t, fw), BF16)],
        compiler_params=_params("arbitrary", "arbitrary"),
    )(zf, zf, zf, zf, c, ct)


def _fox_bwd(zf, do, c, ct, lse_r, dd_r, *, fw):
    t = zf.shape[0]
    tq = _fox_tile(t)
    n_pair, n_q = fw // LANES, t // tq
    scale = HEAD ** -0.5

    def body(q_ref, k_ref, v_ref, do_ref, c_ref, ct_ref, lse_ref, dd_ref,
             dq_ref, dk_ref, dv_ref, dcq_ref, dck_ref, dq_acc, dcq_acc):
        hp, j = pl.program_id(0), pl.program_id(1)

        @pl.when(j == 0)
        def _():
            dq_acc[...] = jnp.zeros_like(dq_acc)
            dcq_acc[...] = jnp.zeros_like(dcq_acc)

        lane = lax.broadcasted_iota(jnp.int32, (1, LANES), 1)
        in_head = (lane < HEAD, lane >= HEAD)
        kb = k_ref[...]
        kh = [jnp.where(m, kb, 0.0).astype(BF16) for m in in_head]
        vb = v_ref[...].astype(BF16)
        c_k = c_ref[...]
        ck = [_lane_pick(c_k, 2 * hp + h) for h in (0, 1)]
        kidx = lax.broadcasted_iota(jnp.int32, (tq, tq), 0)
        qidx = lax.broadcasted_iota(jnp.int32, (tq, tq), 1)

        def q_step(i, carry, diagonal):
            dk, dv, dck = carry
            qs = pl.ds(pl.multiple_of(i * tq, tq), tq)
            qf = q_ref[qs, :] * scale
            dof = do_ref[qs, :]
            qh = [jnp.where(m, qf, 0.0).astype(BF16) for m in in_head]
            doh = [jnp.where(m, dof, 0.0).astype(BF16) for m in in_head]
            row = lambda ref, h: ref[0, i, pl.ds(h, 1), :]
            st = [_dot_nt(kh[h], qh[h]) + row(ct_ref, h) - ck[h] for h in (0, 1)]
            p = [jnp.exp(st[h] - row(lse_ref, h)) for h in (0, 1)]
            if diagonal:
                p = [jnp.where(kidx <= qidx, p[h], 0.0) for h in (0, 1)]
            dst = [p[h] * (_dot_nt(vb, doh[h]) - row(dd_ref, h)) for h in (0, 1)]
            p16 = [x.astype(BF16) for x in p]
            ds16 = [x.astype(BF16) for x in dst]
            dv = dv + _dot(p16[0], doh[0]) + _dot(p16[1], doh[1])
            dk = dk + _dot(ds16[0], qh[0]) + _dot(ds16[1], qh[1])
            dq_acc[qs, :] += _dot_tn(ds16[0], kh[0]) + _dot_tn(ds16[1], kh[1])
            for h in (0, 1):
                dcq_acc[i, pl.ds(h, 1), :] += jnp.sum(dst[h], axis=0, keepdims=True)
            dck = tuple(dck[h] - jnp.sum(dst[h], axis=1, keepdims=True) for h in (0, 1))
            return dk, dv, dck

        zero = jnp.zeros((tq, LANES), F32)
        carry = q_step(j, (zero, zero, (jnp.zeros((tq, 1), F32),) * 2), True)
        dk, dv, dck = lax.fori_loop(j + 1, n_q, functools.partial(q_step, diagonal=False), carry)
        dk_ref[...] = dk.astype(dk_ref.dtype)
        dv_ref[...] = dv.astype(dv_ref.dtype)
        dck_ref[...] = jnp.where(lane == 0, dck[0], jnp.where(lane == 1, dck[1], 0.0))

        @pl.when(j == n_q - 1)
        def _():
            dq_ref[...] = (dq_acc[...] * scale).astype(dq_ref.dtype)
            dcq_ref[0] = dcq_acc[...]

    npw = fw // LANES
    res_z = lambda col0: pl.BlockSpec((t, LANES), lambda hp, j: (0, col0 + hp))
    blk_z = lambda col0: pl.BlockSpec((tq, LANES), lambda hp, j: (j, col0 + hp))
    res = pl.BlockSpec((t, LANES), lambda hp, j: (0, hp))
    blk = pl.BlockSpec((tq, LANES), lambda hp, j: (j, hp))
    rows = pl.BlockSpec((1, n_q, 2, tq), lambda hp, j: (hp, 0, 0, 0))
    return pl.pallas_call(
        body, name="fox_attn_bwd", grid=(n_pair, n_q),
        in_specs=[res_z(0), blk_z(npw), blk_z(2 * npw), res, pl.BlockSpec((tq, LANES), lambda hp, j: (j, 0)),
                  rows, rows, rows],
        out_specs=[res, blk, blk, rows, blk],
        out_shape=[jax.ShapeDtypeStruct((t, fw), BF16), jax.ShapeDtypeStruct((t, fw), BF16),
                   jax.ShapeDtypeStruct((t, fw), BF16), jax.ShapeDtypeStruct((n_pair, n_q, 2, tq), F32),
                   jax.ShapeDtypeStruct((t, fw), F32)],
        scratch_shapes=[pltpu.VMEM((t, LANES), F32), pltpu.VMEM((n_q, 2, tq), F32)],
        compiler_params=_params("arbitrary", "arbitrary"),
    )(zf, zf, zf, do, c, ct, lse_r, dd_r)


def _adamw_math(w, g, m, v):
    m = ADAM_B1 * m + (1.0 - ADAM_B1) * g
    v = ADAM_B2 * v + (1.0 - ADAM_B2) * jnp.square(g)
    m_hat = m / (1.0 - ADAM_B1 ** ADAM_STEP)
    v_hat = v / (1.0 - ADAM_B2 ** ADAM_STEP)
    delta = -ADAM_LR * (m_hat / (jnp.sqrt(v_hat) + ADAM_EPS) + ADAM_WD * w)
    return delta, m, v


def _adamw(name, w, g, m, v):
    lead, rows, cols = w.shape
    if lead == 1 and cols % LANES:
        outs = _adamw(name, *[jnp.transpose(a, (2, 0, 1)) for a in (w, g, m, v)])
        return [jnp.transpose(o, (1, 2, 0)) for o in outs]
    if lead == 1:
        tile = _tile(rows, (128, 64, 32, 16, 8))
        spec, steps = pl.BlockSpec((1, tile, cols), lambda i: (0, i, 0)), rows // tile
    else:
        tile = _div_tile(lead, 256, 1)
        spec, steps = pl.BlockSpec((tile, rows, cols), lambda i: (i, 0, 0)), lead // tile

    def body(w_ref, g_ref, m_ref, v_ref, d_ref, mo_ref, vo_ref):
        d_ref[...], mo_ref[...], vo_ref[...] = _adamw_math(w_ref[...], g_ref[...], m_ref[...], v_ref[...])

    return pl.pallas_call(
        body, name=name, grid=(steps,), in_specs=[spec] * 4, out_specs=[spec] * 3,
        out_shape=[jax.ShapeDtypeStruct(w.shape, F32)] * 3, compiler_params=_params("parallel"),
    )(w, g, m, v)


def _place():
    return lax.axis_index("x"), lax.axis_index("y"), lax.axis_index("c")


def _other_chips(x, y):
    return [(1 - x, y), (x, 1 - y), (1 - x, 1 - y)]


HBM_SPEC = pl.BlockSpec(memory_space=pltpu.HBM)


def _all_gather_shards(slabs):
    n = len(slabs)

    def body(*refs):
        src_refs, out_refs, send_sems, recv_sems = refs[:n], refs[n:2 * n], refs[2 * n], refs[2 * n + 1]
        x, y, c = _place()
        me = 2 * x + y
        sibling = (x, y, 1 - c)
        chips = _other_chips(x, y)
        first, passed, waits = [], [], []
        for g, (src_ref, out_ref) in enumerate(zip(src_refs, out_refs)):
            rh = src_ref.shape[0] // 2

            def part(chip, half, out_ref=out_ref, rh=rh):
                return out_ref.at[chip, pl.ds(half * rh, rh), :]

            def copy(k, src, dst, to, g=g):
                return pltpu.make_async_remote_copy(src_ref=src, dst_ref=dst, send_sem=send_sems.at[6 * g + k],
                                                    recv_sem=recv_sems.at[6 * g + k], device_id=to, device_id_type=MESH)

            first += [copy(j, src_ref.at[pl.ds(c * rh, rh), :], part(me, c), (px, py, c))
                      for j, (px, py) in enumerate(chips)]
            for j, (px, py) in enumerate(chips):
                theirs = part(2 * px + py, c)
                passed.append((copy(j, theirs, theirs, sibling), copy(3 + j, theirs, theirs, sibling)))
                other = part(2 * px + py, 1 - c)
                waits.append(copy(3 + j, other, other, sibling))
        for cp in first:
            cp.start()
        for landed, forward in passed:
            landed.wait_recv()
            forward.start()
        for cp in waits:
            cp.wait_recv()
        for cp in first + [fwd for _, fwd in passed]:
            cp.wait_send()

    return pl.pallas_call(
        body, name="weights_all_gather", in_specs=[HBM_SPEC] * n, out_specs=[HBM_SPEC] * n,
        out_shape=[jax.ShapeDtypeStruct((N_SHARD,) + a.shape, a.dtype) for a in slabs],
        scratch_shapes=[pltpu.SemaphoreType.DMA((6 * n,)), pltpu.SemaphoreType.DMA((6 * n,))],
    )(*slabs)


def _chip_index():
    return jnp.reshape(2 * lax.axis_index("x") + lax.axis_index("y"), (1,)).astype(jnp.int32)


def _place_own_shard(name, gathered, slab):
    rows, width = slab.shape
    tile = _div_tile(rows, 256, 16)

    def body(me_ref, s_ref, g_ref, o_ref):
        o_ref[0] = s_ref[...]

    return pl.pallas_call(
        body, name=name,
        grid_spec=pltpu.PrefetchScalarGridSpec(
            num_scalar_prefetch=1, grid=(rows // tile,),
            in_specs=[pl.BlockSpec((tile, width), lambda i, me: (i, 0)), pl.BlockSpec(memory_space=pl.ANY)],
            out_specs=pl.BlockSpec((1, tile, width), lambda i, me: (me[0], i, 0))),
        out_shape=jax.ShapeDtypeStruct(gathered.shape, gathered.dtype), input_output_aliases={2: 0},
        compiler_params=_params("parallel"),
    )(_chip_index(), slab, gathered)


def _sibling_exchange(gs):
    n = len(gs)

    def body(*refs):
        g_refs, out_refs, send_sems, recv_sems = refs[:n], refs[n:2 * n], refs[2 * n], refs[2 * n + 1]
        x, y, c = _place()
        copies = [pltpu.make_async_remote_copy(
            src_ref=g_ref.at[s, 1 - c], dst_ref=out_ref.at[s], send_sem=send_sems.at[N_SHARD * g + s],
            recv_sem=recv_sems.at[N_SHARD * g + s], device_id=(x, y, 1 - c), device_id_type=MESH)
            for g, (g_ref, out_ref) in enumerate(zip(g_refs, out_refs)) for s in range(N_SHARD)]
        for cp in copies:
            cp.start()
        for cp in copies:
            cp.wait()

    return pl.pallas_call(
        body, name="grad_sibling_exchange", in_specs=[HBM_SPEC] * n, out_specs=[HBM_SPEC] * n,
        out_shape=[jax.ShapeDtypeStruct((N_SHARD,) + g.shape[2:], g.dtype) for g in gs],
        scratch_shapes=[pltpu.SemaphoreType.DMA((N_SHARD * n,)), pltpu.SemaphoreType.DMA((N_SHARD * n,))],
    )(*gs)


def _add_sibling(name, g, got):
    _, _, rh, width = g.shape
    tile = _div_tile(rh, 256, 16)
    c_arr = jnp.reshape(lax.axis_index("c"), (1,)).astype(jnp.int32)

    def body(c_ref, a_ref, b_ref, o_ref):
        o_ref[...] = (a_ref[0] + b_ref[...]).astype(o_ref.dtype)

    return pl.pallas_call(
        body, name=name,
        grid_spec=pltpu.PrefetchScalarGridSpec(
            num_scalar_prefetch=1, grid=(N_SHARD, rh // tile),
            in_specs=[pl.BlockSpec((1, 1, tile, width), lambda s, i, c: (s, c[0], i, 0)),
                      pl.BlockSpec((1, tile, width), lambda s, i, c: (s, i, 0))],
            out_specs=pl.BlockSpec((1, tile, width), lambda s, i, c: (s, i, 0))),
        out_shape=jax.ShapeDtypeStruct((N_SHARD, rh, width), BF16),
        compiler_params=_params("parallel", "parallel"),
    )(c_arr, g, got)


SEM_SPEC = pl.BlockSpec(memory_space=pltpu.SEMAPHORE)


def _exchange_copies(p_refs, land_refs, send_sems, recv_sems):
    x, y, c = _place()
    me = 2 * x + y
    return [pltpu.make_async_remote_copy(
        src_ref=p_ref.at[2 * px + py], dst_ref=land_ref.at[me], send_sem=send_sems.at[3 * g + j],
        recv_sem=recv_sems.at[3 * g + j], device_id=(px, py, c), device_id_type=MESH)
        for g, (p_ref, land_ref) in enumerate(zip(p_refs, land_refs)) for j, (px, py) in enumerate(_other_chips(x, y))]


def _chip_exchange_start(ps):
    n = len(ps)

    def body(*refs):
        for cp in _exchange_copies(refs[:n], refs[n:2 * n], refs[2 * n], refs[2 * n + 1]):
            cp.start()
        refs[-1][...] = jnp.zeros_like(refs[-1])

    hbm = lambda a: pltpu.with_memory_space_constraint(a, pltpu.HBM)
    out = pl.pallas_call(
        body, name="grad_chip_exchange_start",
        out_shape=(pltpu.SemaphoreType.DMA((3 * n,)), pltpu.SemaphoreType.DMA((3 * n,)),
                   *[pltpu.HBM(a.shape, a.dtype) for a in ps], *[pltpu.HBM(a.shape, a.dtype) for a in ps],
                   jax.ShapeDtypeStruct((8, LANES), F32)),
        in_specs=[HBM_SPEC] * (2 * n),
        out_specs=(SEM_SPEC, SEM_SPEC, *[HBM_SPEC] * (2 * n), pl.BlockSpec(memory_space=pltpu.VMEM)),
        input_output_aliases={i: 2 + i for i in range(2 * n)},
        compiler_params=pltpu.CompilerParams(has_side_effects=pltpu.SideEffectType.DATAFLOW_SIDE_EFFECTING),
    )(*[hbm(a) for a in ps], *[hbm(lax.empty(a.shape, a.dtype)) for a in ps])
    return out[0], out[1], list(out[2:2 + n]), list(out[2 + n:2 + 2 * n]), out[-1]


def _chip_exchange_wait(send_sems, recv_sems, ps, landed, after):
    n = len(ps)

    def body(*refs):
        for cp in _exchange_copies(refs[:n], refs[n:2 * n], refs[2 * n], refs[2 * n + 1]):
            cp.wait_send()
            cp.wait_recv()

    out = pl.pallas_call(
        body, name="grad_chip_exchange_wait",
        out_shape=[pltpu.HBM(a.shape, a.dtype) for a in ps + landed],
        in_specs=[HBM_SPEC] * (2 * n) + [SEM_SPEC, SEM_SPEC, pl.BlockSpec(memory_space=pl.ANY)],
        out_specs=[HBM_SPEC] * (2 * n), input_output_aliases={i: i for i in range(2 * n)},
        compiler_params=pltpu.CompilerParams(has_side_effects=pltpu.SideEffectType.DATAFLOW_SIDE_EFFECTING),
    )(*ps, *landed, send_sems, recv_sems, after)
    return list(out[:n]), list(out[n:])


def _sum_chips(name, p, got):
    _, rh, width = p.shape
    tile = _div_tile(rh, 256, 16)
    n_t = rh // tile
    place = jnp.stack([2 * lax.axis_index("x") + lax.axis_index("y"), lax.axis_index("c")]).astype(jnp.int32)

    def body(pl_ref, own_ref, r0, r1, r2, r3, o_ref):
        me = pl_ref[0]
        own = own_ref[0].astype(F32)
        t = [jnp.where(me == s, own, r[0].astype(F32)) for s, r in enumerate((r0, r1, r2, r3))]
        o_ref[...] = ((t[0] + t[1]) + t[2]) + t[3]

    def slot(s):
        return pl.BlockSpec((1, tile, width), lambda i, pc: (jnp.where(pc[0] == s, (s + 1) % N_SHARD, s), i, 0))

    return pl.pallas_call(
        body, name=name,
        grid_spec=pltpu.PrefetchScalarGridSpec(
            num_scalar_prefetch=1, grid=(n_t,),
            in_specs=[pl.BlockSpec((1, tile, width), lambda i, pc: (pc[0], i, 0))] + [slot(s) for s in range(N_SHARD)],
            out_specs=pl.BlockSpec((tile, width), lambda i, pc: (pc[1] * n_t + i, 0))),
        out_shape=jax.ShapeDtypeStruct((2 * rh, width), F32), compiler_params=_params("parallel"),
    )(place, p, got, got, got, got)


def _join_halves(fulls):
    n = len(fulls)

    def body(*refs):
        f_refs, out_refs, send_sems, recv_sems = refs[:n], refs[n:2 * n], refs[2 * n], refs[2 * n + 1]
        x, y, c = _place()

        def copy(g, half):
            rh = f_refs[g].shape[0] // 2
            return pltpu.make_async_remote_copy(
                src_ref=f_refs[g].at[pl.ds(half * rh, rh), :], dst_ref=out_refs[g].at[pl.ds(half * rh, rh), :],
                send_sem=send_sems.at[g], recv_sem=recv_sems.at[g], device_id=(x, y, 1 - c), device_id_type=MESH)

        for g in range(n):
            copy(g, c).start()
        for g in range(n):
            copy(g, c).wait_send()
            copy(g, 1 - c).wait_recv()

    return pl.pallas_call(
        body, name="grad_join_halves", in_specs=[HBM_SPEC] * n, out_specs=[HBM_SPEC] * n,
        out_shape=[jax.ShapeDtypeStruct(f.shape, f.dtype) for f in fulls],
        input_output_aliases={g: g for g in range(n)},
        scratch_shapes=[pltpu.SemaphoreType.DMA((n,)), pltpu.SemaphoreType.DMA((n,))],
    )(*fulls)


def _reduce_scatter_start(gs):
    gs = [g.reshape(N_SHARD, 2, g.shape[1] // 2, g.shape[2]) for g in gs]
    got = _sibling_exchange(gs)
    chip_sums = [_add_sibling(f"grad_add_sibling_{i}", g, r) for i, (g, r) in enumerate(zip(gs, got))]
    *state, token = _chip_exchange_start(chip_sums)
    return state, token


def _reduce_scatter_end(state, after):
    chip_sums, landed = _chip_exchange_wait(*state, after)
    return _join_halves([_sum_chips(f"grad_sum_chips_{i}", p, r) for i, (p, r) in enumerate(zip(chip_sums, landed))])


def _all_reduce_small(v):
    rows = v.shape[0]

    def body(v_ref, out_ref, gather, send_sems, recv_sems):
        x, y, c = _place()
        gather[4 * x + 2 * y + c] = v_ref[...]
        flips = [(dx, dy, dc) for dx in (0, 1) for dy in (0, 1) for dc in (0, 1)][1:]
        peers = [((x + dx) % 2, (y + dy) % 2, (c + dc) % 2) for dx, dy, dc in flips]
        copies = [pltpu.make_async_remote_copy(
            src_ref=v_ref, dst_ref=gather.at[4 * x + 2 * y + c], send_sem=send_sems.at[j], recv_sem=recv_sems.at[j],
            device_id=peer, device_id_type=MESH) for j, peer in enumerate(peers)]
        for cp in copies:
            cp.start()
        for j, (px, py, pc) in enumerate(peers):
            pltpu.make_async_remote_copy(
                src_ref=v_ref, dst_ref=gather.at[4 * px + 2 * py + pc], send_sem=send_sems.at[j],
                recv_sem=recv_sems.at[j], device_id=(px, py, pc), device_id_type=MESH).wait_recv()
        for cp in copies:
            cp.wait_send()
        acc = gather[0]
        for d in range(1, 8):
            acc = acc + gather[d]
        out_ref[...] = acc

    vm = pl.BlockSpec(memory_space=pltpu.VMEM)
    return pl.pallas_call(
        body, name="small_grads_all_reduce", in_specs=[vm], out_specs=vm,
        out_shape=jax.ShapeDtypeStruct(v.shape, F32),
        scratch_shapes=[pltpu.VMEM((8, rows, LANES), F32), pltpu.SemaphoreType.DMA((7,)), pltpu.SemaphoreType.DMA((7,))],
    )(v)


def _pad_lanes(v):
    v = v.reshape(1, -1)
    return jnp.pad(v, ((0, 0), (0, -v.shape[1] % LANES)))


def _pack_small(vs, rows):
    flat = jnp.concatenate([_pad_lanes(v) for v in vs], axis=1)
    return jnp.pad(flat, ((0, 0), (0, rows * LANES - flat.shape[1]))).reshape(rows, LANES)


def _unpack_small(packed, shapes):
    flat = packed.reshape(-1)
    out, off = [], 0
    for s in shapes:
        n = 1
        for d in s:
            n *= d
        out.append(flat[off:off + n].reshape(s))
        off += n + (-n % LANES)
    return out


BIG = ("w_in", "rw_w_lora_up", "rw_a_lora_up", "w_up_rwkv", "w_up_fox", "w_out", "ple_proj", "ple_gate_w")
ROW_SHARDED = ("w_out", "ple_gate_w")
SMALL = ("norm_g", "rw_shift_mu", "rw_w0", "rw_a0", "rw_k_k", "rw_k_a", "rw_r_k", "rw_ln_g", "rw_ln_b", "fox_b_f",
         "ple_norm_g", "final_norm_g")
WEIGHTS = ("norm_g", "w_in", "rw_shift_mu", "rw_w0", "rw_w_lora_up", "rw_a0", "rw_a_lora_up", "rw_k_k", "rw_k_a",
           "rw_r_k", "rw_ln_g", "rw_ln_b", "fox_b_f", "w_up_rwkv", "w_up_fox", "w_out", "ple_proj", "ple_gate_w",
           "ple_norm_g", "final_norm_g")


def kernel(x, p, norm_g, w_in, rw_shift_mu, rw_w0, rw_w_lora_up, rw_a0, rw_a_lora_up, rw_k_k, rw_k_a, rw_r_k, rw_ln_g, rw_ln_b, fox_b_f, w_up_rwkv, w_up_fox, w_out, ple_proj, ple_gate_w, ple_norm_g, final_norm_g, loss_target, m_norm_g, m_w_in, m_rw_shift_mu, m_rw_w0, m_rw_w_lora_up, m_rw_a0, m_rw_a_lora_up, m_rw_k_k, m_rw_k_a, m_rw_r_k, m_rw_ln_g, m_rw_ln_b, m_fox_b_f, m_w_up_rwkv, m_w_up_fox, m_w_out, m_ple_proj, m_ple_gate_w, m_ple_norm_g, m_final_norm_g, v_norm_g, v_w_in, v_rw_shift_mu, v_rw_w0, v_rw_w_lora_up, v_rw_a0, v_rw_a_lora_up, v_rw_k_k, v_rw_k_a, v_rw_r_k, v_rw_ln_g, v_rw_ln_b, v_fox_b_f, v_w_up_rwkv, v_w_up_fox, v_w_out, v_ple_proj, v_ple_gate_w, v_ple_norm_g, v_final_norm_g):
    wts = dict(norm_g=norm_g, w_in=w_in, rw_shift_mu=rw_shift_mu, rw_w0=rw_w0, rw_w_lora_up=rw_w_lora_up, rw_a0=rw_a0,
               rw_a_lora_up=rw_a_lora_up, rw_k_k=rw_k_k, rw_k_a=rw_k_a, rw_r_k=rw_r_k, rw_ln_g=rw_ln_g, rw_ln_b=rw_ln_b,
               fox_b_f=fox_b_f, w_up_rwkv=w_up_rwkv, w_up_fox=w_up_fox, w_out=w_out, ple_proj=ple_proj,
               ple_gate_w=ple_gate_w, ple_norm_g=ple_norm_g, final_norm_g=final_norm_g)
    mom = dict(norm_g=m_norm_g, w_in=m_w_in, rw_shift_mu=m_rw_shift_mu, rw_w0=m_rw_w0, rw_w_lora_up=m_rw_w_lora_up,
               rw_a0=m_rw_a0, rw_a_lora_up=m_rw_a_lora_up, rw_k_k=m_rw_k_k, rw_k_a=m_rw_k_a, rw_r_k=m_rw_r_k,
               rw_ln_g=m_rw_ln_g, rw_ln_b=m_rw_ln_b, fox_b_f=m_fox_b_f, w_up_rwkv=m_w_up_rwkv, w_up_fox=m_w_up_fox,
               w_out=m_w_out, ple_proj=m_ple_proj, ple_gate_w=m_ple_gate_w, ple_norm_g=m_ple_norm_g,
               final_norm_g=m_final_norm_g)
    vel = dict(norm_g=v_norm_g, w_in=v_w_in, rw_shift_mu=v_rw_shift_mu, rw_w0=v_rw_w0, rw_w_lora_up=v_rw_w_lora_up,
               rw_a0=v_rw_a0, rw_a_lora_up=v_rw_a_lora_up, rw_k_k=v_rw_k_k, rw_k_a=v_rw_k_a, rw_r_k=v_rw_r_k,
               rw_ln_g=v_rw_ln_g, rw_ln_b=v_rw_ln_b, fox_b_f=v_fox_b_f, w_up_rwkv=v_w_up_rwkv, w_up_fox=v_w_up_fox,
               w_out=v_w_out, ple_proj=v_ple_proj, ple_gate_w=v_ple_gate_w, ple_norm_g=v_ple_norm_g,
               final_norm_g=v_final_norm_g)

    t, d = x.shape[1], x.shape[2]
    cw = rw_w0.shape[1]
    lr = rw_w_lora_up.shape[1]
    fh = fox_b_f.shape[1]
    fw = fh * HEAD
    rw_cols = 4 * cw + 2 * lr
    fox_cols = 4 * fw + fh
    assert 2 * lr == LANES and cw % LANES == 0 and fw % LANES == 0 and fh <= LANES
    xs = x[0]
    ps = p[0, 0]
    tgt = loss_target[0]

    groups = {}
    for n in BIG:
        groups.setdefault(wts[n].shape[2], []).append(n)
    groups = list(groups.values())
    slabs16 = []
    for gi, names in enumerate(groups):
        slab = jnp.concatenate([wts[n][0] for n in names], axis=0)
        rows, width = slab.shape
        tile_c = _div_tile(rows, 256, 32)
        slabs16 += _rows(f"weights_to_bf16_{gi}", lambda i, a: (a,), rows, tile_c, [(slab, _row_spec(tile_c, width))],
                         [(width, BF16)])
    gathered = [_place_own_shard(f"weights_place_own_{gi}", g, own)
                for gi, (g, own) in enumerate(zip(_all_gather_shards(slabs16), slabs16))]
    full = {}
    for names, g in zip(groups, gathered):
        off = 0
        for n in names:
            r = wts[n].shape[1]
            part = g[:, off:off + r, :]
            full[n] = (part.reshape(N_SHARD * r, -1) if n in ROW_SHARDED
                       else jnp.concatenate([part[s] for s in range(N_SHARD)], axis=1))
            off += r
    w_rw = full["w_in"][:, :rw_cols]
    w_fox = jnp.pad(full["w_in"][:, rw_cols:rw_cols + fox_cols], ((0, 0), (0, LANES - fh)))
    w_gate = full["w_in"][:, rw_cols + fox_cols:]
    wup_pad = jnp.pad(full["rw_w_lora_up"], ((0, lr), (0, 0)))
    aup_pad = jnp.pad(full["rw_a_lora_up"], ((lr, 0), (0, 0)))
    b_pad = _pad_lanes(fox_b_f)
    r_k_row = rw_r_k.reshape(1, cw)
    gf_row = final_norm_g.reshape(1, d)

    tile = _tile(t, (256, 128, 64, 32, 16, 8))
    tile_s = _tile(t, (128, 64, 32, 16, 8))
    n_s = t // tile_s
    full2 = lambda a: (a, _full_spec(a.shape))

    (h,) = _rows("norm1", lambda i, a, g: (_rms(a, g),), t, tile, [(xs, _row_spec(tile, d)), full2(norm_g)], [(d, BF16)])
    z_rw = _matmul("proj_rw", h, w_rw)
    z_fox = _matmul("proj_fox", h, w_fox)
    z_gate = _matmul("proj_gate", h, w_gate)

    pre_consts = [full2(rw_shift_mu), full2(rw_w0), full2(rw_a0), full2(wup_pad), full2(aup_pad), full2(rw_k_k),
                  full2(rw_k_a)]

    def pre_fwd(i, z, prev8, *consts):
        return _rw_pre(z, _shifted(i, z, prev8), *consts, cw=cw)

    r_, w_, k_, v_, kk_, a_, g_ = _rows(
        "rwkv_pre", pre_fwd, t, tile_s,
        [(z_rw, _row_spec(tile_s, rw_cols)), (z_rw, _prev_rows_spec(tile_s, rw_cols))] + pre_consts, [(cw, F32)] * 7)
    y_scan, states, tinvs = _scan_fwd(r_, w_, k_, v_, kk_, a_)
    post_consts = [full2(rw_ln_g), full2(rw_ln_b), full2(r_k_row)]
    post_rows = lambda *arrs: [(a, _row_spec(tile_s, cw)) for a in arrs]
    (y_rw,) = _rows("rwkv_post", lambda i, *a: (_rw_post(*a, cw=cw),), t, tile_s,
                    post_rows(y_scan, r_, k_, v_, g_) + post_consts, [(cw, BF16)])

    c_fox = _fox_cumsum(z_fox, b_pad, fw=fw, fh=fh)
    tq = _fox_tile(t)
    n_pair_f = fw // LANES
    head_rows = lambda a: a.T.reshape(n_pair_f, 2, t // tq, tq).transpose(0, 2, 1, 3)
    head_cols = lambda a: a.transpose(0, 2, 1, 3).reshape(fh, t).T
    ct_fox = head_rows(c_fox[:, :fh])
    o_fox, lse_fox, y_fox = _fox_fwd(z_fox, c_fox, ct_fox, fw=fw)

    u_rw = _matmul("up_rwkv", y_rw, full["w_up_rwkv"])
    u_fox = _matmul("up_fox", y_fox, full["w_up_fox"])
    (merged,) = _rows("merge", lambda i, zg, a, b: (_merge(zg, a, b, d=d),), t, tile,
                      [(z_gate, _row_spec(tile, 2 * d)), (u_rw, _row_spec(tile, d)), (u_fox, _row_spec(tile, d))],
                      [(d, BF16)])
    x1 = _matmul("out_proj", merged, full["w_out"], add=xs)
    (n2,) = _rows("norm2", lambda i, a, g: (_rms(a, g),), t, tile, [(x1, _row_spec(tile, d)), full2(ple_norm_g)],
                  [(d, BF16)])
    gl = _matmul("ple_gate", n2, full["ple_gate_w"])
    ple = _matmul("ple_proj", ps, full["ple_proj"])

    def head_bwd(i, x1_t, ple_t, gl_t, gf, tg):
        loss, vjp = jax.vjp(lambda a, b, cc, g: _head_loss(a, b, cc, g, tg), x1_t, ple_t, gl_t, gf)
        dx1, dple, dgl, dgf = vjp(jnp.ones((1, 1), F32))
        return dx1, dple, dgl, jnp.broadcast_to(loss, (1, LANES)), dgf

    dx2, dple, dgl, loss_row, d_gf = _rows(
        "loss_head", head_bwd, t, tile_s,
        [(x1, _row_spec(tile_s, d)), (ple, _row_spec(tile_s, d)), (gl, _row_spec(tile_s, d)), full2(gf_row),
         (tgt, _row_spec(tile_s, d))],
        [(d, F32), (d, BF16), (d, BF16)], [(1, LANES), (1, d)])

    g_ple_proj = _matmul("d_ple_proj", ps, dple, ta=True)
    g_ple_gate = _matmul("d_ple_gate_w", n2, dgl, ta=True)
    dn2 = _matmul("d_n2", dgl, full["ple_gate_w"], tb=True)

    def norm_bwd(i, a, g, dh, res):
        _, vjp = jax.vjp(_rms, a, g)
        da, dg = vjp(dh)
        return res + da, dg

    dx1, d_g2 = _rows("norm2_bwd", norm_bwd, t, tile_s,
                      [(x1, _row_spec(tile_s, d)), full2(ple_norm_g), (dn2, _row_spec(tile_s, d)),
                       (dx2, _row_spec(tile_s, d))], [(d, F32)], [(1, d)])
    g_w_out = _matmul("d_w_out", merged, dx1, ta=True)
    dmerged = _matmul("d_merged", dx1, full["w_out"], tb=True)

    def merge_bwd(i, zg, a, b, dm):
        _, vjp = jax.vjp(functools.partial(_merge, d=d), zg, a, b)
        return vjp(dm)

    dz_gate, du_rw, du_fox = _rows(
        "merge_bwd", merge_bwd, t, tile_s,
        [(z_gate, _row_spec(tile_s, 2 * d)), (u_rw, _row_spec(tile_s, d)), (u_fox, _row_spec(tile_s, d)),
         (dmerged, _row_spec(tile_s, d))], [(2 * d, BF16), (d, BF16), (d, BF16)])
    g_up_rw = _matmul("d_w_up_rwkv", y_rw, du_rw, ta=True)
    g_up_fox = _matmul("d_w_up_fox", y_fox, du_fox, ta=True)
    dy_rw = _matmul("d_y_rwkv", du_rw, full["w_up_rwkv"], tb=True)
    dy_fox = _matmul("d_y_fox", du_fox, full["w_up_fox"], tb=True)

    def post_bwd(i, y, r, k, v, g, ln_g, ln_b, r_k, dy):
        _, vjp = jax.vjp(functools.partial(_rw_post, cw=cw), y, r, k, v, g, ln_g, ln_b, r_k)
        return vjp(dy)

    dys, dr1, dk1, dv1, dg1, d_ln_g, d_ln_b, d_r_k = _rows(
        "rwkv_post_bwd", post_bwd, t, tile_s,
        post_rows(y_scan, r_, k_, v_, g_) + post_consts + post_rows(dy_rw), [(cw, F32)] * 5, [(1, cw)] * 3)
    dr2, dw2, dk2, dv2, dkk2, da2 = _scan_bwd(r_, w_, k_, v_, kk_, a_, states, tinvs, dys)

    def pre_bwd(i, z, prev8, mu, w0, a0, wup, aup, k_k, k_a, dr_a, dr_b, dk_a, dk_b, dv_a, dv_b, dw, dkk, da, dg):
        zp = _shifted(i, z, prev8)
        _, vjp = jax.vjp(functools.partial(_rw_pre, cw=cw), z, zp, mu, w0, a0, wup, aup, k_k, k_a)
        dz, dzp, dmu, dw0, da0, dwup, daup, dk_k, dk_a = vjp((dr_a + dr_b, dw, dk_a + dk_b, dv_a + dv_b, dkk, da, dg))
        row = lax.broadcasted_iota(jnp.int32, dz.shape, 0)
        dz = dz + jnp.where(row < tile_s - 1, pltpu.roll(dzp, tile_s - 1, 0), 0.0)
        first = jnp.where(lax.broadcasted_iota(jnp.int32, (8, dz.shape[1]), 0) == 0, _row_of(dzp, 0), 0.0)
        return dz, first, dmu, dw0, da0, dwup, daup, dk_k, dk_a

    def pre_bwd_call():
        n_in = 2 + len(pre_consts) + 10
        ins = ([(z_rw, _row_spec(tile_s, rw_cols)), (z_rw, _prev_rows_spec(tile_s, rw_cols))] + pre_consts
               + post_rows(dr1, dr2, dk1, dk2, dv1, dv2, dw2, dkk2, da2, dg1))

        def body(*refs):
            i = pl.program_id(0)
            vals = pre_bwd(i, *[r[...] for r in refs[:n_in]])
            refs[n_in][...] = vals[0]
            refs[n_in + 1][...] = vals[1]
            for r, v in zip(refs[n_in + 2:], vals[2:]):
                @pl.when(i == 0)
                def _(r=r, v=v):
                    r[...] = v

                @pl.when(i > 0)
                def _(r=r, v=v):
                    r[...] += v

        acc_shapes = [(1, rw_cols), (1, cw), (1, cw), (LANES, cw), (LANES, cw), (1, cw), (1, cw)]
        return pl.pallas_call(
            body, name="rwkv_pre_bwd", grid=(n_s,), in_specs=[s for _, s in ins],
            out_specs=[_row_spec(tile_s, rw_cols), pl.BlockSpec((8, rw_cols), lambda i: (i, 0))]
            + [_full_spec(s) for s in acc_shapes],
            out_shape=[jax.ShapeDtypeStruct((t, rw_cols), F32), jax.ShapeDtypeStruct((8 * n_s, rw_cols), F32)]
            + [jax.ShapeDtypeStruct(s, F32) for s in acc_shapes],
            compiler_params=_params("arbitrary"),
        )(*[a for a, _ in ins])

    dz_main, dz_first, d_mu, d_w0, d_a0, d_wup, d_aup, d_k_k, d_k_a = pre_bwd_call()

    def add_next_row(i, dz, nxt8):
        row = lax.broadcasted_iota(jnp.int32, dz.shape, 0)
        carry = jnp.where(i < n_s - 1, _row_of(nxt8, 0), 0.0)
        return (dz + jnp.where(row == tile_s - 1, carry, 0.0),)

    (dz_rw,) = _rows("rwkv_shift_bwd", add_next_row, t, tile_s,
                     [(dz_main, _row_spec(tile_s, rw_cols)), (dz_first, _next_rows_spec(tile_s, rw_cols, n_s))],
                     [(rw_cols, BF16)])

    def fox_post_bwd(i, o, g, dy):
        _, vjp = jax.vjp(lambda oo, gg: oo * (gg * jax.nn.sigmoid(gg)), o, g)
        do, dg = vjp(dy)
        return do, _head_sum(do * o, _head_matrix(fw)), dg

    do_fox, dd_fox, dg_fox = _rows(
        "fox_post_bwd", fox_post_bwd, t, tile_s,
        [(o_fox, _row_spec(tile_s, fw)), (z_fox, _row_spec(tile_s, fw, 3)), (dy_fox, _row_spec(tile_s, fw))],
        [(fw, F32), (fw, F32), (fw, BF16)])
    dq_f, dk_f, dv_f, dcq, dck = _fox_bwd(z_fox, do_fox, c_fox, ct_fox, head_rows(lse_fox[:, ::HEAD]),
                                          head_rows(dd_fox[:, ::HEAD]), fw=fw)
    dc = head_cols(dcq) + dck.reshape(t, n_pair_f, LANES)[:, :, :2].reshape(t, fh)
    dfl, d_bf = _fox_cumsum_bwd(z_fox, b_pad, jnp.pad(dc, ((0, 0), (0, LANES - fh))), fw=fw, fh=fh)
    dz_fox = jnp.concatenate([dq_f, dk_f, dv_f, dg_fox, dfl], axis=1)

    g_w_rw = _matmul("d_w_in_rw", h, dz_rw, ta=True)
    g_w_fox = _matmul("d_w_in_fox", h, dz_fox, ta=True)
    g_w_gate = _matmul("d_w_in_gate", h, dz_gate, ta=True)

    g_full = {
        "w_in": jnp.concatenate([g_w_rw, g_w_fox[:, :fox_cols], g_w_gate], axis=1),
        "rw_w_lora_up": d_wup[:lr], "rw_a_lora_up": d_aup[lr:], "w_up_rwkv": g_up_rw, "w_up_fox": g_up_fox,
        "w_out": g_w_out, "ple_proj": g_ple_proj, "ple_gate_w": g_ple_gate,
    }
    def by_shard(n):
        g = g_full[n]
        if n in ROW_SHARDED:
            return g.reshape(N_SHARD, g.shape[0] // N_SHARD, g.shape[1])
        return jnp.stack(jnp.split(g, N_SHARD, axis=1))

    rs_state, token = _reduce_scatter_start([jnp.concatenate([by_shard(n) for n in names], axis=1) for names in groups])
    dz_rw, _ = lax.optimization_barrier((dz_rw, token))
    dh = _matmul("d_h_rw", dz_rw, w_rw, tb=True)
    dh = _matmul("d_h_fox", dz_fox, w_fox, tb=True, add=dh)
    dh = _matmul("d_h_gate", dz_gate, w_gate, tb=True, add=dh)
    grad_x, d_g1 = _rows("norm1_bwd", norm_bwd, t, tile_s,
                         [(xs, _row_spec(tile_s, d)), full2(norm_g), (dh, _row_spec(tile_s, d)),
                          (dx1, _row_spec(tile_s, d))], [(d, F32)], [(1, d)])
    reduced = _reduce_scatter_end(rs_state, d_g1)
    grads = {}
    for names, g in zip(groups, reduced):
        off = 0
        for n in names:
            r = wts[n].shape[1]
            grads[n] = g[off:off + r][None]
            off += r

    small_parts = dict(norm_g=d_g1, rw_shift_mu=d_mu, rw_w0=d_w0, rw_a0=d_a0, rw_k_k=d_k_k, rw_k_a=d_k_a, rw_r_k=d_r_k,
                       rw_ln_g=d_ln_g, rw_ln_b=d_ln_b, fox_b_f=d_bf[:, :fh], ple_norm_g=d_g2, final_norm_g=d_gf)
    n_small = sum(-(-wts[n].size // LANES) for n in SMALL)
    small_rows = -(-n_small // 8) * 8
    small_shapes = [wts[n].shape for n in SMALL]
    g_small = _all_reduce_small(_pack_small([small_parts[n] for n in SMALL], small_rows))
    for n, g in zip(SMALL, _unpack_small(g_small, small_shapes)):
        grads[n] = g

    delta, new_m, new_v = {}, {}, {}
    for n in BIG:
        delta[n], new_m[n], new_v[n] = _adamw("adamw_" + n, wts[n], grads[n], mom[n], vel[n])
    packed = lambda src: _pack_small([src[n] for n in SMALL], small_rows)[None]
    for store, out in zip((delta, new_m, new_v), _adamw("adamw_small", packed(wts), g_small[None], packed(mom), packed(vel))):
        for n, a in zip(SMALL, _unpack_small(out[0], small_shapes)):
            store[n] = a

    loss = lax.psum(loss_row[0, 0], ("x", "y", "c"))
    return (loss, grad_x[None], *[grads[n] for n in WEIGHTS], *[delta[n] for n in WEIGHTS],
            *[new_m[n] for n in WEIGHTS], *[new_v[n] for n in WEIGHTS])
```

```python
import functools

import jax
import jax.numpy as jnp
from jax import lax
from jax.experimental import pallas as pl
from jax.experimental.pallas import tpu as pltpu

F32 = jnp.float32
BF16 = jnp.bfloat16
HIGHEST = lax.Precision.HIGHEST
SCAN_PREC = lax.Precision.HIGH
MESH = pl.DeviceIdType.MESH

LANES = 128
HEAD = 64
NORM_EPS = 1e-6
GN_EPS = 64e-5
ADAM_LR = 0.001
ADAM_B1 = 0.9
ADAM_B2 = 0.999
ADAM_EPS = 1e-08
ADAM_WD = 0.01
ADAM_STEP = 10
N_SHARD = 4
VMEM_LIMIT = 56 * 1024 * 1024
PAIRS_PER_STEP = 4


def _params(*sem):
    return pltpu.CompilerParams(dimension_semantics=sem, vmem_limit_bytes=VMEM_LIMIT)


def _tile(n, cands):
    for c in cands:
        if c <= n and n % c == 0:
            return c
    return n


def _div_tile(n, cap, mult):
    return max(c for c in range(mult, min(n, cap) + 1, mult) if n % c == 0)


_ROW_TILES = (512, 256, 128, 64, 32, 16, 8)


def _dot(a, b, prec=None):
    return lax.dot_general(a, b, (((1,), (0,)), ((), ())), precision=prec, preferred_element_type=F32)


def _dot_nt(a, b, prec=None):
    return lax.dot_general(a, b, (((1,), (1,)), ((), ())), precision=prec, preferred_element_type=F32)


def _dot_tn(a, b, prec=None):
    return lax.dot_general(a, b, (((0,), (0,)), ((), ())), precision=prec, preferred_element_type=F32)


@jax.custom_vjp
def _bdot(x, w):
    return _dot(x.astype(BF16), w.astype(BF16))


def _bdot_fwd(x, w):
    return _bdot(x, w), (x, w)


def _bdot_bwd(res, ct):
    x, w = res
    return _dot_nt(ct.astype(BF16), w.astype(BF16)), _dot_tn(x.astype(BF16), ct.astype(BF16))


_bdot.defvjp(_bdot_fwd, _bdot_bwd)


def _head_matrix(width):
    c = lax.broadcasted_iota(jnp.int32, (width, LANES), 0)
    h = lax.broadcasted_iota(jnp.int32, (width, LANES), 1)
    return (c // HEAD == h).astype(F32)


def _head_sum(x, e):
    return _dot_nt(_dot(x, e, SCAN_PREC), e, SCAN_PREC)


def _softplus(x):
    return jnp.maximum(x, 0.0) + jnp.log1p(jnp.exp(-jnp.abs(x)))


def _lane_pick(x, idx):
    lane = lax.broadcasted_iota(jnp.int32, x.shape, 1)
    return jnp.sum(jnp.where(lane == idx, x, 0.0), axis=1, keepdims=True)


def _matmul(name, a, b, *, ta=False, tb=False, add=None, out_dtype=F32, after=None):
    m, k = (a.shape[1], a.shape[0]) if ta else a.shape
    n = b.shape[0] if tb else b.shape[1]
    tm = _tile(m, (1024, 512, 256, 128))
    tn = _tile(n, (1408, 1024, 768, 640, 512, 384, 256, 128))
    tk = _tile(k, (1408, 1024, 768, 640, 512, 384, 256, 128, 64, 32, 16))
    nk = k // tk
    dims = (((0 if ta else 1,), (1 if tb else 0,)), ((), ()))

    def body(*refs):
        a_ref, b_ref = refs[0], refs[1]
        o_ref, acc_ref = refs[-2], refs[-1]
        kk = pl.program_id(2)

        @pl.when(kk == 0)
        def _():
            acc_ref[...] = jnp.zeros_like(acc_ref)

        acc_ref[...] += lax.dot_general(a_ref[...].astype(BF16), b_ref[...].astype(BF16), dims,
                                        preferred_element_type=F32)

        @pl.when(kk == nk - 1)
        def _():
            r = acc_ref[...]
            if add is not None:
                r = r + refs[2][...].astype(F32)
            o_ref[...] = r.astype(o_ref.dtype)

    a_spec = pl.BlockSpec((tk, tm), lambda i, j, kk: (kk, i)) if ta else pl.BlockSpec((tm, tk), lambda i, j, kk: (i, kk))
    b_spec = pl.BlockSpec((tn, tk), lambda i, j, kk: (j, kk)) if tb else pl.BlockSpec((tk, tn), lambda i, j, kk: (kk, j))
    o_spec = pl.BlockSpec((tm, tn), lambda i, j, kk: (i, j))
    ins, specs = [a, b], [a_spec, b_spec]
    if add is not None:
        ins.append(add)
        specs.append(o_spec)
    if after is not None:
        ins.append(after)
        specs.append(pl.BlockSpec(after.shape, lambda i, j, kk: (0,) * after.ndim))
    return pl.pallas_call(
        body, name=name, grid=(m // tm, n // tn, nk), in_specs=specs, out_specs=o_spec,
        out_shape=jax.ShapeDtypeStruct((m, n), out_dtype),
        scratch_shapes=[pltpu.VMEM((tm, tn), F32)],
        compiler_params=_params("parallel", "parallel", "arbitrary"),
    )(*ins)


def _rows(name, fn, n_rows, tile, ins, outs, accs=()):
    n_in, n_out = len(ins), len(outs)

    def body(*refs):
        i = pl.program_id(0)
        vals = fn(i, *[r[...] for r in refs[:n_in]])
        for r, v in zip(refs[n_in:n_in + n_out], vals[:n_out]):
            r[...] = v.astype(r.dtype)
        for r, v in zip(refs[n_in + n_out:], vals[n_out:]):
            @pl.when(i == 0)
            def _(r=r, v=v):
                r[...] = v

            @pl.when(i > 0)
            def _(r=r, v=v):
                r[...] += v

    out_specs = [pl.BlockSpec((tile, w), lambda i: (i, 0)) for w, _ in outs]
    out_specs += [pl.BlockSpec(s, lambda i: (0, 0)) for s in accs]
    out_shape = [jax.ShapeDtypeStruct((n_rows, w), d) for w, d in outs]
    out_shape += [jax.ShapeDtypeStruct(s, F32) for s in accs]
    return pl.pallas_call(
        body, name=name, grid=(n_rows // tile,), in_specs=[s for _, s in ins], out_specs=out_specs,
        out_shape=out_shape, compiler_params=_params("arbitrary"),
    )(*[a for a, _ in ins])


def _row_spec(tile, width, col=0):
    return pl.BlockSpec((tile, width), lambda i: (i, col))


def _full_spec(shape):
    return pl.BlockSpec(shape, lambda i: (0,) * len(shape))


def _prev_rows_spec(tile, width):
    return pl.BlockSpec((8, width), lambda i: (jnp.maximum(i * (tile // 8) - 1, 0), 0))


def _next_rows_spec(tile, width, n_tiles):
    return pl.BlockSpec((8, width), lambda i: (jnp.minimum(i + 1, n_tiles - 1), 0))


def _row_of(x8, idx):
    r = lax.broadcasted_iota(jnp.int32, x8.shape, 0)
    return jnp.sum(jnp.where(r == idx, x8, 0.0), axis=0, keepdims=True)


def _rms(x, g):
    return x * lax.rsqrt(jnp.mean(x * x, axis=-1, keepdims=True) + NORM_EPS) * g


def _shifted(i, z, prev8):
    first = jnp.where(i > 0, _row_of(prev8, 7), 0.0)
    row = lax.broadcasted_iota(jnp.int32, z.shape, 0)
    return jnp.where(row == 0, first, pltpu.roll(z, 1, 0))


def _rw_pre(z, zp, mu, w0, a0, wup, aup, k_k, k_a, *, cw):
    zs = z + (zp - z) * mu
    r, k, v, g = (zs[:, j * cw:(j + 1) * cw] for j in range(4))
    lo = zs[:, 4 * cw:4 * cw + LANES]
    w_raw = w0 + _bdot(jnp.tanh(lo), wup)
    decay = jnp.exp(-jnp.exp(-_softplus(-w_raw) - 0.5))
    a = jax.nn.sigmoid(a0 + _bdot(lo, aup))
    e = _head_matrix(cw)
    kk = k * k_k
    kk = kk / jnp.maximum(jnp.sqrt(_head_sum(kk * kk, e)), 1e-12)
    k_mod = k * (1.0 + (a - 1.0) * k_a)
    return r, decay, k_mod, v, kk, a, g


def _rw_post(y, r, k_mod, v, g, ln_g, ln_b, r_k, *, cw):
    e = _head_matrix(cw)
    mu = _head_sum(y, e) * (1.0 / HEAD)
    d = y - mu
    var = _head_sum(d * d, e) * (1.0 / HEAD)
    yn = d * lax.rsqrt(var + GN_EPS) * ln_g + ln_b
    bonus = _head_sum(r * k_mod * r_k, e) * v
    return (yn + bonus) * (g * jax.nn.sigmoid(g))


def _merge(zg, u_rw, u_fox, *, d):
    return jax.nn.sigmoid(zg[:, :d]) * u_rw + jax.nn.sigmoid(zg[:, d:]) * u_fox


def _head_loss(x1, ple, gl, gf, tgt):
    x2 = x1 + ple * jax.nn.sigmoid(gl)
    err = _rms(x2, gf) - tgt
    return 0.5 * jnp.sum(jnp.mean(err * err, axis=-1, keepdims=True), axis=0, keepdims=True)


def _eliminate(lo):
    n, c, _ = lo.shape
    ri = lax.broadcasted_iota(jnp.int32, (n, c, c), 1)
    ci = lax.broadcasted_iota(jnp.int32, (n, c, c), 2)
    x = (ri == ci).astype(F32)
    for s in range(c - 1):
        col = jnp.sum(jnp.where(ci == s, lo, 0.0), axis=2, keepdims=True)
        row = jnp.sum(jnp.where(ri == s, x, 0.0), axis=1, keepdims=True)
        x = x - col * row
    return x


def _batched(a, b, ca, cb):
    return lax.dot_general(a, b, (((ca,), (cb,)), ((0,), (0,))), precision=SCAN_PREC, preferred_element_type=F32)


@jax.custom_vjp
def _unit_lower_inverse(lo, known):
    return _eliminate(lo) if known is None else known


def _uli_fwd(lo, known):
    x = _unit_lower_inverse(lo, known)
    return x, (x, known)


def _uli_bwd(res, dx):
    x, known = res
    dlo = -_batched(_batched(x, dx, 1, 1), x, 2, 2)
    return dlo, (None if known is None else jnp.zeros_like(known))


_unit_lower_inverse.defvjp(_uli_fwd, _uli_bwd)


def _rwkv_chunk(s0, r, w, k, v, kk, a, *, c, tinv_known=None):
    pairs = range(len(s0))
    lane = lax.broadcasted_iota(jnp.int32, (1, LANES), 1)
    heads = (lane < HEAD, lane >= HEAD)
    ti = lax.broadcasted_iota(jnp.int32, (c, c), 0)
    si = lax.broadcasted_iota(jnp.int32, (c, c), 1)
    incl = si <= ti
    strict = si < ti
    tri = incl.astype(F32)
    logw = [jnp.log(w[p]) for p in pairs]
    cum = [_dot(tri, logw[p], HIGHEST) for p in pairs]
    cum_end = [jnp.sum(logw[p], axis=0, keepdims=True) for p in pairs]
    g_inv = [jnp.exp(-cum[p]) for p in pairs]
    to_end = [jnp.exp(cum_end[p] - cum[p]) for p in pairs]
    b = [kk[p] * a[p] for p in pairs]
    beta = [b[p] * g_inv[p] for p in pairs]
    kap = [kk[p] * jnp.exp(cum[p] - logw[p]) for p in pairs]
    kt = [k[p] * g_inv[p] for p in pairs]
    rt = [r[p] * jnp.exp(cum[p]) for p in pairs]
    lhs = [jnp.concatenate([jnp.where(m, x[p], 0.0) for x in (kap, rt) for m in heads], axis=0) for p in pairs]
    vs_beta = [_dot_nt(lhs[p], beta[p], None) for p in pairs]
    vs_kt = [_dot_nt(lhs[p], kt[p], None) for p in pairs]
    strict2 = jnp.concatenate([strict, strict], axis=0)
    incl2 = jnp.concatenate([incl, incl], axis=0)
    lo = [jnp.where(strict2, vs_beta[p][:2 * c], 0.0) for p in pairs]
    mm = [jnp.where(strict2, vs_kt[p][:2 * c], 0.0) for p in pairs]
    arb = [jnp.where(incl2, vs_beta[p][2 * c:], 0.0) for p in pairs]
    ark = [jnp.where(incl2, vs_kt[p][2 * c:], 0.0) for p in pairs]
    per_head = lambda xs: jnp.concatenate([xs[p][h * c:(h + 1) * c][None] for p in pairs for h in (0, 1)])
    tinv = _unit_lower_inverse(per_head(lo), None if tinv_known is None else per_head(tinv_known))
    tinv = [jnp.concatenate([tinv[2 * p], tinv[2 * p + 1]], axis=0) for p in pairs]
    both = lambda x: jnp.where(heads[0], x[:c], x[c:])
    vs_s = [_dot_nt(jnp.concatenate([kap[p], rt[p]], axis=0), s0[p], None) for p in pairs]
    rhs = [vs_s[p][:c] + both(_dot(mm[p], v[p], None)) for p in pairs]
    u = [-both(_dot(tinv[p], rhs[p], None)) for p in pairs]
    y = [vs_s[p][c:] + both(_dot(arb[p], u[p], None) + _dot(ark[p], v[p], None)) for p in pairs]
    rr = lax.broadcasted_iota(jnp.int32, (LANES, LANES), 0) < HEAD
    cc = lax.broadcasted_iota(jnp.int32, (LANES, LANES), 1) < HEAD
    ds = [_dot_tn(jnp.concatenate([u[p], v[p]], axis=0),
                  jnp.concatenate([b[p] * to_end[p], k[p] * to_end[p]], axis=0), None) for p in pairs]
    s1 = [s0[p] * jnp.exp(cum_end[p]) + jnp.where(rr == cc, ds[p], 0.0) for p in pairs]
    return tuple(y), tuple(s1), tuple(tinv)


def _scan_tiles(t, n_pair):
    return _tile(t, (32, 16, 8)), _tile(t, (512, 256, 128, 64, 32)), _tile(n_pair, (PAIRS_PER_STEP, 2, 1))


def _scan_fwd(r, w, k, v, kk, a):
    t, width = r.shape
    c, tb, npb = _scan_tiles(t, width // LANES)
    n_grp, n_blk, n_cb = width // (LANES * npb), t // tb, tb // c

    def body(r_ref, w_ref, k_ref, v_ref, kk_ref, a_ref, y_ref, st_ref, ti_ref, s_scr):
        @pl.when(pl.program_id(1) == 0)
        def _():
            s_scr[...] = jnp.zeros_like(s_scr)

        def chunk(j, carry):
            sl = pl.ds(pl.multiple_of(j * c, c), c)
            lanes = [pl.ds(q * LANES, LANES) for q in range(npb)]
            s0 = tuple(s_scr[q] for q in range(npb))
            cols = lambda ref: tuple(ref[sl, ln] for ln in lanes)
            y, s1, tinv = _rwkv_chunk(s0, cols(r_ref), cols(w_ref), cols(k_ref), cols(v_ref), cols(kk_ref), cols(a_ref),
                                      c=c)
            for q, ln in enumerate(lanes):
                st_ref[q, j] = s0[q]
                ti_ref[q, j] = tinv[q]
                y_ref[sl, ln] = y[q]
                s_scr[q] = s1[q]
            return carry

        lax.fori_loop(0, n_cb, chunk, 0)

    blk = pl.BlockSpec((tb, npb * LANES), lambda p, i: (i, p))
    return pl.pallas_call(
        body, name="rwkv_scan_fwd", grid=(n_grp, n_blk), in_specs=[blk] * 6,
        out_specs=[blk, pl.BlockSpec((npb, n_cb, LANES, LANES), lambda p, i: (p, i, 0, 0)),
                   pl.BlockSpec((npb, n_cb, 2 * c, c), lambda p, i: (p, i, 0, 0))],
        out_shape=[jax.ShapeDtypeStruct((t, width), F32),
                   jax.ShapeDtypeStruct((width // LANES, t // c, LANES, LANES), F32),
                   jax.ShapeDtypeStruct((width // LANES, t // c, 2 * c, c), F32)],
        scratch_shapes=[pltpu.VMEM((npb, LANES, LANES), F32)],
        compiler_params=_params("arbitrary", "arbitrary"),
    )(r, w, k, v, kk, a)


def _scan_bwd(r, w, k, v, kk, a, st, ti, dy):
    t, width = r.shape
    c, tb, npb = _scan_tiles(t, width // LANES)
    n_grp, n_blk, n_cb = width // (LANES * npb), t // tb, tb // c

    def body(r_ref, w_ref, k_ref, v_ref, kk_ref, a_ref, st_ref, ti_ref, dy_ref,
             dr_ref, dw_ref, dk_ref, dv_ref, dkk_ref, da_ref, ds_scr):
        @pl.when(pl.program_id(1) == 0)
        def _():
            ds_scr[...] = jnp.zeros_like(ds_scr)

        def chunk(jj, carry):
            j = n_cb - 1 - jj
            sl = pl.ds(pl.multiple_of(j * c, c), c)
            lanes = [pl.ds(q * LANES, LANES) for q in range(npb)]
            cols = lambda ref: tuple(ref[sl, ln] for ln in lanes)
            args = (tuple(st_ref[q, j] for q in range(npb)), cols(r_ref), cols(w_ref), cols(k_ref), cols(v_ref),
                    cols(kk_ref), cols(a_ref))
            known = tuple(ti_ref[q, j] for q in range(npb))
            _, vjp = jax.vjp(lambda *xs: _rwkv_chunk(*xs, c=c, tinv_known=known)[:2], *args)
            grads = vjp((cols(dy_ref), tuple(ds_scr[q] for q in range(npb))))
            for q, ln in enumerate(lanes):
                ds_scr[q] = grads[0][q]
                for ref, g in zip((dr_ref, dw_ref, dk_ref, dv_ref, dkk_ref, da_ref), grads[1:]):
                    ref[sl, ln] = g[q]
            return carry

        lax.fori_loop(0, n_cb, chunk, 0)

    blk = pl.BlockSpec((tb, npb * LANES), lambda p, i: (n_blk - 1 - i, p))
    stb = pl.BlockSpec((npb, n_cb, LANES, LANES), lambda p, i: (p, n_blk - 1 - i, 0, 0))
    tib = pl.BlockSpec((npb, n_cb, 2 * c, c), lambda p, i: (p, n_blk - 1 - i, 0, 0))
    return pl.pallas_call(
        body, name="rwkv_scan_bwd", grid=(n_grp, n_blk), in_specs=[blk] * 6 + [stb, tib, blk], out_specs=[blk] * 6,
        out_shape=[jax.ShapeDtypeStruct((t, width), F32)] * 6,
        scratch_shapes=[pltpu.VMEM((npb, LANES, LANES), F32)],
        compiler_params=_params("arbitrary", "arbitrary"),
    )(r, w, k, v, kk, a, st, ti, dy)


NEG = -1e30


def _fox_cumsum(zf, b_pad, *, fw, fh):
    t = zf.shape[0]
    tile = _tile(t, (256, 128, 64, 32, 16, 8))

    def body(fl_ref, b_ref, c_ref, carry):
        @pl.when(pl.program_id(0) == 0)
        def _():
            carry[...] = jnp.zeros_like(carry)

        lane = lax.broadcasted_iota(jnp.int32, (tile, LANES), 1)
        logf = jnp.where(lane < fh, -_softplus(-(fl_ref[...] + b_ref[...])), 0.0)
        ri = lax.broadcasted_iota(jnp.int32, (tile, tile), 0)
        ci = lax.broadcasted_iota(jnp.int32, (tile, tile), 1)
        c_ref[...] = carry[...] + _dot((ci <= ri).astype(F32), logf, HIGHEST)
        carry[...] += jnp.sum(logf, axis=0, keepdims=True)

    return pl.pallas_call(
        body, name="fox_cumsum", grid=(t // tile,),
        in_specs=[_row_spec(tile, LANES, 4 * fw // LANES), _full_spec((1, LANES))],
        out_specs=_row_spec(tile, LANES), out_shape=jax.ShapeDtypeStruct((t, LANES), F32),
        scratch_shapes=[pltpu.VMEM((1, LANES), F32)], compiler_params=_params("arbitrary"),
    )(zf, b_pad)


def _fox_cumsum_bwd(zf, b_pad, dc, *, fw, fh):
    t = zf.shape[0]
    tile = _tile(t, (256, 128, 64, 32, 16, 8))
    n = t // tile

    def body(fl_ref, b_ref, dc_ref, dfl_ref, db_ref, carry):
        i = pl.program_id(0)

        @pl.when(i == 0)
        def _():
            carry[...] = jnp.zeros_like(carry)
            db_ref[...] = jnp.zeros_like(db_ref)

        lane = lax.broadcasted_iota(jnp.int32, (tile, LANES), 1)
        dc_t = dc_ref[...]
        ri = lax.broadcasted_iota(jnp.int32, (tile, tile), 0)
        ci = lax.broadcasted_iota(jnp.int32, (tile, tile), 1)
        dlogf = carry[...] + _dot((ci >= ri).astype(F32), dc_t, HIGHEST)
        carry[...] += jnp.sum(dc_t, axis=0, keepdims=True)
        dfl = jnp.where(lane < fh, dlogf * jax.nn.sigmoid(-(fl_ref[...] + b_ref[...])), 0.0)
        dfl_ref[...] = dfl.astype(dfl_ref.dtype)
        db_ref[...] += jnp.sum(dfl, axis=0, keepdims=True)

    rev = lambda col: pl.BlockSpec((tile, LANES), lambda i: (n - 1 - i, col))
    return pl.pallas_call(
        body, name="fox_cumsum_bwd", grid=(n,),
        in_specs=[rev(4 * fw // LANES), _full_spec((1, LANES)), rev(0)],
        out_specs=[rev(0), _full_spec((1, LANES))],
        out_shape=[jax.ShapeDtypeStruct((t, LANES), BF16), jax.ShapeDtypeStruct((1, LANES), F32)],
        scratch_shapes=[pltpu.VMEM((1, LANES), F32)], compiler_params=_params("arbitrary"),
    )(zf, b_pad, dc)


def _fox_tile(t):
    return _tile(t, (512, 256, 128))


def _fox_fwd(zf, c, ct, *, fw):
    t = zf.shape[0]
    tq = _fox_tile(t)
    th = tq // 2
    n_pair, n_q = fw // LANES, t // tq
    scale = HEAD ** -0.5
    chains = [(h, qq) for h in (0, 1) for qq in (0, 1)]

    def body(q_ref, k_ref, v_ref, g_ref, c_ref, ct_ref, o_ref, lse_ref, y_ref):
        hp, i = pl.program_id(0), pl.program_id(1)
        lane = lax.broadcasted_iota(jnp.int32, (1, LANES), 1)
        in_head = (lane < HEAD, lane >= HEAD)
        rows = [pl.ds(qq * th, th) for qq in (0, 1)]
        qh = [jnp.where(in_head[h], q_ref[rows[qq], :] * scale, 0.0).astype(BF16) for h, qq in chains]
        cq = [_lane_pick(c_ref[rows[qq], :], 2 * hp + h) for h, qq in chains]
        qidx = lax.broadcasted_iota(jnp.int32, (th, tq), 0)
        kidx = lax.broadcasted_iota(jnp.int32, (th, tq), 1)

        def kv_step(j, carry, diagonal):
            m, l, acc = carry
            ks = pl.ds(pl.multiple_of(j * tq, tq), tq)
            kb = k_ref[ks, :].astype(BF16)
            vb = v_ref[ks, :]
            vh = [jnp.where(in_head[h], vb, 0.0).astype(BF16) for h in (0, 1)]
            ck = [ct_ref[0, j, pl.ds(h, 1), :] for h in (0, 1)]
            s = [_dot_nt(qh[n], kb) + cq[n] - ck[h] for n, (h, qq) in enumerate(chains)]
            if diagonal:
                s = [jnp.where(qq * th + qidx >= kidx, s[n], NEG) for n, (h, qq) in enumerate(chains)]
            m_new = [jnp.maximum(m[n], jnp.max(s[n], axis=1, keepdims=True)) for n in range(4)]
            p = [jnp.exp(s[n] - m_new[n]) for n in range(4)]
            alpha = [jnp.exp(m[n] - m_new[n]) for n in range(4)]
            l = [l[n] * alpha[n] + jnp.sum(p[n], axis=1, keepdims=True) for n in range(4)]
            pv = [_dot(p[n].astype(BF16), vh[h]) for n, (h, qq) in enumerate(chains)]
            acc = [acc[qq] * jnp.where(in_head[0], alpha[qq], alpha[2 + qq]) + pv[qq] + pv[2 + qq] for qq in (0, 1)]
            return tuple(m_new), tuple(l), tuple(acc)

        init = (tuple(jnp.full((th, 1), NEG, F32) for _ in chains), tuple(jnp.zeros((th, 1), F32) for _ in chains),
                tuple(jnp.zeros((th, LANES), F32) for _ in (0, 1)))
        carry = lax.fori_loop(0, i, functools.partial(kv_step, diagonal=False), init)
        m, l, acc = kv_step(i, carry, True)
        for qq in (0, 1):
            o = acc[qq] / jnp.where(in_head[0], l[qq], l[2 + qq])
            g = g_ref[rows[qq], :]
            o_ref[rows[qq], :] = o
            lse_ref[rows[qq], :] = jnp.where(in_head[0], m[qq] + jnp.log(l[qq]), m[2 + qq] + jnp.log(l[2 + qq]))
            y_ref[rows[qq], :] = (o * (g * jax.nn.sigmoid(g))).astype(y_ref.dtype)

    npw = fw // LANES
    blk = lambda col0: pl.BlockSpec((tq, LANES), lambda hp, i: (i, col0 + hp))
    res = lambda col0: pl.BlockSpec((t, LANES), lambda hp, i: (0, col0 + hp))
    out_blk = pl.BlockSpec((tq, LANES), lambda hp, i: (i, hp))
    return pl.pallas_call(
        body, name="fox_attn_fwd", grid=(n_pair, n_q),
        in_specs=[blk(0), res(npw), res(2 * npw), blk(3 * npw),
                  pl.BlockSpec((tq, LANES), lambda hp, i: (i, 0)),
                  pl.BlockSpec((1, n_q, 2, tq), lambda hp, i: (hp, 0, 0, 0))],
        out_specs=[out_blk, out_blk, out_blk],
        out_shape=[jax.ShapeDtypeStruct((t, fw), F32), jax.ShapeDtypeStruct((t, fw), F32),
                   jax.ShapeDtypeStruct((t, fw), BF16)],
        compiler_params=_params("arbitrary", "arbitrary"),
    )(zf, zf, zf, zf, c, ct)


def _fox_bwd(zf, do, c, ct, lse_r, dd_r, *, fw):
    t = zf.shape[0]
    tq = _fox_tile(t)
    n_pair, n_q = fw // LANES, t // tq
    scale = HEAD ** -0.5

    def body(q_ref, k_ref, v_ref, do_ref, c_ref, ct_ref, lse_ref, dd_ref,
             dq_ref, dk_ref, dv_ref, dcq_ref, dck_ref, dq_acc, dcq_acc):
        hp, j = pl.program_id(0), pl.program_id(1)

        @pl.when(j == 0)
        def _():
            dq_acc[...] = jnp.zeros_like(dq_acc)
            dcq_acc[...] = jnp.zeros_like(dcq_acc)

        lane = lax.broadcasted_iota(jnp.int32, (1, LANES), 1)
        in_head = (lane < HEAD, lane >= HEAD)
        kb = k_ref[...]
        kh = [jnp.where(m, kb, 0.0).astype(BF16) for m in in_head]
        vb = v_ref[...].astype(BF16)
        c_k = c_ref[...]
        ck = [_lane_pick(c_k, 2 * hp + h) for h in (0, 1)]
        kidx = lax.broadcasted_iota(jnp.int32, (tq, tq), 0)
        qidx = lax.broadcasted_iota(jnp.int32, (tq, tq), 1)

        def q_step(i, carry, diagonal):
            dk, dv, dck = carry
            qs = pl.ds(pl.multiple_of(i * tq, tq), tq)
            qf = q_ref[qs, :] * scale
            dof = do_ref[qs, :]
            qh = [jnp.where(m, qf, 0.0).astype(BF16) for m in in_head]
            doh = [jnp.where(m, dof, 0.0).astype(BF16) for m in in_head]
            row = lambda ref, h: ref[0, i, pl.ds(h, 1), :]
            st = [_dot_nt(kh[h], qh[h]) + row(ct_ref, h) - ck[h] for h in (0, 1)]
            p = [jnp.exp(st[h] - row(lse_ref, h)) for h in (0, 1)]
            if diagonal:
                p = [jnp.where(kidx <= qidx, p[h], 0.0) for h in (0, 1)]
            dst = [p[h] * (_dot_nt(vb, doh[h]) - row(dd_ref, h)) for h in (0, 1)]
            p16 = [x.astype(BF16) for x in p]
            ds16 = [x.astype(BF16) for x in dst]
            dv = dv + _dot(p16[0], doh[0]) + _dot(p16[1], doh[1])
            dk = dk + _dot(ds16[0], qh[0]) + _dot(ds16[1], qh[1])
            dq_acc[qs, :] += _dot_tn(ds16[0], kh[0]) + _dot_tn(ds16[1], kh[1])
            for h in (0, 1):
                dcq_acc[i, pl.ds(h, 1), :] += jnp.sum(dst[h], axis=0, keepdims=True)
            dck = tuple(dck[h] - jnp.sum(dst[h], axis=1, keepdims=True) for h in (0, 1))
            return dk, dv, dck

        zero = jnp.zeros((tq, LANES), F32)
        carry = q_step(j, (zero, zero, (jnp.zeros((tq, 1), F32),) * 2), True)
        dk, dv, dck = lax.fori_loop(j + 1, n_q, functools.partial(q_step, diagonal=False), carry)
        dk_ref[...] = dk.astype(dk_ref.dtype)
        dv_ref[...] = dv.astype(dv_ref.dtype)
        dck_ref[...] = jnp.where(lane == 0, dck[0], jnp.where(lane == 1, dck[1], 0.0))

        @pl.when(j == n_q - 1)
        def _():
            dq_ref[...] = (dq_acc[...] * scale).astype(dq_ref.dtype)
            dcq_ref[0] = dcq_acc[...]

    npw = fw // LANES
    res_z = lambda col0: pl.BlockSpec((t, LANES), lambda hp, j: (0, col0 + hp))
    blk_z = lambda col0: pl.BlockSpec((tq, LANES), lambda hp, j: (j, col0 + hp))
    res = pl.BlockSpec((t, LANES), lambda hp, j: (0, hp))
    blk = pl.BlockSpec((tq, LANES), lambda hp, j: (j, hp))
    rows = pl.BlockSpec((1, n_q, 2, tq), lambda hp, j: (hp, 0, 0, 0))
    return pl.pallas_call(
        body, name="fox_attn_bwd", grid=(n_pair, n_q),
        in_specs=[res_z(0), blk_z(npw), blk_z(2 * npw), res, pl.BlockSpec((tq, LANES), lambda hp, j: (j, 0)),
                  rows, rows, rows],
        out_specs=[res, blk, blk, rows, blk],
        out_shape=[jax.ShapeDtypeStruct((t, fw), BF16), jax.ShapeDtypeStruct((t, fw), BF16),
                   jax.ShapeDtypeStruct((t, fw), BF16), jax.ShapeDtypeStruct((n_pair, n_q, 2, tq), F32),
                   jax.ShapeDtypeStruct((t, fw), F32)],
        scratch_shapes=[pltpu.VMEM((t, LANES), F32), pltpu.VMEM((n_q, 2, tq), F32)],
        compiler_params=_params("arbitrary", "arbitrary"),
    )(zf, zf, zf, do, c, ct, lse_r, dd_r)


def _adamw_math(w, g, m, v):
    m = ADAM_B1 * m + (1.0 - ADAM_B1) * g
    v = ADAM_B2 * v + (1.0 - ADAM_B2) * jnp.square(g)
    m_hat = m / (1.0 - ADAM_B1 ** ADAM_STEP)
    v_hat = v / (1.0 - ADAM_B2 ** ADAM_STEP)
    delta = -ADAM_LR * (m_hat / (jnp.sqrt(v_hat) + ADAM_EPS) + ADAM_WD * w)
    return delta, m, v


def _adamw(name, w, g, m, v):
    lead, rows, cols = w.shape
    if lead == 1 and cols % LANES:
        outs = _adamw(name, *[jnp.transpose(a, (2, 0, 1)) for a in (w, g, m, v)])
        return [jnp.transpose(o, (1, 2, 0)) for o in outs]
    if lead == 1:
        tile = _tile(rows, (128, 64, 32, 16, 8))
        spec, steps = pl.BlockSpec((1, tile, cols), lambda i: (0, i, 0)), rows // tile
    else:
        tile = _div_tile(lead, 256, 1)
        spec, steps = pl.BlockSpec((tile, rows, cols), lambda i: (i, 0, 0)), lead // tile

    def body(w_ref, g_ref, m_ref, v_ref, d_ref, mo_ref, vo_ref):
        d_ref[...], mo_ref[...], vo_ref[...] = _adamw_math(w_ref[...], g_ref[...], m_ref[...], v_ref[...])

    return pl.pallas_call(
        body, name=name, grid=(steps,), in_specs=[spec] * 4, out_specs=[spec] * 3,
        out_shape=[jax.ShapeDtypeStruct(w.shape, F32)] * 3, compiler_params=_params("parallel"),
    )(w, g, m, v)


def _place():
    return lax.axis_index("x"), lax.axis_index("y"), lax.axis_index("c")


def _other_chips(x, y):
    return [(1 - x, y), (x, 1 - y), (1 - x, 1 - y)]


HBM_SPEC = pl.BlockSpec(memory_space=pltpu.HBM)


def _all_gather_shards(slabs):
    n = len(slabs)

    def body(*refs):
        src_refs, out_refs, send_sems, recv_sems = refs[:n], refs[n:2 * n], refs[2 * n], refs[2 * n + 1]
        x, y, c = _place()
        me = 2 * x + y
        sibling = (x, y, 1 - c)
        chips = _other_chips(x, y)
        first, passed, waits = [], [], []
        for g, (src_ref, out_ref) in enumerate(zip(src_refs, out_refs)):
            rh = src_ref.shape[0] // 2

            def part(chip, half, out_ref=out_ref, rh=rh):
                return out_ref.at[chip, pl.ds(half * rh, rh), :]

            def copy(k, src, dst, to, g=g):
                return pltpu.make_async_remote_copy(src_ref=src, dst_ref=dst, send_sem=send_sems.at[6 * g + k],
                                                    recv_sem=recv_sems.at[6 * g + k], device_id=to, device_id_type=MESH)

            first += [copy(j, src_ref.at[pl.ds(c * rh, rh), :], part(me, c), (px, py, c))
                      for j, (px, py) in enumerate(chips)]
            for j, (px, py) in enumerate(chips):
                theirs = part(2 * px + py, c)
                passed.append((copy(j, theirs, theirs, sibling), copy(3 + j, theirs, theirs, sibling)))
                other = part(2 * px + py, 1 - c)
                waits.append(copy(3 + j, other, other, sibling))
        for cp in first:
            cp.start()
        for landed, forward in passed:
            landed.wait_recv()
            forward.start()
        for cp in waits:
            cp.wait_recv()
        for cp in first + [fwd for _, fwd in passed]:
            cp.wait_send()

    return pl.pallas_call(
        body, name="weights_all_gather", in_specs=[HBM_SPEC] * n, out_specs=[HBM_SPEC] * n,
        out_shape=[jax.ShapeDtypeStruct((N_SHARD,) + a.shape, a.dtype) for a in slabs],
        scratch_shapes=[pltpu.SemaphoreType.DMA((6 * n,)), pltpu.SemaphoreType.DMA((6 * n,))],
    )(*slabs)


def _chip_index():
    return jnp.reshape(2 * lax.axis_index("x") + lax.axis_index("y"), (1,)).astype(jnp.int32)


def _place_own_shard(name, gathered, slab):
    rows, width = slab.shape
    tile = _div_tile(rows, 256, 16)

    def body(me_ref, s_ref, g_ref, o_ref):
        o_ref[0] = s_ref[...]

    return pl.pallas_call(
        body, name=name,
        grid_spec=pltpu.PrefetchScalarGridSpec(
            num_scalar_prefetch=1, grid=(rows // tile,),
            in_specs=[pl.BlockSpec((tile, width), lambda i, me: (i, 0)), pl.BlockSpec(memory_space=pl.ANY)],
            out_specs=pl.BlockSpec((1, tile, width), lambda i, me: (me[0], i, 0))),
        out_shape=jax.ShapeDtypeStruct(gathered.shape, gathered.dtype), input_output_aliases={2: 0},
        compiler_params=_params("parallel"),
    )(_chip_index(), slab, gathered)


def _sibling_exchange(gs):
    n = len(gs)

    def body(*refs):
        g_refs, out_refs, send_sems, recv_sems = refs[:n], refs[n:2 * n], refs[2 * n], refs[2 * n + 1]
        x, y, c = _place()
        copies = [pltpu.make_async_remote_copy(
            src_ref=g_ref.at[s, 1 - c], dst_ref=out_ref.at[s], send_sem=send_sems.at[N_SHARD * g + s],
            recv_sem=recv_sems.at[N_SHARD * g + s], device_id=(x, y, 1 - c), device_id_type=MESH)
            for g, (g_ref, out_ref) in enumerate(zip(g_refs, out_refs)) for s in range(N_SHARD)]
        for cp in copies:
            cp.start()
        for cp in copies:
            cp.wait()

    return pl.pallas_call(
        body, name="grad_sibling_exchange", in_specs=[HBM_SPEC] * n, out_specs=[HBM_SPEC] * n,
        out_shape=[jax.ShapeDtypeStruct((N_SHARD,) + g.shape[2:], g.dtype) for g in gs],
        scratch_shapes=[pltpu.SemaphoreType.DMA((N_SHARD * n,)), pltpu.SemaphoreType.DMA((N_SHARD * n,))],
    )(*gs)


def _add_sibling(name, g, got):
    _, _, rh, width = g.shape
    tile = _div_tile(rh, 256, 16)
    c_arr = jnp.reshape(lax.axis_index("c"), (1,)).astype(jnp.int32)

    def body(c_ref, a_ref, b_ref, o_ref):
        o_ref[...] = (a_ref[0] + b_ref[...]).astype(o_ref.dtype)

    return pl.pallas_call(
        body, name=name,
        grid_spec=pltpu.PrefetchScalarGridSpec(
            num_scalar_prefetch=1, grid=(N_SHARD, rh // tile),
            in_specs=[pl.BlockSpec((1, 1, tile, width), lambda s, i, c: (s, c[0], i, 0)),
                      pl.BlockSpec((1, tile, width), lambda s, i, c: (s, i, 0))],
            out_specs=pl.BlockSpec((1, tile, width), lambda s, i, c: (s, i, 0))),
        out_shape=jax.ShapeDtypeStruct((N_SHARD, rh, width), BF16),
        compiler_params=_params("parallel", "parallel"),
    )(c_arr, g, got)


SEM_SPEC = pl.BlockSpec(memory_space=pltpu.SEMAPHORE)


def _exchange_copies(p_refs, land_refs, send_sems, recv_sems):
    x, y, c = _place()
    me = 2 * x + y
    return [pltpu.make_async_remote_copy(
        src_ref=p_ref.at[2 * px + py], dst_ref=land_ref.at[me], send_sem=send_sems.at[3 * g + j],
        recv_sem=recv_sems.at[3 * g + j], device_id=(px, py, c), device_id_type=MESH)
        for g, (p_ref, land_ref) in enumerate(zip(p_refs, land_refs)) for j, (px, py) in enumerate(_other_chips(x, y))]


def _chip_exchange_start(ps):
    n = len(ps)

    def body(*refs):
        for cp in _exchange_copies(refs[:n], refs[n:2 * n], refs[2 * n], refs[2 * n + 1]):
            cp.start()
        refs[-1][...] = jnp.zeros_like(refs[-1])

    hbm = lambda a: pltpu.with_memory_space_constraint(a, pltpu.HBM)
    out = pl.pallas_call(
        body, name="grad_chip_exchange_start",
        out_shape=(pltpu.SemaphoreType.DMA((3 * n,)), pltpu.SemaphoreType.DMA((3 * n,)),
                   *[pltpu.HBM(a.shape, a.dtype) for a in ps], *[pltpu.HBM(a.shape, a.dtype) for a in ps],
                   jax.ShapeDtypeStruct((8, LANES), F32)),
        in_specs=[HBM_SPEC] * (2 * n),
        out_specs=(SEM_SPEC, SEM_SPEC, *[HBM_SPEC] * (2 * n), pl.BlockSpec(memory_space=pltpu.VMEM)),
        input_output_aliases={i: 2 + i for i in range(2 * n)},
        compiler_params=pltpu.CompilerParams(has_side_effects=pltpu.SideEffectType.DATAFLOW_SIDE_EFFECTING),
    )(*[hbm(a) for a in ps], *[hbm(lax.empty(a.shape, a.dtype)) for a in ps])
    return out[0], out[1], list(out[2:2 + n]), list(out[2 + n:2 + 2 * n]), out[-1]


def _chip_exchange_wait(send_sems, recv_sems, ps, landed, after):
    n = len(ps)

    def body(*refs):
        for cp in _exchange_copies(refs[:n], refs[n:2 * n], refs[2 * n], refs[2 * n + 1]):
            cp.wait_send()
            cp.wait_recv()

    out = pl.pallas_call(
        body, name="grad_chip_exchange_wait",
        out_shape=[pltpu.HBM(a.shape, a.dtype) for a in ps + landed],
        in_specs=[HBM_SPEC] * (2 * n) + [SEM_SPEC, SEM_SPEC, pl.BlockSpec(memory_space=pl.ANY)],
        out_specs=[HBM_SPEC] * (2 * n), input_output_aliases={i: i for i in range(2 * n)},
        compiler_params=pltpu.CompilerParams(has_side_effects=pltpu.SideEffectType.DATAFLOW_SIDE_EFFECTING),
    )(*ps, *landed, send_sems, recv_sems, after)
    return list(out[:n]), list(out[n:])


def _sum_chips(name, p, got):
    _, rh, width = p.shape
    tile = _div_tile(rh, 256, 16)
    n_t = rh // tile
    place = jnp.stack([2 * lax.axis_index("x") + lax.axis_index("y"), lax.axis_index("c")]).astype(jnp.int32)

    def body(pl_ref, own_ref, r0, r1, r2, r3, o_ref):
        me = pl_ref[0]
        own = own_ref[0].astype(F32)
        t = [jnp.where(me == s, own, r[0].astype(F32)) for s, r in enumerate((r0, r1, r2, r3))]
        o_ref[...] = ((t[0] + t[1]) + t[2]) + t[3]

    def slot(s):
        return pl.BlockSpec((1, tile, width), lambda i, pc: (jnp.where(pc[0] == s, (s + 1) % N_SHARD, s), i, 0))

    return pl.pallas_call(
        body, name=name,
        grid_spec=pltpu.PrefetchScalarGridSpec(
            num_scalar_prefetch=1, grid=(n_t,),
            in_specs=[pl.BlockSpec((1, tile, width), lambda i, pc: (pc[0], i, 0))] + [slot(s) for s in range(N_SHARD)],
            out_specs=pl.BlockSpec((tile, width), lambda i, pc: (pc[1] * n_t + i, 0))),
        out_shape=jax.ShapeDtypeStruct((2 * rh, width), F32), compiler_params=_params("parallel"),
    )(place, p, got, got, got, got)


def _join_halves(fulls):
    n = len(fulls)

    def body(*refs):
        f_refs, out_refs, send_sems, recv_sems = refs[:n], refs[n:2 * n], refs[2 * n], refs[2 * n + 1]
        x, y, c = _place()

        def copy(g, half):
            rh = f_refs[g].shape[0] // 2
            return pltpu.make_async_remote_copy(
                src_ref=f_refs[g].at[pl.ds(half * rh, rh), :], dst_ref=out_refs[g].at[pl.ds(half * rh, rh), :],
                send_sem=send_sems.at[g], recv_sem=recv_sems.at[g], device_id=(x, y, 1 - c), device_id_type=MESH)

        for g in range(n):
            copy(g, c).start()
        for g in range(n):
            copy(g, c).wait_send()
            copy(g, 1 - c).wait_recv()

    return pl.pallas_call(
        body, name="grad_join_halves", in_specs=[HBM_SPEC] * n, out_specs=[HBM_SPEC] * n,
        out_shape=[jax.ShapeDtypeStruct(f.shape, f.dtype) for f in fulls],
        input_output_aliases={g: g for g in range(n)},
        scratch_shapes=[pltpu.SemaphoreType.DMA((n,)), pltpu.SemaphoreType.DMA((n,))],
    )(*fulls)


def _reduce_scatter_start(gs):
    gs = [g.reshape(N_SHARD, 2, g.shape[1] // 2, g.shape[2]) for g in gs]
    got = _sibling_exchange(gs)
    chip_sums = [_add_sibling(f"grad_add_sibling_{i}", g, r) for i, (g, r) in enumerate(zip(gs, got))]
    *state, token = _chip_exchange_start(chip_sums)
    return state, token


def _reduce_scatter_end(state, after):
    chip_sums, landed = _chip_exchange_wait(*state, after)
    return _join_halves([_sum_chips(f"grad_sum_chips_{i}", p, r) for i, (p, r) in enumerate(zip(chip_sums, landed))])


def _all_reduce_small(v):
    rows = v.shape[0]

    def body(v_ref, out_ref, gather, send_sems, recv_sems):
        x, y, c = _place()
        gather[4 * x + 2 * y + c] = v_ref[...]
        flips = [(dx, dy, dc) for dx in (0, 1) for dy in (0, 1) for dc in (0, 1)][1:]
        peers = [((x + dx) % 2, (y + dy) % 2, (c + dc) % 2) for dx, dy, dc in flips]
        copies = [pltpu.make_async_remote_copy(
            src_ref=v_ref, dst_ref=gather.at[4 * x + 2 * y + c], send_sem=send_sems.at[j], recv_sem=recv_sems.at[j],
            device_id=peer, device_id_type=MESH) for j, peer in enumerate(peers)]
        for cp in copies:
            cp.start()
        for j, (px, py, pc) in enumerate(peers):
            pltpu.make_async_remote_copy(
                src_ref=v_ref, dst_ref=gather.at[4 * px + 2 * py + pc], send_sem=send_sems.at[j],
                recv_sem=recv_sems.at[j], device_id=(px, py, pc), device_id_type=MESH).wait_recv()
        for cp in copies:
            cp.wait_send()
        acc = gather[0]
        for d in range(1, 8):
            acc = acc + gather[d]
        out_ref[...] = acc

    vm = pl.BlockSpec(memory_space=pltpu.VMEM)
    return pl.pallas_call(
        body, name="small_grads_all_reduce", in_specs=[vm], out_specs=vm,
        out_shape=jax.ShapeDtypeStruct(v.shape, F32),
        scratch_shapes=[pltpu.VMEM((8, rows, LANES), F32), pltpu.SemaphoreType.DMA((7,)), pltpu.SemaphoreType.DMA((7,))],
    )(v)


def _pad_lanes(v):
    v = v.reshape(1, -1)
    return jnp.pad(v, ((0, 0), (0, -v.shape[1] % LANES)))


def _pack_small(vs, rows):
    flat = jnp.concatenate([_pad_lanes(v) for v in vs], axis=1)
    return jnp.pad(flat, ((0, 0), (0, rows * LANES - flat.shape[1]))).reshape(rows, LANES)


def _unpack_small(packed, shapes):
    flat = packed.reshape(-1)
    out, off = [], 0
    for s in shapes:
        n = 1
        for d in s:
            n *= d
        out.append(flat[off:off + n].reshape(s))
        off += n + (-n % LANES)
    return out


BIG = ("w_in", "rw_w_lora_up", "rw_a_lora_up", "w_up_rwkv", "w_up_fox", "w_out", "ple_proj", "ple_gate_w")
ROW_SHARDED = ("w_out", "ple_gate_w")
SMALL = ("norm_g", "rw_shift_mu", "rw_w0", "rw_a0", "rw_k_k", "rw_k_a", "rw_r_k", "rw_ln_g", "rw_ln_b", "fox_b_f",
         "ple_norm_g", "final_norm_g")
WEIGHTS = ("norm_g", "w_in", "rw_shift_mu", "rw_w0", "rw_w_lora_up", "rw_a0", "rw_a_lora_up", "rw_k_k", "rw_k_a",
           "rw_r_k", "rw_ln_g", "rw_ln_b", "fox_b_f", "w_up_rwkv", "w_up_fox", "w_out", "ple_proj", "ple_gate_w",
           "ple_norm_g", "final_norm_g")


def kernel(x, p, norm_g, w_in, rw_shift_mu, rw_w0, rw_w_lora_up, rw_a0, rw_a_lora_up, rw_k_k, rw_k_a, rw_r_k, rw_ln_g, rw_ln_b, fox_b_f, w_up_rwkv, w_up_fox, w_out, ple_proj, ple_gate_w, ple_norm_g, final_norm_g, loss_target, m_norm_g, m_w_in, m_rw_shift_mu, m_rw_w0, m_rw_w_lora_up, m_rw_a0, m_rw_a_lora_up, m_rw_k_k, m_rw_k_a, m_rw_r_k, m_rw_ln_g, m_rw_ln_b, m_fox_b_f, m_w_up_rwkv, m_w_up_fox, m_w_out, m_ple_proj, m_ple_gate_w, m_ple_norm_g, m_final_norm_g, v_norm_g, v_w_in, v_rw_shift_mu, v_rw_w0, v_rw_w_lora_up, v_rw_a0, v_rw_a_lora_up, v_rw_k_k, v_rw_k_a, v_rw_r_k, v_rw_ln_g, v_rw_ln_b, v_fox_b_f, v_w_up_rwkv, v_w_up_fox, v_w_out, v_ple_proj, v_ple_gate_w, v_ple_norm_g, v_final_norm_g):
    wts = dict(norm_g=norm_g, w_in=w_in, rw_shift_mu=rw_shift_mu, rw_w0=rw_w0, rw_w_lora_up=rw_w_lora_up, rw_a0=rw_a0,
               rw_a_lora_up=rw_a_lora_up, rw_k_k=rw_k_k, rw_k_a=rw_k_a, rw_r_k=rw_r_k, rw_ln_g=rw_ln_g, rw_ln_b=rw_ln_b,
               fox_b_f=fox_b_f, w_up_rwkv=w_up_rwkv, w_up_fox=w_up_fox, w_out=w_out, ple_proj=ple_proj,
               ple_gate_w=ple_gate_w, ple_norm_g=ple_norm_g, final_norm_g=final_norm_g)
    mom = dict(norm_g=m_norm_g, w_in=m_w_in, rw_shift_mu=m_rw_shift_mu, rw_w0=m_rw_w0, rw_w_lora_up=m_rw_w_lora_up,
               rw_a0=m_rw_a0, rw_a_lora_up=m_rw_a_lora_up, rw_k_k=m_rw_k_k, rw_k_a=m_rw_k_a, rw_r_k=m_rw_r_k,
               rw_ln_g=m_rw_ln_g, rw_ln_b=m_rw_ln_b, fox_b_f=m_fox_b_f, w_up_rwkv=m_w_up_rwkv, w_up_fox=m_w_up_fox,
               w_out=m_w_out, ple_proj=m_ple_proj, ple_gate_w=m_ple_gate_w, ple_norm_g=m_ple_norm_g,
               final_norm_g=m_final_norm_g)
    vel = dict(norm_g=v_norm_g, w_in=v_w_in, rw_shift_mu=v_rw_shift_mu, rw_w0=v_rw_w0, rw_w_lora_up=v_rw_w_lora_up,
               rw_a0=v_rw_a0, rw_a_lora_up=v_rw_a_lora_up, rw_k_k=v_rw_k_k, rw_k_a=v_rw_k_a, rw_r_k=v_rw_r_k,
               rw_ln_g=v_rw_ln_g, rw_ln_b=v_rw_ln_b, fox_b_f=v_fox_b_f, w_up_rwkv=v_w_up_rwkv, w_up_fox=v_w_up_fox,
               w_out=v_w_out, ple_proj=v_ple_proj, ple_gate_w=v_ple_gate_w, ple_norm_g=v_ple_norm_g,
               final_norm_g=v_final_norm_g)

    t, d = x.shape[1], x.shape[2]
    cw = rw_w0.shape[1]
    lr = rw_w_lora_up.shape[1]
    fh = fox_b_f.shape[1]
    fw = fh * HEAD
    rw_cols = 4 * cw + 2 * lr
    fox_cols = 4 * fw + fh
    assert 2 * lr == LANES and cw % LANES == 0 and fw % LANES == 0 and fh <= LANES
    xs = x[0]
    ps = p[0, 0]
    tgt = loss_target[0]

    groups = {}
    for n in BIG:
        groups.setdefault(wts[n].shape[2], []).append(n)
    groups = list(groups.values())
    slabs16 = []
    for gi, names in enumerate(groups):
        slab = jnp.concatenate([wts[n][0] for n in names], axis=0)
        rows, width = slab.shape
        tile_c = _div_tile(rows, 256, 32)
        slabs16 += _rows(f"weights_to_bf16_{gi}", lambda i, a: (a,), rows, tile_c, [(slab, _row_spec(tile_c, width))],
                         [(width, BF16)])
    gathered = [_place_own_shard(f"weights_place_own_{gi}", g, own)
                for gi, (g, own) in enumerate(zip(_all_gather_shards(slabs16), slabs16))]
    full = {}
    for names, g in zip(groups, gathered):
        off = 0
        for n in names:
            r = wts[n].shape[1]
            part = g[:, off:off + r, :]
            full[n] = (part.reshape(N_SHARD * r, -1) if n in ROW_SHARDED
                       else jnp.concatenate([part[s] for s in range(N_SHARD)], axis=1))
            off += r
    w_rw = full["w_in"][:, :rw_cols]
    w_fox = jnp.pad(full["w_in"][:, rw_cols:rw_cols + fox_cols], ((0, 0), (0, LANES - fh)))
    w_gate = full["w_in"][:, rw_cols + fox_cols:]
    wup_pad = jnp.pad(full["rw_w_lora_up"], ((0, lr), (0, 0)))
    aup_pad = jnp.pad(full["rw_a_lora_up"], ((lr, 0), (0, 0)))
    b_pad = _pad_lanes(fox_b_f)
    r_k_row = rw_r_k.reshape(1, cw)
    gf_row = final_norm_g.reshape(1, d)

    tile = _tile(t, (256, 128, 64, 32, 16, 8))
    tile_s = _tile(t, (128, 64, 32, 16, 8))
    n_s = t // tile_s
    full2 = lambda a: (a, _full_spec(a.shape))

    (h,) = _rows("norm1", lambda i, a, g: (_rms(a, g),), t, tile, [(xs, _row_spec(tile, d)), full2(norm_g)], [(d, BF16)])
    z_rw = _matmul("proj_rw", h, w_rw)
    z_fox = _matmul("proj_fox", h, w_fox)
    z_gate = _matmul("proj_gate", h, w_gate)

    pre_consts = [full2(rw_shift_mu), full2(rw_w0), full2(rw_a0), full2(wup_pad), full2(aup_pad), full2(rw_k_k),
                  full2(rw_k_a)]

    def pre_fwd(i, z, prev8, *consts):
        return _rw_pre(z, _shifted(i, z, prev8), *consts, cw=cw)

    r_, w_, k_, v_, kk_, a_, g_ = _rows(
        "rwkv_pre", pre_fwd, t, tile_s,
        [(z_rw, _row_spec(tile_s, rw_cols)), (z_rw, _prev_rows_spec(tile_s, rw_cols))] + pre_consts, [(cw, F32)] * 7)
    y_scan, states, tinvs = _scan_fwd(r_, w_, k_, v_, kk_, a_)
    post_consts = [full2(rw_ln_g), full2(rw_ln_b), full2(r_k_row)]
    post_rows = lambda *arrs: [(a, _row_spec(tile_s, cw)) for a in arrs]
    (y_rw,) = _rows("rwkv_post", lambda i, *a: (_rw_post(*a, cw=cw),), t, tile_s,
                    post_rows(y_scan, r_, k_, v_, g_) + post_consts, [(cw, BF16)])

    c_fox = _fox_cumsum(z_fox, b_pad, fw=fw, fh=fh)
    tq = _fox_tile(t)
    n_pair_f = fw // LANES
    head_rows = lambda a: a.T.reshape(n_pair_f, 2, t // tq, tq).transpose(0, 2, 1, 3)
    head_cols = lambda a: a.transpose(0, 2, 1, 3).reshape(fh, t).T
    ct_fox = head_rows(c_fox[:, :fh])
    o_fox, lse_fox, y_fox = _fox_fwd(z_fox, c_fox, ct_fox, fw=fw)

    u_rw = _matmul("up_rwkv", y_rw, full["w_up_rwkv"])
    u_fox = _matmul("up_fox", y_fox, full["w_up_fox"])
    (merged,) = _rows("merge", lambda i, zg, a, b: (_merge(zg, a, b, d=d),), t, tile,
                      [(z_gate, _row_spec(tile, 2 * d)), (u_rw, _row_spec(tile, d)), (u_fox, _row_spec(tile, d))],
                      [(d, BF16)])
    x1 = _matmul("out_proj", merged, full["w_out"], add=xs)
    (n2,) = _rows("norm2", lambda i, a, g: (_rms(a, g),), t, tile, [(x1, _row_spec(tile, d)), full2(ple_norm_g)],
                  [(d, BF16)])
    gl = _matmul("ple_gate", n2, full["ple_gate_w"])
    ple = _matmul("ple_proj", ps, full["ple_proj"])

    def head_bwd(i, x1_t, ple_t, gl_t, gf, tg):
        loss, vjp = jax.vjp(lambda a, b, cc, g: _head_loss(a, b, cc, g, tg), x1_t, ple_t, gl_t, gf)
        dx1, dple, dgl, dgf = vjp(jnp.ones((1, 1), F32))
        return dx1, dple, dgl, jnp.broadcast_to(loss, (1, LANES)), dgf

    dx2, dple, dgl, loss_row, d_gf = _rows(
        "loss_head", head_bwd, t, tile_s,
        [(x1, _row_spec(tile_s, d)), (ple, _row_spec(tile_s, d)), (gl, _row_spec(tile_s, d)), full2(gf_row),
         (tgt, _row_spec(tile_s, d))],
        [(d, F32), (d, BF16), (d, BF16)], [(1, LANES), (1, d)])

    g_ple_proj = _matmul("d_ple_proj", ps, dple, ta=True)
    g_ple_gate = _matmul("d_ple_gate_w", n2, dgl, ta=True)
    dn2 = _matmul("d_n2", dgl, full["ple_gate_w"], tb=True)

    def norm_bwd(i, a, g, dh, res):
        _, vjp = jax.vjp(_rms, a, g)
        da, dg = vjp(dh)
        return res + da, dg

    dx1, d_g2 = _rows("norm2_bwd", norm_bwd, t, tile_s,
                      [(x1, _row_spec(tile_s, d)), full2(ple_norm_g), (dn2, _row_spec(tile_s, d)),
                       (dx2, _row_spec(tile_s, d))], [(d, F32)], [(1, d)])
    g_w_out = _matmul("d_w_out", merged, dx1, ta=True)
    dmerged = _matmul("d_merged", dx1, full["w_out"], tb=True)

    def merge_bwd(i, zg, a, b, dm):
        _, vjp = jax.vjp(functools.partial(_merge, d=d), zg, a, b)
        return vjp(dm)

    dz_gate, du_rw, du_fox = _rows(
        "merge_bwd", merge_bwd, t, tile_s,
        [(z_gate, _row_spec(tile_s, 2 * d)), (u_rw, _row_spec(tile_s, d)), (u_fox, _row_spec(tile_s, d)),
         (dmerged, _row_spec(tile_s, d))], [(2 * d, BF16), (d, BF16), (d, BF16)])
    g_up_rw = _matmul("d_w_up_rwkv", y_rw, du_rw, ta=True)
    g_up_fox = _matmul("d_w_up_fox", y_fox, du_fox, ta=True)
    dy_rw = _matmul("d_y_rwkv", du_rw, full["w_up_rwkv"], tb=True)
    dy_fox = _matmul("d_y_fox", du_fox, full["w_up_fox"], tb=True)

    def post_bwd(i, y, r, k, v, g, ln_g, ln_b, r_k, dy):
        _, vjp = jax.vjp(functools.partial(_rw_post, cw=cw), y, r, k, v, g, ln_g, ln_b, r_k)
        return vjp(dy)

    dys, dr1, dk1, dv1, dg1, d_ln_g, d_ln_b, d_r_k = _rows(
        "rwkv_post_bwd", post_bwd, t, tile_s,
        post_rows(y_scan, r_, k_, v_, g_) + post_consts + post_rows(dy_rw), [(cw, F32)] * 5, [(1, cw)] * 3)
    dr2, dw2, dk2, dv2, dkk2, da2 = _scan_bwd(r_, w_, k_, v_, kk_, a_, states, tinvs, dys)

    def pre_bwd(i, z, prev8, mu, w0, a0, wup, aup, k_k, k_a, dr_a, dr_b, dk_a, dk_b, dv_a, dv_b, dw, dkk, da, dg):
        zp = _shifted(i, z, prev8)
        _, vjp = jax.vjp(functools.partial(_rw_pre, cw=cw), z, zp, mu, w0, a0, wup, aup, k_k, k_a)
        dz, dzp, dmu, dw0, da0, dwup, daup, dk_k, dk_a = vjp((dr_a + dr_b, dw, dk_a + dk_b, dv_a + dv_b, dkk, da, dg))
        row = lax.broadcasted_iota(jnp.int32, dz.shape, 0)
        dz = dz + jnp.where(row < tile_s - 1, pltpu.roll(dzp, tile_s - 1, 0), 0.0)
        first = jnp.where(lax.broadcasted_iota(jnp.int32, (8, dz.shape[1]), 0) == 0, _row_of(dzp, 0), 0.0)
        return dz, first, dmu, dw0, da0, dwup, daup, dk_k, dk_a

    def pre_bwd_call():
        n_in = 2 + len(pre_consts) + 10
        ins = ([(z_rw, _row_spec(tile_s, rw_cols)), (z_rw, _prev_rows_spec(tile_s, rw_cols))] + pre_consts
               + post_rows(dr1, dr2, dk1, dk2, dv1, dv2, dw2, dkk2, da2, dg1))

        def body(*refs):
            i = pl.program_id(0)
            vals = pre_bwd(i, *[r[...] for r in refs[:n_in]])
            refs[n_in][...] = vals[0]
            refs[n_in + 1][...] = vals[1]
            for r, v in zip(refs[n_in + 2:], vals[2:]):
                @pl.when(i == 0)
                def _(r=r, v=v):
                    r[...] = v

                @pl.when(i > 0)
                def _(r=r, v=v):
                    r[...] += v

        acc_shapes = [(1, rw_cols), (1, cw), (1, cw), (LANES, cw), (LANES, cw), (1, cw), (1, cw)]
        return pl.pallas_call(
            body, name="rwkv_pre_bwd", grid=(n_s,), in_specs=[s for _, s in ins],
            out_specs=[_row_spec(tile_s, rw_cols), pl.BlockSpec((8, rw_cols), lambda i: (i, 0))]
            + [_full_spec(s) for s in acc_shapes],
            out_shape=[jax.ShapeDtypeStruct((t, rw_cols), F32), jax.ShapeDtypeStruct((8 * n_s, rw_cols), F32)]
            + [jax.ShapeDtypeStruct(s, F32) for s in acc_shapes],
            compiler_params=_params("arbitrary"),
        )(*[a for a, _ in ins])

    dz_main, dz_first, d_mu, d_w0, d_a0, d_wup, d_aup, d_k_k, d_k_a = pre_bwd_call()

    def add_next_row(i, dz, nxt8):
        row = lax.broadcasted_iota(jnp.int32, dz.shape, 0)
        carry = jnp.where(i < n_s - 1, _row_of(nxt8, 0), 0.0)
        return (dz + jnp.where(row == tile_s - 1, carry, 0.0),)

    (dz_rw,) = _rows("rwkv_shift_bwd", add_next_row, t, tile_s,
                     [(dz_main, _row_spec(tile_s, rw_cols)), (dz_first, _next_rows_spec(tile_s, rw_cols, n_s))],
                     [(rw_cols, BF16)])

    def fox_post_bwd(i, o, g, dy):
        _, vjp = jax.vjp(lambda oo, gg: oo * (gg * jax.nn.sigmoid(gg)), o, g)
        do, dg = vjp(dy)
        return do, _head_sum(do * o, _head_matrix(fw)), dg

    do_fox, dd_fox, dg_fox = _rows(
        "fox_post_bwd", fox_post_bwd, t, tile_s,
        [(o_fox, _row_spec(tile_s, fw)), (z_fox, _row_spec(tile_s, fw, 3)), (dy_fox, _row_spec(tile_s, fw))],
        [(fw, F32), (fw, F32), (fw, BF16)])
    dq_f, dk_f, dv_f, dcq, dck = _fox_bwd(z_fox, do_fox, c_fox, ct_fox, head_rows(lse_fox[:, ::HEAD]),
                                          head_rows(dd_fox[:, ::HEAD]), fw=fw)
    dc = head_cols(dcq) + dck.reshape(t, n_pair_f, LANES)[:, :, :2].reshape(t, fh)
    dfl, d_bf = _fox_cumsum_bwd(z_fox, b_pad, jnp.pad(dc, ((0, 0), (0, LANES - fh))), fw=fw, fh=fh)
    dz_fox = jnp.concatenate([dq_f, dk_f, dv_f, dg_fox, dfl], axis=1)

    g_w_rw = _matmul("d_w_in_rw", h, dz_rw, ta=True)
    g_w_fox = _matmul("d_w_in_fox", h, dz_fox, ta=True)
    g_w_gate = _matmul("d_w_in_gate", h, dz_gate, ta=True)

    g_full = {
        "w_in": jnp.concatenate([g_w_rw, g_w_fox[:, :fox_cols], g_w_gate], axis=1),
        "rw_w_lora_up": d_wup[:lr], "rw_a_lora_up": d_aup[lr:], "w_up_rwkv": g_up_rw, "w_up_fox": g_up_fox,
        "w_out": g_w_out, "ple_proj": g_ple_proj, "ple_gate_w": g_ple_gate,
    }
    def by_shard(n):
        g = g_full[n]
        if n in ROW_SHARDED:
            return g.reshape(N_SHARD, g.shape[0] // N_SHARD, g.shape[1])
        return jnp.stack(jnp.split(g, N_SHARD, axis=1))

    rs_state, token = _reduce_scatter_start([jnp.concatenate([by_shard(n) for n in names], axis=1) for names in groups])
    dh = _matmul("d_h_rw", dz_rw, w_rw, tb=True, after=token)
    dh = _matmul("d_h_fox", dz_fox, w_fox, tb=True, add=dh)
    dh = _matmul("d_h_gate", dz_gate, w_gate, tb=True, add=dh)
    grad_x, d_g1 = _rows("norm1_bwd", norm_bwd, t, tile_s,
                         [(xs, _row_spec(tile_s, d)), full2(norm_g), (dh, _row_spec(tile_s, d)),
                          (dx1, _row_spec(tile_s, d))], [(d, F32)], [(1, d)])
    reduced = _reduce_scatter_end(rs_state, d_g1)
    grads = {}
    for names, g in zip(groups, reduced):
        off = 0
        for n in names:
            r = wts[n].shape[1]
            grads[n] = g[off:off + r][None]
            off += r

    small_parts = dict(norm_g=d_g1, rw_shift_mu=d_mu, rw_w0=d_w0, rw_a0=d_a0, rw_k_k=d_k_k, rw_k_a=d_k_a, rw_r_k=d_r_k,
                       rw_ln_g=d_ln_g, rw_ln_b=d_ln_b, fox_b_f=d_bf[:, :fh], ple_norm_g=d_g2, final_norm_g=d_gf)
    n_small = sum(-(-wts[n].size // LANES) for n in SMALL)
    small_rows = -(-n_small // 8) * 8
    small_shapes = [wts[n].shape for n in SMALL]
    g_small = _all_reduce_small(_pack_small([small_parts[n] for n in SMALL], small_rows))
    for n, g in zip(SMALL, _unpack_small(g_small, small_shapes)):
        grads[n] = g

    delta, new_m, new_v = {}, {}, {}
    for n in BIG:
        delta[n], new_m[n], new_v[n] = _adamw("adamw_" + n, wts[n], grads[n], mom[n], vel[n])
    packed = lambda src: _pack_small([src[n] for n in SMALL], small_rows)[None]
    for store, out in zip((delta, new_m, new_v), _adamw("adamw_small", packed(wts), g_small[None], packed(mom), packed(vel))):
        for n, a in zip(SMALL, _unpack_small(out[0], small_shapes)):
            store[n] = a

    loss = lax.psum(loss_row[0, 0], ("x", "y", "c"))
    return (loss, grad_x[None], *[grads[n] for n in WEIGHTS], *[delta[n] for n in WEIGHTS],
            *[new_m[n] for n in WEIGHTS], *[new_v[n] for n in WEIGHTS])
```

```python
import functools

import jax
import jax.numpy as jnp
from jax import lax
from jax.experimental import pallas as pl
from jax.experimental.pallas import tpu as pltpu

F32 = jnp.float32
BF16 = jnp.bfloat16
HIGHEST = lax.Precision.HIGHEST
SCAN_PREC = lax.Precision.HIGH
MESH = pl.DeviceIdType.MESH

LANES = 128
HEAD = 64
NORM_EPS = 1e-6
GN_EPS = 64e-5
ADAM_LR = 0.001
ADAM_B1 = 0.9
ADAM_B2 = 0.999
ADAM_EPS = 1e-08
ADAM_WD = 0.01
ADAM_STEP = 10
N_SHARD = 4
VMEM_LIMIT = 56 * 1024 * 1024
PAIRS_PER_STEP = 8


def _params(*sem):
    return pltpu.CompilerParams(dimension_semantics=sem, vmem_limit_bytes=VMEM_LIMIT)


def _tile(n, cands):
    for c in cands:
        if c <= n and n % c == 0:
            return c
    return n


def _div_tile(n, cap, mult):
    return max(c for c in range(mult, min(n, cap) + 1, mult) if n % c == 0)


_ROW_TILES = (512, 256, 128, 64, 32, 16, 8)


def _dot(a, b, prec=None):
    return lax.dot_general(a, b, (((1,), (0,)), ((), ())), precision=prec, preferred_element_type=F32)


def _dot_nt(a, b, prec=None):
    return lax.dot_general(a, b, (((1,), (1,)), ((), ())), precision=prec, preferred_element_type=F32)


def _dot_tn(a, b, prec=None):
    return lax.dot_general(a, b, (((0,), (0,)), ((), ())), precision=prec, preferred_element_type=F32)


@jax.custom_vjp
def _bdot(x, w):
    return _dot(x.astype(BF16), w.astype(BF16))


def _bdot_fwd(x, w):
    return _bdot(x, w), (x, w)


def _bdot_bwd(res, ct):
    x, w = res
    return _dot_nt(ct.astype(BF16), w.astype(BF16)), _dot_tn(x.astype(BF16), ct.astype(BF16))


_bdot.defvjp(_bdot_fwd, _bdot_bwd)


def _head_matrix(width):
    c = lax.broadcasted_iota(jnp.int32, (width, LANES), 0)
    h = lax.broadcasted_iota(jnp.int32, (width, LANES), 1)
    return (c // HEAD == h).astype(F32)


def _head_sum(x, e):
    return _dot_nt(_dot(x, e, SCAN_PREC), e, SCAN_PREC)


def _softplus(x):
    return jnp.maximum(x, 0.0) + jnp.log1p(jnp.exp(-jnp.abs(x)))


def _lane_pick(x, idx):
    lane = lax.broadcasted_iota(jnp.int32, x.shape, 1)
    return jnp.sum(jnp.where(lane == idx, x, 0.0), axis=1, keepdims=True)


def _matmul(name, a, b, *, ta=False, tb=False, add=None, out_dtype=F32, after=None):
    m, k = (a.shape[1], a.shape[0]) if ta else a.shape
    n = b.shape[0] if tb else b.shape[1]
    tm = _tile(m, (1024, 512, 256, 128))
    tn = _tile(n, (1408, 1024, 768, 640, 512, 384, 256, 128))
    tk = _tile(k, (1408, 1024, 768, 640, 512, 384, 256, 128, 64, 32, 16))
    nk = k // tk
    dims = (((0 if ta else 1,), (1 if tb else 0,)), ((), ()))

    def body(*refs):
        a_ref, b_ref = refs[0], refs[1]
        o_ref, acc_ref = refs[-2], refs[-1]
        kk = pl.program_id(2)

        @pl.when(kk == 0)
        def _():
            acc_ref[...] = jnp.zeros_like(acc_ref)

        acc_ref[...] += lax.dot_general(a_ref[...].astype(BF16), b_ref[...].astype(BF16), dims,
                                        preferred_element_type=F32)

        @pl.when(kk == nk - 1)
        def _():
            r = acc_ref[...]
            if add is not None:
                r = r + refs[2][...].astype(F32)
            o_ref[...] = r.astype(o_ref.dtype)

    a_spec = pl.BlockSpec((tk, tm), lambda i, j, kk: (kk, i)) if ta else pl.BlockSpec((tm, tk), lambda i, j, kk: (i, kk))
    b_spec = pl.BlockSpec((tn, tk), lambda i, j, kk: (j, kk)) if tb else pl.BlockSpec((tk, tn), lambda i, j, kk: (kk, j))
    o_spec = pl.BlockSpec((tm, tn), lambda i, j, kk: (i, j))
    ins, specs = [a, b], [a_spec, b_spec]
    if add is not None:
        ins.append(add)
        specs.append(o_spec)
    if after is not None:
        ins.append(after)
        specs.append(pl.BlockSpec(after.shape, lambda i, j, kk: (0,) * after.ndim))
    return pl.pallas_call(
        body, name=name, grid=(m // tm, n // tn, nk), in_specs=specs, out_specs=o_spec,
        out_shape=jax.ShapeDtypeStruct((m, n), out_dtype),
        scratch_shapes=[pltpu.VMEM((tm, tn), F32)],
        compiler_params=_params("parallel", "parallel", "arbitrary"),
    )(*ins)


def _rows(name, fn, n_rows, tile, ins, outs, accs=()):
    n_in, n_out = len(ins), len(outs)

    def body(*refs):
        i = pl.program_id(0)
        vals = fn(i, *[r[...] for r in refs[:n_in]])
        for r, v in zip(refs[n_in:n_in + n_out], vals[:n_out]):
            r[...] = v.astype(r.dtype)
        for r, v in zip(refs[n_in + n_out:], vals[n_out:]):
            @pl.when(i == 0)
            def _(r=r, v=v):
                r[...] = v

            @pl.when(i > 0)
            def _(r=r, v=v):
                r[...] += v

    out_specs = [pl.BlockSpec((tile, w), lambda i: (i, 0)) for w, _ in outs]
    out_specs += [pl.BlockSpec(s, lambda i: (0, 0)) for s in accs]
    out_shape = [jax.ShapeDtypeStruct((n_rows, w), d) for w, d in outs]
    out_shape += [jax.ShapeDtypeStruct(s, F32) for s in accs]
    return pl.pallas_call(
        body, name=name, grid=(n_rows // tile,), in_specs=[s for _, s in ins], out_specs=out_specs,
        out_shape=out_shape, compiler_params=_params("arbitrary"),
    )(*[a for a, _ in ins])


def _row_spec(tile, width, col=0):
    return pl.BlockSpec((tile, width), lambda i: (i, col))


def _full_spec(shape):
    return pl.BlockSpec(shape, lambda i: (0,) * len(shape))


def _prev_rows_spec(tile, width):
    return pl.BlockSpec((8, width), lambda i: (jnp.maximum(i * (tile // 8) - 1, 0), 0))


def _next_rows_spec(tile, width, n_tiles):
    return pl.BlockSpec((8, width), lambda i: (jnp.minimum(i + 1, n_tiles - 1), 0))


def _row_of(x8, idx):
    r = lax.broadcasted_iota(jnp.int32, x8.shape, 0)
    return jnp.sum(jnp.where(r == idx, x8, 0.0), axis=0, keepdims=True)


def _rms(x, g):
    return x * lax.rsqrt(jnp.mean(x * x, axis=-1, keepdims=True) + NORM_EPS) * g


def _shifted(i, z, prev8):
    first = jnp.where(i > 0, _row_of(prev8, 7), 0.0)
    row = lax.broadcasted_iota(jnp.int32, z.shape, 0)
    return jnp.where(row == 0, first, pltpu.roll(z, 1, 0))


def _rw_pre(z, zp, mu, w0, a0, wup, aup, k_k, k_a, *, cw):
    zs = z + (zp - z) * mu
    r, k, v, g = (zs[:, j * cw:(j + 1) * cw] for j in range(4))
    lo = zs[:, 4 * cw:4 * cw + LANES]
    w_raw = w0 + _bdot(jnp.tanh(lo), wup)
    decay = jnp.exp(-jnp.exp(-_softplus(-w_raw) - 0.5))
    a = jax.nn.sigmoid(a0 + _bdot(lo, aup))
    e = _head_matrix(cw)
    kk = k * k_k
    kk = kk / jnp.maximum(jnp.sqrt(_head_sum(kk * kk, e)), 1e-12)
    k_mod = k * (1.0 + (a - 1.0) * k_a)
    return r, decay, k_mod, v, kk, a, g


def _rw_post(y, r, k_mod, v, g, ln_g, ln_b, r_k, *, cw):
    e = _head_matrix(cw)
    mu = _head_sum(y, e) * (1.0 / HEAD)
    d = y - mu
    var = _head_sum(d * d, e) * (1.0 / HEAD)
    yn = d * lax.rsqrt(var + GN_EPS) * ln_g + ln_b
    bonus = _head_sum(r * k_mod * r_k, e) * v
    return (yn + bonus) * (g * jax.nn.sigmoid(g))


def _merge(zg, u_rw, u_fox, *, d):
    return jax.nn.sigmoid(zg[:, :d]) * u_rw + jax.nn.sigmoid(zg[:, d:]) * u_fox


def _head_loss(x1, ple, gl, gf, tgt):
    x2 = x1 + ple * jax.nn.sigmoid(gl)
    err = _rms(x2, gf) - tgt
    return 0.5 * jnp.sum(jnp.mean(err * err, axis=-1, keepdims=True), axis=0, keepdims=True)


def _eliminate(lo):
    n, c, _ = lo.shape
    ri = lax.broadcasted_iota(jnp.int32, (n, c, c), 1)
    ci = lax.broadcasted_iota(jnp.int32, (n, c, c), 2)
    x = (ri == ci).astype(F32)
    for s in range(c - 1):
        col = jnp.sum(jnp.where(ci == s, lo, 0.0), axis=2, keepdims=True)
        row = jnp.sum(jnp.where(ri == s, x, 0.0), axis=1, keepdims=True)
        x = x - col * row
    return x


def _batched(a, b, ca, cb):
    return lax.dot_general(a, b, (((ca,), (cb,)), ((0,), (0,))), precision=SCAN_PREC, preferred_element_type=F32)


@jax.custom_vjp
def _unit_lower_inverse(lo, known):
    return _eliminate(lo) if known is None else known


def _uli_fwd(lo, known):
    x = _unit_lower_inverse(lo, known)
    return x, (x, known)


def _uli_bwd(res, dx):
    x, known = res
    dlo = -_batched(_batched(x, dx, 1, 1), x, 2, 2)
    return dlo, (None if known is None else jnp.zeros_like(known))


_unit_lower_inverse.defvjp(_uli_fwd, _uli_bwd)


def _rwkv_chunk(s0, r, w, k, v, kk, a, *, c, tinv_known=None):
    pairs = range(len(s0))
    lane = lax.broadcasted_iota(jnp.int32, (1, LANES), 1)
    heads = (lane < HEAD, lane >= HEAD)
    ti = lax.broadcasted_iota(jnp.int32, (c, c), 0)
    si = lax.broadcasted_iota(jnp.int32, (c, c), 1)
    incl = si <= ti
    strict = si < ti
    tri = incl.astype(F32)
    logw = [jnp.log(w[p]) for p in pairs]
    cum = [_dot(tri, logw[p], HIGHEST) for p in pairs]
    cum_end = [jnp.sum(logw[p], axis=0, keepdims=True) for p in pairs]
    g_inv = [jnp.exp(-cum[p]) for p in pairs]
    to_end = [jnp.exp(cum_end[p] - cum[p]) for p in pairs]
    b = [kk[p] * a[p] for p in pairs]
    beta = [b[p] * g_inv[p] for p in pairs]
    kap = [kk[p] * jnp.exp(cum[p] - logw[p]) for p in pairs]
    kt = [k[p] * g_inv[p] for p in pairs]
    rt = [r[p] * jnp.exp(cum[p]) for p in pairs]
    lhs = [jnp.concatenate([jnp.where(m, x[p], 0.0) for x in (kap, rt) for m in heads], axis=0) for p in pairs]
    vs_beta = [_dot_nt(lhs[p], beta[p], None) for p in pairs]
    vs_kt = [_dot_nt(lhs[p], kt[p], None) for p in pairs]
    strict2 = jnp.concatenate([strict, strict], axis=0)
    incl2 = jnp.concatenate([incl, incl], axis=0)
    lo = [jnp.where(strict2, vs_beta[p][:2 * c], 0.0) for p in pairs]
    mm = [jnp.where(strict2, vs_kt[p][:2 * c], 0.0) for p in pairs]
    arb = [jnp.where(incl2, vs_beta[p][2 * c:], 0.0) for p in pairs]
    ark = [jnp.where(incl2, vs_kt[p][2 * c:], 0.0) for p in pairs]
    per_head = lambda xs: jnp.concatenate([xs[p][h * c:(h + 1) * c][None] for p in pairs for h in (0, 1)])
    tinv = _unit_lower_inverse(per_head(lo), None if tinv_known is None else per_head(tinv_known))
    tinv = [jnp.concatenate([tinv[2 * p], tinv[2 * p + 1]], axis=0) for p in pairs]
    both = lambda x: jnp.where(heads[0], x[:c], x[c:])
    vs_s = [_dot_nt(jnp.concatenate([kap[p], rt[p]], axis=0), s0[p], None) for p in pairs]
    rhs = [vs_s[p][:c] + both(_dot(mm[p], v[p], None)) for p in pairs]
    u = [-both(_dot(tinv[p], rhs[p], None)) for p in pairs]
    y = [vs_s[p][c:] + both(_dot(arb[p], u[p], None) + _dot(ark[p], v[p], None)) for p in pairs]
    rr = lax.broadcasted_iota(jnp.int32, (LANES, LANES), 0) < HEAD
    cc = lax.broadcasted_iota(jnp.int32, (LANES, LANES), 1) < HEAD
    ds = [_dot_tn(jnp.concatenate([u[p], v[p]], axis=0),
                  jnp.concatenate([b[p] * to_end[p], k[p] * to_end[p]], axis=0), None) for p in pairs]
    s1 = [s0[p] * jnp.exp(cum_end[p]) + jnp.where(rr == cc, ds[p], 0.0) for p in pairs]
    return tuple(y), tuple(s1), tuple(tinv)


def _scan_tiles(t, n_pair):
    return _tile(t, (32, 16, 8)), _tile(t, (256, 128, 64, 32)), _tile(n_pair, (PAIRS_PER_STEP, 4, 2, 1))


def _scan_fwd(r, w, k, v, kk, a):
    t, width = r.shape
    c, tb, npb = _scan_tiles(t, width // LANES)
    n_grp, n_blk, n_cb = width // (LANES * npb), t // tb, tb // c

    def body(r_ref, w_ref, k_ref, v_ref, kk_ref, a_ref, y_ref, st_ref, ti_ref, s_scr):
        @pl.when(pl.program_id(1) == 0)
        def _():
            s_scr[...] = jnp.zeros_like(s_scr)

        def chunk(j, carry):
            sl = pl.ds(pl.multiple_of(j * c, c), c)
            lanes = [pl.ds(q * LANES, LANES) for q in range(npb)]
            s0 = tuple(s_scr[q] for q in range(npb))
            cols = lambda ref: tuple(ref[sl, ln] for ln in lanes)
            y, s1, tinv = _rwkv_chunk(s0, cols(r_ref), cols(w_ref), cols(k_ref), cols(v_ref), cols(kk_ref), cols(a_ref),
                                      c=c)
            for q, ln in enumerate(lanes):
                st_ref[q, j] = s0[q]
                ti_ref[q, j] = tinv[q]
                y_ref[sl, ln] = y[q]
                s_scr[q] = s1[q]
            return carry

        lax.fori_loop(0, n_cb, chunk, 0)

    blk = pl.BlockSpec((tb, npb * LANES), lambda p, i: (i, p))
    return pl.pallas_call(
        body, name="rwkv_scan_fwd", grid=(n_grp, n_blk), in_specs=[blk] * 6,
        out_specs=[blk, pl.BlockSpec((npb, n_cb, LANES, LANES), lambda p, i: (p, i, 0, 0)),
                   pl.BlockSpec((npb, n_cb, 2 * c, c), lambda p, i: (p, i, 0, 0))],
        out_shape=[jax.ShapeDtypeStruct((t, width), F32),
                   jax.ShapeDtypeStruct((width // LANES, t // c, LANES, LANES), F32),
                   jax.ShapeDtypeStruct((width // LANES, t // c, 2 * c, c), F32)],
        scratch_shapes=[pltpu.VMEM((npb, LANES, LANES), F32)],
        compiler_params=_params("arbitrary", "arbitrary"),
    )(r, w, k, v, kk, a)


def _scan_bwd(r, w, k, v, kk, a, st, ti, dy):
    t, width = r.shape
    c, tb, npb = _scan_tiles(t, width // LANES)
    n_grp, n_blk, n_cb = width // (LANES * npb), t // tb, tb // c

    def body(r_ref, w_ref, k_ref, v_ref, kk_ref, a_ref, st_ref, ti_ref, dy_ref,
             dr_ref, dw_ref, dk_ref, dv_ref, dkk_ref, da_ref, ds_scr):
        @pl.when(pl.program_id(1) == 0)
        def _():
            ds_scr[...] = jnp.zeros_like(ds_scr)

        def chunk(jj, carry):
            j = n_cb - 1 - jj
            sl = pl.ds(pl.multiple_of(j * c, c), c)
            lanes = [pl.ds(q * LANES, LANES) for q in range(npb)]
            cols = lambda ref: tuple(ref[sl, ln] for ln in lanes)
            args = (tuple(st_ref[q, j] for q in range(npb)), cols(r_ref), cols(w_ref), cols(k_ref), cols(v_ref),
                    cols(kk_ref), cols(a_ref))
            known = tuple(ti_ref[q, j] for q in range(npb))
            _, vjp = jax.vjp(lambda *xs: _rwkv_chunk(*xs, c=c, tinv_known=known)[:2], *args)
            grads = vjp((cols(dy_ref), tuple(ds_scr[q] for q in range(npb))))
            for q, ln in enumerate(lanes):
                ds_scr[q] = grads[0][q]
                for ref, g in zip((dr_ref, dw_ref, dk_ref, dv_ref, dkk_ref, da_ref), grads[1:]):
                    ref[sl, ln] = g[q]
            return carry

        lax.fori_loop(0, n_cb, chunk, 0)

    blk = pl.BlockSpec((tb, npb * LANES), lambda p, i: (n_blk - 1 - i, p))
    stb = pl.BlockSpec((npb, n_cb, LANES, LANES), lambda p, i: (p, n_blk - 1 - i, 0, 0))
    tib = pl.BlockSpec((npb, n_cb, 2 * c, c), lambda p, i: (p, n_blk - 1 - i, 0, 0))
    return pl.pallas_call(
        body, name="rwkv_scan_bwd", grid=(n_grp, n_blk), in_specs=[blk] * 6 + [stb, tib, blk], out_specs=[blk] * 6,
        out_shape=[jax.ShapeDtypeStruct((t, width), F32)] * 6,
        scratch_shapes=[pltpu.VMEM((npb, LANES, LANES), F32)],
        compiler_params=_params("arbitrary", "arbitrary"),
    )(r, w, k, v, kk, a, st, ti, dy)


NEG = -1e30


def _fox_cumsum(zf, b_pad, *, fw, fh):
    t = zf.shape[0]
    tile = _tile(t, (256, 128, 64, 32, 16, 8))

    def body(fl_ref, b_ref, c_ref, carry):
        @pl.when(pl.program_id(0) == 0)
        def _():
            carry[...] = jnp.zeros_like(carry)

        lane = lax.broadcasted_iota(jnp.int32, (tile, LANES), 1)
        logf = jnp.where(lane < fh, -_softplus(-(fl_ref[...] + b_ref[...])), 0.0)
        ri = lax.broadcasted_iota(jnp.int32, (tile, tile), 0)
        ci = lax.broadcasted_iota(jnp.int32, (tile, tile), 1)
        c_ref[...] = carry[...] + _dot((ci <= ri).astype(F32), logf, HIGHEST)
        carry[...] += jnp.sum(logf, axis=0, keepdims=True)

    return pl.pallas_call(
        body, name="fox_cumsum", grid=(t // tile,),
        in_specs=[_row_spec(tile, LANES, 4 * fw // LANES), _full_spec((1, LANES))],
        out_specs=_row_spec(tile, LANES), out_shape=jax.ShapeDtypeStruct((t, LANES), F32),
        scratch_shapes=[pltpu.VMEM((1, LANES), F32)], compiler_params=_params("arbitrary"),
    )(zf, b_pad)


def _fox_cumsum_bwd(zf, b_pad, dc, *, fw, fh):
    t = zf.shape[0]
    tile = _tile(t, (256, 128, 64, 32, 16, 8))
    n = t // tile

    def body(fl_ref, b_ref, dc_ref, dfl_ref, db_ref, carry):
        i = pl.program_id(0)

        @pl.when(i == 0)
        def _():
            carry[...] = jnp.zeros_like(carry)
            db_ref[...] = jnp.zeros_like(db_ref)

        lane = lax.broadcasted_iota(jnp.int32, (tile, LANES), 1)
        dc_t = dc_ref[...]
        ri = lax.broadcasted_iota(jnp.int32, (tile, tile), 0)
        ci = lax.broadcasted_iota(jnp.int32, (tile, tile), 1)
        dlogf = carry[...] + _dot((ci >= ri).astype(F32), dc_t, HIGHEST)
        carry[...] += jnp.sum(dc_t, axis=0, keepdims=True)
        dfl = jnp.where(lane < fh, dlogf * jax.nn.sigmoid(-(fl_ref[...] + b_ref[...])), 0.0)
        dfl_ref[...] = dfl.astype(dfl_ref.dtype)
        db_ref[...] += jnp.sum(dfl, axis=0, keepdims=True)

    rev = lambda col: pl.BlockSpec((tile, LANES), lambda i: (n - 1 - i, col))
    return pl.pallas_call(
        body, name="fox_cumsum_bwd", grid=(n,),
        in_specs=[rev(4 * fw // LANES), _full_spec((1, LANES)), rev(0)],
        out_specs=[rev(0), _full_spec((1, LANES))],
        out_shape=[jax.ShapeDtypeStruct((t, LANES), BF16), jax.ShapeDtypeStruct((1, LANES), F32)],
        scratch_shapes=[pltpu.VMEM((1, LANES), F32)], compiler_params=_params("arbitrary"),
    )(zf, b_pad, dc)


def _fox_tile(t):
    return _tile(t, (512, 256, 128))


def _fox_fwd(zf, c, ct, *, fw):
    t = zf.shape[0]
    tq = _fox_tile(t)
    th = tq // 2
    n_pair, n_q = fw // LANES, t // tq
    scale = HEAD ** -0.5
    chains = [(h, qq) for h in (0, 1) for qq in (0, 1)]

    def body(q_ref, k_ref, v_ref, g_ref, c_ref, ct_ref, o_ref, lse_ref, y_ref):
        hp, i = pl.program_id(0), pl.program_id(1)
        lane = lax.broadcasted_iota(jnp.int32, (1, LANES), 1)
        in_head = (lane < HEAD, lane >= HEAD)
        rows = [pl.ds(qq * th, th) for qq in (0, 1)]
        qh = [jnp.where(in_head[h], q_ref[rows[qq], :] * scale, 0.0).astype(BF16) for h, qq in chains]
        cq = [_lane_pick(c_ref[rows[qq], :], 2 * hp + h) for h, qq in chains]
        qidx = lax.broadcasted_iota(jnp.int32, (th, tq), 0)
        kidx = lax.broadcasted_iota(jnp.int32, (th, tq), 1)

        def kv_step(j, carry, diagonal):
            m, l, acc = carry
            ks = pl.ds(pl.multiple_of(j * tq, tq), tq)
            kb = k_ref[ks, :].astype(BF16)
            vb = v_ref[ks, :]
            vh = [jnp.where(in_head[h], vb, 0.0).astype(BF16) for h in (0, 1)]
            ck = [ct_ref[0, j, pl.ds(h, 1), :] for h in (0, 1)]
            s = [_dot_nt(qh[n], kb) + cq[n] - ck[h] for n, (h, qq) in enumerate(chains)]
            if diagonal:
                s = [jnp.where(qq * th + qidx >= kidx, s[n], NEG) for n, (h, qq) in enumerate(chains)]
            m_new = [jnp.maximum(m[n], jnp.max(s[n], axis=1, keepdims=True)) for n in range(4)]
            p = [jnp.exp(s[n] - m_new[n]) for n in range(4)]
            alpha = [jnp.exp(m[n] - m_new[n]) for n in range(4)]
            l = [l[n] * alpha[n] + jnp.sum(p[n], axis=1, keepdims=True) for n in range(4)]
            pv = [_dot(p[n].astype(BF16), vh[h]) for n, (h, qq) in enumerate(chains)]
            acc = [acc[qq] * jnp.where(in_head[0], alpha[qq], alpha[2 + qq]) + pv[qq] + pv[2 + qq] for qq in (0, 1)]
            return tuple(m_new), tuple(l), tuple(acc)

        init = (tuple(jnp.full((th, 1), NEG, F32) for _ in chains), tuple(jnp.zeros((th, 1), F32) for _ in chains),
                tuple(jnp.zeros((th, LANES), F32) for _ in (0, 1)))
        carry = lax.fori_loop(0, i, functools.partial(kv_step, diagonal=False), init)
        m, l, acc = kv_step(i, carry, True)
        for qq in (0, 1):
            o = acc[qq] / jnp.where(in_head[0], l[qq], l[2 + qq])
            g = g_ref[rows[qq], :]
            o_ref[rows[qq], :] = o
            lse_ref[rows[qq], :] = jnp.where(in_head[0], m[qq] + jnp.log(l[qq]), m[2 + qq] + jnp.log(l[2 + qq]))
            y_ref[rows[qq], :] = (o * (g * jax.nn.sigmoid(g))).astype(y_ref.dtype)

    npw = fw // LANES
    blk = lambda col0: pl.BlockSpec((tq, LANES), lambda hp, i: (i, col0 + hp))
    res = lambda col0: pl.BlockSpec((t, LANES), lambda hp, i: (0, col0 + hp))
    out_blk = pl.BlockSpec((tq, LANES), lambda hp, i: (i, hp))
    return pl.pallas_call(
        body, name="fox_attn_fwd", grid=(n_pair, n_q),
        in_specs=[blk(0), res(npw), res(2 * npw), blk(3 * npw),
                  pl.BlockSpec((tq, LANES), lambda hp, i: (i, 0)),
                  pl.BlockSpec((1, n_q, 2, tq), lambda hp, i: (hp, 0, 0, 0))],
        out_specs=[out_blk, out_blk, out_blk],
        out_shape=[jax.ShapeDtypeStruct((t, fw), F32), jax.ShapeDtypeStruct((t, fw), F32),
                   jax.ShapeDtypeStruct((t, fw), BF16)],
        compiler_params=_params("arbitrary", "arbitrary"),
    )(zf, zf, zf, zf, c, ct)


def _fox_bwd(zf, do, c, ct, lse_r, dd_r, *, fw):
    t = zf.shape[0]
    tq = _fox_tile(t)
    n_pair, n_q = fw // LANES, t // tq
    scale = HEAD ** -0.5

    def body(q_ref, k_ref, v_ref, do_ref, c_ref, ct_ref, lse_ref, dd_ref,
             dq_ref, dk_ref, dv_ref, dcq_ref, dck_ref, dq_acc, dcq_acc):
        hp, j = pl.program_id(0), pl.program_id(1)

        @pl.when(j == 0)
        def _():
            dq_acc[...] = jnp.zeros_like(dq_acc)
            dcq_acc[...] = jnp.zeros_like(dcq_acc)

        lane = lax.broadcasted_iota(jnp.int32, (1, LANES), 1)
        in_head = (lane < HEAD, lane >= HEAD)
        kb = k_ref[...]
        kh = [jnp.where(m, kb, 0.0).astype(BF16) for m in in_head]
        vb = v_ref[...].astype(BF16)
        c_k = c_ref[...]
        ck = [_lane_pick(c_k, 2 * hp + h) for h in (0, 1)]
        kidx = lax.broadcasted_iota(jnp.int32, (tq, tq), 0)
        qidx = lax.broadcasted_iota(jnp.int32, (tq, tq), 1)

        def q_step(i, carry, diagonal):
            dk, dv, dck = carry
            qs = pl.ds(pl.multiple_of(i * tq, tq), tq)
            qf = q_ref[qs, :] * scale
            dof = do_ref[qs, :]
            qh = [jnp.where(m, qf, 0.0).astype(BF16) for m in in_head]
            doh = [jnp.where(m, dof, 0.0).astype(BF16) for m in in_head]
            row = lambda ref, h: ref[0, i, pl.ds(h, 1), :]
            st = [_dot_nt(kh[h], qh[h]) + row(ct_ref, h) - ck[h] for h in (0, 1)]
            p = [jnp.exp(st[h] - row(lse_ref, h)) for h in (0, 1)]
            if diagonal:
                p = [jnp.where(kidx <= qidx, p[h], 0.0) for h in (0, 1)]
            dst = [p[h] * (_dot_nt(vb, doh[h]) - row(dd_ref, h)) for h in (0, 1)]
            p16 = [x.astype(BF16) for x in p]
            ds16 = [x.astype(BF16) for x in dst]
            dv = dv + _dot(p16[0], doh[0]) + _dot(p16[1], doh[1])
            dk = dk + _dot(ds16[0], qh[0]) + _dot(ds16[1], qh[1])
            dq_acc[qs, :] += _dot_tn(ds16[0], kh[0]) + _dot_tn(ds16[1], kh[1])
            for h in (0, 1):
                dcq_acc[i, pl.ds(h, 1), :] += jnp.sum(dst[h], axis=0, keepdims=True)
            dck = tuple(dck[h] - jnp.sum(dst[h], axis=1, keepdims=True) for h in (0, 1))
            return dk, dv, dck

        zero = jnp.zeros((tq, LANES), F32)
        carry = q_step(j, (zero, zero, (jnp.zeros((tq, 1), F32),) * 2), True)
        dk, dv, dck = lax.fori_loop(j + 1, n_q, functools.partial(q_step, diagonal=False), carry)
        dk_ref[...] = dk.astype(dk_ref.dtype)
        dv_ref[...] = dv.astype(dv_ref.dtype)
        dck_ref[...] = jnp.where(lane == 0, dck[0], jnp.where(lane == 1, dck[1], 0.0))

        @pl.when(j == n_q - 1)
        def _():
            dq_ref[...] = (dq_acc[...] * scale).astype(dq_ref.dtype)
            dcq_ref[0] = dcq_acc[...]

    npw = fw // LANES
    res_z = lambda col0: pl.BlockSpec((t, LANES), lambda hp, j: (0, col0 + hp))
    blk_z = lambda col0: pl.BlockSpec((tq, LANES), lambda hp, j: (j, col0 + hp))
    res = pl.BlockSpec((t, LANES), lambda hp, j: (0, hp))
    blk = pl.BlockSpec((tq, LANES), lambda hp, j: (j, hp))
    rows = pl.BlockSpec((1, n_q, 2, tq), lambda hp, j: (hp, 0, 0, 0))
    return pl.pallas_call(
        body, name="fox_attn_bwd", grid=(n_pair, n_q),
        in_specs=[res_z(0), blk_z(npw), blk_z(2 * npw), res, pl.BlockSpec((tq, LANES), lambda hp, j: (j, 0)),
                  rows, rows, rows],
        out_specs=[res, blk, blk, rows, blk],
        out_shape=[jax.ShapeDtypeStruct((t, fw), BF16), jax.ShapeDtypeStruct((t, fw), BF16),
                   jax.ShapeDtypeStruct((t, fw), BF16), jax.ShapeDtypeStruct((n_pair, n_q, 2, tq), F32),
                   jax.ShapeDtypeStruct((t, fw), F32)],
        scratch_shapes=[pltpu.VMEM((t, LANES), F32), pltpu.VMEM((n_q, 2, tq), F32)],
        compiler_params=_params("arbitrary", "arbitrary"),
    )(zf, zf, zf, do, c, ct, lse_r, dd_r)


def _adamw_math(w, g, m, v):
    m = ADAM_B1 * m + (1.0 - ADAM_B1) * g
    v = ADAM_B2 * v + (1.0 - ADAM_B2) * jnp.square(g)
    m_hat = m / (1.0 - ADAM_B1 ** ADAM_STEP)
    v_hat = v / (1.0 - ADAM_B2 ** ADAM_STEP)
    delta = -ADAM_LR * (m_hat / (jnp.sqrt(v_hat) + ADAM_EPS) + ADAM_WD * w)
    return delta, m, v


def _adamw(name, w, g, m, v):
    lead, rows, cols = w.shape
    if lead == 1 and cols % LANES:
        outs = _adamw(name, *[jnp.transpose(a, (2, 0, 1)) for a in (w, g, m, v)])
        return [jnp.transpose(o, (1, 2, 0)) for o in outs]
    if lead == 1:
        tile = _tile(rows, (128, 64, 32, 16, 8))
        spec, steps = pl.BlockSpec((1, tile, cols), lambda i: (0, i, 0)), rows // tile
    else:
        tile = _div_tile(lead, 256, 1)
        spec, steps = pl.BlockSpec((tile, rows, cols), lambda i: (i, 0, 0)), lead // tile

    def body(w_ref, g_ref, m_ref, v_ref, d_ref, mo_ref, vo_ref):
        d_ref[...], mo_ref[...], vo_ref[...] = _adamw_math(w_ref[...], g_ref[...], m_ref[...], v_ref[...])

    return pl.pallas_call(
        body, name=name, grid=(steps,), in_specs=[spec] * 4, out_specs=[spec] * 3,
        out_shape=[jax.ShapeDtypeStruct(w.shape, F32)] * 3, compiler_params=_params("parallel"),
    )(w, g, m, v)


def _place():
    return lax.axis_index("x"), lax.axis_index("y"), lax.axis_index("c")


def _other_chips(x, y):
    return [(1 - x, y), (x, 1 - y), (1 - x, 1 - y)]


HBM_SPEC = pl.BlockSpec(memory_space=pltpu.HBM)
SEM_SPEC = pl.BlockSpec(memory_space=pltpu.SEMAPHORE)


def _all_gather_shards(slabs):
    n = len(slabs)

    def body(*refs):
        src_refs, out_refs, send_sems, recv_sems = refs[:n], refs[n:2 * n], refs[2 * n], refs[2 * n + 1]
        x, y, c = _place()
        me = 2 * x + y
        sibling = (x, y, 1 - c)
        chips = _other_chips(x, y)
        first, passed, waits = [], [], []
        for g, (src_ref, out_ref) in enumerate(zip(src_refs, out_refs)):
            rh = src_ref.shape[0] // 2

            def part(chip, half, out_ref=out_ref, rh=rh):
                return out_ref.at[chip, pl.ds(half * rh, rh), :]

            def copy(k, src, dst, to, g=g):
                return pltpu.make_async_remote_copy(src_ref=src, dst_ref=dst, send_sem=send_sems.at[6 * g + k],
                                                    recv_sem=recv_sems.at[6 * g + k], device_id=to, device_id_type=MESH)

            first += [copy(j, src_ref.at[pl.ds(c * rh, rh), :], part(me, c), (px, py, c))
                      for j, (px, py) in enumerate(chips)]
            for j, (px, py) in enumerate(chips):
                theirs = part(2 * px + py, c)
                passed.append((copy(j, theirs, theirs, sibling), copy(3 + j, theirs, theirs, sibling)))
                other = part(2 * px + py, 1 - c)
                waits.append(copy(3 + j, other, other, sibling))
        for cp in first:
            cp.start()
        for landed, forward in passed:
            landed.wait_recv()
            forward.start()
        for cp in waits:
            cp.wait_recv()
        for cp in first + [fwd for _, fwd in passed]:
            cp.wait_send()

    return pl.pallas_call(
        body, name="weights_all_gather", in_specs=[HBM_SPEC] * n, out_specs=[HBM_SPEC] * n,
        out_shape=[jax.ShapeDtypeStruct((N_SHARD,) + a.shape, a.dtype) for a in slabs],
        scratch_shapes=[pltpu.SemaphoreType.DMA((6 * n,)), pltpu.SemaphoreType.DMA((6 * n,))],
    )(*slabs)


def _gather_ici_copies(src_refs, out_refs, send_sems, recv_sems):
    x, y, c = _place()
    me = 2 * x + y
    copies = []
    for g, (src_ref, out_ref) in enumerate(zip(src_refs, out_refs)):
        rh = src_ref.shape[0] // 2
        copies += [pltpu.make_async_remote_copy(
            src_ref=src_ref.at[pl.ds(c * rh, rh), :], dst_ref=out_ref.at[me, pl.ds(c * rh, rh), :],
            send_sem=send_sems.at[3 * g + j], recv_sem=recv_sems.at[3 * g + j], device_id=(px, py, c),
            device_id_type=MESH) for j, (px, py) in enumerate(_other_chips(x, y))]
    return copies


def _gather_start(slabs):
    n = len(slabs)

    def body(*refs):
        for cp in _gather_ici_copies(refs[:n], refs[n:2 * n], refs[2 * n], refs[2 * n + 1]):
            cp.start()
        refs[-1][...] = jnp.zeros_like(refs[-1])

    hbm = lambda a: pltpu.with_memory_space_constraint(a, pltpu.HBM)
    lands = [(N_SHARD,) + a.shape for a in slabs]
    out = pl.pallas_call(
        body, name="weights_gather_start",
        out_shape=(pltpu.SemaphoreType.DMA((3 * n,)), pltpu.SemaphoreType.DMA((3 * n,)),
                   *[pltpu.HBM(a.shape, a.dtype) for a in slabs],
                   *[pltpu.HBM(sh, a.dtype) for sh, a in zip(lands, slabs)], jax.ShapeDtypeStruct((8, LANES), F32)),
        in_specs=[HBM_SPEC] * (2 * n),
        out_specs=(SEM_SPEC, SEM_SPEC, *[HBM_SPEC] * (2 * n), pl.BlockSpec(memory_space=pltpu.VMEM)),
        input_output_aliases={i: 2 + i for i in range(2 * n)},
        compiler_params=pltpu.CompilerParams(has_side_effects=pltpu.SideEffectType.DATAFLOW_SIDE_EFFECTING),
    )(*[hbm(a) for a in slabs], *[hbm(lax.empty(sh, a.dtype)) for sh, a in zip(lands, slabs)])
    return out[0], out[1], list(out[2:2 + n]), list(out[2 + n:2 + 2 * n]), out[-1]


def _gather_wait(send_sems, recv_sems, slabs, landed, after):
    n = len(slabs)

    def body(*refs):
        for cp in _gather_ici_copies(refs[:n], refs[n:2 * n], refs[2 * n], refs[2 * n + 1]):
            cp.wait_send()
            cp.wait_recv()

    out = pl.pallas_call(
        body, name="weights_gather_wait",
        out_shape=[pltpu.HBM(a.shape, a.dtype) for a in slabs + landed],
        in_specs=[HBM_SPEC] * (2 * n) + [SEM_SPEC, SEM_SPEC, pl.BlockSpec(memory_space=pl.ANY)],
        out_specs=[HBM_SPEC] * (2 * n), input_output_aliases={i: i for i in range(2 * n)},
        compiler_params=pltpu.CompilerParams(has_side_effects=pltpu.SideEffectType.DATAFLOW_SIDE_EFFECTING),
    )(*slabs, *landed, send_sems, recv_sems, after)
    return list(out[n:])


def _gather_forward(gathered):
    n = len(gathered)

    def body(*refs):
        in_refs, out_refs, send_sems, recv_sems = refs[:n], refs[n:2 * n], refs[2 * n], refs[2 * n + 1]
        x, y, c = _place()

        def copy(g, j, chip, half):
            rh = in_refs[g].shape[1] // 2
            return pltpu.make_async_remote_copy(
                src_ref=in_refs[g].at[chip, pl.ds(half * rh, rh), :], dst_ref=out_refs[g].at[chip, pl.ds(half * rh, rh), :],
                send_sem=send_sems.at[3 * g + j], recv_sem=recv_sems.at[3 * g + j], device_id=(x, y, 1 - c),
                device_id_type=MESH)

        chips = [2 * px + py for px, py in _other_chips(x, y)]
        for g in range(n):
            for j, chip in enumerate(chips):
                copy(g, j, chip, c).start()
        for g in range(n):
            for j, chip in enumerate(chips):
                copy(g, j, chip, c).wait_send()
                copy(g, j, chip, 1 - c).wait_recv()

    return pl.pallas_call(
        body, name="weights_gather_forward", in_specs=[HBM_SPEC] * n, out_specs=[HBM_SPEC] * n,
        out_shape=[jax.ShapeDtypeStruct(a.shape, a.dtype) for a in gathered],
        input_output_aliases={g: g for g in range(n)},
        scratch_shapes=[pltpu.SemaphoreType.DMA((3 * n,)), pltpu.SemaphoreType.DMA((3 * n,))],
    )(*gathered)


def _chip_index():
    return jnp.reshape(2 * lax.axis_index("x") + lax.axis_index("y"), (1,)).astype(jnp.int32)


def _place_own_shard(name, gathered, slab):
    rows, width = slab.shape
    tile = _div_tile(rows, 256, 16)

    def body(me_ref, s_ref, g_ref, o_ref):
        o_ref[0] = s_ref[...]

    return pl.pallas_call(
        body, name=name,
        grid_spec=pltpu.PrefetchScalarGridSpec(
            num_scalar_prefetch=1, grid=(rows // tile,),
            in_specs=[pl.BlockSpec((tile, width), lambda i, me: (i, 0)), pl.BlockSpec(memory_space=pl.ANY)],
            out_specs=pl.BlockSpec((1, tile, width), lambda i, me: (me[0], i, 0))),
        out_shape=jax.ShapeDtypeStruct(gathered.shape, gathered.dtype), input_output_aliases={2: 0},
        compiler_params=_params("parallel"),
    )(_chip_index(), slab, gathered)


def _sibling_exchange(gs):
    n = len(gs)

    def body(*refs):
        g_refs, out_refs, send_sems, recv_sems = refs[:n], refs[n:2 * n], refs[2 * n], refs[2 * n + 1]
        x, y, c = _place()
        copies = [pltpu.make_async_remote_copy(
            src_ref=g_ref.at[s, 1 - c], dst_ref=out_ref.at[s], send_sem=send_sems.at[N_SHARD * g + s],
            recv_sem=recv_sems.at[N_SHARD * g + s], device_id=(x, y, 1 - c), device_id_type=MESH)
            for g, (g_ref, out_ref) in enumerate(zip(g_refs, out_refs)) for s in range(N_SHARD)]
        for cp in copies:
            cp.start()
        for cp in copies:
            cp.wait()

    return pl.pallas_call(
        body, name="grad_sibling_exchange", in_specs=[HBM_SPEC] * n, out_specs=[HBM_SPEC] * n,
        out_shape=[jax.ShapeDtypeStruct((N_SHARD,) + g.shape[2:], g.dtype) for g in gs],
        scratch_shapes=[pltpu.SemaphoreType.DMA((N_SHARD * n,)), pltpu.SemaphoreType.DMA((N_SHARD * n,))],
    )(*gs)


def _add_sibling(name, g, got):
    _, _, rh, width = g.shape
    tile = _div_tile(rh, 256, 16)
    c_arr = jnp.reshape(lax.axis_index("c"), (1,)).astype(jnp.int32)

    def body(c_ref, a_ref, b_ref, o_ref):
        o_ref[...] = (a_ref[0] + b_ref[...]).astype(o_ref.dtype)

    return pl.pallas_call(
        body, name=name,
        grid_spec=pltpu.PrefetchScalarGridSpec(
            num_scalar_prefetch=1, grid=(N_SHARD, rh // tile),
            in_specs=[pl.BlockSpec((1, 1, tile, width), lambda s, i, c: (s, c[0], i, 0)),
                      pl.BlockSpec((1, tile, width), lambda s, i, c: (s, i, 0))],
            out_specs=pl.BlockSpec((1, tile, width), lambda s, i, c: (s, i, 0))),
        out_shape=jax.ShapeDtypeStruct((N_SHARD, rh, width), BF16),
        compiler_params=_params("parallel", "parallel"),
    )(c_arr, g, got)


def _exchange_copies(p_refs, land_refs, send_sems, recv_sems):
    x, y, c = _place()
    me = 2 * x + y
    return [pltpu.make_async_remote_copy(
        src_ref=p_ref.at[2 * px + py], dst_ref=land_ref.at[me], send_sem=send_sems.at[3 * g + j],
        recv_sem=recv_sems.at[3 * g + j], device_id=(px, py, c), device_id_type=MESH)
        for g, (p_ref, land_ref) in enumerate(zip(p_refs, land_refs)) for j, (px, py) in enumerate(_other_chips(x, y))]


def _chip_exchange_start(ps):
    n = len(ps)

    def body(*refs):
        for cp in _exchange_copies(refs[:n], refs[n:2 * n], refs[2 * n], refs[2 * n + 1]):
            cp.start()
        refs[-1][...] = jnp.zeros_like(refs[-1])

    hbm = lambda a: pltpu.with_memory_space_constraint(a, pltpu.HBM)
    out = pl.pallas_call(
        body, name="grad_chip_exchange_start",
        out_shape=(pltpu.SemaphoreType.DMA((3 * n,)), pltpu.SemaphoreType.DMA((3 * n,)),
                   *[pltpu.HBM(a.shape, a.dtype) for a in ps], *[pltpu.HBM(a.shape, a.dtype) for a in ps],
                   jax.ShapeDtypeStruct((8, LANES), F32)),
        in_specs=[HBM_SPEC] * (2 * n),
        out_specs=(SEM_SPEC, SEM_SPEC, *[HBM_SPEC] * (2 * n), pl.BlockSpec(memory_space=pltpu.VMEM)),
        input_output_aliases={i: 2 + i for i in range(2 * n)},
        compiler_params=pltpu.CompilerParams(has_side_effects=pltpu.SideEffectType.DATAFLOW_SIDE_EFFECTING),
    )(*[hbm(a) for a in ps], *[hbm(lax.empty(a.shape, a.dtype)) for a in ps])
    return out[0], out[1], list(out[2:2 + n]), list(out[2 + n:2 + 2 * n]), out[-1]


def _chip_exchange_wait(send_sems, recv_sems, ps, landed, after):
    n = len(ps)

    def body(*refs):
        for cp in _exchange_copies(refs[:n], refs[n:2 * n], refs[2 * n], refs[2 * n + 1]):
            cp.wait_send()
            cp.wait_recv()

    out = pl.pallas_call(
        body, name="grad_chip_exchange_wait",
        out_shape=[pltpu.HBM(a.shape, a.dtype) for a in ps + landed],
        in_specs=[HBM_SPEC] * (2 * n) + [SEM_SPEC, SEM_SPEC, pl.BlockSpec(memory_space=pl.ANY)],
        out_specs=[HBM_SPEC] * (2 * n), input_output_aliases={i: i for i in range(2 * n)},
        compiler_params=pltpu.CompilerParams(has_side_effects=pltpu.SideEffectType.DATAFLOW_SIDE_EFFECTING),
    )(*ps, *landed, send_sems, recv_sems, after)
    return list(out[:n]), list(out[n:])


def _sum_chips(name, p, got):
    _, rh, width = p.shape
    tile = _div_tile(rh, 256, 16)
    n_t = rh // tile
    place = jnp.stack([2 * lax.axis_index("x") + lax.axis_index("y"), lax.axis_index("c")]).astype(jnp.int32)

    def body(pl_ref, own_ref, r0, r1, r2, r3, o_ref):
        me = pl_ref[0]
        own = own_ref[0].astype(F32)
        t = [jnp.where(me == s, own, r[0].astype(F32)) for s, r in enumerate((r0, r1, r2, r3))]
        o_ref[...] = ((t[0] + t[1]) + t[2]) + t[3]

    def slot(s):
        return pl.BlockSpec((1, tile, width), lambda i, pc: (jnp.where(pc[0] == s, (s + 1) % N_SHARD, s), i, 0))

    return pl.pallas_call(
        body, name=name,
        grid_spec=pltpu.PrefetchScalarGridSpec(
            num_scalar_prefetch=1, grid=(n_t,),
            in_specs=[pl.BlockSpec((1, tile, width), lambda i, pc: (pc[0], i, 0))] + [slot(s) for s in range(N_SHARD)],
            out_specs=pl.BlockSpec((tile, width), lambda i, pc: (pc[1] * n_t + i, 0))),
        out_shape=jax.ShapeDtypeStruct((2 * rh, width), F32), compiler_params=_params("parallel"),
    )(place, p, got, got, got, got)


def _join_halves(fulls):
    n = len(fulls)

    def body(*refs):
        f_refs, out_refs, send_sems, recv_sems = refs[:n], refs[n:2 * n], refs[2 * n], refs[2 * n + 1]
        x, y, c = _place()

        def copy(g, half):
            rh = f_refs[g].shape[0] // 2
            return pltpu.make_async_remote_copy(
                src_ref=f_refs[g].at[pl.ds(half * rh, rh), :], dst_ref=out_refs[g].at[pl.ds(half * rh, rh), :],
                send_sem=send_sems.at[g], recv_sem=recv_sems.at[g], device_id=(x, y, 1 - c), device_id_type=MESH)

        for g in range(n):
            copy(g, c).start()
        for g in range(n):
            copy(g, c).wait_send()
            copy(g, 1 - c).wait_recv()

    return pl.pallas_call(
        body, name="grad_join_halves", in_specs=[HBM_SPEC] * n, out_specs=[HBM_SPEC] * n,
        out_shape=[jax.ShapeDtypeStruct(f.shape, f.dtype) for f in fulls],
        input_output_aliases={g: g for g in range(n)},
        scratch_shapes=[pltpu.SemaphoreType.DMA((n,)), pltpu.SemaphoreType.DMA((n,))],
    )(*fulls)


def _reduce_scatter_start(gs):
    gs = [g.reshape(N_SHARD, 2, g.shape[1] // 2, g.shape[2]) for g in gs]
    got = _sibling_exchange(gs)
    chip_sums = [_add_sibling(f"grad_add_sibling_{i}", g, r) for i, (g, r) in enumerate(zip(gs, got))]
    *state, token = _chip_exchange_start(chip_sums)
    return state, token


def _reduce_scatter_end(state, after):
    chip_sums, landed = _chip_exchange_wait(*state, after)
    return _join_halves([_sum_chips(f"grad_sum_chips_{i}", p, r) for i, (p, r) in enumerate(zip(chip_sums, landed))])


def _all_reduce_small(v):
    rows = v.shape[0]

    def body(v_ref, out_ref, gather, send_sems, recv_sems):
        x, y, c = _place()
        gather[4 * x + 2 * y + c] = v_ref[...]
        flips = [(dx, dy, dc) for dx in (0, 1) for dy in (0, 1) for dc in (0, 1)][1:]
        peers = [((x + dx) % 2, (y + dy) % 2, (c + dc) % 2) for dx, dy, dc in flips]
        copies = [pltpu.make_async_remote_copy(
            src_ref=v_ref, dst_ref=gather.at[4 * x + 2 * y + c], send_sem=send_sems.at[j], recv_sem=recv_sems.at[j],
            device_id=peer, device_id_type=MESH) for j, peer in enumerate(peers)]
        for cp in copies:
            cp.start()
        for j, (px, py, pc) in enumerate(peers):
            pltpu.make_async_remote_copy(
                src_ref=v_ref, dst_ref=gather.at[4 * px + 2 * py + pc], send_sem=send_sems.at[j],
                recv_sem=recv_sems.at[j], device_id=(px, py, pc), device_id_type=MESH).wait_recv()
        for cp in copies:
            cp.wait_send()
        acc = gather[0]
        for d in range(1, 8):
            acc = acc + gather[d]
        out_ref[...] = acc

    vm = pl.BlockSpec(memory_space=pltpu.VMEM)
    return pl.pallas_call(
        body, name="small_grads_all_reduce", in_specs=[vm], out_specs=vm,
        out_shape=jax.ShapeDtypeStruct(v.shape, F32),
        scratch_shapes=[pltpu.VMEM((8, rows, LANES), F32), pltpu.SemaphoreType.DMA((7,)), pltpu.SemaphoreType.DMA((7,))],
    )(v)


def _pad_lanes(v):
    v = v.reshape(1, -1)
    return jnp.pad(v, ((0, 0), (0, -v.shape[1] % LANES)))


def _pack_small(vs, rows):
    flat = jnp.concatenate([_pad_lanes(v) for v in vs], axis=1)
    return jnp.pad(flat, ((0, 0), (0, rows * LANES - flat.shape[1]))).reshape(rows, LANES)


def _unpack_small(packed, shapes):
    flat = packed.reshape(-1)
    out, off = [], 0
    for s in shapes:
        n = 1
        for d in s:
            n *= d
        out.append(flat[off:off + n].reshape(s))
        off += n + (-n % LANES)
    return out


BIG = ("w_in", "rw_w_lora_up", "rw_a_lora_up", "w_up_rwkv", "w_up_fox", "w_out", "ple_proj", "ple_gate_w")
ROW_SHARDED = ("w_out", "ple_gate_w")
FIRST_NEEDED = ("w_in", "rw_w_lora_up", "rw_a_lora_up")
SMALL = ("norm_g", "rw_shift_mu", "rw_w0", "rw_a0", "rw_k_k", "rw_k_a", "rw_r_k", "rw_ln_g", "rw_ln_b", "fox_b_f",
         "ple_norm_g", "final_norm_g")
WEIGHTS = ("norm_g", "w_in", "rw_shift_mu", "rw_w0", "rw_w_lora_up", "rw_a0", "rw_a_lora_up", "rw_k_k", "rw_k_a",
           "rw_r_k", "rw_ln_g", "rw_ln_b", "fox_b_f", "w_up_rwkv", "w_up_fox", "w_out", "ple_proj", "ple_gate_w",
           "ple_norm_g", "final_norm_g")


def kernel(x, p, norm_g, w_in, rw_shift_mu, rw_w0, rw_w_lora_up, rw_a0, rw_a_lora_up, rw_k_k, rw_k_a, rw_r_k, rw_ln_g, rw_ln_b, fox_b_f, w_up_rwkv, w_up_fox, w_out, ple_proj, ple_gate_w, ple_norm_g, final_norm_g, loss_target, m_norm_g, m_w_in, m_rw_shift_mu, m_rw_w0, m_rw_w_lora_up, m_rw_a0, m_rw_a_lora_up, m_rw_k_k, m_rw_k_a, m_rw_r_k, m_rw_ln_g, m_rw_ln_b, m_fox_b_f, m_w_up_rwkv, m_w_up_fox, m_w_out, m_ple_proj, m_ple_gate_w, m_ple_norm_g, m_final_norm_g, v_norm_g, v_w_in, v_rw_shift_mu, v_rw_w0, v_rw_w_lora_up, v_rw_a0, v_rw_a_lora_up, v_rw_k_k, v_rw_k_a, v_rw_r_k, v_rw_ln_g, v_rw_ln_b, v_fox_b_f, v_w_up_rwkv, v_w_up_fox, v_w_out, v_ple_proj, v_ple_gate_w, v_ple_norm_g, v_final_norm_g):
    wts = dict(norm_g=norm_g, w_in=w_in, rw_shift_mu=rw_shift_mu, rw_w0=rw_w0, rw_w_lora_up=rw_w_lora_up, rw_a0=rw_a0,
               rw_a_lora_up=rw_a_lora_up, rw_k_k=rw_k_k, rw_k_a=rw_k_a, rw_r_k=rw_r_k, rw_ln_g=rw_ln_g, rw_ln_b=rw_ln_b,
               fox_b_f=fox_b_f, w_up_rwkv=w_up_rwkv, w_up_fox=w_up_fox, w_out=w_out, ple_proj=ple_proj,
               ple_gate_w=ple_gate_w, ple_norm_g=ple_norm_g, final_norm_g=final_norm_g)
    mom = dict(norm_g=m_norm_g, w_in=m_w_in, rw_shift_mu=m_rw_shift_mu, rw_w0=m_rw_w0, rw_w_lora_up=m_rw_w_lora_up,
               rw_a0=m_rw_a0, rw_a_lora_up=m_rw_a_lora_up, rw_k_k=m_rw_k_k, rw_k_a=m_rw_k_a, rw_r_k=m_rw_r_k,
               rw_ln_g=m_rw_ln_g, rw_ln_b=m_rw_ln_b, fox_b_f=m_fox_b_f, w_up_rwkv=m_w_up_rwkv, w_up_fox=m_w_up_fox,
               w_out=m_w_out, ple_proj=m_ple_proj, ple_gate_w=m_ple_gate_w, ple_norm_g=m_ple_norm_g,
               final_norm_g=m_final_norm_g)
    vel = dict(norm_g=v_norm_g, w_in=v_w_in, rw_shift_mu=v_rw_shift_mu, rw_w0=v_rw_w0, rw_w_lora_up=v_rw_w_lora_up,
               rw_a0=v_rw_a0, rw_a_lora_up=v_rw_a_lora_up, rw_k_k=v_rw_k_k, rw_k_a=v_rw_k_a, rw_r_k=v_rw_r_k,
               rw_ln_g=v_rw_ln_g, rw_ln_b=v_rw_ln_b, fox_b_f=v_fox_b_f, w_up_rwkv=v_w_up_rwkv, w_up_fox=v_w_up_fox,
               w_out=v_w_out, ple_proj=v_ple_proj, ple_gate_w=v_ple_gate_w, ple_norm_g=v_ple_norm_g,
               final_norm_g=v_final_norm_g)

    t, d = x.shape[1], x.shape[2]
    cw = rw_w0.shape[1]
    lr = rw_w_lora_up.shape[1]
    fh = fox_b_f.shape[1]
    fw = fh * HEAD
    rw_cols = 4 * cw + 2 * lr
    fox_cols = 4 * fw + fh
    assert 2 * lr == LANES and cw % LANES == 0 and fw % LANES == 0 and fh <= LANES
    xs = x[0]
    ps = p[0, 0]
    tgt = loss_target[0]

    groups = {}
    for n in BIG:
        groups.setdefault(wts[n].shape[2], []).append(n)
    groups = list(groups.values())
    slabs16 = []
    for gi, names in enumerate(groups):
        slab = jnp.concatenate([wts[n][0] for n in names], axis=0)
        rows, width = slab.shape
        tile_c = _div_tile(rows, 256, 32)
        slabs16 += _rows(f"weights_to_bf16_{gi}", lambda i, a: (a,), rows, tile_c, [(slab, _row_spec(tile_c, width))],
                         [(width, BF16)])
    first = [gi for gi, names in enumerate(groups) if not set(names).isdisjoint(FIRST_NEEDED)]
    later = [gi for gi in range(len(groups)) if gi not in first]
    gathered = dict(zip(first, _all_gather_shards([slabs16[gi] for gi in first])))
    *gather_state, gather_token = _gather_start([slabs16[gi] for gi in later])
    full = {}

    def assemble(gis):
        for gi in gis:
            g = _place_own_shard(f"weights_place_own_{gi}", gathered[gi], slabs16[gi])
            off = 0
            for n in groups[gi]:
                r = wts[n].shape[1]
                part = g[:, off:off + r, :]
                full[n] = (part.reshape(N_SHARD * r, -1) if n in ROW_SHARDED
                           else jnp.concatenate([part[s] for s in range(N_SHARD)], axis=1))
                off += r

    assemble(first)
    w_rw = full["w_in"][:, :rw_cols]
    w_fox = jnp.pad(full["w_in"][:, rw_cols:rw_cols + fox_cols], ((0, 0), (0, LANES - fh)))
    w_gate = full["w_in"][:, rw_cols + fox_cols:]
    wup_pad = jnp.pad(full["rw_w_lora_up"], ((0, lr), (0, 0)))
    aup_pad = jnp.pad(full["rw_a_lora_up"], ((lr, 0), (0, 0)))
    b_pad = _pad_lanes(fox_b_f)
    r_k_row = rw_r_k.reshape(1, cw)
    gf_row = final_norm_g.reshape(1, d)

    tile = _tile(t, (256, 128, 64, 32, 16, 8))
    tile_s = _tile(t, (128, 64, 32, 16, 8))
    n_s = t // tile_s
    full2 = lambda a: (a, _full_spec(a.shape))

    (h,) = _rows("norm1", lambda i, a, g: (_rms(a, g),), t, tile, [(xs, _row_spec(tile, d)), full2(norm_g)], [(d, BF16)])
    z_rw = _matmul("proj_rw", h, w_rw, after=gather_token)
    z_fox = _matmul("proj_fox", h, w_fox, after=z_rw[:8, :LANES])
    z_gate = _matmul("proj_gate", h, w_gate, after=z_fox[:8, :LANES])
    gathered.update(zip(later, _gather_forward(_gather_wait(*gather_state, z_gate))))
    assemble(later)

    pre_consts = [full2(rw_shift_mu), full2(rw_w0), full2(rw_a0), full2(wup_pad), full2(aup_pad), full2(rw_k_k),
                  full2(rw_k_a)]

    def pre_fwd(i, z, prev8, *consts):
        return _rw_pre(z, _shifted(i, z, prev8), *consts, cw=cw)

    r_, w_, k_, v_, kk_, a_, g_ = _rows(
        "rwkv_pre", pre_fwd, t, tile_s,
        [(z_rw, _row_spec(tile_s, rw_cols)), (z_rw, _prev_rows_spec(tile_s, rw_cols))] + pre_consts, [(cw, F32)] * 7)
    y_scan, states, tinvs = _scan_fwd(r_, w_, k_, v_, kk_, a_)
    post_consts = [full2(rw_ln_g), full2(rw_ln_b), full2(r_k_row)]
    post_rows = lambda *arrs: [(a, _row_spec(tile_s, cw)) for a in arrs]
    (y_rw,) = _rows("rwkv_post", lambda i, *a: (_rw_post(*a, cw=cw),), t, tile_s,
                    post_rows(y_scan, r_, k_, v_, g_) + post_consts, [(cw, BF16)])

    c_fox = _fox_cumsum(z_fox, b_pad, fw=fw, fh=fh)
    tq = _fox_tile(t)
    n_pair_f = fw // LANES
    head_rows = lambda a: a.T.reshape(n_pair_f, 2, t // tq, tq).transpose(0, 2, 1, 3)
    head_cols = lambda a: a.transpose(0, 2, 1, 3).reshape(fh, t).T
    ct_fox = head_rows(c_fox[:, :fh])
    o_fox, lse_fox, y_fox = _fox_fwd(z_fox, c_fox, ct_fox, fw=fw)

    u_rw = _matmul("up_rwkv", y_rw, full["w_up_rwkv"])
    u_fox = _matmul("up_fox", y_fox, full["w_up_fox"])
    (merged,) = _rows("merge", lambda i, zg, a, b: (_merge(zg, a, b, d=d),), t, tile,
                      [(z_gate, _row_spec(tile, 2 * d)), (u_rw, _row_spec(tile, d)), (u_fox, _row_spec(tile, d))],
                      [(d, BF16)])
    x1 = _matmul("out_proj", merged, full["w_out"], add=xs)
    (n2,) = _rows("norm2", lambda i, a, g: (_rms(a, g),), t, tile, [(x1, _row_spec(tile, d)), full2(ple_norm_g)],
                  [(d, BF16)])
    gl = _matmul("ple_gate", n2, full["ple_gate_w"])
    ple = _matmul("ple_proj", ps, full["ple_proj"])

    def head_bwd(i, x1_t, ple_t, gl_t, gf, tg):
        loss, vjp = jax.vjp(lambda a, b, cc, g: _head_loss(a, b, cc, g, tg), x1_t, ple_t, gl_t, gf)
        dx1, dple, dgl, dgf = vjp(jnp.ones((1, 1), F32))
        return dx1, dple, dgl, jnp.broadcast_to(loss, (1, LANES)), dgf

    dx2, dple, dgl, loss_row, d_gf = _rows(
        "loss_head", head_bwd, t, tile_s,
        [(x1, _row_spec(tile_s, d)), (ple, _row_spec(tile_s, d)), (gl, _row_spec(tile_s, d)), full2(gf_row),
         (tgt, _row_spec(tile_s, d))],
        [(d, F32), (d, BF16), (d, BF16)], [(1, LANES), (1, d)])

    g_ple_proj = _matmul("d_ple_proj", ps, dple, ta=True)
    g_ple_gate = _matmul("d_ple_gate_w", n2, dgl, ta=True)
    dn2 = _matmul("d_n2", dgl, full["ple_gate_w"], tb=True)

    def norm_bwd(i, a, g, dh, res):
        _, vjp = jax.vjp(_rms, a, g)
        da, dg = vjp(dh)
        return res + da, dg

    dx1, d_g2 = _rows("norm2_bwd", norm_bwd, t, tile_s,
                      [(x1, _row_spec(tile_s, d)), full2(ple_norm_g), (dn2, _row_spec(tile_s, d)),
                       (dx2, _row_spec(tile_s, d))], [(d, F32)], [(1, d)])
    g_w_out = _matmul("d_w_out", merged, dx1, ta=True)
    dmerged = _matmul("d_merged", dx1, full["w_out"], tb=True)

    def merge_bwd(i, zg, a, b, dm):
        _, vjp = jax.vjp(functools.partial(_merge, d=d), zg, a, b)
        return vjp(dm)

    dz_gate, du_rw, du_fox = _rows(
        "merge_bwd", merge_bwd, t, tile_s,
        [(z_gate, _row_spec(tile_s, 2 * d)), (u_rw, _row_spec(tile_s, d)), (u_fox, _row_spec(tile_s, d)),
         (dmerged, _row_spec(tile_s, d))], [(2 * d, BF16), (d, BF16), (d, BF16)])
    g_up_rw = _matmul("d_w_up_rwkv", y_rw, du_rw, ta=True)
    g_up_fox = _matmul("d_w_up_fox", y_fox, du_fox, ta=True)
    dy_rw = _matmul("d_y_rwkv", du_rw, full["w_up_rwkv"], tb=True)
    dy_fox = _matmul("d_y_fox", du_fox, full["w_up_fox"], tb=True)

    def post_bwd(i, y, r, k, v, g, ln_g, ln_b, r_k, dy):
        _, vjp = jax.vjp(functools.partial(_rw_post, cw=cw), y, r, k, v, g, ln_g, ln_b, r_k)
        return vjp(dy)

    dys, dr1, dk1, dv1, dg1, d_ln_g, d_ln_b, d_r_k = _rows(
        "rwkv_post_bwd", post_bwd, t, tile_s,
        post_rows(y_scan, r_, k_, v_, g_) + post_consts + post_rows(dy_rw), [(cw, F32)] * 5, [(1, cw)] * 3)
    dr2, dw2, dk2, dv2, dkk2, da2 = _scan_bwd(r_, w_, k_, v_, kk_, a_, states, tinvs, dys)

    def pre_bwd(i, z, prev8, mu, w0, a0, wup, aup, k_k, k_a, dr_a, dr_b, dk_a, dk_b, dv_a, dv_b, dw, dkk, da, dg):
        zp = _shifted(i, z, prev8)
        _, vjp = jax.vjp(functools.partial(_rw_pre, cw=cw), z, zp, mu, w0, a0, wup, aup, k_k, k_a)
        dz, dzp, dmu, dw0, da0, dwup, daup, dk_k, dk_a = vjp((dr_a + dr_b, dw, dk_a + dk_b, dv_a + dv_b, dkk, da, dg))
        row = lax.broadcasted_iota(jnp.int32, dz.shape, 0)
        dz = dz + jnp.where(row < tile_s - 1, pltpu.roll(dzp, tile_s - 1, 0), 0.0)
        first = jnp.where(lax.broadcasted_iota(jnp.int32, (8, dz.shape[1]), 0) == 0, _row_of(dzp, 0), 0.0)
        return dz, first, dmu, dw0, da0, dwup, daup, dk_k, dk_a

    def pre_bwd_call():
        n_in = 2 + len(pre_consts) + 10
        ins = ([(z_rw, _row_spec(tile_s, rw_cols)), (z_rw, _prev_rows_spec(tile_s, rw_cols))] + pre_consts
               + post_rows(dr1, dr2, dk1, dk2, dv1, dv2, dw2, dkk2, da2, dg1))

        def body(*refs):
            i = pl.program_id(0)
            vals = pre_bwd(i, *[r[...] for r in refs[:n_in]])
            refs[n_in][...] = vals[0]
            refs[n_in + 1][...] = vals[1]
            for r, v in zip(refs[n_in + 2:], vals[2:]):
                @pl.when(i == 0)
                def _(r=r, v=v):
                    r[...] = v

                @pl.when(i > 0)
                def _(r=r, v=v):
                    r[...] += v

        acc_shapes = [(1, rw_cols), (1, cw), (1, cw), (LANES, cw), (LANES, cw), (1, cw), (1, cw)]
        return pl.pallas_call(
            body, name="rwkv_pre_bwd", grid=(n_s,), in_specs=[s for _, s in ins],
            out_specs=[_row_spec(tile_s, rw_cols), pl.BlockSpec((8, rw_cols), lambda i: (i, 0))]
            + [_full_spec(s) for s in acc_shapes],
            out_shape=[jax.ShapeDtypeStruct((t, rw_cols), F32), jax.ShapeDtypeStruct((8 * n_s, rw_cols), F32)]
            + [jax.ShapeDtypeStruct(s, F32) for s in acc_shapes],
            compiler_params=_params("arbitrary"),
        )(*[a for a, _ in ins])

    dz_main, dz_first, d_mu, d_w0, d_a0, d_wup, d_aup, d_k_k, d_k_a = pre_bwd_call()

    def add_next_row(i, dz, nxt8):
        row = lax.broadcasted_iota(jnp.int32, dz.shape, 0)
        carry = jnp.where(i < n_s - 1, _row_of(nxt8, 0), 0.0)
        return (dz + jnp.where(row == tile_s - 1, carry, 0.0),)

    (dz_rw,) = _rows("rwkv_shift_bwd", add_next_row, t, tile_s,
                     [(dz_main, _row_spec(tile_s, rw_cols)), (dz_first, _next_rows_spec(tile_s, rw_cols, n_s))],
                     [(rw_cols, BF16)])

    def fox_post_bwd(i, o, g, dy):
        _, vjp = jax.vjp(lambda oo, gg: oo * (gg * jax.nn.sigmoid(gg)), o, g)
        do, dg = vjp(dy)
        return do, _head_sum(do * o, _head_matrix(fw)), dg

    do_fox, dd_fox, dg_fox = _rows(
        "fox_post_bwd", fox_post_bwd, t, tile_s,
        [(o_fox, _row_spec(tile_s, fw)), (z_fox, _row_spec(tile_s, fw, 3)), (dy_fox, _row_spec(tile_s, fw))],
        [(fw, F32), (fw, F32), (fw, BF16)])
    dq_f, dk_f, dv_f, dcq, dck = _fox_bwd(z_fox, do_fox, c_fox, ct_fox, head_rows(lse_fox[:, ::HEAD]),
                                          head_rows(dd_fox[:, ::HEAD]), fw=fw)
    dc = head_cols(dcq) + dck.reshape(t, n_pair_f, LANES)[:, :, :2].reshape(t, fh)
    dfl, d_bf = _fox_cumsum_bwd(z_fox, b_pad, jnp.pad(dc, ((0, 0), (0, LANES - fh))), fw=fw, fh=fh)
    dz_fox = jnp.concatenate([dq_f, dk_f, dv_f, dg_fox, dfl], axis=1)

    g_w_rw = _matmul("d_w_in_rw", h, dz_rw, ta=True)
    g_w_fox = _matmul("d_w_in_fox", h, dz_fox, ta=True)
    g_w_gate = _matmul("d_w_in_gate", h, dz_gate, ta=True)

    g_full = {
        "w_in": jnp.concatenate([g_w_rw, g_w_fox[:, :fox_cols], g_w_gate], axis=1),
        "rw_w_lora_up": d_wup[:lr], "rw_a_lora_up": d_aup[lr:], "w_up_rwkv": g_up_rw, "w_up_fox": g_up_fox,
        "w_out": g_w_out, "ple_proj": g_ple_proj, "ple_gate_w": g_ple_gate,
    }
    def by_shard(n):
        g = g_full[n]
        if n in ROW_SHARDED:
            return g.reshape(N_SHARD, g.shape[0] // N_SHARD, g.shape[1])
        return jnp.stack(jnp.split(g, N_SHARD, axis=1))

    rs_state, token = _reduce_scatter_start([jnp.concatenate([by_shard(n) for n in names], axis=1) for names in groups])
    dh = _matmul("d_h_rw", dz_rw, w_rw, tb=True, after=token)
    dh = _matmul("d_h_fox", dz_fox, w_fox, tb=True, add=dh)
    dh = _matmul("d_h_gate", dz_gate, w_gate, tb=True, add=dh)
    grad_x, d_g1 = _rows("norm1_bwd", norm_bwd, t, tile_s,
                         [(xs, _row_spec(tile_s, d)), full2(norm_g), (dh, _row_spec(tile_s, d)),
                          (dx1, _row_spec(tile_s, d))], [(d, F32)], [(1, d)])
    small_parts = dict(norm_g=d_g1, rw_shift_mu=d_mu, rw_w0=d_w0, rw_a0=d_a0, rw_k_k=d_k_k, rw_k_a=d_k_a, rw_r_k=d_r_k,
                       rw_ln_g=d_ln_g, rw_ln_b=d_ln_b, fox_b_f=d_bf[:, :fh], ple_norm_g=d_g2, final_norm_g=d_gf)
    n_small = sum(-(-wts[n].size // LANES) for n in SMALL)
    small_rows = -(-n_small // 8) * 8
    small_shapes = [wts[n].shape for n in SMALL]
    g_small = _all_reduce_small(_pack_small([small_parts[n] for n in SMALL], small_rows))
    reduced = _reduce_scatter_end(rs_state, g_small)
    grads = {}
    for names, g in zip(groups, reduced):
        off = 0
        for n in names:
            r = wts[n].shape[1]
            grads[n] = g[off:off + r][None]
            off += r

    for n, g in zip(SMALL, _unpack_small(g_small, small_shapes)):
        grads[n] = g

    delta, new_m, new_v = {}, {}, {}
    for n in BIG:
        delta[n], new_m[n], new_v[n] = _adamw("adamw_" + n, wts[n], grads[n], mom[n], vel[n])
    packed = lambda src: _pack_small([src[n] for n in SMALL], small_rows)[None]
    for store, out in zip((delta, new_m, new_v), _adamw("adamw_small", packed(wts), g_small[None], packed(mom), packed(vel))):
        for n, a in zip(SMALL, _unpack_small(out[0], small_shapes)):
            store[n] = a

    loss = lax.psum(loss_row[0, 0], ("x", "y", "c"))
    return (loss, grad_x[None], *[grads[n] for n in WEIGHTS], *[delta[n] for n in WEIGHTS],
            *[new_m[n] for n in WEIGHTS], *[new_v[n] for n in WEIGHTS])
```

```python
import functools

import jax
import jax.numpy as jnp
from jax import lax
from jax.experimental import pallas as pl
from jax.experimental.pallas import tpu as pltpu

F32 = jnp.float32
BF16 = jnp.bfloat16
HIGHEST = lax.Precision.HIGHEST
SCAN_PREC = lax.Precision.HIGH
MESH = pl.DeviceIdType.MESH

LANES = 128
HEAD = 64
NORM_EPS = 1e-6
GN_EPS = 64e-5
ADAM_LR = 0.001
ADAM_B1 = 0.9
ADAM_B2 = 0.999
ADAM_EPS = 1e-08
ADAM_WD = 0.01
ADAM_STEP = 10
N_SHARD = 4
VMEM_LIMIT = 56 * 1024 * 1024
PAIRS_PER_STEP = 8
ELIM_BASE = 8


def _params(*sem):
    return pltpu.CompilerParams(dimension_semantics=sem, vmem_limit_bytes=VMEM_LIMIT)


def _tile(n, cands):
    for c in cands:
        if c <= n and n % c == 0:
            return c
    return n


def _div_tile(n, cap, mult):
    return max(c for c in range(mult, min(n, cap) + 1, mult) if n % c == 0)


_ROW_TILES = (512, 256, 128, 64, 32, 16, 8)


def _dot(a, b, prec=None):
    return lax.dot_general(a, b, (((1,), (0,)), ((), ())), precision=prec, preferred_element_type=F32)


def _dot_nt(a, b, prec=None):
    return lax.dot_general(a, b, (((1,), (1,)), ((), ())), precision=prec, preferred_element_type=F32)


def _dot_tn(a, b, prec=None):
    return lax.dot_general(a, b, (((0,), (0,)), ((), ())), precision=prec, preferred_element_type=F32)


@jax.custom_vjp
def _bdot(x, w):
    return _dot(x.astype(BF16), w.astype(BF16))


def _bdot_fwd(x, w):
    return _bdot(x, w), (x, w)


def _bdot_bwd(res, ct):
    x, w = res
    return _dot_nt(ct.astype(BF16), w.astype(BF16)), _dot_tn(x.astype(BF16), ct.astype(BF16))


_bdot.defvjp(_bdot_fwd, _bdot_bwd)


def _head_matrix(width):
    c = lax.broadcasted_iota(jnp.int32, (width, LANES), 0)
    h = lax.broadcasted_iota(jnp.int32, (width, LANES), 1)
    return (c // HEAD == h).astype(F32)


def _head_sum(x, e):
    return _dot_nt(_dot(x, e, SCAN_PREC), e, SCAN_PREC)


def _softplus(x):
    return jnp.maximum(x, 0.0) + jnp.log1p(jnp.exp(-jnp.abs(x)))


def _lane_pick(x, idx):
    lane = lax.broadcasted_iota(jnp.int32, x.shape, 1)
    return jnp.sum(jnp.where(lane == idx, x, 0.0), axis=1, keepdims=True)


def _matmul(name, a, b, *, ta=False, tb=False, add=None, out_dtype=F32, after=None):
    m, k = (a.shape[1], a.shape[0]) if ta else a.shape
    n = b.shape[0] if tb else b.shape[1]
    tm = _tile(m, (1024, 512, 256, 128))
    tn = _tile(n, (1408, 1024, 768, 640, 512, 384, 256, 128))
    tk = _tile(k, (1408, 1024, 768, 640, 512, 384, 256, 128, 64, 32, 16))
    nk = k // tk
    dims = (((0 if ta else 1,), (1 if tb else 0,)), ((), ()))

    def body(*refs):
        a_ref, b_ref = refs[0], refs[1]
        o_ref, acc_ref = refs[-2], refs[-1]
        kk = pl.program_id(2)

        @pl.when(kk == 0)
        def _():
            acc_ref[...] = jnp.zeros_like(acc_ref)

        acc_ref[...] += lax.dot_general(a_ref[...].astype(BF16), b_ref[...].astype(BF16), dims,
                                        preferred_element_type=F32)

        @pl.when(kk == nk - 1)
        def _():
            r = acc_ref[...]
            if add is not None:
                r = r + refs[2][...].astype(F32)
            o_ref[...] = r.astype(o_ref.dtype)

    a_spec = pl.BlockSpec((tk, tm), lambda i, j, kk: (kk, i)) if ta else pl.BlockSpec((tm, tk), lambda i, j, kk: (i, kk))
    b_spec = pl.BlockSpec((tn, tk), lambda i, j, kk: (j, kk)) if tb else pl.BlockSpec((tk, tn), lambda i, j, kk: (kk, j))
    o_spec = pl.BlockSpec((tm, tn), lambda i, j, kk: (i, j))
    ins, specs = [a, b], [a_spec, b_spec]
    if add is not None:
        ins.append(add)
        specs.append(o_spec)
    if after is not None:
        ins.append(after)
        specs.append(pl.BlockSpec(after.shape, lambda i, j, kk: (0,) * after.ndim))
    return pl.pallas_call(
        body, name=name, grid=(m // tm, n // tn, nk), in_specs=specs, out_specs=o_spec,
        out_shape=jax.ShapeDtypeStruct((m, n), out_dtype),
        scratch_shapes=[pltpu.VMEM((tm, tn), F32)],
        compiler_params=_params("parallel", "parallel", "arbitrary"),
    )(*ins)


def _rows(name, fn, n_rows, tile, ins, outs, accs=()):
    n_in, n_out = len(ins), len(outs)

    def body(*refs):
        i = pl.program_id(0)
        vals = fn(i, *[r[...] for r in refs[:n_in]])
        for r, v in zip(refs[n_in:n_in + n_out], vals[:n_out]):
            r[...] = v.astype(r.dtype)
        for r, v in zip(refs[n_in + n_out:], vals[n_out:]):
            @pl.when(i == 0)
            def _(r=r, v=v):
                r[...] = v

            @pl.when(i > 0)
            def _(r=r, v=v):
                r[...] += v

    out_specs = [pl.BlockSpec((tile, w), lambda i: (i, 0)) for w, _ in outs]
    out_specs += [pl.BlockSpec(s, lambda i: (0, 0)) for s in accs]
    out_shape = [jax.ShapeDtypeStruct((n_rows, w), d) for w, d in outs]
    out_shape += [jax.ShapeDtypeStruct(s, F32) for s in accs]
    return pl.pallas_call(
        body, name=name, grid=(n_rows // tile,), in_specs=[s for _, s in ins], out_specs=out_specs,
        out_shape=out_shape, compiler_params=_params("arbitrary"),
    )(*[a for a, _ in ins])


def _row_spec(tile, width, col=0):
    return pl.BlockSpec((tile, width), lambda i: (i, col))


def _full_spec(shape):
    return pl.BlockSpec(shape, lambda i: (0,) * len(shape))


def _prev_rows_spec(tile, width):
    return pl.BlockSpec((8, width), lambda i: (jnp.maximum(i * (tile // 8) - 1, 0), 0))


def _next_rows_spec(tile, width, n_tiles):
    return pl.BlockSpec((8, width), lambda i: (jnp.minimum(i + 1, n_tiles - 1), 0))


def _row_of(x8, idx):
    r = lax.broadcasted_iota(jnp.int32, x8.shape, 0)
    return jnp.sum(jnp.where(r == idx, x8, 0.0), axis=0, keepdims=True)


def _rms(x, g):
    return x * lax.rsqrt(jnp.mean(x * x, axis=-1, keepdims=True) + NORM_EPS) * g


def _shifted(i, z, prev8):
    first = jnp.where(i > 0, _row_of(prev8, 7), 0.0)
    row = lax.broadcasted_iota(jnp.int32, z.shape, 0)
    return jnp.where(row == 0, first, pltpu.roll(z, 1, 0))


def _rw_pre(z, zp, mu, w0, a0, wup, aup, k_k, k_a, *, cw):
    zs = z + (zp - z) * mu
    r, k, v, g = (zs[:, j * cw:(j + 1) * cw] for j in range(4))
    lo = zs[:, 4 * cw:4 * cw + LANES]
    w_raw = w0 + _bdot(jnp.tanh(lo), wup)
    decay = jnp.exp(-jnp.exp(-_softplus(-w_raw) - 0.5))
    a = jax.nn.sigmoid(a0 + _bdot(lo, aup))
    e = _head_matrix(cw)
    kk = k * k_k
    kk = kk / jnp.maximum(jnp.sqrt(_head_sum(kk * kk, e)), 1e-12)
    k_mod = k * (1.0 + (a - 1.0) * k_a)
    return r, decay, k_mod, v, kk, a, g


def _rw_post(y, r, k_mod, v, g, ln_g, ln_b, r_k, *, cw):
    e = _head_matrix(cw)
    mu = _head_sum(y, e) * (1.0 / HEAD)
    d = y - mu
    var = _head_sum(d * d, e) * (1.0 / HEAD)
    yn = d * lax.rsqrt(var + GN_EPS) * ln_g + ln_b
    bonus = _head_sum(r * k_mod * r_k, e) * v
    return (yn + bonus) * (g * jax.nn.sigmoid(g))


def _merge(zg, u_rw, u_fox, *, d):
    return jax.nn.sigmoid(zg[:, :d]) * u_rw + jax.nn.sigmoid(zg[:, d:]) * u_fox


def _head_loss(x1, ple, gl, gf, tgt):
    x2 = x1 + ple * jax.nn.sigmoid(gl)
    err = _rms(x2, gf) - tgt
    return 0.5 * jnp.sum(jnp.mean(err * err, axis=-1, keepdims=True), axis=0, keepdims=True)


def _batched(a, b, ca, cb):
    return lax.dot_general(a, b, (((ca,), (cb,)), ((0,), (0,))), precision=SCAN_PREC, preferred_element_type=F32)


def _eliminate_block(lo, off, m):
    n, c, _ = lo.shape
    ci = lax.broadcasted_iota(jnp.int32, (n, m, c), 2)
    if m <= ELIM_BASE:
        ri = lax.broadcasted_iota(jnp.int32, (n, m, c), 1)
        rows = jnp.where((ci >= off) & (ci < off + m), lo[:, off:off + m, :], 0.0)
        x = (ri + off == ci).astype(F32)
        for s in range(m - 1):
            col = jnp.sum(jnp.where(ci == off + s, rows, 0.0), axis=2, keepdims=True)
            x = x - col * x[:, s:s + 1, :]
        return x
    h = m // 2
    xa = _eliminate_block(lo, off, h)
    xd = _eliminate_block(lo, off + h, h)
    ci_h = lax.broadcasted_iota(jnp.int32, (n, h, c), 2)
    b = jnp.where((ci_h >= off) & (ci_h < off + h), lo[:, off + h:off + m, :], 0.0)

    def at_rows(x, start):
        parts = ([jnp.zeros((n, start, c), F32)] if start else []) + [x]
        rest = c - start - x.shape[1]
        return jnp.concatenate(parts + ([jnp.zeros((n, rest, c), F32)] if rest else []), axis=1)

    low = xd - _batched(_batched(xd, at_rows(b, off + h), 2, 1), at_rows(xa, off), 2, 1)
    return jnp.concatenate([xa, low], axis=1)


def _eliminate(lo):
    return _eliminate_block(lo, 0, lo.shape[1])


@jax.custom_vjp
def _unit_lower_inverse(lo, known):
    return _eliminate(lo) if known is None else known


def _uli_fwd(lo, known):
    x = _unit_lower_inverse(lo, known)
    return x, (x, known)


def _uli_bwd(res, dx):
    x, known = res
    dlo = -_batched(_batched(x, dx, 1, 1), x, 2, 2)
    return dlo, (None if known is None else jnp.zeros_like(known))


_unit_lower_inverse.defvjp(_uli_fwd, _uli_bwd)


def _rwkv_chunk(s0, r, w, k, v, kk, a, *, c, tinv_known=None):
    pairs = range(len(s0))
    lane = lax.broadcasted_iota(jnp.int32, (1, LANES), 1)
    heads = (lane < HEAD, lane >= HEAD)
    ti = lax.broadcasted_iota(jnp.int32, (c, c), 0)
    si = lax.broadcasted_iota(jnp.int32, (c, c), 1)
    incl = si <= ti
    strict = si < ti
    tri = incl.astype(F32)
    logw = [jnp.log(w[p]) for p in pairs]
    cum = [_dot(tri, logw[p], HIGHEST) for p in pairs]
    cum_end = [jnp.sum(logw[p], axis=0, keepdims=True) for p in pairs]
    g_inv = [jnp.exp(-cum[p]) for p in pairs]
    to_end = [jnp.exp(cum_end[p] - cum[p]) for p in pairs]
    b = [kk[p] * a[p] for p in pairs]
    beta = [b[p] * g_inv[p] for p in pairs]
    kap = [kk[p] * jnp.exp(cum[p] - logw[p]) for p in pairs]
    kt = [k[p] * g_inv[p] for p in pairs]
    rt = [r[p] * jnp.exp(cum[p]) for p in pairs]
    lhs = [jnp.concatenate([jnp.where(m, x[p], 0.0) for x in (kap, rt) for m in heads], axis=0) for p in pairs]
    vs_beta = [_dot_nt(lhs[p], beta[p], None) for p in pairs]
    vs_kt = [_dot_nt(lhs[p], kt[p], None) for p in pairs]
    strict2 = jnp.concatenate([strict, strict], axis=0)
    incl2 = jnp.concatenate([incl, incl], axis=0)
    lo = [jnp.where(strict2, vs_beta[p][:2 * c], 0.0) for p in pairs]
    mm = [jnp.where(strict2, vs_kt[p][:2 * c], 0.0) for p in pairs]
    arb = [jnp.where(incl2, vs_beta[p][2 * c:], 0.0) for p in pairs]
    ark = [jnp.where(incl2, vs_kt[p][2 * c:], 0.0) for p in pairs]
    per_head = lambda xs: jnp.concatenate([xs[p][h * c:(h + 1) * c][None] for p in pairs for h in (0, 1)])
    tinv = _unit_lower_inverse(per_head(lo), None if tinv_known is None else per_head(tinv_known))
    tinv = [jnp.concatenate([tinv[2 * p], tinv[2 * p + 1]], axis=0) for p in pairs]
    both = lambda x: jnp.where(heads[0], x[:c], x[c:])
    vs_s = [_dot_nt(jnp.concatenate([kap[p], rt[p]], axis=0), s0[p], None) for p in pairs]
    rhs = [vs_s[p][:c] + both(_dot(mm[p], v[p], None)) for p in pairs]
    u = [-both(_dot(tinv[p], rhs[p], None)) for p in pairs]
    y = [vs_s[p][c:] + both(_dot(arb[p], u[p], None) + _dot(ark[p], v[p], None)) for p in pairs]
    rr = lax.broadcasted_iota(jnp.int32, (LANES, LANES), 0) < HEAD
    cc = lax.broadcasted_iota(jnp.int32, (LANES, LANES), 1) < HEAD
    ds = [_dot_tn(jnp.concatenate([u[p], v[p]], axis=0),
                  jnp.concatenate([b[p] * to_end[p], k[p] * to_end[p]], axis=0), None) for p in pairs]
    s1 = [s0[p] * jnp.exp(cum_end[p]) + jnp.where(rr == cc, ds[p], 0.0) for p in pairs]
    return tuple(y), tuple(s1), tuple(tinv)


def _scan_tiles(t, n_pair):
    return _tile(t, (32, 16, 8)), _tile(t, (256, 128, 64, 32)), _tile(n_pair, (PAIRS_PER_STEP, 4, 2, 1))


def _scan_fwd(r, w, k, v, kk, a):
    t, width = r.shape
    c, tb, npb = _scan_tiles(t, width // LANES)
    n_grp, n_blk, n_cb = width // (LANES * npb), t // tb, tb // c

    def body(r_ref, w_ref, k_ref, v_ref, kk_ref, a_ref, y_ref, st_ref, ti_ref, s_scr):
        @pl.when(pl.program_id(1) == 0)
        def _():
            s_scr[...] = jnp.zeros_like(s_scr)

        def chunk(j, carry):
            sl = pl.ds(pl.multiple_of(j * c, c), c)
            lanes = [pl.ds(q * LANES, LANES) for q in range(npb)]
            s0 = tuple(s_scr[q] for q in range(npb))
            cols = lambda ref: tuple(ref[sl, ln] for ln in lanes)
            y, s1, tinv = _rwkv_chunk(s0, cols(r_ref), cols(w_ref), cols(k_ref), cols(v_ref), cols(kk_ref), cols(a_ref),
                                      c=c)
            for q, ln in enumerate(lanes):
                st_ref[q, j] = s0[q]
                ti_ref[q, j] = tinv[q]
                y_ref[sl, ln] = y[q]
                s_scr[q] = s1[q]
            return carry

        lax.fori_loop(0, n_cb, chunk, 0)

    blk = pl.BlockSpec((tb, npb * LANES), lambda p, i: (i, p))
    return pl.pallas_call(
        body, name="rwkv_scan_fwd", grid=(n_grp, n_blk), in_specs=[blk] * 6,
        out_specs=[blk, pl.BlockSpec((npb, n_cb, LANES, LANES), lambda p, i: (p, i, 0, 0)),
                   pl.BlockSpec((npb, n_cb, 2 * c, c), lambda p, i: (p, i, 0, 0))],
        out_shape=[jax.ShapeDtypeStruct((t, width), F32),
                   jax.ShapeDtypeStruct((width // LANES, t // c, LANES, LANES), F32),
                   jax.ShapeDtypeStruct((width // LANES, t // c, 2 * c, c), F32)],
        scratch_shapes=[pltpu.VMEM((npb, LANES, LANES), F32)],
        compiler_params=_params("arbitrary", "arbitrary"),
    )(r, w, k, v, kk, a)


def _scan_bwd(r, w, k, v, kk, a, st, ti, dy):
    t, width = r.shape
    c, tb, npb = _scan_tiles(t, width // LANES)
    n_grp, n_blk, n_cb = width // (LANES * npb), t // tb, tb // c

    def body(r_ref, w_ref, k_ref, v_ref, kk_ref, a_ref, st_ref, ti_ref, dy_ref,
             dr_ref, dw_ref, dk_ref, dv_ref, dkk_ref, da_ref, ds_scr):
        @pl.when(pl.program_id(1) == 0)
        def _():
            ds_scr[...] = jnp.zeros_like(ds_scr)

        def chunk(jj, carry):
            j = n_cb - 1 - jj
            sl = pl.ds(pl.multiple_of(j * c, c), c)
            lanes = [pl.ds(q * LANES, LANES) for q in range(npb)]
            cols = lambda ref: tuple(ref[sl, ln] for ln in lanes)
            args = (tuple(st_ref[q, j] for q in range(npb)), cols(r_ref), cols(w_ref), cols(k_ref), cols(v_ref),
                    cols(kk_ref), cols(a_ref))
            known = tuple(ti_ref[q, j] for q in range(npb))
            _, vjp = jax.vjp(lambda *xs: _rwkv_chunk(*xs, c=c, tinv_known=known)[:2], *args)
            grads = vjp((cols(dy_ref), tuple(ds_scr[q] for q in range(npb))))
            for q, ln in enumerate(lanes):
                ds_scr[q] = grads[0][q]
                for ref, g in zip((dr_ref, dw_ref, dk_ref, dv_ref, dkk_ref, da_ref), grads[1:]):
                    ref[sl, ln] = g[q]
            return carry

        lax.fori_loop(0, n_cb, chunk, 0)

    blk = pl.BlockSpec((tb, npb * LANES), lambda p, i: (n_blk - 1 - i, p))
    stb = pl.BlockSpec((npb, n_cb, LANES, LANES), lambda p, i: (p, n_blk - 1 - i, 0, 0))
    tib = pl.BlockSpec((npb, n_cb, 2 * c, c), lambda p, i: (p, n_blk - 1 - i, 0, 0))
    return pl.pallas_call(
        body, name="rwkv_scan_bwd", grid=(n_grp, n_blk), in_specs=[blk] * 6 + [stb, tib, blk], out_specs=[blk] * 6,
        out_shape=[jax.ShapeDtypeStruct((t, width), F32)] * 6,
        scratch_shapes=[pltpu.VMEM((npb, LANES, LANES), F32)],
        compiler_params=_params("arbitrary", "arbitrary"),
    )(r, w, k, v, kk, a, st, ti, dy)


NEG = -1e30


def _fox_cumsum(zf, b_pad, *, fw, fh):
    t = zf.shape[0]
    tile = _tile(t, (256, 128, 64, 32, 16, 8))

    def body(fl_ref, b_ref, c_ref, carry):
        @pl.when(pl.program_id(0) == 0)
        def _():
            carry[...] = jnp.zeros_like(carry)

        lane = lax.broadcasted_iota(jnp.int32, (tile, LANES), 1)
        logf = jnp.where(lane < fh, -_softplus(-(fl_ref[...] + b_ref[...])), 0.0)
        ri = lax.broadcasted_iota(jnp.int32, (tile, tile), 0)
        ci = lax.broadcasted_iota(jnp.int32, (tile, tile), 1)
        c_ref[...] = carry[...] + _dot((ci <= ri).astype(F32), logf, HIGHEST)
        carry[...] += jnp.sum(logf, axis=0, keepdims=True)

    return pl.pallas_call(
        body, name="fox_cumsum", grid=(t // tile,),
        in_specs=[_row_spec(tile, LANES, 4 * fw // LANES), _full_spec((1, LANES))],
        out_specs=_row_spec(tile, LANES), out_shape=jax.ShapeDtypeStruct((t, LANES), F32),
        scratch_shapes=[pltpu.VMEM((1, LANES), F32)], compiler_params=_params("arbitrary"),
    )(zf, b_pad)


def _fox_cumsum_bwd(zf, b_pad, dc, *, fw, fh):
    t = zf.shape[0]
    tile = _tile(t, (256, 128, 64, 32, 16, 8))
    n = t // tile

    def body(fl_ref, b_ref, dc_ref, dfl_ref, db_ref, carry):
        i = pl.program_id(0)

        @pl.when(i == 0)
        def _():
            carry[...] = jnp.zeros_like(carry)
            db_ref[...] = jnp.zeros_like(db_ref)

        lane = lax.broadcasted_iota(jnp.int32, (tile, LANES), 1)
        dc_t = dc_ref[...]
        ri = lax.broadcasted_iota(jnp.int32, (tile, tile), 0)
        ci = lax.broadcasted_iota(jnp.int32, (tile, tile), 1)
        dlogf = carry[...] + _dot((ci >= ri).astype(F32), dc_t, HIGHEST)
        carry[...] += jnp.sum(dc_t, axis=0, keepdims=True)
        dfl = jnp.where(lane < fh, dlogf * jax.nn.sigmoid(-(fl_ref[...] + b_ref[...])), 0.0)
        dfl_ref[...] = dfl.astype(dfl_ref.dtype)
        db_ref[...] += jnp.sum(dfl, axis=0, keepdims=True)

    rev = lambda col: pl.BlockSpec((tile, LANES), lambda i: (n - 1 - i, col))
    return pl.pallas_call(
        body, name="fox_cumsum_bwd", grid=(n,),
        in_specs=[rev(4 * fw // LANES), _full_spec((1, LANES)), rev(0)],
        out_specs=[rev(0), _full_spec((1, LANES))],
        out_shape=[jax.ShapeDtypeStruct((t, LANES), BF16), jax.ShapeDtypeStruct((1, LANES), F32)],
        scratch_shapes=[pltpu.VMEM((1, LANES), F32)], compiler_params=_params("arbitrary"),
    )(zf, b_pad, dc)


def _fox_tile(t):
    return _tile(t, (512, 256, 128))


def _fox_fwd(zf, c, ct, *, fw):
    t = zf.shape[0]
    tq = _fox_tile(t)
    th = tq // 2
    n_pair, n_q = fw // LANES, t // tq
    scale = HEAD ** -0.5
    chains = [(h, qq) for h in (0, 1) for qq in (0, 1)]

    def body(q_ref, k_ref, v_ref, g_ref, c_ref, ct_ref, o_ref, lse_ref, y_ref):
        hp, i = pl.program_id(0), pl.program_id(1)
        lane = lax.broadcasted_iota(jnp.int32, (1, LANES), 1)
        in_head = (lane < HEAD, lane >= HEAD)
        rows = [pl.ds(qq * th, th) for qq in (0, 1)]
        qh = [jnp.where(in_head[h], q_ref[rows[qq], :] * scale, 0.0).astype(BF16) for h, qq in chains]
        cq = [_lane_pick(c_ref[rows[qq], :], 2 * hp + h) for h, qq in chains]
        qidx = lax.broadcasted_iota(jnp.int32, (th, tq), 0)
        kidx = lax.broadcasted_iota(jnp.int32, (th, tq), 1)

        def kv_step(j, carry, diagonal):
            m, l, acc = carry
            ks = pl.ds(pl.multiple_of(j * tq, tq), tq)
            kb = k_ref[ks, :].astype(BF16)
            vb = v_ref[ks, :]
            vh = [jnp.where(in_head[h], vb, 0.0).astype(BF16) for h in (0, 1)]
            ck = [ct_ref[0, j, pl.ds(h, 1), :] for h in (0, 1)]
            s = [_dot_nt(qh[n], kb) + cq[n] - ck[h] for n, (h, qq) in enumerate(chains)]
            if diagonal:
                s = [jnp.where(qq * th + qidx >= kidx, s[n], NEG) for n, (h, qq) in enumerate(chains)]
            m_new = [jnp.maximum(m[n], jnp.max(s[n], axis=1, keepdims=True)) for n in range(4)]
            p = [jnp.exp(s[n] - m_new[n]) for n in range(4)]
            alpha = [jnp.exp(m[n] - m_new[n]) for n in range(4)]
            l = [l[n] * alpha[n] + jnp.sum(p[n], axis=1, keepdims=True) for n in range(4)]
            pv = [_dot(p[n].astype(BF16), vh[h]) for n, (h, qq) in enumerate(chains)]
            acc = [acc[qq] * jnp.where(in_head[0], alpha[qq], alpha[2 + qq]) + pv[qq] + pv[2 + qq] for qq in (0, 1)]
            return tuple(m_new), tuple(l), tuple(acc)

        init = (tuple(jnp.full((th, 1), NEG, F32) for _ in chains), tuple(jnp.zeros((th, 1), F32) for _ in chains),
                tuple(jnp.zeros((th, LANES), F32) for _ in (0, 1)))
        carry = lax.fori_loop(0, i, functools.partial(kv_step, diagonal=False), init)
        m, l, acc = kv_step(i, carry, True)
        for qq in (0, 1):
            o = acc[qq] / jnp.where(in_head[0], l[qq], l[2 + qq])
            g = g_ref[rows[qq], :]
            o_ref[rows[qq], :] = o
            lse_ref[rows[qq], :] = jnp.where(in_head[0], m[qq] + jnp.log(l[qq]), m[2 + qq] + jnp.log(l[2 + qq]))
            y_ref[rows[qq], :] = (o * (g * jax.nn.sigmoid(g))).astype(y_ref.dtype)

    npw = fw // LANES
    blk = lambda col0: pl.BlockSpec((tq, LANES), lambda hp, i: (i, col0 + hp))
    res = lambda col0: pl.BlockSpec((t, LANES), lambda hp, i: (0, col0 + hp))
    out_blk = pl.BlockSpec((tq, LANES), lambda hp, i: (i, hp))
    return pl.pallas_call(
        body, name="fox_attn_fwd", grid=(n_pair, n_q),
        in_specs=[blk(0), res(npw), res(2 * npw), blk(3 * npw),
                  pl.BlockSpec((tq, LANES), lambda hp, i: (i, 0)),
                  pl.BlockSpec((1, n_q, 2, tq), lambda hp, i: (hp, 0, 0, 0))],
        out_specs=[out_blk, out_blk, out_blk],
        out_shape=[jax.ShapeDtypeStruct((t, fw), F32), jax.ShapeDtypeStruct((t, fw), F32),
                   jax.ShapeDtypeStruct((t, fw), BF16)],
        compiler_params=_params("arbitrary", "arbitrary"),
    )(zf, zf, zf, zf, c, ct)


def _fox_bwd(zf, do, c, ct, lse_r, dd_r, *, fw):
    t = zf.shape[0]
    tq = _fox_tile(t)
    n_pair, n_q = fw // LANES, t // tq
    scale = HEAD ** -0.5

    def body(q_ref, k_ref, v_ref, do_ref, c_ref, ct_ref, lse_ref, dd_ref,
             dq_ref, dk_ref, dv_ref, dcq_ref, dck_ref, dq_acc, dcq_acc):
        hp, j = pl.program_id(0), pl.program_id(1)

        @pl.when(j == 0)
        def _():
            dq_acc[...] = jnp.zeros_like(dq_acc)
            dcq_acc[...] = jnp.zeros_like(dcq_acc)

        lane = lax.broadcasted_iota(jnp.int32, (1, LANES), 1)
        in_head = (lane < HEAD, lane >= HEAD)
        kb = k_ref[...]
        kh = [jnp.where(m, kb, 0.0).astype(BF16) for m in in_head]
        vb = v_ref[...].astype(BF16)
        c_k = c_ref[...]
        ck = [_lane_pick(c_k, 2 * hp + h) for h in (0, 1)]
        kidx = lax.broadcasted_iota(jnp.int32, (tq, tq), 0)
        qidx = lax.broadcasted_iota(jnp.int32, (tq, tq), 1)

        def q_step(i, carry, diagonal):
            dk, dv, dck = carry
            qs = pl.ds(pl.multiple_of(i * tq, tq), tq)
            qf = q_ref[qs, :] * scale
            dof = do_ref[qs, :]
            qh = [jnp.where(m, qf, 0.0).astype(BF16) for m in in_head]
            doh = [jnp.where(m, dof, 0.0).astype(BF16) for m in in_head]
            row = lambda ref, h: ref[0, i, pl.ds(h, 1), :]
            st = [_dot_nt(kh[h], qh[h]) + row(ct_ref, h) - ck[h] for h in (0, 1)]
            p = [jnp.exp(st[h] - row(lse_ref, h)) for h in (0, 1)]
            if diagonal:
                p = [jnp.where(kidx <= qidx, p[h], 0.0) for h in (0, 1)]
            dst = [p[h] * (_dot_nt(vb, doh[h]) - row(dd_ref, h)) for h in (0, 1)]
            p16 = [x.astype(BF16) for x in p]
            ds16 = [x.astype(BF16) for x in dst]
            dv = dv + _dot(p16[0], doh[0]) + _dot(p16[1], doh[1])
            dk = dk + _dot(ds16[0], qh[0]) + _dot(ds16[1], qh[1])
            dq_acc[qs, :] += _dot_tn(ds16[0], kh[0]) + _dot_tn(ds16[1], kh[1])
            for h in (0, 1):
                dcq_acc[i, pl.ds(h, 1), :] += jnp.sum(dst[h], axis=0, keepdims=True)
            dck = tuple(dck[h] - jnp.sum(dst[h], axis=1, keepdims=True) for h in (0, 1))
            return dk, dv, dck

        zero = jnp.zeros((tq, LANES), F32)
        carry = q_step(j, (zero, zero, (jnp.zeros((tq, 1), F32),) * 2), True)
        dk, dv, dck = lax.fori_loop(j + 1, n_q, functools.partial(q_step, diagonal=False), carry)
        dk_ref[...] = dk.astype(dk_ref.dtype)
        dv_ref[...] = dv.astype(dv_ref.dtype)
        dck_ref[...] = jnp.where(lane == 0, dck[0], jnp.where(lane == 1, dck[1], 0.0))

        @pl.when(j == n_q - 1)
        def _():
            dq_ref[...] = (dq_acc[...] * scale).astype(dq_ref.dtype)
            dcq_ref[0] = dcq_acc[...]

    npw = fw // LANES
    res_z = lambda col0: pl.BlockSpec((t, LANES), lambda hp, j: (0, col0 + hp))
    blk_z = lambda col0: pl.BlockSpec((tq, LANES), lambda hp, j: (j, col0 + hp))
    res = pl.BlockSpec((t, LANES), lambda hp, j: (0, hp))
    blk = pl.BlockSpec((tq, LANES), lambda hp, j: (j, hp))
    rows = pl.BlockSpec((1, n_q, 2, tq), lambda hp, j: (hp, 0, 0, 0))
    return pl.pallas_call(
        body, name="fox_attn_bwd", grid=(n_pair, n_q),
        in_specs=[res_z(0), blk_z(npw), blk_z(2 * npw), res, pl.BlockSpec((tq, LANES), lambda hp, j: (j, 0)),
                  rows, rows, rows],
        out_specs=[res, blk, blk, rows, blk],
        out_shape=[jax.ShapeDtypeStruct((t, fw), BF16), jax.ShapeDtypeStruct((t, fw), BF16),
                   jax.ShapeDtypeStruct((t, fw), BF16), jax.ShapeDtypeStruct((n_pair, n_q, 2, tq), F32),
                   jax.ShapeDtypeStruct((t, fw), F32)],
        scratch_shapes=[pltpu.VMEM((t, LANES), F32), pltpu.VMEM((n_q, 2, tq), F32)],
        compiler_params=_params("arbitrary", "arbitrary"),
    )(zf, zf, zf, do, c, ct, lse_r, dd_r)


def _adamw_math(w, g, m, v):
    m = ADAM_B1 * m + (1.0 - ADAM_B1) * g
    v = ADAM_B2 * v + (1.0 - ADAM_B2) * jnp.square(g)
    m_hat = m / (1.0 - ADAM_B1 ** ADAM_STEP)
    v_hat = v / (1.0 - ADAM_B2 ** ADAM_STEP)
    delta = -ADAM_LR * (m_hat / (jnp.sqrt(v_hat) + ADAM_EPS) + ADAM_WD * w)
    return delta, m, v


def _adamw(name, w, g, m, v):
    lead, rows, cols = w.shape
    if lead == 1 and cols % LANES:
        outs = _adamw(name, *[jnp.transpose(a, (2, 0, 1)) for a in (w, g, m, v)])
        return [jnp.transpose(o, (1, 2, 0)) for o in outs]
    if lead == 1:
        tile = _tile(rows, (128, 64, 32, 16, 8))
        spec, steps = pl.BlockSpec((1, tile, cols), lambda i: (0, i, 0)), rows // tile
    else:
        tile = _div_tile(lead, 256, 1)
        spec, steps = pl.BlockSpec((tile, rows, cols), lambda i: (i, 0, 0)), lead // tile

    def body(w_ref, g_ref, m_ref, v_ref, d_ref, mo_ref, vo_ref):
        d_ref[...], mo_ref[...], vo_ref[...] = _adamw_math(w_ref[...], g_ref[...], m_ref[...], v_ref[...])

    return pl.pallas_call(
        body, name=name, grid=(steps,), in_specs=[spec] * 4, out_specs=[spec] * 3,
        out_shape=[jax.ShapeDtypeStruct(w.shape, F32)] * 3, compiler_params=_params("parallel"),
    )(w, g, m, v)


def _place():
    return lax.axis_index("x"), lax.axis_index("y"), lax.axis_index("c")


def _other_chips(x, y):
    return [(1 - x, y), (x, 1 - y), (1 - x, 1 - y)]


HBM_SPEC = pl.BlockSpec(memory_space=pltpu.HBM)
SEM_SPEC = pl.BlockSpec(memory_space=pltpu.SEMAPHORE)


def _all_gather_shards(slabs):
    n = len(slabs)

    def body(*refs):
        src_refs, out_refs, send_sems, recv_sems = refs[:n], refs[n:2 * n], refs[2 * n], refs[2 * n + 1]
        x, y, c = _place()
        me = 2 * x + y
        sibling = (x, y, 1 - c)
        chips = _other_chips(x, y)
        first, passed, waits = [], [], []
        for g, (src_ref, out_ref) in enumerate(zip(src_refs, out_refs)):
            rh = src_ref.shape[0] // 2

            def part(chip, half, out_ref=out_ref, rh=rh):
                return out_ref.at[chip, pl.ds(half * rh, rh), :]

            def copy(k, src, dst, to, g=g):
                return pltpu.make_async_remote_copy(src_ref=src, dst_ref=dst, send_sem=send_sems.at[6 * g + k],
                                                    recv_sem=recv_sems.at[6 * g + k], device_id=to, device_id_type=MESH)

            first += [copy(j, src_ref.at[pl.ds(c * rh, rh), :], part(me, c), (px, py, c))
                      for j, (px, py) in enumerate(chips)]
            for j, (px, py) in enumerate(chips):
                theirs = part(2 * px + py, c)
                passed.append((copy(j, theirs, theirs, sibling), copy(3 + j, theirs, theirs, sibling)))
                other = part(2 * px + py, 1 - c)
                waits.append(copy(3 + j, other, other, sibling))
        for cp in first:
            cp.start()
        for landed, forward in passed:
            landed.wait_recv()
            forward.start()
        for cp in waits:
            cp.wait_recv()
        for cp in first + [fwd for _, fwd in passed]:
            cp.wait_send()

    return pl.pallas_call(
        body, name="weights_all_gather", in_specs=[HBM_SPEC] * n, out_specs=[HBM_SPEC] * n,
        out_shape=[jax.ShapeDtypeStruct((N_SHARD,) + a.shape, a.dtype) for a in slabs],
        scratch_shapes=[pltpu.SemaphoreType.DMA((6 * n,)), pltpu.SemaphoreType.DMA((6 * n,))],
    )(*slabs)


def _gather_ici_copies(src_refs, out_refs, send_sems, recv_sems):
    x, y, c = _place()
    me = 2 * x + y
    copies = []
    for g, (src_ref, out_ref) in enumerate(zip(src_refs, out_refs)):
        rh = src_ref.shape[0] // 2
        copies += [pltpu.make_async_remote_copy(
            src_ref=src_ref.at[pl.ds(c * rh, rh), :], dst_ref=out_ref.at[me, pl.ds(c * rh, rh), :],
            send_sem=send_sems.at[3 * g + j], recv_sem=recv_sems.at[3 * g + j], device_id=(px, py, c),
            device_id_type=MESH) for j, (px, py) in enumerate(_other_chips(x, y))]
    return copies


def _gather_start(slabs):
    n = len(slabs)

    def body(*refs):
        for cp in _gather_ici_copies(refs[:n], refs[n:2 * n], refs[2 * n], refs[2 * n + 1]):
            cp.start()
        refs[-1][...] = jnp.zeros_like(refs[-1])

    hbm = lambda a: pltpu.with_memory_space_constraint(a, pltpu.HBM)
    lands = [(N_SHARD,) + a.shape for a in slabs]
    out = pl.pallas_call(
        body, name="weights_gather_start",
        out_shape=(pltpu.SemaphoreType.DMA((3 * n,)), pltpu.SemaphoreType.DMA((3 * n,)),
                   *[pltpu.HBM(a.shape, a.dtype) for a in slabs],
                   *[pltpu.HBM(sh, a.dtype) for sh, a in zip(lands, slabs)], jax.ShapeDtypeStruct((8, LANES), F32)),
        in_specs=[HBM_SPEC] * (2 * n),
        out_specs=(SEM_SPEC, SEM_SPEC, *[HBM_SPEC] * (2 * n), pl.BlockSpec(memory_space=pltpu.VMEM)),
        input_output_aliases={i: 2 + i for i in range(2 * n)},
        compiler_params=pltpu.CompilerParams(has_side_effects=pltpu.SideEffectType.DATAFLOW_SIDE_EFFECTING),
    )(*[hbm(a) for a in slabs], *[hbm(lax.empty(sh, a.dtype)) for sh, a in zip(lands, slabs)])
    return out[0], out[1], list(out[2:2 + n]), list(out[2 + n:2 + 2 * n]), out[-1]


def _gather_wait(send_sems, recv_sems, slabs, landed, after):
    n = len(slabs)

    def body(*refs):
        for cp in _gather_ici_copies(refs[:n], refs[n:2 * n], refs[2 * n], refs[2 * n + 1]):
            cp.wait_send()
            cp.wait_recv()

    out = pl.pallas_call(
        body, name="weights_gather_wait",
        out_shape=[pltpu.HBM(a.shape, a.dtype) for a in slabs + landed],
        in_specs=[HBM_SPEC] * (2 * n) + [SEM_SPEC, SEM_SPEC, pl.BlockSpec(memory_space=pl.ANY)],
        out_specs=[HBM_SPEC] * (2 * n), input_output_aliases={i: i for i in range(2 * n)},
        compiler_params=pltpu.CompilerParams(has_side_effects=pltpu.SideEffectType.DATAFLOW_SIDE_EFFECTING),
    )(*slabs, *landed, send_sems, recv_sems, after)
    return list(out[n:])


def _gather_forward(gathered):
    n = len(gathered)

    def body(*refs):
        in_refs, out_refs, send_sems, recv_sems = refs[:n], refs[n:2 * n], refs[2 * n], refs[2 * n + 1]
        x, y, c = _place()

        def copy(g, j, chip, half):
            rh = in_refs[g].shape[1] // 2
            return pltpu.make_async_remote_copy(
                src_ref=in_refs[g].at[chip, pl.ds(half * rh, rh), :], dst_ref=out_refs[g].at[chip, pl.ds(half * rh, rh), :],
                send_sem=send_sems.at[3 * g + j], recv_sem=recv_sems.at[3 * g + j], device_id=(x, y, 1 - c),
                device_id_type=MESH)

        chips = [2 * px + py for px, py in _other_chips(x, y)]
        for g in range(n):
            for j, chip in enumerate(chips):
                copy(g, j, chip, c).start()
        for g in range(n):
            for j, chip in enumerate(chips):
                copy(g, j, chip, c).wait_send()
                copy(g, j, chip, 1 - c).wait_recv()

    return pl.pallas_call(
        body, name="weights_gather_forward", in_specs=[HBM_SPEC] * n, out_specs=[HBM_SPEC] * n,
        out_shape=[jax.ShapeDtypeStruct(a.shape, a.dtype) for a in gathered],
        input_output_aliases={g: g for g in range(n)},
        scratch_shapes=[pltpu.SemaphoreType.DMA((3 * n,)), pltpu.SemaphoreType.DMA((3 * n,))],
    )(*gathered)


def _chip_index():
    return jnp.reshape(2 * lax.axis_index("x") + lax.axis_index("y"), (1,)).astype(jnp.int32)


def _place_own_shard(name, gathered, slab):
    rows, width = slab.shape
    tile = _div_tile(rows, 256, 16)

    def body(me_ref, s_ref, g_ref, o_ref):
        o_ref[0] = s_ref[...]

    return pl.pallas_call(
        body, name=name,
        grid_spec=pltpu.PrefetchScalarGridSpec(
            num_scalar_prefetch=1, grid=(rows // tile,),
            in_specs=[pl.BlockSpec((tile, width), lambda i, me: (i, 0)), pl.BlockSpec(memory_space=pl.ANY)],
            out_specs=pl.BlockSpec((1, tile, width), lambda i, me: (me[0], i, 0))),
        out_shape=jax.ShapeDtypeStruct(gathered.shape, gathered.dtype), input_output_aliases={2: 0},
        compiler_params=_params("parallel"),
    )(_chip_index(), slab, gathered)


def _sibling_exchange(gs):
    n = len(gs)

    def body(*refs):
        g_refs, out_refs, send_sems, recv_sems = refs[:n], refs[n:2 * n], refs[2 * n], refs[2 * n + 1]
        x, y, c = _place()
        copies = [pltpu.make_async_remote_copy(
            src_ref=g_ref.at[s, 1 - c], dst_ref=out_ref.at[s], send_sem=send_sems.at[N_SHARD * g + s],
            recv_sem=recv_sems.at[N_SHARD * g + s], device_id=(x, y, 1 - c), device_id_type=MESH)
            for g, (g_ref, out_ref) in enumerate(zip(g_refs, out_refs)) for s in range(N_SHARD)]
        for cp in copies:
            cp.start()
        for cp in copies:
            cp.wait()

    return pl.pallas_call(
        body, name="grad_sibling_exchange", in_specs=[HBM_SPEC] * n, out_specs=[HBM_SPEC] * n,
        out_shape=[jax.ShapeDtypeStruct((N_SHARD,) + g.shape[2:], g.dtype) for g in gs],
        scratch_shapes=[pltpu.SemaphoreType.DMA((N_SHARD * n,)), pltpu.SemaphoreType.DMA((N_SHARD * n,))],
    )(*gs)


def _add_sibling(name, g, got):
    _, _, rh, width = g.shape
    tile = _div_tile(rh, 256, 16)
    c_arr = jnp.reshape(lax.axis_index("c"), (1,)).astype(jnp.int32)

    def body(c_ref, a_ref, b_ref, o_ref):
        o_ref[...] = (a_ref[0] + b_ref[...]).astype(o_ref.dtype)

    return pl.pallas_call(
        body, name=name,
        grid_spec=pltpu.PrefetchScalarGridSpec(
            num_scalar_prefetch=1, grid=(N_SHARD, rh // tile),
            in_specs=[pl.BlockSpec((1, 1, tile, width), lambda s, i, c: (s, c[0], i, 0)),
                      pl.BlockSpec((1, tile, width), lambda s, i, c: (s, i, 0))],
            out_specs=pl.BlockSpec((1, tile, width), lambda s, i, c: (s, i, 0))),
        out_shape=jax.ShapeDtypeStruct((N_SHARD, rh, width), BF16),
        compiler_params=_params("parallel", "parallel"),
    )(c_arr, g, got)


def _exchange_copies(p_refs, land_refs, send_sems, recv_sems):
    x, y, c = _place()
    me = 2 * x + y
    return [pltpu.make_async_remote_copy(
        src_ref=p_ref.at[2 * px + py], dst_ref=land_ref.at[me], send_sem=send_sems.at[3 * g + j],
        recv_sem=recv_sems.at[3 * g + j], device_id=(px, py, c), device_id_type=MESH)
        for g, (p_ref, land_ref) in enumerate(zip(p_refs, land_refs)) for j, (px, py) in enumerate(_other_chips(x, y))]


def _chip_exchange_start(ps):
    n = len(ps)

    def body(*refs):
        for cp in _exchange_copies(refs[:n], refs[n:2 * n], refs[2 * n], refs[2 * n + 1]):
            cp.start()
        refs[-1][...] = jnp.zeros_like(refs[-1])

    hbm = lambda a: pltpu.with_memory_space_constraint(a, pltpu.HBM)
    out = pl.pallas_call(
        body, name="grad_chip_exchange_start",
        out_shape=(pltpu.SemaphoreType.DMA((3 * n,)), pltpu.SemaphoreType.DMA((3 * n,)),
                   *[pltpu.HBM(a.shape, a.dtype) for a in ps], *[pltpu.HBM(a.shape, a.dtype) for a in ps],
                   jax.ShapeDtypeStruct((8, LANES), F32)),
        in_specs=[HBM_SPEC] * (2 * n),
        out_specs=(SEM_SPEC, SEM_SPEC, *[HBM_SPEC] * (2 * n), pl.BlockSpec(memory_space=pltpu.VMEM)),
        input_output_aliases={i: 2 + i for i in range(2 * n)},
        compiler_params=pltpu.CompilerParams(has_side_effects=pltpu.SideEffectType.DATAFLOW_SIDE_EFFECTING),
    )(*[hbm(a) for a in ps], *[hbm(lax.empty(a.shape, a.dtype)) for a in ps])
    return out[0], out[1], list(out[2:2 + n]), list(out[2 + n:2 + 2 * n]), out[-1]


def _chip_exchange_wait(send_sems, recv_sems, ps, landed, after):
    n = len(ps)

    def body(*refs):
        for cp in _exchange_copies(refs[:n], refs[n:2 * n], refs[2 * n], refs[2 * n + 1]):
            cp.wait_send()
            cp.wait_recv()

    out = pl.pallas_call(
        body, name="grad_chip_exchange_wait",
        out_shape=[pltpu.HBM(a.shape, a.dtype) for a in ps + landed],
        in_specs=[HBM_SPEC] * (2 * n) + [SEM_SPEC, SEM_SPEC, pl.BlockSpec(memory_space=pl.ANY)],
        out_specs=[HBM_SPEC] * (2 * n), input_output_aliases={i: i for i in range(2 * n)},
        compiler_params=pltpu.CompilerParams(has_side_effects=pltpu.SideEffectType.DATAFLOW_SIDE_EFFECTING),
    )(*ps, *landed, send_sems, recv_sems, after)
    return list(out[:n]), list(out[n:])


def _sum_chips(name, p, got):
    _, rh, width = p.shape
    tile = _div_tile(rh, 256, 16)
    n_t = rh // tile
    place = jnp.stack([2 * lax.axis_index("x") + lax.axis_index("y"), lax.axis_index("c")]).astype(jnp.int32)

    def body(pl_ref, own_ref, r0, r1, r2, r3, o_ref):
        me = pl_ref[0]
        own = own_ref[0].astype(F32)
        t = [jnp.where(me == s, own, r[0].astype(F32)) for s, r in enumerate((r0, r1, r2, r3))]
        o_ref[...] = ((t[0] + t[1]) + t[2]) + t[3]

    def slot(s):
        return pl.BlockSpec((1, tile, width), lambda i, pc: (jnp.where(pc[0] == s, (s + 1) % N_SHARD, s), i, 0))

    return pl.pallas_call(
        body, name=name,
        grid_spec=pltpu.PrefetchScalarGridSpec(
            num_scalar_prefetch=1, grid=(n_t,),
            in_specs=[pl.BlockSpec((1, tile, width), lambda i, pc: (pc[0], i, 0))] + [slot(s) for s in range(N_SHARD)],
            out_specs=pl.BlockSpec((tile, width), lambda i, pc: (pc[1] * n_t + i, 0))),
        out_shape=jax.ShapeDtypeStruct((2 * rh, width), F32), compiler_params=_params("parallel"),
    )(place, p, got, got, got, got)


def _join_halves(fulls):
    n = len(fulls)

    def body(*refs):
        f_refs, out_refs, send_sems, recv_sems = refs[:n], refs[n:2 * n], refs[2 * n], refs[2 * n + 1]
        x, y, c = _place()

        def copy(g, half):
            rh = f_refs[g].shape[0] // 2
            return pltpu.make_async_remote_copy(
                src_ref=f_refs[g].at[pl.ds(half * rh, rh), :], dst_ref=out_refs[g].at[pl.ds(half * rh, rh), :],
                send_sem=send_sems.at[g], recv_sem=recv_sems.at[g], device_id=(x, y, 1 - c), device_id_type=MESH)

        for g in range(n):
            copy(g, c).start()
        for g in range(n):
            copy(g, c).wait_send()
            copy(g, 1 - c).wait_recv()

    return pl.pallas_call(
        body, name="grad_join_halves", in_specs=[HBM_SPEC] * n, out_specs=[HBM_SPEC] * n,
        out_shape=[jax.ShapeDtypeStruct(f.shape, f.dtype) for f in fulls],
        input_output_aliases={g: g for g in range(n)},
        scratch_shapes=[pltpu.SemaphoreType.DMA((n,)), pltpu.SemaphoreType.DMA((n,))],
    )(*fulls)


def _reduce_scatter_start(gs):
    gs = [g.reshape(N_SHARD, 2, g.shape[1] // 2, g.shape[2]) for g in gs]
    got = _sibling_exchange(gs)
    chip_sums = [_add_sibling(f"grad_add_sibling_{i}", g, r) for i, (g, r) in enumerate(zip(gs, got))]
    *state, token = _chip_exchange_start(chip_sums)
    return state, token


def _reduce_scatter_end(state, after):
    chip_sums, landed = _chip_exchange_wait(*state, after)
    return _join_halves([_sum_chips(f"grad_sum_chips_{i}", p, r) for i, (p, r) in enumerate(zip(chip_sums, landed))])


def _all_reduce_small(v):
    rows = v.shape[0]

    def body(v_ref, out_ref, gather, send_sems, recv_sems):
        x, y, c = _place()
        gather[4 * x + 2 * y + c] = v_ref[...]
        flips = [(dx, dy, dc) for dx in (0, 1) for dy in (0, 1) for dc in (0, 1)][1:]
        peers = [((x + dx) % 2, (y + dy) % 2, (c + dc) % 2) for dx, dy, dc in flips]
        copies = [pltpu.make_async_remote_copy(
            src_ref=v_ref, dst_ref=gather.at[4 * x + 2 * y + c], send_sem=send_sems.at[j], recv_sem=recv_sems.at[j],
            device_id=peer, device_id_type=MESH) for j, peer in enumerate(peers)]
        for cp in copies:
            cp.start()
        for j, (px, py, pc) in enumerate(peers):
            pltpu.make_async_remote_copy(
                src_ref=v_ref, dst_ref=gather.at[4 * px + 2 * py + pc], send_sem=send_sems.at[j],
                recv_sem=recv_sems.at[j], device_id=(px, py, pc), device_id_type=MESH).wait_recv()
        for cp in copies:
            cp.wait_send()
        acc = gather[0]
        for d in range(1, 8):
            acc = acc + gather[d]
        out_ref[...] = acc

    vm = pl.BlockSpec(memory_space=pltpu.VMEM)
    return pl.pallas_call(
        body, name="small_grads_all_reduce", in_specs=[vm], out_specs=vm,
        out_shape=jax.ShapeDtypeStruct(v.shape, F32),
        scratch_shapes=[pltpu.VMEM((8, rows, LANES), F32), pltpu.SemaphoreType.DMA((7,)), pltpu.SemaphoreType.DMA((7,))],
    )(v)


def _pad_lanes(v):
    v = v.reshape(1, -1)
    return jnp.pad(v, ((0, 0), (0, -v.shape[1] % LANES)))


def _pack_small(vs, rows):
    flat = jnp.concatenate([_pad_lanes(v) for v in vs], axis=1)
    return jnp.pad(flat, ((0, 0), (0, rows * LANES - flat.shape[1]))).reshape(rows, LANES)


def _unpack_small(packed, shapes):
    flat = packed.reshape(-1)
    out, off = [], 0
    for s in shapes:
        n = 1
        for d in s:
            n *= d
        out.append(flat[off:off + n].reshape(s))
        off += n + (-n % LANES)
    return out


BIG = ("w_in", "rw_w_lora_up", "rw_a_lora_up", "w_up_rwkv", "w_up_fox", "w_out", "ple_proj", "ple_gate_w")
ROW_SHARDED = ("w_out", "ple_gate_w")
FIRST_NEEDED = ("w_in", "rw_w_lora_up", "rw_a_lora_up")
SMALL = ("norm_g", "rw_shift_mu", "rw_w0", "rw_a0", "rw_k_k", "rw_k_a", "rw_r_k", "rw_ln_g", "rw_ln_b", "fox_b_f",
         "ple_norm_g", "final_norm_g")
WEIGHTS = ("norm_g", "w_in", "rw_shift_mu", "rw_w0", "rw_w_lora_up", "rw_a0", "rw_a_lora_up", "rw_k_k", "rw_k_a",
           "rw_r_k", "rw_ln_g", "rw_ln_b", "fox_b_f", "w_up_rwkv", "w_up_fox", "w_out", "ple_proj", "ple_gate_w",
           "ple_norm_g", "final_norm_g")


def kernel(x, p, norm_g, w_in, rw_shift_mu, rw_w0, rw_w_lora_up, rw_a0, rw_a_lora_up, rw_k_k, rw_k_a, rw_r_k, rw_ln_g, rw_ln_b, fox_b_f, w_up_rwkv, w_up_fox, w_out, ple_proj, ple_gate_w, ple_norm_g, final_norm_g, loss_target, m_norm_g, m_w_in, m_rw_shift_mu, m_rw_w0, m_rw_w_lora_up, m_rw_a0, m_rw_a_lora_up, m_rw_k_k, m_rw_k_a, m_rw_r_k, m_rw_ln_g, m_rw_ln_b, m_fox_b_f, m_w_up_rwkv, m_w_up_fox, m_w_out, m_ple_proj, m_ple_gate_w, m_ple_norm_g, m_final_norm_g, v_norm_g, v_w_in, v_rw_shift_mu, v_rw_w0, v_rw_w_lora_up, v_rw_a0, v_rw_a_lora_up, v_rw_k_k, v_rw_k_a, v_rw_r_k, v_rw_ln_g, v_rw_ln_b, v_fox_b_f, v_w_up_rwkv, v_w_up_fox, v_w_out, v_ple_proj, v_ple_gate_w, v_ple_norm_g, v_final_norm_g):
    wts = dict(norm_g=norm_g, w_in=w_in, rw_shift_mu=rw_shift_mu, rw_w0=rw_w0, rw_w_lora_up=rw_w_lora_up, rw_a0=rw_a0,
               rw_a_lora_up=rw_a_lora_up, rw_k_k=rw_k_k, rw_k_a=rw_k_a, rw_r_k=rw_r_k, rw_ln_g=rw_ln_g, rw_ln_b=rw_ln_b,
               fox_b_f=fox_b_f, w_up_rwkv=w_up_rwkv, w_up_fox=w_up_fox, w_out=w_out, ple_proj=ple_proj,
               ple_gate_w=ple_gate_w, ple_norm_g=ple_norm_g, final_norm_g=final_norm_g)
    mom = dict(norm_g=m_norm_g, w_in=m_w_in, rw_shift_mu=m_rw_shift_mu, rw_w0=m_rw_w0, rw_w_lora_up=m_rw_w_lora_up,
               rw_a0=m_rw_a0, rw_a_lora_up=m_rw_a_lora_up, rw_k_k=m_rw_k_k, rw_k_a=m_rw_k_a, rw_r_k=m_rw_r_k,
               rw_ln_g=m_rw_ln_g, rw_ln_b=m_rw_ln_b, fox_b_f=m_fox_b_f, w_up_rwkv=m_w_up_rwkv, w_up_fox=m_w_up_fox,
               w_out=m_w_out, ple_proj=m_ple_proj, ple_gate_w=m_ple_gate_w, ple_norm_g=m_ple_norm_g,
               final_norm_g=m_final_norm_g)
    vel = dict(norm_g=v_norm_g, w_in=v_w_in, rw_shift_mu=v_rw_shift_mu, rw_w0=v_rw_w0, rw_w_lora_up=v_rw_w_lora_up,
               rw_a0=v_rw_a0, rw_a_lora_up=v_rw_a_lora_up, rw_k_k=v_rw_k_k, rw_k_a=v_rw_k_a, rw_r_k=v_rw_r_k,
               rw_ln_g=v_rw_ln_g, rw_ln_b=v_rw_ln_b, fox_b_f=v_fox_b_f, w_up_rwkv=v_w_up_rwkv, w_up_fox=v_w_up_fox,
               w_out=v_w_out, ple_proj=v_ple_proj, ple_gate_w=v_ple_gate_w, ple_norm_g=v_ple_norm_g,
               final_norm_g=v_final_norm_g)

    t, d = x.shape[1], x.shape[2]
    cw = rw_w0.shape[1]
    lr = rw_w_lora_up.shape[1]
    fh = fox_b_f.shape[1]
    fw = fh * HEAD
    rw_cols = 4 * cw + 2 * lr
    fox_cols = 4 * fw + fh
    assert 2 * lr == LANES and cw % LANES == 0 and fw % LANES == 0 and fh <= LANES
    xs = x[0]
    ps = p[0, 0]
    tgt = loss_target[0]

    groups = {}
    for n in BIG:
        groups.setdefault(wts[n].shape[2], []).append(n)
    groups = list(groups.values())
    slabs16 = []
    for gi, names in enumerate(groups):
        slab = jnp.concatenate([wts[n][0] for n in names], axis=0)
        rows, width = slab.shape
        tile_c = _div_tile(rows, 256, 32)
        slabs16 += _rows(f"weights_to_bf16_{gi}", lambda i, a: (a,), rows, tile_c, [(slab, _row_spec(tile_c, width))],
                         [(width, BF16)])
    first = [gi for gi, names in enumerate(groups) if not set(names).isdisjoint(FIRST_NEEDED)]
    later = [gi for gi in range(len(groups)) if gi not in first]
    gathered = dict(zip(first, _all_gather_shards([slabs16[gi] for gi in first])))
    *gather_state, gather_token = _gather_start([slabs16[gi] for gi in later])
    full = {}

    def assemble(gis):
        for gi in gis:
            g = _place_own_shard(f"weights_place_own_{gi}", gathered[gi], slabs16[gi])
            off = 0
            for n in groups[gi]:
                r = wts[n].shape[1]
                part = g[:, off:off + r, :]
                full[n] = (part.reshape(N_SHARD * r, -1) if n in ROW_SHARDED
                           else jnp.concatenate([part[s] for s in range(N_SHARD)], axis=1))
                off += r

    assemble(first)
    w_rw = full["w_in"][:, :rw_cols]
    w_fox = jnp.pad(full["w_in"][:, rw_cols:rw_cols + fox_cols], ((0, 0), (0, LANES - fh)))
    w_gate = full["w_in"][:, rw_cols + fox_cols:]
    wup_pad = jnp.pad(full["rw_w_lora_up"], ((0, lr), (0, 0)))
    aup_pad = jnp.pad(full["rw_a_lora_up"], ((lr, 0), (0, 0)))
    b_pad = _pad_lanes(fox_b_f)
    r_k_row = rw_r_k.reshape(1, cw)
    gf_row = final_norm_g.reshape(1, d)

    tile = _tile(t, (256, 128, 64, 32, 16, 8))
    tile_s = _tile(t, (128, 64, 32, 16, 8))
    n_s = t // tile_s
    full2 = lambda a: (a, _full_spec(a.shape))

    (h,) = _rows("norm1", lambda i, a, g: (_rms(a, g),), t, tile, [(xs, _row_spec(tile, d)), full2(norm_g)], [(d, BF16)])
    z_rw = _matmul("proj_rw", h, w_rw, after=gather_token)
    z_fox = _matmul("proj_fox", h, w_fox, after=z_rw[:8, :LANES])
    z_gate = _matmul("proj_gate", h, w_gate, after=z_fox[:8, :LANES])
    gathered.update(zip(later, _gather_forward(_gather_wait(*gather_state, z_gate))))
    assemble(later)

    pre_consts = [full2(rw_shift_mu), full2(rw_w0), full2(rw_a0), full2(wup_pad), full2(aup_pad), full2(rw_k_k),
                  full2(rw_k_a)]

    def pre_fwd(i, z, prev8, *consts):
        return _rw_pre(z, _shifted(i, z, prev8), *consts, cw=cw)

    r_, w_, k_, v_, kk_, a_, g_ = _rows(
        "rwkv_pre", pre_fwd, t, tile_s,
        [(z_rw, _row_spec(tile_s, rw_cols)), (z_rw, _prev_rows_spec(tile_s, rw_cols))] + pre_consts, [(cw, F32)] * 7)
    y_scan, states, tinvs = _scan_fwd(r_, w_, k_, v_, kk_, a_)
    post_consts = [full2(rw_ln_g), full2(rw_ln_b), full2(r_k_row)]
    post_rows = lambda *arrs: [(a, _row_spec(tile_s, cw)) for a in arrs]
    (y_rw,) = _rows("rwkv_post", lambda i, *a: (_rw_post(*a, cw=cw),), t, tile_s,
                    post_rows(y_scan, r_, k_, v_, g_) + post_consts, [(cw, BF16)])

    c_fox = _fox_cumsum(z_fox, b_pad, fw=fw, fh=fh)
    tq = _fox_tile(t)
    n_pair_f = fw // LANES
    head_rows = lambda a: a.T.reshape(n_pair_f, 2, t // tq, tq).transpose(0, 2, 1, 3)
    head_cols = lambda a: a.transpose(0, 2, 1, 3).reshape(fh, t).T
    ct_fox = head_rows(c_fox[:, :fh])
    o_fox, lse_fox, y_fox = _fox_fwd(z_fox, c_fox, ct_fox, fw=fw)

    u_rw = _matmul("up_rwkv", y_rw, full["w_up_rwkv"])
    u_fox = _matmul("up_fox", y_fox, full["w_up_fox"])
    (merged,) = _rows("merge", lambda i, zg, a, b: (_merge(zg, a, b, d=d),), t, tile,
                      [(z_gate, _row_spec(tile, 2 * d)), (u_rw, _row_spec(tile, d)), (u_fox, _row_spec(tile, d))],
                      [(d, BF16)])
    x1 = _matmul("out_proj", merged, full["w_out"], add=xs)
    (n2,) = _rows("norm2", lambda i, a, g: (_rms(a, g),), t, tile, [(x1, _row_spec(tile, d)), full2(ple_norm_g)],
                  [(d, BF16)])
    gl = _matmul("ple_gate", n2, full["ple_gate_w"])
    ple = _matmul("ple_proj", ps, full["ple_proj"])

    def head_bwd(i, x1_t, ple_t, gl_t, gf, tg):
        loss, vjp = jax.vjp(lambda a, b, cc, g: _head_loss(a, b, cc, g, tg), x1_t, ple_t, gl_t, gf)
        dx1, dple, dgl, dgf = vjp(jnp.ones((1, 1), F32))
        return dx1, dple, dgl, jnp.broadcast_to(loss, (1, LANES)), dgf

    dx2, dple, dgl, loss_row, d_gf = _rows(
        "loss_head", head_bwd, t, tile_s,
        [(x1, _row_spec(tile_s, d)), (ple, _row_spec(tile_s, d)), (gl, _row_spec(tile_s, d)), full2(gf_row),
         (tgt, _row_spec(tile_s, d))],
        [(d, F32), (d, BF16), (d, BF16)], [(1, LANES), (1, d)])

    g_ple_proj = _matmul("d_ple_proj", ps, dple, ta=True)
    g_ple_gate = _matmul("d_ple_gate_w", n2, dgl, ta=True)
    dn2 = _matmul("d_n2", dgl, full["ple_gate_w"], tb=True)

    def norm_bwd(i, a, g, dh, res):
        _, vjp = jax.vjp(_rms, a, g)
        da, dg = vjp(dh)
        return res + da, dg

    dx1, d_g2 = _rows("norm2_bwd", norm_bwd, t, tile_s,
                      [(x1, _row_spec(tile_s, d)), full2(ple_norm_g), (dn2, _row_spec(tile_s, d)),
                       (dx2, _row_spec(tile_s, d))], [(d, F32)], [(1, d)])
    g_w_out = _matmul("d_w_out", merged, dx1, ta=True)
    dmerged = _matmul("d_merged", dx1, full["w_out"], tb=True)

    def merge_bwd(i, zg, a, b, dm):
        _, vjp = jax.vjp(functools.partial(_merge, d=d), zg, a, b)
        return vjp(dm)

    dz_gate, du_rw, du_fox = _rows(
        "merge_bwd", merge_bwd, t, tile_s,
        [(z_gate, _row_spec(tile_s, 2 * d)), (u_rw, _row_spec(tile_s, d)), (u_fox, _row_spec(tile_s, d)),
         (dmerged, _row_spec(tile_s, d))], [(2 * d, BF16), (d, BF16), (d, BF16)])
    g_up_rw = _matmul("d_w_up_rwkv", y_rw, du_rw, ta=True)
    g_up_fox = _matmul("d_w_up_fox", y_fox, du_fox, ta=True)
    dy_rw = _matmul("d_y_rwkv", du_rw, full["w_up_rwkv"], tb=True)
    dy_fox = _matmul("d_y_fox", du_fox, full["w_up_fox"], tb=True)

    def post_bwd(i, y, r, k, v, g, ln_g, ln_b, r_k, dy):
        _, vjp = jax.vjp(functools.partial(_rw_post, cw=cw), y, r, k, v, g, ln_g, ln_b, r_k)
        return vjp(dy)

    dys, dr1, dk1, dv1, dg1, d_ln_g, d_ln_b, d_r_k = _rows(
        "rwkv_post_bwd", post_bwd, t, tile_s,
        post_rows(y_scan, r_, k_, v_, g_) + post_consts + post_rows(dy_rw), [(cw, F32)] * 5, [(1, cw)] * 3)
    dr2, dw2, dk2, dv2, dkk2, da2 = _scan_bwd(r_, w_, k_, v_, kk_, a_, states, tinvs, dys)

    def pre_bwd(i, z, prev8, mu, w0, a0, wup, aup, k_k, k_a, dr_a, dr_b, dk_a, dk_b, dv_a, dv_b, dw, dkk, da, dg):
        zp = _shifted(i, z, prev8)
        _, vjp = jax.vjp(functools.partial(_rw_pre, cw=cw), z, zp, mu, w0, a0, wup, aup, k_k, k_a)
        dz, dzp, dmu, dw0, da0, dwup, daup, dk_k, dk_a = vjp((dr_a + dr_b, dw, dk_a + dk_b, dv_a + dv_b, dkk, da, dg))
        row = lax.broadcasted_iota(jnp.int32, dz.shape, 0)
        dz = dz + jnp.where(row < tile_s - 1, pltpu.roll(dzp, tile_s - 1, 0), 0.0)
        first = jnp.where(lax.broadcasted_iota(jnp.int32, (8, dz.shape[1]), 0) == 0, _row_of(dzp, 0), 0.0)
        return dz, first, dmu, dw0, da0, dwup, daup, dk_k, dk_a

    def pre_bwd_call():
        n_in = 2 + len(pre_consts) + 10
        ins = ([(z_rw, _row_spec(tile_s, rw_cols)), (z_rw, _prev_rows_spec(tile_s, rw_cols))] + pre_consts
               + post_rows(dr1, dr2, dk1, dk2, dv1, dv2, dw2, dkk2, da2, dg1))

        def body(*refs):
            i = pl.program_id(0)
            vals = pre_bwd(i, *[r[...] for r in refs[:n_in]])
            refs[n_in][...] = vals[0]
            refs[n_in + 1][...] = vals[1]
            for r, v in zip(refs[n_in + 2:], vals[2:]):
                @pl.when(i == 0)
                def _(r=r, v=v):
                    r[...] = v

                @pl.when(i > 0)
                def _(r=r, v=v):
                    r[...] += v

        acc_shapes = [(1, rw_cols), (1, cw), (1, cw), (LANES, cw), (LANES, cw), (1, cw), (1, cw)]
        return pl.pallas_call(
            body, name="rwkv_pre_bwd", grid=(n_s,), in_specs=[s for _, s in ins],
            out_specs=[_row_spec(tile_s, rw_cols), pl.BlockSpec((8, rw_cols), lambda i: (i, 0))]
            + [_full_spec(s) for s in acc_shapes],
            out_shape=[jax.ShapeDtypeStruct((t, rw_cols), F32), jax.ShapeDtypeStruct((8 * n_s, rw_cols), F32)]
            + [jax.ShapeDtypeStruct(s, F32) for s in acc_shapes],
            compiler_params=_params("arbitrary"),
        )(*[a for a, _ in ins])

    dz_main, dz_first, d_mu, d_w0, d_a0, d_wup, d_aup, d_k_k, d_k_a = pre_bwd_call()

    def add_next_row(i, dz, nxt8):
        row = lax.broadcasted_iota(jnp.int32, dz.shape, 0)
        carry = jnp.where(i < n_s - 1, _row_of(nxt8, 0), 0.0)
        return (dz + jnp.where(row == tile_s - 1, carry, 0.0),)

    (dz_rw,) = _rows("rwkv_shift_bwd", add_next_row, t, tile_s,
                     [(dz_main, _row_spec(tile_s, rw_cols)), (dz_first, _next_rows_spec(tile_s, rw_cols, n_s))],
                     [(rw_cols, BF16)])

    def fox_post_bwd(i, o, g, dy):
        _, vjp = jax.vjp(lambda oo, gg: oo * (gg * jax.nn.sigmoid(gg)), o, g)
        do, dg = vjp(dy)
        return do, _head_sum(do * o, _head_matrix(fw)), dg

    do_fox, dd_fox, dg_fox = _rows(
        "fox_post_bwd", fox_post_bwd, t, tile_s,
        [(o_fox, _row_spec(tile_s, fw)), (z_fox, _row_spec(tile_s, fw, 3)), (dy_fox, _row_spec(tile_s, fw))],
        [(fw, F32), (fw, F32), (fw, BF16)])
    dq_f, dk_f, dv_f, dcq, dck = _fox_bwd(z_fox, do_fox, c_fox, ct_fox, head_rows(lse_fox[:, ::HEAD]),
                                          head_rows(dd_fox[:, ::HEAD]), fw=fw)
    dc = head_cols(dcq) + dck.reshape(t, n_pair_f, LANES)[:, :, :2].reshape(t, fh)
    dfl, d_bf = _fox_cumsum_bwd(z_fox, b_pad, jnp.pad(dc, ((0, 0), (0, LANES - fh))), fw=fw, fh=fh)
    dz_fox = jnp.concatenate([dq_f, dk_f, dv_f, dg_fox, dfl], axis=1)

    g_w_rw = _matmul("d_w_in_rw", h, dz_rw, ta=True)
    g_w_fox = _matmul("d_w_in_fox", h, dz_fox, ta=True)
    g_w_gate = _matmul("d_w_in_gate", h, dz_gate, ta=True)

    g_full = {
        "w_in": jnp.concatenate([g_w_rw, g_w_fox[:, :fox_cols], g_w_gate], axis=1),
        "rw_w_lora_up": d_wup[:lr], "rw_a_lora_up": d_aup[lr:], "w_up_rwkv": g_up_rw, "w_up_fox": g_up_fox,
        "w_out": g_w_out, "ple_proj": g_ple_proj, "ple_gate_w": g_ple_gate,
    }
    def by_shard(n):
        g = g_full[n]
        if n in ROW_SHARDED:
            return g.reshape(N_SHARD, g.shape[0] // N_SHARD, g.shape[1])
        return jnp.stack(jnp.split(g, N_SHARD, axis=1))

    rs_state, token = _reduce_scatter_start([jnp.concatenate([by_shard(n) for n in names], axis=1) for names in groups])
    dh = _matmul("d_h_rw", dz_rw, w_rw, tb=True, after=token)
    dh = _matmul("d_h_fox", dz_fox, w_fox, tb=True, add=dh)
    dh = _matmul("d_h_gate", dz_gate, w_gate, tb=True, add=dh)
    grad_x, d_g1 = _rows("norm1_bwd", norm_bwd, t, tile_s,
                         [(xs, _row_spec(tile_s, d)), full2(norm_g), (dh, _row_spec(tile_s, d)),
                          (dx1, _row_spec(tile_s, d))], [(d, F32)], [(1, d)])
    small_parts = dict(norm_g=d_g1, rw_shift_mu=d_mu, rw_w0=d_w0, rw_a0=d_a0, rw_k_k=d_k_k, rw_k_a=d_k_a, rw_r_k=d_r_k,
                       rw_ln_g=d_ln_g, rw_ln_b=d_ln_b, fox_b_f=d_bf[:, :fh], ple_norm_g=d_g2, final_norm_g=d_gf)
    n_small = sum(-(-wts[n].size // LANES) for n in SMALL)
    small_rows = -(-n_small // 8) * 8
    small_shapes = [wts[n].shape for n in SMALL]
    g_small = _all_reduce_small(_pack_small([small_parts[n] for n in SMALL], small_rows))
    reduced = _reduce_scatter_end(rs_state, g_small)
    grads = {}
    for names, g in zip(groups, reduced):
        off = 0
        for n in names:
            r = wts[n].shape[1]
            grads[n] = g[off:off + r][None]
            off += r

    for n, g in zip(SMALL, _unpack_small(g_small, small_shapes)):
        grads[n] = g

    delta, new_m, new_v = {}, {}, {}
    for n in BIG:
        delta[n], new_m[n], new_v[n] = _adamw("adamw_" + n, wts[n], grads[n], mom[n], vel[n])
    packed = lambda src: _pack_small([src[n] for n in SMALL], small_rows)[None]
    for store, out in zip((delta, new_m, new_v), _adamw("adamw_small", packed(wts), g_small[None], packed(mom), packed(vel))):
        for n, a in zip(SMALL, _unpack_small(out[0], small_shapes)):
            store[n] = a

    loss = lax.psum(loss_row[0, 0], ("x", "y", "c"))
    return (loss, grad_x[None], *[grads[n] for n in WEIGHTS], *[delta[n] for n in WEIGHTS],
            *[new_m[n] for n in WEIGHTS], *[new_v[n] for n in WEIGHTS])
```

```python
import functools

import jax
import jax.numpy as jnp
from jax import lax
from jax.experimental import pallas as pl
from jax.experimental.pallas import tpu as pltpu

F32 = jnp.float32
BF16 = jnp.bfloat16
HIGHEST = lax.Precision.HIGHEST
SCAN_PREC = lax.Precision.HIGH
MESH = pl.DeviceIdType.MESH

LANES = 128
HEAD = 64
NORM_EPS = 1e-6
GN_EPS = 64e-5
ADAM_LR = 0.001
ADAM_B1 = 0.9
ADAM_B2 = 0.999
ADAM_EPS = 1e-08
ADAM_WD = 0.01
ADAM_STEP = 10
N_SHARD = 4
VMEM_LIMIT = 56 * 1024 * 1024
PAIRS_PER_STEP = 8
ELIM_BASE = 8


def _params(*sem):
    return pltpu.CompilerParams(dimension_semantics=sem, vmem_limit_bytes=VMEM_LIMIT)


def _tile(n, cands):
    for c in cands:
        if c <= n and n % c == 0:
            return c
    return n


def _div_tile(n, cap, mult):
    return max(c for c in range(mult, min(n, cap) + 1, mult) if n % c == 0)


_ROW_TILES = (512, 256, 128, 64, 32, 16, 8)


def _dot(a, b, prec=None):
    return lax.dot_general(a, b, (((1,), (0,)), ((), ())), precision=prec, preferred_element_type=F32)


def _dot_nt(a, b, prec=None):
    return lax.dot_general(a, b, (((1,), (1,)), ((), ())), precision=prec, preferred_element_type=F32)


def _dot_tn(a, b, prec=None):
    return lax.dot_general(a, b, (((0,), (0,)), ((), ())), precision=prec, preferred_element_type=F32)


@jax.custom_vjp
def _bdot(x, w):
    return _dot(x.astype(BF16), w.astype(BF16))


def _bdot_fwd(x, w):
    return _bdot(x, w), (x, w)


def _bdot_bwd(res, ct):
    x, w = res
    return _dot_nt(ct.astype(BF16), w.astype(BF16)), _dot_tn(x.astype(BF16), ct.astype(BF16))


_bdot.defvjp(_bdot_fwd, _bdot_bwd)


def _head_matrix(width):
    c = lax.broadcasted_iota(jnp.int32, (width, LANES), 0)
    h = lax.broadcasted_iota(jnp.int32, (width, LANES), 1)
    return (c // HEAD == h).astype(F32)


def _head_sum(x, e):
    return _dot_nt(_dot(x, e, SCAN_PREC), e, SCAN_PREC)


def _softplus(x):
    return jnp.maximum(x, 0.0) + jnp.log1p(jnp.exp(-jnp.abs(x)))


def _lane_pick(x, idx):
    lane = lax.broadcasted_iota(jnp.int32, x.shape, 1)
    return jnp.sum(jnp.where(lane == idx, x, 0.0), axis=1, keepdims=True)


def _matmul(name, a, b, *, ta=False, tb=False, add=None, out_dtype=F32, after=None):
    m, k = (a.shape[1], a.shape[0]) if ta else a.shape
    n = b.shape[0] if tb else b.shape[1]
    tm = _tile(m, (1024, 512, 256, 128))
    tn = _tile(n, (1408, 1024, 768, 640, 512, 384, 256, 128))
    tk = _tile(k, (2048, 1408, 1024, 768, 640, 512, 384, 256, 128, 64, 32, 16))
    nk = k // tk
    dims = (((0 if ta else 1,), (1 if tb else 0,)), ((), ()))

    def body(*refs):
        a_ref, b_ref = refs[0], refs[1]
        o_ref, acc_ref = (refs[-1], None) if nk == 1 else refs[-2:]
        kk = pl.program_id(2)
        part = lax.dot_general(a_ref[...].astype(BF16), b_ref[...].astype(BF16), dims, preferred_element_type=F32)

        def finish(r):
            if add is not None:
                r = r + refs[2][...].astype(F32)
            o_ref[...] = r.astype(o_ref.dtype)

        if nk == 1:
            finish(part)
            return

        @pl.when(kk == 0)
        def _():
            acc_ref[...] = part

        @pl.when((kk > 0) & (kk < nk - 1))
        def _():
            acc_ref[...] += part

        @pl.when(kk == nk - 1)
        def _():
            finish(acc_ref[...] + part)

    a_spec = pl.BlockSpec((tk, tm), lambda i, j, kk: (kk, i)) if ta else pl.BlockSpec((tm, tk), lambda i, j, kk: (i, kk))
    b_spec = pl.BlockSpec((tn, tk), lambda i, j, kk: (j, kk)) if tb else pl.BlockSpec((tk, tn), lambda i, j, kk: (kk, j))
    o_spec = pl.BlockSpec((tm, tn), lambda i, j, kk: (i, j))
    ins, specs = [a, b], [a_spec, b_spec]
    if add is not None:
        ins.append(add)
        specs.append(o_spec)
    if after is not None:
        ins.append(after)
        specs.append(pl.BlockSpec(after.shape, lambda i, j, kk: (0,) * after.ndim))
    return pl.pallas_call(
        body, name=name, grid=(m // tm, n // tn, nk), in_specs=specs, out_specs=o_spec,
        out_shape=jax.ShapeDtypeStruct((m, n), out_dtype),
        scratch_shapes=[pltpu.VMEM((tm, tn), F32)] if nk > 1 else [],
        compiler_params=_params("parallel", "parallel", "arbitrary"),
    )(*ins)


def _rows(name, fn, n_rows, tile, ins, outs, accs=()):
    n_in, n_out = len(ins), len(outs)

    def body(*refs):
        i = pl.program_id(0)
        vals = fn(i, *[r[...] for r in refs[:n_in]])
        for r, v in zip(refs[n_in:n_in + n_out], vals[:n_out]):
            r[...] = v.astype(r.dtype)
        for r, v in zip(refs[n_in + n_out:], vals[n_out:]):
            @pl.when(i == 0)
            def _(r=r, v=v):
                r[...] = v

            @pl.when(i > 0)
            def _(r=r, v=v):
                r[...] += v

    out_specs = [pl.BlockSpec((tile, w), lambda i: (i, 0)) for w, _ in outs]
    out_specs += [pl.BlockSpec(s, lambda i: (0, 0)) for s in accs]
    out_shape = [jax.ShapeDtypeStruct((n_rows, w), d) for w, d in outs]
    out_shape += [jax.ShapeDtypeStruct(s, F32) for s in accs]
    return pl.pallas_call(
        body, name=name, grid=(n_rows // tile,), in_specs=[s for _, s in ins], out_specs=out_specs,
        out_shape=out_shape, compiler_params=_params("arbitrary"),
    )(*[a for a, _ in ins])


def _row_spec(tile, width, col=0):
    return pl.BlockSpec((tile, width), lambda i: (i, col))


def _full_spec(shape):
    return pl.BlockSpec(shape, lambda i: (0,) * len(shape))


def _prev_rows_spec(tile, width):
    return pl.BlockSpec((8, width), lambda i: (jnp.maximum(i * (tile // 8) - 1, 0), 0))


def _next_rows_spec(tile, width, n_tiles):
    return pl.BlockSpec((8, width), lambda i: (jnp.minimum(i + 1, n_tiles - 1), 0))


def _row_of(x8, idx):
    r = lax.broadcasted_iota(jnp.int32, x8.shape, 0)
    return jnp.sum(jnp.where(r == idx, x8, 0.0), axis=0, keepdims=True)


def _rms(x, g):
    return x * lax.rsqrt(jnp.mean(x * x, axis=-1, keepdims=True) + NORM_EPS) * g


def _shifted(i, z, prev8):
    first = jnp.where(i > 0, _row_of(prev8, 7), 0.0)
    row = lax.broadcasted_iota(jnp.int32, z.shape, 0)
    return jnp.where(row == 0, first, pltpu.roll(z, 1, 0))


def _rw_pre(z, zp, mu, w0, a0, wup, aup, k_k, k_a, *, cw):
    zs = z + (zp - z) * mu
    r, k, v, g = (zs[:, j * cw:(j + 1) * cw] for j in range(4))
    lo = zs[:, 4 * cw:4 * cw + LANES]
    w_raw = w0 + _bdot(jnp.tanh(lo), wup)
    decay = jnp.exp(-jnp.exp(-_softplus(-w_raw) - 0.5))
    a = jax.nn.sigmoid(a0 + _bdot(lo, aup))
    e = _head_matrix(cw)
    kk = k * k_k
    kk = kk / jnp.maximum(jnp.sqrt(_head_sum(kk * kk, e)), 1e-12)
    k_mod = k * (1.0 + (a - 1.0) * k_a)
    return r, decay, k_mod, v, kk, a, g


def _rw_post(y, r, k_mod, v, g, ln_g, ln_b, r_k, *, cw):
    e = _head_matrix(cw)
    mu = _head_sum(y, e) * (1.0 / HEAD)
    d = y - mu
    var = _head_sum(d * d, e) * (1.0 / HEAD)
    yn = d * lax.rsqrt(var + GN_EPS) * ln_g + ln_b
    bonus = _head_sum(r * k_mod * r_k, e) * v
    return (yn + bonus) * (g * jax.nn.sigmoid(g))


def _merge(zg, u_rw, u_fox, *, d):
    return jax.nn.sigmoid(zg[:, :d]) * u_rw + jax.nn.sigmoid(zg[:, d:]) * u_fox


def _head_loss(x1, ple, gl, gf, tgt):
    x2 = x1 + ple * jax.nn.sigmoid(gl)
    err = _rms(x2, gf) - tgt
    return 0.5 * jnp.sum(jnp.mean(err * err, axis=-1, keepdims=True), axis=0, keepdims=True)


def _batched(a, b, ca, cb):
    return lax.dot_general(a, b, (((ca,), (cb,)), ((0,), (0,))), precision=SCAN_PREC, preferred_element_type=F32)


def _eliminate_block(lo, off, m):
    n, c, _ = lo.shape
    ci = lax.broadcasted_iota(jnp.int32, (n, m, c), 2)
    if m <= ELIM_BASE:
        ri = lax.broadcasted_iota(jnp.int32, (n, m, c), 1)
        rows = jnp.where((ci >= off) & (ci < off + m), lo[:, off:off + m, :], 0.0)
        x = (ri + off == ci).astype(F32)
        for s in range(m - 1):
            col = jnp.sum(jnp.where(ci == off + s, rows, 0.0), axis=2, keepdims=True)
            x = x - col * x[:, s:s + 1, :]
        return x
    h = m // 2
    xa = _eliminate_block(lo, off, h)
    xd = _eliminate_block(lo, off + h, h)
    ci_h = lax.broadcasted_iota(jnp.int32, (n, h, c), 2)
    b = jnp.where((ci_h >= off) & (ci_h < off + h), lo[:, off + h:off + m, :], 0.0)

    def at_rows(x, start):
        parts = ([jnp.zeros((n, start, c), F32)] if start else []) + [x]
        rest = c - start - x.shape[1]
        return jnp.concatenate(parts + ([jnp.zeros((n, rest, c), F32)] if rest else []), axis=1)

    low = xd - _batched(_batched(xd, at_rows(b, off + h), 2, 1), at_rows(xa, off), 2, 1)
    return jnp.concatenate([xa, low], axis=1)


def _eliminate(lo):
    return _eliminate_block(lo, 0, lo.shape[1])


@jax.custom_vjp
def _unit_lower_inverse(lo, known):
    return _eliminate(lo) if known is None else known


def _uli_fwd(lo, known):
    x = _unit_lower_inverse(lo, known)
    return x, (x, known)


def _uli_bwd(res, dx):
    x, known = res
    dlo = -_batched(_batched(x, dx, 1, 1), x, 2, 2)
    return dlo, (None if known is None else jnp.zeros_like(known))


_unit_lower_inverse.defvjp(_uli_fwd, _uli_bwd)


def _rwkv_chunk(s0, r, w, k, v, kk, a, *, c, tinv_known=None):
    pairs = range(len(s0))
    lane = lax.broadcasted_iota(jnp.int32, (1, LANES), 1)
    heads = (lane < HEAD, lane >= HEAD)
    ti = lax.broadcasted_iota(jnp.int32, (c, c), 0)
    si = lax.broadcasted_iota(jnp.int32, (c, c), 1)
    incl = si <= ti
    strict = si < ti
    tri = incl.astype(F32)
    logw = [jnp.log(w[p]) for p in pairs]
    cum = [_dot(tri, logw[p], HIGHEST) for p in pairs]
    cum_end = [jnp.sum(logw[p], axis=0, keepdims=True) for p in pairs]
    g_inv = [jnp.exp(-cum[p]) for p in pairs]
    to_end = [jnp.exp(cum_end[p] - cum[p]) for p in pairs]
    b = [kk[p] * a[p] for p in pairs]
    beta = [b[p] * g_inv[p] for p in pairs]
    kap = [kk[p] * jnp.exp(cum[p] - logw[p]) for p in pairs]
    kt = [k[p] * g_inv[p] for p in pairs]
    rt = [r[p] * jnp.exp(cum[p]) for p in pairs]
    lhs = [jnp.concatenate([jnp.where(m, x[p], 0.0) for x in (kap, rt) for m in heads], axis=0) for p in pairs]
    vs_beta = [_dot_nt(lhs[p], beta[p], None) for p in pairs]
    vs_kt = [_dot_nt(lhs[p], kt[p], None) for p in pairs]
    strict2 = jnp.concatenate([strict, strict], axis=0)
    incl2 = jnp.concatenate([incl, incl], axis=0)
    lo = [jnp.where(strict2, vs_beta[p][:2 * c], 0.0) for p in pairs]
    mm = [jnp.where(strict2, vs_kt[p][:2 * c], 0.0) for p in pairs]
    arb = [jnp.where(incl2, vs_beta[p][2 * c:], 0.0) for p in pairs]
    ark = [jnp.where(incl2, vs_kt[p][2 * c:], 0.0) for p in pairs]
    per_head = lambda xs: jnp.concatenate([xs[p][h * c:(h + 1) * c][None] for p in pairs for h in (0, 1)])
    tinv = _unit_lower_inverse(per_head(lo), None if tinv_known is None else per_head(tinv_known))
    tinv = [jnp.concatenate([tinv[2 * p], tinv[2 * p + 1]], axis=0) for p in pairs]
    both = lambda x: jnp.where(heads[0], x[:c], x[c:])
    vs_s = [_dot_nt(jnp.concatenate([kap[p], rt[p]], axis=0), s0[p], None) for p in pairs]
    rhs = [vs_s[p][:c] + both(_dot(mm[p], v[p], None)) for p in pairs]
    u = [-both(_dot(tinv[p], rhs[p], None)) for p in pairs]
    y = [vs_s[p][c:] + both(_dot(arb[p], u[p], None) + _dot(ark[p], v[p], None)) for p in pairs]
    rr = lax.broadcasted_iota(jnp.int32, (LANES, LANES), 0) < HEAD
    cc = lax.broadcasted_iota(jnp.int32, (LANES, LANES), 1) < HEAD
    ds = [_dot_tn(jnp.concatenate([u[p], v[p]], axis=0),
                  jnp.concatenate([b[p] * to_end[p], k[p] * to_end[p]], axis=0), None) for p in pairs]
    s1 = [s0[p] * jnp.exp(cum_end[p]) + jnp.where(rr == cc, ds[p], 0.0) for p in pairs]
    return tuple(y), tuple(s1), tuple(tinv)


def _scan_tiles(t, n_pair):
    return _tile(t, (32, 16, 8)), _tile(t, (256, 128, 64, 32)), _tile(n_pair, (PAIRS_PER_STEP, 4, 2, 1))


def _scan_fwd(r, w, k, v, kk, a):
    t, width = r.shape
    c, tb, npb = _scan_tiles(t, width // LANES)
    n_grp, n_blk, n_cb = width // (LANES * npb), t // tb, tb // c

    def body(r_ref, w_ref, k_ref, v_ref, kk_ref, a_ref, y_ref, st_ref, ti_ref, s_scr):
        @pl.when(pl.program_id(1) == 0)
        def _():
            s_scr[...] = jnp.zeros_like(s_scr)

        def chunk(j, carry):
            sl = pl.ds(pl.multiple_of(j * c, c), c)
            lanes = [pl.ds(q * LANES, LANES) for q in range(npb)]
            s0 = tuple(s_scr[q] for q in range(npb))
            cols = lambda ref: tuple(ref[sl, ln] for ln in lanes)
            y, s1, tinv = _rwkv_chunk(s0, cols(r_ref), cols(w_ref), cols(k_ref), cols(v_ref), cols(kk_ref), cols(a_ref),
                                      c=c)
            for q, ln in enumerate(lanes):
                st_ref[q, j] = s0[q]
                ti_ref[q, j] = tinv[q]
                y_ref[sl, ln] = y[q]
                s_scr[q] = s1[q]
            return carry

        lax.fori_loop(0, n_cb, chunk, 0)

    blk = pl.BlockSpec((tb, npb * LANES), lambda p, i: (i, p))
    return pl.pallas_call(
        body, name="rwkv_scan_fwd", grid=(n_grp, n_blk), in_specs=[blk] * 6,
        out_specs=[blk, pl.BlockSpec((npb, n_cb, LANES, LANES), lambda p, i: (p, i, 0, 0)),
                   pl.BlockSpec((npb, n_cb, 2 * c, c), lambda p, i: (p, i, 0, 0))],
        out_shape=[jax.ShapeDtypeStruct((t, width), F32),
                   jax.ShapeDtypeStruct((width // LANES, t // c, LANES, LANES), F32),
                   jax.ShapeDtypeStruct((width // LANES, t // c, 2 * c, c), F32)],
        scratch_shapes=[pltpu.VMEM((npb, LANES, LANES), F32)],
        compiler_params=_params("arbitrary", "arbitrary"),
    )(r, w, k, v, kk, a)


def _scan_bwd(r, w, k, v, kk, a, st, ti, dy):
    t, width = r.shape
    c, tb, npb = _scan_tiles(t, width // LANES)
    n_grp, n_blk, n_cb = width // (LANES * npb), t // tb, tb // c

    def body(r_ref, w_ref, k_ref, v_ref, kk_ref, a_ref, st_ref, ti_ref, dy_ref,
             dr_ref, dw_ref, dk_ref, dv_ref, dkk_ref, da_ref, ds_scr):
        @pl.when(pl.program_id(1) == 0)
        def _():
            ds_scr[...] = jnp.zeros_like(ds_scr)

        def chunk(jj, carry):
            j = n_cb - 1 - jj
            sl = pl.ds(pl.multiple_of(j * c, c), c)
            lanes = [pl.ds(q * LANES, LANES) for q in range(npb)]
            cols = lambda ref: tuple(ref[sl, ln] for ln in lanes)
            args = (tuple(st_ref[q, j] for q in range(npb)), cols(r_ref), cols(w_ref), cols(k_ref), cols(v_ref),
                    cols(kk_ref), cols(a_ref))
            known = tuple(ti_ref[q, j] for q in range(npb))
            _, vjp = jax.vjp(lambda *xs: _rwkv_chunk(*xs, c=c, tinv_known=known)[:2], *args)
            grads = vjp((cols(dy_ref), tuple(ds_scr[q] for q in range(npb))))
            for q, ln in enumerate(lanes):
                ds_scr[q] = grads[0][q]
                for ref, g in zip((dr_ref, dw_ref, dk_ref, dv_ref, dkk_ref, da_ref), grads[1:]):
                    ref[sl, ln] = g[q]
            return carry

        lax.fori_loop(0, n_cb, chunk, 0)

    blk = pl.BlockSpec((tb, npb * LANES), lambda p, i: (n_blk - 1 - i, p))
    stb = pl.BlockSpec((npb, n_cb, LANES, LANES), lambda p, i: (p, n_blk - 1 - i, 0, 0))
    tib = pl.BlockSpec((npb, n_cb, 2 * c, c), lambda p, i: (p, n_blk - 1 - i, 0, 0))
    return pl.pallas_call(
        body, name="rwkv_scan_bwd", grid=(n_grp, n_blk), in_specs=[blk] * 6 + [stb, tib, blk], out_specs=[blk] * 6,
        out_shape=[jax.ShapeDtypeStruct((t, width), F32)] * 6,
        scratch_shapes=[pltpu.VMEM((npb, LANES, LANES), F32)],
        compiler_params=_params("arbitrary", "arbitrary"),
    )(r, w, k, v, kk, a, st, ti, dy)


NEG = -1e30


def _fox_cumsum(zf, b_pad, *, fw, fh):
    t = zf.shape[0]
    tile = _tile(t, (256, 128, 64, 32, 16, 8))

    def body(fl_ref, b_ref, c_ref, carry):
        @pl.when(pl.program_id(0) == 0)
        def _():
            carry[...] = jnp.zeros_like(carry)

        lane = lax.broadcasted_iota(jnp.int32, (tile, LANES), 1)
        logf = jnp.where(lane < fh, -_softplus(-(fl_ref[...] + b_ref[...])), 0.0)
        ri = lax.broadcasted_iota(jnp.int32, (tile, tile), 0)
        ci = lax.broadcasted_iota(jnp.int32, (tile, tile), 1)
        c_ref[...] = carry[...] + _dot((ci <= ri).astype(F32), logf, HIGHEST)
        carry[...] += jnp.sum(logf, axis=0, keepdims=True)

    return pl.pallas_call(
        body, name="fox_cumsum", grid=(t // tile,),
        in_specs=[_row_spec(tile, LANES, 4 * fw // LANES), _full_spec((1, LANES))],
        out_specs=_row_spec(tile, LANES), out_shape=jax.ShapeDtypeStruct((t, LANES), F32),
        scratch_shapes=[pltpu.VMEM((1, LANES), F32)], compiler_params=_params("arbitrary"),
    )(zf, b_pad)


def _fox_cumsum_bwd(zf, b_pad, dc, *, fw, fh):
    t = zf.shape[0]
    tile = _tile(t, (256, 128, 64, 32, 16, 8))
    n = t // tile

    def body(fl_ref, b_ref, dc_ref, dfl_ref, db_ref, carry):
        i = pl.program_id(0)

        @pl.when(i == 0)
        def _():
            carry[...] = jnp.zeros_like(carry)
            db_ref[...] = jnp.zeros_like(db_ref)

        lane = lax.broadcasted_iota(jnp.int32, (tile, LANES), 1)
        dc_t = dc_ref[...]
        ri = lax.broadcasted_iota(jnp.int32, (tile, tile), 0)
        ci = lax.broadcasted_iota(jnp.int32, (tile, tile), 1)
        dlogf = carry[...] + _dot((ci >= ri).astype(F32), dc_t, HIGHEST)
        carry[...] += jnp.sum(dc_t, axis=0, keepdims=True)
        dfl = jnp.where(lane < fh, dlogf * jax.nn.sigmoid(-(fl_ref[...] + b_ref[...])), 0.0)
        dfl_ref[...] = dfl.astype(dfl_ref.dtype)
        db_ref[...] += jnp.sum(dfl, axis=0, keepdims=True)

    rev = lambda col: pl.BlockSpec((tile, LANES), lambda i: (n - 1 - i, col))
    return pl.pallas_call(
        body, name="fox_cumsum_bwd", grid=(n,),
        in_specs=[rev(4 * fw // LANES), _full_spec((1, LANES)), rev(0)],
        out_specs=[rev(0), _full_spec((1, LANES))],
        out_shape=[jax.ShapeDtypeStruct((t, LANES), BF16), jax.ShapeDtypeStruct((1, LANES), F32)],
        scratch_shapes=[pltpu.VMEM((1, LANES), F32)], compiler_params=_params("arbitrary"),
    )(zf, b_pad, dc)


def _fox_tile(t):
    return _tile(t, (512, 256, 128))


def _fox_fwd(zf, c, ct, *, fw):
    t = zf.shape[0]
    tq = _fox_tile(t)
    th = tq // 2
    n_pair, n_q = fw // LANES, t // tq
    scale = HEAD ** -0.5
    chains = [(h, qq) for h in (0, 1) for qq in (0, 1)]

    def body(q_ref, k_ref, v_ref, g_ref, c_ref, ct_ref, o_ref, lse_ref, y_ref):
        hp, i = pl.program_id(0), pl.program_id(1)
        lane = lax.broadcasted_iota(jnp.int32, (1, LANES), 1)
        in_head = (lane < HEAD, lane >= HEAD)
        rows = [pl.ds(qq * th, th) for qq in (0, 1)]
        qh = [jnp.where(in_head[h], q_ref[rows[qq], :] * scale, 0.0).astype(BF16) for h, qq in chains]
        cq = [_lane_pick(c_ref[rows[qq], :], 2 * hp + h) for h, qq in chains]
        qidx = lax.broadcasted_iota(jnp.int32, (th, tq), 0)
        kidx = lax.broadcasted_iota(jnp.int32, (th, tq), 1)

        def kv_step(j, carry, diagonal):
            m, l, acc = carry
            ks = pl.ds(pl.multiple_of(j * tq, tq), tq)
            kb = k_ref[ks, :].astype(BF16)
            vb = v_ref[ks, :]
            vh = [jnp.where(in_head[h], vb, 0.0).astype(BF16) for h in (0, 1)]
            ck = [ct_ref[0, j, pl.ds(h, 1), :] for h in (0, 1)]
            s = [_dot_nt(qh[n], kb) + cq[n] - ck[h] for n, (h, qq) in enumerate(chains)]
            if diagonal:
                s = [jnp.where(qq * th + qidx >= kidx, s[n], NEG) for n, (h, qq) in enumerate(chains)]
            m_new = [jnp.maximum(m[n], jnp.max(s[n], axis=1, keepdims=True)) for n in range(4)]
            p = [jnp.exp(s[n] - m_new[n]) for n in range(4)]
            alpha = [jnp.exp(m[n] - m_new[n]) for n in range(4)]
            l = [l[n] * alpha[n] + jnp.sum(p[n], axis=1, keepdims=True) for n in range(4)]
            pv = [_dot(p[n].astype(BF16), vh[h]) for n, (h, qq) in enumerate(chains)]
            acc = [acc[qq] * jnp.where(in_head[0], alpha[qq], alpha[2 + qq]) + pv[qq] + pv[2 + qq] for qq in (0, 1)]
            return tuple(m_new), tuple(l), tuple(acc)

        init = (tuple(jnp.full((th, 1), NEG, F32) for _ in chains), tuple(jnp.zeros((th, 1), F32) for _ in chains),
                tuple(jnp.zeros((th, LANES), F32) for _ in (0, 1)))
        carry = lax.fori_loop(0, i, functools.partial(kv_step, diagonal=False), init)
        m, l, acc = kv_step(i, carry, True)
        for qq in (0, 1):
            o = acc[qq] / jnp.where(in_head[0], l[qq], l[2 + qq])
            g = g_ref[rows[qq], :]
            o_ref[rows[qq], :] = o
            lse_ref[rows[qq], :] = jnp.where(in_head[0], m[qq] + jnp.log(l[qq]), m[2 + qq] + jnp.log(l[2 + qq]))
            y_ref[rows[qq], :] = (o * (g * jax.nn.sigmoid(g))).astype(y_ref.dtype)

    npw = fw // LANES
    blk = lambda col0: pl.BlockSpec((tq, LANES), lambda hp, i: (i, col0 + hp))
    res = lambda col0: pl.BlockSpec((t, LANES), lambda hp, i: (0, col0 + hp))
    out_blk = pl.BlockSpec((tq, LANES), lambda hp, i: (i, hp))
    return pl.pallas_call(
        body, name="fox_attn_fwd", grid=(n_pair, n_q),
        in_specs=[blk(0), res(npw), res(2 * npw), blk(3 * npw),
                  pl.BlockSpec((tq, LANES), lambda hp, i: (i, 0)),
                  pl.BlockSpec((1, n_q, 2, tq), lambda hp, i: (hp, 0, 0, 0))],
        out_specs=[out_blk, out_blk, out_blk],
        out_shape=[jax.ShapeDtypeStruct((t, fw), F32), jax.ShapeDtypeStruct((t, fw), F32),
                   jax.ShapeDtypeStruct((t, fw), BF16)],
        compiler_params=_params("arbitrary", "arbitrary"),
    )(zf, zf, zf, zf, c, ct)


def _fox_bwd(zf, do, c, ct, lse_r, dd_r, *, fw):
    t = zf.shape[0]
    tq = _fox_tile(t)
    n_pair, n_q = fw // LANES, t // tq
    scale = HEAD ** -0.5

    def body(q_ref, k_ref, v_ref, do_ref, c_ref, ct_ref, lse_ref, dd_ref,
             dq_ref, dk_ref, dv_ref, dcq_ref, dck_ref, dq_acc, dcq_acc):
        hp, j = pl.program_id(0), pl.program_id(1)

        @pl.when(j == 0)
        def _():
            dq_acc[...] = jnp.zeros_like(dq_acc)
            dcq_acc[...] = jnp.zeros_like(dcq_acc)

        lane = lax.broadcasted_iota(jnp.int32, (1, LANES), 1)
        in_head = (lane < HEAD, lane >= HEAD)
        kb = k_ref[...]
        kh = [jnp.where(m, kb, 0.0).astype(BF16) for m in in_head]
        vb = v_ref[...].astype(BF16)
        c_k = c_ref[...]
        ck = [_lane_pick(c_k, 2 * hp + h) for h in (0, 1)]
        kidx = lax.broadcasted_iota(jnp.int32, (tq, tq), 0)
        qidx = lax.broadcasted_iota(jnp.int32, (tq, tq), 1)

        def q_step(i, carry, diagonal):
            dk, dv, dck = carry
            qs = pl.ds(pl.multiple_of(i * tq, tq), tq)
            qf = q_ref[qs, :] * scale
            dof = do_ref[qs, :]
            qh = [jnp.where(m, qf, 0.0).astype(BF16) for m in in_head]
            doh = [jnp.where(m, dof, 0.0).astype(BF16) for m in in_head]
            row = lambda ref, h: ref[0, i, pl.ds(h, 1), :]
            st = [_dot_nt(kh[h], qh[h]) + row(ct_ref, h) - ck[h] for h in (0, 1)]
            p = [jnp.exp(st[h] - row(lse_ref, h)) for h in (0, 1)]
            if diagonal:
                p = [jnp.where(kidx <= qidx, p[h], 0.0) for h in (0, 1)]
            dst = [p[h] * (_dot_nt(vb, doh[h]) - row(dd_ref, h)) for h in (0, 1)]
            p16 = [x.astype(BF16) for x in p]
            ds16 = [x.astype(BF16) for x in dst]
            dv = dv + _dot(p16[0], doh[0]) + _dot(p16[1], doh[1])
            dk = dk + _dot(ds16[0], qh[0]) + _dot(ds16[1], qh[1])
            dq_acc[qs, :] += _dot_tn(ds16[0], kh[0]) + _dot_tn(ds16[1], kh[1])
            for h in (0, 1):
                dcq_acc[i, pl.ds(h, 1), :] += jnp.sum(dst[h], axis=0, keepdims=True)
            dck = tuple(dck[h] - jnp.sum(dst[h], axis=1, keepdims=True) for h in (0, 1))
            return dk, dv, dck

        zero = jnp.zeros((tq, LANES), F32)
        carry = q_step(j, (zero, zero, (jnp.zeros((tq, 1), F32),) * 2), True)
        dk, dv, dck = lax.fori_loop(j + 1, n_q, functools.partial(q_step, diagonal=False), carry)
        dk_ref[...] = dk.astype(dk_ref.dtype)
        dv_ref[...] = dv.astype(dv_ref.dtype)
        dck_ref[...] = jnp.where(lane == 0, dck[0], jnp.where(lane == 1, dck[1], 0.0))

        @pl.when(j == n_q - 1)
        def _():
            dq_ref[...] = (dq_acc[...] * scale).astype(dq_ref.dtype)
            dcq_ref[0] = dcq_acc[...]

    npw = fw // LANES
    res_z = lambda col0: pl.BlockSpec((t, LANES), lambda hp, j: (0, col0 + hp))
    blk_z = lambda col0: pl.BlockSpec((tq, LANES), lambda hp, j: (j, col0 + hp))
    res = pl.BlockSpec((t, LANES), lambda hp, j: (0, hp))
    blk = pl.BlockSpec((tq, LANES), lambda hp, j: (j, hp))
    rows = pl.BlockSpec((1, n_q, 2, tq), lambda hp, j: (hp, 0, 0, 0))
    return pl.pallas_call(
        body, name="fox_attn_bwd", grid=(n_pair, n_q),
        in_specs=[res_z(0), blk_z(npw), blk_z(2 * npw), res, pl.BlockSpec((tq, LANES), lambda hp, j: (j, 0)),
                  rows, rows, rows],
        out_specs=[res, blk, blk, rows, blk],
        out_shape=[jax.ShapeDtypeStruct((t, fw), BF16), jax.ShapeDtypeStruct((t, fw), BF16),
                   jax.ShapeDtypeStruct((t, fw), BF16), jax.ShapeDtypeStruct((n_pair, n_q, 2, tq), F32),
                   jax.ShapeDtypeStruct((t, fw), F32)],
        scratch_shapes=[pltpu.VMEM((t, LANES), F32), pltpu.VMEM((n_q, 2, tq), F32)],
        compiler_params=_params("arbitrary", "arbitrary"),
    )(zf, zf, zf, do, c, ct, lse_r, dd_r)


def _adamw_math(w, g, m, v):
    m = ADAM_B1 * m + (1.0 - ADAM_B1) * g
    v = ADAM_B2 * v + (1.0 - ADAM_B2) * jnp.square(g)
    m_hat = m / (1.0 - ADAM_B1 ** ADAM_STEP)
    v_hat = v / (1.0 - ADAM_B2 ** ADAM_STEP)
    delta = -ADAM_LR * (m_hat / (jnp.sqrt(v_hat) + ADAM_EPS) + ADAM_WD * w)
    return delta, m, v


def _adamw(name, w, g, m, v):
    lead, rows, cols = w.shape
    if lead == 1 and cols % LANES:
        outs = _adamw(name, *[jnp.transpose(a, (2, 0, 1)) for a in (w, g, m, v)])
        return [jnp.transpose(o, (1, 2, 0)) for o in outs]
    if lead == 1:
        tile = _tile(rows, (128, 64, 32, 16, 8))
        spec, steps = pl.BlockSpec((1, tile, cols), lambda i: (0, i, 0)), rows // tile
    else:
        tile = _div_tile(lead, 256, 1)
        spec, steps = pl.BlockSpec((tile, rows, cols), lambda i: (i, 0, 0)), lead // tile

    def body(w_ref, g_ref, m_ref, v_ref, d_ref, mo_ref, vo_ref):
        d_ref[...], mo_ref[...], vo_ref[...] = _adamw_math(w_ref[...], g_ref[...], m_ref[...], v_ref[...])

    return pl.pallas_call(
        body, name=name, grid=(steps,), in_specs=[spec] * 4, out_specs=[spec] * 3,
        out_shape=[jax.ShapeDtypeStruct(w.shape, F32)] * 3, compiler_params=_params("parallel"),
    )(w, g, m, v)


def _place():
    return lax.axis_index("x"), lax.axis_index("y"), lax.axis_index("c")


def _other_chips(x, y):
    return [(1 - x, y), (x, 1 - y), (1 - x, 1 - y)]


HBM_SPEC = pl.BlockSpec(memory_space=pltpu.HBM)
SEM_SPEC = pl.BlockSpec(memory_space=pltpu.SEMAPHORE)


def _all_gather_shards(slabs):
    n = len(slabs)

    def body(*refs):
        src_refs, out_refs, send_sems, recv_sems = refs[:n], refs[n:2 * n], refs[2 * n], refs[2 * n + 1]
        x, y, c = _place()
        me = 2 * x + y
        sibling = (x, y, 1 - c)
        chips = _other_chips(x, y)
        first, passed, waits = [], [], []
        for g, (src_ref, out_ref) in enumerate(zip(src_refs, out_refs)):
            rh = src_ref.shape[0] // 2

            def part(chip, half, out_ref=out_ref, rh=rh):
                return out_ref.at[chip, pl.ds(half * rh, rh), :]

            def copy(k, src, dst, to, g=g):
                return pltpu.make_async_remote_copy(src_ref=src, dst_ref=dst, send_sem=send_sems.at[6 * g + k],
                                                    recv_sem=recv_sems.at[6 * g + k], device_id=to, device_id_type=MESH)

            first += [copy(j, src_ref.at[pl.ds(c * rh, rh), :], part(me, c), (px, py, c))
                      for j, (px, py) in enumerate(chips)]
            for j, (px, py) in enumerate(chips):
                theirs = part(2 * px + py, c)
                passed.append((copy(j, theirs, theirs, sibling), copy(3 + j, theirs, theirs, sibling)))
                other = part(2 * px + py, 1 - c)
                waits.append(copy(3 + j, other, other, sibling))
        for cp in first:
            cp.start()
        for landed, forward in passed:
            landed.wait_recv()
            forward.start()
        for cp in waits:
            cp.wait_recv()
        for cp in first + [fwd for _, fwd in passed]:
            cp.wait_send()

    return pl.pallas_call(
        body, name="weights_all_gather", in_specs=[HBM_SPEC] * n, out_specs=[HBM_SPEC] * n,
        out_shape=[jax.ShapeDtypeStruct((N_SHARD,) + a.shape, a.dtype) for a in slabs],
        scratch_shapes=[pltpu.SemaphoreType.DMA((6 * n,)), pltpu.SemaphoreType.DMA((6 * n,))],
    )(*slabs)


def _gather_ici_copies(src_refs, out_refs, send_sems, recv_sems):
    x, y, c = _place()
    me = 2 * x + y
    copies = []
    for g, (src_ref, out_ref) in enumerate(zip(src_refs, out_refs)):
        rh = src_ref.shape[0] // 2
        copies += [pltpu.make_async_remote_copy(
            src_ref=src_ref.at[pl.ds(c * rh, rh), :], dst_ref=out_ref.at[me, pl.ds(c * rh, rh), :],
            send_sem=send_sems.at[3 * g + j], recv_sem=recv_sems.at[3 * g + j], device_id=(px, py, c),
            device_id_type=MESH) for j, (px, py) in enumerate(_other_chips(x, y))]
    return copies


def _gather_start(slabs):
    n = len(slabs)

    def body(*refs):
        for cp in _gather_ici_copies(refs[:n], refs[n:2 * n], refs[2 * n], refs[2 * n + 1]):
            cp.start()
        refs[-1][...] = jnp.zeros_like(refs[-1])

    hbm = lambda a: pltpu.with_memory_space_constraint(a, pltpu.HBM)
    lands = [(N_SHARD,) + a.shape for a in slabs]
    out = pl.pallas_call(
        body, name="weights_gather_start",
        out_shape=(pltpu.SemaphoreType.DMA((3 * n,)), pltpu.SemaphoreType.DMA((3 * n,)),
                   *[pltpu.HBM(a.shape, a.dtype) for a in slabs],
                   *[pltpu.HBM(sh, a.dtype) for sh, a in zip(lands, slabs)], jax.ShapeDtypeStruct((8, LANES), F32)),
        in_specs=[HBM_SPEC] * (2 * n),
        out_specs=(SEM_SPEC, SEM_SPEC, *[HBM_SPEC] * (2 * n), pl.BlockSpec(memory_space=pltpu.VMEM)),
        input_output_aliases={i: 2 + i for i in range(2 * n)},
        compiler_params=pltpu.CompilerParams(has_side_effects=pltpu.SideEffectType.DATAFLOW_SIDE_EFFECTING),
    )(*[hbm(a) for a in slabs], *[hbm(lax.empty(sh, a.dtype)) for sh, a in zip(lands, slabs)])
    return out[0], out[1], list(out[2:2 + n]), list(out[2 + n:2 + 2 * n]), out[-1]


def _gather_wait(send_sems, recv_sems, slabs, landed, after):
    n = len(slabs)

    def body(*refs):
        for cp in _gather_ici_copies(refs[:n], refs[n:2 * n], refs[2 * n], refs[2 * n + 1]):
            cp.wait_send()
            cp.wait_recv()

    out = pl.pallas_call(
        body, name="weights_gather_wait",
        out_shape=[pltpu.HBM(a.shape, a.dtype) for a in slabs + landed],
        in_specs=[HBM_SPEC] * (2 * n) + [SEM_SPEC, SEM_SPEC, pl.BlockSpec(memory_space=pl.ANY)],
        out_specs=[HBM_SPEC] * (2 * n), input_output_aliases={i: i for i in range(2 * n)},
        compiler_params=pltpu.CompilerParams(has_side_effects=pltpu.SideEffectType.DATAFLOW_SIDE_EFFECTING),
    )(*slabs, *landed, send_sems, recv_sems, after)
    return list(out[n:])


def _gather_forward(gathered):
    n = len(gathered)

    def body(*refs):
        in_refs, out_refs, send_sems, recv_sems = refs[:n], refs[n:2 * n], refs[2 * n], refs[2 * n + 1]
        x, y, c = _place()

        def copy(g, j, chip, half):
            rh = in_refs[g].shape[1] // 2
            return pltpu.make_async_remote_copy(
                src_ref=in_refs[g].at[chip, pl.ds(half * rh, rh), :], dst_ref=out_refs[g].at[chip, pl.ds(half * rh, rh), :],
                send_sem=send_sems.at[3 * g + j], recv_sem=recv_sems.at[3 * g + j], device_id=(x, y, 1 - c),
                device_id_type=MESH)

        chips = [2 * px + py for px, py in _other_chips(x, y)]
        for g in range(n):
            for j, chip in enumerate(chips):
                copy(g, j, chip, c).start()
        for g in range(n):
            for j, chip in enumerate(chips):
                copy(g, j, chip, c).wait_send()
                copy(g, j, chip, 1 - c).wait_recv()

    return pl.pallas_call(
        body, name="weights_gather_forward", in_specs=[HBM_SPEC] * n, out_specs=[HBM_SPEC] * n,
        out_shape=[jax.ShapeDtypeStruct(a.shape, a.dtype) for a in gathered],
        input_output_aliases={g: g for g in range(n)},
        scratch_shapes=[pltpu.SemaphoreType.DMA((3 * n,)), pltpu.SemaphoreType.DMA((3 * n,))],
    )(*gathered)


def _chip_index():
    return jnp.reshape(2 * lax.axis_index("x") + lax.axis_index("y"), (1,)).astype(jnp.int32)


def _place_own_shard(name, gathered, slab):
    rows, width = slab.shape
    tile = _div_tile(rows, 256, 16)

    def body(me_ref, s_ref, g_ref, o_ref):
        o_ref[0] = s_ref[...]

    return pl.pallas_call(
        body, name=name,
        grid_spec=pltpu.PrefetchScalarGridSpec(
            num_scalar_prefetch=1, grid=(rows // tile,),
            in_specs=[pl.BlockSpec((tile, width), lambda i, me: (i, 0)), pl.BlockSpec(memory_space=pl.ANY)],
            out_specs=pl.BlockSpec((1, tile, width), lambda i, me: (me[0], i, 0))),
        out_shape=jax.ShapeDtypeStruct(gathered.shape, gathered.dtype), input_output_aliases={2: 0},
        compiler_params=_params("parallel"),
    )(_chip_index(), slab, gathered)


def _sibling_exchange(sent):
    n = len(sent)

    def body(*refs):
        g_refs, out_refs, send_sems, recv_sems = refs[:n], refs[n:2 * n], refs[2 * n], refs[2 * n + 1]
        x, y, c = _place()
        copies = [pltpu.make_async_remote_copy(
            src_ref=g_ref.at[s], dst_ref=out_ref.at[s], send_sem=send_sems.at[N_SHARD * g + s],
            recv_sem=recv_sems.at[N_SHARD * g + s], device_id=(x, y, 1 - c), device_id_type=MESH)
            for g, (g_ref, out_ref) in enumerate(zip(g_refs, out_refs)) for s in range(N_SHARD)]
        for cp in copies:
            cp.start()
        for cp in copies:
            cp.wait()

    return pl.pallas_call(
        body, name="grad_sibling_exchange", in_specs=[HBM_SPEC] * n, out_specs=[HBM_SPEC] * n,
        out_shape=[jax.ShapeDtypeStruct(g.shape, g.dtype) for g in sent],
        scratch_shapes=[pltpu.SemaphoreType.DMA((N_SHARD * n,)), pltpu.SemaphoreType.DMA((N_SHARD * n,))],
    )(*sent)


def _add_sibling(name, kept, got):
    _, rh, width = kept.shape
    tile = _div_tile(rh, 256, 16)

    def body(a_ref, b_ref, o_ref):
        o_ref[...] = (a_ref[...] + b_ref[...].astype(F32)).astype(o_ref.dtype)

    spec = pl.BlockSpec((1, tile, width), lambda s, i: (s, i, 0))
    return pl.pallas_call(
        body, name=name, grid=(N_SHARD, rh // tile), in_specs=[spec, spec], out_specs=spec,
        out_shape=jax.ShapeDtypeStruct(kept.shape, BF16), compiler_params=_params("parallel", "parallel"),
    )(kept, got)


def _exchange_copies(p_refs, land_refs, send_sems, recv_sems):
    x, y, c = _place()
    me = 2 * x + y
    return [pltpu.make_async_remote_copy(
        src_ref=p_ref.at[2 * px + py], dst_ref=land_ref.at[me], send_sem=send_sems.at[3 * g + j],
        recv_sem=recv_sems.at[3 * g + j], device_id=(px, py, c), device_id_type=MESH)
        for g, (p_ref, land_ref) in enumerate(zip(p_refs, land_refs)) for j, (px, py) in enumerate(_other_chips(x, y))]


def _chip_exchange_start(ps):
    n = len(ps)

    def body(*refs):
        for cp in _exchange_copies(refs[:n], refs[n:2 * n], refs[2 * n], refs[2 * n + 1]):
            cp.start()
        refs[-1][...] = jnp.zeros_like(refs[-1])

    hbm = lambda a: pltpu.with_memory_space_constraint(a, pltpu.HBM)
    out = pl.pallas_call(
        body, name="grad_chip_exchange_start",
        out_shape=(pltpu.SemaphoreType.DMA((3 * n,)), pltpu.SemaphoreType.DMA((3 * n,)),
                   *[pltpu.HBM(a.shape, a.dtype) for a in ps], *[pltpu.HBM(a.shape, a.dtype) for a in ps],
                   jax.ShapeDtypeStruct((8, LANES), F32)),
        in_specs=[HBM_SPEC] * (2 * n),
        out_specs=(SEM_SPEC, SEM_SPEC, *[HBM_SPEC] * (2 * n), pl.BlockSpec(memory_space=pltpu.VMEM)),
        input_output_aliases={i: 2 + i for i in range(2 * n)},
        compiler_params=pltpu.CompilerParams(has_side_effects=pltpu.SideEffectType.DATAFLOW_SIDE_EFFECTING),
    )(*[hbm(a) for a in ps], *[hbm(lax.empty(a.shape, a.dtype)) for a in ps])
    return out[0], out[1], list(out[2:2 + n]), list(out[2 + n:2 + 2 * n]), out[-1]


def _chip_exchange_wait(send_sems, recv_sems, ps, landed, after):
    n = len(ps)

    def body(*refs):
        for cp in _exchange_copies(refs[:n], refs[n:2 * n], refs[2 * n], refs[2 * n + 1]):
            cp.wait_send()
            cp.wait_recv()

    out = pl.pallas_call(
        body, name="grad_chip_exchange_wait",
        out_shape=[pltpu.HBM(a.shape, a.dtype) for a in ps + landed],
        in_specs=[HBM_SPEC] * (2 * n) + [SEM_SPEC, SEM_SPEC, pl.BlockSpec(memory_space=pl.ANY)],
        out_specs=[HBM_SPEC] * (2 * n), input_output_aliases={i: i for i in range(2 * n)},
        compiler_params=pltpu.CompilerParams(has_side_effects=pltpu.SideEffectType.DATAFLOW_SIDE_EFFECTING),
    )(*ps, *landed, send_sems, recv_sems, after)
    return list(out[:n]), list(out[n:])


def _sum_chips(name, p, got):
    _, rh, width = p.shape
    tile = _div_tile(rh, 256, 16)
    n_t = rh // tile
    place = jnp.stack([2 * lax.axis_index("x") + lax.axis_index("y"), lax.axis_index("c")]).astype(jnp.int32)

    def body(pl_ref, own_ref, r0, r1, r2, r3, o_ref):
        me = pl_ref[0]
        own = own_ref[0].astype(F32)
        t = [jnp.where(me == s, own, r[0].astype(F32)) for s, r in enumerate((r0, r1, r2, r3))]
        o_ref[...] = ((t[0] + t[1]) + t[2]) + t[3]

    def slot(s):
        return pl.BlockSpec((1, tile, width), lambda i, pc: (jnp.where(pc[0] == s, (s + 1) % N_SHARD, s), i, 0))

    return pl.pallas_call(
        body, name=name,
        grid_spec=pltpu.PrefetchScalarGridSpec(
            num_scalar_prefetch=1, grid=(n_t,),
            in_specs=[pl.BlockSpec((1, tile, width), lambda i, pc: (pc[0], i, 0))] + [slot(s) for s in range(N_SHARD)],
            out_specs=pl.BlockSpec((tile, width), lambda i, pc: (pc[1] * n_t + i, 0))),
        out_shape=jax.ShapeDtypeStruct((2 * rh, width), F32), compiler_params=_params("parallel"),
    )(place, p, got, got, got, got)


def _join_halves(fulls):
    n = len(fulls)

    def body(*refs):
        f_refs, out_refs, send_sems, recv_sems = refs[:n], refs[n:2 * n], refs[2 * n], refs[2 * n + 1]
        x, y, c = _place()

        def copy(g, half):
            rh = f_refs[g].shape[0] // 2
            return pltpu.make_async_remote_copy(
                src_ref=f_refs[g].at[pl.ds(half * rh, rh), :], dst_ref=out_refs[g].at[pl.ds(half * rh, rh), :],
                send_sem=send_sems.at[g], recv_sem=recv_sems.at[g], device_id=(x, y, 1 - c), device_id_type=MESH)

        for g in range(n):
            copy(g, c).start()
        for g in range(n):
            copy(g, c).wait_send()
            copy(g, 1 - c).wait_recv()

    return pl.pallas_call(
        body, name="grad_join_halves", in_specs=[HBM_SPEC] * n, out_specs=[HBM_SPEC] * n,
        out_shape=[jax.ShapeDtypeStruct(f.shape, f.dtype) for f in fulls],
        input_output_aliases={g: g for g in range(n)},
        scratch_shapes=[pltpu.SemaphoreType.DMA((n,)), pltpu.SemaphoreType.DMA((n,))],
    )(*fulls)


def _reduce_scatter_start(gs):
    c = lax.axis_index("c")
    gs = [g.reshape(N_SHARD, 2, g.shape[1] // 2, g.shape[2]) for g in gs]
    kept = [lax.dynamic_index_in_dim(g, c, axis=1, keepdims=False) for g in gs]
    got = _sibling_exchange([lax.dynamic_index_in_dim(g, 1 - c, axis=1, keepdims=False).astype(BF16) for g in gs])
    chip_sums = [_add_sibling(f"grad_add_sibling_{i}", k, r) for i, (k, r) in enumerate(zip(kept, got))]
    *state, token = _chip_exchange_start(chip_sums)
    return state, token


def _reduce_scatter_end(state, after):
    chip_sums, landed = _chip_exchange_wait(*state, after)
    return _join_halves([_sum_chips(f"grad_sum_chips_{i}", p, r) for i, (p, r) in enumerate(zip(chip_sums, landed))])


def _all_reduce_small(v):
    rows = v.shape[0]

    def body(v_ref, out_ref, gather, send_sems, recv_sems):
        x, y, c = _place()
        gather[4 * x + 2 * y + c] = v_ref[...]
        flips = [(dx, dy, dc) for dx in (0, 1) for dy in (0, 1) for dc in (0, 1)][1:]
        peers = [((x + dx) % 2, (y + dy) % 2, (c + dc) % 2) for dx, dy, dc in flips]
        copies = [pltpu.make_async_remote_copy(
            src_ref=v_ref, dst_ref=gather.at[4 * x + 2 * y + c], send_sem=send_sems.at[j], recv_sem=recv_sems.at[j],
            device_id=peer, device_id_type=MESH) for j, peer in enumerate(peers)]
        for cp in copies:
            cp.start()
        for j, (px, py, pc) in enumerate(peers):
            pltpu.make_async_remote_copy(
                src_ref=v_ref, dst_ref=gather.at[4 * px + 2 * py + pc], send_sem=send_sems.at[j],
                recv_sem=recv_sems.at[j], device_id=(px, py, pc), device_id_type=MESH).wait_recv()
        for cp in copies:
            cp.wait_send()
        acc = gather[0]
        for d in range(1, 8):
            acc = acc + gather[d]
        out_ref[...] = acc

    vm = pl.BlockSpec(memory_space=pltpu.VMEM)
    return pl.pallas_call(
        body, name="small_grads_all_reduce", in_specs=[vm], out_specs=vm,
        out_shape=jax.ShapeDtypeStruct(v.shape, F32),
        scratch_shapes=[pltpu.VMEM((8, rows, LANES), F32), pltpu.SemaphoreType.DMA((7,)), pltpu.SemaphoreType.DMA((7,))],
    )(v)


def _pad_lanes(v):
    v = v.reshape(1, -1)
    return jnp.pad(v, ((0, 0), (0, -v.shape[1] % LANES)))


def _pack_small(vs, rows):
    flat = jnp.concatenate([_pad_lanes(v) for v in vs], axis=1)
    return jnp.pad(flat, ((0, 0), (0, rows * LANES - flat.shape[1]))).reshape(rows, LANES)


def _unpack_small(packed, shapes):
    flat = packed.reshape(-1)
    out, off = [], 0
    for s in shapes:
        n = 1
        for d in s:
            n *= d
        out.append(flat[off:off + n].reshape(s))
        off += n + (-n % LANES)
    return out


BIG = ("w_in", "rw_w_lora_up", "rw_a_lora_up", "w_up_rwkv", "w_up_fox", "w_out", "ple_proj", "ple_gate_w")
ROW_SHARDED = ("w_out", "ple_gate_w")
FIRST_NEEDED = ("w_in", "rw_w_lora_up", "rw_a_lora_up")
SMALL = ("norm_g", "rw_shift_mu", "rw_w0", "rw_a0", "rw_k_k", "rw_k_a", "rw_r_k", "rw_ln_g", "rw_ln_b", "fox_b_f",
         "ple_norm_g", "final_norm_g")
WEIGHTS = ("norm_g", "w_in", "rw_shift_mu", "rw_w0", "rw_w_lora_up", "rw_a0", "rw_a_lora_up", "rw_k_k", "rw_k_a",
           "rw_r_k", "rw_ln_g", "rw_ln_b", "fox_b_f", "w_up_rwkv", "w_up_fox", "w_out", "ple_proj", "ple_gate_w",
           "ple_norm_g", "final_norm_g")


def kernel(x, p, norm_g, w_in, rw_shift_mu, rw_w0, rw_w_lora_up, rw_a0, rw_a_lora_up, rw_k_k, rw_k_a, rw_r_k, rw_ln_g, rw_ln_b, fox_b_f, w_up_rwkv, w_up_fox, w_out, ple_proj, ple_gate_w, ple_norm_g, final_norm_g, loss_target, m_norm_g, m_w_in, m_rw_shift_mu, m_rw_w0, m_rw_w_lora_up, m_rw_a0, m_rw_a_lora_up, m_rw_k_k, m_rw_k_a, m_rw_r_k, m_rw_ln_g, m_rw_ln_b, m_fox_b_f, m_w_up_rwkv, m_w_up_fox, m_w_out, m_ple_proj, m_ple_gate_w, m_ple_norm_g, m_final_norm_g, v_norm_g, v_w_in, v_rw_shift_mu, v_rw_w0, v_rw_w_lora_up, v_rw_a0, v_rw_a_lora_up, v_rw_k_k, v_rw_k_a, v_rw_r_k, v_rw_ln_g, v_rw_ln_b, v_fox_b_f, v_w_up_rwkv, v_w_up_fox, v_w_out, v_ple_proj, v_ple_gate_w, v_ple_norm_g, v_final_norm_g):
    wts = dict(norm_g=norm_g, w_in=w_in, rw_shift_mu=rw_shift_mu, rw_w0=rw_w0, rw_w_lora_up=rw_w_lora_up, rw_a0=rw_a0,
               rw_a_lora_up=rw_a_lora_up, rw_k_k=rw_k_k, rw_k_a=rw_k_a, rw_r_k=rw_r_k, rw_ln_g=rw_ln_g, rw_ln_b=rw_ln_b,
               fox_b_f=fox_b_f, w_up_rwkv=w_up_rwkv, w_up_fox=w_up_fox, w_out=w_out, ple_proj=ple_proj,
               ple_gate_w=ple_gate_w, ple_norm_g=ple_norm_g, final_norm_g=final_norm_g)
    mom = dict(norm_g=m_norm_g, w_in=m_w_in, rw_shift_mu=m_rw_shift_mu, rw_w0=m_rw_w0, rw_w_lora_up=m_rw_w_lora_up,
               rw_a0=m_rw_a0, rw_a_lora_up=m_rw_a_lora_up, rw_k_k=m_rw_k_k, rw_k_a=m_rw_k_a, rw_r_k=m_rw_r_k,
               rw_ln_g=m_rw_ln_g, rw_ln_b=m_rw_ln_b, fox_b_f=m_fox_b_f, w_up_rwkv=m_w_up_rwkv, w_up_fox=m_w_up_fox,
               w_out=m_w_out, ple_proj=m_ple_proj, ple_gate_w=m_ple_gate_w, ple_norm_g=m_ple_norm_g,
               final_norm_g=m_final_norm_g)
    vel = dict(norm_g=v_norm_g, w_in=v_w_in, rw_shift_mu=v_rw_shift_mu, rw_w0=v_rw_w0, rw_w_lora_up=v_rw_w_lora_up,
               rw_a0=v_rw_a0, rw_a_lora_up=v_rw_a_lora_up, rw_k_k=v_rw_k_k, rw_k_a=v_rw_k_a, rw_r_k=v_rw_r_k,
               rw_ln_g=v_rw_ln_g, rw_ln_b=v_rw_ln_b, fox_b_f=v_fox_b_f, w_up_rwkv=v_w_up_rwkv, w_up_fox=v_w_up_fox,
               w_out=v_w_out, ple_proj=v_ple_proj, ple_gate_w=v_ple_gate_w, ple_norm_g=v_ple_norm_g,
               final_norm_g=v_final_norm_g)

    t, d = x.shape[1], x.shape[2]
    cw = rw_w0.shape[1]
    lr = rw_w_lora_up.shape[1]
    fh = fox_b_f.shape[1]
    fw = fh * HEAD
    rw_cols = 4 * cw + 2 * lr
    fox_cols = 4 * fw + fh
    assert 2 * lr == LANES and cw % LANES == 0 and fw % LANES == 0 and fh <= LANES
    xs = x[0]
    ps = p[0, 0]
    tgt = loss_target[0]

    groups = {}
    for n in BIG:
        groups.setdefault(wts[n].shape[2], []).append(n)
    groups = list(groups.values())
    slabs16 = []
    for gi, names in enumerate(groups):
        slab = jnp.concatenate([wts[n][0] for n in names], axis=0)
        rows, width = slab.shape
        tile_c = _div_tile(rows, 256, 32)
        slabs16 += _rows(f"weights_to_bf16_{gi}", lambda i, a: (a,), rows, tile_c, [(slab, _row_spec(tile_c, width))],
                         [(width, BF16)])
    first = [gi for gi, names in enumerate(groups) if not set(names).isdisjoint(FIRST_NEEDED)]
    later = [gi for gi in range(len(groups)) if gi not in first]
    gathered = dict(zip(first, _all_gather_shards([slabs16[gi] for gi in first])))
    *gather_state, gather_token = _gather_start([slabs16[gi] for gi in later])
    full = {}

    def assemble(gis):
        for gi in gis:
            g = _place_own_shard(f"weights_place_own_{gi}", gathered[gi], slabs16[gi])
            off = 0
            for n in groups[gi]:
                r = wts[n].shape[1]
                part = g[:, off:off + r, :]
                full[n] = (part.reshape(N_SHARD * r, -1) if n in ROW_SHARDED
                           else jnp.concatenate([part[s] for s in range(N_SHARD)], axis=1))
                off += r

    assemble(first)
    w_rw = full["w_in"][:, :rw_cols]
    w_fox = jnp.pad(full["w_in"][:, rw_cols:rw_cols + fox_cols], ((0, 0), (0, LANES - fh)))
    w_gate = full["w_in"][:, rw_cols + fox_cols:]
    wup_pad = jnp.pad(full["rw_w_lora_up"], ((0, lr), (0, 0)))
    aup_pad = jnp.pad(full["rw_a_lora_up"], ((lr, 0), (0, 0)))
    b_pad = _pad_lanes(fox_b_f)
    r_k_row = rw_r_k.reshape(1, cw)
    gf_row = final_norm_g.reshape(1, d)

    tile = _tile(t, (256, 128, 64, 32, 16, 8))
    tile_s = _tile(t, (128, 64, 32, 16, 8))
    n_s = t // tile_s
    full2 = lambda a: (a, _full_spec(a.shape))

    (h,) = _rows("norm1", lambda i, a, g: (_rms(a, g),), t, tile, [(xs, _row_spec(tile, d)), full2(norm_g)], [(d, BF16)])
    z_rw = _matmul("proj_rw", h, w_rw, after=gather_token)
    z_fox = _matmul("proj_fox", h, w_fox, after=z_rw[:8, :LANES])
    z_gate = _matmul("proj_gate", h, w_gate, after=z_fox[:8, :LANES])
    gathered.update(zip(later, _gather_forward(_gather_wait(*gather_state, z_gate))))
    assemble(later)

    pre_consts = [full2(rw_shift_mu), full2(rw_w0), full2(rw_a0), full2(wup_pad), full2(aup_pad), full2(rw_k_k),
                  full2(rw_k_a)]

    def pre_fwd(i, z, prev8, *consts):
        return _rw_pre(z, _shifted(i, z, prev8), *consts, cw=cw)

    r_, w_, k_, v_, kk_, a_, g_ = _rows(
        "rwkv_pre", pre_fwd, t, tile_s,
        [(z_rw, _row_spec(tile_s, rw_cols)), (z_rw, _prev_rows_spec(tile_s, rw_cols))] + pre_consts, [(cw, F32)] * 7)
    y_scan, states, tinvs = _scan_fwd(r_, w_, k_, v_, kk_, a_)
    post_consts = [full2(rw_ln_g), full2(rw_ln_b), full2(r_k_row)]
    post_rows = lambda *arrs: [(a, _row_spec(tile_s, cw)) for a in arrs]
    (y_rw,) = _rows("rwkv_post", lambda i, *a: (_rw_post(*a, cw=cw),), t, tile_s,
                    post_rows(y_scan, r_, k_, v_, g_) + post_consts, [(cw, BF16)])

    c_fox = _fox_cumsum(z_fox, b_pad, fw=fw, fh=fh)
    tq = _fox_tile(t)
    n_pair_f = fw // LANES
    head_rows = lambda a: a.T.reshape(n_pair_f, 2, t // tq, tq).transpose(0, 2, 1, 3)
    head_cols = lambda a: a.transpose(0, 2, 1, 3).reshape(fh, t).T
    ct_fox = head_rows(c_fox[:, :fh])
    o_fox, lse_fox, y_fox = _fox_fwd(z_fox, c_fox, ct_fox, fw=fw)

    u_rw = _matmul("up_rwkv", y_rw, full["w_up_rwkv"])
    u_fox = _matmul("up_fox", y_fox, full["w_up_fox"])
    (merged,) = _rows("merge", lambda i, zg, a, b: (_merge(zg, a, b, d=d),), t, tile,
                      [(z_gate, _row_spec(tile, 2 * d)), (u_rw, _row_spec(tile, d)), (u_fox, _row_spec(tile, d))],
                      [(d, BF16)])
    x1 = _matmul("out_proj", merged, full["w_out"], add=xs)
    (n2,) = _rows("norm2", lambda i, a, g: (_rms(a, g),), t, tile, [(x1, _row_spec(tile, d)), full2(ple_norm_g)],
                  [(d, BF16)])
    gl = _matmul("ple_gate", n2, full["ple_gate_w"])
    ple = _matmul("ple_proj", ps, full["ple_proj"])

    def head_bwd(i, x1_t, ple_t, gl_t, gf, tg):
        loss, vjp = jax.vjp(lambda a, b, cc, g: _head_loss(a, b, cc, g, tg), x1_t, ple_t, gl_t, gf)
        dx1, dple, dgl, dgf = vjp(jnp.ones((1, 1), F32))
        return dx1, dple, dgl, jnp.broadcast_to(loss, (1, LANES)), dgf

    dx2, dple, dgl, loss_row, d_gf = _rows(
        "loss_head", head_bwd, t, tile_s,
        [(x1, _row_spec(tile_s, d)), (ple, _row_spec(tile_s, d)), (gl, _row_spec(tile_s, d)), full2(gf_row),
         (tgt, _row_spec(tile_s, d))],
        [(d, F32), (d, BF16), (d, BF16)], [(1, LANES), (1, d)])

    g_ple_proj = _matmul("d_ple_proj", ps, dple, ta=True)
    g_ple_gate = _matmul("d_ple_gate_w", n2, dgl, ta=True)
    dn2 = _matmul("d_n2", dgl, full["ple_gate_w"], tb=True)

    def norm_bwd(i, a, g, dh, res):
        _, vjp = jax.vjp(_rms, a, g)
        da, dg = vjp(dh)
        return res + da, dg

    dx1, d_g2 = _rows("norm2_bwd", norm_bwd, t, tile_s,
                      [(x1, _row_spec(tile_s, d)), full2(ple_norm_g), (dn2, _row_spec(tile_s, d)),
                       (dx2, _row_spec(tile_s, d))], [(d, F32)], [(1, d)])
    g_w_out = _matmul("d_w_out", merged, dx1, ta=True)
    dmerged = _matmul("d_merged", dx1, full["w_out"], tb=True)

    def merge_bwd(i, zg, a, b, dm):
        _, vjp = jax.vjp(functools.partial(_merge, d=d), zg, a, b)
        return vjp(dm)

    dz_gate, du_rw, du_fox = _rows(
        "merge_bwd", merge_bwd, t, tile_s,
        [(z_gate, _row_spec(tile_s, 2 * d)), (u_rw, _row_spec(tile_s, d)), (u_fox, _row_spec(tile_s, d)),
         (dmerged, _row_spec(tile_s, d))], [(2 * d, BF16), (d, BF16), (d, BF16)])
    g_up_rw = _matmul("d_w_up_rwkv", y_rw, du_rw, ta=True)
    g_up_fox = _matmul("d_w_up_fox", y_fox, du_fox, ta=True)
    dy_rw = _matmul("d_y_rwkv", du_rw, full["w_up_rwkv"], tb=True)
    dy_fox = _matmul("d_y_fox", du_fox, full["w_up_fox"], tb=True)

    def post_bwd(i, y, r, k, v, g, ln_g, ln_b, r_k, dy):
        _, vjp = jax.vjp(functools.partial(_rw_post, cw=cw), y, r, k, v, g, ln_g, ln_b, r_k)
        return vjp(dy)

    dys, dr1, dk1, dv1, dg1, d_ln_g, d_ln_b, d_r_k = _rows(
        "rwkv_post_bwd", post_bwd, t, tile_s,
        post_rows(y_scan, r_, k_, v_, g_) + post_consts + post_rows(dy_rw), [(cw, F32)] * 5, [(1, cw)] * 3)
    dr2, dw2, dk2, dv2, dkk2, da2 = _scan_bwd(r_, w_, k_, v_, kk_, a_, states, tinvs, dys)

    def pre_bwd(i, z, prev8, mu, w0, a0, wup, aup, k_k, k_a, dr_a, dr_b, dk_a, dk_b, dv_a, dv_b, dw, dkk, da, dg):
        zp = _shifted(i, z, prev8)
        _, vjp = jax.vjp(functools.partial(_rw_pre, cw=cw), z, zp, mu, w0, a0, wup, aup, k_k, k_a)
        dz, dzp, dmu, dw0, da0, dwup, daup, dk_k, dk_a = vjp((dr_a + dr_b, dw, dk_a + dk_b, dv_a + dv_b, dkk, da, dg))
        row = lax.broadcasted_iota(jnp.int32, dz.shape, 0)
        dz = dz + jnp.where(row < tile_s - 1, pltpu.roll(dzp, tile_s - 1, 0), 0.0)
        first = jnp.where(lax.broadcasted_iota(jnp.int32, (8, dz.shape[1]), 0) == 0, _row_of(dzp, 0), 0.0)
        return dz, first, dmu, dw0, da0, dwup, daup, dk_k, dk_a

    def pre_bwd_call():
        n_in = 2 + len(pre_consts) + 10
        ins = ([(z_rw, _row_spec(tile_s, rw_cols)), (z_rw, _prev_rows_spec(tile_s, rw_cols))] + pre_consts
               + post_rows(dr1, dr2, dk1, dk2, dv1, dv2, dw2, dkk2, da2, dg1))

        def body(*refs):
            i = pl.program_id(0)
            vals = pre_bwd(i, *[r[...] for r in refs[:n_in]])
            refs[n_in][...] = vals[0]
            refs[n_in + 1][...] = vals[1]
            for r, v in zip(refs[n_in + 2:], vals[2:]):
                @pl.when(i == 0)
                def _(r=r, v=v):
                    r[...] = v

                @pl.when(i > 0)
                def _(r=r, v=v):
                    r[...] += v

        acc_shapes = [(1, rw_cols), (1, cw), (1, cw), (LANES, cw), (LANES, cw), (1, cw), (1, cw)]
        return pl.pallas_call(
            body, name="rwkv_pre_bwd", grid=(n_s,), in_specs=[s for _, s in ins],
            out_specs=[_row_spec(tile_s, rw_cols), pl.BlockSpec((8, rw_cols), lambda i: (i, 0))]
            + [_full_spec(s) for s in acc_shapes],
            out_shape=[jax.ShapeDtypeStruct((t, rw_cols), F32), jax.ShapeDtypeStruct((8 * n_s, rw_cols), F32)]
            + [jax.ShapeDtypeStruct(s, F32) for s in acc_shapes],
            compiler_params=_params("arbitrary"),
        )(*[a for a, _ in ins])

    dz_main, dz_first, d_mu, d_w0, d_a0, d_wup, d_aup, d_k_k, d_k_a = pre_bwd_call()

    def add_next_row(i, dz, nxt8):
        row = lax.broadcasted_iota(jnp.int32, dz.shape, 0)
        carry = jnp.where(i < n_s - 1, _row_of(nxt8, 0), 0.0)
        return (dz + jnp.where(row == tile_s - 1, carry, 0.0),)

    (dz_rw,) = _rows("rwkv_shift_bwd", add_next_row, t, tile_s,
                     [(dz_main, _row_spec(tile_s, rw_cols)), (dz_first, _next_rows_spec(tile_s, rw_cols, n_s))],
                     [(rw_cols, BF16)])

    def fox_post_bwd(i, o, g, dy):
        _, vjp = jax.vjp(lambda oo, gg: oo * (gg * jax.nn.sigmoid(gg)), o, g)
        do, dg = vjp(dy)
        return do, _head_sum(do * o, _head_matrix(fw)), dg

    do_fox, dd_fox, dg_fox = _rows(
        "fox_post_bwd", fox_post_bwd, t, tile_s,
        [(o_fox, _row_spec(tile_s, fw)), (z_fox, _row_spec(tile_s, fw, 3)), (dy_fox, _row_spec(tile_s, fw))],
        [(fw, F32), (fw, F32), (fw, BF16)])
    dq_f, dk_f, dv_f, dcq, dck = _fox_bwd(z_fox, do_fox, c_fox, ct_fox, head_rows(lse_fox[:, ::HEAD]),
                                          head_rows(dd_fox[:, ::HEAD]), fw=fw)
    dc = head_cols(dcq) + dck.reshape(t, n_pair_f, LANES)[:, :, :2].reshape(t, fh)
    dfl, d_bf = _fox_cumsum_bwd(z_fox, b_pad, jnp.pad(dc, ((0, 0), (0, LANES - fh))), fw=fw, fh=fh)
    dz_fox = jnp.concatenate([dq_f, dk_f, dv_f, dg_fox, dfl], axis=1)

    g_w_rw = _matmul("d_w_in_rw", h, dz_rw, ta=True)
    g_w_fox = _matmul("d_w_in_fox", h, dz_fox, ta=True)
    g_w_gate = _matmul("d_w_in_gate", h, dz_gate, ta=True)

    g_full = {
        "w_in": jnp.concatenate([g_w_rw, g_w_fox[:, :fox_cols], g_w_gate], axis=1),
        "rw_w_lora_up": d_wup[:lr], "rw_a_lora_up": d_aup[lr:], "w_up_rwkv": g_up_rw, "w_up_fox": g_up_fox,
        "w_out": g_w_out, "ple_proj": g_ple_proj, "ple_gate_w": g_ple_gate,
    }
    def by_shard(n):
        g = g_full[n]
        if n in ROW_SHARDED:
            return g.reshape(N_SHARD, g.shape[0] // N_SHARD, g.shape[1])
        return jnp.stack(jnp.split(g, N_SHARD, axis=1))

    rs_state, token = _reduce_scatter_start([jnp.concatenate([by_shard(n) for n in names], axis=1) for names in groups])
    dh = _matmul("d_h_rw", dz_rw, w_rw, tb=True, after=token)
    dh = _matmul("d_h_fox", dz_fox, w_fox, tb=True, add=dh)
    dh = _matmul("d_h_gate", dz_gate, w_gate, tb=True, add=dh)
    grad_x, d_g1 = _rows("norm1_bwd", norm_bwd, t, tile_s,
                         [(xs, _row_spec(tile_s, d)), full2(norm_g), (dh, _row_spec(tile_s, d)),
                          (dx1, _row_spec(tile_s, d))], [(d, F32)], [(1, d)])
    small_parts = dict(norm_g=d_g1, rw_shift_mu=d_mu, rw_w0=d_w0, rw_a0=d_a0, rw_k_k=d_k_k, rw_k_a=d_k_a, rw_r_k=d_r_k,
                       rw_ln_g=d_ln_g, rw_ln_b=d_ln_b, fox_b_f=d_bf[:, :fh], ple_norm_g=d_g2, final_norm_g=d_gf)
    n_small = sum(-(-wts[n].size // LANES) for n in SMALL)
    small_rows = -(-n_small // 8) * 8
    small_shapes = [wts[n].shape for n in SMALL]
    g_small = _all_reduce_small(_pack_small([small_parts[n] for n in SMALL], small_rows))
    reduced = _reduce_scatter_end(rs_state, g_small)
    grads = {}
    for names, g in zip(groups, reduced):
        off = 0
        for n in names:
            r = wts[n].shape[1]
            grads[n] = g[off:off + r][None]
            off += r

    for n, g in zip(SMALL, _unpack_small(g_small, small_shapes)):
        grads[n] = g

    delta, new_m, new_v = {}, {}, {}
    for n in BIG:
        delta[n], new_m[n], new_v[n] = _adamw("adamw_" + n, wts[n], grads[n], mom[n], vel[n])
    packed = lambda src: _pack_small([src[n] for n in SMALL], small_rows)[None]
    for store, out in zip((delta, new_m, new_v), _adamw("adamw_small", packed(wts), g_small[None], packed(mom), packed(vel))):
        for n, a in zip(SMALL, _unpack_small(out[0], small_shapes)):
            store[n] = a

    loss = lax.psum(loss_row[0, 0], ("x", "y", "c"))
    return (loss, grad_x[None], *[grads[n] for n in WEIGHTS], *[delta[n] for n in WEIGHTS],
            *[new_m[n] for n in WEIGHTS], *[new_v[n] for n in WEIGHTS])
```

```python
import functools

import jax
import jax.numpy as jnp
from jax import lax
from jax.experimental import pallas as pl
from jax.experimental.pallas import tpu as pltpu

F32 = jnp.float32
BF16 = jnp.bfloat16
HIGHEST = lax.Precision.HIGHEST
SCAN_PREC = lax.Precision.HIGH
MESH = pl.DeviceIdType.MESH

LANES = 128
HEAD = 64
NORM_EPS = 1e-6
GN_EPS = 64e-5
ADAM_LR = 0.001
ADAM_B1 = 0.9
ADAM_B2 = 0.999
ADAM_EPS = 1e-08
ADAM_WD = 0.01
ADAM_STEP = 10
N_SHARD = 4
VMEM_LIMIT = 56 * 1024 * 1024
MATMUL_VMEM_BUDGET = 40 * 1024 * 1024
PAIRS_PER_STEP = 8
ELIM_BASE = 8


def _params(*sem):
    return pltpu.CompilerParams(dimension_semantics=sem, vmem_limit_bytes=VMEM_LIMIT)


def _tile(n, cands):
    for c in cands:
        if c <= n and n % c == 0:
            return c
    return n


def _div_tile(n, cap, mult):
    return max(c for c in range(mult, min(n, cap) + 1, mult) if n % c == 0)


_ROW_TILES = (512, 256, 128, 64, 32, 16, 8)


def _dot(a, b, prec=None):
    return lax.dot_general(a, b, (((1,), (0,)), ((), ())), precision=prec, preferred_element_type=F32)


def _dot_nt(a, b, prec=None):
    return lax.dot_general(a, b, (((1,), (1,)), ((), ())), precision=prec, preferred_element_type=F32)


def _dot_tn(a, b, prec=None):
    return lax.dot_general(a, b, (((0,), (0,)), ((), ())), precision=prec, preferred_element_type=F32)


@jax.custom_vjp
def _bdot(x, w):
    return _dot(x.astype(BF16), w.astype(BF16))


def _bdot_fwd(x, w):
    return _bdot(x, w), (x, w)


def _bdot_bwd(res, ct):
    x, w = res
    return _dot_nt(ct.astype(BF16), w.astype(BF16)), _dot_tn(x.astype(BF16), ct.astype(BF16))


_bdot.defvjp(_bdot_fwd, _bdot_bwd)


def _head_matrix(width):
    c = lax.broadcasted_iota(jnp.int32, (width, LANES), 0)
    h = lax.broadcasted_iota(jnp.int32, (width, LANES), 1)
    return (c // HEAD == h).astype(F32)


def _head_sum(x, e):
    return _dot_nt(_dot(x, e, SCAN_PREC), e, SCAN_PREC)


def _softplus(x):
    return jnp.maximum(x, 0.0) + jnp.log1p(jnp.exp(-jnp.abs(x)))


def _lane_pick(x, idx):
    lane = lax.broadcasted_iota(jnp.int32, x.shape, 1)
    return jnp.sum(jnp.where(lane == idx, x, 0.0), axis=1, keepdims=True)


def _matmul_tiles(m, n, k, a_bytes, b_bytes, out_bytes, has_add):
    best = None
    for tk in [c for c in (k, 2048, 1408, 1024, 768, 640, 512, 384, 256, 128, 64, 32, 16) if c <= k and k % c == 0]:
        for tm in [c for c in (1024, 512, 256, 128) if m % c == 0] or [m]:
            for tn in [c for c in (1408, 1024, 768, 640, 512, 384, 256, 128) if n % c == 0] or [n]:
                nk = k // tk
                tile_out = tm * tn
                vmem = 2 * (tm * tk * a_bytes + tk * tn * b_bytes + tile_out * (out_bytes + 4 * has_add))
                vmem += tile_out * 4 * (2 if nk > 1 else 1)
                if vmem > MATMUL_VMEM_BUDGET:
                    continue
                traffic = m * k * a_bytes * (n // tn) + k * n * b_bytes * (m // tm)
                key = (nk > 1, traffic, nk)
                if best is None or key < best[0]:
                    best = (key, (tm, tn, tk))
    return best[1]


def _matmul(name, a, b, *, ta=False, tb=False, add=None, out_dtype=F32, after=None):
    m, k = (a.shape[1], a.shape[0]) if ta else a.shape
    n = b.shape[0] if tb else b.shape[1]
    tm, tn, tk = _matmul_tiles(m, n, k, a.dtype.itemsize, b.dtype.itemsize, jnp.dtype(out_dtype).itemsize,
                               add is not None)
    nk = k // tk
    dims = (((0 if ta else 1,), (1 if tb else 0,)), ((), ()))

    def body(*refs):
        a_ref, b_ref = refs[0], refs[1]
        o_ref, acc_ref = (refs[-1], None) if nk == 1 else refs[-2:]
        kk = pl.program_id(2)
        part = lax.dot_general(a_ref[...].astype(BF16), b_ref[...].astype(BF16), dims, preferred_element_type=F32)

        def finish(r):
            if add is not None:
                r = r + refs[2][...].astype(F32)
            o_ref[...] = r.astype(o_ref.dtype)

        if nk == 1:
            finish(part)
            return

        @pl.when(kk == 0)
        def _():
            acc_ref[...] = part

        @pl.when((kk > 0) & (kk < nk - 1))
        def _():
            acc_ref[...] += part

        @pl.when(kk == nk - 1)
        def _():
            finish(acc_ref[...] + part)

    a_spec = pl.BlockSpec((tk, tm), lambda i, j, kk: (kk, i)) if ta else pl.BlockSpec((tm, tk), lambda i, j, kk: (i, kk))
    b_spec = pl.BlockSpec((tn, tk), lambda i, j, kk: (j, kk)) if tb else pl.BlockSpec((tk, tn), lambda i, j, kk: (kk, j))
    o_spec = pl.BlockSpec((tm, tn), lambda i, j, kk: (i, j))
    ins, specs = [a, b], [a_spec, b_spec]
    if add is not None:
        ins.append(add)
        specs.append(o_spec)
    if after is not None:
        ins.append(after)
        specs.append(pl.BlockSpec(after.shape, lambda i, j, kk: (0,) * after.ndim))
    return pl.pallas_call(
        body, name=name, grid=(m // tm, n // tn, nk), in_specs=specs, out_specs=o_spec,
        out_shape=jax.ShapeDtypeStruct((m, n), out_dtype),
        scratch_shapes=[pltpu.VMEM((tm, tn), F32)] if nk > 1 else [],
        compiler_params=_params("parallel", "parallel", "arbitrary"),
    )(*ins)


def _rows(name, fn, n_rows, tile, ins, outs, accs=()):
    n_in, n_out = len(ins), len(outs)

    def body(*refs):
        i = pl.program_id(0)
        vals = fn(i, *[r[...] for r in refs[:n_in]])
        for r, v in zip(refs[n_in:n_in + n_out], vals[:n_out]):
            r[...] = v.astype(r.dtype)
        for r, v in zip(refs[n_in + n_out:], vals[n_out:]):
            @pl.when(i == 0)
            def _(r=r, v=v):
                r[...] = v

            @pl.when(i > 0)
            def _(r=r, v=v):
                r[...] += v

    out_specs = [pl.BlockSpec((tile, w), lambda i: (i, 0)) for w, _ in outs]
    out_specs += [pl.BlockSpec(s, lambda i: (0, 0)) for s in accs]
    out_shape = [jax.ShapeDtypeStruct((n_rows, w), d) for w, d in outs]
    out_shape += [jax.ShapeDtypeStruct(s, F32) for s in accs]
    return pl.pallas_call(
        body, name=name, grid=(n_rows // tile,), in_specs=[s for _, s in ins], out_specs=out_specs,
        out_shape=out_shape, compiler_params=_params("arbitrary"),
    )(*[a for a, _ in ins])


def _row_spec(tile, width, col=0):
    return pl.BlockSpec((tile, width), lambda i: (i, col))


def _full_spec(shape):
    return pl.BlockSpec(shape, lambda i: (0,) * len(shape))


def _prev_rows_spec(tile, width):
    return pl.BlockSpec((8, width), lambda i: (jnp.maximum(i * (tile // 8) - 1, 0), 0))


def _next_rows_spec(tile, width, n_tiles):
    return pl.BlockSpec((8, width), lambda i: (jnp.minimum(i + 1, n_tiles - 1), 0))


def _row_of(x8, idx):
    r = lax.broadcasted_iota(jnp.int32, x8.shape, 0)
    return jnp.sum(jnp.where(r == idx, x8, 0.0), axis=0, keepdims=True)


def _rms(x, g):
    return x * lax.rsqrt(jnp.mean(x * x, axis=-1, keepdims=True) + NORM_EPS) * g


def _shifted(i, z, prev8):
    first = jnp.where(i > 0, _row_of(prev8, 7), 0.0)
    row = lax.broadcasted_iota(jnp.int32, z.shape, 0)
    return jnp.where(row == 0, first, pltpu.roll(z, 1, 0))


def _rw_pre(z, zp, mu, w0, a0, wup, aup, k_k, k_a, *, cw):
    zs = z + (zp - z) * mu
    r, k, v, g = (zs[:, j * cw:(j + 1) * cw] for j in range(4))
    lo = zs[:, 4 * cw:4 * cw + LANES]
    w_raw = w0 + _bdot(jnp.tanh(lo), wup)
    decay = jnp.exp(-jnp.exp(-_softplus(-w_raw) - 0.5))
    a = jax.nn.sigmoid(a0 + _bdot(lo, aup))
    e = _head_matrix(cw)
    kk = k * k_k
    kk = kk / jnp.maximum(jnp.sqrt(_head_sum(kk * kk, e)), 1e-12)
    k_mod = k * (1.0 + (a - 1.0) * k_a)
    return r, decay, k_mod, v, kk, a, g


def _rw_post(y, r, k_mod, v, g, ln_g, ln_b, r_k, *, cw):
    e = _head_matrix(cw)
    mu = _head_sum(y, e) * (1.0 / HEAD)
    d = y - mu
    var = _head_sum(d * d, e) * (1.0 / HEAD)
    yn = d * lax.rsqrt(var + GN_EPS) * ln_g + ln_b
    bonus = _head_sum(r * k_mod * r_k, e) * v
    return (yn + bonus) * (g * jax.nn.sigmoid(g))


def _merge(zg, u_rw, u_fox, *, d):
    return jax.nn.sigmoid(zg[:, :d]) * u_rw + jax.nn.sigmoid(zg[:, d:]) * u_fox


def _head_loss(x1, ple, gl, gf, tgt):
    x2 = x1 + ple * jax.nn.sigmoid(gl)
    err = _rms(x2, gf) - tgt
    return 0.5 * jnp.sum(jnp.mean(err * err, axis=-1, keepdims=True), axis=0, keepdims=True)


def _batched(a, b, ca, cb):
    return lax.dot_general(a, b, (((ca,), (cb,)), ((0,), (0,))), precision=SCAN_PREC, preferred_element_type=F32)


def _eliminate_block(lo, off, m):
    n, c, _ = lo.shape
    ci = lax.broadcasted_iota(jnp.int32, (n, m, c), 2)
    if m <= ELIM_BASE:
        ri = lax.broadcasted_iota(jnp.int32, (n, m, c), 1)
        rows = jnp.where((ci >= off) & (ci < off + m), lo[:, off:off + m, :], 0.0)
        x = (ri + off == ci).astype(F32)
        for s in range(m - 1):
            col = jnp.sum(jnp.where(ci == off + s, rows, 0.0), axis=2, keepdims=True)
            x = x - col * x[:, s:s + 1, :]
        return x
    h = m // 2
    xa = _eliminate_block(lo, off, h)
    xd = _eliminate_block(lo, off + h, h)
    ci_h = lax.broadcasted_iota(jnp.int32, (n, h, c), 2)
    b = jnp.where((ci_h >= off) & (ci_h < off + h), lo[:, off + h:off + m, :], 0.0)

    def at_rows(x, start):
        parts = ([jnp.zeros((n, start, c), F32)] if start else []) + [x]
        rest = c - start - x.shape[1]
        return jnp.concatenate(parts + ([jnp.zeros((n, rest, c), F32)] if rest else []), axis=1)

    low = xd - _batched(_batched(xd, at_rows(b, off + h), 2, 1), at_rows(xa, off), 2, 1)
    return jnp.concatenate([xa, low], axis=1)


def _eliminate(lo):
    return _eliminate_block(lo, 0, lo.shape[1])


@jax.custom_vjp
def _unit_lower_inverse(lo, known):
    return _eliminate(lo) if known is None else known


def _uli_fwd(lo, known):
    x = _unit_lower_inverse(lo, known)
    return x, (x, known)


def _uli_bwd(res, dx):
    x, known = res
    dlo = -_batched(_batched(x, dx, 1, 1), x, 2, 2)
    return dlo, (None if known is None else jnp.zeros_like(known))


_unit_lower_inverse.defvjp(_uli_fwd, _uli_bwd)


def _rwkv_chunk(s0, r, w, k, v, kk, a, *, c, tinv_known=None):
    pairs = range(len(s0))
    lane = lax.broadcasted_iota(jnp.int32, (1, LANES), 1)
    heads = (lane < HEAD, lane >= HEAD)
    ti = lax.broadcasted_iota(jnp.int32, (c, c), 0)
    si = lax.broadcasted_iota(jnp.int32, (c, c), 1)
    incl = si <= ti
    strict = si < ti
    tri = incl.astype(F32)
    logw = [jnp.log(w[p]) for p in pairs]
    cum = [_dot(tri, logw[p], HIGHEST) for p in pairs]
    cum_end = [jnp.sum(logw[p], axis=0, keepdims=True) for p in pairs]
    g_inv = [jnp.exp(-cum[p]) for p in pairs]
    to_end = [jnp.exp(cum_end[p] - cum[p]) for p in pairs]
    b = [kk[p] * a[p] for p in pairs]
    beta = [b[p] * g_inv[p] for p in pairs]
    kap = [kk[p] * jnp.exp(cum[p] - logw[p]) for p in pairs]
    kt = [k[p] * g_inv[p] for p in pairs]
    rt = [r[p] * jnp.exp(cum[p]) for p in pairs]
    lhs = [jnp.concatenate([jnp.where(m, x[p], 0.0) for x in (kap, rt) for m in heads], axis=0) for p in pairs]
    vs_beta = [_dot_nt(lhs[p], beta[p], None) for p in pairs]
    vs_kt = [_dot_nt(lhs[p], kt[p], None) for p in pairs]
    strict2 = jnp.concatenate([strict, strict], axis=0)
    incl2 = jnp.concatenate([incl, incl], axis=0)
    lo = [jnp.where(strict2, vs_beta[p][:2 * c], 0.0) for p in pairs]
    mm = [jnp.where(strict2, vs_kt[p][:2 * c], 0.0) for p in pairs]
    arb = [jnp.where(incl2, vs_beta[p][2 * c:], 0.0) for p in pairs]
    ark = [jnp.where(incl2, vs_kt[p][2 * c:], 0.0) for p in pairs]
    per_head = lambda xs: jnp.concatenate([xs[p][h * c:(h + 1) * c][None] for p in pairs for h in (0, 1)])
    tinv = _unit_lower_inverse(per_head(lo), None if tinv_known is None else per_head(tinv_known))
    tinv = [jnp.concatenate([tinv[2 * p], tinv[2 * p + 1]], axis=0) for p in pairs]
    both = lambda x: jnp.where(heads[0], x[:c], x[c:])
    vs_s = [_dot_nt(jnp.concatenate([kap[p], rt[p]], axis=0), s0[p], None) for p in pairs]
    rhs = [vs_s[p][:c] + both(_dot(mm[p], v[p], None)) for p in pairs]
    u = [-both(_dot(tinv[p], rhs[p], None)) for p in pairs]
    y = [vs_s[p][c:] + both(_dot(arb[p], u[p], None) + _dot(ark[p], v[p], None)) for p in pairs]
    rr = lax.broadcasted_iota(jnp.int32, (LANES, LANES), 0) < HEAD
    cc = lax.broadcasted_iota(jnp.int32, (LANES, LANES), 1) < HEAD
    ds = [_dot_tn(jnp.concatenate([u[p], v[p]], axis=0),
                  jnp.concatenate([b[p] * to_end[p], k[p] * to_end[p]], axis=0), None) for p in pairs]
    s1 = [s0[p] * jnp.exp(cum_end[p]) + jnp.where(rr == cc, ds[p], 0.0) for p in pairs]
    return tuple(y), tuple(s1), tuple(tinv)


def _scan_tiles(t, n_pair):
    return _tile(t, (32, 16, 8)), _tile(t, (256, 128, 64, 32)), _tile(n_pair, (PAIRS_PER_STEP, 4, 2, 1))


def _scan_fwd(r, w, k, v, kk, a):
    t, width = r.shape
    c, tb, npb = _scan_tiles(t, width // LANES)
    n_grp, n_blk, n_cb = width // (LANES * npb), t // tb, tb // c

    def body(r_ref, w_ref, k_ref, v_ref, kk_ref, a_ref, y_ref, st_ref, ti_ref, s_scr):
        @pl.when(pl.program_id(1) == 0)
        def _():
            s_scr[...] = jnp.zeros_like(s_scr)

        def chunk(j, carry):
            sl = pl.ds(pl.multiple_of(j * c, c), c)
            lanes = [pl.ds(q * LANES, LANES) for q in range(npb)]
            s0 = tuple(s_scr[q] for q in range(npb))
            cols = lambda ref: tuple(ref[sl, ln] for ln in lanes)
            y, s1, tinv = _rwkv_chunk(s0, cols(r_ref), cols(w_ref), cols(k_ref), cols(v_ref), cols(kk_ref), cols(a_ref),
                                      c=c)
            for q, ln in enumerate(lanes):
                st_ref[q, j] = s0[q]
                ti_ref[q, j] = tinv[q]
                y_ref[sl, ln] = y[q]
                s_scr[q] = s1[q]
            return carry

        lax.fori_loop(0, n_cb, chunk, 0)

    blk = pl.BlockSpec((tb, npb * LANES), lambda p, i: (i, p))
    return pl.pallas_call(
        body, name="rwkv_scan_fwd", grid=(n_grp, n_blk), in_specs=[blk] * 6,
        out_specs=[blk, pl.BlockSpec((npb, n_cb, LANES, LANES), lambda p, i: (p, i, 0, 0)),
                   pl.BlockSpec((npb, n_cb, 2 * c, c), lambda p, i: (p, i, 0, 0))],
        out_shape=[jax.ShapeDtypeStruct((t, width), F32),
                   jax.ShapeDtypeStruct((width // LANES, t // c, LANES, LANES), F32),
                   jax.ShapeDtypeStruct((width // LANES, t // c, 2 * c, c), F32)],
        scratch_shapes=[pltpu.VMEM((npb, LANES, LANES), F32)],
        compiler_params=_params("arbitrary", "arbitrary"),
    )(r, w, k, v, kk, a)


def _scan_bwd(r, w, k, v, kk, a, st, ti, dy):
    t, width = r.shape
    c, tb, npb = _scan_tiles(t, width // LANES)
    n_grp, n_blk, n_cb = width // (LANES * npb), t // tb, tb // c

    def body(r_ref, w_ref, k_ref, v_ref, kk_ref, a_ref, st_ref, ti_ref, dy_ref,
             dr_ref, dw_ref, dk_ref, dv_ref, dkk_ref, da_ref, ds_scr):
        @pl.when(pl.program_id(1) == 0)
        def _():
            ds_scr[...] = jnp.zeros_like(ds_scr)

        def chunk(jj, carry):
            j = n_cb - 1 - jj
            sl = pl.ds(pl.multiple_of(j * c, c), c)
            lanes = [pl.ds(q * LANES, LANES) for q in range(npb)]
            cols = lambda ref: tuple(ref[sl, ln] for ln in lanes)
            args = (tuple(st_ref[q, j] for q in range(npb)), cols(r_ref), cols(w_ref), cols(k_ref), cols(v_ref),
                    cols(kk_ref), cols(a_ref))
            known = tuple(ti_ref[q, j] for q in range(npb))
            _, vjp = jax.vjp(lambda *xs: _rwkv_chunk(*xs, c=c, tinv_known=known)[:2], *args)
            grads = vjp((cols(dy_ref), tuple(ds_scr[q] for q in range(npb))))
            for q, ln in enumerate(lanes):
                ds_scr[q] = grads[0][q]
                for ref, g in zip((dr_ref, dw_ref, dk_ref, dv_ref, dkk_ref, da_ref), grads[1:]):
                    ref[sl, ln] = g[q]
            return carry

        lax.fori_loop(0, n_cb, chunk, 0)

    blk = pl.BlockSpec((tb, npb * LANES), lambda p, i: (n_blk - 1 - i, p))
    stb = pl.BlockSpec((npb, n_cb, LANES, LANES), lambda p, i: (p, n_blk - 1 - i, 0, 0))
    tib = pl.BlockSpec((npb, n_cb, 2 * c, c), lambda p, i: (p, n_blk - 1 - i, 0, 0))
    return pl.pallas_call(
        body, name="rwkv_scan_bwd", grid=(n_grp, n_blk), in_specs=[blk] * 6 + [stb, tib, blk], out_specs=[blk] * 6,
        out_shape=[jax.ShapeDtypeStruct((t, width), F32)] * 6,
        scratch_shapes=[pltpu.VMEM((npb, LANES, LANES), F32)],
        compiler_params=_params("arbitrary", "arbitrary"),
    )(r, w, k, v, kk, a, st, ti, dy)


NEG = -1e30


def _fox_cumsum(zf, b_pad, *, fw, fh):
    t = zf.shape[0]
    tile = _tile(t, (256, 128, 64, 32, 16, 8))

    def body(fl_ref, b_ref, c_ref, carry):
        @pl.when(pl.program_id(0) == 0)
        def _():
            carry[...] = jnp.zeros_like(carry)

        lane = lax.broadcasted_iota(jnp.int32, (tile, LANES), 1)
        logf = jnp.where(lane < fh, -_softplus(-(fl_ref[...] + b_ref[...])), 0.0)
        ri = lax.broadcasted_iota(jnp.int32, (tile, tile), 0)
        ci = lax.broadcasted_iota(jnp.int32, (tile, tile), 1)
        c_ref[...] = carry[...] + _dot((ci <= ri).astype(F32), logf, HIGHEST)
        carry[...] += jnp.sum(logf, axis=0, keepdims=True)

    return pl.pallas_call(
        body, name="fox_cumsum", grid=(t // tile,),
        in_specs=[_row_spec(tile, LANES, 4 * fw // LANES), _full_spec((1, LANES))],
        out_specs=_row_spec(tile, LANES), out_shape=jax.ShapeDtypeStruct((t, LANES), F32),
        scratch_shapes=[pltpu.VMEM((1, LANES), F32)], compiler_params=_params("arbitrary"),
    )(zf, b_pad)


def _fox_cumsum_bwd(zf, b_pad, dc, *, fw, fh):
    t = zf.shape[0]
    tile = _tile(t, (256, 128, 64, 32, 16, 8))
    n = t // tile

    def body(fl_ref, b_ref, dc_ref, dfl_ref, db_ref, carry):
        i = pl.program_id(0)

        @pl.when(i == 0)
        def _():
            carry[...] = jnp.zeros_like(carry)
            db_ref[...] = jnp.zeros_like(db_ref)

        lane = lax.broadcasted_iota(jnp.int32, (tile, LANES), 1)
        dc_t = dc_ref[...]
        ri = lax.broadcasted_iota(jnp.int32, (tile, tile), 0)
        ci = lax.broadcasted_iota(jnp.int32, (tile, tile), 1)
        dlogf = carry[...] + _dot((ci >= ri).astype(F32), dc_t, HIGHEST)
        carry[...] += jnp.sum(dc_t, axis=0, keepdims=True)
        dfl = jnp.where(lane < fh, dlogf * jax.nn.sigmoid(-(fl_ref[...] + b_ref[...])), 0.0)
        dfl_ref[...] = dfl.astype(dfl_ref.dtype)
        db_ref[...] += jnp.sum(dfl, axis=0, keepdims=True)

    rev = lambda col: pl.BlockSpec((tile, LANES), lambda i: (n - 1 - i, col))
    return pl.pallas_call(
        body, name="fox_cumsum_bwd", grid=(n,),
        in_specs=[rev(4 * fw // LANES), _full_spec((1, LANES)), rev(0)],
        out_specs=[rev(0), _full_spec((1, LANES))],
        out_shape=[jax.ShapeDtypeStruct((t, LANES), BF16), jax.ShapeDtypeStruct((1, LANES), F32)],
        scratch_shapes=[pltpu.VMEM((1, LANES), F32)], compiler_params=_params("arbitrary"),
    )(zf, b_pad, dc)


def _fox_tile(t):
    return _tile(t, (512, 256, 128))


def _fox_fwd(zf, c, ct, *, fw):
    t = zf.shape[0]
    tq = _fox_tile(t)
    th = tq // 2
    n_pair, n_q = fw // LANES, t // tq
    scale = HEAD ** -0.5
    chains = [(h, qq) for h in (0, 1) for qq in (0, 1)]

    def body(q_ref, k_ref, v_ref, g_ref, c_ref, ct_ref, o_ref, lse_ref, y_ref):
        hp, i = pl.program_id(0), pl.program_id(1)
        lane = lax.broadcasted_iota(jnp.int32, (1, LANES), 1)
        in_head = (lane < HEAD, lane >= HEAD)
        rows = [pl.ds(qq * th, th) for qq in (0, 1)]
        qh = [jnp.where(in_head[h], q_ref[rows[qq], :] * scale, 0.0).astype(BF16) for h, qq in chains]
        cq = [_lane_pick(c_ref[rows[qq], :], 2 * hp + h) for h, qq in chains]
        qidx = lax.broadcasted_iota(jnp.int32, (th, tq), 0)
        kidx = lax.broadcasted_iota(jnp.int32, (th, tq), 1)

        def kv_step(j, carry, diagonal):
            m, l, acc = carry
            ks = pl.ds(pl.multiple_of(j * tq, tq), tq)
            kb = k_ref[ks, :].astype(BF16)
            vb = v_ref[ks, :]
            vh = [jnp.where(in_head[h], vb, 0.0).astype(BF16) for h in (0, 1)]
            ck = [ct_ref[0, j, pl.ds(h, 1), :] for h in (0, 1)]
            s = [_dot_nt(qh[n], kb) + cq[n] - ck[h] for n, (h, qq) in enumerate(chains)]
            if diagonal:
                s = [jnp.where(qq * th + qidx >= kidx, s[n], NEG) for n, (h, qq) in enumerate(chains)]
            m_new = [jnp.maximum(m[n], jnp.max(s[n], axis=1, keepdims=True)) for n in range(4)]
            p = [jnp.exp(s[n] - m_new[n]) for n in range(4)]
            alpha = [jnp.exp(m[n] - m_new[n]) for n in range(4)]
            l = [l[n] * alpha[n] + jnp.sum(p[n], axis=1, keepdims=True) for n in range(4)]
            pv = [_dot(p[n].astype(BF16), vh[h]) for n, (h, qq) in enumerate(chains)]
            acc = [acc[qq] * jnp.where(in_head[0], alpha[qq], alpha[2 + qq]) + pv[qq] + pv[2 + qq] for qq in (0, 1)]
            return tuple(m_new), tuple(l), tuple(acc)

        init = (tuple(jnp.full((th, 1), NEG, F32) for _ in chains), tuple(jnp.zeros((th, 1), F32) for _ in chains),
                tuple(jnp.zeros((th, LANES), F32) for _ in (0, 1)))
        carry = lax.fori_loop(0, i, functools.partial(kv_step, diagonal=False), init)
        m, l, acc = kv_step(i, carry, True)
        for qq in (0, 1):
            o = acc[qq] / jnp.where(in_head[0], l[qq], l[2 + qq])
            g = g_ref[rows[qq], :]
            o_ref[rows[qq], :] = o
            lse_ref[rows[qq], :] = jnp.where(in_head[0], m[qq] + jnp.log(l[qq]), m[2 + qq] + jnp.log(l[2 + qq]))
            y_ref[rows[qq], :] = (o * (g * jax.nn.sigmoid(g))).astype(y_ref.dtype)

    npw = fw // LANES
    blk = lambda col0: pl.BlockSpec((tq, LANES), lambda hp, i: (i, col0 + hp))
    res = lambda col0: pl.BlockSpec((t, LANES), lambda hp, i: (0, col0 + hp))
    out_blk = pl.BlockSpec((tq, LANES), lambda hp, i: (i, hp))
    return pl.pallas_call(
        body, name="fox_attn_fwd", grid=(n_pair, n_q),
        in_specs=[blk(0), res(npw), res(2 * npw), blk(3 * npw),
                  pl.BlockSpec((tq, LANES), lambda hp, i: (i, 0)),
                  pl.BlockSpec((1, n_q, 2, tq), lambda hp, i: (hp, 0, 0, 0))],
        out_specs=[out_blk, out_blk, out_blk],
        out_shape=[jax.ShapeDtypeStruct((t, fw), F32), jax.ShapeDtypeStruct((t, fw), F32),
                   jax.ShapeDtypeStruct((t, fw), BF16)],
        compiler_params=_params("arbitrary", "arbitrary"),
    )(zf, zf, zf, zf, c, ct)


def _fox_bwd(zf, do, c, ct, lse_r, dd_r, *, fw):
    t = zf.shape[0]
    tq = _fox_tile(t)
    n_pair, n_q = fw // LANES, t // tq
    scale = HEAD ** -0.5

    def body(q_ref, k_ref, v_ref, do_ref, c_ref, ct_ref, lse_ref, dd_ref,
             dq_ref, dk_ref, dv_ref, dcq_ref, dck_ref, dq_acc, dcq_acc):
        hp, j = pl.program_id(0), pl.program_id(1)

        @pl.when(j == 0)
        def _():
            dq_acc[...] = jnp.zeros_like(dq_acc)
            dcq_acc[...] = jnp.zeros_like(dcq_acc)

        lane = lax.broadcasted_iota(jnp.int32, (1, LANES), 1)
        in_head = (lane < HEAD, lane >= HEAD)
        kb = k_ref[...]
        kh = [jnp.where(m, kb, 0.0).astype(BF16) for m in in_head]
        vb = v_ref[...].astype(BF16)
        c_k = c_ref[...]
        ck = [_lane_pick(c_k, 2 * hp + h) for h in (0, 1)]
        kidx = lax.broadcasted_iota(jnp.int32, (tq, tq), 0)
        qidx = lax.broadcasted_iota(jnp.int32, (tq, tq), 1)

        def q_step(i, carry, diagonal):
            dk, dv, dck = carry
            qs = pl.ds(pl.multiple_of(i * tq, tq), tq)
            qf = q_ref[qs, :] * scale
            dof = do_ref[qs, :]
            qh = [jnp.where(m, qf, 0.0).astype(BF16) for m in in_head]
            doh = [jnp.where(m, dof, 0.0).astype(BF16) for m in in_head]
            row = lambda ref, h: ref[0, i, pl.ds(h, 1), :]
            st = [_dot_nt(kh[h], qh[h]) + row(ct_ref, h) - ck[h] for h in (0, 1)]
            p = [jnp.exp(st[h] - row(lse_ref, h)) for h in (0, 1)]
            if diagonal:
                p = [jnp.where(kidx <= qidx, p[h], 0.0) for h in (0, 1)]
            dst = [p[h] * (_dot_nt(vb, doh[h]) - row(dd_ref, h)) for h in (0, 1)]
            p16 = [x.astype(BF16) for x in p]
            ds16 = [x.astype(BF16) for x in dst]
            dv = dv + _dot(p16[0], doh[0]) + _dot(p16[1], doh[1])
            dk = dk + _dot(ds16[0], qh[0]) + _dot(ds16[1], qh[1])
            dq_acc[qs, :] += _dot_tn(ds16[0], kh[0]) + _dot_tn(ds16[1], kh[1])
            for h in (0, 1):
                dcq_acc[i, pl.ds(h, 1), :] += jnp.sum(dst[h], axis=0, keepdims=True)
            dck = tuple(dck[h] - jnp.sum(dst[h], axis=1, keepdims=True) for h in (0, 1))
            return dk, dv, dck

        zero = jnp.zeros((tq, LANES), F32)
        carry = q_step(j, (zero, zero, (jnp.zeros((tq, 1), F32),) * 2), True)
        dk, dv, dck = lax.fori_loop(j + 1, n_q, functools.partial(q_step, diagonal=False), carry)
        dk_ref[...] = dk.astype(dk_ref.dtype)
        dv_ref[...] = dv.astype(dv_ref.dtype)
        dck_ref[...] = jnp.where(lane == 0, dck[0], jnp.where(lane == 1, dck[1], 0.0))

        @pl.when(j == n_q - 1)
        def _():
            dq_ref[...] = (dq_acc[...] * scale).astype(dq_ref.dtype)
            dcq_ref[0] = dcq_acc[...]

    npw = fw // LANES
    res_z = lambda col0: pl.BlockSpec((t, LANES), lambda hp, j: (0, col0 + hp))
    blk_z = lambda col0: pl.BlockSpec((tq, LANES), lambda hp, j: (j, col0 + hp))
    res = pl.BlockSpec((t, LANES), lambda hp, j: (0, hp))
    blk = pl.BlockSpec((tq, LANES), lambda hp, j: (j, hp))
    rows = pl.BlockSpec((1, n_q, 2, tq), lambda hp, j: (hp, 0, 0, 0))
    return pl.pallas_call(
        body, name="fox_attn_bwd", grid=(n_pair, n_q),
        in_specs=[res_z(0), blk_z(npw), blk_z(2 * npw), res, pl.BlockSpec((tq, LANES), lambda hp, j: (j, 0)),
                  rows, rows, rows],
        out_specs=[res, blk, blk, rows, blk],
        out_shape=[jax.ShapeDtypeStruct((t, fw), BF16), jax.ShapeDtypeStruct((t, fw), BF16),
                   jax.ShapeDtypeStruct((t, fw), BF16), jax.ShapeDtypeStruct((n_pair, n_q, 2, tq), F32),
                   jax.ShapeDtypeStruct((t, fw), F32)],
        scratch_shapes=[pltpu.VMEM((t, LANES), F32), pltpu.VMEM((n_q, 2, tq), F32)],
        compiler_params=_params("arbitrary", "arbitrary"),
    )(zf, zf, zf, do, c, ct, lse_r, dd_r)


def _adamw_math(w, g, m, v):
    m = ADAM_B1 * m + (1.0 - ADAM_B1) * g
    v = ADAM_B2 * v + (1.0 - ADAM_B2) * jnp.square(g)
    m_hat = m / (1.0 - ADAM_B1 ** ADAM_STEP)
    v_hat = v / (1.0 - ADAM_B2 ** ADAM_STEP)
    delta = -ADAM_LR * (m_hat / (jnp.sqrt(v_hat) + ADAM_EPS) + ADAM_WD * w)
    return delta, m, v


def _adamw(name, w, g, m, v):
    lead, rows, cols = w.shape
    if lead == 1 and cols % LANES:
        outs = _adamw(name, *[jnp.transpose(a, (2, 0, 1)) for a in (w, g, m, v)])
        return [jnp.transpose(o, (1, 2, 0)) for o in outs]
    if lead == 1:
        tile = _tile(rows, (128, 64, 32, 16, 8))
        spec, steps = pl.BlockSpec((1, tile, cols), lambda i: (0, i, 0)), rows // tile
    else:
        tile = _div_tile(lead, 256, 1)
        spec, steps = pl.BlockSpec((tile, rows, cols), lambda i: (i, 0, 0)), lead // tile

    def body(w_ref, g_ref, m_ref, v_ref, d_ref, mo_ref, vo_ref):
        d_ref[...], mo_ref[...], vo_ref[...] = _adamw_math(w_ref[...], g_ref[...], m_ref[...], v_ref[...])

    return pl.pallas_call(
        body, name=name, grid=(steps,), in_specs=[spec] * 4, out_specs=[spec] * 3,
        out_shape=[jax.ShapeDtypeStruct(w.shape, F32)] * 3, compiler_params=_params("parallel"),
    )(w, g, m, v)


def _place():
    return lax.axis_index("x"), lax.axis_index("y"), lax.axis_index("c")


def _other_chips(x, y):
    return [(1 - x, y), (x, 1 - y), (1 - x, 1 - y)]


HBM_SPEC = pl.BlockSpec(memory_space=pltpu.HBM)
SEM_SPEC = pl.BlockSpec(memory_space=pltpu.SEMAPHORE)


def _all_gather_shards(slabs):
    n = len(slabs)

    def body(*refs):
        src_refs, out_refs, send_sems, recv_sems = refs[:n], refs[n:2 * n], refs[2 * n], refs[2 * n + 1]
        x, y, c = _place()
        me = 2 * x + y
        sibling = (x, y, 1 - c)
        chips = _other_chips(x, y)
        first, passed, waits = [], [], []
        for g, (src_ref, out_ref) in enumerate(zip(src_refs, out_refs)):
            rh = src_ref.shape[0] // 2

            def part(chip, half, out_ref=out_ref, rh=rh):
                return out_ref.at[chip, pl.ds(half * rh, rh), :]

            def copy(k, src, dst, to, g=g):
                return pltpu.make_async_remote_copy(src_ref=src, dst_ref=dst, send_sem=send_sems.at[6 * g + k],
                                                    recv_sem=recv_sems.at[6 * g + k], device_id=to, device_id_type=MESH)

            first += [copy(j, src_ref.at[pl.ds(c * rh, rh), :], part(me, c), (px, py, c))
                      for j, (px, py) in enumerate(chips)]
            for j, (px, py) in enumerate(chips):
                theirs = part(2 * px + py, c)
                passed.append((copy(j, theirs, theirs, sibling), copy(3 + j, theirs, theirs, sibling)))
                other = part(2 * px + py, 1 - c)
                waits.append(copy(3 + j, other, other, sibling))
        for cp in first:
            cp.start()
        for landed, forward in passed:
            landed.wait_recv()
            forward.start()
        for cp in waits:
            cp.wait_recv()
        for cp in first + [fwd for _, fwd in passed]:
            cp.wait_send()

    return pl.pallas_call(
        body, name="weights_all_gather", in_specs=[HBM_SPEC] * n, out_specs=[HBM_SPEC] * n,
        out_shape=[jax.ShapeDtypeStruct((N_SHARD,) + a.shape, a.dtype) for a in slabs],
        scratch_shapes=[pltpu.SemaphoreType.DMA((6 * n,)), pltpu.SemaphoreType.DMA((6 * n,))],
    )(*slabs)


def _gather_ici_copies(src_refs, out_refs, send_sems, recv_sems):
    x, y, c = _place()
    me = 2 * x + y
    copies = []
    for g, (src_ref, out_ref) in enumerate(zip(src_refs, out_refs)):
        rh = src_ref.shape[0] // 2
        copies += [pltpu.make_async_remote_copy(
            src_ref=src_ref.at[pl.ds(c * rh, rh), :], dst_ref=out_ref.at[me, pl.ds(c * rh, rh), :],
            send_sem=send_sems.at[3 * g + j], recv_sem=recv_sems.at[3 * g + j], device_id=(px, py, c),
            device_id_type=MESH) for j, (px, py) in enumerate(_other_chips(x, y))]
    return copies


def _gather_start(slabs):
    n = len(slabs)

    def body(*refs):
        for cp in _gather_ici_copies(refs[:n], refs[n:2 * n], refs[2 * n], refs[2 * n + 1]):
            cp.start()
        refs[-1][...] = jnp.zeros_like(refs[-1])

    hbm = lambda a: pltpu.with_memory_space_constraint(a, pltpu.HBM)
    lands = [(N_SHARD,) + a.shape for a in slabs]
    out = pl.pallas_call(
        body, name="weights_gather_start",
        out_shape=(pltpu.SemaphoreType.DMA((3 * n,)), pltpu.SemaphoreType.DMA((3 * n,)),
                   *[pltpu.HBM(a.shape, a.dtype) for a in slabs],
                   *[pltpu.HBM(sh, a.dtype) for sh, a in zip(lands, slabs)], jax.ShapeDtypeStruct((8, LANES), F32)),
        in_specs=[HBM_SPEC] * (2 * n),
        out_specs=(SEM_SPEC, SEM_SPEC, *[HBM_SPEC] * (2 * n), pl.BlockSpec(memory_space=pltpu.VMEM)),
        input_output_aliases={i: 2 + i for i in range(2 * n)},
        compiler_params=pltpu.CompilerParams(has_side_effects=pltpu.SideEffectType.DATAFLOW_SIDE_EFFECTING),
    )(*[hbm(a) for a in slabs], *[hbm(lax.empty(sh, a.dtype)) for sh, a in zip(lands, slabs)])
    return out[0], out[1], list(out[2:2 + n]), list(out[2 + n:2 + 2 * n]), out[-1]


def _gather_wait(send_sems, recv_sems, slabs, landed, after):
    n = len(slabs)

    def body(*refs):
        for cp in _gather_ici_copies(refs[:n], refs[n:2 * n], refs[2 * n], refs[2 * n + 1]):
            cp.wait_send()
            cp.wait_recv()

    out = pl.pallas_call(
        body, name="weights_gather_wait",
        out_shape=[pltpu.HBM(a.shape, a.dtype) for a in slabs + landed],
        in_specs=[HBM_SPEC] * (2 * n) + [SEM_SPEC, SEM_SPEC, pl.BlockSpec(memory_space=pl.ANY)],
        out_specs=[HBM_SPEC] * (2 * n), input_output_aliases={i: i for i in range(2 * n)},
        compiler_params=pltpu.CompilerParams(has_side_effects=pltpu.SideEffectType.DATAFLOW_SIDE_EFFECTING),
    )(*slabs, *landed, send_sems, recv_sems, after)
    return list(out[n:])


def _gather_forward(gathered):
    n = len(gathered)

    def body(*refs):
        in_refs, out_refs, send_sems, recv_sems = refs[:n], refs[n:2 * n], refs[2 * n], refs[2 * n + 1]
        x, y, c = _place()

        def copy(g, j, chip, half):
            rh = in_refs[g].shape[1] // 2
            return pltpu.make_async_remote_copy(
                src_ref=in_refs[g].at[chip, pl.ds(half * rh, rh), :], dst_ref=out_refs[g].at[chip, pl.ds(half * rh, rh), :],
                send_sem=send_sems.at[3 * g + j], recv_sem=recv_sems.at[3 * g + j], device_id=(x, y, 1 - c),
                device_id_type=MESH)

        chips = [2 * px + py for px, py in _other_chips(x, y)]
        for g in range(n):
            for j, chip in enumerate(chips):
                copy(g, j, chip, c).start()
        for g in range(n):
            for j, chip in enumerate(chips):
                copy(g, j, chip, c).wait_send()
                copy(g, j, chip, 1 - c).wait_recv()

    return pl.pallas_call(
        body, name="weights_gather_forward", in_specs=[HBM_SPEC] * n, out_specs=[HBM_SPEC] * n,
        out_shape=[jax.ShapeDtypeStruct(a.shape, a.dtype) for a in gathered],
        input_output_aliases={g: g for g in range(n)},
        scratch_shapes=[pltpu.SemaphoreType.DMA((3 * n,)), pltpu.SemaphoreType.DMA((3 * n,))],
    )(*gathered)


def _chip_index():
    return jnp.reshape(2 * lax.axis_index("x") + lax.axis_index("y"), (1,)).astype(jnp.int32)


def _place_own_shard(name, gathered, slab):
    rows, width = slab.shape
    tile = _div_tile(rows, 256, 16)

    def body(me_ref, s_ref, g_ref, o_ref):
        o_ref[0] = s_ref[...]

    return pl.pallas_call(
        body, name=name,
        grid_spec=pltpu.PrefetchScalarGridSpec(
            num_scalar_prefetch=1, grid=(rows // tile,),
            in_specs=[pl.BlockSpec((tile, width), lambda i, me: (i, 0)), pl.BlockSpec(memory_space=pl.ANY)],
            out_specs=pl.BlockSpec((1, tile, width), lambda i, me: (me[0], i, 0))),
        out_shape=jax.ShapeDtypeStruct(gathered.shape, gathered.dtype), input_output_aliases={2: 0},
        compiler_params=_params("parallel"),
    )(_chip_index(), slab, gathered)


def _sibling_exchange(sent):
    n = len(sent)

    def body(*refs):
        g_refs, out_refs, send_sems, recv_sems = refs[:n], refs[n:2 * n], refs[2 * n], refs[2 * n + 1]
        x, y, c = _place()
        copies = [pltpu.make_async_remote_copy(
            src_ref=g_ref.at[s], dst_ref=out_ref.at[s], send_sem=send_sems.at[N_SHARD * g + s],
            recv_sem=recv_sems.at[N_SHARD * g + s], device_id=(x, y, 1 - c), device_id_type=MESH)
            for g, (g_ref, out_ref) in enumerate(zip(g_refs, out_refs)) for s in range(N_SHARD)]
        for cp in copies:
            cp.start()
        for cp in copies:
            cp.wait()

    return pl.pallas_call(
        body, name="grad_sibling_exchange", in_specs=[HBM_SPEC] * n, out_specs=[HBM_SPEC] * n,
        out_shape=[jax.ShapeDtypeStruct(g.shape, g.dtype) for g in sent],
        scratch_shapes=[pltpu.SemaphoreType.DMA((N_SHARD * n,)), pltpu.SemaphoreType.DMA((N_SHARD * n,))],
    )(*sent)


def _add_sibling(name, kept, got):
    _, rh, width = kept.shape
    tile = _div_tile(rh, 256, 16)

    def body(a_ref, b_ref, o_ref):
        o_ref[...] = (a_ref[...] + b_ref[...].astype(F32)).astype(o_ref.dtype)

    spec = pl.BlockSpec((1, tile, width), lambda s, i: (s, i, 0))
    return pl.pallas_call(
        body, name=name, grid=(N_SHARD, rh // tile), in_specs=[spec, spec], out_specs=spec,
        out_shape=jax.ShapeDtypeStruct(kept.shape, BF16), compiler_params=_params("parallel", "parallel"),
    )(kept, got)


def _exchange_copies(p_refs, land_refs, send_sems, recv_sems):
    x, y, c = _place()
    me = 2 * x + y
    return [pltpu.make_async_remote_copy(
        src_ref=p_ref.at[2 * px + py], dst_ref=land_ref.at[me], send_sem=send_sems.at[3 * g + j],
        recv_sem=recv_sems.at[3 * g + j], device_id=(px, py, c), device_id_type=MESH)
        for g, (p_ref, land_ref) in enumerate(zip(p_refs, land_refs)) for j, (px, py) in enumerate(_other_chips(x, y))]


def _chip_exchange_start(ps):
    n = len(ps)

    def body(*refs):
        for cp in _exchange_copies(refs[:n], refs[n:2 * n], refs[2 * n], refs[2 * n + 1]):
            cp.start()
        refs[-1][...] = jnp.zeros_like(refs[-1])

    hbm = lambda a: pltpu.with_memory_space_constraint(a, pltpu.HBM)
    out = pl.pallas_call(
        body, name="grad_chip_exchange_start",
        out_shape=(pltpu.SemaphoreType.DMA((3 * n,)), pltpu.SemaphoreType.DMA((3 * n,)),
                   *[pltpu.HBM(a.shape, a.dtype) for a in ps], *[pltpu.HBM(a.shape, a.dtype) for a in ps],
                   jax.ShapeDtypeStruct((8, LANES), F32)),
        in_specs=[HBM_SPEC] * (2 * n),
        out_specs=(SEM_SPEC, SEM_SPEC, *[HBM_SPEC] * (2 * n), pl.BlockSpec(memory_space=pltpu.VMEM)),
        input_output_aliases={i: 2 + i for i in range(2 * n)},
        compiler_params=pltpu.CompilerParams(has_side_effects=pltpu.SideEffectType.DATAFLOW_SIDE_EFFECTING),
    )(*[hbm(a) for a in ps], *[hbm(lax.empty(a.shape, a.dtype)) for a in ps])
    return out[0], out[1], list(out[2:2 + n]), list(out[2 + n:2 + 2 * n]), out[-1]


def _chip_exchange_wait(send_sems, recv_sems, ps, landed, after):
    n = len(ps)

    def body(*refs):
        for cp in _exchange_copies(refs[:n], refs[n:2 * n], refs[2 * n], refs[2 * n + 1]):
            cp.wait_send()
            cp.wait_recv()

    out = pl.pallas_call(
        body, name="grad_chip_exchange_wait",
        out_shape=[pltpu.HBM(a.shape, a.dtype) for a in ps + landed],
        in_specs=[HBM_SPEC] * (2 * n) + [SEM_SPEC, SEM_SPEC, pl.BlockSpec(memory_space=pl.ANY)],
        out_specs=[HBM_SPEC] * (2 * n), input_output_aliases={i: i for i in range(2 * n)},
        compiler_params=pltpu.CompilerParams(has_side_effects=pltpu.SideEffectType.DATAFLOW_SIDE_EFFECTING),
    )(*ps, *landed, send_sems, recv_sems, after)
    return list(out[:n]), list(out[n:])


def _sum_chips(name, p, got):
    _, rh, width = p.shape
    tile = _div_tile(rh, 256, 16)
    n_t = rh // tile
    place = jnp.stack([2 * lax.axis_index("x") + lax.axis_index("y"), lax.axis_index("c")]).astype(jnp.int32)

    def body(pl_ref, own_ref, r0, r1, r2, r3, o_ref):
        me = pl_ref[0]
        own = own_ref[0].astype(F32)
        t = [jnp.where(me == s, own, r[0].astype(F32)) for s, r in enumerate((r0, r1, r2, r3))]
        o_ref[...] = ((t[0] + t[1]) + t[2]) + t[3]

    def slot(s):
        return pl.BlockSpec((1, tile, width), lambda i, pc: (jnp.where(pc[0] == s, (s + 1) % N_SHARD, s), i, 0))

    return pl.pallas_call(
        body, name=name,
        grid_spec=pltpu.PrefetchScalarGridSpec(
            num_scalar_prefetch=1, grid=(n_t,),
            in_specs=[pl.BlockSpec((1, tile, width), lambda i, pc: (pc[0], i, 0))] + [slot(s) for s in range(N_SHARD)],
            out_specs=pl.BlockSpec((tile, width), lambda i, pc: (pc[1] * n_t + i, 0))),
        out_shape=jax.ShapeDtypeStruct((2 * rh, width), F32), compiler_params=_params("parallel"),
    )(place, p, got, got, got, got)


def _join_halves(fulls):
    n = len(fulls)

    def body(*refs):
        f_refs, out_refs, send_sems, recv_sems = refs[:n], refs[n:2 * n], refs[2 * n], refs[2 * n + 1]
        x, y, c = _place()

        def copy(g, half):
            rh = f_refs[g].shape[0] // 2
            return pltpu.make_async_remote_copy(
                src_ref=f_refs[g].at[pl.ds(half * rh, rh), :], dst_ref=out_refs[g].at[pl.ds(half * rh, rh), :],
                send_sem=send_sems.at[g], recv_sem=recv_sems.at[g], device_id=(x, y, 1 - c), device_id_type=MESH)

        for g in range(n):
            copy(g, c).start()
        for g in range(n):
            copy(g, c).wait_send()
            copy(g, 1 - c).wait_recv()

    return pl.pallas_call(
        body, name="grad_join_halves", in_specs=[HBM_SPEC] * n, out_specs=[HBM_SPEC] * n,
        out_shape=[jax.ShapeDtypeStruct(f.shape, f.dtype) for f in fulls],
        input_output_aliases={g: g for g in range(n)},
        scratch_shapes=[pltpu.SemaphoreType.DMA((n,)), pltpu.SemaphoreType.DMA((n,))],
    )(*fulls)


def _reduce_scatter_start(gs):
    c = lax.axis_index("c")
    gs = [g.reshape(N_SHARD, 2, g.shape[1] // 2, g.shape[2]) for g in gs]
    kept = [lax.dynamic_index_in_dim(g, c, axis=1, keepdims=False) for g in gs]
    got = _sibling_exchange([lax.dynamic_index_in_dim(g, 1 - c, axis=1, keepdims=False).astype(BF16) for g in gs])
    chip_sums = [_add_sibling(f"grad_add_sibling_{i}", k, r) for i, (k, r) in enumerate(zip(kept, got))]
    *state, token = _chip_exchange_start(chip_sums)
    return state, token


def _reduce_scatter_end(state, after):
    chip_sums, landed = _chip_exchange_wait(*state, after)
    return _join_halves([_sum_chips(f"grad_sum_chips_{i}", p, r) for i, (p, r) in enumerate(zip(chip_sums, landed))])


def _all_reduce_small(v):
    rows = v.shape[0]

    def body(v_ref, out_ref, gather, send_sems, recv_sems):
        x, y, c = _place()
        gather[4 * x + 2 * y + c] = v_ref[...]
        flips = [(dx, dy, dc) for dx in (0, 1) for dy in (0, 1) for dc in (0, 1)][1:]
        peers = [((x + dx) % 2, (y + dy) % 2, (c + dc) % 2) for dx, dy, dc in flips]
        copies = [pltpu.make_async_remote_copy(
            src_ref=v_ref, dst_ref=gather.at[4 * x + 2 * y + c], send_sem=send_sems.at[j], recv_sem=recv_sems.at[j],
            device_id=peer, device_id_type=MESH) for j, peer in enumerate(peers)]
        for cp in copies:
            cp.start()
        for j, (px, py, pc) in enumerate(peers):
            pltpu.make_async_remote_copy(
                src_ref=v_ref, dst_ref=gather.at[4 * px + 2 * py + pc], send_sem=send_sems.at[j],
                recv_sem=recv_sems.at[j], device_id=(px, py, pc), device_id_type=MESH).wait_recv()
        for cp in copies:
            cp.wait_send()
        acc = gather[0]
        for d in range(1, 8):
            acc = acc + gather[d]
        out_ref[...] = acc

    vm = pl.BlockSpec(memory_space=pltpu.VMEM)
    return pl.pallas_call(
        body, name="small_grads_all_reduce", in_specs=[vm], out_specs=vm,
        out_shape=jax.ShapeDtypeStruct(v.shape, F32),
        scratch_shapes=[pltpu.VMEM((8, rows, LANES), F32), pltpu.SemaphoreType.DMA((7,)), pltpu.SemaphoreType.DMA((7,))],
    )(v)


def _pad_lanes(v):
    v = v.reshape(1, -1)
    return jnp.pad(v, ((0, 0), (0, -v.shape[1] % LANES)))


def _pack_small(vs, rows):
    flat = jnp.concatenate([_pad_lanes(v) for v in vs], axis=1)
    return jnp.pad(flat, ((0, 0), (0, rows * LANES - flat.shape[1]))).reshape(rows, LANES)


def _unpack_small(packed, shapes):
    flat = packed.reshape(-1)
    out, off = [], 0
    for s in shapes:
        n = 1
        for d in s:
            n *= d
        out.append(flat[off:off + n].reshape(s))
        off += n + (-n % LANES)
    return out


BIG = ("w_in", "rw_w_lora_up", "rw_a_lora_up", "w_up_rwkv", "w_up_fox", "w_out", "ple_proj", "ple_gate_w")
ROW_SHARDED = ("w_out", "ple_gate_w")
FIRST_NEEDED = ("w_in", "rw_w_lora_up", "rw_a_lora_up")
SMALL = ("norm_g", "rw_shift_mu", "rw_w0", "rw_a0", "rw_k_k", "rw_k_a", "rw_r_k", "rw_ln_g", "rw_ln_b", "fox_b_f",
         "ple_norm_g", "final_norm_g")
WEIGHTS = ("norm_g", "w_in", "rw_shift_mu", "rw_w0", "rw_w_lora_up", "rw_a0", "rw_a_lora_up", "rw_k_k", "rw_k_a",
           "rw_r_k", "rw_ln_g", "rw_ln_b", "fox_b_f", "w_up_rwkv", "w_up_fox", "w_out", "ple_proj", "ple_gate_w",
           "ple_norm_g", "final_norm_g")


def kernel(x, p, norm_g, w_in, rw_shift_mu, rw_w0, rw_w_lora_up, rw_a0, rw_a_lora_up, rw_k_k, rw_k_a, rw_r_k, rw_ln_g, rw_ln_b, fox_b_f, w_up_rwkv, w_up_fox, w_out, ple_proj, ple_gate_w, ple_norm_g, final_norm_g, loss_target, m_norm_g, m_w_in, m_rw_shift_mu, m_rw_w0, m_rw_w_lora_up, m_rw_a0, m_rw_a_lora_up, m_rw_k_k, m_rw_k_a, m_rw_r_k, m_rw_ln_g, m_rw_ln_b, m_fox_b_f, m_w_up_rwkv, m_w_up_fox, m_w_out, m_ple_proj, m_ple_gate_w, m_ple_norm_g, m_final_norm_g, v_norm_g, v_w_in, v_rw_shift_mu, v_rw_w0, v_rw_w_lora_up, v_rw_a0, v_rw_a_lora_up, v_rw_k_k, v_rw_k_a, v_rw_r_k, v_rw_ln_g, v_rw_ln_b, v_fox_b_f, v_w_up_rwkv, v_w_up_fox, v_w_out, v_ple_proj, v_ple_gate_w, v_ple_norm_g, v_final_norm_g):
    wts = dict(norm_g=norm_g, w_in=w_in, rw_shift_mu=rw_shift_mu, rw_w0=rw_w0, rw_w_lora_up=rw_w_lora_up, rw_a0=rw_a0,
               rw_a_lora_up=rw_a_lora_up, rw_k_k=rw_k_k, rw_k_a=rw_k_a, rw_r_k=rw_r_k, rw_ln_g=rw_ln_g, rw_ln_b=rw_ln_b,
               fox_b_f=fox_b_f, w_up_rwkv=w_up_rwkv, w_up_fox=w_up_fox, w_out=w_out, ple_proj=ple_proj,
               ple_gate_w=ple_gate_w, ple_norm_g=ple_norm_g, final_norm_g=final_norm_g)
    mom = dict(norm_g=m_norm_g, w_in=m_w_in, rw_shift_mu=m_rw_shift_mu, rw_w0=m_rw_w0, rw_w_lora_up=m_rw_w_lora_up,
               rw_a0=m_rw_a0, rw_a_lora_up=m_rw_a_lora_up, rw_k_k=m_rw_k_k, rw_k_a=m_rw_k_a, rw_r_k=m_rw_r_k,
               rw_ln_g=m_rw_ln_g, rw_ln_b=m_rw_ln_b, fox_b_f=m_fox_b_f, w_up_rwkv=m_w_up_rwkv, w_up_fox=m_w_up_fox,
               w_out=m_w_out, ple_proj=m_ple_proj, ple_gate_w=m_ple_gate_w, ple_norm_g=m_ple_norm_g,
               final_norm_g=m_final_norm_g)
    vel = dict(norm_g=v_norm_g, w_in=v_w_in, rw_shift_mu=v_rw_shift_mu, rw_w0=v_rw_w0, rw_w_lora_up=v_rw_w_lora_up,
               rw_a0=v_rw_a0, rw_a_lora_up=v_rw_a_lora_up, rw_k_k=v_rw_k_k, rw_k_a=v_rw_k_a, rw_r_k=v_rw_r_k,
               rw_ln_g=v_rw_ln_g, rw_ln_b=v_rw_ln_b, fox_b_f=v_fox_b_f, w_up_rwkv=v_w_up_rwkv, w_up_fox=v_w_up_fox,
               w_out=v_w_out, ple_proj=v_ple_proj, ple_gate_w=v_ple_gate_w, ple_norm_g=v_ple_norm_g,
               final_norm_g=v_final_norm_g)

    t, d = x.shape[1], x.shape[2]
    cw = rw_w0.shape[1]
    lr = rw_w_lora_up.shape[1]
    fh = fox_b_f.shape[1]
    fw = fh * HEAD
    rw_cols = 4 * cw + 2 * lr
    fox_cols = 4 * fw + fh
    assert 2 * lr == LANES and cw % LANES == 0 and fw % LANES == 0 and fh <= LANES
    xs = x[0]
    ps = p[0, 0]
    tgt = loss_target[0]

    groups = {}
    for n in BIG:
        groups.setdefault(wts[n].shape[2], []).append(n)
    groups = list(groups.values())
    slabs16 = []
    for gi, names in enumerate(groups):
        slab = jnp.concatenate([wts[n][0] for n in names], axis=0)
        rows, width = slab.shape
        tile_c = _div_tile(rows, 256, 32)
        slabs16 += _rows(f"weights_to_bf16_{gi}", lambda i, a: (a,), rows, tile_c, [(slab, _row_spec(tile_c, width))],
                         [(width, BF16)])
    first = [gi for gi, names in enumerate(groups) if not set(names).isdisjoint(FIRST_NEEDED)]
    later = [gi for gi in range(len(groups)) if gi not in first]
    gathered = dict(zip(first, _all_gather_shards([slabs16[gi] for gi in first])))
    *gather_state, gather_token = _gather_start([slabs16[gi] for gi in later])
    full = {}

    def assemble(gis):
        for gi in gis:
            g = _place_own_shard(f"weights_place_own_{gi}", gathered[gi], slabs16[gi])
            off = 0
            for n in groups[gi]:
                r = wts[n].shape[1]
                part = g[:, off:off + r, :]
                full[n] = (part.reshape(N_SHARD * r, -1) if n in ROW_SHARDED
                           else jnp.concatenate([part[s] for s in range(N_SHARD)], axis=1))
                off += r

    assemble(first)
    w_rw = full["w_in"][:, :rw_cols]
    w_fox = jnp.pad(full["w_in"][:, rw_cols:rw_cols + fox_cols], ((0, 0), (0, LANES - fh)))
    w_gate = full["w_in"][:, rw_cols + fox_cols:]
    wup_pad = jnp.pad(full["rw_w_lora_up"], ((0, lr), (0, 0)))
    aup_pad = jnp.pad(full["rw_a_lora_up"], ((lr, 0), (0, 0)))
    b_pad = _pad_lanes(fox_b_f)
    r_k_row = rw_r_k.reshape(1, cw)
    gf_row = final_norm_g.reshape(1, d)

    tile = _tile(t, (256, 128, 64, 32, 16, 8))
    tile_s = _tile(t, (128, 64, 32, 16, 8))
    n_s = t // tile_s
    full2 = lambda a: (a, _full_spec(a.shape))

    (h,) = _rows("norm1", lambda i, a, g: (_rms(a, g),), t, tile, [(xs, _row_spec(tile, d)), full2(norm_g)], [(d, BF16)])
    z_rw = _matmul("proj_rw", h, w_rw, after=gather_token)
    z_fox = _matmul("proj_fox", h, w_fox, after=z_rw[:8, :LANES])
    z_gate = _matmul("proj_gate", h, w_gate, after=z_fox[:8, :LANES])
    gathered.update(zip(later, _gather_forward(_gather_wait(*gather_state, z_gate))))
    assemble(later)

    pre_consts = [full2(rw_shift_mu), full2(rw_w0), full2(rw_a0), full2(wup_pad), full2(aup_pad), full2(rw_k_k),
                  full2(rw_k_a)]

    def pre_fwd(i, z, prev8, *consts):
        return _rw_pre(z, _shifted(i, z, prev8), *consts, cw=cw)

    r_, w_, k_, v_, kk_, a_, g_ = _rows(
        "rwkv_pre", pre_fwd, t, tile_s,
        [(z_rw, _row_spec(tile_s, rw_cols)), (z_rw, _prev_rows_spec(tile_s, rw_cols))] + pre_consts, [(cw, F32)] * 7)
    y_scan, states, tinvs = _scan_fwd(r_, w_, k_, v_, kk_, a_)
    post_consts = [full2(rw_ln_g), full2(rw_ln_b), full2(r_k_row)]
    post_rows = lambda *arrs: [(a, _row_spec(tile_s, cw)) for a in arrs]
    (y_rw,) = _rows("rwkv_post", lambda i, *a: (_rw_post(*a, cw=cw),), t, tile_s,
                    post_rows(y_scan, r_, k_, v_, g_) + post_consts, [(cw, BF16)])

    c_fox = _fox_cumsum(z_fox, b_pad, fw=fw, fh=fh)
    tq = _fox_tile(t)
    n_pair_f = fw // LANES
    head_rows = lambda a: a.T.reshape(n_pair_f, 2, t // tq, tq).transpose(0, 2, 1, 3)
    head_cols = lambda a: a.transpose(0, 2, 1, 3).reshape(fh, t).T
    ct_fox = head_rows(c_fox[:, :fh])
    o_fox, lse_fox, y_fox = _fox_fwd(z_fox, c_fox, ct_fox, fw=fw)

    u_rw = _matmul("up_rwkv", y_rw, full["w_up_rwkv"])
    u_fox = _matmul("up_fox", y_fox, full["w_up_fox"])
    (merged,) = _rows("merge", lambda i, zg, a, b: (_merge(zg, a, b, d=d),), t, tile,
                      [(z_gate, _row_spec(tile, 2 * d)), (u_rw, _row_spec(tile, d)), (u_fox, _row_spec(tile, d))],
                      [(d, BF16)])
    x1 = _matmul("out_proj", merged, full["w_out"], add=xs)
    (n2,) = _rows("norm2", lambda i, a, g: (_rms(a, g),), t, tile, [(x1, _row_spec(tile, d)), full2(ple_norm_g)],
                  [(d, BF16)])
    gl = _matmul("ple_gate", n2, full["ple_gate_w"])
    ple = _matmul("ple_proj", ps, full["ple_proj"])

    def head_bwd(i, x1_t, ple_t, gl_t, gf, tg):
        loss, vjp = jax.vjp(lambda a, b, cc, g: _head_loss(a, b, cc, g, tg), x1_t, ple_t, gl_t, gf)
        dx1, dple, dgl, dgf = vjp(jnp.ones((1, 1), F32))
        return dx1, dple, dgl, jnp.broadcast_to(loss, (1, LANES)), dgf

    dx2, dple, dgl, loss_row, d_gf = _rows(
        "loss_head", head_bwd, t, tile_s,
        [(x1, _row_spec(tile_s, d)), (ple, _row_spec(tile_s, d)), (gl, _row_spec(tile_s, d)), full2(gf_row),
         (tgt, _row_spec(tile_s, d))],
        [(d, F32), (d, BF16), (d, BF16)], [(1, LANES), (1, d)])

    g_ple_proj = _matmul("d_ple_proj", ps, dple, ta=True)
    g_ple_gate = _matmul("d_ple_gate_w", n2, dgl, ta=True)
    dn2 = _matmul("d_n2", dgl, full["ple_gate_w"], tb=True)

    def norm_bwd(i, a, g, dh, res):
        _, vjp = jax.vjp(_rms, a, g)
        da, dg = vjp(dh)
        return res + da, dg

    dx1, d_g2 = _rows("norm2_bwd", norm_bwd, t, tile_s,
                      [(x1, _row_spec(tile_s, d)), full2(ple_norm_g), (dn2, _row_spec(tile_s, d)),
                       (dx2, _row_spec(tile_s, d))], [(d, F32)], [(1, d)])
    g_w_out = _matmul("d_w_out", merged, dx1, ta=True)
    dmerged = _matmul("d_merged", dx1, full["w_out"], tb=True)

    def merge_bwd(i, zg, a, b, dm):
        _, vjp = jax.vjp(functools.partial(_merge, d=d), zg, a, b)
        return vjp(dm)

    dz_gate, du_rw, du_fox = _rows(
        "merge_bwd", merge_bwd, t, tile_s,
        [(z_gate, _row_spec(tile_s, 2 * d)), (u_rw, _row_spec(tile_s, d)), (u_fox, _row_spec(tile_s, d)),
         (dmerged, _row_spec(tile_s, d))], [(2 * d, BF16), (d, BF16), (d, BF16)])
    g_up_rw = _matmul("d_w_up_rwkv", y_rw, du_rw, ta=True)
    g_up_fox = _matmul("d_w_up_fox", y_fox, du_fox, ta=True)
    dy_rw = _matmul("d_y_rwkv", du_rw, full["w_up_rwkv"], tb=True)
    dy_fox = _matmul("d_y_fox", du_fox, full["w_up_fox"], tb=True)

    def post_bwd(i, y, r, k, v, g, ln_g, ln_b, r_k, dy):
        _, vjp = jax.vjp(functools.partial(_rw_post, cw=cw), y, r, k, v, g, ln_g, ln_b, r_k)
        return vjp(dy)

    dys, dr1, dk1, dv1, dg1, d_ln_g, d_ln_b, d_r_k = _rows(
        "rwkv_post_bwd", post_bwd, t, tile_s,
        post_rows(y_scan, r_, k_, v_, g_) + post_consts + post_rows(dy_rw), [(cw, F32)] * 5, [(1, cw)] * 3)
    dr2, dw2, dk2, dv2, dkk2, da2 = _scan_bwd(r_, w_, k_, v_, kk_, a_, states, tinvs, dys)

    def pre_bwd(i, z, prev8, mu, w0, a0, wup, aup, k_k, k_a, dr_a, dr_b, dk_a, dk_b, dv_a, dv_b, dw, dkk, da, dg):
        zp = _shifted(i, z, prev8)
        _, vjp = jax.vjp(functools.partial(_rw_pre, cw=cw), z, zp, mu, w0, a0, wup, aup, k_k, k_a)
        dz, dzp, dmu, dw0, da0, dwup, daup, dk_k, dk_a = vjp((dr_a + dr_b, dw, dk_a + dk_b, dv_a + dv_b, dkk, da, dg))
        row = lax.broadcasted_iota(jnp.int32, dz.shape, 0)
        dz = dz + jnp.where(row < tile_s - 1, pltpu.roll(dzp, tile_s - 1, 0), 0.0)
        first = jnp.where(lax.broadcasted_iota(jnp.int32, (8, dz.shape[1]), 0) == 0, _row_of(dzp, 0), 0.0)
        return dz, first, dmu, dw0, da0, dwup, daup, dk_k, dk_a

    def pre_bwd_call():
        n_in = 2 + len(pre_consts) + 10
        ins = ([(z_rw, _row_spec(tile_s, rw_cols)), (z_rw, _prev_rows_spec(tile_s, rw_cols))] + pre_consts
               + post_rows(dr1, dr2, dk1, dk2, dv1, dv2, dw2, dkk2, da2, dg1))

        def body(*refs):
            i = pl.program_id(0)
            vals = pre_bwd(i, *[r[...] for r in refs[:n_in]])
            refs[n_in][...] = vals[0]
            refs[n_in + 1][...] = vals[1]
            for r, v in zip(refs[n_in + 2:], vals[2:]):
                @pl.when(i == 0)
                def _(r=r, v=v):
                    r[...] = v

                @pl.when(i > 0)
                def _(r=r, v=v):
                    r[...] += v

        acc_shapes = [(1, rw_cols), (1, cw), (1, cw), (LANES, cw), (LANES, cw), (1, cw), (1, cw)]
        return pl.pallas_call(
            body, name="rwkv_pre_bwd", grid=(n_s,), in_specs=[s for _, s in ins],
            out_specs=[_row_spec(tile_s, rw_cols), pl.BlockSpec((8, rw_cols), lambda i: (i, 0))]
            + [_full_spec(s) for s in acc_shapes],
            out_shape=[jax.ShapeDtypeStruct((t, rw_cols), F32), jax.ShapeDtypeStruct((8 * n_s, rw_cols), F32)]
            + [jax.ShapeDtypeStruct(s, F32) for s in acc_shapes],
            compiler_params=_params("arbitrary"),
        )(*[a for a, _ in ins])

    dz_main, dz_first, d_mu, d_w0, d_a0, d_wup, d_aup, d_k_k, d_k_a = pre_bwd_call()

    def add_next_row(i, dz, nxt8):
        row = lax.broadcasted_iota(jnp.int32, dz.shape, 0)
        carry = jnp.where(i < n_s - 1, _row_of(nxt8, 0), 0.0)
        return (dz + jnp.where(row == tile_s - 1, carry, 0.0),)

    (dz_rw,) = _rows("rwkv_shift_bwd", add_next_row, t, tile_s,
                     [(dz_main, _row_spec(tile_s, rw_cols)), (dz_first, _next_rows_spec(tile_s, rw_cols, n_s))],
                     [(rw_cols, BF16)])

    def fox_post_bwd(i, o, g, dy):
        _, vjp = jax.vjp(lambda oo, gg: oo * (gg * jax.nn.sigmoid(gg)), o, g)
        do, dg = vjp(dy)
        return do, _head_sum(do * o, _head_matrix(fw)), dg

    do_fox, dd_fox, dg_fox = _rows(
        "fox_post_bwd", fox_post_bwd, t, tile_s,
        [(o_fox, _row_spec(tile_s, fw)), (z_fox, _row_spec(tile_s, fw, 3)), (dy_fox, _row_spec(tile_s, fw))],
        [(fw, F32), (fw, F32), (fw, BF16)])
    dq_f, dk_f, dv_f, dcq, dck = _fox_bwd(z_fox, do_fox, c_fox, ct_fox, head_rows(lse_fox[:, ::HEAD]),
                                          head_rows(dd_fox[:, ::HEAD]), fw=fw)
    dc = head_cols(dcq) + dck.reshape(t, n_pair_f, LANES)[:, :, :2].reshape(t, fh)
    dfl, d_bf = _fox_cumsum_bwd(z_fox, b_pad, jnp.pad(dc, ((0, 0), (0, LANES - fh))), fw=fw, fh=fh)
    dz_fox = jnp.concatenate([dq_f, dk_f, dv_f, dg_fox, dfl], axis=1)

    g_w_rw = _matmul("d_w_in_rw", h, dz_rw, ta=True)
    g_w_fox = _matmul("d_w_in_fox", h, dz_fox, ta=True)
    g_w_gate = _matmul("d_w_in_gate", h, dz_gate, ta=True)

    g_full = {
        "w_in": jnp.concatenate([g_w_rw, g_w_fox[:, :fox_cols], g_w_gate], axis=1),
        "rw_w_lora_up": d_wup[:lr], "rw_a_lora_up": d_aup[lr:], "w_up_rwkv": g_up_rw, "w_up_fox": g_up_fox,
        "w_out": g_w_out, "ple_proj": g_ple_proj, "ple_gate_w": g_ple_gate,
    }
    def by_shard(n):
        g = g_full[n]
        if n in ROW_SHARDED:
            return g.reshape(N_SHARD, g.shape[0] // N_SHARD, g.shape[1])
        return jnp.stack(jnp.split(g, N_SHARD, axis=1))

    rs_state, token = _reduce_scatter_start([jnp.concatenate([by_shard(n) for n in names], axis=1) for names in groups])
    dh = _matmul("d_h_rw", dz_rw, w_rw, tb=True, after=token)
    dh = _matmul("d_h_fox", dz_fox, w_fox, tb=True, add=dh)
    dh = _matmul("d_h_gate", dz_gate, w_gate, tb=True, add=dh)
    grad_x, d_g1 = _rows("norm1_bwd", norm_bwd, t, tile_s,
                         [(xs, _row_spec(tile_s, d)), full2(norm_g), (dh, _row_spec(tile_s, d)),
                          (dx1, _row_spec(tile_s, d))], [(d, F32)], [(1, d)])
    small_parts = dict(norm_g=d_g1, rw_shift_mu=d_mu, rw_w0=d_w0, rw_a0=d_a0, rw_k_k=d_k_k, rw_k_a=d_k_a, rw_r_k=d_r_k,
                       rw_ln_g=d_ln_g, rw_ln_b=d_ln_b, fox_b_f=d_bf[:, :fh], ple_norm_g=d_g2, final_norm_g=d_gf)
    n_small = sum(-(-wts[n].size // LANES) for n in SMALL)
    small_rows = -(-n_small // 8) * 8
    small_shapes = [wts[n].shape for n in SMALL]
    g_small = _all_reduce_small(_pack_small([small_parts[n] for n in SMALL], small_rows))
    reduced = _reduce_scatter_end(rs_state, g_small)
    grads = {}
    for names, g in zip(groups, reduced):
        off = 0
        for n in names:
            r = wts[n].shape[1]
            grads[n] = g[off:off + r][None]
            off += r

    for n, g in zip(SMALL, _unpack_small(g_small, small_shapes)):
        grads[n] = g

    delta, new_m, new_v = {}, {}, {}
    for n in BIG:
        delta[n], new_m[n], new_v[n] = _adamw("adamw_" + n, wts[n], grads[n], mom[n], vel[n])
    packed = lambda src: _pack_small([src[n] for n in SMALL], small_rows)[None]
    for store, out in zip((delta, new_m, new_v), _adamw("adamw_small", packed(wts), g_small[None], packed(mom), packed(vel))):
        for n, a in zip(SMALL, _unpack_small(out[0], small_shapes)):
            store[n] = a

    loss = lax.psum(loss_row[0, 0], ("x", "y", "c"))
    return (loss, grad_x[None], *[grads[n] for n in WEIGHTS], *[delta[n] for n in WEIGHTS],
            *[new_m[n] for n in WEIGHTS], *[new_v[n] for n in WEIGHTS])
```

```python
import functools

import jax
import jax.numpy as jnp
from jax import lax
from jax.experimental import pallas as pl
from jax.experimental.pallas import tpu as pltpu

F32 = jnp.float32
BF16 = jnp.bfloat16
HIGHEST = lax.Precision.HIGHEST
SCAN_PREC = lax.Precision.HIGH
MESH = pl.DeviceIdType.MESH

LANES = 128
HEAD = 64
NORM_EPS = 1e-6
GN_EPS = 64e-5
ADAM_LR = 0.001
ADAM_B1 = 0.9
ADAM_B2 = 0.999
ADAM_EPS = 1e-08
ADAM_WD = 0.01
ADAM_STEP = 10
N_SHARD = 4
VMEM_LIMIT = 56 * 1024 * 1024
MATMUL_VMEM_BUDGET = 40 * 1024 * 1024
PAIRS_PER_STEP = 8
ELIM_BASE = 8


def _params(*sem):
    return pltpu.CompilerParams(dimension_semantics=sem, vmem_limit_bytes=VMEM_LIMIT)


def _tile(n, cands):
    for c in cands:
        if c <= n and n % c == 0:
            return c
    return n


def _div_tile(n, cap, mult):
    return max(c for c in range(mult, min(n, cap) + 1, mult) if n % c == 0)


_ROW_TILES = (512, 256, 128, 64, 32, 16, 8)


def _dot(a, b, prec=None):
    return lax.dot_general(a, b, (((1,), (0,)), ((), ())), precision=prec, preferred_element_type=F32)


def _dot_nt(a, b, prec=None):
    return lax.dot_general(a, b, (((1,), (1,)), ((), ())), precision=prec, preferred_element_type=F32)


def _dot_tn(a, b, prec=None):
    return lax.dot_general(a, b, (((0,), (0,)), ((), ())), precision=prec, preferred_element_type=F32)


@jax.custom_vjp
def _bdot(x, w):
    return _dot(x.astype(BF16), w.astype(BF16))


def _bdot_fwd(x, w):
    return _bdot(x, w), (x, w)


def _bdot_bwd(res, ct):
    x, w = res
    return _dot_nt(ct.astype(BF16), w.astype(BF16)), _dot_tn(x.astype(BF16), ct.astype(BF16))


_bdot.defvjp(_bdot_fwd, _bdot_bwd)


def _head_matrix(width):
    c = lax.broadcasted_iota(jnp.int32, (width, LANES), 0)
    h = lax.broadcasted_iota(jnp.int32, (width, LANES), 1)
    return (c // HEAD == h).astype(F32)


def _head_sum(x, e):
    return _dot_nt(_dot(x, e, SCAN_PREC), e, SCAN_PREC)


def _softplus(x):
    return jnp.maximum(x, 0.0) + jnp.log1p(jnp.exp(-jnp.abs(x)))


def _lane_pick(x, idx):
    lane = lax.broadcasted_iota(jnp.int32, x.shape, 1)
    return jnp.sum(jnp.where(lane == idx, x, 0.0), axis=1, keepdims=True)


def _matmul_tiles(m, n, k, a_bytes, b_bytes, out_bytes, has_add):
    best = None
    for tk in [c for c in (k, 2048, 1408, 1024, 768, 640, 512, 384, 256, 128, 64, 32, 16) if c <= k and k % c == 0]:
        for tm in [c for c in (1024, 512, 256, 128) if m % c == 0] or [m]:
            for tn in [c for c in (1408, 1024, 768, 640, 512, 384, 256, 128) if n % c == 0] or [n]:
                nk = k // tk
                tile_out = tm * tn
                vmem = 2 * (tm * tk * a_bytes + tk * tn * b_bytes + tile_out * (out_bytes + 4 * has_add))
                vmem += tile_out * 4 * (2 if nk > 1 else 1)
                if vmem > MATMUL_VMEM_BUDGET:
                    continue
                traffic = m * k * a_bytes * (n // tn) + k * n * b_bytes * (m // tm)
                key = (nk > 1, traffic, nk)
                if best is None or key < best[0]:
                    best = (key, (tm, tn, tk))
    return best[1]


def _matmul(name, a, b, *, ta=False, tb=False, add=None, out_dtype=F32, after=None):
    m, k = (a.shape[1], a.shape[0]) if ta else a.shape
    n = b.shape[0] if tb else b.shape[1]
    tm, tn, tk = _matmul_tiles(m, n, k, a.dtype.itemsize, b.dtype.itemsize, jnp.dtype(out_dtype).itemsize,
                               add is not None)
    nk = k // tk
    dims = (((0 if ta else 1,), (1 if tb else 0,)), ((), ()))

    def body(*refs):
        a_ref, b_ref = refs[0], refs[1]
        o_ref, acc_ref = (refs[-1], None) if nk == 1 else refs[-2:]
        kk = pl.program_id(2)
        part = lax.dot_general(a_ref[...].astype(BF16), b_ref[...].astype(BF16), dims, preferred_element_type=F32)

        def finish(r):
            if add is not None:
                r = r + refs[2][...].astype(F32)
            o_ref[...] = r.astype(o_ref.dtype)

        if nk == 1:
            finish(part)
            return

        @pl.when(kk == 0)
        def _():
            acc_ref[...] = part

        @pl.when((kk > 0) & (kk < nk - 1))
        def _():
            acc_ref[...] += part

        @pl.when(kk == nk - 1)
        def _():
            finish(acc_ref[...] + part)

    a_spec = pl.BlockSpec((tk, tm), lambda i, j, kk: (kk, i)) if ta else pl.BlockSpec((tm, tk), lambda i, j, kk: (i, kk))
    b_spec = pl.BlockSpec((tn, tk), lambda i, j, kk: (j, kk)) if tb else pl.BlockSpec((tk, tn), lambda i, j, kk: (kk, j))
    o_spec = pl.BlockSpec((tm, tn), lambda i, j, kk: (i, j))
    ins, specs = [a, b], [a_spec, b_spec]
    if add is not None:
        ins.append(add)
        specs.append(o_spec)
    if after is not None:
        ins.append(after)
        specs.append(pl.BlockSpec(after.shape, lambda i, j, kk: (0,) * after.ndim))
    return pl.pallas_call(
        body, name=name, grid=(m // tm, n // tn, nk), in_specs=specs, out_specs=o_spec,
        out_shape=jax.ShapeDtypeStruct((m, n), out_dtype),
        scratch_shapes=[pltpu.VMEM((tm, tn), F32)] if nk > 1 else [],
        compiler_params=_params("parallel", "parallel", "arbitrary"),
    )(*ins)


def _rows(name, fn, n_rows, tile, ins, outs, accs=()):
    n_in, n_out = len(ins), len(outs)

    def body(*refs):
        i = pl.program_id(0)
        vals = fn(i, *[r[...] for r in refs[:n_in]])
        for r, v in zip(refs[n_in:n_in + n_out], vals[:n_out]):
            r[...] = v.astype(r.dtype)
        for r, v in zip(refs[n_in + n_out:], vals[n_out:]):
            @pl.when(i == 0)
            def _(r=r, v=v):
                r[...] = v

            @pl.when(i > 0)
            def _(r=r, v=v):
                r[...] += v

    out_specs = [pl.BlockSpec((tile, w), lambda i: (i, 0)) for w, _ in outs]
    out_specs += [pl.BlockSpec(s, lambda i: (0, 0)) for s in accs]
    out_shape = [jax.ShapeDtypeStruct((n_rows, w), d) for w, d in outs]
    out_shape += [jax.ShapeDtypeStruct(s, F32) for s in accs]
    return pl.pallas_call(
        body, name=name, grid=(n_rows // tile,), in_specs=[s for _, s in ins], out_specs=out_specs,
        out_shape=out_shape, compiler_params=_params("arbitrary"),
    )(*[a for a, _ in ins])


def _row_spec(tile, width, col=0):
    return pl.BlockSpec((tile, width), lambda i: (i, col))


def _full_spec(shape):
    return pl.BlockSpec(shape, lambda i: (0,) * len(shape))


def _prev_rows_spec(tile, width):
    return pl.BlockSpec((8, width), lambda i: (jnp.maximum(i * (tile // 8) - 1, 0), 0))


def _next_rows_spec(tile, width, n_tiles):
    return pl.BlockSpec((8, width), lambda i: (jnp.minimum(i + 1, n_tiles - 1), 0))


def _row_of(x8, idx):
    r = lax.broadcasted_iota(jnp.int32, x8.shape, 0)
    return jnp.sum(jnp.where(r == idx, x8, 0.0), axis=0, keepdims=True)


def _rms(x, g):
    return x * lax.rsqrt(jnp.mean(x * x, axis=-1, keepdims=True) + NORM_EPS) * g


def _shifted(i, z, prev8):
    first = jnp.where(i > 0, _row_of(prev8, 7), 0.0)
    row = lax.broadcasted_iota(jnp.int32, z.shape, 0)
    return jnp.where(row == 0, first, pltpu.roll(z, 1, 0))


def _rw_pre(z, zp, mu, w0, a0, wup, aup, k_k, k_a, *, cw):
    zs = z + (zp - z) * mu
    r, k, v, g = (zs[:, j * cw:(j + 1) * cw] for j in range(4))
    lo = zs[:, 4 * cw:4 * cw + LANES]
    w_raw = w0 + _bdot(jnp.tanh(lo), wup)
    decay = jnp.exp(-jnp.exp(-_softplus(-w_raw) - 0.5))
    a = jax.nn.sigmoid(a0 + _bdot(lo, aup))
    e = _head_matrix(cw)
    kk = k * k_k
    kk = kk / jnp.maximum(jnp.sqrt(_head_sum(kk * kk, e)), 1e-12)
    k_mod = k * (1.0 + (a - 1.0) * k_a)
    return r, decay, k_mod, v, kk, a, g


def _rw_post(y, r, k_mod, v, g, ln_g, ln_b, r_k, *, cw):
    e = _head_matrix(cw)
    mu = _head_sum(y, e) * (1.0 / HEAD)
    d = y - mu
    var = _head_sum(d * d, e) * (1.0 / HEAD)
    yn = d * lax.rsqrt(var + GN_EPS) * ln_g + ln_b
    bonus = _head_sum(r * k_mod * r_k, e) * v
    return (yn + bonus) * (g * jax.nn.sigmoid(g))


def _merge(zg, u_rw, u_fox, *, d):
    return jax.nn.sigmoid(zg[:, :d]) * u_rw + jax.nn.sigmoid(zg[:, d:]) * u_fox


def _head_loss(x1, ple, gl, gf, tgt):
    x2 = x1 + ple * jax.nn.sigmoid(gl)
    err = _rms(x2, gf) - tgt
    return 0.5 * jnp.sum(jnp.mean(err * err, axis=-1, keepdims=True), axis=0, keepdims=True)


def _batched(a, b, ca, cb):
    return lax.dot_general(a, b, (((ca,), (cb,)), ((0,), (0,))), precision=SCAN_PREC, preferred_element_type=F32)


def _eliminate_block(lo, off, m):
    n, c, _ = lo.shape
    ci = lax.broadcasted_iota(jnp.int32, (n, m, c), 2)
    if m <= ELIM_BASE:
        ri = lax.broadcasted_iota(jnp.int32, (n, m, c), 1)
        rows = jnp.where((ci >= off) & (ci < off + m), lo[:, off:off + m, :], 0.0)
        x = (ri + off == ci).astype(F32)
        for s in range(m - 1):
            col = jnp.sum(jnp.where(ci == off + s, rows, 0.0), axis=2, keepdims=True)
            x = x - col * x[:, s:s + 1, :]
        return x
    h = m // 2
    xa = _eliminate_block(lo, off, h)
    xd = _eliminate_block(lo, off + h, h)
    ci_h = lax.broadcasted_iota(jnp.int32, (n, h, c), 2)
    b = jnp.where((ci_h >= off) & (ci_h < off + h), lo[:, off + h:off + m, :], 0.0)

    def at_rows(x, start):
        parts = ([jnp.zeros((n, start, c), F32)] if start else []) + [x]
        rest = c - start - x.shape[1]
        return jnp.concatenate(parts + ([jnp.zeros((n, rest, c), F32)] if rest else []), axis=1)

    low = xd - _batched(_batched(xd, at_rows(b, off + h), 2, 1), at_rows(xa, off), 2, 1)
    return jnp.concatenate([xa, low], axis=1)


def _eliminate(lo):
    return _eliminate_block(lo, 0, lo.shape[1])


@jax.custom_vjp
def _unit_lower_inverse(lo, known):
    return _eliminate(lo) if known is None else known


def _uli_fwd(lo, known):
    x = _unit_lower_inverse(lo, known)
    return x, (x, known)


def _uli_bwd(res, dx):
    x, known = res
    dlo = -_batched(_batched(x, dx, 1, 1), x, 2, 2)
    return dlo, (None if known is None else jnp.zeros_like(known))


_unit_lower_inverse.defvjp(_uli_fwd, _uli_bwd)


def _rwkv_chunk(s0, r, w, k, v, kk, a, *, c, tinv_known=None):
    pairs = range(len(s0))
    lane = lax.broadcasted_iota(jnp.int32, (1, LANES), 1)
    heads = (lane < HEAD, lane >= HEAD)
    ti = lax.broadcasted_iota(jnp.int32, (c, c), 0)
    si = lax.broadcasted_iota(jnp.int32, (c, c), 1)
    incl = si <= ti
    strict = si < ti
    tri = incl.astype(F32)
    logw = [jnp.log(w[p]) for p in pairs]
    cum = [_dot(tri, logw[p], HIGHEST) for p in pairs]
    cum_end = [jnp.sum(logw[p], axis=0, keepdims=True) for p in pairs]
    g_inv = [jnp.exp(-cum[p]) for p in pairs]
    to_end = [jnp.exp(cum_end[p] - cum[p]) for p in pairs]
    b = [kk[p] * a[p] for p in pairs]
    beta = [b[p] * g_inv[p] for p in pairs]
    kap = [kk[p] * jnp.exp(cum[p] - logw[p]) for p in pairs]
    kt = [k[p] * g_inv[p] for p in pairs]
    rt = [r[p] * jnp.exp(cum[p]) for p in pairs]
    lhs = [jnp.concatenate([jnp.where(m, x[p], 0.0) for x in (kap, rt) for m in heads], axis=0) for p in pairs]
    vs_beta = [_dot_nt(lhs[p], beta[p], None) for p in pairs]
    vs_kt = [_dot_nt(lhs[p], kt[p], None) for p in pairs]
    strict2 = jnp.concatenate([strict, strict], axis=0)
    incl2 = jnp.concatenate([incl, incl], axis=0)
    lo = [jnp.where(strict2, vs_beta[p][:2 * c], 0.0) for p in pairs]
    mm = [jnp.where(strict2, vs_kt[p][:2 * c], 0.0) for p in pairs]
    arb = [jnp.where(incl2, vs_beta[p][2 * c:], 0.0) for p in pairs]
    ark = [jnp.where(incl2, vs_kt[p][2 * c:], 0.0) for p in pairs]
    per_head = lambda xs: jnp.concatenate([xs[p][h * c:(h + 1) * c][None] for p in pairs for h in (0, 1)])
    tinv = _unit_lower_inverse(per_head(lo), None if tinv_known is None else per_head(tinv_known))
    tinv = [jnp.concatenate([tinv[2 * p], tinv[2 * p + 1]], axis=0) for p in pairs]
    both = lambda x: jnp.where(heads[0], x[:c], x[c:])
    vs_s = [_dot_nt(jnp.concatenate([kap[p], rt[p]], axis=0), s0[p], None) for p in pairs]
    rhs = [vs_s[p][:c] + both(_dot(mm[p], v[p], None)) for p in pairs]
    u = [-both(_dot(tinv[p], rhs[p], None)) for p in pairs]
    y = [vs_s[p][c:] + both(_dot(arb[p], u[p], None) + _dot(ark[p], v[p], None)) for p in pairs]
    rr = lax.broadcasted_iota(jnp.int32, (LANES, LANES), 0) < HEAD
    cc = lax.broadcasted_iota(jnp.int32, (LANES, LANES), 1) < HEAD
    ds = [_dot_tn(jnp.concatenate([u[p], v[p]], axis=0),
                  jnp.concatenate([b[p] * to_end[p], k[p] * to_end[p]], axis=0), None) for p in pairs]
    s1 = [s0[p] * jnp.exp(cum_end[p]) + jnp.where(rr == cc, ds[p], 0.0) for p in pairs]
    return tuple(y), tuple(s1), tuple(tinv)


def _scan_tiles(t, n_pair):
    return _tile(t, (32, 16, 8)), _tile(t, (256, 128, 64, 32)), _tile(n_pair, (PAIRS_PER_STEP, 4, 2, 1))


def _scan_fwd(r, w, k, v, kk, a):
    t, width = r.shape
    c, tb, npb = _scan_tiles(t, width // LANES)
    n_grp, n_blk, n_cb = width // (LANES * npb), t // tb, tb // c

    def body(r_ref, w_ref, k_ref, v_ref, kk_ref, a_ref, y_ref, st_ref, ti_ref, s_scr):
        @pl.when(pl.program_id(1) == 0)
        def _():
            s_scr[...] = jnp.zeros_like(s_scr)

        def chunk(j, carry):
            sl = pl.ds(pl.multiple_of(j * c, c), c)
            lanes = [pl.ds(q * LANES, LANES) for q in range(npb)]
            s0 = tuple(s_scr[q] for q in range(npb))
            cols = lambda ref: tuple(ref[sl, ln] for ln in lanes)
            y, s1, tinv = _rwkv_chunk(s0, cols(r_ref), cols(w_ref), cols(k_ref), cols(v_ref), cols(kk_ref), cols(a_ref),
                                      c=c)
            for q, ln in enumerate(lanes):
                st_ref[q, j] = s0[q]
                ti_ref[q, j] = tinv[q]
                y_ref[sl, ln] = y[q]
                s_scr[q] = s1[q]
            return carry

        lax.fori_loop(0, n_cb, chunk, 0)

    blk = pl.BlockSpec((tb, npb * LANES), lambda p, i: (i, p))
    return pl.pallas_call(
        body, name="rwkv_scan_fwd", grid=(n_grp, n_blk), in_specs=[blk] * 6,
        out_specs=[blk, pl.BlockSpec((npb, n_cb, LANES, LANES), lambda p, i: (p, i, 0, 0)),
                   pl.BlockSpec((npb, n_cb, 2 * c, c), lambda p, i: (p, i, 0, 0))],
        out_shape=[jax.ShapeDtypeStruct((t, width), F32),
                   jax.ShapeDtypeStruct((width // LANES, t // c, LANES, LANES), F32),
                   jax.ShapeDtypeStruct((width // LANES, t // c, 2 * c, c), F32)],
        scratch_shapes=[pltpu.VMEM((npb, LANES, LANES), F32)],
        compiler_params=_params("arbitrary", "arbitrary"),
    )(r, w, k, v, kk, a)


def _scan_bwd(r, w, k, v, kk, a, st, ti, dy):
    t, width = r.shape
    c, tb, npb = _scan_tiles(t, width // LANES)
    n_grp, n_blk, n_cb = width // (LANES * npb), t // tb, tb // c

    def body(r_ref, w_ref, k_ref, v_ref, kk_ref, a_ref, st_ref, ti_ref, dy_ref,
             dr_ref, dw_ref, dk_ref, dv_ref, dkk_ref, da_ref, ds_scr):
        @pl.when(pl.program_id(1) == 0)
        def _():
            ds_scr[...] = jnp.zeros_like(ds_scr)

        def chunk(jj, carry):
            j = n_cb - 1 - jj
            sl = pl.ds(pl.multiple_of(j * c, c), c)
            lanes = [pl.ds(q * LANES, LANES) for q in range(npb)]
            cols = lambda ref: tuple(ref[sl, ln] for ln in lanes)
            args = (tuple(st_ref[q, j] for q in range(npb)), cols(r_ref), cols(w_ref), cols(k_ref), cols(v_ref),
                    cols(kk_ref), cols(a_ref))
            known = tuple(ti_ref[q, j] for q in range(npb))
            _, vjp = jax.vjp(lambda *xs: _rwkv_chunk(*xs, c=c, tinv_known=known)[:2], *args)
            grads = vjp((cols(dy_ref), tuple(ds_scr[q] for q in range(npb))))
            for q, ln in enumerate(lanes):
                ds_scr[q] = grads[0][q]
                for ref, g in zip((dr_ref, dw_ref, dk_ref, dv_ref, dkk_ref, da_ref), grads[1:]):
                    ref[sl, ln] = g[q]
            return carry

        lax.fori_loop(0, n_cb, chunk, 0)

    blk = pl.BlockSpec((tb, npb * LANES), lambda p, i: (n_blk - 1 - i, p))
    stb = pl.BlockSpec((npb, n_cb, LANES, LANES), lambda p, i: (p, n_blk - 1 - i, 0, 0))
    tib = pl.BlockSpec((npb, n_cb, 2 * c, c), lambda p, i: (p, n_blk - 1 - i, 0, 0))
    return pl.pallas_call(
        body, name="rwkv_scan_bwd", grid=(n_grp, n_blk), in_specs=[blk] * 6 + [stb, tib, blk], out_specs=[blk] * 6,
        out_shape=[jax.ShapeDtypeStruct((t, width), F32)] * 6,
        scratch_shapes=[pltpu.VMEM((npb, LANES, LANES), F32)],
        compiler_params=_params("arbitrary", "arbitrary"),
    )(r, w, k, v, kk, a, st, ti, dy)


NEG = -1e30


def _fox_cumsum(zf, b_pad, *, fw, fh):
    t = zf.shape[0]
    tile = _tile(t, (256, 128, 64, 32, 16, 8))

    def body(fl_ref, b_ref, c_ref, carry):
        @pl.when(pl.program_id(0) == 0)
        def _():
            carry[...] = jnp.zeros_like(carry)

        lane = lax.broadcasted_iota(jnp.int32, (tile, LANES), 1)
        logf = jnp.where(lane < fh, -_softplus(-(fl_ref[...] + b_ref[...])), 0.0)
        ri = lax.broadcasted_iota(jnp.int32, (tile, tile), 0)
        ci = lax.broadcasted_iota(jnp.int32, (tile, tile), 1)
        c_ref[...] = carry[...] + _dot((ci <= ri).astype(F32), logf, HIGHEST)
        carry[...] += jnp.sum(logf, axis=0, keepdims=True)

    return pl.pallas_call(
        body, name="fox_cumsum", grid=(t // tile,),
        in_specs=[_row_spec(tile, LANES, 4 * fw // LANES), _full_spec((1, LANES))],
        out_specs=_row_spec(tile, LANES), out_shape=jax.ShapeDtypeStruct((t, LANES), F32),
        scratch_shapes=[pltpu.VMEM((1, LANES), F32)], compiler_params=_params("arbitrary"),
    )(zf, b_pad)


def _fox_cumsum_bwd(zf, b_pad, dc, *, fw, fh):
    t = zf.shape[0]
    tile = _tile(t, (256, 128, 64, 32, 16, 8))
    n = t // tile

    def body(fl_ref, b_ref, dc_ref, dfl_ref, db_ref, carry):
        i = pl.program_id(0)

        @pl.when(i == 0)
        def _():
            carry[...] = jnp.zeros_like(carry)
            db_ref[...] = jnp.zeros_like(db_ref)

        lane = lax.broadcasted_iota(jnp.int32, (tile, LANES), 1)
        dc_t = dc_ref[...]
        ri = lax.broadcasted_iota(jnp.int32, (tile, tile), 0)
        ci = lax.broadcasted_iota(jnp.int32, (tile, tile), 1)
        dlogf = carry[...] + _dot((ci >= ri).astype(F32), dc_t, HIGHEST)
        carry[...] += jnp.sum(dc_t, axis=0, keepdims=True)
        dfl = jnp.where(lane < fh, dlogf * jax.nn.sigmoid(-(fl_ref[...] + b_ref[...])), 0.0)
        dfl_ref[...] = dfl.astype(dfl_ref.dtype)
        db_ref[...] += jnp.sum(dfl, axis=0, keepdims=True)

    rev = lambda col: pl.BlockSpec((tile, LANES), lambda i: (n - 1 - i, col))
    return pl.pallas_call(
        body, name="fox_cumsum_bwd", grid=(n,),
        in_specs=[rev(4 * fw // LANES), _full_spec((1, LANES)), rev(0)],
        out_specs=[rev(0), _full_spec((1, LANES))],
        out_shape=[jax.ShapeDtypeStruct((t, LANES), BF16), jax.ShapeDtypeStruct((1, LANES), F32)],
        scratch_shapes=[pltpu.VMEM((1, LANES), F32)], compiler_params=_params("arbitrary"),
    )(zf, b_pad, dc)


def _fox_tile(t):
    return _tile(t, (512, 256, 128))


def _fox_fwd(zf, c, ct, *, fw):
    t = zf.shape[0]
    tq = _fox_tile(t)
    th = tq // 2
    n_pair, n_q = fw // LANES, t // tq
    scale = HEAD ** -0.5
    chains = [(h, qq) for h in (0, 1) for qq in (0, 1)]

    def body(q_ref, k_ref, v_ref, g_ref, c_ref, ct_ref, o_ref, lse_ref, y_ref):
        hp, i = pl.program_id(0), pl.program_id(1)
        lane = lax.broadcasted_iota(jnp.int32, (1, LANES), 1)
        in_head = (lane < HEAD, lane >= HEAD)
        rows = [pl.ds(qq * th, th) for qq in (0, 1)]
        qh = [jnp.where(in_head[h], q_ref[rows[qq], :] * scale, 0.0).astype(BF16) for h, qq in chains]
        cq = [_lane_pick(c_ref[rows[qq], :], 2 * hp + h) for h, qq in chains]
        qidx = lax.broadcasted_iota(jnp.int32, (th, tq), 0)
        kidx = lax.broadcasted_iota(jnp.int32, (th, tq), 1)

        def kv_step(j, carry, diagonal):
            m, l, acc = carry
            ks = pl.ds(pl.multiple_of(j * tq, tq), tq)
            kb = k_ref[ks, :].astype(BF16)
            vb = v_ref[ks, :]
            vh = [jnp.where(in_head[h], vb, 0.0).astype(BF16) for h in (0, 1)]
            ck = [ct_ref[0, j, pl.ds(h, 1), :] for h in (0, 1)]
            s = [_dot_nt(qh[n], kb) + cq[n] - ck[h] for n, (h, qq) in enumerate(chains)]
            if diagonal:
                s = [jnp.where(qq * th + qidx >= kidx, s[n], NEG) for n, (h, qq) in enumerate(chains)]
            m_new = [jnp.maximum(m[n], jnp.max(s[n], axis=1, keepdims=True)) for n in range(4)]
            p = [jnp.exp(s[n] - m_new[n]) for n in range(4)]
            alpha = [jnp.exp(m[n] - m_new[n]) for n in range(4)]
            l = [l[n] * alpha[n] + jnp.sum(p[n], axis=1, keepdims=True) for n in range(4)]
            pv = [_dot(p[n].astype(BF16), vh[h]) for n, (h, qq) in enumerate(chains)]
            acc = [acc[qq] * jnp.where(in_head[0], alpha[qq], alpha[2 + qq]) + pv[qq] + pv[2 + qq] for qq in (0, 1)]
            return tuple(m_new), tuple(l), tuple(acc)

        init = (tuple(jnp.full((th, 1), NEG, F32) for _ in chains), tuple(jnp.zeros((th, 1), F32) for _ in chains),
                tuple(jnp.zeros((th, LANES), F32) for _ in (0, 1)))
        carry = lax.fori_loop(0, i, functools.partial(kv_step, diagonal=False), init)
        m, l, acc = kv_step(i, carry, True)
        for qq in (0, 1):
            o = acc[qq] / jnp.where(in_head[0], l[qq], l[2 + qq])
            g = g_ref[rows[qq], :]
            o_ref[rows[qq], :] = o
            lse_ref[rows[qq], :] = jnp.where(in_head[0], m[qq] + jnp.log(l[qq]), m[2 + qq] + jnp.log(l[2 + qq]))
            y_ref[rows[qq], :] = (o * (g * jax.nn.sigmoid(g))).astype(y_ref.dtype)

    npw = fw // LANES
    blk = lambda col0: pl.BlockSpec((tq, LANES), lambda hp, i: (i, col0 + hp))
    res = lambda col0: pl.BlockSpec((t, LANES), lambda hp, i: (0, col0 + hp))
    out_blk = pl.BlockSpec((tq, LANES), lambda hp, i: (i, hp))
    return pl.pallas_call(
        body, name="fox_attn_fwd", grid=(n_pair, n_q),
        in_specs=[blk(0), res(npw), res(2 * npw), blk(3 * npw),
                  pl.BlockSpec((tq, LANES), lambda hp, i: (i, 0)),
                  pl.BlockSpec((1, n_q, 2, tq), lambda hp, i: (hp, 0, 0, 0))],
        out_specs=[out_blk, out_blk, out_blk],
        out_shape=[jax.ShapeDtypeStruct((t, fw), F32), jax.ShapeDtypeStruct((t, fw), F32),
                   jax.ShapeDtypeStruct((t, fw), BF16)],
        compiler_params=_params("arbitrary", "arbitrary"),
    )(zf, zf, zf, zf, c, ct)


def _fox_bwd(zf, do, c, ct, lse_r, dd_r, *, fw):
    t = zf.shape[0]
    tq = _fox_tile(t)
    n_pair, n_q = fw // LANES, t // tq
    scale = HEAD ** -0.5

    def body(q_ref, k_ref, v_ref, do_ref, c_ref, ct_ref, lse_ref, dd_ref,
             dq_ref, dk_ref, dv_ref, dcq_ref, dck_ref, dq_acc, dcq_acc):
        hp, j = pl.program_id(0), pl.program_id(1)

        @pl.when(j == 0)
        def _():
            dq_acc[...] = jnp.zeros_like(dq_acc)
            dcq_acc[...] = jnp.zeros_like(dcq_acc)

        lane = lax.broadcasted_iota(jnp.int32, (1, LANES), 1)
        in_head = (lane < HEAD, lane >= HEAD)
        kb = k_ref[...]
        kh = [jnp.where(m, kb, 0.0).astype(BF16) for m in in_head]
        vb = v_ref[...].astype(BF16)
        c_k = c_ref[...]
        ck = [_lane_pick(c_k, 2 * hp + h) for h in (0, 1)]
        kidx = lax.broadcasted_iota(jnp.int32, (tq, tq), 0)
        qidx = lax.broadcasted_iota(jnp.int32, (tq, tq), 1)

        def q_step(i, carry, diagonal):
            dk, dv, dck = carry
            qs = pl.ds(pl.multiple_of(i * tq, tq), tq)
            qf = q_ref[qs, :] * scale
            dof = do_ref[qs, :]
            qh = [jnp.where(m, qf, 0.0).astype(BF16) for m in in_head]
            doh = [jnp.where(m, dof, 0.0).astype(BF16) for m in in_head]
            row = lambda ref, h: ref[0, i, pl.ds(h, 1), :]
            st = [_dot_nt(kh[h], qh[h]) + row(ct_ref, h) - ck[h] for h in (0, 1)]
            p = [jnp.exp(st[h] - row(lse_ref, h)) for h in (0, 1)]
            if diagonal:
                p = [jnp.where(kidx <= qidx, p[h], 0.0) for h in (0, 1)]
            dst = [p[h] * (_dot_nt(vb, doh[h]) - row(dd_ref, h)) for h in (0, 1)]
            p16 = [x.astype(BF16) for x in p]
            ds16 = [x.astype(BF16) for x in dst]
            dv = dv + _dot(p16[0], doh[0]) + _dot(p16[1], doh[1])
            dk = dk + _dot(ds16[0], qh[0]) + _dot(ds16[1], qh[1])
            dq_acc[qs, :] += _dot_tn(ds16[0], kh[0]) + _dot_tn(ds16[1], kh[1])
            for h in (0, 1):
                dcq_acc[i, pl.ds(h, 1), :] += jnp.sum(dst[h], axis=0, keepdims=True)
            dck = tuple(dck[h] - jnp.sum(dst[h], axis=1, keepdims=True) for h in (0, 1))
            return dk, dv, dck

        zero = jnp.zeros((tq, LANES), F32)
        carry = q_step(j, (zero, zero, (jnp.zeros((tq, 1), F32),) * 2), True)
        dk, dv, dck = lax.fori_loop(j + 1, n_q, functools.partial(q_step, diagonal=False), carry)
        dk_ref[...] = dk.astype(dk_ref.dtype)
        dv_ref[...] = dv.astype(dv_ref.dtype)
        dck_ref[...] = jnp.where(lane == 0, dck[0], jnp.where(lane == 1, dck[1], 0.0))

        @pl.when(j == n_q - 1)
        def _():
            dq_ref[...] = (dq_acc[...] * scale).astype(dq_ref.dtype)
            dcq_ref[0] = dcq_acc[...]

    npw = fw // LANES
    res_z = lambda col0: pl.BlockSpec((t, LANES), lambda hp, j: (0, col0 + hp))
    blk_z = lambda col0: pl.BlockSpec((tq, LANES), lambda hp, j: (j, col0 + hp))
    res = pl.BlockSpec((t, LANES), lambda hp, j: (0, hp))
    blk = pl.BlockSpec((tq, LANES), lambda hp, j: (j, hp))
    rows = pl.BlockSpec((1, n_q, 2, tq), lambda hp, j: (hp, 0, 0, 0))
    return pl.pallas_call(
        body, name="fox_attn_bwd", grid=(n_pair, n_q),
        in_specs=[res_z(0), blk_z(npw), blk_z(2 * npw), res, pl.BlockSpec((tq, LANES), lambda hp, j: (j, 0)),
                  rows, rows, rows],
        out_specs=[res, blk, blk, rows, blk],
        out_shape=[jax.ShapeDtypeStruct((t, fw), BF16), jax.ShapeDtypeStruct((t, fw), BF16),
                   jax.ShapeDtypeStruct((t, fw), BF16), jax.ShapeDtypeStruct((n_pair, n_q, 2, tq), F32),
                   jax.ShapeDtypeStruct((t, fw), F32)],
        scratch_shapes=[pltpu.VMEM((t, LANES), F32), pltpu.VMEM((n_q, 2, tq), F32)],
        compiler_params=_params("arbitrary", "arbitrary"),
    )(zf, zf, zf, do, c, ct, lse_r, dd_r)


def _adamw_math(w, g, m, v):
    m = ADAM_B1 * m + (1.0 - ADAM_B1) * g
    v = ADAM_B2 * v + (1.0 - ADAM_B2) * jnp.square(g)
    m_hat = m / (1.0 - ADAM_B1 ** ADAM_STEP)
    v_hat = v / (1.0 - ADAM_B2 ** ADAM_STEP)
    delta = -ADAM_LR * (m_hat / (jnp.sqrt(v_hat) + ADAM_EPS) + ADAM_WD * w)
    return delta, m, v


def _adamw(name, w, g, m, v):
    lead, rows, cols = w.shape
    if lead == 1 and cols % LANES:
        outs = _adamw(name, *[jnp.transpose(a, (2, 0, 1)) for a in (w, g, m, v)])
        return [jnp.transpose(o, (1, 2, 0)) for o in outs]
    if lead == 1:
        tile = _tile(rows, (128, 64, 32, 16, 8))
        spec, steps = pl.BlockSpec((1, tile, cols), lambda i: (0, i, 0)), rows // tile
    else:
        tile = _div_tile(lead, 256, 1)
        spec, steps = pl.BlockSpec((tile, rows, cols), lambda i: (i, 0, 0)), lead // tile

    def body(w_ref, g_ref, m_ref, v_ref, d_ref, mo_ref, vo_ref):
        d_ref[...], mo_ref[...], vo_ref[...] = _adamw_math(w_ref[...], g_ref[...], m_ref[...], v_ref[...])

    return pl.pallas_call(
        body, name=name, grid=(steps,), in_specs=[spec] * 4, out_specs=[spec] * 3,
        out_shape=[jax.ShapeDtypeStruct(w.shape, F32)] * 3, compiler_params=_params("parallel"),
    )(w, g, m, v)


def _place():
    return lax.axis_index("x"), lax.axis_index("y"), lax.axis_index("c")


def _other_chips(x, y):
    return [(1 - x, y), (x, 1 - y), (1 - x, 1 - y)]


HBM_SPEC = pl.BlockSpec(memory_space=pltpu.HBM)
SEM_SPEC = pl.BlockSpec(memory_space=pltpu.SEMAPHORE)


def _all_gather_shards(slabs):
    n = len(slabs)

    def body(*refs):
        src_refs, out_refs, send_sems, recv_sems = refs[:n], refs[n:2 * n], refs[2 * n], refs[2 * n + 1]
        x, y, c = _place()
        me = 2 * x + y
        sibling = (x, y, 1 - c)
        chips = _other_chips(x, y)
        first, passed, waits = [], [], []
        for g, (src_ref, out_ref) in enumerate(zip(src_refs, out_refs)):
            rh = src_ref.shape[0] // 2

            def part(chip, half, out_ref=out_ref, rh=rh):
                return out_ref.at[chip, pl.ds(half * rh, rh), :]

            def copy(k, src, dst, to, g=g):
                return pltpu.make_async_remote_copy(src_ref=src, dst_ref=dst, send_sem=send_sems.at[6 * g + k],
                                                    recv_sem=recv_sems.at[6 * g + k], device_id=to, device_id_type=MESH)

            first += [copy(j, src_ref.at[pl.ds(c * rh, rh), :], part(me, c), (px, py, c))
                      for j, (px, py) in enumerate(chips)]
            for j, (px, py) in enumerate(chips):
                theirs = part(2 * px + py, c)
                passed.append((copy(j, theirs, theirs, sibling), copy(3 + j, theirs, theirs, sibling)))
                other = part(2 * px + py, 1 - c)
                waits.append(copy(3 + j, other, other, sibling))
        for cp in first:
            cp.start()
        for landed, forward in passed:
            landed.wait_recv()
            forward.start()
        for cp in waits:
            cp.wait_recv()
        for cp in first + [fwd for _, fwd in passed]:
            cp.wait_send()

    return pl.pallas_call(
        body, name="weights_all_gather", in_specs=[HBM_SPEC] * n, out_specs=[HBM_SPEC] * n,
        out_shape=[jax.ShapeDtypeStruct((N_SHARD,) + a.shape, a.dtype) for a in slabs],
        scratch_shapes=[pltpu.SemaphoreType.DMA((6 * n,)), pltpu.SemaphoreType.DMA((6 * n,))],
    )(*slabs)


def _gather_ici_copies(src_refs, out_refs, send_sems, recv_sems):
    x, y, c = _place()
    me = 2 * x + y
    copies = []
    for g, (src_ref, out_ref) in enumerate(zip(src_refs, out_refs)):
        rh = src_ref.shape[0] // 2
        copies += [pltpu.make_async_remote_copy(
            src_ref=src_ref.at[pl.ds(c * rh, rh), :], dst_ref=out_ref.at[me, pl.ds(c * rh, rh), :],
            send_sem=send_sems.at[3 * g + j], recv_sem=recv_sems.at[3 * g + j], device_id=(px, py, c),
            device_id_type=MESH) for j, (px, py) in enumerate(_other_chips(x, y))]
    return copies


def _gather_start(slabs):
    n = len(slabs)

    def body(*refs):
        for cp in _gather_ici_copies(refs[:n], refs[n:2 * n], refs[2 * n], refs[2 * n + 1]):
            cp.start()
        refs[-1][...] = jnp.zeros_like(refs[-1])

    hbm = lambda a: pltpu.with_memory_space_constraint(a, pltpu.HBM)
    lands = [(N_SHARD,) + a.shape for a in slabs]
    out = pl.pallas_call(
        body, name="weights_gather_start",
        out_shape=(pltpu.SemaphoreType.DMA((3 * n,)), pltpu.SemaphoreType.DMA((3 * n,)),
                   *[pltpu.HBM(a.shape, a.dtype) for a in slabs],
                   *[pltpu.HBM(sh, a.dtype) for sh, a in zip(lands, slabs)], jax.ShapeDtypeStruct((8, LANES), F32)),
        in_specs=[HBM_SPEC] * (2 * n),
        out_specs=(SEM_SPEC, SEM_SPEC, *[HBM_SPEC] * (2 * n), pl.BlockSpec(memory_space=pltpu.VMEM)),
        input_output_aliases={i: 2 + i for i in range(2 * n)},
        compiler_params=pltpu.CompilerParams(has_side_effects=pltpu.SideEffectType.DATAFLOW_SIDE_EFFECTING),
    )(*[hbm(a) for a in slabs], *[hbm(lax.empty(sh, a.dtype)) for sh, a in zip(lands, slabs)])
    return out[0], out[1], list(out[2:2 + n]), list(out[2 + n:2 + 2 * n]), out[-1]


def _gather_wait(send_sems, recv_sems, slabs, landed, after):
    n = len(slabs)

    def body(*refs):
        for cp in _gather_ici_copies(refs[:n], refs[n:2 * n], refs[2 * n], refs[2 * n + 1]):
            cp.wait_send()
            cp.wait_recv()

    out = pl.pallas_call(
        body, name="weights_gather_wait",
        out_shape=[pltpu.HBM(a.shape, a.dtype) for a in slabs + landed],
        in_specs=[HBM_SPEC] * (2 * n) + [SEM_SPEC, SEM_SPEC, pl.BlockSpec(memory_space=pl.ANY)],
        out_specs=[HBM_SPEC] * (2 * n), input_output_aliases={i: i for i in range(2 * n)},
        compiler_params=pltpu.CompilerParams(has_side_effects=pltpu.SideEffectType.DATAFLOW_SIDE_EFFECTING),
    )(*slabs, *landed, send_sems, recv_sems, after)
    return list(out[n:])


def _gather_forward(gathered):
    n = len(gathered)

    def body(*refs):
        in_refs, out_refs, send_sems, recv_sems = refs[:n], refs[n:2 * n], refs[2 * n], refs[2 * n + 1]
        x, y, c = _place()

        def copy(g, j, chip, half):
            rh = in_refs[g].shape[1] // 2
            return pltpu.make_async_remote_copy(
                src_ref=in_refs[g].at[chip, pl.ds(half * rh, rh), :], dst_ref=out_refs[g].at[chip, pl.ds(half * rh, rh), :],
                send_sem=send_sems.at[3 * g + j], recv_sem=recv_sems.at[3 * g + j], device_id=(x, y, 1 - c),
                device_id_type=MESH)

        chips = [2 * px + py for px, py in _other_chips(x, y)]
        for g in range(n):
            for j, chip in enumerate(chips):
                copy(g, j, chip, c).start()
        for g in range(n):
            for j, chip in enumerate(chips):
                copy(g, j, chip, c).wait_send()
                copy(g, j, chip, 1 - c).wait_recv()

    return pl.pallas_call(
        body, name="weights_gather_forward", in_specs=[HBM_SPEC] * n, out_specs=[HBM_SPEC] * n,
        out_shape=[jax.ShapeDtypeStruct(a.shape, a.dtype) for a in gathered],
        input_output_aliases={g: g for g in range(n)},
        scratch_shapes=[pltpu.SemaphoreType.DMA((3 * n,)), pltpu.SemaphoreType.DMA((3 * n,))],
    )(*gathered)


def _chip_index():
    return jnp.reshape(2 * lax.axis_index("x") + lax.axis_index("y"), (1,)).astype(jnp.int32)


def _place_own_shard(name, gathered, slab):
    rows, width = slab.shape
    tile = _div_tile(rows, 256, 16)

    def body(me_ref, s_ref, g_ref, o_ref):
        o_ref[0] = s_ref[...]

    return pl.pallas_call(
        body, name=name,
        grid_spec=pltpu.PrefetchScalarGridSpec(
            num_scalar_prefetch=1, grid=(rows // tile,),
            in_specs=[pl.BlockSpec((tile, width), lambda i, me: (i, 0)), pl.BlockSpec(memory_space=pl.ANY)],
            out_specs=pl.BlockSpec((1, tile, width), lambda i, me: (me[0], i, 0))),
        out_shape=jax.ShapeDtypeStruct(gathered.shape, gathered.dtype), input_output_aliases={2: 0},
        compiler_params=_params("parallel"),
    )(_chip_index(), slab, gathered)


def _sibling_exchange(sent, tag):
    n = len(sent)

    def body(*refs):
        g_refs, out_refs, send_sems, recv_sems = refs[:n], refs[n:2 * n], refs[2 * n], refs[2 * n + 1]
        x, y, c = _place()
        copies = [pltpu.make_async_remote_copy(
            src_ref=g_ref.at[s], dst_ref=out_ref.at[s], send_sem=send_sems.at[N_SHARD * g + s],
            recv_sem=recv_sems.at[N_SHARD * g + s], device_id=(x, y, 1 - c), device_id_type=MESH)
            for g, (g_ref, out_ref) in enumerate(zip(g_refs, out_refs)) for s in range(N_SHARD)]
        for cp in copies:
            cp.start()
        for cp in copies:
            cp.wait()

    return pl.pallas_call(
        body, name="grad_sibling_exchange_" + tag, in_specs=[HBM_SPEC] * n, out_specs=[HBM_SPEC] * n,
        out_shape=[jax.ShapeDtypeStruct(g.shape, g.dtype) for g in sent],
        scratch_shapes=[pltpu.SemaphoreType.DMA((N_SHARD * n,)), pltpu.SemaphoreType.DMA((N_SHARD * n,))],
    )(*sent)


def _add_sibling(name, kept, got):
    _, rh, width = kept.shape
    tile = _div_tile(rh, 256, 16)

    def body(a_ref, b_ref, o_ref):
        o_ref[...] = (a_ref[...] + b_ref[...].astype(F32)).astype(o_ref.dtype)

    spec = pl.BlockSpec((1, tile, width), lambda s, i: (s, i, 0))
    return pl.pallas_call(
        body, name=name, grid=(N_SHARD, rh // tile), in_specs=[spec, spec], out_specs=spec,
        out_shape=jax.ShapeDtypeStruct(kept.shape, BF16), compiler_params=_params("parallel", "parallel"),
    )(kept, got)


def _exchange_copies(p_refs, land_refs, send_sems, recv_sems):
    x, y, c = _place()
    me = 2 * x + y
    return [pltpu.make_async_remote_copy(
        src_ref=p_ref.at[2 * px + py], dst_ref=land_ref.at[me], send_sem=send_sems.at[3 * g + j],
        recv_sem=recv_sems.at[3 * g + j], device_id=(px, py, c), device_id_type=MESH)
        for g, (p_ref, land_ref) in enumerate(zip(p_refs, land_refs)) for j, (px, py) in enumerate(_other_chips(x, y))]


def _chip_exchange_start(ps, tag):
    n = len(ps)

    def body(*refs):
        for cp in _exchange_copies(refs[:n], refs[n:2 * n], refs[2 * n], refs[2 * n + 1]):
            cp.start()
        refs[-1][...] = jnp.zeros_like(refs[-1])

    hbm = lambda a: pltpu.with_memory_space_constraint(a, pltpu.HBM)
    out = pl.pallas_call(
        body, name="grad_chip_exchange_start_" + tag,
        out_shape=(pltpu.SemaphoreType.DMA((3 * n,)), pltpu.SemaphoreType.DMA((3 * n,)),
                   *[pltpu.HBM(a.shape, a.dtype) for a in ps], *[pltpu.HBM(a.shape, a.dtype) for a in ps],
                   jax.ShapeDtypeStruct((8, LANES), F32)),
        in_specs=[HBM_SPEC] * (2 * n),
        out_specs=(SEM_SPEC, SEM_SPEC, *[HBM_SPEC] * (2 * n), pl.BlockSpec(memory_space=pltpu.VMEM)),
        input_output_aliases={i: 2 + i for i in range(2 * n)},
        compiler_params=pltpu.CompilerParams(has_side_effects=pltpu.SideEffectType.DATAFLOW_SIDE_EFFECTING),
    )(*[hbm(a) for a in ps], *[hbm(lax.empty(a.shape, a.dtype)) for a in ps])
    return out[0], out[1], list(out[2:2 + n]), list(out[2 + n:2 + 2 * n]), out[-1]


def _chip_exchange_wait(send_sems, recv_sems, ps, landed, after, tag):
    n = len(ps)

    def body(*refs):
        for cp in _exchange_copies(refs[:n], refs[n:2 * n], refs[2 * n], refs[2 * n + 1]):
            cp.wait_send()
            cp.wait_recv()

    out = pl.pallas_call(
        body, name="grad_chip_exchange_wait_" + tag,
        out_shape=[pltpu.HBM(a.shape, a.dtype) for a in ps + landed],
        in_specs=[HBM_SPEC] * (2 * n) + [SEM_SPEC, SEM_SPEC, pl.BlockSpec(memory_space=pl.ANY)],
        out_specs=[HBM_SPEC] * (2 * n), input_output_aliases={i: i for i in range(2 * n)},
        compiler_params=pltpu.CompilerParams(has_side_effects=pltpu.SideEffectType.DATAFLOW_SIDE_EFFECTING),
    )(*ps, *landed, send_sems, recv_sems, after)
    return list(out[:n]), list(out[n:])


def _sum_chips(name, p, got):
    _, rh, width = p.shape
    tile = _div_tile(rh, 256, 16)
    n_t = rh // tile
    place = jnp.stack([2 * lax.axis_index("x") + lax.axis_index("y"), lax.axis_index("c")]).astype(jnp.int32)

    def body(pl_ref, own_ref, r0, r1, r2, r3, o_ref):
        me = pl_ref[0]
        own = own_ref[0].astype(F32)
        t = [jnp.where(me == s, own, r[0].astype(F32)) for s, r in enumerate((r0, r1, r2, r3))]
        o_ref[...] = ((t[0] + t[1]) + t[2]) + t[3]

    def slot(s):
        return pl.BlockSpec((1, tile, width), lambda i, pc: (jnp.where(pc[0] == s, (s + 1) % N_SHARD, s), i, 0))

    return pl.pallas_call(
        body, name=name,
        grid_spec=pltpu.PrefetchScalarGridSpec(
            num_scalar_prefetch=1, grid=(n_t,),
            in_specs=[pl.BlockSpec((1, tile, width), lambda i, pc: (pc[0], i, 0))] + [slot(s) for s in range(N_SHARD)],
            out_specs=pl.BlockSpec((tile, width), lambda i, pc: (pc[1] * n_t + i, 0))),
        out_shape=jax.ShapeDtypeStruct((2 * rh, width), F32), compiler_params=_params("parallel"),
    )(place, p, got, got, got, got)


def _join_halves(fulls, tag):
    n = len(fulls)

    def body(*refs):
        f_refs, out_refs, send_sems, recv_sems = refs[:n], refs[n:2 * n], refs[2 * n], refs[2 * n + 1]
        x, y, c = _place()

        def copy(g, half):
            rh = f_refs[g].shape[0] // 2
            return pltpu.make_async_remote_copy(
                src_ref=f_refs[g].at[pl.ds(half * rh, rh), :], dst_ref=out_refs[g].at[pl.ds(half * rh, rh), :],
                send_sem=send_sems.at[g], recv_sem=recv_sems.at[g], device_id=(x, y, 1 - c), device_id_type=MESH)

        for g in range(n):
            copy(g, c).start()
        for g in range(n):
            copy(g, c).wait_send()
            copy(g, 1 - c).wait_recv()

    return pl.pallas_call(
        body, name="grad_join_halves_" + tag, in_specs=[HBM_SPEC] * n, out_specs=[HBM_SPEC] * n,
        out_shape=[jax.ShapeDtypeStruct(f.shape, f.dtype) for f in fulls],
        input_output_aliases={g: g for g in range(n)},
        scratch_shapes=[pltpu.SemaphoreType.DMA((n,)), pltpu.SemaphoreType.DMA((n,))],
    )(*fulls)


def _reduce_scatter_start(gs, tag):
    c = lax.axis_index("c")
    gs = [g.reshape(N_SHARD, 2, g.shape[1] // 2, g.shape[2]) for g in gs]
    kept = [lax.dynamic_index_in_dim(g, c, axis=1, keepdims=False) for g in gs]
    got = _sibling_exchange([lax.dynamic_index_in_dim(g, 1 - c, axis=1, keepdims=False).astype(BF16) for g in gs], tag)
    chip_sums = [_add_sibling(f"grad_add_sibling_{tag}{i}", k, r) for i, (k, r) in enumerate(zip(kept, got))]
    *state, token = _chip_exchange_start(chip_sums, tag)
    return state, token


def _reduce_scatter_end(state, after, tag):
    chip_sums, landed = _chip_exchange_wait(*state, after, tag)
    return _join_halves([_sum_chips(f"grad_sum_chips_{tag}{i}", p, r)
                         for i, (p, r) in enumerate(zip(chip_sums, landed))], tag)


def _all_reduce_small(v):
    rows = v.shape[0]

    def body(v_ref, out_ref, gather, send_sems, recv_sems):
        x, y, c = _place()
        gather[4 * x + 2 * y + c] = v_ref[...]
        flips = [(dx, dy, dc) for dx in (0, 1) for dy in (0, 1) for dc in (0, 1)][1:]
        peers = [((x + dx) % 2, (y + dy) % 2, (c + dc) % 2) for dx, dy, dc in flips]
        copies = [pltpu.make_async_remote_copy(
            src_ref=v_ref, dst_ref=gather.at[4 * x + 2 * y + c], send_sem=send_sems.at[j], recv_sem=recv_sems.at[j],
            device_id=peer, device_id_type=MESH) for j, peer in enumerate(peers)]
        for cp in copies:
            cp.start()
        for j, (px, py, pc) in enumerate(peers):
            pltpu.make_async_remote_copy(
                src_ref=v_ref, dst_ref=gather.at[4 * px + 2 * py + pc], send_sem=send_sems.at[j],
                recv_sem=recv_sems.at[j], device_id=(px, py, pc), device_id_type=MESH).wait_recv()
        for cp in copies:
            cp.wait_send()
        acc = gather[0]
        for d in range(1, 8):
            acc = acc + gather[d]
        out_ref[...] = acc

    vm = pl.BlockSpec(memory_space=pltpu.VMEM)
    return pl.pallas_call(
        body, name="small_grads_all_reduce", in_specs=[vm], out_specs=vm,
        out_shape=jax.ShapeDtypeStruct(v.shape, F32),
        scratch_shapes=[pltpu.VMEM((8, rows, LANES), F32), pltpu.SemaphoreType.DMA((7,)), pltpu.SemaphoreType.DMA((7,))],
    )(v)


def _pad_lanes(v):
    v = v.reshape(1, -1)
    return jnp.pad(v, ((0, 0), (0, -v.shape[1] % LANES)))


def _pack_small(vs, rows):
    flat = jnp.concatenate([_pad_lanes(v) for v in vs], axis=1)
    return jnp.pad(flat, ((0, 0), (0, rows * LANES - flat.shape[1]))).reshape(rows, LANES)


def _unpack_small(packed, shapes):
    flat = packed.reshape(-1)
    out, off = [], 0
    for s in shapes:
        n = 1
        for d in s:
            n *= d
        out.append(flat[off:off + n].reshape(s))
        off += n + (-n % LANES)
    return out


BIG = ("w_in", "rw_w_lora_up", "rw_a_lora_up", "w_up_rwkv", "w_up_fox", "w_out", "ple_proj", "ple_gate_w")
ROW_SHARDED = ("w_out", "ple_gate_w")
FIRST_NEEDED = ("w_in", "rw_w_lora_up", "rw_a_lora_up")
SMALL = ("norm_g", "rw_shift_mu", "rw_w0", "rw_a0", "rw_k_k", "rw_k_a", "rw_r_k", "rw_ln_g", "rw_ln_b", "fox_b_f",
         "ple_norm_g", "final_norm_g")
WEIGHTS = ("norm_g", "w_in", "rw_shift_mu", "rw_w0", "rw_w_lora_up", "rw_a0", "rw_a_lora_up", "rw_k_k", "rw_k_a",
           "rw_r_k", "rw_ln_g", "rw_ln_b", "fox_b_f", "w_up_rwkv", "w_up_fox", "w_out", "ple_proj", "ple_gate_w",
           "ple_norm_g", "final_norm_g")


def kernel(x, p, norm_g, w_in, rw_shift_mu, rw_w0, rw_w_lora_up, rw_a0, rw_a_lora_up, rw_k_k, rw_k_a, rw_r_k, rw_ln_g, rw_ln_b, fox_b_f, w_up_rwkv, w_up_fox, w_out, ple_proj, ple_gate_w, ple_norm_g, final_norm_g, loss_target, m_norm_g, m_w_in, m_rw_shift_mu, m_rw_w0, m_rw_w_lora_up, m_rw_a0, m_rw_a_lora_up, m_rw_k_k, m_rw_k_a, m_rw_r_k, m_rw_ln_g, m_rw_ln_b, m_fox_b_f, m_w_up_rwkv, m_w_up_fox, m_w_out, m_ple_proj, m_ple_gate_w, m_ple_norm_g, m_final_norm_g, v_norm_g, v_w_in, v_rw_shift_mu, v_rw_w0, v_rw_w_lora_up, v_rw_a0, v_rw_a_lora_up, v_rw_k_k, v_rw_k_a, v_rw_r_k, v_rw_ln_g, v_rw_ln_b, v_fox_b_f, v_w_up_rwkv, v_w_up_fox, v_w_out, v_ple_proj, v_ple_gate_w, v_ple_norm_g, v_final_norm_g):
    wts = dict(norm_g=norm_g, w_in=w_in, rw_shift_mu=rw_shift_mu, rw_w0=rw_w0, rw_w_lora_up=rw_w_lora_up, rw_a0=rw_a0,
               rw_a_lora_up=rw_a_lora_up, rw_k_k=rw_k_k, rw_k_a=rw_k_a, rw_r_k=rw_r_k, rw_ln_g=rw_ln_g, rw_ln_b=rw_ln_b,
               fox_b_f=fox_b_f, w_up_rwkv=w_up_rwkv, w_up_fox=w_up_fox, w_out=w_out, ple_proj=ple_proj,
               ple_gate_w=ple_gate_w, ple_norm_g=ple_norm_g, final_norm_g=final_norm_g)
    mom = dict(norm_g=m_norm_g, w_in=m_w_in, rw_shift_mu=m_rw_shift_mu, rw_w0=m_rw_w0, rw_w_lora_up=m_rw_w_lora_up,
               rw_a0=m_rw_a0, rw_a_lora_up=m_rw_a_lora_up, rw_k_k=m_rw_k_k, rw_k_a=m_rw_k_a, rw_r_k=m_rw_r_k,
               rw_ln_g=m_rw_ln_g, rw_ln_b=m_rw_ln_b, fox_b_f=m_fox_b_f, w_up_rwkv=m_w_up_rwkv, w_up_fox=m_w_up_fox,
               w_out=m_w_out, ple_proj=m_ple_proj, ple_gate_w=m_ple_gate_w, ple_norm_g=m_ple_norm_g,
               final_norm_g=m_final_norm_g)
    vel = dict(norm_g=v_norm_g, w_in=v_w_in, rw_shift_mu=v_rw_shift_mu, rw_w0=v_rw_w0, rw_w_lora_up=v_rw_w_lora_up,
               rw_a0=v_rw_a0, rw_a_lora_up=v_rw_a_lora_up, rw_k_k=v_rw_k_k, rw_k_a=v_rw_k_a, rw_r_k=v_rw_r_k,
               rw_ln_g=v_rw_ln_g, rw_ln_b=v_rw_ln_b, fox_b_f=v_fox_b_f, w_up_rwkv=v_w_up_rwkv, w_up_fox=v_w_up_fox,
               w_out=v_w_out, ple_proj=v_ple_proj, ple_gate_w=v_ple_gate_w, ple_norm_g=v_ple_norm_g,
               final_norm_g=v_final_norm_g)

    t, d = x.shape[1], x.shape[2]
    cw = rw_w0.shape[1]
    lr = rw_w_lora_up.shape[1]
    fh = fox_b_f.shape[1]
    fw = fh * HEAD
    rw_cols = 4 * cw + 2 * lr
    fox_cols = 4 * fw + fh
    assert 2 * lr == LANES and cw % LANES == 0 and fw % LANES == 0 and fh <= LANES
    xs = x[0]
    ps = p[0, 0]
    tgt = loss_target[0]

    groups = {}
    for n in BIG:
        groups.setdefault(wts[n].shape[2], []).append(n)
    groups = list(groups.values())
    slabs16 = []
    for gi, names in enumerate(groups):
        slab = jnp.concatenate([wts[n][0] for n in names], axis=0)
        rows, width = slab.shape
        tile_c = _div_tile(rows, 256, 32)
        slabs16 += _rows(f"weights_to_bf16_{gi}", lambda i, a: (a,), rows, tile_c, [(slab, _row_spec(tile_c, width))],
                         [(width, BF16)])
    first = [gi for gi, names in enumerate(groups) if not set(names).isdisjoint(FIRST_NEEDED)]
    later = [gi for gi in range(len(groups)) if gi not in first]
    gathered = dict(zip(first, _all_gather_shards([slabs16[gi] for gi in first])))
    *gather_state, gather_token = _gather_start([slabs16[gi] for gi in later])
    full = {}

    def assemble(gis):
        for gi in gis:
            g = _place_own_shard(f"weights_place_own_{gi}", gathered[gi], slabs16[gi])
            off = 0
            for n in groups[gi]:
                r = wts[n].shape[1]
                part = g[:, off:off + r, :]
                full[n] = (part.reshape(N_SHARD * r, -1) if n in ROW_SHARDED
                           else jnp.concatenate([part[s] for s in range(N_SHARD)], axis=1))
                off += r

    assemble(first)
    w_rw = full["w_in"][:, :rw_cols]
    w_fox = jnp.pad(full["w_in"][:, rw_cols:rw_cols + fox_cols], ((0, 0), (0, LANES - fh)))
    w_gate = full["w_in"][:, rw_cols + fox_cols:]
    wup_pad = jnp.pad(full["rw_w_lora_up"], ((0, lr), (0, 0)))
    aup_pad = jnp.pad(full["rw_a_lora_up"], ((lr, 0), (0, 0)))
    b_pad = _pad_lanes(fox_b_f)
    r_k_row = rw_r_k.reshape(1, cw)
    gf_row = final_norm_g.reshape(1, d)

    tile = _tile(t, (256, 128, 64, 32, 16, 8))
    tile_s = _tile(t, (128, 64, 32, 16, 8))
    n_s = t // tile_s
    full2 = lambda a: (a, _full_spec(a.shape))

    (h,) = _rows("norm1", lambda i, a, g: (_rms(a, g),), t, tile, [(xs, _row_spec(tile, d)), full2(norm_g)], [(d, BF16)])
    z_rw = _matmul("proj_rw", h, w_rw, after=gather_token)
    z_fox = _matmul("proj_fox", h, w_fox, after=z_rw[:8, :LANES])
    z_gate = _matmul("proj_gate", h, w_gate, after=z_fox[:8, :LANES])
    gathered.update(zip(later, _gather_forward(_gather_wait(*gather_state, z_gate))))
    assemble(later)

    pre_consts = [full2(rw_shift_mu), full2(rw_w0), full2(rw_a0), full2(wup_pad), full2(aup_pad), full2(rw_k_k),
                  full2(rw_k_a)]

    def pre_fwd(i, z, prev8, *consts):
        return _rw_pre(z, _shifted(i, z, prev8), *consts, cw=cw)

    r_, w_, k_, v_, kk_, a_, g_ = _rows(
        "rwkv_pre", pre_fwd, t, tile_s,
        [(z_rw, _row_spec(tile_s, rw_cols)), (z_rw, _prev_rows_spec(tile_s, rw_cols))] + pre_consts, [(cw, F32)] * 7)
    y_scan, states, tinvs = _scan_fwd(r_, w_, k_, v_, kk_, a_)
    post_consts = [full2(rw_ln_g), full2(rw_ln_b), full2(r_k_row)]
    post_rows = lambda *arrs: [(a, _row_spec(tile_s, cw)) for a in arrs]
    (y_rw,) = _rows("rwkv_post", lambda i, *a: (_rw_post(*a, cw=cw),), t, tile_s,
                    post_rows(y_scan, r_, k_, v_, g_) + post_consts, [(cw, BF16)])

    c_fox = _fox_cumsum(z_fox, b_pad, fw=fw, fh=fh)
    tq = _fox_tile(t)
    n_pair_f = fw // LANES
    head_rows = lambda a: a.T.reshape(n_pair_f, 2, t // tq, tq).transpose(0, 2, 1, 3)
    head_cols = lambda a: a.transpose(0, 2, 1, 3).reshape(fh, t).T
    ct_fox = head_rows(c_fox[:, :fh])
    o_fox, lse_fox, y_fox = _fox_fwd(z_fox, c_fox, ct_fox, fw=fw)

    u_rw = _matmul("up_rwkv", y_rw, full["w_up_rwkv"])
    u_fox = _matmul("up_fox", y_fox, full["w_up_fox"])
    (merged,) = _rows("merge", lambda i, zg, a, b: (_merge(zg, a, b, d=d),), t, tile,
                      [(z_gate, _row_spec(tile, 2 * d)), (u_rw, _row_spec(tile, d)), (u_fox, _row_spec(tile, d))],
                      [(d, BF16)])
    x1 = _matmul("out_proj", merged, full["w_out"], add=xs)
    (n2,) = _rows("norm2", lambda i, a, g: (_rms(a, g),), t, tile, [(x1, _row_spec(tile, d)), full2(ple_norm_g)],
                  [(d, BF16)])
    gl = _matmul("ple_gate", n2, full["ple_gate_w"])
    ple = _matmul("ple_proj", ps, full["ple_proj"])

    def head_bwd(i, x1_t, ple_t, gl_t, gf, tg):
        loss, vjp = jax.vjp(lambda a, b, cc, g: _head_loss(a, b, cc, g, tg), x1_t, ple_t, gl_t, gf)
        dx1, dple, dgl, dgf = vjp(jnp.ones((1, 1), F32))
        return dx1, dple, dgl, jnp.broadcast_to(loss, (1, LANES)), dgf

    dx2, dple, dgl, loss_row, d_gf = _rows(
        "loss_head", head_bwd, t, tile_s,
        [(x1, _row_spec(tile_s, d)), (ple, _row_spec(tile_s, d)), (gl, _row_spec(tile_s, d)), full2(gf_row),
         (tgt, _row_spec(tile_s, d))],
        [(d, F32), (d, BF16), (d, BF16)], [(1, LANES), (1, d)])

    g_ple_proj = _matmul("d_ple_proj", ps, dple, ta=True)
    g_ple_gate = _matmul("d_ple_gate_w", n2, dgl, ta=True)
    dn2 = _matmul("d_n2", dgl, full["ple_gate_w"], tb=True)

    def norm_bwd(i, a, g, dh, res):
        _, vjp = jax.vjp(_rms, a, g)
        da, dg = vjp(dh)
        return res + da, dg

    dx1, d_g2 = _rows("norm2_bwd", norm_bwd, t, tile_s,
                      [(x1, _row_spec(tile_s, d)), full2(ple_norm_g), (dn2, _row_spec(tile_s, d)),
                       (dx2, _row_spec(tile_s, d))], [(d, F32)], [(1, d)])
    g_w_out = _matmul("d_w_out", merged, dx1, ta=True)
    dmerged = _matmul("d_merged", dx1, full["w_out"], tb=True)

    def merge_bwd(i, zg, a, b, dm):
        _, vjp = jax.vjp(functools.partial(_merge, d=d), zg, a, b)
        return vjp(dm)

    dz_gate, du_rw, du_fox = _rows(
        "merge_bwd", merge_bwd, t, tile_s,
        [(z_gate, _row_spec(tile_s, 2 * d)), (u_rw, _row_spec(tile_s, d)), (u_fox, _row_spec(tile_s, d)),
         (dmerged, _row_spec(tile_s, d))], [(2 * d, BF16), (d, BF16), (d, BF16)])
    g_up_rw = _matmul("d_w_up_rwkv", y_rw, du_rw, ta=True)
    g_up_fox = _matmul("d_w_up_fox", y_fox, du_fox, ta=True)

    def slab_grads(gis, g_full):
        def by_shard(g, row_sharded):
            if row_sharded:
                return g.reshape(N_SHARD, g.shape[0] // N_SHARD, g.shape[1])
            return jnp.stack(jnp.split(g, N_SHARD, axis=1))

        return [jnp.concatenate([by_shard(g_full[n], n in ROW_SHARDED) for n in groups[gi]], axis=1) for gi in gis]

    g_full = {"w_up_rwkv": g_up_rw, "w_up_fox": g_up_fox, "w_out": g_w_out, "ple_proj": g_ple_proj,
              "ple_gate_w": g_ple_gate}
    rs_late, token_late = _reduce_scatter_start(slab_grads(later, g_full), "late")
    dy_rw = _matmul("d_y_rwkv", du_rw, full["w_up_rwkv"], tb=True, after=token_late)
    dy_fox = _matmul("d_y_fox", du_fox, full["w_up_fox"], tb=True)

    def post_bwd(i, y, r, k, v, g, ln_g, ln_b, r_k, dy):
        _, vjp = jax.vjp(functools.partial(_rw_post, cw=cw), y, r, k, v, g, ln_g, ln_b, r_k)
        return vjp(dy)

    dys, dr1, dk1, dv1, dg1, d_ln_g, d_ln_b, d_r_k = _rows(
        "rwkv_post_bwd", post_bwd, t, tile_s,
        post_rows(y_scan, r_, k_, v_, g_) + post_consts + post_rows(dy_rw), [(cw, F32)] * 5, [(1, cw)] * 3)
    dr2, dw2, dk2, dv2, dkk2, da2 = _scan_bwd(r_, w_, k_, v_, kk_, a_, states, tinvs, dys)

    def pre_bwd(i, z, prev8, mu, w0, a0, wup, aup, k_k, k_a, dr_a, dr_b, dk_a, dk_b, dv_a, dv_b, dw, dkk, da, dg):
        zp = _shifted(i, z, prev8)
        _, vjp = jax.vjp(functools.partial(_rw_pre, cw=cw), z, zp, mu, w0, a0, wup, aup, k_k, k_a)
        dz, dzp, dmu, dw0, da0, dwup, daup, dk_k, dk_a = vjp((dr_a + dr_b, dw, dk_a + dk_b, dv_a + dv_b, dkk, da, dg))
        row = lax.broadcasted_iota(jnp.int32, dz.shape, 0)
        dz = dz + jnp.where(row < tile_s - 1, pltpu.roll(dzp, tile_s - 1, 0), 0.0)
        first = jnp.where(lax.broadcasted_iota(jnp.int32, (8, dz.shape[1]), 0) == 0, _row_of(dzp, 0), 0.0)
        return dz, first, dmu, dw0, da0, dwup, daup, dk_k, dk_a

    def pre_bwd_call():
        n_in = 2 + len(pre_consts) + 10
        ins = ([(z_rw, _row_spec(tile_s, rw_cols)), (z_rw, _prev_rows_spec(tile_s, rw_cols))] + pre_consts
               + post_rows(dr1, dr2, dk1, dk2, dv1, dv2, dw2, dkk2, da2, dg1))

        def body(*refs):
            i = pl.program_id(0)
            vals = pre_bwd(i, *[r[...] for r in refs[:n_in]])
            refs[n_in][...] = vals[0]
            refs[n_in + 1][...] = vals[1]
            for r, v in zip(refs[n_in + 2:], vals[2:]):
                @pl.when(i == 0)
                def _(r=r, v=v):
                    r[...] = v

                @pl.when(i > 0)
                def _(r=r, v=v):
                    r[...] += v

        acc_shapes = [(1, rw_cols), (1, cw), (1, cw), (LANES, cw), (LANES, cw), (1, cw), (1, cw)]
        return pl.pallas_call(
            body, name="rwkv_pre_bwd", grid=(n_s,), in_specs=[s for _, s in ins],
            out_specs=[_row_spec(tile_s, rw_cols), pl.BlockSpec((8, rw_cols), lambda i: (i, 0))]
            + [_full_spec(s) for s in acc_shapes],
            out_shape=[jax.ShapeDtypeStruct((t, rw_cols), F32), jax.ShapeDtypeStruct((8 * n_s, rw_cols), F32)]
            + [jax.ShapeDtypeStruct(s, F32) for s in acc_shapes],
            compiler_params=_params("arbitrary"),
        )(*[a for a, _ in ins])

    dz_main, dz_first, d_mu, d_w0, d_a0, d_wup, d_aup, d_k_k, d_k_a = pre_bwd_call()

    def add_next_row(i, dz, nxt8):
        row = lax.broadcasted_iota(jnp.int32, dz.shape, 0)
        carry = jnp.where(i < n_s - 1, _row_of(nxt8, 0), 0.0)
        return (dz + jnp.where(row == tile_s - 1, carry, 0.0),)

    (dz_rw,) = _rows("rwkv_shift_bwd", add_next_row, t, tile_s,
                     [(dz_main, _row_spec(tile_s, rw_cols)), (dz_first, _next_rows_spec(tile_s, rw_cols, n_s))],
                     [(rw_cols, BF16)])

    def fox_post_bwd(i, o, g, dy):
        _, vjp = jax.vjp(lambda oo, gg: oo * (gg * jax.nn.sigmoid(gg)), o, g)
        do, dg = vjp(dy)
        return do, _head_sum(do * o, _head_matrix(fw)), dg

    do_fox, dd_fox, dg_fox = _rows(
        "fox_post_bwd", fox_post_bwd, t, tile_s,
        [(o_fox, _row_spec(tile_s, fw)), (z_fox, _row_spec(tile_s, fw, 3)), (dy_fox, _row_spec(tile_s, fw))],
        [(fw, F32), (fw, F32), (fw, BF16)])
    dq_f, dk_f, dv_f, dcq, dck = _fox_bwd(z_fox, do_fox, c_fox, ct_fox, head_rows(lse_fox[:, ::HEAD]),
                                          head_rows(dd_fox[:, ::HEAD]), fw=fw)
    dc = head_cols(dcq) + dck.reshape(t, n_pair_f, LANES)[:, :, :2].reshape(t, fh)
    dfl, d_bf = _fox_cumsum_bwd(z_fox, b_pad, jnp.pad(dc, ((0, 0), (0, LANES - fh))), fw=fw, fh=fh)
    dz_fox = jnp.concatenate([dq_f, dk_f, dv_f, dg_fox, dfl], axis=1)

    g_w_rw = _matmul("d_w_in_rw", h, dz_rw, ta=True)
    g_w_fox = _matmul("d_w_in_fox", h, dz_fox, ta=True)
    g_w_gate = _matmul("d_w_in_gate", h, dz_gate, ta=True)

    reduced = dict(zip(later, _reduce_scatter_end(rs_late, dz_rw, "late")))
    g_full.update({"w_in": jnp.concatenate([g_w_rw, g_w_fox[:, :fox_cols], g_w_gate], axis=1),
                   "rw_w_lora_up": d_wup[:lr], "rw_a_lora_up": d_aup[lr:]})
    rs_state, token = _reduce_scatter_start(slab_grads(first, g_full), "first")
    dh = _matmul("d_h_rw", dz_rw, w_rw, tb=True, after=token)
    dh = _matmul("d_h_fox", dz_fox, w_fox, tb=True, add=dh)
    dh = _matmul("d_h_gate", dz_gate, w_gate, tb=True, add=dh)
    grad_x, d_g1 = _rows("norm1_bwd", norm_bwd, t, tile_s,
                         [(xs, _row_spec(tile_s, d)), full2(norm_g), (dh, _row_spec(tile_s, d)),
                          (dx1, _row_spec(tile_s, d))], [(d, F32)], [(1, d)])
    small_parts = dict(norm_g=d_g1, rw_shift_mu=d_mu, rw_w0=d_w0, rw_a0=d_a0, rw_k_k=d_k_k, rw_k_a=d_k_a, rw_r_k=d_r_k,
                       rw_ln_g=d_ln_g, rw_ln_b=d_ln_b, fox_b_f=d_bf[:, :fh], ple_norm_g=d_g2, final_norm_g=d_gf)
    n_small = sum(-(-wts[n].size // LANES) for n in SMALL)
    small_rows = -(-n_small // 8) * 8
    small_shapes = [wts[n].shape for n in SMALL]
    g_small = _all_reduce_small(_pack_small([small_parts[n] for n in SMALL], small_rows))
    reduced.update(zip(first, _reduce_scatter_end(rs_state, g_small, "first")))
    grads = {}
    for gi, names in enumerate(groups):
        g, off = reduced[gi], 0
        for n in names:
            r = wts[n].shape[1]
            grads[n] = g[off:off + r][None]
            off += r

    for n, g in zip(SMALL, _unpack_small(g_small, small_shapes)):
        grads[n] = g

    delta, new_m, new_v = {}, {}, {}
    for n in BIG:
        delta[n], new_m[n], new_v[n] = _adamw("adamw_" + n, wts[n], grads[n], mom[n], vel[n])
    packed = lambda src: _pack_small([src[n] for n in SMALL], small_rows)[None]
    for store, out in zip((delta, new_m, new_v), _adamw("adamw_small", packed(wts), g_small[None], packed(mom), packed(vel))):
        for n, a in zip(SMALL, _unpack_small(out[0], small_shapes)):
            store[n] = a

    loss = lax.psum(loss_row[0, 0], ("x", "y", "c"))
    return (loss, grad_x[None], *[grads[n] for n in WEIGHTS], *[delta[n] for n in WEIGHTS],
            *[new_m[n] for n in WEIGHTS], *[new_v[n] for n in WEIGHTS])
```

```python
import functools

import jax
import jax.numpy as jnp
from jax import lax
from jax.experimental import pallas as pl
from jax.experimental.pallas import tpu as pltpu

F32 = jnp.float32
BF16 = jnp.bfloat16
HIGHEST = lax.Precision.HIGHEST
SCAN_PREC = lax.Precision.HIGH
MESH = pl.DeviceIdType.MESH

LANES = 128
HEAD = 64
NORM_EPS = 1e-6
GN_EPS = 64e-5
ADAM_LR = 0.001
ADAM_B1 = 0.9
ADAM_B2 = 0.999
ADAM_EPS = 1e-08
ADAM_WD = 0.01
ADAM_STEP = 10
N_SHARD = 4
VMEM_LIMIT = 56 * 1024 * 1024
MATMUL_VMEM_BUDGET = 40 * 1024 * 1024
PAIRS_PER_STEP = 8
ELIM_BASE = 8


def _params(*sem):
    return pltpu.CompilerParams(dimension_semantics=sem, vmem_limit_bytes=VMEM_LIMIT)


def _tile(n, cands):
    for c in cands:
        if c <= n and n % c == 0:
            return c
    return n


def _div_tile(n, cap, mult):
    return max(c for c in range(mult, min(n, cap) + 1, mult) if n % c == 0)


_ROW_TILES = (512, 256, 128, 64, 32, 16, 8)


def _dot(a, b, prec=None):
    return lax.dot_general(a, b, (((1,), (0,)), ((), ())), precision=prec, preferred_element_type=F32)


def _dot_nt(a, b, prec=None):
    return lax.dot_general(a, b, (((1,), (1,)), ((), ())), precision=prec, preferred_element_type=F32)


def _dot_tn(a, b, prec=None):
    return lax.dot_general(a, b, (((0,), (0,)), ((), ())), precision=prec, preferred_element_type=F32)


@jax.custom_vjp
def _bdot(x, w):
    return _dot(x.astype(BF16), w.astype(BF16))


def _bdot_fwd(x, w):
    return _bdot(x, w), (x, w)


def _bdot_bwd(res, ct):
    x, w = res
    return _dot_nt(ct.astype(BF16), w.astype(BF16)), _dot_tn(x.astype(BF16), ct.astype(BF16))


_bdot.defvjp(_bdot_fwd, _bdot_bwd)


def _head_matrix(width):
    c = lax.broadcasted_iota(jnp.int32, (width, LANES), 0)
    h = lax.broadcasted_iota(jnp.int32, (width, LANES), 1)
    return (c // HEAD == h).astype(F32)


def _head_sum(x, e):
    return _dot_nt(_dot(x, e, SCAN_PREC), e, SCAN_PREC)


def _softplus(x):
    return jnp.maximum(x, 0.0) + jnp.log1p(jnp.exp(-jnp.abs(x)))


def _lane_pick(x, idx):
    lane = lax.broadcasted_iota(jnp.int32, x.shape, 1)
    return jnp.sum(jnp.where(lane == idx, x, 0.0), axis=1, keepdims=True)


def _matmul_tiles(m, n, k, a_bytes, b_bytes, out_bytes, has_add):
    best = None
    for tk in [c for c in (k, 2048, 1408, 1024, 768, 640, 512, 384, 256, 128, 64, 32, 16) if c <= k and k % c == 0]:
        for tm in [c for c in (1024, 512, 256, 128) if m % c == 0] or [m]:
            for tn in [c for c in (1408, 1024, 768, 640, 512, 384, 256, 128) if n % c == 0] or [n]:
                nk = k // tk
                tile_out = tm * tn
                vmem = 2 * (tm * tk * a_bytes + tk * tn * b_bytes + tile_out * (out_bytes + 4 * has_add))
                vmem += tile_out * 4 * (2 if nk > 1 else 1)
                if vmem > MATMUL_VMEM_BUDGET:
                    continue
                traffic = m * k * a_bytes * (n // tn) + k * n * b_bytes * (m // tm)
                key = (nk > 1, traffic, nk)
                if best is None or key < best[0]:
                    best = (key, (tm, tn, tk))
    return best[1]


def _matmul(name, a, b, *, ta=False, tb=False, add=None, out_dtype=F32, after=None):
    m, k = (a.shape[1], a.shape[0]) if ta else a.shape
    n = b.shape[0] if tb else b.shape[1]
    tm, tn, tk = _matmul_tiles(m, n, k, a.dtype.itemsize, b.dtype.itemsize, jnp.dtype(out_dtype).itemsize,
                               add is not None)
    nk = k // tk
    dims = (((0 if ta else 1,), (1 if tb else 0,)), ((), ()))

    def body(*refs):
        a_ref, b_ref = refs[0], refs[1]
        o_ref, acc_ref = (refs[-1], None) if nk == 1 else refs[-2:]
        kk = pl.program_id(2)
        part = lax.dot_general(a_ref[...].astype(BF16), b_ref[...].astype(BF16), dims, preferred_element_type=F32)

        def finish(r):
            if add is not None:
                r = r + refs[2][...].astype(F32)
            o_ref[...] = r.astype(o_ref.dtype)

        if nk == 1:
            finish(part)
            return

        @pl.when(kk == 0)
        def _():
            acc_ref[...] = part

        @pl.when((kk > 0) & (kk < nk - 1))
        def _():
            acc_ref[...] += part

        @pl.when(kk == nk - 1)
        def _():
            finish(acc_ref[...] + part)

    a_spec = pl.BlockSpec((tk, tm), lambda i, j, kk: (kk, i)) if ta else pl.BlockSpec((tm, tk), lambda i, j, kk: (i, kk))
    b_spec = pl.BlockSpec((tn, tk), lambda i, j, kk: (j, kk)) if tb else pl.BlockSpec((tk, tn), lambda i, j, kk: (kk, j))
    o_spec = pl.BlockSpec((tm, tn), lambda i, j, kk: (i, j))
    ins, specs = [a, b], [a_spec, b_spec]
    if add is not None:
        ins.append(add)
        specs.append(o_spec)
    if after is not None:
        ins.append(after)
        specs.append(pl.BlockSpec(after.shape, lambda i, j, kk: (0,) * after.ndim))
    return pl.pallas_call(
        body, name=name, grid=(m // tm, n // tn, nk), in_specs=specs, out_specs=o_spec,
        out_shape=jax.ShapeDtypeStruct((m, n), out_dtype),
        scratch_shapes=[pltpu.VMEM((tm, tn), F32)] if nk > 1 else [],
        compiler_params=_params("parallel", "parallel", "arbitrary"),
    )(*ins)


def _rows(name, fn, n_rows, tile, ins, outs, accs=()):
    n_in, n_out = len(ins), len(outs)

    def body(*refs):
        i = pl.program_id(0)
        vals = fn(i, *[r[...] for r in refs[:n_in]])
        for r, v in zip(refs[n_in:n_in + n_out], vals[:n_out]):
            r[...] = v.astype(r.dtype)
        for r, v in zip(refs[n_in + n_out:], vals[n_out:]):
            @pl.when(i == 0)
            def _(r=r, v=v):
                r[...] = v

            @pl.when(i > 0)
            def _(r=r, v=v):
                r[...] += v

    out_specs = [pl.BlockSpec((tile, w), lambda i: (i, 0)) for w, _ in outs]
    out_specs += [pl.BlockSpec(s, lambda i: (0, 0)) for s in accs]
    out_shape = [jax.ShapeDtypeStruct((n_rows, w), d) for w, d in outs]
    out_shape += [jax.ShapeDtypeStruct(s, F32) for s in accs]
    return pl.pallas_call(
        body, name=name, grid=(n_rows // tile,), in_specs=[s for _, s in ins], out_specs=out_specs,
        out_shape=out_shape, compiler_params=_params("arbitrary"),
    )(*[a for a, _ in ins])


def _row_spec(tile, width, col=0):
    return pl.BlockSpec((tile, width), lambda i: (i, col))


def _full_spec(shape):
    return pl.BlockSpec(shape, lambda i: (0,) * len(shape))


def _prev_rows_spec(tile, width):
    return pl.BlockSpec((8, width), lambda i: (jnp.maximum(i * (tile // 8) - 1, 0), 0))


def _next_rows_spec(tile, width, n_tiles):
    return pl.BlockSpec((8, width), lambda i: (jnp.minimum(i + 1, n_tiles - 1), 0))


def _row_of(x8, idx):
    r = lax.broadcasted_iota(jnp.int32, x8.shape, 0)
    return jnp.sum(jnp.where(r == idx, x8, 0.0), axis=0, keepdims=True)


def _rms(x, g):
    return x * lax.rsqrt(jnp.mean(x * x, axis=-1, keepdims=True) + NORM_EPS) * g


def _shifted(i, z, prev8):
    first = jnp.where(i > 0, _row_of(prev8, 7), 0.0)
    row = lax.broadcasted_iota(jnp.int32, z.shape, 0)
    return jnp.where(row == 0, first, pltpu.roll(z, 1, 0))


def _rw_pre(z, zp, mu, w0, a0, wup, aup, k_k, k_a, *, cw):
    zs = z + (zp - z) * mu
    r, k, v, g = (zs[:, j * cw:(j + 1) * cw] for j in range(4))
    lo = zs[:, 4 * cw:4 * cw + LANES]
    w_raw = w0 + _bdot(jnp.tanh(lo), wup)
    decay = jnp.exp(-jnp.exp(-_softplus(-w_raw) - 0.5))
    a = jax.nn.sigmoid(a0 + _bdot(lo, aup))
    e = _head_matrix(cw)
    kk = k * k_k
    kk = kk / jnp.maximum(jnp.sqrt(_head_sum(kk * kk, e)), 1e-12)
    k_mod = k * (1.0 + (a - 1.0) * k_a)
    return r, decay, k_mod, v, kk, a, g


def _rw_post(y, r, k_mod, v, g, ln_g, ln_b, r_k, *, cw):
    e = _head_matrix(cw)
    mu = _head_sum(y, e) * (1.0 / HEAD)
    d = y - mu
    var = _head_sum(d * d, e) * (1.0 / HEAD)
    yn = d * lax.rsqrt(var + GN_EPS) * ln_g + ln_b
    bonus = _head_sum(r * k_mod * r_k, e) * v
    return (yn + bonus) * (g * jax.nn.sigmoid(g))


def _merge(zg, u_rw, u_fox, *, d):
    return jax.nn.sigmoid(zg[:, :d]) * u_rw + jax.nn.sigmoid(zg[:, d:]) * u_fox


def _head_loss(x1, ple, gl, gf, tgt):
    x2 = x1 + ple * jax.nn.sigmoid(gl)
    err = _rms(x2, gf) - tgt
    return 0.5 * jnp.sum(jnp.mean(err * err, axis=-1, keepdims=True), axis=0, keepdims=True)


def _batched(a, b, ca, cb):
    return lax.dot_general(a, b, (((ca,), (cb,)), ((0,), (0,))), precision=SCAN_PREC, preferred_element_type=F32)


def _eliminate_block(lo, off, m):
    n, c, _ = lo.shape
    ci = lax.broadcasted_iota(jnp.int32, (n, m, c), 2)
    if m <= ELIM_BASE:
        ri = lax.broadcasted_iota(jnp.int32, (n, m, c), 1)
        rows = jnp.where((ci >= off) & (ci < off + m), lo[:, off:off + m, :], 0.0)
        x = (ri + off == ci).astype(F32)
        for s in range(m - 1):
            col = jnp.sum(jnp.where(ci == off + s, rows, 0.0), axis=2, keepdims=True)
            x = x - col * x[:, s:s + 1, :]
        return x
    h = m // 2
    xa = _eliminate_block(lo, off, h)
    xd = _eliminate_block(lo, off + h, h)
    ci_h = lax.broadcasted_iota(jnp.int32, (n, h, c), 2)
    b = jnp.where((ci_h >= off) & (ci_h < off + h), lo[:, off + h:off + m, :], 0.0)

    def at_rows(x, start):
        parts = ([jnp.zeros((n, start, c), F32)] if start else []) + [x]
        rest = c - start - x.shape[1]
        return jnp.concatenate(parts + ([jnp.zeros((n, rest, c), F32)] if rest else []), axis=1)

    low = xd - _batched(_batched(xd, at_rows(b, off + h), 2, 1), at_rows(xa, off), 2, 1)
    return jnp.concatenate([xa, low], axis=1)


def _eliminate(lo):
    return _eliminate_block(lo, 0, lo.shape[1])


@jax.custom_vjp
def _unit_lower_inverse(lo, known):
    return _eliminate(lo) if known is None else known


def _uli_fwd(lo, known):
    x = _unit_lower_inverse(lo, known)
    return x, (x, known)


def _uli_bwd(res, dx):
    x, known = res
    dlo = -_batched(_batched(x, dx, 1, 1), x, 2, 2)
    return dlo, (None if known is None else jnp.zeros_like(known))


_unit_lower_inverse.defvjp(_uli_fwd, _uli_bwd)


def _rwkv_chunk(s0, r, w, k, v, kk, a, *, c, tinv_known=None):
    pairs = range(len(s0))
    lane = lax.broadcasted_iota(jnp.int32, (1, LANES), 1)
    heads = (lane < HEAD, lane >= HEAD)
    ti = lax.broadcasted_iota(jnp.int32, (c, c), 0)
    si = lax.broadcasted_iota(jnp.int32, (c, c), 1)
    incl = si <= ti
    strict = si < ti
    tri = incl.astype(F32)
    logw = [jnp.log(w[p]) for p in pairs]
    cum = [_dot(tri, logw[p], HIGHEST) for p in pairs]
    cum_end = [jnp.sum(logw[p], axis=0, keepdims=True) for p in pairs]
    g_inv = [jnp.exp(-cum[p]) for p in pairs]
    to_end = [jnp.exp(cum_end[p] - cum[p]) for p in pairs]
    b = [kk[p] * a[p] for p in pairs]
    beta = [b[p] * g_inv[p] for p in pairs]
    kap = [kk[p] * jnp.exp(cum[p] - logw[p]) for p in pairs]
    kt = [k[p] * g_inv[p] for p in pairs]
    rt = [r[p] * jnp.exp(cum[p]) for p in pairs]
    lhs = [jnp.concatenate([jnp.where(m, x[p], 0.0) for x in (kap, rt) for m in heads], axis=0) for p in pairs]
    vs_beta = [_dot_nt(lhs[p], beta[p], None) for p in pairs]
    vs_kt = [_dot_nt(lhs[p], kt[p], None) for p in pairs]
    strict2 = jnp.concatenate([strict, strict], axis=0)
    incl2 = jnp.concatenate([incl, incl], axis=0)
    lo = [jnp.where(strict2, vs_beta[p][:2 * c], 0.0) for p in pairs]
    mm = [jnp.where(strict2, vs_kt[p][:2 * c], 0.0) for p in pairs]
    arb = [jnp.where(incl2, vs_beta[p][2 * c:], 0.0) for p in pairs]
    ark = [jnp.where(incl2, vs_kt[p][2 * c:], 0.0) for p in pairs]
    per_head = lambda xs: jnp.concatenate([xs[p][h * c:(h + 1) * c][None] for p in pairs for h in (0, 1)])
    tinv = _unit_lower_inverse(per_head(lo), None if tinv_known is None else per_head(tinv_known))
    tinv = [jnp.concatenate([tinv[2 * p], tinv[2 * p + 1]], axis=0) for p in pairs]
    both = lambda x: jnp.where(heads[0], x[:c], x[c:])
    vs_s = [_dot_nt(jnp.concatenate([kap[p], rt[p]], axis=0), s0[p], None) for p in pairs]
    rhs = [vs_s[p][:c] + both(_dot(mm[p], v[p], None)) for p in pairs]
    u = [-both(_dot(tinv[p], rhs[p], None)) for p in pairs]
    y = [vs_s[p][c:] + both(_dot(arb[p], u[p], None) + _dot(ark[p], v[p], None)) for p in pairs]
    rr = lax.broadcasted_iota(jnp.int32, (LANES, LANES), 0) < HEAD
    cc = lax.broadcasted_iota(jnp.int32, (LANES, LANES), 1) < HEAD
    ds = [_dot_tn(jnp.concatenate([u[p], v[p]], axis=0),
                  jnp.concatenate([b[p] * to_end[p], k[p] * to_end[p]], axis=0), None) for p in pairs]
    s1 = [s0[p] * jnp.exp(cum_end[p]) + jnp.where(rr == cc, ds[p], 0.0) for p in pairs]
    return tuple(y), tuple(s1), tuple(tinv)


def _scan_tiles(t, n_pair):
    return _tile(t, (32, 16, 8)), _tile(t, (256, 128, 64, 32)), _tile(n_pair, (PAIRS_PER_STEP, 4, 2, 1))


def _scan_fwd(r, w, k, v, kk, a):
    t, width = r.shape
    c, tb, npb = _scan_tiles(t, width // LANES)
    n_grp, n_blk, n_cb = width // (LANES * npb), t // tb, tb // c

    def body(r_ref, w_ref, k_ref, v_ref, kk_ref, a_ref, y_ref, st_ref, ti_ref, s_scr):
        @pl.when(pl.program_id(1) == 0)
        def _():
            s_scr[...] = jnp.zeros_like(s_scr)

        def chunk(j, carry):
            sl = pl.ds(pl.multiple_of(j * c, c), c)
            lanes = [pl.ds(q * LANES, LANES) for q in range(npb)]
            s0 = tuple(s_scr[q] for q in range(npb))
            cols = lambda ref: tuple(ref[sl, ln] for ln in lanes)
            y, s1, tinv = _rwkv_chunk(s0, cols(r_ref), cols(w_ref), cols(k_ref), cols(v_ref), cols(kk_ref), cols(a_ref),
                                      c=c)
            for q, ln in enumerate(lanes):
                st_ref[q, j] = s0[q]
                ti_ref[q, j] = tinv[q]
                y_ref[sl, ln] = y[q]
                s_scr[q] = s1[q]
            return carry

        lax.fori_loop(0, n_cb, chunk, 0)

    blk = pl.BlockSpec((tb, npb * LANES), lambda p, i: (i, p))
    return pl.pallas_call(
        body, name="rwkv_scan_fwd", grid=(n_grp, n_blk), in_specs=[blk] * 6,
        out_specs=[blk, pl.BlockSpec((npb, n_cb, LANES, LANES), lambda p, i: (p, i, 0, 0)),
                   pl.BlockSpec((npb, n_cb, 2 * c, c), lambda p, i: (p, i, 0, 0))],
        out_shape=[jax.ShapeDtypeStruct((t, width), F32),
                   jax.ShapeDtypeStruct((width // LANES, t // c, LANES, LANES), F32),
                   jax.ShapeDtypeStruct((width // LANES, t // c, 2 * c, c), F32)],
        scratch_shapes=[pltpu.VMEM((npb, LANES, LANES), F32)],
        compiler_params=_params("arbitrary", "arbitrary"),
    )(r, w, k, v, kk, a)


def _scan_bwd(r, w, k, v, kk, a, st, ti, dy):
    t, width = r.shape
    c, tb, npb = _scan_tiles(t, width // LANES)
    n_grp, n_blk, n_cb = width // (LANES * npb), t // tb, tb // c

    def body(r_ref, w_ref, k_ref, v_ref, kk_ref, a_ref, st_ref, ti_ref, dy_ref,
             dr_ref, dw_ref, dk_ref, dv_ref, dkk_ref, da_ref, ds_scr):
        @pl.when(pl.program_id(1) == 0)
        def _():
            ds_scr[...] = jnp.zeros_like(ds_scr)

        def chunk(jj, carry):
            j = n_cb - 1 - jj
            sl = pl.ds(pl.multiple_of(j * c, c), c)
            lanes = [pl.ds(q * LANES, LANES) for q in range(npb)]
            cols = lambda ref: tuple(ref[sl, ln] for ln in lanes)
            args = (tuple(st_ref[q, j] for q in range(npb)), cols(r_ref), cols(w_ref), cols(k_ref), cols(v_ref),
                    cols(kk_ref), cols(a_ref))
            known = tuple(ti_ref[q, j] for q in range(npb))
            _, vjp = jax.vjp(lambda *xs: _rwkv_chunk(*xs, c=c, tinv_known=known)[:2], *args)
            grads = vjp((cols(dy_ref), tuple(ds_scr[q] for q in range(npb))))
            for q, ln in enumerate(lanes):
                ds_scr[q] = grads[0][q]
                for ref, g in zip((dr_ref, dw_ref, dk_ref, dv_ref, dkk_ref, da_ref), grads[1:]):
                    ref[sl, ln] = g[q]
            return carry

        lax.fori_loop(0, n_cb, chunk, 0)

    blk = pl.BlockSpec((tb, npb * LANES), lambda p, i: (n_blk - 1 - i, p))
    stb = pl.BlockSpec((npb, n_cb, LANES, LANES), lambda p, i: (p, n_blk - 1 - i, 0, 0))
    tib = pl.BlockSpec((npb, n_cb, 2 * c, c), lambda p, i: (p, n_blk - 1 - i, 0, 0))
    return pl.pallas_call(
        body, name="rwkv_scan_bwd", grid=(n_grp, n_blk), in_specs=[blk] * 6 + [stb, tib, blk], out_specs=[blk] * 6,
        out_shape=[jax.ShapeDtypeStruct((t, width), F32)] * 6,
        scratch_shapes=[pltpu.VMEM((npb, LANES, LANES), F32)],
        compiler_params=_params("arbitrary", "arbitrary"),
    )(r, w, k, v, kk, a, st, ti, dy)


NEG = -1e30


def _fox_cumsum(zf, b_pad, *, fw, fh):
    t = zf.shape[0]
    tile = _tile(t, (256, 128, 64, 32, 16, 8))

    def body(fl_ref, b_ref, c_ref, carry):
        @pl.when(pl.program_id(0) == 0)
        def _():
            carry[...] = jnp.zeros_like(carry)

        lane = lax.broadcasted_iota(jnp.int32, (tile, LANES), 1)
        logf = jnp.where(lane < fh, -_softplus(-(fl_ref[...] + b_ref[...])), 0.0)
        ri = lax.broadcasted_iota(jnp.int32, (tile, tile), 0)
        ci = lax.broadcasted_iota(jnp.int32, (tile, tile), 1)
        c_ref[...] = carry[...] + _dot((ci <= ri).astype(F32), logf, HIGHEST)
        carry[...] += jnp.sum(logf, axis=0, keepdims=True)

    return pl.pallas_call(
        body, name="fox_cumsum", grid=(t // tile,),
        in_specs=[_row_spec(tile, LANES, 4 * fw // LANES), _full_spec((1, LANES))],
        out_specs=_row_spec(tile, LANES), out_shape=jax.ShapeDtypeStruct((t, LANES), F32),
        scratch_shapes=[pltpu.VMEM((1, LANES), F32)], compiler_params=_params("arbitrary"),
    )(zf, b_pad)


def _fox_cumsum_bwd(zf, b_pad, dc, *, fw, fh):
    t = zf.shape[0]
    tile = _tile(t, (256, 128, 64, 32, 16, 8))
    n = t // tile

    def body(fl_ref, b_ref, dc_ref, dfl_ref, db_ref, carry):
        i = pl.program_id(0)

        @pl.when(i == 0)
        def _():
            carry[...] = jnp.zeros_like(carry)
            db_ref[...] = jnp.zeros_like(db_ref)

        lane = lax.broadcasted_iota(jnp.int32, (tile, LANES), 1)
        dc_t = dc_ref[...]
        ri = lax.broadcasted_iota(jnp.int32, (tile, tile), 0)
        ci = lax.broadcasted_iota(jnp.int32, (tile, tile), 1)
        dlogf = carry[...] + _dot((ci >= ri).astype(F32), dc_t, HIGHEST)
        carry[...] += jnp.sum(dc_t, axis=0, keepdims=True)
        dfl = jnp.where(lane < fh, dlogf * jax.nn.sigmoid(-(fl_ref[...] + b_ref[...])), 0.0)
        dfl_ref[...] = dfl.astype(dfl_ref.dtype)
        db_ref[...] += jnp.sum(dfl, axis=0, keepdims=True)

    rev = lambda col: pl.BlockSpec((tile, LANES), lambda i: (n - 1 - i, col))
    return pl.pallas_call(
        body, name="fox_cumsum_bwd", grid=(n,),
        in_specs=[rev(4 * fw // LANES), _full_spec((1, LANES)), rev(0)],
        out_specs=[rev(0), _full_spec((1, LANES))],
        out_shape=[jax.ShapeDtypeStruct((t, LANES), BF16), jax.ShapeDtypeStruct((1, LANES), F32)],
        scratch_shapes=[pltpu.VMEM((1, LANES), F32)], compiler_params=_params("arbitrary"),
    )(zf, b_pad, dc)


def _fox_tile(t):
    return _tile(t, (512, 256, 128))


def _fox_fwd(zf, c, ct, *, fw):
    t = zf.shape[0]
    tq = _fox_tile(t)
    th = tq // 2
    n_pair, n_q = fw // LANES, t // tq
    scale = HEAD ** -0.5
    chains = [(h, qq) for h in (0, 1) for qq in (0, 1)]

    def body(q_ref, k_ref, v_ref, g_ref, c_ref, ct_ref, o_ref, lse_ref, y_ref):
        hp, i = pl.program_id(0), pl.program_id(1)
        lane = lax.broadcasted_iota(jnp.int32, (1, LANES), 1)
        in_head = (lane < HEAD, lane >= HEAD)
        rows = [pl.ds(qq * th, th) for qq in (0, 1)]
        qh = [jnp.where(in_head[h], q_ref[rows[qq], :] * scale, 0.0).astype(BF16) for h, qq in chains]
        cq = [_lane_pick(c_ref[rows[qq], :], 2 * hp + h) for h, qq in chains]
        qidx = lax.broadcasted_iota(jnp.int32, (th, tq), 0)
        kidx = lax.broadcasted_iota(jnp.int32, (th, tq), 1)

        def kv_step(j, carry, diagonal):
            m, l, acc = carry
            ks = pl.ds(pl.multiple_of(j * tq, tq), tq)
            kb = k_ref[ks, :].astype(BF16)
            vb = v_ref[ks, :]
            vh = [jnp.where(in_head[h], vb, 0.0).astype(BF16) for h in (0, 1)]
            ck = [ct_ref[0, j, pl.ds(h, 1), :] for h in (0, 1)]
            s = [_dot_nt(qh[n], kb) + cq[n] - ck[h] for n, (h, qq) in enumerate(chains)]
            if diagonal:
                s = [jnp.where(qq * th + qidx >= kidx, s[n], NEG) for n, (h, qq) in enumerate(chains)]
            m_new = [jnp.maximum(m[n], jnp.max(s[n], axis=1, keepdims=True)) for n in range(4)]
            p = [jnp.exp(s[n] - m_new[n]) for n in range(4)]
            alpha = [jnp.exp(m[n] - m_new[n]) for n in range(4)]
            l = [l[n] * alpha[n] + jnp.sum(p[n], axis=1, keepdims=True) for n in range(4)]
            pv = [_dot(p[n].astype(BF16), vh[h]) for n, (h, qq) in enumerate(chains)]
            acc = [acc[qq] * jnp.where(in_head[0], alpha[qq], alpha[2 + qq]) + pv[qq] + pv[2 + qq] for qq in (0, 1)]
            return tuple(m_new), tuple(l), tuple(acc)

        init = (tuple(jnp.full((th, 1), NEG, F32) for _ in chains), tuple(jnp.zeros((th, 1), F32) for _ in chains),
                tuple(jnp.zeros((th, LANES), F32) for _ in (0, 1)))
        carry = lax.fori_loop(0, i, functools.partial(kv_step, diagonal=False), init)
        m, l, acc = kv_step(i, carry, True)
        for qq in (0, 1):
            o = acc[qq] / jnp.where(in_head[0], l[qq], l[2 + qq])
            g = g_ref[rows[qq], :]
            o_ref[rows[qq], :] = o
            lse_ref[rows[qq], :] = jnp.where(in_head[0], m[qq] + jnp.log(l[qq]), m[2 + qq] + jnp.log(l[2 + qq]))
            y_ref[rows[qq], :] = (o * (g * jax.nn.sigmoid(g))).astype(y_ref.dtype)

    npw = fw // LANES
    blk = lambda col0: pl.BlockSpec((tq, LANES), lambda hp, i: (i, col0 + hp))
    res = lambda col0: pl.BlockSpec((t, LANES), lambda hp, i: (0, col0 + hp))
    out_blk = pl.BlockSpec((tq, LANES), lambda hp, i: (i, hp))
    return pl.pallas_call(
        body, name="fox_attn_fwd", grid=(n_pair, n_q),
        in_specs=[blk(0), res(npw), res(2 * npw), blk(3 * npw),
                  pl.BlockSpec((tq, LANES), lambda hp, i: (i, 0)),
                  pl.BlockSpec((1, n_q, 2, tq), lambda hp, i: (hp, 0, 0, 0))],
        out_specs=[out_blk, out_blk, out_blk],
        out_shape=[jax.ShapeDtypeStruct((t, fw), F32), jax.ShapeDtypeStruct((t, fw), F32),
                   jax.ShapeDtypeStruct((t, fw), BF16)],
        compiler_params=_params("arbitrary", "arbitrary"),
    )(zf, zf, zf, zf, c, ct)


def _fox_bwd(zf, do, c, ct, lse_r, dd_r, *, fw):
    t = zf.shape[0]
    tq = _fox_tile(t)
    n_pair, n_q = fw // LANES, t // tq
    scale = HEAD ** -0.5

    def body(q_ref, k_ref, v_ref, do_ref, c_ref, ct_ref, lse_ref, dd_ref,
             dq_ref, dk_ref, dv_ref, dcq_ref, dck_ref, dq_acc, dcq_acc):
        hp, j = pl.program_id(0), pl.program_id(1)

        @pl.when(j == 0)
        def _():
            dq_acc[...] = jnp.zeros_like(dq_acc)
            dcq_acc[...] = jnp.zeros_like(dcq_acc)

        lane = lax.broadcasted_iota(jnp.int32, (1, LANES), 1)
        in_head = (lane < HEAD, lane >= HEAD)
        kb = k_ref[...]
        kh = [jnp.where(m, kb, 0.0).astype(BF16) for m in in_head]
        vb = v_ref[...].astype(BF16)
        c_k = c_ref[...]
        ck = [_lane_pick(c_k, 2 * hp + h) for h in (0, 1)]
        kidx = lax.broadcasted_iota(jnp.int32, (tq, tq), 0)
        qidx = lax.broadcasted_iota(jnp.int32, (tq, tq), 1)

        def q_step(i, carry, diagonal):
            dk, dv, dck = carry
            qs = pl.ds(pl.multiple_of(i * tq, tq), tq)
            qf = q_ref[qs, :] * scale
            dof = do_ref[qs, :]
            qh = [jnp.where(m, qf, 0.0).astype(BF16) for m in in_head]
            doh = [jnp.where(m, dof, 0.0).astype(BF16) for m in in_head]
            row = lambda ref, h: ref[0, i, pl.ds(h, 1), :]
            st = [_dot_nt(kh[h], qh[h]) + row(ct_ref, h) - ck[h] for h in (0, 1)]
            p = [jnp.exp(st[h] - row(lse_ref, h)) for h in (0, 1)]
            if diagonal:
                p = [jnp.where(kidx <= qidx, p[h], 0.0) for h in (0, 1)]
            dst = [p[h] * (_dot_nt(vb, doh[h]) - row(dd_ref, h)) for h in (0, 1)]
            p16 = [x.astype(BF16) for x in p]
            ds16 = [x.astype(BF16) for x in dst]
            dv = dv + _dot(p16[0], doh[0]) + _dot(p16[1], doh[1])
            dk = dk + _dot(ds16[0], qh[0]) + _dot(ds16[1], qh[1])
            dq_acc[qs, :] += _dot_tn(ds16[0], kh[0]) + _dot_tn(ds16[1], kh[1])
            for h in (0, 1):
                dcq_acc[i, pl.ds(h, 1), :] += jnp.sum(dst[h], axis=0, keepdims=True)
            dck = tuple(dck[h] - jnp.sum(dst[h], axis=1, keepdims=True) for h in (0, 1))
            return dk, dv, dck

        zero = jnp.zeros((tq, LANES), F32)
        carry = q_step(j, (zero, zero, (jnp.zeros((tq, 1), F32),) * 2), True)
        dk, dv, dck = lax.fori_loop(j + 1, n_q, functools.partial(q_step, diagonal=False), carry)
        dk_ref[...] = dk.astype(dk_ref.dtype)
        dv_ref[...] = dv.astype(dv_ref.dtype)
        dck_ref[...] = jnp.where(lane == 0, dck[0], jnp.where(lane == 1, dck[1], 0.0))

        @pl.when(j == n_q - 1)
        def _():
            dq_ref[...] = (dq_acc[...] * scale).astype(dq_ref.dtype)
            dcq_ref[0] = dcq_acc[...]

    npw = fw // LANES
    res_z = lambda col0: pl.BlockSpec((t, LANES), lambda hp, j: (0, col0 + hp))
    blk_z = lambda col0: pl.BlockSpec((tq, LANES), lambda hp, j: (j, col0 + hp))
    res = pl.BlockSpec((t, LANES), lambda hp, j: (0, hp))
    blk = pl.BlockSpec((tq, LANES), lambda hp, j: (j, hp))
    rows = pl.BlockSpec((1, n_q, 2, tq), lambda hp, j: (hp, 0, 0, 0))
    return pl.pallas_call(
        body, name="fox_attn_bwd", grid=(n_pair, n_q),
        in_specs=[res_z(0), blk_z(npw), blk_z(2 * npw), res, pl.BlockSpec((tq, LANES), lambda hp, j: (j, 0)),
                  rows, rows, rows],
        out_specs=[res, blk, blk, rows, blk],
        out_shape=[jax.ShapeDtypeStruct((t, fw), BF16), jax.ShapeDtypeStruct((t, fw), BF16),
                   jax.ShapeDtypeStruct((t, fw), BF16), jax.ShapeDtypeStruct((n_pair, n_q, 2, tq), F32),
                   jax.ShapeDtypeStruct((t, fw), F32)],
        scratch_shapes=[pltpu.VMEM((t, LANES), F32), pltpu.VMEM((n_q, 2, tq), F32)],
        compiler_params=_params("arbitrary", "arbitrary"),
    )(zf, zf, zf, do, c, ct, lse_r, dd_r)


def _adamw_math(w, g, m, v):
    m = ADAM_B1 * m + (1.0 - ADAM_B1) * g
    v = ADAM_B2 * v + (1.0 - ADAM_B2) * jnp.square(g)
    m_hat = m / (1.0 - ADAM_B1 ** ADAM_STEP)
    v_hat = v / (1.0 - ADAM_B2 ** ADAM_STEP)
    delta = -ADAM_LR * (m_hat / (jnp.sqrt(v_hat) + ADAM_EPS) + ADAM_WD * w)
    return delta, m, v


def _adamw(name, w, g, m, v):
    lead, rows, cols = w.shape
    if lead == 1 and cols % LANES:
        outs = _adamw(name, *[jnp.transpose(a, (2, 0, 1)) for a in (w, g, m, v)])
        return [jnp.transpose(o, (1, 2, 0)) for o in outs]
    if lead == 1:
        tile = _tile(rows, (128, 64, 32, 16, 8))
        spec, steps = pl.BlockSpec((1, tile, cols), lambda i: (0, i, 0)), rows // tile
    else:
        tile = _div_tile(lead, 256, 1)
        spec, steps = pl.BlockSpec((tile, rows, cols), lambda i: (i, 0, 0)), lead // tile

    def body(w_ref, g_ref, m_ref, v_ref, d_ref, mo_ref, vo_ref):
        d_ref[...], mo_ref[...], vo_ref[...] = _adamw_math(w_ref[...], g_ref[...], m_ref[...], v_ref[...])

    return pl.pallas_call(
        body, name=name, grid=(steps,), in_specs=[spec] * 4, out_specs=[spec] * 3,
        out_shape=[jax.ShapeDtypeStruct(w.shape, F32)] * 3, compiler_params=_params("parallel"),
    )(w, g, m, v)


def _place():
    return lax.axis_index("x"), lax.axis_index("y"), lax.axis_index("c")


def _other_chips(x, y):
    return [(1 - x, y), (x, 1 - y), (1 - x, 1 - y)]


HBM_SPEC = pl.BlockSpec(memory_space=pltpu.HBM)
SEM_SPEC = pl.BlockSpec(memory_space=pltpu.SEMAPHORE)


def _all_gather_shards(slabs):
    n = len(slabs)

    def body(*refs):
        src_refs, out_refs, send_sems, recv_sems = refs[:n], refs[n:2 * n], refs[2 * n], refs[2 * n + 1]
        x, y, c = _place()
        me = 2 * x + y
        sibling = (x, y, 1 - c)
        chips = _other_chips(x, y)
        first, passed, waits = [], [], []
        for g, (src_ref, out_ref) in enumerate(zip(src_refs, out_refs)):
            rh = src_ref.shape[0] // 2

            def part(chip, half, out_ref=out_ref, rh=rh):
                return out_ref.at[chip, pl.ds(half * rh, rh), :]

            def copy(k, src, dst, to, g=g):
                return pltpu.make_async_remote_copy(src_ref=src, dst_ref=dst, send_sem=send_sems.at[6 * g + k],
                                                    recv_sem=recv_sems.at[6 * g + k], device_id=to, device_id_type=MESH)

            first += [copy(j, src_ref.at[pl.ds(c * rh, rh), :], part(me, c), (px, py, c))
                      for j, (px, py) in enumerate(chips)]
            for j, (px, py) in enumerate(chips):
                theirs = part(2 * px + py, c)
                passed.append((copy(j, theirs, theirs, sibling), copy(3 + j, theirs, theirs, sibling)))
                other = part(2 * px + py, 1 - c)
                waits.append(copy(3 + j, other, other, sibling))
        for cp in first:
            cp.start()
        for landed, forward in passed:
            landed.wait_recv()
            forward.start()
        for cp in waits:
            cp.wait_recv()
        for cp in first + [fwd for _, fwd in passed]:
            cp.wait_send()

    return pl.pallas_call(
        body, name="weights_all_gather", in_specs=[HBM_SPEC] * n, out_specs=[HBM_SPEC] * n,
        out_shape=[jax.ShapeDtypeStruct((N_SHARD,) + a.shape, a.dtype) for a in slabs],
        scratch_shapes=[pltpu.SemaphoreType.DMA((6 * n,)), pltpu.SemaphoreType.DMA((6 * n,))],
    )(*slabs)


def _gather_ici_copies(src_refs, out_refs, send_sems, recv_sems):
    x, y, c = _place()
    me = 2 * x + y
    copies = []
    for g, (src_ref, out_ref) in enumerate(zip(src_refs, out_refs)):
        rh = src_ref.shape[0] // 2
        copies += [pltpu.make_async_remote_copy(
            src_ref=src_ref.at[pl.ds(c * rh, rh), :], dst_ref=out_ref.at[me, pl.ds(c * rh, rh), :],
            send_sem=send_sems.at[3 * g + j], recv_sem=recv_sems.at[3 * g + j], device_id=(px, py, c),
            device_id_type=MESH) for j, (px, py) in enumerate(_other_chips(x, y))]
    return copies


def _gather_start(slabs):
    n = len(slabs)

    def body(*refs):
        for cp in _gather_ici_copies(refs[:n], refs[n:2 * n], refs[2 * n], refs[2 * n + 1]):
            cp.start()
        refs[-1][...] = jnp.zeros_like(refs[-1])

    hbm = lambda a: pltpu.with_memory_space_constraint(a, pltpu.HBM)
    lands = [(N_SHARD,) + a.shape for a in slabs]
    out = pl.pallas_call(
        body, name="weights_gather_start",
        out_shape=(pltpu.SemaphoreType.DMA((3 * n,)), pltpu.SemaphoreType.DMA((3 * n,)),
                   *[pltpu.HBM(a.shape, a.dtype) for a in slabs],
                   *[pltpu.HBM(sh, a.dtype) for sh, a in zip(lands, slabs)], jax.ShapeDtypeStruct((8, LANES), F32)),
        in_specs=[HBM_SPEC] * (2 * n),
        out_specs=(SEM_SPEC, SEM_SPEC, *[HBM_SPEC] * (2 * n), pl.BlockSpec(memory_space=pltpu.VMEM)),
        input_output_aliases={i: 2 + i for i in range(2 * n)},
        compiler_params=pltpu.CompilerParams(has_side_effects=pltpu.SideEffectType.DATAFLOW_SIDE_EFFECTING),
    )(*[hbm(a) for a in slabs], *[hbm(lax.empty(sh, a.dtype)) for sh, a in zip(lands, slabs)])
    return out[0], out[1], list(out[2:2 + n]), list(out[2 + n:2 + 2 * n]), out[-1]


def _gather_wait(send_sems, recv_sems, slabs, landed, after):
    n = len(slabs)

    def body(*refs):
        for cp in _gather_ici_copies(refs[:n], refs[n:2 * n], refs[2 * n], refs[2 * n + 1]):
            cp.wait_send()
            cp.wait_recv()

    out = pl.pallas_call(
        body, name="weights_gather_wait",
        out_shape=[pltpu.HBM(a.shape, a.dtype) for a in slabs + landed],
        in_specs=[HBM_SPEC] * (2 * n) + [SEM_SPEC, SEM_SPEC, pl.BlockSpec(memory_space=pl.ANY)],
        out_specs=[HBM_SPEC] * (2 * n), input_output_aliases={i: i for i in range(2 * n)},
        compiler_params=pltpu.CompilerParams(has_side_effects=pltpu.SideEffectType.DATAFLOW_SIDE_EFFECTING),
    )(*slabs, *landed, send_sems, recv_sems, after)
    return list(out[n:])


def _gather_forward(gathered):
    n = len(gathered)

    def body(*refs):
        in_refs, out_refs, send_sems, recv_sems = refs[:n], refs[n:2 * n], refs[2 * n], refs[2 * n + 1]
        x, y, c = _place()

        def copy(g, j, chip, half):
            rh = in_refs[g].shape[1] // 2
            return pltpu.make_async_remote_copy(
                src_ref=in_refs[g].at[chip, pl.ds(half * rh, rh), :], dst_ref=out_refs[g].at[chip, pl.ds(half * rh, rh), :],
                send_sem=send_sems.at[3 * g + j], recv_sem=recv_sems.at[3 * g + j], device_id=(x, y, 1 - c),
                device_id_type=MESH)

        chips = [2 * px + py for px, py in _other_chips(x, y)]
        for g in range(n):
            for j, chip in enumerate(chips):
                copy(g, j, chip, c).start()
        for g in range(n):
            for j, chip in enumerate(chips):
                copy(g, j, chip, c).wait_send()
                copy(g, j, chip, 1 - c).wait_recv()

    return pl.pallas_call(
        body, name="weights_gather_forward", in_specs=[HBM_SPEC] * n, out_specs=[HBM_SPEC] * n,
        out_shape=[jax.ShapeDtypeStruct(a.shape, a.dtype) for a in gathered],
        input_output_aliases={g: g for g in range(n)},
        scratch_shapes=[pltpu.SemaphoreType.DMA((3 * n,)), pltpu.SemaphoreType.DMA((3 * n,))],
    )(*gathered)


def _chip_index():
    return jnp.reshape(2 * lax.axis_index("x") + lax.axis_index("y"), (1,)).astype(jnp.int32)


def _place_own_shard(name, gathered, slab):
    rows, width = slab.shape
    tile = _div_tile(rows, 256, 16)

    def body(me_ref, s_ref, g_ref, o_ref):
        o_ref[0] = s_ref[...]

    return pl.pallas_call(
        body, name=name,
        grid_spec=pltpu.PrefetchScalarGridSpec(
            num_scalar_prefetch=1, grid=(rows // tile,),
            in_specs=[pl.BlockSpec((tile, width), lambda i, me: (i, 0)), pl.BlockSpec(memory_space=pl.ANY)],
            out_specs=pl.BlockSpec((1, tile, width), lambda i, me: (me[0], i, 0))),
        out_shape=jax.ShapeDtypeStruct(gathered.shape, gathered.dtype), input_output_aliases={2: 0},
        compiler_params=_params("parallel"),
    )(_chip_index(), slab, gathered)


def _sibling_exchange(sent, tag):
    n = len(sent)

    def body(*refs):
        g_refs, out_refs, send_sems, recv_sems = refs[:n], refs[n:2 * n], refs[2 * n], refs[2 * n + 1]
        x, y, c = _place()
        copies = [pltpu.make_async_remote_copy(
            src_ref=g_ref.at[s], dst_ref=out_ref.at[s], send_sem=send_sems.at[N_SHARD * g + s],
            recv_sem=recv_sems.at[N_SHARD * g + s], device_id=(x, y, 1 - c), device_id_type=MESH)
            for g, (g_ref, out_ref) in enumerate(zip(g_refs, out_refs)) for s in range(N_SHARD)]
        for cp in copies:
            cp.start()
        for cp in copies:
            cp.wait()

    return pl.pallas_call(
        body, name="grad_sibling_exchange_" + tag, in_specs=[HBM_SPEC] * n, out_specs=[HBM_SPEC] * n,
        out_shape=[jax.ShapeDtypeStruct(g.shape, g.dtype) for g in sent],
        scratch_shapes=[pltpu.SemaphoreType.DMA((N_SHARD * n,)), pltpu.SemaphoreType.DMA((N_SHARD * n,))],
    )(*sent)


def _add_sibling(name, kept, got):
    _, rh, width = kept.shape
    tile = _div_tile(rh, 256, 16)

    def body(a_ref, b_ref, o_ref):
        o_ref[...] = (a_ref[...] + b_ref[...].astype(F32)).astype(o_ref.dtype)

    spec = pl.BlockSpec((1, tile, width), lambda s, i: (s, i, 0))
    return pl.pallas_call(
        body, name=name, grid=(N_SHARD, rh // tile), in_specs=[spec, spec], out_specs=spec,
        out_shape=jax.ShapeDtypeStruct(kept.shape, BF16), compiler_params=_params("parallel", "parallel"),
    )(kept, got)


def _exchange_copies(p_refs, land_refs, send_sems, recv_sems):
    x, y, c = _place()
    me = 2 * x + y
    return [pltpu.make_async_remote_copy(
        src_ref=p_ref.at[2 * px + py], dst_ref=land_ref.at[me], send_sem=send_sems.at[3 * g + j],
        recv_sem=recv_sems.at[3 * g + j], device_id=(px, py, c), device_id_type=MESH)
        for g, (p_ref, land_ref) in enumerate(zip(p_refs, land_refs)) for j, (px, py) in enumerate(_other_chips(x, y))]


def _chip_exchange_start(ps, tag):
    n = len(ps)

    def body(*refs):
        for cp in _exchange_copies(refs[:n], refs[n:2 * n], refs[2 * n], refs[2 * n + 1]):
            cp.start()
        refs[-1][...] = jnp.zeros_like(refs[-1])

    hbm = lambda a: pltpu.with_memory_space_constraint(a, pltpu.HBM)
    out = pl.pallas_call(
        body, name="grad_chip_exchange_start_" + tag,
        out_shape=(pltpu.SemaphoreType.DMA((3 * n,)), pltpu.SemaphoreType.DMA((3 * n,)),
                   *[pltpu.HBM(a.shape, a.dtype) for a in ps], *[pltpu.HBM(a.shape, a.dtype) for a in ps],
                   jax.ShapeDtypeStruct((8, LANES), F32)),
        in_specs=[HBM_SPEC] * (2 * n),
        out_specs=(SEM_SPEC, SEM_SPEC, *[HBM_SPEC] * (2 * n), pl.BlockSpec(memory_space=pltpu.VMEM)),
        input_output_aliases={i: 2 + i for i in range(2 * n)},
        compiler_params=pltpu.CompilerParams(has_side_effects=pltpu.SideEffectType.DATAFLOW_SIDE_EFFECTING),
    )(*[hbm(a) for a in ps], *[hbm(lax.empty(a.shape, a.dtype)) for a in ps])
    return out[0], out[1], list(out[2:2 + n]), list(out[2 + n:2 + 2 * n]), out[-1]


def _chip_exchange_wait(send_sems, recv_sems, ps, landed, after, tag):
    n = len(ps)

    def body(*refs):
        for cp in _exchange_copies(refs[:n], refs[n:2 * n], refs[2 * n], refs[2 * n + 1]):
            cp.wait_send()
            cp.wait_recv()

    out = pl.pallas_call(
        body, name="grad_chip_exchange_wait_" + tag,
        out_shape=[pltpu.HBM(a.shape, a.dtype) for a in ps + landed],
        in_specs=[HBM_SPEC] * (2 * n) + [SEM_SPEC, SEM_SPEC, pl.BlockSpec(memory_space=pl.ANY)],
        out_specs=[HBM_SPEC] * (2 * n), input_output_aliases={i: i for i in range(2 * n)},
        compiler_params=pltpu.CompilerParams(has_side_effects=pltpu.SideEffectType.DATAFLOW_SIDE_EFFECTING),
    )(*ps, *landed, send_sems, recv_sems, after)
    return list(out[:n]), list(out[n:])


def _sum_chips(name, p, got):
    _, rh, width = p.shape
    tile = _div_tile(rh, 256, 16)
    n_t = rh // tile
    place = jnp.stack([2 * lax.axis_index("x") + lax.axis_index("y"), lax.axis_index("c")]).astype(jnp.int32)

    def body(pl_ref, own_ref, r0, r1, r2, r3, o_ref):
        me = pl_ref[0]
        own = own_ref[0].astype(F32)
        t = [jnp.where(me == s, own, r[0].astype(F32)) for s, r in enumerate((r0, r1, r2, r3))]
        o_ref[...] = ((t[0] + t[1]) + t[2]) + t[3]

    def slot(s):
        return pl.BlockSpec((1, tile, width), lambda i, pc: (jnp.where(pc[0] == s, (s + 1) % N_SHARD, s), i, 0))

    return pl.pallas_call(
        body, name=name,
        grid_spec=pltpu.PrefetchScalarGridSpec(
            num_scalar_prefetch=1, grid=(n_t,),
            in_specs=[pl.BlockSpec((1, tile, width), lambda i, pc: (pc[0], i, 0))] + [slot(s) for s in range(N_SHARD)],
            out_specs=pl.BlockSpec((tile, width), lambda i, pc: (pc[1] * n_t + i, 0))),
        out_shape=jax.ShapeDtypeStruct((2 * rh, width), F32), compiler_params=_params("parallel"),
    )(place, p, got, got, got, got)


def _join_halves(fulls, tag):
    n = len(fulls)

    def body(*refs):
        f_refs, out_refs, send_sems, recv_sems = refs[:n], refs[n:2 * n], refs[2 * n], refs[2 * n + 1]
        x, y, c = _place()

        def copy(g, half):
            rh = f_refs[g].shape[0] // 2
            return pltpu.make_async_remote_copy(
                src_ref=f_refs[g].at[pl.ds(half * rh, rh), :], dst_ref=out_refs[g].at[pl.ds(half * rh, rh), :],
                send_sem=send_sems.at[g], recv_sem=recv_sems.at[g], device_id=(x, y, 1 - c), device_id_type=MESH)

        for g in range(n):
            copy(g, c).start()
        for g in range(n):
            copy(g, c).wait_send()
            copy(g, 1 - c).wait_recv()

    return pl.pallas_call(
        body, name="grad_join_halves_" + tag, in_specs=[HBM_SPEC] * n, out_specs=[HBM_SPEC] * n,
        out_shape=[jax.ShapeDtypeStruct(f.shape, f.dtype) for f in fulls],
        input_output_aliases={g: g for g in range(n)},
        scratch_shapes=[pltpu.SemaphoreType.DMA((n,)), pltpu.SemaphoreType.DMA((n,))],
    )(*fulls)


def _reduce_scatter_start(gs, tag):
    c = lax.axis_index("c")
    gs = [g.reshape(N_SHARD, 2, g.shape[1] // 2, g.shape[2]) for g in gs]
    kept = [lax.dynamic_index_in_dim(g, c, axis=1, keepdims=False) for g in gs]
    got = _sibling_exchange([lax.dynamic_index_in_dim(g, 1 - c, axis=1, keepdims=False).astype(BF16) for g in gs], tag)
    chip_sums = [_add_sibling(f"grad_add_sibling_{tag}{i}", k, r) for i, (k, r) in enumerate(zip(kept, got))]
    *state, token = _chip_exchange_start(chip_sums, tag)
    return state, token


def _reduce_scatter_end(state, after, tag):
    chip_sums, landed = _chip_exchange_wait(*state, after, tag)
    return _join_halves([_sum_chips(f"grad_sum_chips_{tag}{i}", p, r)
                         for i, (p, r) in enumerate(zip(chip_sums, landed))], tag)


def _all_reduce_small(v):
    rows = v.shape[0]

    def body(v_ref, out_ref, gather, send_sems, recv_sems):
        x, y, c = _place()
        gather[4 * x + 2 * y + c] = v_ref[...]
        flips = [(dx, dy, dc) for dx in (0, 1) for dy in (0, 1) for dc in (0, 1)][1:]
        peers = [((x + dx) % 2, (y + dy) % 2, (c + dc) % 2) for dx, dy, dc in flips]
        copies = [pltpu.make_async_remote_copy(
            src_ref=v_ref, dst_ref=gather.at[4 * x + 2 * y + c], send_sem=send_sems.at[j], recv_sem=recv_sems.at[j],
            device_id=peer, device_id_type=MESH) for j, peer in enumerate(peers)]
        for cp in copies:
            cp.start()
        for j, (px, py, pc) in enumerate(peers):
            pltpu.make_async_remote_copy(
                src_ref=v_ref, dst_ref=gather.at[4 * px + 2 * py + pc], send_sem=send_sems.at[j],
                recv_sem=recv_sems.at[j], device_id=(px, py, pc), device_id_type=MESH).wait_recv()
        for cp in copies:
            cp.wait_send()
        acc = gather[0]
        for d in range(1, 8):
            acc = acc + gather[d]
        out_ref[...] = acc

    vm = pl.BlockSpec(memory_space=pltpu.VMEM)
    return pl.pallas_call(
        body, name="small_grads_all_reduce", in_specs=[vm], out_specs=vm,
        out_shape=jax.ShapeDtypeStruct(v.shape, F32),
        scratch_shapes=[pltpu.VMEM((8, rows, LANES), F32), pltpu.SemaphoreType.DMA((7,)), pltpu.SemaphoreType.DMA((7,))],
    )(v)


def _pad_lanes(v):
    v = v.reshape(1, -1)
    return jnp.pad(v, ((0, 0), (0, -v.shape[1] % LANES)))


def _pack_small(vs, rows):
    flat = jnp.concatenate([_pad_lanes(v) for v in vs], axis=1)
    return jnp.pad(flat, ((0, 0), (0, rows * LANES - flat.shape[1]))).reshape(rows, LANES)


def _unpack_small(packed, shapes):
    flat = packed.reshape(-1)
    out, off = [], 0
    for s in shapes:
        n = 1
        for d in s:
            n *= d
        out.append(flat[off:off + n].reshape(s))
        off += n + (-n % LANES)
    return out


BIG = ("w_in", "rw_w_lora_up", "rw_a_lora_up", "w_up_rwkv", "w_up_fox", "w_out", "ple_proj", "ple_gate_w")
ROW_SHARDED = ("w_out", "ple_gate_w")
FIRST_NEEDED = ("w_in", "rw_w_lora_up", "rw_a_lora_up")
SMALL = ("norm_g", "rw_shift_mu", "rw_w0", "rw_a0", "rw_k_k", "rw_k_a", "rw_r_k", "rw_ln_g", "rw_ln_b", "fox_b_f",
         "ple_norm_g", "final_norm_g")
WEIGHTS = ("norm_g", "w_in", "rw_shift_mu", "rw_w0", "rw_w_lora_up", "rw_a0", "rw_a_lora_up", "rw_k_k", "rw_k_a",
           "rw_r_k", "rw_ln_g", "rw_ln_b", "fox_b_f", "w_up_rwkv", "w_up_fox", "w_out", "ple_proj", "ple_gate_w",
           "ple_norm_g", "final_norm_g")


def kernel(x, p, norm_g, w_in, rw_shift_mu, rw_w0, rw_w_lora_up, rw_a0, rw_a_lora_up, rw_k_k, rw_k_a, rw_r_k, rw_ln_g, rw_ln_b, fox_b_f, w_up_rwkv, w_up_fox, w_out, ple_proj, ple_gate_w, ple_norm_g, final_norm_g, loss_target, m_norm_g, m_w_in, m_rw_shift_mu, m_rw_w0, m_rw_w_lora_up, m_rw_a0, m_rw_a_lora_up, m_rw_k_k, m_rw_k_a, m_rw_r_k, m_rw_ln_g, m_rw_ln_b, m_fox_b_f, m_w_up_rwkv, m_w_up_fox, m_w_out, m_ple_proj, m_ple_gate_w, m_ple_norm_g, m_final_norm_g, v_norm_g, v_w_in, v_rw_shift_mu, v_rw_w0, v_rw_w_lora_up, v_rw_a0, v_rw_a_lora_up, v_rw_k_k, v_rw_k_a, v_rw_r_k, v_rw_ln_g, v_rw_ln_b, v_fox_b_f, v_w_up_rwkv, v_w_up_fox, v_w_out, v_ple_proj, v_ple_gate_w, v_ple_norm_g, v_final_norm_g):
    wts = dict(norm_g=norm_g, w_in=w_in, rw_shift_mu=rw_shift_mu, rw_w0=rw_w0, rw_w_lora_up=rw_w_lora_up, rw_a0=rw_a0,
               rw_a_lora_up=rw_a_lora_up, rw_k_k=rw_k_k, rw_k_a=rw_k_a, rw_r_k=rw_r_k, rw_ln_g=rw_ln_g, rw_ln_b=rw_ln_b,
               fox_b_f=fox_b_f, w_up_rwkv=w_up_rwkv, w_up_fox=w_up_fox, w_out=w_out, ple_proj=ple_proj,
               ple_gate_w=ple_gate_w, ple_norm_g=ple_norm_g, final_norm_g=final_norm_g)
    mom = dict(norm_g=m_norm_g, w_in=m_w_in, rw_shift_mu=m_rw_shift_mu, rw_w0=m_rw_w0, rw_w_lora_up=m_rw_w_lora_up,
               rw_a0=m_rw_a0, rw_a_lora_up=m_rw_a_lora_up, rw_k_k=m_rw_k_k, rw_k_a=m_rw_k_a, rw_r_k=m_rw_r_k,
               rw_ln_g=m_rw_ln_g, rw_ln_b=m_rw_ln_b, fox_b_f=m_fox_b_f, w_up_rwkv=m_w_up_rwkv, w_up_fox=m_w_up_fox,
               w_out=m_w_out, ple_proj=m_ple_proj, ple_gate_w=m_ple_gate_w, ple_norm_g=m_ple_norm_g,
               final_norm_g=m_final_norm_g)
    vel = dict(norm_g=v_norm_g, w_in=v_w_in, rw_shift_mu=v_rw_shift_mu, rw_w0=v_rw_w0, rw_w_lora_up=v_rw_w_lora_up,
               rw_a0=v_rw_a0, rw_a_lora_up=v_rw_a_lora_up, rw_k_k=v_rw_k_k, rw_k_a=v_rw_k_a, rw_r_k=v_rw_r_k,
               rw_ln_g=v_rw_ln_g, rw_ln_b=v_rw_ln_b, fox_b_f=v_fox_b_f, w_up_rwkv=v_w_up_rwkv, w_up_fox=v_w_up_fox,
               w_out=v_w_out, ple_proj=v_ple_proj, ple_gate_w=v_ple_gate_w, ple_norm_g=v_ple_norm_g,
               final_norm_g=v_final_norm_g)

    t, d = x.shape[1], x.shape[2]
    cw = rw_w0.shape[1]
    lr = rw_w_lora_up.shape[1]
    fh = fox_b_f.shape[1]
    fw = fh * HEAD
    rw_cols = 4 * cw + 2 * lr
    fox_cols = 4 * fw + fh
    assert 2 * lr == LANES and cw % LANES == 0 and fw % LANES == 0 and fh <= LANES
    xs = x[0]
    ps = p[0, 0]
    tgt = loss_target[0]

    groups = {}
    for n in BIG:
        groups.setdefault(wts[n].shape[2], []).append(n)
    groups = list(groups.values())
    slabs16 = []
    for gi, names in enumerate(groups):
        slab = jnp.concatenate([wts[n][0] for n in names], axis=0)
        rows, width = slab.shape
        tile_c = _div_tile(rows, 256, 32)
        slabs16 += _rows(f"weights_to_bf16_{gi}", lambda i, a: (a,), rows, tile_c, [(slab, _row_spec(tile_c, width))],
                         [(width, BF16)])
    first = [gi for gi, names in enumerate(groups) if not set(names).isdisjoint(FIRST_NEEDED)]
    later = [gi for gi in range(len(groups)) if gi not in first]
    gathered = dict(zip(first, _all_gather_shards([slabs16[gi] for gi in first])))
    *gather_state, gather_token = _gather_start([slabs16[gi] for gi in later])
    full = {}

    def assemble(gis):
        for gi in gis:
            g = _place_own_shard(f"weights_place_own_{gi}", gathered[gi], slabs16[gi])
            off = 0
            for n in groups[gi]:
                r = wts[n].shape[1]
                part = g[:, off:off + r, :]
                full[n] = (part.reshape(N_SHARD * r, -1) if n in ROW_SHARDED
                           else jnp.concatenate([part[s] for s in range(N_SHARD)], axis=1))
                off += r

    assemble(first)
    w_rw = full["w_in"][:, :rw_cols]
    w_fox = jnp.pad(full["w_in"][:, rw_cols:rw_cols + fox_cols], ((0, 0), (0, LANES - fh)))
    w_gate = full["w_in"][:, rw_cols + fox_cols:]
    wup_pad = jnp.pad(full["rw_w_lora_up"], ((0, lr), (0, 0)))
    aup_pad = jnp.pad(full["rw_a_lora_up"], ((lr, 0), (0, 0)))
    b_pad = _pad_lanes(fox_b_f)
    r_k_row = rw_r_k.reshape(1, cw)
    gf_row = final_norm_g.reshape(1, d)

    tile = _tile(t, (256, 128, 64, 32, 16, 8))
    tile_s = _tile(t, (128, 64, 32, 16, 8))
    n_s = t // tile_s
    full2 = lambda a: (a, _full_spec(a.shape))

    (h,) = _rows("norm1", lambda i, a, g: (_rms(a, g),), t, tile, [(xs, _row_spec(tile, d)), full2(norm_g)], [(d, BF16)])
    z_rw = _matmul("proj_rw", h, w_rw, after=gather_token)
    z_fox = _matmul("proj_fox", h, w_fox, after=z_rw[:8, :LANES])
    z_gate = _matmul("proj_gate", h, w_gate, after=z_fox[:8, :LANES])
    gathered.update(zip(later, _gather_forward(_gather_wait(*gather_state, z_gate))))
    assemble(later)

    pre_consts = [full2(rw_shift_mu), full2(rw_w0), full2(rw_a0), full2(wup_pad), full2(aup_pad), full2(rw_k_k),
                  full2(rw_k_a)]

    def pre_fwd(i, z, prev8, *consts):
        return _rw_pre(z, _shifted(i, z, prev8), *consts, cw=cw)

    r_, w_, k_, v_, kk_, a_, g_ = _rows(
        "rwkv_pre", pre_fwd, t, tile_s,
        [(z_rw, _row_spec(tile_s, rw_cols)), (z_rw, _prev_rows_spec(tile_s, rw_cols))] + pre_consts, [(cw, F32)] * 7)
    y_scan, states, tinvs = _scan_fwd(r_, w_, k_, v_, kk_, a_)
    post_consts = [full2(rw_ln_g), full2(rw_ln_b), full2(r_k_row)]
    post_rows = lambda *arrs: [(a, _row_spec(tile_s, cw)) for a in arrs]
    (y_rw,) = _rows("rwkv_post", lambda i, *a: (_rw_post(*a, cw=cw),), t, tile_s,
                    post_rows(y_scan, r_, k_, v_, g_) + post_consts, [(cw, BF16)])

    c_fox = _fox_cumsum(z_fox, b_pad, fw=fw, fh=fh)
    tq = _fox_tile(t)
    n_pair_f = fw // LANES
    head_rows = lambda a: a.T.reshape(n_pair_f, 2, t // tq, tq).transpose(0, 2, 1, 3)
    head_cols = lambda a: a.transpose(0, 2, 1, 3).reshape(fh, t).T
    ct_fox = head_rows(c_fox[:, :fh])
    o_fox, lse_fox, y_fox = _fox_fwd(z_fox, c_fox, ct_fox, fw=fw)

    u_rw = _matmul("up_rwkv", y_rw, full["w_up_rwkv"])
    u_fox = _matmul("up_fox", y_fox, full["w_up_fox"])
    (merged,) = _rows("merge", lambda i, zg, a, b: (_merge(zg, a, b, d=d),), t, tile,
                      [(z_gate, _row_spec(tile, 2 * d)), (u_rw, _row_spec(tile, d)), (u_fox, _row_spec(tile, d))],
                      [(d, BF16)])
    x1 = _matmul("out_proj", merged, full["w_out"], add=xs)
    (n2,) = _rows("norm2", lambda i, a, g: (_rms(a, g),), t, tile, [(x1, _row_spec(tile, d)), full2(ple_norm_g)],
                  [(d, BF16)])
    gl = _matmul("ple_gate", n2, full["ple_gate_w"])
    ple = _matmul("ple_proj", ps, full["ple_proj"])

    def head_bwd(i, x1_t, ple_t, gl_t, gf, tg):
        loss, vjp = jax.vjp(lambda a, b, cc, g: _head_loss(a, b, cc, g, tg), x1_t, ple_t, gl_t, gf)
        dx1, dple, dgl, dgf = vjp(jnp.ones((1, 1), F32))
        return dx1, dple, dgl, jnp.broadcast_to(loss, (1, LANES)), dgf

    dx2, dple, dgl, loss_row, d_gf = _rows(
        "loss_head", head_bwd, t, tile_s,
        [(x1, _row_spec(tile_s, d)), (ple, _row_spec(tile_s, d)), (gl, _row_spec(tile_s, d)), full2(gf_row),
         (tgt, _row_spec(tile_s, d))],
        [(d, F32), (d, BF16), (d, BF16)], [(1, LANES), (1, d)])

    g_ple_proj = _matmul("d_ple_proj", ps, dple, ta=True)
    g_ple_gate = _matmul("d_ple_gate_w", n2, dgl, ta=True)
    dn2 = _matmul("d_n2", dgl, full["ple_gate_w"], tb=True)

    def norm_bwd(i, a, g, dh, res):
        _, vjp = jax.vjp(_rms, a, g)
        da, dg = vjp(dh)
        return res + da, dg

    dx1, d_g2 = _rows("norm2_bwd", norm_bwd, t, tile_s,
                      [(x1, _row_spec(tile_s, d)), full2(ple_norm_g), (dn2, _row_spec(tile_s, d)),
                       (dx2, _row_spec(tile_s, d))], [(d, F32)], [(1, d)])
    g_w_out = _matmul("d_w_out", merged, dx1, ta=True)
    dmerged = _matmul("d_merged", dx1, full["w_out"], tb=True)

    def merge_bwd(i, zg, a, b, dm):
        _, vjp = jax.vjp(functools.partial(_merge, d=d), zg, a, b)
        return vjp(dm)

    dz_gate, du_rw, du_fox = _rows(
        "merge_bwd", merge_bwd, t, tile_s,
        [(z_gate, _row_spec(tile_s, 2 * d)), (u_rw, _row_spec(tile_s, d)), (u_fox, _row_spec(tile_s, d)),
         (dmerged, _row_spec(tile_s, d))], [(2 * d, BF16), (d, BF16), (d, BF16)])
    g_up_rw = _matmul("d_w_up_rwkv", y_rw, du_rw, ta=True)
    g_up_fox = _matmul("d_w_up_fox", y_fox, du_fox, ta=True)

    def slab_grads(gis, g_full):
        def by_shard(g, row_sharded):
            if row_sharded:
                return g.reshape(N_SHARD, g.shape[0] // N_SHARD, g.shape[1])
            return jnp.stack(jnp.split(g, N_SHARD, axis=1))

        return [jnp.concatenate([by_shard(g_full[n], n in ROW_SHARDED) for n in groups[gi]], axis=1) for gi in gis]

    g_full = {"w_up_rwkv": g_up_rw, "w_up_fox": g_up_fox, "w_out": g_w_out, "ple_proj": g_ple_proj,
              "ple_gate_w": g_ple_gate}
    rs_late, token_late = _reduce_scatter_start(slab_grads(later, g_full), "late")
    dy_rw = _matmul("d_y_rwkv", du_rw, full["w_up_rwkv"], tb=True, after=token_late)
    dy_fox = _matmul("d_y_fox", du_fox, full["w_up_fox"], tb=True)

    def post_bwd(i, y, r, k, v, g, ln_g, ln_b, r_k, dy):
        _, vjp = jax.vjp(functools.partial(_rw_post, cw=cw), y, r, k, v, g, ln_g, ln_b, r_k)
        return vjp(dy)

    dys, dr1, dk1, dv1, dg1, d_ln_g, d_ln_b, d_r_k = _rows(
        "rwkv_post_bwd", post_bwd, t, tile_s,
        post_rows(y_scan, r_, k_, v_, g_) + post_consts + post_rows(dy_rw), [(cw, F32)] * 5, [(1, cw)] * 3)
    dr2, dw2, dk2, dv2, dkk2, da2 = _scan_bwd(r_, w_, k_, v_, kk_, a_, states, tinvs, dys)

    def pre_bwd(i, z, prev8, mu, w0, a0, wup, aup, k_k, k_a, dr_a, dr_b, dk_a, dk_b, dv_a, dv_b, dw, dkk, da, dg):
        zp = _shifted(i, z, prev8)
        _, vjp = jax.vjp(functools.partial(_rw_pre, cw=cw), z, zp, mu, w0, a0, wup, aup, k_k, k_a)
        dz, dzp, dmu, dw0, da0, dwup, daup, dk_k, dk_a = vjp((dr_a + dr_b, dw, dk_a + dk_b, dv_a + dv_b, dkk, da, dg))
        row = lax.broadcasted_iota(jnp.int32, dz.shape, 0)
        dz = dz + jnp.where(row < tile_s - 1, pltpu.roll(dzp, tile_s - 1, 0), 0.0)
        return dz, _row_of(dzp, 0), dmu, dw0, da0, dwup, daup, dk_k, dk_a

    def pre_bwd_call():
        n_in = 2 + len(pre_consts) + 10
        rev = lambda w: pl.BlockSpec((tile_s, w), lambda g: (n_s - 1 - g, 0))
        prev = pl.BlockSpec((8, rw_cols), lambda g: (jnp.maximum((n_s - 1 - g) * (tile_s // 8) - 1, 0), 0))
        ins = ([(z_rw, rev(rw_cols)), (z_rw, prev)] + pre_consts
               + [(a, rev(cw)) for a in (dr1, dr2, dk1, dk2, dv1, dv2, dw2, dkk2, da2, dg1)])

        def body(*refs):
            g = pl.program_id(0)
            carry = refs[-1]

            @pl.when(g == 0)
            def _():
                carry[...] = jnp.zeros_like(carry)

            vals = pre_bwd(n_s - 1 - g, *[r[...] for r in refs[:n_in]])
            row = lax.broadcasted_iota(jnp.int32, vals[0].shape, 0)
            refs[n_in][...] = (vals[0] + jnp.where(row == tile_s - 1, carry[...], 0.0)).astype(BF16)
            carry[...] = vals[1]
            for r, v in zip(refs[n_in + 1:-1], vals[2:]):
                @pl.when(g == 0)
                def _(r=r, v=v):
                    r[...] = v

                @pl.when(g > 0)
                def _(r=r, v=v):
                    r[...] += v

        acc_shapes = [(1, rw_cols), (1, cw), (1, cw), (LANES, cw), (LANES, cw), (1, cw), (1, cw)]
        return pl.pallas_call(
            body, name="rwkv_pre_bwd", grid=(n_s,), in_specs=[s for _, s in ins],
            out_specs=[rev(rw_cols)] + [_full_spec(s) for s in acc_shapes],
            out_shape=[jax.ShapeDtypeStruct((t, rw_cols), BF16)] + [jax.ShapeDtypeStruct(s, F32) for s in acc_shapes],
            scratch_shapes=[pltpu.VMEM((1, rw_cols), F32)],
            compiler_params=_params("arbitrary"),
        )(*[a for a, _ in ins])

    dz_rw, d_mu, d_w0, d_a0, d_wup, d_aup, d_k_k, d_k_a = pre_bwd_call()

    def fox_post_bwd(i, o, g, dy):
        _, vjp = jax.vjp(lambda oo, gg: oo * (gg * jax.nn.sigmoid(gg)), o, g)
        do, dg = vjp(dy)
        return do, _head_sum(do * o, _head_matrix(fw)), dg

    do_fox, dd_fox, dg_fox = _rows(
        "fox_post_bwd", fox_post_bwd, t, tile_s,
        [(o_fox, _row_spec(tile_s, fw)), (z_fox, _row_spec(tile_s, fw, 3)), (dy_fox, _row_spec(tile_s, fw))],
        [(fw, F32), (fw, F32), (fw, BF16)])
    dq_f, dk_f, dv_f, dcq, dck = _fox_bwd(z_fox, do_fox, c_fox, ct_fox, head_rows(lse_fox[:, ::HEAD]),
                                          head_rows(dd_fox[:, ::HEAD]), fw=fw)
    dc = head_cols(dcq) + dck.reshape(t, n_pair_f, LANES)[:, :, :2].reshape(t, fh)
    dfl, d_bf = _fox_cumsum_bwd(z_fox, b_pad, jnp.pad(dc, ((0, 0), (0, LANES - fh))), fw=fw, fh=fh)
    dz_fox = jnp.concatenate([dq_f, dk_f, dv_f, dg_fox, dfl], axis=1)

    g_w_rw = _matmul("d_w_in_rw", h, dz_rw, ta=True)
    g_w_fox = _matmul("d_w_in_fox", h, dz_fox, ta=True)
    g_w_gate = _matmul("d_w_in_gate", h, dz_gate, ta=True)

    reduced = dict(zip(later, _reduce_scatter_end(rs_late, dz_rw, "late")))
    g_full.update({"w_in": jnp.concatenate([g_w_rw, g_w_fox[:, :fox_cols], g_w_gate], axis=1),
                   "rw_w_lora_up": d_wup[:lr], "rw_a_lora_up": d_aup[lr:]})
    rs_state, token = _reduce_scatter_start(slab_grads(first, g_full), "first")
    dh = _matmul("d_h_rw", dz_rw, w_rw, tb=True, after=token)
    dh = _matmul("d_h_fox", dz_fox, w_fox, tb=True, add=dh)
    dh = _matmul("d_h_gate", dz_gate, w_gate, tb=True, add=dh)
    grad_x, d_g1 = _rows("norm1_bwd", norm_bwd, t, tile_s,
                         [(xs, _row_spec(tile_s, d)), full2(norm_g), (dh, _row_spec(tile_s, d)),
                          (dx1, _row_spec(tile_s, d))], [(d, F32)], [(1, d)])
    small_parts = dict(norm_g=d_g1, rw_shift_mu=d_mu, rw_w0=d_w0, rw_a0=d_a0, rw_k_k=d_k_k, rw_k_a=d_k_a, rw_r_k=d_r_k,
                       rw_ln_g=d_ln_g, rw_ln_b=d_ln_b, fox_b_f=d_bf[:, :fh], ple_norm_g=d_g2, final_norm_g=d_gf)
    n_small = sum(-(-wts[n].size // LANES) for n in SMALL)
    small_rows = -(-n_small // 8) * 8
    small_shapes = [wts[n].shape for n in SMALL]
    g_small = _all_reduce_small(_pack_small([small_parts[n] for n in SMALL], small_rows))
    reduced.update(zip(first, _reduce_scatter_end(rs_state, g_small, "first")))
    grads = {}
    for gi, names in enumerate(groups):
        g, off = reduced[gi], 0
        for n in names:
            r = wts[n].shape[1]
            grads[n] = g[off:off + r][None]
            off += r

    for n, g in zip(SMALL, _unpack_small(g_small, small_shapes)):
        grads[n] = g

    delta, new_m, new_v = {}, {}, {}
    for n in BIG:
        delta[n], new_m[n], new_v[n] = _adamw("adamw_" + n, wts[n], grads[n], mom[n], vel[n])
    packed = lambda src: _pack_small([src[n] for n in SMALL], small_rows)[None]
    for store, out in zip((delta, new_m, new_v), _adamw("adamw_small", packed(wts), g_small[None], packed(mom), packed(vel))):
        for n, a in zip(SMALL, _unpack_small(out[0], small_shapes)):
            store[n] = a

    loss = lax.psum(loss_row[0, 0], ("x", "y", "c"))
    return (loss, grad_x[None], *[grads[n] for n in WEIGHTS], *[delta[n] for n in WEIGHTS],
            *[new_m[n] for n in WEIGHTS], *[new_v[n] for n in WEIGHTS])
```
